```python
import jax, jax.numpy as jnp
from jax import lax
import numpy as np

D_MODEL = 2048
BATCH = 8
SEQ = 8192
DEPTH = 2

MEM_LEN = 256
N_MIXERS = 2
D_FF = 5632
CHUNK = 128
GMLP_WIDTH = 2048
GMLP_GROUPS = 8
GMLP_GROUP_DIM = GMLP_WIDTH // GMLP_GROUPS
CONV_WIDTH = 3
XATTN_HEADS = 4
XATTN_HEAD_DIM = D_MODEL // XATTN_HEADS
RMS_EPS = 1e-6
LN_EPS = 1e-5

kernel_name = "hybrid_gmlp_shortconv_macaron_memxattn"


def rmsnorm(x, g):
    xf = x.astype(jnp.float32)
    y = xf * lax.rsqrt(jnp.mean(xf * xf, axis=-1, keepdims=True) + RMS_EPS)
    return (y * g.astype(jnp.float32)).astype(x.dtype)


def layernorm(x, g, b):
    xf = x.astype(jnp.float32)
    mu = jnp.mean(xf, axis=-1, keepdims=True)
    xc = xf - mu
    var = jnp.mean(xc * xc, axis=-1, keepdims=True)
    y = xc * lax.rsqrt(var + LN_EPS) * g.astype(jnp.float32) + b.astype(jnp.float32)
    return y.astype(x.dtype)


def swiglu(h, w13, w2):
    gate, up = jnp.split(h @ w13, 2, axis=-1)
    return (jax.nn.silu(gate) * up) @ w2


def gmlp_mixer(h, w_in, ln_g, ln_b, w_s, b_s, w_out):
    bsz, seq, _ = h.shape
    z = jax.nn.gelu(h @ w_in, approximate=False)
    u, v = jnp.split(z, 2, axis=-1)
    v = layernorm(v, ln_g, ln_b)
    vc = v.reshape(bsz, seq // CHUNK, CHUNK, GMLP_GROUPS, GMLP_GROUP_DIM)
    causal = jnp.tril(jnp.ones((CHUNK, CHUNK), dtype=bool))
    w = jnp.where(causal[None], w_s, jnp.zeros_like(w_s)).astype(vc.dtype)
    f = jnp.einsum('gts,bcsge->bctge', w, vc) + b_s.T[:, :, None].astype(vc.dtype)
    return (u * f.reshape(bsz, seq, GMLP_WIDTH)) @ w_out


def short_conv_mixer(h, w_in, conv_w, w_out):
    d = h.shape[-1]
    gate_b, gate_c, val = jnp.split(h @ w_in, 3, axis=-1)
    z = gate_c * val
    kern = conv_w[:, None, :].astype(z.dtype)
    conv = lax.conv_general_dilated(
        z, kern, window_strides=(1,), padding=[(CONV_WIDTH - 1, 0)],
        dimension_numbers=('NWC', 'WIO', 'NWC'), feature_group_count=d)
    return (gate_b * conv) @ w_out


def mem_cross_attn(h, mem_n, wq, wkv, wo):
    bsz, seq, d = h.shape
    m = mem_n.shape[1]
    q = (h @ wq).reshape(bsz, seq, XATTN_HEADS, XATTN_HEAD_DIM)
    k, v = jnp.split(mem_n @ wkv, 2, axis=-1)
    k = k.reshape(bsz, m, XATTN_HEADS, XATTN_HEAD_DIM)
    v = v.reshape(bsz, m, XATTN_HEADS, XATTN_HEAD_DIM)
    s = jnp.einsum('bshd,bmhd->bhsm', q, k).astype(jnp.float32) * (XATTN_HEAD_DIM ** -0.5)
    p = jax.nn.softmax(s, axis=-1).astype(v.dtype)
    o = jnp.einsum('bhsm,bmhd->bshd', p, v).reshape(bsz, seq, d)
    return o @ wo


def _fwd_setup_inputs(seed: int = 0) -> dict:
    key = jax.random.key(seed)
    ks = iter(jax.random.split(key, 32))
    n_a = (DEPTH + 1) // 2
    n_b = DEPTH // 2
    D, F, E = D_MODEL, D_FF, GMLP_WIDTH

    def w(shape, fan_in):
        return jax.random.normal(next(ks), shape, jnp.float32) * (fan_in ** -0.5)

    def gain(shape):
        return 1.0 + 0.02 * jax.random.normal(next(ks), shape, jnp.float32)

    def bias(shape):
        return 0.02 * jax.random.normal(next(ks), shape, jnp.float32)

    return {
        "x": jax.random.normal(next(ks), (BATCH, SEQ, D), jnp.float32),
        "mem": jax.random.normal(next(ks), (BATCH, MEM_LEN, D), jnp.float32),
        "ffn1_norm": gain((DEPTH, D)),
        "ffn1_w13": w((DEPTH, D, 2 * F), D),
        "ffn1_w2": w((DEPTH, F, D), F),
        "mix_norm": gain((DEPTH, D)),
        "gmlp_w_in": w((n_a, D, 2 * E), D),
        "gmlp_ln_g": gain((n_a, E)),
        "gmlp_ln_b": bias((n_a, E)),
        "gmlp_w_s": w((n_a, GMLP_GROUPS, CHUNK, CHUNK), CHUNK),
        "gmlp_b_s": gain((n_a, GMLP_GROUPS, CHUNK)),
        "gmlp_w_out": w((n_a, E, D), E),
        "conv_w_in": w((n_b, D, 3 * D), D),
        "conv_w": w((n_b, CONV_WIDTH, D), CONV_WIDTH),
        "conv_w_out": w((n_b, D, D), D),
        "xattn_norm": gain((DEPTH, D)),
        "mem_norm": gain((DEPTH, D)),
        "xattn_wq": w((DEPTH, D, D), D),
        "xattn_wkv": w((DEPTH, D, 2 * D), D),
        "xattn_wo": w((DEPTH, D, D), D),
        "ffn2_norm": gain((DEPTH, D)),
        "ffn2_w13": w((DEPTH, D, 2 * F), D),
        "ffn2_w2": w((DEPTH, F, D), F),
        "final_norm": gain((D,)),
    }


def _fwd_reference(x, mem, ffn1_norm, ffn1_w13, ffn1_w2, mix_norm,
              gmlp_w_in, gmlp_ln_g, gmlp_ln_b, gmlp_w_s, gmlp_b_s, gmlp_w_out,
              conv_w_in, conv_w, conv_w_out,
              xattn_norm, mem_norm, xattn_wq, xattn_wkv, xattn_wo,
              ffn2_norm, ffn2_w13, ffn2_w2, final_norm):
    for i in range(DEPTH):
        x = x + 0.5 * swiglu(rmsnorm(x, ffn1_norm[i]), ffn1_w13[i], ffn1_w2[i])
        h = rmsnorm(x, mix_norm[i])
        j = i // N_MIXERS
        if i % N_MIXERS == 0:
            x = x + gmlp_mixer(h, gmlp_w_in[j], gmlp_ln_g[j], gmlp_ln_b[j],
                               gmlp_w_s[j], gmlp_b_s[j], gmlp_w_out[j])
        else:
            x = x + short_conv_mixer(h, conv_w_in[j], conv_w[j], conv_w_out[j])
        x = x + mem_cross_attn(rmsnorm(x, xattn_norm[i]), rmsnorm(mem, mem_norm[i]),
                               xattn_wq[i], xattn_wkv[i], xattn_wo[i])
        x = x + 0.5 * swiglu(rmsnorm(x, ffn2_norm[i]), ffn2_w13[i], ffn2_w2[i])
    return rmsnorm(x, final_norm)


import jax as _jax
import jax.numpy as _jnp

TWIN_FORMAT = 'train_step'
FWD_PARAMS = ['x', 'mem', 'ffn1_norm', 'ffn1_w13', 'ffn1_w2', 'mix_norm', 'gmlp_w_in', 'gmlp_ln_g', 'gmlp_ln_b', 'gmlp_w_s', 'gmlp_b_s', 'gmlp_w_out', 'conv_w_in', 'conv_w', 'conv_w_out', 'xattn_norm', 'mem_norm', 'xattn_wq', 'xattn_wkv', 'xattn_wo', 'ffn2_norm', 'ffn2_w13', 'ffn2_w2', 'final_norm']
TWIN_WEIGHTS = ['ffn1_norm', 'ffn1_w13', 'ffn1_w2', 'mix_norm', 'gmlp_w_in', 'gmlp_ln_g', 'gmlp_ln_b', 'gmlp_w_s', 'gmlp_b_s', 'gmlp_w_out', 'conv_w_in', 'conv_w', 'conv_w_out', 'xattn_norm', 'mem_norm', 'xattn_wq', 'xattn_wkv', 'xattn_wo', 'ffn2_norm', 'ffn2_w13', 'ffn2_w2', 'final_norm']
TWIN_DIFF_INPUT = 'x'
TWIN_INPUTS = ['x', 'mem', 'ffn1_norm', 'ffn1_w13', 'ffn1_w2', 'mix_norm', 'gmlp_w_in', 'gmlp_ln_g', 'gmlp_ln_b', 'gmlp_w_s', 'gmlp_b_s', 'gmlp_w_out', 'conv_w_in', 'conv_w', 'conv_w_out', 'xattn_norm', 'mem_norm', 'xattn_wq', 'xattn_wkv', 'xattn_wo', 'ffn2_norm', 'ffn2_w13', 'ffn2_w2', 'final_norm', 'loss_target', 'm_ffn1_norm', 'm_ffn1_w13', 'm_ffn1_w2', 'm_mix_norm', 'm_gmlp_w_in', 'm_gmlp_ln_g', 'm_gmlp_ln_b', 'm_gmlp_w_s', 'm_gmlp_b_s', 'm_gmlp_w_out', 'm_conv_w_in', 'm_conv_w', 'm_conv_w_out', 'm_xattn_norm', 'm_mem_norm', 'm_xattn_wq', 'm_xattn_wkv', 'm_xattn_wo', 'm_ffn2_norm', 'm_ffn2_w13', 'm_ffn2_w2', 'm_final_norm', 'v_ffn1_norm', 'v_ffn1_w13', 'v_ffn1_w2', 'v_mix_norm', 'v_gmlp_w_in', 'v_gmlp_ln_g', 'v_gmlp_ln_b', 'v_gmlp_w_s', 'v_gmlp_b_s', 'v_gmlp_w_out', 'v_conv_w_in', 'v_conv_w', 'v_conv_w_out', 'v_xattn_norm', 'v_mem_norm', 'v_xattn_wq', 'v_xattn_wkv', 'v_xattn_wo', 'v_ffn2_norm', 'v_ffn2_w13', 'v_ffn2_w2', 'v_final_norm']
TWIN_OUTPUTS = ['loss', 'grad_x', 'grad_ffn1_norm', 'grad_ffn1_w13', 'grad_ffn1_w2', 'grad_mix_norm', 'grad_gmlp_w_in', 'grad_gmlp_ln_g', 'grad_gmlp_ln_b', 'grad_gmlp_w_s', 'grad_gmlp_b_s', 'grad_gmlp_w_out', 'grad_conv_w_in', 'grad_conv_w', 'grad_conv_w_out', 'grad_xattn_norm', 'grad_mem_norm', 'grad_xattn_wq', 'grad_xattn_wkv', 'grad_xattn_wo', 'grad_ffn2_norm', 'grad_ffn2_w13', 'grad_ffn2_w2', 'grad_final_norm', 'delta_ffn1_norm', 'delta_ffn1_w13', 'delta_ffn1_w2', 'delta_mix_norm', 'delta_gmlp_w_in', 'delta_gmlp_ln_g', 'delta_gmlp_ln_b', 'delta_gmlp_w_s', 'delta_gmlp_b_s', 'delta_gmlp_w_out', 'delta_conv_w_in', 'delta_conv_w', 'delta_conv_w_out', 'delta_xattn_norm', 'delta_mem_norm', 'delta_xattn_wq', 'delta_xattn_wkv', 'delta_xattn_wo', 'delta_ffn2_norm', 'delta_ffn2_w13', 'delta_ffn2_w2', 'delta_final_norm', 'new_m_ffn1_norm', 'new_m_ffn1_w13', 'new_m_ffn1_w2', 'new_m_mix_norm', 'new_m_gmlp_w_in', 'new_m_gmlp_ln_g', 'new_m_gmlp_ln_b', 'new_m_gmlp_w_s', 'new_m_gmlp_b_s', 'new_m_gmlp_w_out', 'new_m_conv_w_in', 'new_m_conv_w', 'new_m_conv_w_out', 'new_m_xattn_norm', 'new_m_mem_norm', 'new_m_xattn_wq', 'new_m_xattn_wkv', 'new_m_xattn_wo', 'new_m_ffn2_norm', 'new_m_ffn2_w13', 'new_m_ffn2_w2', 'new_m_final_norm', 'new_v_ffn1_norm', 'new_v_ffn1_w13', 'new_v_ffn1_w2', 'new_v_mix_norm', 'new_v_gmlp_w_in', 'new_v_gmlp_ln_g', 'new_v_gmlp_ln_b', 'new_v_gmlp_w_s', 'new_v_gmlp_b_s', 'new_v_gmlp_w_out', 'new_v_conv_w_in', 'new_v_conv_w', 'new_v_conv_w_out', 'new_v_xattn_norm', 'new_v_mem_norm', 'new_v_xattn_wq', 'new_v_xattn_wkv', 'new_v_xattn_wo', 'new_v_ffn2_norm', 'new_v_ffn2_w13', 'new_v_ffn2_w2', 'new_v_final_norm']
TWIN_LEAF_KINDS = {'loss': 'loss', 'grad_x': 'grad_x', 'grad_ffn1_norm': 'grad_w', 'grad_ffn1_w13': 'grad_w', 'grad_ffn1_w2': 'grad_w', 'grad_mix_norm': 'grad_w', 'grad_gmlp_w_in': 'grad_w', 'grad_gmlp_ln_g': 'grad_w', 'grad_gmlp_ln_b': 'grad_w', 'grad_gmlp_w_s': 'grad_w', 'grad_gmlp_b_s': 'grad_w', 'grad_gmlp_w_out': 'grad_w', 'grad_conv_w_in': 'grad_w', 'grad_conv_w': 'grad_w', 'grad_conv_w_out': 'grad_w', 'grad_xattn_norm': 'grad_w', 'grad_mem_norm': 'grad_w', 'grad_xattn_wq': 'grad_w', 'grad_xattn_wkv': 'grad_w', 'grad_xattn_wo': 'grad_w', 'grad_ffn2_norm': 'grad_w', 'grad_ffn2_w13': 'grad_w', 'grad_ffn2_w2': 'grad_w', 'grad_final_norm': 'grad_w', 'delta_ffn1_norm': 'delta_w', 'delta_ffn1_w13': 'delta_w', 'delta_ffn1_w2': 'delta_w', 'delta_mix_norm': 'delta_w', 'delta_gmlp_w_in': 'delta_w', 'delta_gmlp_ln_g': 'delta_w', 'delta_gmlp_ln_b': 'delta_w', 'delta_gmlp_w_s': 'delta_w', 'delta_gmlp_b_s': 'delta_w', 'delta_gmlp_w_out': 'delta_w', 'delta_conv_w_in': 'delta_w', 'delta_conv_w': 'delta_w', 'delta_conv_w_out': 'delta_w', 'delta_xattn_norm': 'delta_w', 'delta_mem_norm': 'delta_w', 'delta_xattn_wq': 'delta_w', 'delta_xattn_wkv': 'delta_w', 'delta_xattn_wo': 'delta_w', 'delta_ffn2_norm': 'delta_w', 'delta_ffn2_w13': 'delta_w', 'delta_ffn2_w2': 'delta_w', 'delta_final_norm': 'delta_w', 'new_m_ffn1_norm': 'new_m', 'new_m_ffn1_w13': 'new_m', 'new_m_ffn1_w2': 'new_m', 'new_m_mix_norm': 'new_m', 'new_m_gmlp_w_in': 'new_m', 'new_m_gmlp_ln_g': 'new_m', 'new_m_gmlp_ln_b': 'new_m', 'new_m_gmlp_w_s': 'new_m', 'new_m_gmlp_b_s': 'new_m', 'new_m_gmlp_w_out': 'new_m', 'new_m_conv_w_in': 'new_m', 'new_m_conv_w': 'new_m', 'new_m_conv_w_out': 'new_m', 'new_m_xattn_norm': 'new_m', 'new_m_mem_norm': 'new_m', 'new_m_xattn_wq': 'new_m', 'new_m_xattn_wkv': 'new_m', 'new_m_xattn_wo': 'new_m', 'new_m_ffn2_norm': 'new_m', 'new_m_ffn2_w13': 'new_m', 'new_m_ffn2_w2': 'new_m', 'new_m_final_norm': 'new_m', 'new_v_ffn1_norm': 'new_v', 'new_v_ffn1_w13': 'new_v', 'new_v_ffn1_w2': 'new_v', 'new_v_mix_norm': 'new_v', 'new_v_gmlp_w_in': 'new_v', 'new_v_gmlp_ln_g': 'new_v', 'new_v_gmlp_ln_b': 'new_v', 'new_v_gmlp_w_s': 'new_v', 'new_v_gmlp_b_s': 'new_v', 'new_v_gmlp_w_out': 'new_v', 'new_v_conv_w_in': 'new_v', 'new_v_conv_w': 'new_v', 'new_v_conv_w_out': 'new_v', 'new_v_xattn_norm': 'new_v', 'new_v_mem_norm': 'new_v', 'new_v_xattn_wq': 'new_v', 'new_v_xattn_wkv': 'new_v', 'new_v_xattn_wo': 'new_v', 'new_v_ffn2_norm': 'new_v', 'new_v_ffn2_w13': 'new_v', 'new_v_ffn2_w2': 'new_v', 'new_v_final_norm': 'new_v'}


def _forward(args):
    return _fwd_reference(*[args[k] for k in FWD_PARAMS])


def _output_shape():
    def fwd():
        inp = _fwd_setup_inputs(0)
        return _fwd_reference(*[inp[k] for k in FWD_PARAMS])
    out = _jax.eval_shape(fwd)
    return out.shape, out.dtype

N_MICROBATCH = 1
ADAM_LR = 0.001
ADAM_B1 = 0.9
ADAM_B2 = 0.999
ADAM_EPS = 1e-08
ADAM_WD = 0.01
ADAM_STEP = 10
PER_EXAMPLE_BATCH_AXIS = {'x': 0, 'mem': 0, 'loss_target': 0}
SHARED_INPUTS = []
_WEIGHT_DTYPES = {'ffn1_norm': _jnp.float32, 'ffn1_w13': _jnp.float32, 'ffn1_w2': _jnp.float32, 'mix_norm': _jnp.float32, 'gmlp_w_in': _jnp.float32, 'gmlp_ln_g': _jnp.float32, 'gmlp_ln_b': _jnp.float32, 'gmlp_w_s': _jnp.float32, 'gmlp_b_s': _jnp.float32, 'gmlp_w_out': _jnp.float32, 'conv_w_in': _jnp.float32, 'conv_w': _jnp.float32, 'conv_w_out': _jnp.float32, 'xattn_norm': _jnp.float32, 'mem_norm': _jnp.float32, 'xattn_wq': _jnp.float32, 'xattn_wkv': _jnp.float32, 'xattn_wo': _jnp.float32, 'ffn2_norm': _jnp.float32, 'ffn2_w13': _jnp.float32, 'ffn2_w2': _jnp.float32, 'final_norm': _jnp.float32}
MOMENT_SCALE = {'ffn1_norm': 6.691011e-02, 'ffn1_w13': 2.808768e-02, 'ffn1_w2': 4.576656e-02, 'mix_norm': 1.319782e-01, 'gmlp_w_in': 9.073558e-02, 'gmlp_ln_g': 6.040240e-02, 'gmlp_ln_b': 6.065208e-02, 'gmlp_w_s': 8.531024e-02, 'gmlp_b_s': 1.224446e-01, 'gmlp_w_out': 1.056648e-01, 'conv_w_in': 7.446156e-02, 'conv_w': 7.398104e-02, 'conv_w_out': 7.482019e-02, 'xattn_norm': 1.096393e-02, 'mem_norm': 1.658684e-02, 'xattn_wq': 1.095568e-02, 'xattn_wkv': 1.111938e-02, 'xattn_wo': 1.123230e-02, 'ffn2_norm': 4.468534e-02, 'ffn2_w13': 1.872667e-02, 'ffn2_w2': 3.057911e-02, 'final_norm': 3.198794e+01}


def _to_microbatches(a, axis):
    t = _jnp.moveaxis(a, axis, 0)
    t = t.reshape((N_MICROBATCH, t.shape[0] // N_MICROBATCH) + t.shape[1:])
    return _jnp.moveaxis(t, 1, axis + 1)


def setup_inputs(seed: int = 0) -> dict:
    inp = _fwd_setup_inputs(seed)
    key = _jax.random.fold_in(_jax.random.key(seed), 7919)
    shape, _ = _output_shape()
    out = dict(inp)
    out["loss_target"] = _jax.random.normal(_jax.random.fold_in(key, 0), shape, _jnp.float32)
    for i, name in enumerate(TWIN_WEIGHTS):
        w = inp[name].astype(_jnp.float32)
        if MOMENT_SCALE is None:
            s = _jnp.sqrt(_jnp.mean(_jnp.square(w)) + 1e-30)
        else:
            s = MOMENT_SCALE[name]
        km, kv = _jax.random.split(_jax.random.fold_in(key, i + 1))
        out[name] = w
        out["m_" + name] = s * _jax.random.normal(km, w.shape, _jnp.float32)
        out["v_" + name] = (s * s) * _jax.random.uniform(kv, w.shape, _jnp.float32, 0.5, 1.5)
    if N_MICROBATCH > 1:
        for name, axis in PER_EXAMPLE_BATCH_AXIS.items():
            out[name] = _to_microbatches(out[name], axis)
    return {'x': out['x'], 'mem': out['mem'], 'ffn1_norm': out['ffn1_norm'], 'ffn1_w13': out['ffn1_w13'], 'ffn1_w2': out['ffn1_w2'], 'mix_norm': out['mix_norm'], 'gmlp_w_in': out['gmlp_w_in'], 'gmlp_ln_g': out['gmlp_ln_g'], 'gmlp_ln_b': out['gmlp_ln_b'], 'gmlp_w_s': out['gmlp_w_s'], 'gmlp_b_s': out['gmlp_b_s'], 'gmlp_w_out': out['gmlp_w_out'], 'conv_w_in': out['conv_w_in'], 'conv_w': out['conv_w'], 'conv_w_out': out['conv_w_out'], 'xattn_norm': out['xattn_norm'], 'mem_norm': out['mem_norm'], 'xattn_wq': out['xattn_wq'], 'xattn_wkv': out['xattn_wkv'], 'xattn_wo': out['xattn_wo'], 'ffn2_norm': out['ffn2_norm'], 'ffn2_w13': out['ffn2_w13'], 'ffn2_w2': out['ffn2_w2'], 'final_norm': out['final_norm'], 'loss_target': out['loss_target'], 'm_ffn1_norm': out['m_ffn1_norm'], 'm_ffn1_w13': out['m_ffn1_w13'], 'm_ffn1_w2': out['m_ffn1_w2'], 'm_mix_norm': out['m_mix_norm'], 'm_gmlp_w_in': out['m_gmlp_w_in'], 'm_gmlp_ln_g': out['m_gmlp_ln_g'], 'm_gmlp_ln_b': out['m_gmlp_ln_b'], 'm_gmlp_w_s': out['m_gmlp_w_s'], 'm_gmlp_b_s': out['m_gmlp_b_s'], 'm_gmlp_w_out': out['m_gmlp_w_out'], 'm_conv_w_in': out['m_conv_w_in'], 'm_conv_w': out['m_conv_w'], 'm_conv_w_out': out['m_conv_w_out'], 'm_xattn_norm': out['m_xattn_norm'], 'm_mem_norm': out['m_mem_norm'], 'm_xattn_wq': out['m_xattn_wq'], 'm_xattn_wkv': out['m_xattn_wkv'], 'm_xattn_wo': out['m_xattn_wo'], 'm_ffn2_norm': out['m_ffn2_norm'], 'm_ffn2_w13': out['m_ffn2_w13'], 'm_ffn2_w2': out['m_ffn2_w2'], 'm_final_norm': out['m_final_norm'], 'v_ffn1_norm': out['v_ffn1_norm'], 'v_ffn1_w13': out['v_ffn1_w13'], 'v_ffn1_w2': out['v_ffn1_w2'], 'v_mix_norm': out['v_mix_norm'], 'v_gmlp_w_in': out['v_gmlp_w_in'], 'v_gmlp_ln_g': out['v_gmlp_ln_g'], 'v_gmlp_ln_b': out['v_gmlp_ln_b'], 'v_gmlp_w_s': out['v_gmlp_w_s'], 'v_gmlp_b_s': out['v_gmlp_b_s'], 'v_gmlp_w_out': out['v_gmlp_w_out'], 'v_conv_w_in': out['v_conv_w_in'], 'v_conv_w': out['v_conv_w'], 'v_conv_w_out': out['v_conv_w_out'], 'v_xattn_norm': out['v_xattn_norm'], 'v_mem_norm': out['v_mem_norm'], 'v_xattn_wq': out['v_xattn_wq'], 'v_xattn_wkv': out['v_xattn_wkv'], 'v_xattn_wo': out['v_xattn_wo'], 'v_ffn2_norm': out['v_ffn2_norm'], 'v_ffn2_w13': out['v_ffn2_w13'], 'v_ffn2_w2': out['v_ffn2_w2'], 'v_final_norm': out['v_final_norm']}


def _loss(weights, diff, rest, loss_target):
    with _jax.named_scope("forward"):
        args = {**rest, TWIN_DIFF_INPUT: diff, **{k: w.astype(_WEIGHT_DTYPES[k]) for k, w in weights.items()}}
        y = _forward(args)
    with _jax.named_scope("loss_head"):
        err = _jnp.square(y.astype(_jnp.float32) - loss_target)
        return 0.5 * _jnp.sum(_jnp.mean(err, axis=-1)) if err.ndim else 0.5 * err


def _adamw(w, g, m, v):
    m = ADAM_B1 * m + (1.0 - ADAM_B1) * g
    v = ADAM_B2 * v + (1.0 - ADAM_B2) * _jnp.square(g)
    m_hat = m / (1.0 - ADAM_B1 ** ADAM_STEP)
    v_hat = v / (1.0 - ADAM_B2 ** ADAM_STEP)
    delta = -ADAM_LR * (m_hat / (_jnp.sqrt(v_hat) + ADAM_EPS) + ADAM_WD * w)
    return delta, m, v


def reference(x, mem, ffn1_norm, ffn1_w13, ffn1_w2, mix_norm, gmlp_w_in, gmlp_ln_g, gmlp_ln_b, gmlp_w_s, gmlp_b_s, gmlp_w_out, conv_w_in, conv_w, conv_w_out, xattn_norm, mem_norm, xattn_wq, xattn_wkv, xattn_wo, ffn2_norm, ffn2_w13, ffn2_w2, final_norm, loss_target, m_ffn1_norm, m_ffn1_w13, m_ffn1_w2, m_mix_norm, m_gmlp_w_in, m_gmlp_ln_g, m_gmlp_ln_b, m_gmlp_w_s, m_gmlp_b_s, m_gmlp_w_out, m_conv_w_in, m_conv_w, m_conv_w_out, m_xattn_norm, m_mem_norm, m_xattn_wq, m_xattn_wkv, m_xattn_wo, m_ffn2_norm, m_ffn2_w13, m_ffn2_w2, m_final_norm, v_ffn1_norm, v_ffn1_w13, v_ffn1_w2, v_mix_norm, v_gmlp_w_in, v_gmlp_ln_g, v_gmlp_ln_b, v_gmlp_w_s, v_gmlp_b_s, v_gmlp_w_out, v_conv_w_in, v_conv_w, v_conv_w_out, v_xattn_norm, v_mem_norm, v_xattn_wq, v_xattn_wkv, v_xattn_wo, v_ffn2_norm, v_ffn2_w13, v_ffn2_w2, v_final_norm):
    given = dict(x=x, mem=mem, ffn1_norm=ffn1_norm, ffn1_w13=ffn1_w13, ffn1_w2=ffn1_w2, mix_norm=mix_norm, gmlp_w_in=gmlp_w_in, gmlp_ln_g=gmlp_ln_g, gmlp_ln_b=gmlp_ln_b, gmlp_w_s=gmlp_w_s, gmlp_b_s=gmlp_b_s, gmlp_w_out=gmlp_w_out, conv_w_in=conv_w_in, conv_w=conv_w, conv_w_out=conv_w_out, xattn_norm=xattn_norm, mem_norm=mem_norm, xattn_wq=xattn_wq, xattn_wkv=xattn_wkv, xattn_wo=xattn_wo, ffn2_norm=ffn2_norm, ffn2_w13=ffn2_w13, ffn2_w2=ffn2_w2, final_norm=final_norm, loss_target=loss_target, m_ffn1_norm=m_ffn1_norm, m_ffn1_w13=m_ffn1_w13, m_ffn1_w2=m_ffn1_w2, m_mix_norm=m_mix_norm, m_gmlp_w_in=m_gmlp_w_in, m_gmlp_ln_g=m_gmlp_ln_g, m_gmlp_ln_b=m_gmlp_ln_b, m_gmlp_w_s=m_gmlp_w_s, m_gmlp_b_s=m_gmlp_b_s, m_gmlp_w_out=m_gmlp_w_out, m_conv_w_in=m_conv_w_in, m_conv_w=m_conv_w, m_conv_w_out=m_conv_w_out, m_xattn_norm=m_xattn_norm, m_mem_norm=m_mem_norm, m_xattn_wq=m_xattn_wq, m_xattn_wkv=m_xattn_wkv, m_xattn_wo=m_xattn_wo, m_ffn2_norm=m_ffn2_norm, m_ffn2_w13=m_ffn2_w13, m_ffn2_w2=m_ffn2_w2, m_final_norm=m_final_norm, v_ffn1_norm=v_ffn1_norm, v_ffn1_w13=v_ffn1_w13, v_ffn1_w2=v_ffn1_w2, v_mix_norm=v_mix_norm, v_gmlp_w_in=v_gmlp_w_in, v_gmlp_ln_g=v_gmlp_ln_g, v_gmlp_ln_b=v_gmlp_ln_b, v_gmlp_w_s=v_gmlp_w_s, v_gmlp_b_s=v_gmlp_b_s, v_gmlp_w_out=v_gmlp_w_out, v_conv_w_in=v_conv_w_in, v_conv_w=v_conv_w, v_conv_w_out=v_conv_w_out, v_xattn_norm=v_xattn_norm, v_mem_norm=v_mem_norm, v_xattn_wq=v_xattn_wq, v_xattn_wkv=v_xattn_wkv, v_xattn_wo=v_xattn_wo, v_ffn2_norm=v_ffn2_norm, v_ffn2_w13=v_ffn2_w13, v_ffn2_w2=v_ffn2_w2, v_final_norm=v_final_norm)
    weights = {n: given[n] for n in TWIN_WEIGHTS}
    shared = {n: given[n] for n in SHARED_INPUTS}
    per_example = {n: given[n] for n in ['x', 'mem']}
    grad_fn = _jax.value_and_grad(_loss, argnums=(0, 1))

    def one_microbatch(ex, loss_target):
        ex = dict(ex)
        diff = ex.pop(TWIN_DIFF_INPUT)
        return grad_fn(weights, diff, {**shared, **ex}, loss_target)

    if N_MICROBATCH == 1:
        loss, (grad_w, grad_x) = one_microbatch(per_example, given["loss_target"])
    else:
        def body(carry, xs):
            loss_sum, grad_sum = carry
            l_k, (gw_k, gx_k) = one_microbatch(xs[0], xs[1])
            with _jax.named_scope("update"):
                return (loss_sum + l_k, _jax.tree.map(_jnp.add, grad_sum, gw_k)), gx_k

        init = (_jnp.zeros((), _jnp.float32), _jax.tree.map(_jnp.zeros_like, weights))
        (loss, grad_w), grad_x = _jax.lax.scan(body, init, (per_example, given["loss_target"]))
    with _jax.named_scope("update"):
        delta_w, new_m, new_v = {}, {}, {}
        for n in TWIN_WEIGHTS:
            delta_w[n], new_m[n], new_v[n] = _adamw(weights[n], grad_w[n], given["m_" + n], given["v_" + n])
    return (loss, grad_x, *[grad_w[n] for n in TWIN_WEIGHTS], *[delta_w[n] for n in TWIN_WEIGHTS],
            *[new_m[n] for n in TWIN_WEIGHTS], *[new_v[n] for n in TWIN_WEIGHTS])
```

```python
import functools
import math

import jax
import jax.numpy as jnp
from jax import lax
from jax.experimental import pallas as pl
from jax.experimental.pallas import tpu as pltpu

F32 = jnp.float32
BF16 = jnp.bfloat16
MESH = pl.DeviceIdType.MESH

CHUNK = 128
GMLP_GROUPS = 8
XATTN_HEADS = 4
RMS_EPS = 1e-6
LN_EPS = 1e-5
ADAM_LR = 0.001
ADAM_B1 = 0.9
ADAM_B2 = 0.999
ADAM_EPS = 1e-08
ADAM_WD = 0.01
ADAM_STEP = 10

N_CHIPS = 4
N_DEV = 8

VMEM_LIMIT_BYTES = 56 * 2**20
VMEM_PLAN_BYTES = 36 * 2**20
STEP_COST_BYTES = 1.3e6
LANE = 128

WEIGHTS = ['ffn1_norm', 'ffn1_w13', 'ffn1_w2', 'mix_norm', 'gmlp_w_in', 'gmlp_ln_g', 'gmlp_ln_b', 'gmlp_w_s',
           'gmlp_b_s', 'gmlp_w_out', 'conv_w_in', 'conv_w', 'conv_w_out', 'xattn_norm', 'mem_norm', 'xattn_wq',
           'xattn_wkv', 'xattn_wo', 'ffn2_norm', 'ffn2_w13', 'ffn2_w2', 'final_norm']
BIG = {'ffn1_w13': 'col', 'ffn1_w2': 'row', 'gmlp_w_in': 'col', 'gmlp_w_out': 'row', 'conv_w_in': 'col',
       'conv_w_out': 'row', 'xattn_wq': 'row', 'xattn_wkv': 'col', 'xattn_wo': 'row', 'ffn2_w13': 'col',
       'ffn2_w2': 'row'}
ARG_NAMES = (['x', 'mem'] + WEIGHTS + ['loss_target'] + ['m_' + n for n in WEIGHTS] + ['v_' + n for n in WEIGHTS])


def _params(**kw):
    return pltpu.CompilerParams(vmem_limit_bytes=VMEM_LIMIT_BYTES, **kw)


def _divisors(n, mult, cap):
    return [d for d in range(mult, min(n, cap) + 1, mult) if n % d == 0] or [n]


def _row_tile(rows, width_bytes, budget=4 * 2**20):
    best = None
    for d in _divisors(rows, 16, 1024):
        if d * width_bytes <= budget:
            best = d
    return best or _divisors(rows, 16, 1024)[0]


def _plan_mm(m, n_tiles_of, k_tiles_of, n, k, a_item, o_item, has_res):
    best, best_cost = None, None
    for tm in _divisors(m, 16, 1024):
        for tn in n_tiles_of:
            for tk in k_tiles_of:
                ni, nj, nk = m // tm, n // tn, k // tk
                blocks = tm * tk * a_item + tk * tn * 2 + tm * tn * o_item + (tm * tn * 4 if has_res else 0)
                vmem = 2 * blocks + tm * tn * 4 * (2 if nk > 1 else 1)
                if vmem > VMEM_PLAN_BYTES:
                    continue
                traffic = nj * m * k * a_item + (k * n * 2 if nk == 1 else ni * k * n * 2)
                traffic += m * n * (o_item + (4 if has_res else 0))
                cost = traffic + ni * nj * nk * STEP_COST_BYTES
                if nk > 1:
                    cost += 0.25 * 2 * tm * tn * 4 * ni * nj * nk
                if best_cost is None or cost < best_cost:
                    best, best_cost = (tm, tn, tk), cost
    assert best is not None, (m, n, k)
    return best


def _mm_call(name, grid, operands, in_specs, out_shape, out_spec, contract, nk, scale, has_res, tile):
    def body(*refs):
        a_ref, b_ref = refs[0], refs[1]
        res_ref = refs[2] if has_res else None
        o_ref = refs[3] if has_res else refs[2]
        acc_ref = refs[-1] if nk > 1 else None

        def finish(v):
            if scale != 1.0:
                v = v * scale
            if has_res:
                v = res_ref[...] + v
            o_ref[...] = v.astype(o_ref.dtype)

        part = lax.dot_general(a_ref[...], b_ref[...], contract, preferred_element_type=F32)
        if nk == 1:
            finish(part)
        else:
            kk = pl.program_id(2)

            @pl.when(kk == 0)
            def _():
                acc_ref[...] = part

            @pl.when(jnp.logical_and(kk > 0, kk < nk - 1))
            def _():
                acc_ref[...] += part

            @pl.when(kk == nk - 1)
            def _():
                finish(acc_ref[...] + part)

    return pl.pallas_call(
        body, name=name, grid=grid, in_specs=in_specs, out_specs=out_spec, out_shape=out_shape,
        scratch_shapes=[pltpu.VMEM(tile, F32)] if nk > 1 else [],
        compiler_params=_params(dimension_semantics=("arbitrary", "arbitrary", "arbitrary")),
    )(*operands)


def mm_nn(name, a, w, kind, out_dtype, res=None, scale=1.0):
    m, k = a.shape
    p, r, c = w.shape
    n = p * c if kind == 'col' else c
    assert k == (r if kind == 'col' else p * r), (name, a.shape, w.shape)
    n_tiles = _divisors(c, LANE, 2816)
    k_tiles = _divisors(r, LANE, 4096)
    tm, tn, tk = _plan_mm(m, n_tiles, k_tiles, n, k, a.dtype.itemsize, jnp.dtype(out_dtype).itemsize, res is not None)
    nk = k // tk
    if kind == 'col':
        cpt = c // tn
        w_spec = pl.BlockSpec((None, tk, tn), lambda j, i, kk: (j // cpt, kk, j % cpt))
    else:
        rpt = r // tk
        w_spec = pl.BlockSpec((None, tk, tn), lambda j, i, kk: (kk // rpt, kk % rpt, j))
    in_specs = [pl.BlockSpec((tm, tk), lambda j, i, kk: (i, kk)), w_spec]
    operands = [a, w]
    if res is not None:
        in_specs.append(pl.BlockSpec((tm, tn), lambda j, i, kk: (i, j)))
        operands.append(res)
    return _mm_call(name, (n // tn, m // tm, nk), operands, in_specs, jax.ShapeDtypeStruct((m, n), out_dtype),
                    pl.BlockSpec((tm, tn), lambda j, i, kk: (i, j)), (((1,), (0,)), ((), ())), nk, scale,
                    res is not None, (tm, tn))


def mm_nt(name, a, w, kind, out_dtype, scale=1.0):
    m, kc = a.shape
    p, r, c = w.shape
    n = r if kind == 'col' else p * r
    assert kc == (p * c if kind == 'col' else c), (name, a.shape, w.shape)
    n_tiles = _divisors(r, LANE, 2816)
    k_tiles = _divisors(c, LANE, 4096)
    tm, tn, tk = _plan_mm(m, n_tiles, k_tiles, n, kc, a.dtype.itemsize, jnp.dtype(out_dtype).itemsize, False)
    nk = kc // tk
    if kind == 'col':
        cpt = c // tk
        w_spec = pl.BlockSpec((None, tn, tk), lambda j, i, kk: (kk // cpt, j, kk % cpt))
    else:
        rpt = r // tn
        w_spec = pl.BlockSpec((None, tn, tk), lambda j, i, kk: (j // rpt, j % rpt, kk))
    in_specs = [pl.BlockSpec((tm, tk), lambda j, i, kk: (i, kk)), w_spec]
    return _mm_call(name, (n // tn, m // tm, nk), [a, w], in_specs, jax.ShapeDtypeStruct((m, n), out_dtype),
                    pl.BlockSpec((tm, tn), lambda j, i, kk: (i, j)), (((1,), (1,)), ((), ())), nk, scale, False,
                    (tm, tn))


def _plan_tn(s, ka, nd, r_tiles, n_tiles):
    best, best_cost = None, None
    for ts in _divisors(s, 16, 2048):
        for tr in r_tiles:
            for tn in n_tiles:
                ni, nj, ns = ka // tr, nd // tn, s // ts
                vmem = 2 * (ts * tr * 2 + ts * tn * 2 + tr * tn * 2) + tr * tn * 4 * (2 if ns > 1 else 1)
                if vmem > VMEM_PLAN_BYTES:
                    continue
                traffic = nj * s * ka * 2 + ni * s * nd * 2 + ka * nd * 2
                cost = traffic + ni * nj * ns * STEP_COST_BYTES
                if ns > 1:
                    cost += 0.25 * 2 * tr * tn * 4 * ni * nj * ns
                if best_cost is None or cost < best_cost:
                    best, best_cost = (ts, tr, tn), cost
    assert best is not None, (s, ka, nd)
    return best


def mm_tn(name, a, dy, kind, scale=1.0):
    s, ka = a.shape
    s2, nd = dy.shape
    assert s == s2
    p = N_CHIPS
    r, c = (ka, nd // p) if kind == 'col' else (ka // p, nd)
    ts, tr, tn = _plan_tn(s, ka, nd, _divisors(r, LANE, 2048), _divisors(c, LANE, 2816))
    ns = s // ts
    if kind == 'col':
        cpt = c // tn
        o_spec = pl.BlockSpec((None, tr, tn), lambda j, i, kk: (j // cpt, i, j % cpt))
    else:
        rpt = r // tr
        o_spec = pl.BlockSpec((None, tr, tn), lambda j, i, kk: (i // rpt, i % rpt, j))
    in_specs = [pl.BlockSpec((ts, tr), lambda j, i, kk: (kk, i)), pl.BlockSpec((ts, tn), lambda j, i, kk: (kk, j))]
    return _mm_call(name, (nd // tn, ka // tr, ns), [a, dy], in_specs, jax.ShapeDtypeStruct((p, r, c), BF16), o_spec,
                    (((0,), (0,)), ((), ())), ns, scale, False, (tr, tn))


def _rms_rows(x, g):
    r = lax.rsqrt(jnp.mean(x * x, axis=-1, keepdims=True) + RMS_EPS)
    xhat = x * r
    return xhat, r, xhat * g


def rms_fwd(name, x, g):
    s, d = x.shape
    tm = _row_tile(s, d * 4)

    def body(x_ref, g_ref, o_ref):
        o_ref[...] = _rms_rows(x_ref[...], g_ref[...])[2].astype(BF16)

    return pl.pallas_call(
        body, name=name, grid=(s // tm,),
        in_specs=[pl.BlockSpec((tm, d), lambda i: (i, 0)), pl.BlockSpec((1, d), lambda i: (0, 0))],
        out_specs=pl.BlockSpec((tm, d), lambda i: (i, 0)), out_shape=jax.ShapeDtypeStruct((s, d), BF16),
        compiler_params=_params(dimension_semantics=("arbitrary",)),
    )(x, g.reshape(1, d))


def _rms_bwd_rows(x, g, dh):
    xhat, r, _ = _rms_rows(x, g)
    u = dh * g
    dx = r * (u - xhat * jnp.mean(u * xhat, axis=-1, keepdims=True))
    return dx, jnp.sum(dh * xhat, axis=0, keepdims=True)


def rms_bwd(name, x, g, dh, dres):
    s, d = x.shape
    tm = _row_tile(s, d * 4, 2 * 2**20)
    has_res = dres is not None

    def body(*refs):
        x_ref, g_ref, dh_ref = refs[:3]
        dres_ref = refs[3] if has_res else None
        dx_ref, dxb_ref, dg_ref = refs[-3:]
        dx, dg = _rms_bwd_rows(x_ref[...], g_ref[...], dh_ref[...].astype(F32))
        if has_res:
            dx = dx + dres_ref[...]
        dx_ref[...] = dx
        dxb_ref[...] = dx.astype(BF16)

        @pl.when(pl.program_id(0) == 0)
        def _():
            dg_ref[...] = dg

        @pl.when(pl.program_id(0) > 0)
        def _():
            dg_ref[...] += dg

    row = pl.BlockSpec((tm, d), lambda i: (i, 0))
    vec = pl.BlockSpec((1, d), lambda i: (0, 0))
    return pl.pallas_call(
        body, name=name, grid=(s // tm,),
        in_specs=[row, vec, row] + ([row] if has_res else []),
        out_specs=[row, row, vec],
        out_shape=[jax.ShapeDtypeStruct((s, d), F32), jax.ShapeDtypeStruct((s, d), BF16),
                   jax.ShapeDtypeStruct((1, d), F32)],
        compiler_params=_params(dimension_semantics=("arbitrary",)),
    )(x, g.reshape(1, d), dh, *([dres] if has_res else []))


def loss_head(name, x, g, target):
    s, d = x.shape
    tm = _row_tile(s, d * 4, 2 * 2**20)

    def body(x_ref, g_ref, t_ref, dx_ref, dxb_ref, dg_ref, loss_ref):
        x = x_ref[...]
        gain = g_ref[...]
        y = _rms_rows(x, gain)[2]
        diff = y - t_ref[...]
        dx, dg = _rms_bwd_rows(x, gain, diff * (1.0 / d))
        dx_ref[...] = dx
        dxb_ref[...] = dx.astype(BF16)
        sq = jnp.sum(diff * diff, axis=0, keepdims=True)

        @pl.when(pl.program_id(0) == 0)
        def _():
            dg_ref[...] = dg
            loss_ref[...] = sq

        @pl.when(pl.program_id(0) > 0)
        def _():
            dg_ref[...] += dg
            loss_ref[...] += sq

    row = pl.BlockSpec((tm, d), lambda i: (i, 0))
    vec = pl.BlockSpec((1, d), lambda i: (0, 0))
    return pl.pallas_call(
        body, name=name, grid=(s // tm,), in_specs=[row, vec, row], out_specs=[row, row, vec, vec],
        out_shape=[jax.ShapeDtypeStruct((s, d), F32), jax.ShapeDtypeStruct((s, d), BF16),
                   jax.ShapeDtypeStruct((1, d), F32), jax.ShapeDtypeStruct((1, d), F32)],
        compiler_params=_params(dimension_semantics=("arbitrary",)),
    )(x, g.reshape(1, d), target)


def _sigmoid(x):
    return 1.0 / (1.0 + jnp.exp(-x))


def swiglu_fwd(name, gu):
    s, f2 = gu.shape
    f = f2 // 2
    tf = _divisors(f, LANE, 1408)[-1]
    tm = _row_tile(s, tf * 4, 2 * 2**20)
    nf = f // tf

    def body(g_ref, u_ref, o_ref):
        g = g_ref[...].astype(F32)
        o_ref[...] = (g * _sigmoid(g) * u_ref[...].astype(F32)).astype(BF16)

    return pl.pallas_call(
        body, name=name, grid=(s // tm, nf),
        in_specs=[pl.BlockSpec((tm, tf), lambda i, j: (i, j)), pl.BlockSpec((tm, tf), lambda i, j: (i, j + nf))],
        out_specs=pl.BlockSpec((tm, tf), lambda i, j: (i, j)), out_shape=jax.ShapeDtypeStruct((s, f), BF16),
        compiler_params=_params(dimension_semantics=("arbitrary", "arbitrary")),
    )(gu, gu)


def swiglu_bwd(name, da, gu):
    s, f2 = gu.shape
    f = f2 // 2
    tf = _divisors(f, LANE, 1408)[-1]
    tm = _row_tile(s, tf * 4, 2 * 2**20)
    nf = f // tf

    def body(da_ref, g_ref, u_ref, o_ref):
        g = g_ref[...].astype(F32)
        d = da_ref[...].astype(F32)
        sg = _sigmoid(g)
        is_gate = pl.program_id(1) < nf
        d_gate = d * u_ref[...].astype(F32) * (sg * (1.0 + g * (1.0 - sg)))
        d_up = d * (g * sg)
        o_ref[...] = jnp.where(is_gate, d_gate, d_up).astype(BF16)

    return pl.pallas_call(
        body, name=name, grid=(s // tm, 2 * nf),
        in_specs=[pl.BlockSpec((tm, tf), lambda i, j: (i, j % nf)), pl.BlockSpec((tm, tf), lambda i, j: (i, j % nf)),
                  pl.BlockSpec((tm, tf), lambda i, j: (i, j % nf + nf))],
        out_specs=pl.BlockSpec((tm, tf), lambda i, j: (i, j)), out_shape=jax.ShapeDtypeStruct((s, f2), BF16),
        compiler_params=_params(dimension_semantics=("arbitrary", "arbitrary")),
    )(da, gu, gu)


_INV_SQRT2 = 0.7071067811865476
_INV_SQRT_2PI = 0.3989422804014327


def _gelu(z):
    return 0.5 * z * (1.0 + lax.erf(z * _INV_SQRT2))


def _gelu_grad(z):
    return 0.5 * (1.0 + lax.erf(z * _INV_SQRT2)) + z * (_INV_SQRT_2PI * jnp.exp(-0.5 * z * z))


def _causal_weights(ws_ref, g):
    t = ws_ref.shape[-1]
    keep = lax.broadcasted_iota(jnp.int32, (t, t), 0) >= lax.broadcasted_iota(jnp.int32, (t, t), 1)
    return jnp.where(keep, ws_ref[g], 0.0).astype(BF16), keep


def _gmlp_gate_rows(z_ref, lg_ref, lb_ref, e):
    z = z_ref[...].astype(F32)
    gz = _gelu(z)
    u, v = gz[:, :e], gz[:, e:]
    mu = jnp.mean(v, axis=-1, keepdims=True)
    xc = v - mu
    rs = lax.rsqrt(jnp.mean(xc * xc, axis=-1, keepdims=True) + LN_EPS)
    vhat = xc * rs
    return z, u, vhat, rs, vhat * lg_ref[...] + lb_ref[...]


def gmlp_fwd(name, z, ln_g, ln_b, w_s, bias):
    s, e2 = z.shape
    e = e2 // 2
    eg = e // GMLP_GROUPS

    def body(z_ref, lg_ref, lb_ref, ws_ref, b_ref, o_ref):
        _, u, _, _, vln = _gmlp_gate_rows(z_ref, lg_ref, lb_ref, e)
        vb = vln.astype(BF16)
        for g in range(GMLP_GROUPS):
            cols = slice(g * eg, (g + 1) * eg)
            wm, _ = _causal_weights(ws_ref, g)
            f = jnp.dot(wm, vb[:, cols], preferred_element_type=F32) + b_ref[:, cols]
            o_ref[:, cols] = (u[:, cols] * f).astype(BF16)

    full = lambda shape: pl.BlockSpec(shape, lambda i: (0,) * len(shape))
    return pl.pallas_call(
        body, name=name, grid=(s // CHUNK,),
        in_specs=[pl.BlockSpec((CHUNK, e2), lambda i: (i, 0)), full((1, e)), full((1, e)),
                  full((GMLP_GROUPS, CHUNK, CHUNK)), full((CHUNK, e))],
        out_specs=pl.BlockSpec((CHUNK, e), lambda i: (i, 0)), out_shape=jax.ShapeDtypeStruct((s, e), BF16),
        compiler_params=_params(dimension_semantics=("arbitrary",)),
    )(z, ln_g.reshape(1, e), ln_b.reshape(1, e), w_s, bias)


def gmlp_bwd(name, z, dp, ln_g, ln_b, w_s, bias):
    s, e2 = z.shape
    e = e2 // 2
    eg = e // GMLP_GROUPS
    t = CHUNK

    def body(z_ref, dp_ref, lg_ref, lb_ref, ws_ref, b_ref, dz_ref, dws_ref, dbs_ref, dlg_ref, dlb_ref):
        first = pl.program_id(0) == 0
        zf, u, vhat, rs, vln = _gmlp_gate_rows(z_ref, lg_ref, lb_ref, e)
        vb = vln.astype(BF16)
        dp = dp_ref[...].astype(F32)
        lane = lax.broadcasted_iota(jnp.int32, (t, LANE), 1)
        dbs = jnp.zeros((t, LANE), F32)
        dvln_parts = []
        for g in range(GMLP_GROUPS):
            cols = slice(g * eg, (g + 1) * eg)
            wm, keep = _causal_weights(ws_ref, g)
            f = jnp.dot(wm, vb[:, cols], preferred_element_type=F32) + b_ref[:, cols]
            dz_ref[:, cols] = (dp[:, cols] * f * _gelu_grad(zf[:, cols])).astype(BF16)
            df = dp[:, cols] * u[:, cols]
            dfb = df.astype(BF16)
            dbs = dbs + jnp.where(lane == g, jnp.sum(df, axis=-1, keepdims=True), 0.0)
            dw = lax.dot_general(dfb, vb[:, cols], (((1,), (1,)), ((), ())), preferred_element_type=F32)
            dw = jnp.where(keep, dw, 0.0)

            @pl.when(first)
            def _():
                dws_ref[g] = dw

            @pl.when(jnp.logical_not(first))
            def _():
                dws_ref[g] += dw

            dvln_parts.append(lax.dot_general(wm, dfb, (((0,), (0,)), ((), ())), preferred_element_type=F32))
        dvln = jnp.concatenate(dvln_parts, axis=-1)
        dvhat = dvln * lg_ref[...]
        dv = rs * (dvhat - jnp.mean(dvhat, axis=-1, keepdims=True)
                   - vhat * jnp.mean(dvhat * vhat, axis=-1, keepdims=True))
        dz_ref[:, e:] = (dv * _gelu_grad(zf[:, e:])).astype(BF16)
        dlg = jnp.sum(dvln * vhat, axis=0, keepdims=True)
        dlb = jnp.sum(dvln, axis=0, keepdims=True)

        @pl.when(first)
        def _():
            dbs_ref[...] = dbs
            dlg_ref[...] = dlg
            dlb_ref[...] = dlb

        @pl.when(jnp.logical_not(first))
        def _():
            dbs_ref[...] += dbs
            dlg_ref[...] += dlg
            dlb_ref[...] += dlb

    full = lambda shape: pl.BlockSpec(shape, lambda i: (0,) * len(shape))
    return pl.pallas_call(
        body, name=name, grid=(s // t,),
        in_specs=[pl.BlockSpec((t, e2), lambda i: (i, 0)), pl.BlockSpec((t, e), lambda i: (i, 0)), full((1, e)),
                  full((1, e)), full((GMLP_GROUPS, t, t)), full((t, e))],
        out_specs=[pl.BlockSpec((t, e2), lambda i: (i, 0)), full((GMLP_GROUPS, t, t)), full((t, LANE)), full((1, e)),
                   full((1, e))],
        out_shape=[jax.ShapeDtypeStruct((s, e2), BF16), jax.ShapeDtypeStruct((GMLP_GROUPS, t, t), F32),
                   jax.ShapeDtypeStruct((t, LANE), F32), jax.ShapeDtypeStruct((1, e), F32),
                   jax.ShapeDtypeStruct((1, e), F32)],
        compiler_params=_params(dimension_semantics=("arbitrary",)),
    )(z, dp, ln_g.reshape(1, e), ln_b.reshape(1, e), w_s, bias)


EDGE = 16


def _shift_down(zc, prev, k):
    tm = zc.shape[0]
    row = lax.broadcasted_iota(jnp.int32, (tm, 1), 0)
    out = pltpu.roll(zc, k, 0)
    for j in range(k):
        out = jnp.where(row == j, prev[EDGE - k + j:EDGE - k + j + 1, :], out)
    return out


def _shift_up(dc, nxt, k):
    tm = dc.shape[0]
    row = lax.broadcasted_iota(jnp.int32, (tm, 1), 0)
    out = pltpu.roll(dc, tm - k, 0)
    for j in range(k):
        out = jnp.where(row == tm - k + j, nxt[j:j + 1, :], out)
    return out


def conv_fwd(name, bcv, cw):
    s, d3 = bcv.shape
    d = d3 // 3
    tm = _row_tile(s, d * 4, 2 * 2**20)
    per = tm // EDGE

    def body(b_ref, c_ref, v_ref, cp_ref, vp_ref, w_ref, o_ref):
        i = pl.program_id(0)
        zc = c_ref[...].astype(F32) * v_ref[...].astype(F32)
        prev = jnp.where(i > 0, cp_ref[...].astype(F32) * vp_ref[...].astype(F32), 0.0)
        conv = w_ref[2:3, :] * zc + w_ref[1:2, :] * _shift_down(zc, prev, 1) + w_ref[0:1, :] * _shift_down(zc, prev, 2)
        o_ref[...] = (b_ref[...].astype(F32) * conv).astype(BF16)

    blk = lambda col: pl.BlockSpec((tm, d), lambda i: (i, col))
    edge = lambda col: pl.BlockSpec((EDGE, d), lambda i: (jnp.maximum(i * per - 1, 0), col))
    return pl.pallas_call(
        body, name=name, grid=(s // tm,),
        in_specs=[blk(0), blk(1), blk(2), edge(1), edge(2), pl.BlockSpec((3, d), lambda i: (0, 0))],
        out_specs=pl.BlockSpec((tm, d), lambda i: (i, 0)), out_shape=jax.ShapeDtypeStruct((s, d), BF16),
        compiler_params=_params(dimension_semantics=("arbitrary",)),
    )(bcv, bcv, bcv, bcv, bcv, cw)


def conv_bwd(name, bcv, dq, cw):
    s, d3 = bcv.shape
    d = d3 // 3
    tm = _row_tile(s, d * 4, 2**20)
    per = tm // EDGE
    n_tiles = s // tm
    last_edge = s // EDGE - 1

    def body(b_ref, c_ref, v_ref, cp_ref, vp_ref, bn_ref, dq_ref, dqn_ref, w_ref, o_ref, dw_ref):
        i = pl.program_id(0)
        b = b_ref[...].astype(F32)
        c = c_ref[...].astype(F32)
        v = v_ref[...].astype(F32)
        dq = dq_ref[...].astype(F32)
        zc = c * v
        prev = jnp.where(i > 0, cp_ref[...].astype(F32) * vp_ref[...].astype(F32), 0.0)
        z1 = _shift_down(zc, prev, 1)
        z2 = _shift_down(zc, prev, 2)
        w0, w1, w2 = w_ref[0:1, :], w_ref[1:2, :], w_ref[2:3, :]
        conv = w2 * zc + w1 * z1 + w0 * z2
        dconv = dq * b
        nxt = jnp.where(i < n_tiles - 1, dqn_ref[...].astype(F32) * bn_ref[...].astype(F32), 0.0)
        dz = w2 * dconv + w1 * _shift_up(dconv, nxt, 1) + w0 * _shift_up(dconv, nxt, 2)
        o_ref[:, :d] = (dq * conv).astype(BF16)
        o_ref[:, d:2 * d] = (dz * v).astype(BF16)
        o_ref[:, 2 * d:] = (dz * c).astype(BF16)
        dw = jnp.concatenate([jnp.sum(dconv * z2, axis=0, keepdims=True), jnp.sum(dconv * z1, axis=0, keepdims=True),
                              jnp.sum(dconv * zc, axis=0, keepdims=True), jnp.zeros((5, d), F32)], axis=0)

        @pl.when(i == 0)
        def _():
            dw_ref[...] = dw

        @pl.when(i > 0)
        def _():
            dw_ref[...] += dw

    blk = lambda col: pl.BlockSpec((tm, d), lambda i: (i, col))
    before = lambda col: pl.BlockSpec((EDGE, d), lambda i: (jnp.maximum(i * per - 1, 0), col))
    after = lambda col: pl.BlockSpec((EDGE, d), lambda i: (jnp.minimum((i + 1) * per, last_edge), col))
    return pl.pallas_call(
        body, name=name, grid=(n_tiles,),
        in_specs=[blk(0), blk(1), blk(2), before(1), before(2), after(0), blk(0), after(0),
                  pl.BlockSpec((3, d), lambda i: (0, 0))],
        out_specs=[pl.BlockSpec((tm, d3), lambda i: (i, 0)), pl.BlockSpec((8, d), lambda i: (0, 0))],
        out_shape=[jax.ShapeDtypeStruct((s, d3), BF16), jax.ShapeDtypeStruct((8, d), F32)],
        compiler_params=_params(dimension_semantics=("arbitrary",)),
    )(bcv, bcv, bcv, bcv, bcv, bcv, dq, dq, cw)


def _attn_probs(qh, kh, scale):
    sc = lax.dot_general(qh, kh, (((1,), (1,)), ((), ())), preferred_element_type=F32) * scale
    ex = jnp.exp(sc - jnp.max(sc, axis=-1, keepdims=True))
    return ex / jnp.sum(ex, axis=-1, keepdims=True)


def attn_fwd(name, q, kv):
    s, d = q.shape
    mlen = kv.shape[0]
    dh = d // XATTN_HEADS
    scale = dh ** -0.5
    tm = _row_tile(s, d * 4, 2 * 2**20)

    def body(q_ref, kv_ref, o_ref):
        for h in range(XATTN_HEADS):
            cols = slice(h * dh, (h + 1) * dh)
            p = _attn_probs(q_ref[:, cols], kv_ref[:, cols], scale)
            o_ref[:, cols] = jnp.dot(p.astype(BF16), kv_ref[:, d + h * dh:d + (h + 1) * dh],
                                     preferred_element_type=F32).astype(BF16)

    return pl.pallas_call(
        body, name=name, grid=(s // tm,),
        in_specs=[pl.BlockSpec((tm, d), lambda i: (i, 0)), pl.BlockSpec((mlen, 2 * d), lambda i: (0, 0))],
        out_specs=pl.BlockSpec((tm, d), lambda i: (i, 0)), out_shape=jax.ShapeDtypeStruct((s, d), BF16),
        compiler_params=_params(dimension_semantics=("arbitrary",)),
    )(q, kv)


def attn_bwd(name, q, kv, do):
    s, d = q.shape
    mlen = kv.shape[0]
    dh = d // XATTN_HEADS
    scale = dh ** -0.5
    tm = _row_tile(s, d * 4, 2 * 2**20)

    def body(q_ref, kv_ref, do_ref, dq_ref, dkv_ref):
        first = pl.program_id(0) == 0
        for h in range(XATTN_HEADS):
            cols = slice(h * dh, (h + 1) * dh)
            vcols = slice(d + h * dh, d + (h + 1) * dh)
            qh, kh, vh, doh = q_ref[:, cols], kv_ref[:, cols], kv_ref[:, vcols], do_ref[:, cols]
            p = _attn_probs(qh, kh, scale)
            dp = lax.dot_general(doh, vh, (((1,), (1,)), ((), ())), preferred_element_type=F32)
            ds = (p * (dp - jnp.sum(dp * p, axis=-1, keepdims=True)) * scale).astype(BF16)
            dq_ref[:, cols] = jnp.dot(ds, kh, preferred_element_type=F32).astype(BF16)
            dk = lax.dot_general(ds, qh, (((0,), (0,)), ((), ())), preferred_element_type=F32)
            dv = lax.dot_general(p.astype(BF16), doh, (((0,), (0,)), ((), ())), preferred_element_type=F32)

            @pl.when(first)
            def _():
                dkv_ref[:, cols] = dk
                dkv_ref[:, vcols] = dv

            @pl.when(jnp.logical_not(first))
            def _():
                dkv_ref[:, cols] += dk
                dkv_ref[:, vcols] += dv

    row = pl.BlockSpec((tm, d), lambda i: (i, 0))
    whole = pl.BlockSpec((mlen, 2 * d), lambda i: (0, 0))
    return pl.pallas_call(
        body, name=name, grid=(s // tm,), in_specs=[row, whole, row], out_specs=[row, whole],
        out_shape=[jax.ShapeDtypeStruct((s, d), BF16), jax.ShapeDtypeStruct((mlen, 2 * d), F32)],
        compiler_params=_params(dimension_semantics=("arbitrary",)),
    )(q, kv, do)


def _as_rows(a):
    if a.ndim >= 2 and a.shape[-1] % LANE == 0:
        return a.reshape(-1, a.shape[-1])
    return a.reshape(-1, LANE) if a.size % LANE == 0 else a.reshape(1, -1)


def add_halves(name, dw, other, core):
    p, r, c = dw.shape
    h = r // 2
    th = _row_tile(h, c * 2, 2 * 2**20)

    def body(core_ref, a_ref, b_ref, o_ref):
        o_ref[...] = (a_ref[...].astype(F32) + b_ref[...].astype(F32)).astype(BF16)

    grid_spec = pltpu.PrefetchScalarGridSpec(
        num_scalar_prefetch=1, grid=(p, h // th),
        in_specs=[pl.BlockSpec((None, None, th, c), lambda pi, i, core_ref: (pi, core_ref[0], i, 0)),
                  pl.BlockSpec((None, th, c), lambda pi, i, core_ref: (pi, i, 0))],
        out_specs=pl.BlockSpec((None, th, c), lambda pi, i, core_ref: (pi, i, 0)))
    return pl.pallas_call(
        body, name=name, grid_spec=grid_spec, out_shape=jax.ShapeDtypeStruct((p, h, c), BF16),
        compiler_params=_params(dimension_semantics=("arbitrary", "arbitrary")),
    )(core, dw.reshape(p, 2, h, c), other)


def sum_leading(name, parts):
    n, r, c = parts.shape
    tr = _row_tile(r, c * 4 * 2, 2 * 2**20)

    def body(p_ref, o_ref):
        acc = p_ref[0].astype(F32)
        for k in range(1, n):
            acc = acc + p_ref[k].astype(F32)
        o_ref[...] = acc

    return pl.pallas_call(
        body, name=name, grid=(r // tr,), in_specs=[pl.BlockSpec((n, tr, c), lambda i: (0, i, 0))],
        out_specs=pl.BlockSpec((tr, c), lambda i: (i, 0)), out_shape=jax.ShapeDtypeStruct((r, c), F32),
        compiler_params=_params(dimension_semantics=("arbitrary",)),
    )(parts)


def _adamw_rows(w, g, m, v):
    m = ADAM_B1 * m + (1.0 - ADAM_B1) * g
    v = ADAM_B2 * v + (1.0 - ADAM_B2) * (g * g)
    m_hat = m / (1.0 - ADAM_B1 ** ADAM_STEP)
    v_hat = v / (1.0 - ADAM_B2 ** ADAM_STEP)
    delta = -ADAM_LR * (m_hat / (jnp.sqrt(v_hat) + ADAM_EPS) + ADAM_WD * w)
    return delta, m, v


def adamw_layer(name, w, m, v, g, layer, carried):
    nl, r, c = w.shape
    tr = _row_tile(r, c * 4, 2**20)
    n_carried = 4 if carried is not None else 0

    def body(*refs):
        w_ref, m_ref, v_ref, g_ref = refs[:4]
        go_ref, d_ref, mo_ref, vo_ref = refs[4 + n_carried:]
        g = g_ref[...]
        delta, m_new, v_new = _adamw_rows(w_ref[...], g, m_ref[...], v_ref[...])
        go_ref[...] = g
        d_ref[...] = delta
        mo_ref[...] = m_new
        vo_ref[...] = v_new

    stacked = pl.BlockSpec((None, tr, c), lambda i: (layer, i, 0))
    in_specs = [stacked, stacked, stacked, pl.BlockSpec((tr, c), lambda i: (i, 0))]
    in_specs += [pl.BlockSpec(memory_space=pl.ANY)] * n_carried
    shape = jax.ShapeDtypeStruct((nl, r, c), F32)
    return pl.pallas_call(
        body, name=name, grid=(r // tr,), in_specs=in_specs, out_specs=[stacked] * 4, out_shape=[shape] * 4,
        input_output_aliases={4 + k: k for k in range(n_carried)},
        compiler_params=_params(dimension_semantics=("arbitrary",)),
    )(w, m, v, g, *(carried or ()))


def adamw_flat(name, w, m, v, g):
    r, c = w.shape

    def body(w_ref, m_ref, v_ref, g_ref, d_ref, mo_ref, vo_ref):
        delta, m_new, v_new = _adamw_rows(w_ref[...], g_ref[...], m_ref[...], v_ref[...])
        d_ref[...] = delta
        mo_ref[...] = m_new
        vo_ref[...] = v_new

    shape = jax.ShapeDtypeStruct((r, c), F32)
    return pl.pallas_call(body, name=name, out_shape=[shape] * 3, compiler_params=_params())(w, m, v, g)


ANY = pl.BlockSpec(memory_space=pl.ANY)


def _place():
    x, y, c = lax.axis_index("x"), lax.axis_index("y"), lax.axis_index("c")
    chips = [(1 - x, y), (x, 1 - y), (1 - x, 1 - y)]
    return x, y, c, 2 * x + y, chips


def _remote(src, dst, sems, k, device):
    return pltpu.make_async_remote_copy(src_ref=src, dst_ref=dst, send_sem=sems[0].at[k], recv_sem=sems[1].at[k],
                                        device_id=device, device_id_type=MESH)


def gather_weights(name, shards, layer):
    n = len(shards)

    def body(*refs):
        ins, outs = refs[:n], refs[n:2 * n]
        send_sems, recv_sems, local_sems = refs[2 * n:]
        sems = (send_sems, recv_sems)
        x, y, c, mine, chips = _place()
        sibling = (x, y, 1 - c)
        local, sent, passed = [], [], []
        for a in range(n):
            src = ins[a].at[layer]
            h = src.shape[0] // 2
            half = pl.ds(pl.multiple_of(c * h, 16), h)
            cp = pltpu.make_async_copy(src, outs[a].at[mine], local_sems.at[a])
            cp.start()
            local.append(cp)
            for j, chip in enumerate(chips):
                cp = _remote(src.at[half], outs[a].at[mine, half], sems, 6 * a + j, (*chip, c))
                cp.start()
                sent.append(cp)
        for a in range(n):
            h = outs[a].shape[1] // 2
            half = pl.ds(pl.multiple_of(c * h, 16), h)
            for j, chip in enumerate(chips):
                rows = outs[a].at[2 * chip[0] + chip[1], half]
                _remote(rows, rows, sems, 6 * a + j, sibling).wait_recv()
                cp = _remote(rows, rows, sems, 6 * a + 3 + j, sibling)
                cp.start()
                passed.append(cp)
        for a in range(n):
            h = outs[a].shape[1] // 2
            other = pl.ds(pl.multiple_of((1 - c) * h, 16), h)
            for j, chip in enumerate(chips):
                rows = outs[a].at[2 * chip[0] + chip[1], other]
                _remote(rows, rows, sems, 6 * a + 3 + j, sibling).wait_recv()
        for cp in sent + passed:
            cp.wait_send()
        for cp in local:
            cp.wait()

    return pl.pallas_call(
        body, name=name, in_specs=[ANY] * n, out_specs=[ANY] * n,
        out_shape=[jax.ShapeDtypeStruct((N_CHIPS,) + s.shape[1:], s.dtype) for s in shards],
        scratch_shapes=[pltpu.SemaphoreType.DMA((6 * n,)), pltpu.SemaphoreType.DMA((6 * n,)),
                        pltpu.SemaphoreType.DMA((n,))],
    )(*shards)


def swap_halves(name, grads):
    n = len(grads)

    def body(*refs):
        ins, outs = refs[:n], refs[n:2 * n]
        sems = refs[2 * n:]
        x, y, c, _, _ = _place()
        copies = []
        for a in range(n):
            h = ins[a].shape[1] // 2
            theirs = pl.ds(pl.multiple_of((1 - c) * h, 16), h)
            cp = _remote(ins[a].at[:, theirs], outs[a], sems, a, (x, y, 1 - c))
            cp.start()
            copies.append(cp)
        for cp in copies:
            cp.wait()

    return pl.pallas_call(
        body, name=name, in_specs=[ANY] * n, out_specs=[ANY] * n,
        out_shape=[jax.ShapeDtypeStruct((g.shape[0], g.shape[1] // 2, g.shape[2]), g.dtype) for g in grads],
        scratch_shapes=[pltpu.SemaphoreType.DMA((n,)), pltpu.SemaphoreType.DMA((n,))],
    )(*grads)


def scatter_panels(name, parts):
    n = len(parts)

    def body(*refs):
        ins, outs = refs[:n], refs[n:2 * n]
        send_sems, recv_sems, local_sems = refs[2 * n:]
        sems = (send_sems, recv_sems)
        x, y, c, mine, chips = _place()
        local, sent = [], []
        for a in range(n):
            cp = pltpu.make_async_copy(ins[a].at[mine], outs[a].at[mine], local_sems.at[a])
            cp.start()
            local.append(cp)
            for j, chip in enumerate(chips):
                cp = _remote(ins[a].at[2 * chip[0] + chip[1]], outs[a].at[mine], sems, 3 * a + j, (*chip, c))
                cp.start()
                sent.append(cp)
        for a in range(n):
            for j, chip in enumerate(chips):
                landing = outs[a].at[2 * chip[0] + chip[1]]
                _remote(landing, landing, sems, 3 * a + j, (*chip, c)).wait_recv()
        for cp in sent:
            cp.wait_send()
        for cp in local:
            cp.wait()

    return pl.pallas_call(
        body, name=name, in_specs=[ANY] * n, out_specs=[ANY] * n,
        out_shape=[jax.ShapeDtypeStruct(g.shape, g.dtype) for g in parts],
        scratch_shapes=[pltpu.SemaphoreType.DMA((3 * n,)), pltpu.SemaphoreType.DMA((3 * n,)),
                        pltpu.SemaphoreType.DMA((n,))],
    )(*parts)


def join_halves(name, halves):
    n = len(halves)

    def body(*refs):
        ins, outs = refs[:n], refs[n:2 * n]
        send_sems, recv_sems, local_sems = refs[2 * n:]
        sems = (send_sems, recv_sems)
        x, y, c, _, _ = _place()
        sibling = (x, y, 1 - c)
        local, sent = [], []
        for a in range(n):
            cp = pltpu.make_async_copy(ins[a], outs[a].at[c], local_sems.at[a])
            cp.start()
            local.append(cp)
            cp = _remote(ins[a], outs[a].at[c], sems, a, sibling)
            cp.start()
            sent.append(cp)
        for a in range(n):
            landing = outs[a].at[1 - c]
            _remote(landing, landing, sems, a, sibling).wait_recv()
        for cp in sent:
            cp.wait_send()
        for cp in local:
            cp.wait()

    return pl.pallas_call(
        body, name=name, in_specs=[ANY] * n, out_specs=[ANY] * n,
        out_shape=[jax.ShapeDtypeStruct((2,) + g.shape, g.dtype) for g in halves],
        scratch_shapes=[pltpu.SemaphoreType.DMA((n,)), pltpu.SemaphoreType.DMA((n,)), pltpu.SemaphoreType.DMA((n,))],
    )(*halves)


def gather_all(name, rows):
    def body(in_ref, out_ref, send_sems, recv_sems, local_sem):
        sems = (send_sems, recv_sems)
        x, y, c, _, _ = _place()
        me = 4 * x + 2 * y + c
        local = pltpu.make_async_copy(in_ref, out_ref.at[me], local_sem)
        local.start()
        peers = [(1 - x if k & 4 else x, 1 - y if k & 2 else y, 1 - c if k & 1 else c) for k in range(1, N_DEV)]
        sent = []
        for k, peer in enumerate(peers):
            cp = _remote(in_ref, out_ref.at[me], sems, k, peer)
            cp.start()
            sent.append(cp)
        for k, peer in enumerate(peers):
            landing = out_ref.at[4 * peer[0] + 2 * peer[1] + peer[2]]
            _remote(landing, landing, sems, k, peer).wait_recv()
        for cp in sent:
            cp.wait_send()
        local.wait()

    return pl.pallas_call(
        body, name=name, in_specs=[ANY], out_specs=ANY,
        out_shape=jax.ShapeDtypeStruct((N_DEV,) + rows.shape, rows.dtype),
        scratch_shapes=[pltpu.SemaphoreType.DMA((N_DEV - 1,)), pltpu.SemaphoreType.DMA((N_DEV - 1,)),
                        pltpu.SemaphoreType.DMA],
    )(rows)


def _ffn_fwd(tag, x, gain, w13, w2):
    h = rms_fwd(tag + "_norm", x, gain)
    gu = mm_nn(tag + "_w13", h, w13, 'col', BF16)
    act = swiglu_fwd(tag + "_act", gu)
    out = mm_nn(tag + "_w2", act, w2, 'row', F32, res=x, scale=0.5)
    return out, (x, h, gu, act)


def _ffn_bwd(tag, dx, dxb, saved, gain, w13, w2):
    x, h, gu, act = saved
    d_act = mm_nt(tag + "_dact", dxb, w2, 'row', BF16, scale=0.5)
    d_w2 = mm_tn(tag + "_dw2", act, dxb, 'row', scale=0.5)
    d_gu = swiglu_bwd(tag + "_dgu", d_act, gu)
    d_w13 = mm_tn(tag + "_dw13", h, d_gu, 'col')
    dh = mm_nt(tag + "_dh", d_gu, w13, 'col', F32)
    dx, dxb, d_gain = rms_bwd(tag + "_dnorm", x, gain, dh, dx)
    return dx, dxb, d_gain, d_w13, d_w2


def kernel(x, mem, ffn1_norm, ffn1_w13, ffn1_w2, mix_norm, gmlp_w_in, gmlp_ln_g, gmlp_ln_b, gmlp_w_s, gmlp_b_s, gmlp_w_out, conv_w_in, conv_w, conv_w_out, xattn_norm, mem_norm, xattn_wq, xattn_wkv, xattn_wo, ffn2_norm, ffn2_w13, ffn2_w2, final_norm, loss_target, m_ffn1_norm, m_ffn1_w13, m_ffn1_w2, m_mix_norm, m_gmlp_w_in, m_gmlp_ln_g, m_gmlp_ln_b, m_gmlp_w_s, m_gmlp_b_s, m_gmlp_w_out, m_conv_w_in, m_conv_w, m_conv_w_out, m_xattn_norm, m_mem_norm, m_xattn_wq, m_xattn_wkv, m_xattn_wo, m_ffn2_norm, m_ffn2_w13, m_ffn2_w2, m_final_norm, v_ffn1_norm, v_ffn1_w13, v_ffn1_w2, v_mix_norm, v_gmlp_w_in, v_gmlp_ln_g, v_gmlp_ln_b, v_gmlp_w_s, v_gmlp_b_s, v_gmlp_w_out, v_conv_w_in, v_conv_w, v_conv_w_out, v_xattn_norm, v_mem_norm, v_xattn_wq, v_xattn_wkv, v_xattn_wo, v_ffn2_norm, v_ffn2_w13, v_ffn2_w2, v_final_norm):
    return _step(dict(locals()))


def _step(p):
    assert sorted(p) == sorted(ARG_NAMES)
    x = p['x'][0]
    mem = p['mem'][0]
    target = p['loss_target'][0]
    s, d = x.shape
    depth = p['ffn1_norm'].shape[0]
    core = lax.axis_index("c").astype(jnp.int32).reshape(1)
    chip = 2 * lax.axis_index("x") + lax.axis_index("y")

    wb = {n: p[n].astype(BF16) for n in BIG}
    groups = {}
    for l in range(depth):
        j = l // 2
        mixer = ['gmlp_w_in', 'gmlp_w_out'] if l % 2 == 0 else ['conv_w_in', 'conv_w_out']
        for tag, names, idx in (("ffn1", ['ffn1_w13', 'ffn1_w2'], l), ("mix", mixer, j),
                                ("xattn", ['xattn_wq', 'xattn_wkv', 'xattn_wo'], l), ("ffn2", ['ffn2_w13', 'ffn2_w2'], l)):
            got = gather_weights(f"gather_{tag}{l}", [wb[n] for n in names], idx)
            groups[(tag, l)] = dict(zip(names, got, strict=True))

    cw_local = p['conv_w']
    n_conv, cwid, dq4 = cw_local.shape
    cw_rows = jnp.pad(cw_local.reshape(-1, LANE), ((0, (-cw_local.size // LANE) % 8), (0, 0)))
    cw_all = gather_all("gather_conv_w", cw_rows)[0::2, :cw_local.size // LANE]
    conv_w_full = cw_all.reshape(N_CHIPS, n_conv, cwid, dq4).transpose(1, 2, 0, 3).reshape(n_conv, cwid, N_CHIPS * dq4)

    saved = []
    for l in range(depth):
        j = l // 2
        rec = {}
        g1 = groups[("ffn1", l)]
        x, rec['ffn1'] = _ffn_fwd(f"l{l}_ffn1", x, p['ffn1_norm'][l], g1['ffn1_w13'], g1['ffn1_w2'])
        gm = groups[("mix", l)]
        h = rms_fwd(f"l{l}_mix_norm", x, p['mix_norm'][l])
        if l % 2 == 0:
            e = p['gmlp_ln_g'].shape[-1]
            bias = jnp.repeat(p['gmlp_b_s'][j].T, e // GMLP_GROUPS, axis=1)
            z = mm_nn(f"l{l}_gmlp_in", h, gm['gmlp_w_in'], 'col', BF16)
            gate = gmlp_fwd(f"l{l}_gmlp_gate", z, p['gmlp_ln_g'][j], p['gmlp_ln_b'][j], p['gmlp_w_s'][j], bias)
            x_new = mm_nn(f"l{l}_gmlp_out", gate, gm['gmlp_w_out'], 'row', F32, res=x)
            rec['mix'] = (x, h, z, gate, bias)
        else:
            bcv = mm_nn(f"l{l}_conv_in", h, gm['conv_w_in'], 'col', BF16)
            gate = conv_fwd(f"l{l}_conv_gate", bcv, conv_w_full[j])
            x_new = mm_nn(f"l{l}_conv_out", gate, gm['conv_w_out'], 'row', F32, res=x)
            rec['mix'] = (x, h, bcv, gate)
        x = x_new
        ga = groups[("xattn", l)]
        hq = rms_fwd(f"l{l}_xattn_norm", x, p['xattn_norm'][l])
        q = mm_nn(f"l{l}_xattn_q", hq, ga['xattn_wq'], 'row', BF16)
        mem_n = rms_fwd(f"l{l}_mem_norm", mem, p['mem_norm'][l])
        kv = mm_nn(f"l{l}_xattn_kv", mem_n, ga['xattn_wkv'], 'col', BF16)
        o = attn_fwd(f"l{l}_xattn_core", q, kv)
        x_new = mm_nn(f"l{l}_xattn_o", o, ga['xattn_wo'], 'row', F32, res=x)
        rec['xattn'] = (x, hq, q, mem_n, kv, o)
        x = x_new
        g2 = groups[("ffn2", l)]
        x, rec['ffn2'] = _ffn_fwd(f"l{l}_ffn2", x, p['ffn2_norm'][l], g2['ffn2_w13'], g2['ffn2_w2'])
        saved.append(rec)

    dx, dxb, d_final, loss_lanes = loss_head("loss_head", x, p['final_norm'], target)
    loss = lax.psum(0.5 * jnp.sum(loss_lanes) / d, ("x", "y", "c"))

    big_grads = {}
    small = {n: [None] * p[n].shape[0] for n in ('ffn1_norm', 'mix_norm', 'xattn_norm', 'mem_norm', 'ffn2_norm',
                                                  'gmlp_ln_g', 'gmlp_ln_b', 'gmlp_w_s', 'gmlp_b_s', 'conv_w')}
    for l in reversed(range(depth)):
        j = l // 2
        rec = saved[l]
        g2 = groups[("ffn2", l)]
        dx, dxb, dg, dw13, dw2 = _ffn_bwd(f"l{l}_ffn2", dx, dxb, rec['ffn2'], p['ffn2_norm'][l], g2['ffn2_w13'], g2['ffn2_w2'])
        small['ffn2_norm'][l] = dg
        big_grads[('ffn2_w13', l)], big_grads[('ffn2_w2', l)] = dw13, dw2

        ga = groups[("xattn", l)]
        x_in, hq, q, mem_n, kv, o = rec['xattn']
        do = mm_nt(f"l{l}_xattn_do", dxb, ga['xattn_wo'], 'row', BF16)
        big_grads[('xattn_wo', l)] = mm_tn(f"l{l}_xattn_dwo", o, dxb, 'row')
        dq, dkv = attn_bwd(f"l{l}_xattn_dcore", q, kv, do)
        big_grads[('xattn_wq', l)] = mm_tn(f"l{l}_xattn_dwq", hq, dq, 'row')
        dh = mm_nt(f"l{l}_xattn_dh", dq, ga['xattn_wq'], 'row', F32)
        dx, dxb, small['xattn_norm'][l] = rms_bwd(f"l{l}_xattn_dnorm", x_in, p['xattn_norm'][l], dh, dx)
        dkvb = dkv.astype(BF16)
        big_grads[('xattn_wkv', l)] = mm_tn(f"l{l}_xattn_dwkv", mem_n, dkvb, 'col')
        dmem_n = mm_nt(f"l{l}_xattn_dmem", dkvb, ga['xattn_wkv'], 'col', F32)
        small['mem_norm'][l] = rms_bwd(f"l{l}_mem_dnorm", mem, p['mem_norm'][l], dmem_n, None)[2]

        gm = groups[("mix", l)]
        if l % 2 == 0:
            x_in, h, z, gate, bias = rec['mix']
            dgate = mm_nt(f"l{l}_gmlp_dgate", dxb, gm['gmlp_w_out'], 'row', BF16)
            big_grads[('gmlp_w_out', j)] = mm_tn(f"l{l}_gmlp_dwout", gate, dxb, 'row')
            dz, dws, dbs, dlg, dlb = gmlp_bwd(f"l{l}_gmlp_dgate_core", z, dgate, p['gmlp_ln_g'][j], p['gmlp_ln_b'][j],
                                              p['gmlp_w_s'][j], bias)
            small['gmlp_w_s'][j], small['gmlp_b_s'][j] = dws, dbs[:, :GMLP_GROUPS].T
            small['gmlp_ln_g'][j], small['gmlp_ln_b'][j] = dlg, dlb
            big_grads[('gmlp_w_in', j)] = mm_tn(f"l{l}_gmlp_dwin", h, dz, 'col')
            dh = mm_nt(f"l{l}_gmlp_dh", dz, gm['gmlp_w_in'], 'col', F32)
        else:
            x_in, h, bcv, gate = rec['mix']
            dgate = mm_nt(f"l{l}_conv_dgate", dxb, gm['conv_w_out'], 'row', BF16)
            big_grads[('conv_w_out', j)] = mm_tn(f"l{l}_conv_dwout", gate, dxb, 'row')
            dbcv, dcw = conv_bwd(f"l{l}_conv_dgate_core", bcv, dgate, conv_w_full[j])
            small['conv_w'][j] = dcw[:cwid]
            big_grads[('conv_w_in', j)] = mm_tn(f"l{l}_conv_dwin", h, dbcv, 'col')
            dh = mm_nt(f"l{l}_conv_dh", dbcv, gm['conv_w_in'], 'col', F32)
        dx, dxb, small['mix_norm'][l] = rms_bwd(f"l{l}_mix_dnorm", x_in, p['mix_norm'][l], dh, dx)

        g1 = groups[("ffn1", l)]
        dx, dxb, dg, dw13, dw2 = _ffn_bwd(f"l{l}_ffn1", dx, dxb, rec['ffn1'], p['ffn1_norm'][l], g1['ffn1_w13'], g1['ffn1_w2'])
        small['ffn1_norm'][l] = dg
        big_grads[('ffn1_w13', l)], big_grads[('ffn1_w2', l)] = dw13, dw2

    keys = sorted(big_grads)
    theirs = swap_halves("reduce_swap", [big_grads[k] for k in keys])
    partial = [add_halves(f"reduce_add_{n}{i}", big_grads[(n, i)], t, core) for (n, i), t in zip(keys, theirs, strict=True)]
    landed = scatter_panels("reduce_scatter", partial)
    halves = [sum_leading(f"reduce_sum_{n}{i}", t) for (n, i), t in zip(keys, landed, strict=True)]
    joined = join_halves("reduce_join", halves)
    big_g = {k: g.reshape(-1, g.shape[-1]) for k, g in zip(keys, joined, strict=True)}

    small_names = ['ffn1_norm', 'mix_norm', 'xattn_norm', 'mem_norm', 'ffn2_norm', 'gmlp_ln_g', 'gmlp_ln_b', 'gmlp_w_s',
                   'gmlp_b_s', 'final_norm', 'conv_w']
    small_full = {n: jnp.stack([g.reshape(p[n].shape[1:]) for g in small[n]]) for n in small_names
                  if n not in ('final_norm', 'conv_w')}
    small_full['final_norm'] = d_final.reshape(p['final_norm'].shape)
    small_full['conv_w'] = jnp.stack(small['conv_w'])
    packed = jnp.concatenate([small_full[n].reshape(-1, LANE) for n in small_names], axis=0)
    total = sum_leading("small_sum", gather_all("small_gather", packed))
    small_g, at = {}, 0
    for n in small_names:
        rows = small_full[n].size // LANE
        small_g[n] = total[at:at + rows].reshape(small_full[n].shape)
        at += rows
    small_g['conv_w'] = lax.dynamic_slice_in_dim(small_g['conv_w'], chip * dq4, dq4, axis=2)

    grads, deltas, new_m, new_v = {}, {}, {}, {}
    for n in WEIGHTS:
        w, m, v = p[n], p['m_' + n], p['v_' + n]
        if n in BIG:
            carried = None
            for i in range(w.shape[0]):
                carried = adamw_layer(f"adamw_{n}{i}", w, m, v, big_g[(n, i)], i, carried)
            grads[n], deltas[n], new_m[n], new_v[n] = carried
        else:
            g = small_g[n]
            out = adamw_flat(f"adamw_{n}", _as_rows(w), _as_rows(m), _as_rows(v), _as_rows(g))
            grads[n] = g
            deltas[n], new_m[n], new_v[n] = (o.reshape(w.shape) for o in out)

    grad_x = dx.reshape(p['x'].shape)
    return (loss, grad_x, *[grads[n] for n in WEIGHTS], *[deltas[n] for n in WEIGHTS], *[new_m[n] for n in WEIGHTS],
            *[new_v[n] for n in WEIGHTS])
```

```python
from typing import Callable, NamedTuple

import jax
import jax.numpy as jnp
from jax import lax
from jax.experimental import pallas as pl
from jax.experimental.pallas import tpu as pltpu

F32 = jnp.float32
BF16 = jnp.bfloat16
MESH = pl.DeviceIdType.MESH

CHUNK = 128
GMLP_GROUPS = 8
XATTN_HEADS = 4
RMS_EPS = 1e-6
LN_EPS = 1e-5
ADAM_LR = 0.001
ADAM_B1 = 0.9
ADAM_B2 = 0.999
ADAM_EPS = 1e-08
ADAM_WD = 0.01
ADAM_STEP = 10

N_CHIPS = 4
N_DEV = 8

VMEM_LIMIT_BYTES = 56 * 2**20
VMEM_PLAN_BYTES = 36 * 2**20
STEP_COST_BYTES = 1.3e6
LANE = 128

WEIGHTS = ['ffn1_norm', 'ffn1_w13', 'ffn1_w2', 'mix_norm', 'gmlp_w_in', 'gmlp_ln_g', 'gmlp_ln_b', 'gmlp_w_s',
           'gmlp_b_s', 'gmlp_w_out', 'conv_w_in', 'conv_w', 'conv_w_out', 'xattn_norm', 'mem_norm', 'xattn_wq',
           'xattn_wkv', 'xattn_wo', 'ffn2_norm', 'ffn2_w13', 'ffn2_w2', 'final_norm']
BIG = {'ffn1_w13': 'col', 'ffn1_w2': 'row', 'gmlp_w_in': 'col', 'gmlp_w_out': 'row', 'conv_w_in': 'col',
       'conv_w_out': 'row', 'xattn_wq': 'row', 'xattn_wkv': 'col', 'xattn_wo': 'row', 'ffn2_w13': 'col',
       'ffn2_w2': 'row'}
ARG_NAMES = (['x', 'mem'] + WEIGHTS + ['loss_target'] + ['m_' + n for n in WEIGHTS] + ['v_' + n for n in WEIGHTS])


def _params(**kw):
    return pltpu.CompilerParams(vmem_limit_bytes=VMEM_LIMIT_BYTES, **kw)


def _divisors(n, mult, cap):
    return [d for d in range(mult, min(n, cap) + 1, mult) if n % d == 0] or [n]


def _row_tile(rows, width_bytes, budget=4 * 2**20):
    best = None
    for d in _divisors(rows, 16, 1024):
        if d * width_bytes <= budget:
            best = d
    return best or _divisors(rows, 16, 1024)[0]


ANY = pl.BlockSpec(memory_space=pl.ANY)


class Exchange(NamedTuple):
    inputs: list
    out_shapes: list
    aliases: dict
    n_sems: int
    start: Callable
    finish: Callable


def _place():
    x, y, c = lax.axis_index("x"), lax.axis_index("y"), lax.axis_index("c")
    chips = [(1 - x, y), (x, 1 - y), (1 - x, 1 - y)]
    return x, y, c, 2 * x + y, chips


def _remote(src, dst, sems, k, device):
    return pltpu.make_async_remote_copy(src_ref=src, dst_ref=dst, send_sem=sems[0].at[k], recv_sem=sems[1].at[k],
                                        device_id=device, device_id_type=MESH)


def _plan_mm(m, n_tiles_of, k_tiles_of, n, k, a_item, o_item, has_res):
    best, best_cost = None, None
    for tm in _divisors(m, 16, 1024):
        for tn in n_tiles_of:
            for tk in k_tiles_of:
                ni, nj, nk = m // tm, n // tn, k // tk
                blocks = tm * tk * a_item + tk * tn * 2 + tm * tn * o_item + (tm * tn * 4 if has_res else 0)
                vmem = 2 * blocks + tm * tn * 4 * (2 if nk > 1 else 1)
                if vmem > VMEM_PLAN_BYTES:
                    continue
                traffic = nj * m * k * a_item + (k * n * 2 if nk == 1 else ni * k * n * 2)
                traffic += m * n * (o_item + (4 if has_res else 0))
                cost = traffic + ni * nj * nk * STEP_COST_BYTES
                if nk > 1:
                    cost += 0.25 * 2 * tm * tn * 4 * ni * nj * nk
                if best_cost is None or cost < best_cost:
                    best, best_cost = (tm, tn, tk), cost
    assert best is not None, (m, n, k)
    return best


def _mm_call(name, grid, operands, in_specs, out_shape, out_spec, contract, nk, scale, has_res, tile, exchange=None):
    n_reg = len(operands)
    n_xin = len(exchange.inputs) if exchange else 0
    n_xout = len(exchange.out_shapes) if exchange else 0

    def body(*refs):
        a_ref, b_ref = refs[0], refs[1]
        res_ref = refs[2] if has_res else None
        o_ref = refs[n_reg + n_xin]
        acc_ref = refs[n_reg + n_xin + 1 + n_xout] if nk > 1 else None
        if exchange:
            x_ins = refs[n_reg:n_reg + n_xin]
            x_outs = refs[n_reg + n_xin + 1:n_reg + n_xin + 1 + n_xout]
            sems = refs[-2:]
            step = [pl.program_id(k) for k in range(3)]
            at_first = jnp.logical_and(jnp.logical_and(step[0] == 0, step[1] == 0), step[2] == 0)
            at_last = jnp.logical_and(jnp.logical_and(step[0] == grid[0] - 1, step[1] == grid[1] - 1),
                                      step[2] == grid[2] - 1)

            @pl.when(at_first)
            def _():
                exchange.start(x_ins, x_outs, sems)

        def finish(v):
            if scale != 1.0:
                v = v * scale
            if has_res:
                v = res_ref[...] + v
            o_ref[...] = v.astype(o_ref.dtype)

        part = lax.dot_general(a_ref[...], b_ref[...], contract, preferred_element_type=F32)
        if nk == 1:
            finish(part)
        else:
            kk = pl.program_id(2)

            @pl.when(kk == 0)
            def _():
                acc_ref[...] = part

            @pl.when(jnp.logical_and(kk > 0, kk < nk - 1))
            def _():
                acc_ref[...] += part

            @pl.when(kk == nk - 1)
            def _():
                finish(acc_ref[...] + part)

        if exchange:
            @pl.when(at_last)
            def _():
                exchange.finish(x_ins, x_outs, sems)

    scratch = [pltpu.VMEM(tile, F32)] if nk > 1 else []
    if not exchange:
        return pl.pallas_call(
            body, name=name, grid=grid, in_specs=in_specs, out_specs=out_spec, out_shape=out_shape,
            scratch_shapes=scratch,
            compiler_params=_params(dimension_semantics=("arbitrary", "arbitrary", "arbitrary")),
        )(*operands)
    scratch += [pltpu.SemaphoreType.DMA((exchange.n_sems,)), pltpu.SemaphoreType.DMA((exchange.n_sems,))]
    got = pl.pallas_call(
        body, name=name, grid=grid, in_specs=in_specs + [ANY] * n_xin, out_specs=[out_spec] + [ANY] * n_xout,
        out_shape=[out_shape] + list(exchange.out_shapes), scratch_shapes=scratch,
        input_output_aliases={n_reg + i: 1 + o for i, o in exchange.aliases.items()},
        compiler_params=_params(dimension_semantics=("arbitrary", "arbitrary", "arbitrary")),
    )(*operands, *exchange.inputs)
    return got[0], list(got[1:])


def mm_nn(name, a, w, kind, out_dtype, res=None, scale=1.0, exchange=None):
    m, k = a.shape
    p, r, c = w.shape
    n = p * c if kind == 'col' else c
    assert k == (r if kind == 'col' else p * r), (name, a.shape, w.shape)
    n_tiles = _divisors(c, LANE, 2816)
    k_tiles = _divisors(r, LANE, 4096)
    tm, tn, tk = _plan_mm(m, n_tiles, k_tiles, n, k, a.dtype.itemsize, jnp.dtype(out_dtype).itemsize, res is not None)
    nk = k // tk
    if kind == 'col':
        cpt = c // tn
        w_spec = pl.BlockSpec((None, tk, tn), lambda j, i, kk: (j // cpt, kk, j % cpt))
    else:
        rpt = r // tk
        w_spec = pl.BlockSpec((None, tk, tn), lambda j, i, kk: (kk // rpt, kk % rpt, j))
    in_specs = [pl.BlockSpec((tm, tk), lambda j, i, kk: (i, kk)), w_spec]
    operands = [a, w]
    if res is not None:
        in_specs.append(pl.BlockSpec((tm, tn), lambda j, i, kk: (i, j)))
        operands.append(res)
    return _mm_call(name, (n // tn, m // tm, nk), operands, in_specs, jax.ShapeDtypeStruct((m, n), out_dtype),
                    pl.BlockSpec((tm, tn), lambda j, i, kk: (i, j)), (((1,), (0,)), ((), ())), nk, scale,
                    res is not None, (tm, tn), exchange)


def mm_nt(name, a, w, kind, out_dtype, scale=1.0, exchange=None):
    m, kc = a.shape
    p, r, c = w.shape
    n = r if kind == 'col' else p * r
    assert kc == (p * c if kind == 'col' else c), (name, a.shape, w.shape)
    n_tiles = _divisors(r, LANE, 2816)
    k_tiles = _divisors(c, LANE, 4096)
    tm, tn, tk = _plan_mm(m, n_tiles, k_tiles, n, kc, a.dtype.itemsize, jnp.dtype(out_dtype).itemsize, False)
    nk = kc // tk
    if kind == 'col':
        cpt = c // tk
        w_spec = pl.BlockSpec((None, tn, tk), lambda j, i, kk: (kk // cpt, j, kk % cpt))
    else:
        rpt = r // tn
        w_spec = pl.BlockSpec((None, tn, tk), lambda j, i, kk: (j // rpt, j % rpt, kk))
    in_specs = [pl.BlockSpec((tm, tk), lambda j, i, kk: (i, kk)), w_spec]
    return _mm_call(name, (n // tn, m // tm, nk), [a, w], in_specs, jax.ShapeDtypeStruct((m, n), out_dtype),
                    pl.BlockSpec((tm, tn), lambda j, i, kk: (i, j)), (((1,), (1,)), ((), ())), nk, scale, False,
                    (tm, tn), exchange)


def _plan_tn(s, ka, nd, r_tiles, n_tiles):
    best, best_cost = None, None
    for ts in _divisors(s, 16, 2048):
        for tr in r_tiles:
            for tn in n_tiles:
                ni, nj, ns = ka // tr, nd // tn, s // ts
                vmem = 2 * (ts * tr * 2 + ts * tn * 2 + tr * tn * 2) + tr * tn * 4 * (2 if ns > 1 else 1)
                if vmem > VMEM_PLAN_BYTES:
                    continue
                traffic = nj * s * ka * 2 + ni * s * nd * 2 + ka * nd * 2
                cost = traffic + ni * nj * ns * STEP_COST_BYTES
                if ns > 1:
                    cost += 0.25 * 2 * tr * tn * 4 * ni * nj * ns
                if best_cost is None or cost < best_cost:
                    best, best_cost = (ts, tr, tn), cost
    assert best is not None, (s, ka, nd)
    return best


def mm_tn(name, a, dy, kind, scale=1.0):
    s, ka = a.shape
    s2, nd = dy.shape
    assert s == s2
    p = N_CHIPS
    r, c = (ka, nd // p) if kind == 'col' else (ka // p, nd)
    ts, tr, tn = _plan_tn(s, ka, nd, _divisors(r, LANE, 2048), _divisors(c, LANE, 2816))
    ns = s // ts
    if kind == 'col':
        cpt = c // tn
        o_spec = pl.BlockSpec((None, tr, tn), lambda j, i, kk: (j // cpt, i, j % cpt))
    else:
        rpt = r // tr
        o_spec = pl.BlockSpec((None, tr, tn), lambda j, i, kk: (i // rpt, i % rpt, j))
    in_specs = [pl.BlockSpec((ts, tr), lambda j, i, kk: (kk, i)), pl.BlockSpec((ts, tn), lambda j, i, kk: (kk, j))]
    return _mm_call(name, (nd // tn, ka // tr, ns), [a, dy], in_specs, jax.ShapeDtypeStruct((p, r, c), BF16), o_spec,
                    (((0,), (0,)), ((), ())), ns, scale, False, (tr, tn))


def _rms_rows(x, g):
    r = lax.rsqrt(jnp.mean(x * x, axis=-1, keepdims=True) + RMS_EPS)
    xhat = x * r
    return xhat, r, xhat * g


def rms_fwd(name, x, g):
    s, d = x.shape
    tm = _row_tile(s, d * 4)

    def body(x_ref, g_ref, o_ref):
        o_ref[...] = _rms_rows(x_ref[...], g_ref[...])[2].astype(BF16)

    return pl.pallas_call(
        body, name=name, grid=(s // tm,),
        in_specs=[pl.BlockSpec((tm, d), lambda i: (i, 0)), pl.BlockSpec((1, d), lambda i: (0, 0))],
        out_specs=pl.BlockSpec((tm, d), lambda i: (i, 0)), out_shape=jax.ShapeDtypeStruct((s, d), BF16),
        compiler_params=_params(dimension_semantics=("arbitrary",)),
    )(x, g.reshape(1, d))


def _rms_bwd_rows(x, g, dh):
    xhat, r, _ = _rms_rows(x, g)
    u = dh * g
    dx = r * (u - xhat * jnp.mean(u * xhat, axis=-1, keepdims=True))
    return dx, jnp.sum(dh * xhat, axis=0, keepdims=True)


def rms_bwd(name, x, g, dh, dres):
    s, d = x.shape
    tm = _row_tile(s, d * 4, 2 * 2**20)
    has_res = dres is not None

    def body(*refs):
        x_ref, g_ref, dh_ref = refs[:3]
        dres_ref = refs[3] if has_res else None
        dx_ref, dxb_ref, dg_ref = refs[-3:]
        dx, dg = _rms_bwd_rows(x_ref[...], g_ref[...], dh_ref[...].astype(F32))
        if has_res:
            dx = dx + dres_ref[...]
        dx_ref[...] = dx
        dxb_ref[...] = dx.astype(BF16)

        @pl.when(pl.program_id(0) == 0)
        def _():
            dg_ref[...] = dg

        @pl.when(pl.program_id(0) > 0)
        def _():
            dg_ref[...] += dg

    row = pl.BlockSpec((tm, d), lambda i: (i, 0))
    vec = pl.BlockSpec((1, d), lambda i: (0, 0))
    return pl.pallas_call(
        body, name=name, grid=(s // tm,),
        in_specs=[row, vec, row] + ([row] if has_res else []),
        out_specs=[row, row, vec],
        out_shape=[jax.ShapeDtypeStruct((s, d), F32), jax.ShapeDtypeStruct((s, d), BF16),
                   jax.ShapeDtypeStruct((1, d), F32)],
        compiler_params=_params(dimension_semantics=("arbitrary",)),
    )(x, g.reshape(1, d), dh, *([dres] if has_res else []))


def loss_head(name, x, g, target):
    s, d = x.shape
    tm = _row_tile(s, d * 4, 2 * 2**20)

    def body(x_ref, g_ref, t_ref, dx_ref, dxb_ref, dg_ref, loss_ref):
        x = x_ref[...]
        gain = g_ref[...]
        y = _rms_rows(x, gain)[2]
        diff = y - t_ref[...]
        dx, dg = _rms_bwd_rows(x, gain, diff * (1.0 / d))
        dx_ref[...] = dx
        dxb_ref[...] = dx.astype(BF16)
        sq = jnp.sum(diff * diff, axis=0, keepdims=True)

        @pl.when(pl.program_id(0) == 0)
        def _():
            dg_ref[...] = dg
            loss_ref[...] = sq

        @pl.when(pl.program_id(0) > 0)
        def _():
            dg_ref[...] += dg
            loss_ref[...] += sq

    row = pl.BlockSpec((tm, d), lambda i: (i, 0))
    vec = pl.BlockSpec((1, d), lambda i: (0, 0))
    return pl.pallas_call(
        body, name=name, grid=(s // tm,), in_specs=[row, vec, row], out_specs=[row, row, vec, vec],
        out_shape=[jax.ShapeDtypeStruct((s, d), F32), jax.ShapeDtypeStruct((s, d), BF16),
                   jax.ShapeDtypeStruct((1, d), F32), jax.ShapeDtypeStruct((1, d), F32)],
        compiler_params=_params(dimension_semantics=("arbitrary",)),
    )(x, g.reshape(1, d), target)


def _sigmoid(x):
    return 1.0 / (1.0 + jnp.exp(-x))


def swiglu_fwd(name, gu):
    s, f2 = gu.shape
    f = f2 // 2
    tf = _divisors(f, LANE, 1408)[-1]
    tm = _row_tile(s, tf * 4, 2 * 2**20)
    nf = f // tf

    def body(g_ref, u_ref, o_ref):
        g = g_ref[...].astype(F32)
        o_ref[...] = (g * _sigmoid(g) * u_ref[...].astype(F32)).astype(BF16)

    return pl.pallas_call(
        body, name=name, grid=(s // tm, nf),
        in_specs=[pl.BlockSpec((tm, tf), lambda i, j: (i, j)), pl.BlockSpec((tm, tf), lambda i, j: (i, j + nf))],
        out_specs=pl.BlockSpec((tm, tf), lambda i, j: (i, j)), out_shape=jax.ShapeDtypeStruct((s, f), BF16),
        compiler_params=_params(dimension_semantics=("arbitrary", "arbitrary")),
    )(gu, gu)


def swiglu_bwd(name, da, gu):
    s, f2 = gu.shape
    f = f2 // 2
    tf = _divisors(f, LANE, 1408)[-1]
    tm = _row_tile(s, tf * 4, 2 * 2**20)
    nf = f // tf

    def body(da_ref, g_ref, u_ref, o_ref):
        g = g_ref[...].astype(F32)
        d = da_ref[...].astype(F32)
        sg = _sigmoid(g)
        is_gate = pl.program_id(1) < nf
        d_gate = d * u_ref[...].astype(F32) * (sg * (1.0 + g * (1.0 - sg)))
        d_up = d * (g * sg)
        o_ref[...] = jnp.where(is_gate, d_gate, d_up).astype(BF16)

    return pl.pallas_call(
        body, name=name, grid=(s // tm, 2 * nf),
        in_specs=[pl.BlockSpec((tm, tf), lambda i, j: (i, j % nf)), pl.BlockSpec((tm, tf), lambda i, j: (i, j % nf)),
                  pl.BlockSpec((tm, tf), lambda i, j: (i, j % nf + nf))],
        out_specs=pl.BlockSpec((tm, tf), lambda i, j: (i, j)), out_shape=jax.ShapeDtypeStruct((s, f2), BF16),
        compiler_params=_params(dimension_semantics=("arbitrary", "arbitrary")),
    )(da, gu, gu)


_INV_SQRT2 = 0.7071067811865476
_INV_SQRT_2PI = 0.3989422804014327


def _gelu(z):
    return 0.5 * z * (1.0 + lax.erf(z * _INV_SQRT2))


def _gelu_grad(z):
    return 0.5 * (1.0 + lax.erf(z * _INV_SQRT2)) + z * (_INV_SQRT_2PI * jnp.exp(-0.5 * z * z))


def _causal_weights(ws_ref, g):
    t = ws_ref.shape[-1]
    keep = lax.broadcasted_iota(jnp.int32, (t, t), 0) >= lax.broadcasted_iota(jnp.int32, (t, t), 1)
    return jnp.where(keep, ws_ref[g], 0.0).astype(BF16), keep


def _gmlp_gate_rows(z_ref, lg_ref, lb_ref, e):
    z = z_ref[...].astype(F32)
    gz = _gelu(z)
    u, v = gz[:, :e], gz[:, e:]
    mu = jnp.mean(v, axis=-1, keepdims=True)
    xc = v - mu
    rs = lax.rsqrt(jnp.mean(xc * xc, axis=-1, keepdims=True) + LN_EPS)
    vhat = xc * rs
    return z, u, vhat, rs, vhat * lg_ref[...] + lb_ref[...]


def gmlp_fwd(name, z, ln_g, ln_b, w_s, bias):
    s, e2 = z.shape
    e = e2 // 2
    eg = e // GMLP_GROUPS

    def body(z_ref, lg_ref, lb_ref, ws_ref, b_ref, o_ref):
        _, u, _, _, vln = _gmlp_gate_rows(z_ref, lg_ref, lb_ref, e)
        vb = vln.astype(BF16)
        for g in range(GMLP_GROUPS):
            cols = slice(g * eg, (g + 1) * eg)
            wm, _ = _causal_weights(ws_ref, g)
            f = jnp.dot(wm, vb[:, cols], preferred_element_type=F32) + b_ref[:, cols]
            o_ref[:, cols] = (u[:, cols] * f).astype(BF16)

    full = lambda shape: pl.BlockSpec(shape, lambda i: (0,) * len(shape))
    return pl.pallas_call(
        body, name=name, grid=(s // CHUNK,),
        in_specs=[pl.BlockSpec((CHUNK, e2), lambda i: (i, 0)), full((1, e)), full((1, e)),
                  full((GMLP_GROUPS, CHUNK, CHUNK)), full((CHUNK, e))],
        out_specs=pl.BlockSpec((CHUNK, e), lambda i: (i, 0)), out_shape=jax.ShapeDtypeStruct((s, e), BF16),
        compiler_params=_params(dimension_semantics=("arbitrary",)),
    )(z, ln_g.reshape(1, e), ln_b.reshape(1, e), w_s, bias)


def gmlp_bwd(name, z, dp, ln_g, ln_b, w_s, bias):
    s, e2 = z.shape
    e = e2 // 2
    eg = e // GMLP_GROUPS
    t = CHUNK

    def body(z_ref, dp_ref, lg_ref, lb_ref, ws_ref, b_ref, dz_ref, dws_ref, dbs_ref, dlg_ref, dlb_ref):
        first = pl.program_id(0) == 0
        zf, u, vhat, rs, vln = _gmlp_gate_rows(z_ref, lg_ref, lb_ref, e)
        vb = vln.astype(BF16)
        dp = dp_ref[...].astype(F32)
        lane = lax.broadcasted_iota(jnp.int32, (t, LANE), 1)
        dbs = jnp.zeros((t, LANE), F32)
        dvln_parts = []
        for g in range(GMLP_GROUPS):
            cols = slice(g * eg, (g + 1) * eg)
            wm, keep = _causal_weights(ws_ref, g)
            f = jnp.dot(wm, vb[:, cols], preferred_element_type=F32) + b_ref[:, cols]
            dz_ref[:, cols] = (dp[:, cols] * f * _gelu_grad(zf[:, cols])).astype(BF16)
            df = dp[:, cols] * u[:, cols]
            dfb = df.astype(BF16)
            dbs = dbs + jnp.where(lane == g, jnp.sum(df, axis=-1, keepdims=True), 0.0)
            dw = lax.dot_general(dfb, vb[:, cols], (((1,), (1,)), ((), ())), preferred_element_type=F32)
            dw = jnp.where(keep, dw, 0.0)

            @pl.when(first)
            def _():
                dws_ref[g] = dw

            @pl.when(jnp.logical_not(first))
            def _():
                dws_ref[g] += dw

            dvln_parts.append(lax.dot_general(wm, dfb, (((0,), (0,)), ((), ())), preferred_element_type=F32))
        dvln = jnp.concatenate(dvln_parts, axis=-1)
        dvhat = dvln * lg_ref[...]
        dv = rs * (dvhat - jnp.mean(dvhat, axis=-1, keepdims=True)
                   - vhat * jnp.mean(dvhat * vhat, axis=-1, keepdims=True))
        dz_ref[:, e:] = (dv * _gelu_grad(zf[:, e:])).astype(BF16)
        dlg = jnp.sum(dvln * vhat, axis=0, keepdims=True)
        dlb = jnp.sum(dvln, axis=0, keepdims=True)

        @pl.when(first)
        def _():
            dbs_ref[...] = dbs
            dlg_ref[...] = dlg
            dlb_ref[...] = dlb

        @pl.when(jnp.logical_not(first))
        def _():
            dbs_ref[...] += dbs
            dlg_ref[...] += dlg
            dlb_ref[...] += dlb

    full = lambda shape: pl.BlockSpec(shape, lambda i: (0,) * len(shape))
    return pl.pallas_call(
        body, name=name, grid=(s // t,),
        in_specs=[pl.BlockSpec((t, e2), lambda i: (i, 0)), pl.BlockSpec((t, e), lambda i: (i, 0)), full((1, e)),
                  full((1, e)), full((GMLP_GROUPS, t, t)), full((t, e))],
        out_specs=[pl.BlockSpec((t, e2), lambda i: (i, 0)), full((GMLP_GROUPS, t, t)), full((t, LANE)), full((1, e)),
                   full((1, e))],
        out_shape=[jax.ShapeDtypeStruct((s, e2), BF16), jax.ShapeDtypeStruct((GMLP_GROUPS, t, t), F32),
                   jax.ShapeDtypeStruct((t, LANE), F32), jax.ShapeDtypeStruct((1, e), F32),
                   jax.ShapeDtypeStruct((1, e), F32)],
        compiler_params=_params(dimension_semantics=("arbitrary",)),
    )(z, dp, ln_g.reshape(1, e), ln_b.reshape(1, e), w_s, bias)


EDGE = 16


def _shift_down(zc, prev, k):
    tm = zc.shape[0]
    row = lax.broadcasted_iota(jnp.int32, (tm, 1), 0)
    out = pltpu.roll(zc, k, 0)
    for j in range(k):
        out = jnp.where(row == j, prev[EDGE - k + j:EDGE - k + j + 1, :], out)
    return out


def _shift_up(dc, nxt, k):
    tm = dc.shape[0]
    row = lax.broadcasted_iota(jnp.int32, (tm, 1), 0)
    out = pltpu.roll(dc, tm - k, 0)
    for j in range(k):
        out = jnp.where(row == tm - k + j, nxt[j:j + 1, :], out)
    return out


def conv_fwd(name, bcv, cw):
    s, d3 = bcv.shape
    d = d3 // 3
    tm = _row_tile(s, d * 4, 2 * 2**20)
    per = tm // EDGE

    def body(b_ref, c_ref, v_ref, cp_ref, vp_ref, w_ref, o_ref):
        i = pl.program_id(0)
        zc = c_ref[...].astype(F32) * v_ref[...].astype(F32)
        prev = jnp.where(i > 0, cp_ref[...].astype(F32) * vp_ref[...].astype(F32), 0.0)
        conv = w_ref[2:3, :] * zc + w_ref[1:2, :] * _shift_down(zc, prev, 1) + w_ref[0:1, :] * _shift_down(zc, prev, 2)
        o_ref[...] = (b_ref[...].astype(F32) * conv).astype(BF16)

    blk = lambda col: pl.BlockSpec((tm, d), lambda i: (i, col))
    edge = lambda col: pl.BlockSpec((EDGE, d), lambda i: (jnp.maximum(i * per - 1, 0), col))
    return pl.pallas_call(
        body, name=name, grid=(s // tm,),
        in_specs=[blk(0), blk(1), blk(2), edge(1), edge(2), pl.BlockSpec((3, d), lambda i: (0, 0))],
        out_specs=pl.BlockSpec((tm, d), lambda i: (i, 0)), out_shape=jax.ShapeDtypeStruct((s, d), BF16),
        compiler_params=_params(dimension_semantics=("arbitrary",)),
    )(bcv, bcv, bcv, bcv, bcv, cw)


def conv_bwd(name, bcv, dq, cw):
    s, d3 = bcv.shape
    d = d3 // 3
    tm = _row_tile(s, d * 4, 2**20)
    per = tm // EDGE
    n_tiles = s // tm
    last_edge = s // EDGE - 1

    def body(b_ref, c_ref, v_ref, cp_ref, vp_ref, bn_ref, dq_ref, dqn_ref, w_ref, o_ref, dw_ref):
        i = pl.program_id(0)
        b = b_ref[...].astype(F32)
        c = c_ref[...].astype(F32)
        v = v_ref[...].astype(F32)
        dq = dq_ref[...].astype(F32)
        zc = c * v
        prev = jnp.where(i > 0, cp_ref[...].astype(F32) * vp_ref[...].astype(F32), 0.0)
        z1 = _shift_down(zc, prev, 1)
        z2 = _shift_down(zc, prev, 2)
        w0, w1, w2 = w_ref[0:1, :], w_ref[1:2, :], w_ref[2:3, :]
        conv = w2 * zc + w1 * z1 + w0 * z2
        dconv = dq * b
        nxt = jnp.where(i < n_tiles - 1, dqn_ref[...].astype(F32) * bn_ref[...].astype(F32), 0.0)
        dz = w2 * dconv + w1 * _shift_up(dconv, nxt, 1) + w0 * _shift_up(dconv, nxt, 2)
        o_ref[:, :d] = (dq * conv).astype(BF16)
        o_ref[:, d:2 * d] = (dz * v).astype(BF16)
        o_ref[:, 2 * d:] = (dz * c).astype(BF16)
        dw = jnp.concatenate([jnp.sum(dconv * z2, axis=0, keepdims=True), jnp.sum(dconv * z1, axis=0, keepdims=True),
                              jnp.sum(dconv * zc, axis=0, keepdims=True), jnp.zeros((5, d), F32)], axis=0)

        @pl.when(i == 0)
        def _():
            dw_ref[...] = dw

        @pl.when(i > 0)
        def _():
            dw_ref[...] += dw

    blk = lambda col: pl.BlockSpec((tm, d), lambda i: (i, col))
    before = lambda col: pl.BlockSpec((EDGE, d), lambda i: (jnp.maximum(i * per - 1, 0), col))
    after = lambda col: pl.BlockSpec((EDGE, d), lambda i: (jnp.minimum((i + 1) * per, last_edge), col))
    return pl.pallas_call(
        body, name=name, grid=(n_tiles,),
        in_specs=[blk(0), blk(1), blk(2), before(1), before(2), after(0), blk(0), after(0),
                  pl.BlockSpec((3, d), lambda i: (0, 0))],
        out_specs=[pl.BlockSpec((tm, d3), lambda i: (i, 0)), pl.BlockSpec((8, d), lambda i: (0, 0))],
        out_shape=[jax.ShapeDtypeStruct((s, d3), BF16), jax.ShapeDtypeStruct((8, d), F32)],
        compiler_params=_params(dimension_semantics=("arbitrary",)),
    )(bcv, bcv, bcv, bcv, bcv, bcv, dq, dq, cw)


def _attn_probs(qh, kh, scale):
    sc = lax.dot_general(qh, kh, (((1,), (1,)), ((), ())), preferred_element_type=F32) * scale
    ex = jnp.exp(sc - jnp.max(sc, axis=-1, keepdims=True))
    return ex / jnp.sum(ex, axis=-1, keepdims=True)


def attn_fwd(name, q, kv):
    s, d = q.shape
    mlen = kv.shape[0]
    dh = d // XATTN_HEADS
    scale = dh ** -0.5
    tm = _row_tile(s, d * 4, 2 * 2**20)

    def body(q_ref, kv_ref, o_ref):
        for h in range(XATTN_HEADS):
            cols = slice(h * dh, (h + 1) * dh)
            p = _attn_probs(q_ref[:, cols], kv_ref[:, cols], scale)
            o_ref[:, cols] = jnp.dot(p.astype(BF16), kv_ref[:, d + h * dh:d + (h + 1) * dh],
                                     preferred_element_type=F32).astype(BF16)

    return pl.pallas_call(
        body, name=name, grid=(s // tm,),
        in_specs=[pl.BlockSpec((tm, d), lambda i: (i, 0)), pl.BlockSpec((mlen, 2 * d), lambda i: (0, 0))],
        out_specs=pl.BlockSpec((tm, d), lambda i: (i, 0)), out_shape=jax.ShapeDtypeStruct((s, d), BF16),
        compiler_params=_params(dimension_semantics=("arbitrary",)),
    )(q, kv)


def attn_bwd(name, q, kv, do):
    s, d = q.shape
    mlen = kv.shape[0]
    dh = d // XATTN_HEADS
    scale = dh ** -0.5
    tm = _row_tile(s, d * 4, 2 * 2**20)

    def body(q_ref, kv_ref, do_ref, dq_ref, dkv_ref):
        first = pl.program_id(0) == 0
        for h in range(XATTN_HEADS):
            cols = slice(h * dh, (h + 1) * dh)
            vcols = slice(d + h * dh, d + (h + 1) * dh)
            qh, kh, vh, doh = q_ref[:, cols], kv_ref[:, cols], kv_ref[:, vcols], do_ref[:, cols]
            p = _attn_probs(qh, kh, scale)
            dp = lax.dot_general(doh, vh, (((1,), (1,)), ((), ())), preferred_element_type=F32)
            ds = (p * (dp - jnp.sum(dp * p, axis=-1, keepdims=True)) * scale).astype(BF16)
            dq_ref[:, cols] = jnp.dot(ds, kh, preferred_element_type=F32).astype(BF16)
            dk = lax.dot_general(ds, qh, (((0,), (0,)), ((), ())), preferred_element_type=F32)
            dv = lax.dot_general(p.astype(BF16), doh, (((0,), (0,)), ((), ())), preferred_element_type=F32)

            @pl.when(first)
            def _():
                dkv_ref[:, cols] = dk
                dkv_ref[:, vcols] = dv

            @pl.when(jnp.logical_not(first))
            def _():
                dkv_ref[:, cols] += dk
                dkv_ref[:, vcols] += dv

    row = pl.BlockSpec((tm, d), lambda i: (i, 0))
    whole = pl.BlockSpec((mlen, 2 * d), lambda i: (0, 0))
    return pl.pallas_call(
        body, name=name, grid=(s // tm,), in_specs=[row, whole, row], out_specs=[row, whole],
        out_shape=[jax.ShapeDtypeStruct((s, d), BF16), jax.ShapeDtypeStruct((mlen, 2 * d), F32)],
        compiler_params=_params(dimension_semantics=("arbitrary",)),
    )(q, kv, do)


def _as_rows(a):
    if a.ndim >= 2 and a.shape[-1] % LANE == 0:
        return a.reshape(-1, a.shape[-1])
    return a.reshape(-1, LANE) if a.size % LANE == 0 else a.reshape(1, -1)


def add_halves(name, dw, other, core):
    p, r, c = dw.shape
    h = r // 2
    th = _row_tile(h, c * 2, 2 * 2**20)

    def body(core_ref, a_ref, b_ref, o_ref):
        o_ref[...] = (a_ref[...].astype(F32) + b_ref[...].astype(F32)).astype(BF16)

    grid_spec = pltpu.PrefetchScalarGridSpec(
        num_scalar_prefetch=1, grid=(p, h // th),
        in_specs=[pl.BlockSpec((None, None, th, c), lambda pi, i, core_ref: (pi, core_ref[0], i, 0)),
                  pl.BlockSpec((None, th, c), lambda pi, i, core_ref: (pi, i, 0))],
        out_specs=pl.BlockSpec((None, th, c), lambda pi, i, core_ref: (pi, i, 0)))
    return pl.pallas_call(
        body, name=name, grid_spec=grid_spec, out_shape=jax.ShapeDtypeStruct((p, h, c), BF16),
        compiler_params=_params(dimension_semantics=("arbitrary", "arbitrary")),
    )(core, dw.reshape(p, 2, h, c), other)


def sum_leading(name, parts):
    n, r, c = parts.shape
    tr = _row_tile(r, c * 4 * 2, 2 * 2**20)

    def body(p_ref, o_ref):
        acc = p_ref[0].astype(F32)
        for k in range(1, n):
            acc = acc + p_ref[k].astype(F32)
        o_ref[...] = acc

    return pl.pallas_call(
        body, name=name, grid=(r // tr,), in_specs=[pl.BlockSpec((n, tr, c), lambda i: (0, i, 0))],
        out_specs=pl.BlockSpec((tr, c), lambda i: (i, 0)), out_shape=jax.ShapeDtypeStruct((r, c), F32),
        compiler_params=_params(dimension_semantics=("arbitrary",)),
    )(parts)


def _adamw_rows(w, g, m, v):
    m = ADAM_B1 * m + (1.0 - ADAM_B1) * g
    v = ADAM_B2 * v + (1.0 - ADAM_B2) * (g * g)
    m_hat = m / (1.0 - ADAM_B1 ** ADAM_STEP)
    v_hat = v / (1.0 - ADAM_B2 ** ADAM_STEP)
    delta = -ADAM_LR * (m_hat / (jnp.sqrt(v_hat) + ADAM_EPS) + ADAM_WD * w)
    return delta, m, v


def adamw_layer(name, w, m, v, g, layer, carried):
    nl, r, c = w.shape
    tr = _row_tile(r, c * 4, 2**20)
    n_carried = 4 if carried is not None else 0

    def body(*refs):
        w_ref, m_ref, v_ref, g_ref = refs[:4]
        go_ref, d_ref, mo_ref, vo_ref = refs[4 + n_carried:]
        g = g_ref[...]
        delta, m_new, v_new = _adamw_rows(w_ref[...], g, m_ref[...], v_ref[...])
        go_ref[...] = g
        d_ref[...] = delta
        mo_ref[...] = m_new
        vo_ref[...] = v_new

    stacked = pl.BlockSpec((None, tr, c), lambda i: (layer, i, 0))
    in_specs = [stacked, stacked, stacked, pl.BlockSpec((tr, c), lambda i: (i, 0))]
    in_specs += [pl.BlockSpec(memory_space=pl.ANY)] * n_carried
    shape = jax.ShapeDtypeStruct((nl, r, c), F32)
    return pl.pallas_call(
        body, name=name, grid=(r // tr,), in_specs=in_specs, out_specs=[stacked] * 4, out_shape=[shape] * 4,
        input_output_aliases={4 + k: k for k in range(n_carried)},
        compiler_params=_params(dimension_semantics=("arbitrary",)),
    )(w, m, v, g, *(carried or ()))


def adamw_flat(name, w, m, v, g):
    r, c = w.shape

    def body(w_ref, m_ref, v_ref, g_ref, d_ref, mo_ref, vo_ref):
        delta, m_new, v_new = _adamw_rows(w_ref[...], g_ref[...], m_ref[...], v_ref[...])
        d_ref[...] = delta
        mo_ref[...] = m_new
        vo_ref[...] = v_new

    shape = jax.ShapeDtypeStruct((r, c), F32)
    return pl.pallas_call(body, name=name, out_shape=[shape] * 3, compiler_params=_params())(w, m, v, g)


def cast_place(name, w, layer, place):
    nl, r, c = w.shape
    tr = _row_tile(r, c * 4, 2 * 2**20)

    def body(x_ref, y_ref, c_ref, w_ref, o_ref):
        o_ref[...] = w_ref[...].astype(BF16)

    grid_spec = pltpu.PrefetchScalarGridSpec(
        num_scalar_prefetch=3, grid=(r // tr,),
        in_specs=[pl.BlockSpec((None, tr, c), lambda i, x_ref, y_ref, c_ref: (layer, i, 0))],
        out_specs=pl.BlockSpec((None, tr, c), lambda i, x_ref, y_ref, c_ref: (2 * x_ref[0] + y_ref[0], i, 0)))
    return pl.pallas_call(
        body, name=name, grid_spec=grid_spec, out_shape=jax.ShapeDtypeStruct((N_CHIPS, r, c), BF16),
        compiler_params=_params(dimension_semantics=("arbitrary",)),
    )(*place, w)


def reduce_sum4(name, own, landed, place):
    p, h, c = own.shape
    tr = _row_tile(h, c * 4, 2**20)

    def body(x_ref, y_ref, c_ref, t_ref, y1_ref, y2_ref, y3_ref, o_ref):
        acc = t_ref[...].astype(F32)
        for part_ref in (y1_ref, y2_ref, y3_ref):
            acc = acc + part_ref[...].astype(F32)
        o_ref[...] = acc

    def panel(fx, fy):
        return pl.BlockSpec((None, tr, c), lambda i, x_ref, y_ref, c_ref: (
            2 * (1 - x_ref[0] if fx else x_ref[0]) + (1 - y_ref[0] if fy else y_ref[0]), i, 0))

    grid_spec = pltpu.PrefetchScalarGridSpec(
        num_scalar_prefetch=3, grid=(h // tr,),
        in_specs=[panel(0, 0), panel(1, 0), panel(0, 1), panel(1, 1)],
        out_specs=pl.BlockSpec((None, tr, c), lambda i, x_ref, y_ref, c_ref: (c_ref[0], i, 0)))
    return pl.pallas_call(
        body, name=name, grid_spec=grid_spec, out_shape=jax.ShapeDtypeStruct((2, h, c), F32),
        compiler_params=_params(dimension_semantics=("arbitrary",)),
    )(*place, own, landed, landed, landed)


def run_exchange(name, exchange):
    n_in, n_out = len(exchange.inputs), len(exchange.out_shapes)

    def body(*refs):
        ins, outs, sems = refs[:n_in], refs[n_in:n_in + n_out], refs[n_in + n_out:]
        exchange.start(ins, outs, sems)
        exchange.finish(ins, outs, sems)

    return pl.pallas_call(
        body, name=name, in_specs=[ANY] * n_in, out_specs=[ANY] * n_out, out_shape=list(exchange.out_shapes),
        scratch_shapes=[pltpu.SemaphoreType.DMA((exchange.n_sems,)), pltpu.SemaphoreType.DMA((exchange.n_sems,))],
        input_output_aliases=dict(exchange.aliases),
    )(*exchange.inputs)


def _row_halves(ref, c):
    h = ref.shape[1] // 2
    return pl.ds(pl.multiple_of(c * h, 16), h), pl.ds(pl.multiple_of((1 - c) * h, 16), h)


def gather_exchange(fulls):
    n = len(fulls)

    def start(ins, outs, sems):
        x, y, c, mine, chips = _place()
        for a in range(n):
            rows = outs[a].at[mine, _row_halves(outs[a], c)[0]]
            for j, chip in enumerate(chips):
                _remote(rows, rows, sems, 6 * a + j, (*chip, c)).start()

    def finish(ins, outs, sems):
        x, y, c, mine, chips = _place()
        sibling = (x, y, 1 - c)
        for a in range(n):
            half = _row_halves(outs[a], c)[0]
            for j, chip in enumerate(chips):
                rows = outs[a].at[2 * chip[0] + chip[1], half]
                _remote(rows, rows, sems, 6 * a + j, sibling).wait_recv()
                _remote(rows, rows, sems, 6 * a + 3 + j, sibling).start()
        for a in range(n):
            half, other = _row_halves(outs[a], c)
            for j, chip in enumerate(chips):
                rows = outs[a].at[2 * chip[0] + chip[1], other]
                _remote(rows, rows, sems, 6 * a + 3 + j, sibling).wait_recv()
            for j, chip in enumerate(chips):
                rows = outs[a].at[mine, half]
                _remote(rows, rows, sems, 6 * a + j, (*chip, c)).wait_send()
                rows = outs[a].at[2 * chip[0] + chip[1], half]
                _remote(rows, rows, sems, 6 * a + 3 + j, sibling).wait_send()

    return Exchange(list(fulls), [jax.ShapeDtypeStruct(f.shape, f.dtype) for f in fulls], {a: a for a in range(n)},
                    6 * n, start, finish)


def swap_exchange(grads):
    n = len(grads)

    def copies(ins, outs, sems):
        x, y, c, _, _ = _place()
        return [_remote(ins[a].at[:, _row_halves(ins[a], c)[1]], outs[a], sems, a, (x, y, 1 - c)) for a in range(n)]

    def start(ins, outs, sems):
        for cp in copies(ins, outs, sems):
            cp.start()

    def finish(ins, outs, sems):
        for cp in copies(ins, outs, sems):
            cp.wait()

    shapes = [jax.ShapeDtypeStruct((g.shape[0], g.shape[1] // 2, g.shape[2]), g.dtype) for g in grads]
    return Exchange(list(grads), shapes, {}, n, start, finish)


def scatter_exchange(parts):
    n = len(parts)

    def sends(ins, outs, sems):
        x, y, c, mine, chips = _place()
        return [_remote(ins[a].at[2 * chip[0] + chip[1]], outs[a].at[mine], sems, 3 * a + j, (*chip, c))
                for a in range(n) for j, chip in enumerate(chips)]

    def start(ins, outs, sems):
        for cp in sends(ins, outs, sems):
            cp.start()

    def finish(ins, outs, sems):
        x, y, c, mine, chips = _place()
        for a in range(n):
            for j, chip in enumerate(chips):
                landing = outs[a].at[2 * chip[0] + chip[1]]
                _remote(landing, landing, sems, 3 * a + j, (*chip, c)).wait_recv()
        for cp in sends(ins, outs, sems):
            cp.wait_send()

    return Exchange(list(parts), [jax.ShapeDtypeStruct(g.shape, g.dtype) for g in parts], {}, 3 * n, start, finish)


def join_exchange(halves):
    n = len(halves)

    def start(ins, outs, sems):
        x, y, c, _, _ = _place()
        for a in range(n):
            _remote(outs[a].at[c], outs[a].at[c], sems, a, (x, y, 1 - c)).start()

    def finish(ins, outs, sems):
        x, y, c, _, _ = _place()
        for a in range(n):
            _remote(outs[a].at[1 - c], outs[a].at[1 - c], sems, a, (x, y, 1 - c)).wait_recv()
        for a in range(n):
            _remote(outs[a].at[c], outs[a].at[c], sems, a, (x, y, 1 - c)).wait_send()

    return Exchange(list(halves), [jax.ShapeDtypeStruct(g.shape, g.dtype) for g in halves], {a: a for a in range(n)},
                    n, start, finish)


def gather_all(name, rows):
    def body(in_ref, out_ref, send_sems, recv_sems, local_sem):
        sems = (send_sems, recv_sems)
        x, y, c, _, _ = _place()
        me = 4 * x + 2 * y + c
        local = pltpu.make_async_copy(in_ref, out_ref.at[me], local_sem)
        local.start()
        peers = [(1 - x if k & 4 else x, 1 - y if k & 2 else y, 1 - c if k & 1 else c) for k in range(1, N_DEV)]
        sent = []
        for k, peer in enumerate(peers):
            cp = _remote(in_ref, out_ref.at[me], sems, k, peer)
            cp.start()
            sent.append(cp)
        for k, peer in enumerate(peers):
            landing = out_ref.at[4 * peer[0] + 2 * peer[1] + peer[2]]
            _remote(landing, landing, sems, k, peer).wait_recv()
        for cp in sent:
            cp.wait_send()
        local.wait()

    return pl.pallas_call(
        body, name=name, in_specs=[ANY], out_specs=ANY,
        out_shape=jax.ShapeDtypeStruct((N_DEV,) + rows.shape, rows.dtype),
        scratch_shapes=[pltpu.SemaphoreType.DMA((N_DEV - 1,)), pltpu.SemaphoreType.DMA((N_DEV - 1,)),
                        pltpu.SemaphoreType.DMA],
    )(rows)


class _Step:
    def __init__(self, p):
        self.p = p
        xi, yi, ci = lax.axis_index("x"), lax.axis_index("y"), lax.axis_index("c")
        self.chip = 2 * xi + yi
        self.place_refs = tuple(v.astype(jnp.int32).reshape(1) for v in (xi, yi, ci))
        self.core_ref = self.place_refs[2]
        self.depth = p['ffn1_norm'].shape[0]
        self.placed, self.weights, self.big_g = {}, {}, {}

    def group(self, tag, l):
        j = l // 2
        mixer = ['gmlp_w_in', 'gmlp_w_out'] if l % 2 == 0 else ['conv_w_in', 'conv_w_out']
        return {"ffn1": (['ffn1_w13', 'ffn1_w2'], l), "mix": (mixer, j),
                "xattn": (['xattn_wq', 'xattn_wkv', 'xattn_wo'], l), "ffn2": (['ffn2_w13', 'ffn2_w2'], l)}[tag]

    def place(self, tag, l):
        names, idx = self.group(tag, l)
        self.placed[(tag, l)] = [cast_place(f"place_{n}{idx}", self.p[n], idx, self.place_refs) for n in names]

    def gather_of(self, keys):
        keys = [k for k in keys if k[1] < self.depth]
        if not keys:
            return None, keys
        return gather_exchange([f for k in keys for f in self.placed[k]]), keys

    def deliver(self, keys, got):
        for k in keys:
            names = self.group(*k)[0]
            self.weights[k] = dict(zip(names, got[:len(names)], strict=True))
            got = got[len(names):]

    def mm_carrying(self, mm, keys, *args, **kw):
        exchange, keys = self.gather_of(keys)
        if exchange is None:
            return mm(*args, **kw)
        out, got = mm(*args, exchange=exchange, **kw)
        self.deliver(keys, got)
        return out

    def reduce_begin(self, tag, dws):
        theirs = run_exchange(tag + "_swap", swap_exchange(dws))
        return [add_halves(f"{tag}_add{i}", dw, t, self.core_ref) for i, (dw, t) in enumerate(zip(dws, theirs, strict=True))]

    def reduce_end(self, tag, keys, parts, landed):
        halves = [reduce_sum4(f"{tag}_sum{i}", t, y, self.place_refs) for i, (t, y) in enumerate(zip(parts, landed, strict=True))]
        joined = run_exchange(tag + "_join", join_exchange(halves))
        for k, g in zip(keys, joined, strict=True):
            self.big_g[k] = g.reshape(-1, g.shape[-1])

    def dh_carrying(self, tag, keys, dws, *args):
        parts = self.reduce_begin(tag, dws)
        dh, landed = mm_nt(tag + "_dh", *args, exchange=scatter_exchange(parts))
        self.reduce_end(tag, keys, parts, landed)
        return dh

    def ffn_fwd(self, tag, l, x, gain, w, carry13, carry2):
        w13, w2 = w[tag + '_w13'], w[tag + '_w2']
        name = f"l{l}_{tag}"
        h = rms_fwd(name + "_norm", x, gain)
        gu = self.mm_carrying(mm_nn, carry13, name + "_w13", h, w13, 'col', BF16)
        act = swiglu_fwd(name + "_act", gu)
        out = self.mm_carrying(mm_nn, carry2, name + "_w2", act, w2, 'row', F32, res=x, scale=0.5)
        return out, (x, h, gu, act)

    def ffn_bwd(self, tag, l, dx, dxb, saved, gain, w):
        w13, w2 = w[tag + '_w13'], w[tag + '_w2']
        name = f"l{l}_{tag}"
        x, h, gu, act = saved
        d_act = mm_nt(name + "_dact", dxb, w2, 'row', BF16, scale=0.5)
        d_w2 = mm_tn(name + "_dw2", act, dxb, 'row', scale=0.5)
        d_gu = swiglu_bwd(name + "_dgu", d_act, gu)
        d_w13 = mm_tn(name + "_dw13", h, d_gu, 'col')
        dh = self.dh_carrying(name, [(tag + '_w13', l), (tag + '_w2', l)], [d_w13, d_w2], d_gu, w13, 'col', F32)
        return rms_bwd(name + "_dnorm", x, gain, dh, dx)


def kernel(x, mem, ffn1_norm, ffn1_w13, ffn1_w2, mix_norm, gmlp_w_in, gmlp_ln_g, gmlp_ln_b, gmlp_w_s, gmlp_b_s, gmlp_w_out, conv_w_in, conv_w, conv_w_out, xattn_norm, mem_norm, xattn_wq, xattn_wkv, xattn_wo, ffn2_norm, ffn2_w13, ffn2_w2, final_norm, loss_target, m_ffn1_norm, m_ffn1_w13, m_ffn1_w2, m_mix_norm, m_gmlp_w_in, m_gmlp_ln_g, m_gmlp_ln_b, m_gmlp_w_s, m_gmlp_b_s, m_gmlp_w_out, m_conv_w_in, m_conv_w, m_conv_w_out, m_xattn_norm, m_mem_norm, m_xattn_wq, m_xattn_wkv, m_xattn_wo, m_ffn2_norm, m_ffn2_w13, m_ffn2_w2, m_final_norm, v_ffn1_norm, v_ffn1_w13, v_ffn1_w2, v_mix_norm, v_gmlp_w_in, v_gmlp_ln_g, v_gmlp_ln_b, v_gmlp_w_s, v_gmlp_b_s, v_gmlp_w_out, v_conv_w_in, v_conv_w, v_conv_w_out, v_xattn_norm, v_mem_norm, v_xattn_wq, v_xattn_wkv, v_xattn_wo, v_ffn2_norm, v_ffn2_w13, v_ffn2_w2, v_final_norm):
    return _step(dict(locals()))


def _step(p):
    assert sorted(p) == sorted(ARG_NAMES)
    st = _Step(p)
    x = p['x'][0]
    mem = p['mem'][0]
    target = p['loss_target'][0]
    s, d = x.shape
    depth = st.depth

    for l in range(depth):
        for tag in ("ffn1", "mix", "xattn", "ffn2"):
            st.place(tag, l)
    exchange, keys = st.gather_of([("ffn1", 0)])
    st.deliver(keys, run_exchange("gather_first", exchange))

    cw_local = p['conv_w']
    n_conv, cwid, dq4 = cw_local.shape
    cw_rows = jnp.pad(cw_local.reshape(-1, LANE), ((0, (-cw_local.size // LANE) % 8), (0, 0)))
    cw_all = gather_all("gather_conv_w", cw_rows)[0::2, :cw_local.size // LANE]
    conv_w_full = cw_all.reshape(N_CHIPS, n_conv, cwid, dq4).transpose(1, 2, 0, 3).reshape(n_conv, cwid, N_CHIPS * dq4)

    saved = []
    for l in range(depth):
        j = l // 2
        rec = {}
        x, rec['ffn1'] = st.ffn_fwd("ffn1", l, x, p['ffn1_norm'][l], st.weights[("ffn1", l)],
                                    [("mix", l), ("xattn", l)], [("ffn2", l)])
        gm = st.weights[("mix", l)]
        h = rms_fwd(f"l{l}_mix_norm", x, p['mix_norm'][l])
        if l % 2 == 0:
            e = p['gmlp_ln_g'].shape[-1]
            bias = jnp.repeat(p['gmlp_b_s'][j].T, e // GMLP_GROUPS, axis=1)
            z = mm_nn(f"l{l}_gmlp_in", h, gm['gmlp_w_in'], 'col', BF16)
            gate = gmlp_fwd(f"l{l}_gmlp_gate", z, p['gmlp_ln_g'][j], p['gmlp_ln_b'][j], p['gmlp_w_s'][j], bias)
            x_new = mm_nn(f"l{l}_gmlp_out", gate, gm['gmlp_w_out'], 'row', F32, res=x)
            rec['mix'] = (x, h, z, gate, bias)
        else:
            bcv = mm_nn(f"l{l}_conv_in", h, gm['conv_w_in'], 'col', BF16)
            gate = conv_fwd(f"l{l}_conv_gate", bcv, conv_w_full[j])
            x_new = mm_nn(f"l{l}_conv_out", gate, gm['conv_w_out'], 'row', F32, res=x)
            rec['mix'] = (x, h, bcv, gate)
        x = x_new
        ga = st.weights[("xattn", l)]
        hq = rms_fwd(f"l{l}_xattn_norm", x, p['xattn_norm'][l])
        q = mm_nn(f"l{l}_xattn_q", hq, ga['xattn_wq'], 'row', BF16)
        mem_n = rms_fwd(f"l{l}_mem_norm", mem, p['mem_norm'][l])
        kv = mm_nn(f"l{l}_xattn_kv", mem_n, ga['xattn_wkv'], 'col', BF16)
        o = attn_fwd(f"l{l}_xattn_core", q, kv)
        x_new = mm_nn(f"l{l}_xattn_o", o, ga['xattn_wo'], 'row', F32, res=x)
        rec['xattn'] = (x, hq, q, mem_n, kv, o)
        x = x_new
        x, rec['ffn2'] = st.ffn_fwd("ffn2", l, x, p['ffn2_norm'][l], st.weights[("ffn2", l)], [("ffn1", l + 1)], [])
        saved.append(rec)

    dx, dxb, d_final, loss_lanes = loss_head("loss_head", x, p['final_norm'], target)
    loss = lax.psum(0.5 * jnp.sum(loss_lanes) / d, ("x", "y", "c"))

    small = {n: [None] * p[n].shape[0] for n in ('ffn1_norm', 'mix_norm', 'xattn_norm', 'mem_norm', 'ffn2_norm',
                                                  'gmlp_ln_g', 'gmlp_ln_b', 'gmlp_w_s', 'gmlp_b_s', 'conv_w')}
    for l in reversed(range(depth)):
        j = l // 2
        rec = saved[l]
        dx, dxb, small['ffn2_norm'][l] = st.ffn_bwd("ffn2", l, dx, dxb, rec['ffn2'], p['ffn2_norm'][l],
                                                    st.weights[("ffn2", l)])

        ga = st.weights[("xattn", l)]
        x_in, hq, q, mem_n, kv, o = rec['xattn']
        name = f"l{l}_xattn"
        do = mm_nt(name + "_do", dxb, ga['xattn_wo'], 'row', BF16)
        d_wo = mm_tn(name + "_dwo", o, dxb, 'row')
        dq, dkv = attn_bwd(name + "_dcore", q, kv, do)
        d_wq = mm_tn(name + "_dwq", hq, dq, 'row')
        dkvb = dkv.astype(BF16)
        d_wkv = mm_tn(name + "_dwkv", mem_n, dkvb, 'col')
        dh = st.dh_carrying(name, [('xattn_wq', l), ('xattn_wkv', l), ('xattn_wo', l)], [d_wq, d_wkv, d_wo],
                            dq, ga['xattn_wq'], 'row', F32)
        dx, dxb, small['xattn_norm'][l] = rms_bwd(name + "_dnorm", x_in, p['xattn_norm'][l], dh, dx)
        dmem_n = mm_nt(name + "_dmem", dkvb, ga['xattn_wkv'], 'col', F32)
        small['mem_norm'][l] = rms_bwd(f"l{l}_mem_dnorm", mem, p['mem_norm'][l], dmem_n, None)[2]

        gm = st.weights[("mix", l)]
        if l % 2 == 0:
            x_in, h, z, gate, bias = rec['mix']
            name = f"l{l}_gmlp"
            dgate = mm_nt(name + "_dgate", dxb, gm['gmlp_w_out'], 'row', BF16)
            d_wout = mm_tn(name + "_dwout", gate, dxb, 'row')
            dz, dws, dbs, dlg, dlb = gmlp_bwd(name + "_dgate_core", z, dgate, p['gmlp_ln_g'][j], p['gmlp_ln_b'][j],
                                              p['gmlp_w_s'][j], bias)
            small['gmlp_w_s'][j], small['gmlp_b_s'][j] = dws, dbs[:, :GMLP_GROUPS].T
            small['gmlp_ln_g'][j], small['gmlp_ln_b'][j] = dlg, dlb
            d_win = mm_tn(name + "_dwin", h, dz, 'col')
            dh = st.dh_carrying(name, [('gmlp_w_in', j), ('gmlp_w_out', j)], [d_win, d_wout], dz, gm['gmlp_w_in'],
                                'col', F32)
        else:
            x_in, h, bcv, gate = rec['mix']
            name = f"l{l}_conv"
            dgate = mm_nt(name + "_dgate", dxb, gm['conv_w_out'], 'row', BF16)
            d_wout = mm_tn(name + "_dwout", gate, dxb, 'row')
            dbcv, dcw = conv_bwd(name + "_dgate_core", bcv, dgate, conv_w_full[j])
            small['conv_w'][j] = dcw[:cwid]
            d_win = mm_tn(name + "_dwin", h, dbcv, 'col')
            dh = st.dh_carrying(name, [('conv_w_in', j), ('conv_w_out', j)], [d_win, d_wout], dbcv, gm['conv_w_in'],
                                'col', F32)
        dx, dxb, small['mix_norm'][l] = rms_bwd(f"l{l}_mix_dnorm", x_in, p['mix_norm'][l], dh, dx)

        dx, dxb, small['ffn1_norm'][l] = st.ffn_bwd("ffn1", l, dx, dxb, rec['ffn1'], p['ffn1_norm'][l],
                                                    st.weights[("ffn1", l)])

    small_names = ['ffn1_norm', 'mix_norm', 'xattn_norm', 'mem_norm', 'ffn2_norm', 'gmlp_ln_g', 'gmlp_ln_b', 'gmlp_w_s',
                   'gmlp_b_s', 'final_norm', 'conv_w']
    small_full = {n: jnp.stack([g.reshape(p[n].shape[1:]) for g in small[n]]) for n in small_names
                  if n not in ('final_norm', 'conv_w')}
    small_full['final_norm'] = d_final.reshape(p['final_norm'].shape)
    small_full['conv_w'] = jnp.stack(small['conv_w'])
    packed = jnp.concatenate([small_full[n].reshape(-1, LANE) for n in small_names], axis=0)
    total = sum_leading("small_sum", gather_all("small_gather", packed))
    small_g, at = {}, 0
    for n in small_names:
        rows = small_full[n].size // LANE
        small_g[n] = total[at:at + rows].reshape(small_full[n].shape)
        at += rows
    small_g['conv_w'] = lax.dynamic_slice_in_dim(small_g['conv_w'], st.chip * dq4, dq4, axis=2)

    grads, deltas, new_m, new_v = {}, {}, {}, {}
    for n in WEIGHTS:
        w, m, v = p[n], p['m_' + n], p['v_' + n]
        if n in BIG:
            carried = None
            for i in range(w.shape[0]):
                carried = adamw_layer(f"adamw_{n}{i}", w, m, v, st.big_g[(n, i)], i, carried)
            grads[n], deltas[n], new_m[n], new_v[n] = carried
        else:
            g = small_g[n]
            out = adamw_flat(f"adamw_{n}", _as_rows(w), _as_rows(m), _as_rows(v), _as_rows(g))
            grads[n] = g
            deltas[n], new_m[n], new_v[n] = (o.reshape(w.shape) for o in out)

    grad_x = dx.reshape(p['x'].shape)
    return (loss, grad_x, *[grads[n] for n in WEIGHTS], *[deltas[n] for n in WEIGHTS], *[new_m[n] for n in WEIGHTS],
            *[new_v[n] for n in WEIGHTS])
```

```python
from typing import Callable, NamedTuple

import jax
import jax.numpy as jnp
from jax import lax
from jax.experimental import pallas as pl
from jax.experimental.pallas import tpu as pltpu

F32 = jnp.float32
BF16 = jnp.bfloat16
MESH = pl.DeviceIdType.MESH

CHUNK = 128
GMLP_GROUPS = 8
XATTN_HEADS = 4
RMS_EPS = 1e-6
LN_EPS = 1e-5
ADAM_LR = 0.001
ADAM_B1 = 0.9
ADAM_B2 = 0.999
ADAM_EPS = 1e-08
ADAM_WD = 0.01
ADAM_STEP = 10

N_CHIPS = 4
N_DEV = 8

VMEM_LIMIT_BYTES = 58 * 2**20
VMEM_PLAN_BYTES = 46 * 2**20
LANE = 128
MXU_DIM = 256
MXU_FLOPS_PER_US = 996e6
HBM_BYTES_PER_US = 3.3e6
STEP_US = 0.35
ACC_US_PER_VREG = 0.58e-3

WEIGHTS = ['ffn1_norm', 'ffn1_w13', 'ffn1_w2', 'mix_norm', 'gmlp_w_in', 'gmlp_ln_g', 'gmlp_ln_b', 'gmlp_w_s',
           'gmlp_b_s', 'gmlp_w_out', 'conv_w_in', 'conv_w', 'conv_w_out', 'xattn_norm', 'mem_norm', 'xattn_wq',
           'xattn_wkv', 'xattn_wo', 'ffn2_norm', 'ffn2_w13', 'ffn2_w2', 'final_norm']
BIG = {'ffn1_w13': 'col', 'ffn1_w2': 'row', 'gmlp_w_in': 'col', 'gmlp_w_out': 'row', 'conv_w_in': 'col',
       'conv_w_out': 'row', 'xattn_wq': 'row', 'xattn_wkv': 'col', 'xattn_wo': 'row', 'ffn2_w13': 'col',
       'ffn2_w2': 'row'}
ARG_NAMES = (['x', 'mem'] + WEIGHTS + ['loss_target'] + ['m_' + n for n in WEIGHTS] + ['v_' + n for n in WEIGHTS])


def _params(**kw):
    return pltpu.CompilerParams(vmem_limit_bytes=VMEM_LIMIT_BYTES, **kw)


def _divisors(n, mult, cap):
    return [d for d in range(mult, min(n, cap) + 1, mult) if n % d == 0] or [n]


def _row_tile(rows, width_bytes, budget=4 * 2**20):
    best = None
    for d in _divisors(rows, 16, 1024):
        if d * width_bytes <= budget:
            best = d
    return best or _divisors(rows, 16, 1024)[0]


ANY = pl.BlockSpec(memory_space=pl.ANY)


class Exchange(NamedTuple):
    inputs: list
    out_shapes: list
    aliases: dict
    n_sems: int
    start: Callable
    finish: Callable


def _place():
    x, y, c = lax.axis_index("x"), lax.axis_index("y"), lax.axis_index("c")
    chips = [(1 - x, y), (x, 1 - y), (1 - x, 1 - y)]
    return x, y, c, 2 * x + y, chips


def _remote(src, dst, sems, k, device):
    return pltpu.make_async_remote_copy(src_ref=src, dst_ref=dst, send_sem=sems[0].at[k], recv_sem=sems[1].at[k],
                                        device_id=device, device_id_type=MESH)


def _mxu_fill(dim):
    return dim / (-(-dim // MXU_DIM) * MXU_DIM)


def _tile_time(flops, fill, traffic, steps, acc_vregs):
    return (max(flops / (MXU_FLOPS_PER_US * fill), traffic / HBM_BYTES_PER_US) + steps * STEP_US
            + steps * acc_vregs * ACC_US_PER_VREG)


def _plan_mm(m, n_tiles_of, k_tiles_of, n, k, a_item, o_item, has_res):
    best, best_cost = None, None
    for tm in _divisors(m, 16, 1024):
        for tn in n_tiles_of:
            for tk in k_tiles_of:
                ni, nj, nk = m // tm, n // tn, k // tk
                blocks = tm * tk * a_item + tk * tn * 2 + tm * tn * o_item + (tm * tn * 4 if has_res else 0)
                vmem = 2 * blocks + tm * tn * 4 * (2 if nk > 1 else 1)
                if vmem > VMEM_PLAN_BYTES:
                    continue
                traffic = nj * m * k * a_item + (k * n * 2 if nk == 1 else ni * k * n * 2)
                traffic += m * n * (o_item + (4 if has_res else 0))
                cost = _tile_time(2 * m * n * k, _mxu_fill(tk) * _mxu_fill(tn), traffic, ni * nj * nk,
                                  tm * tn // 1024 if nk > 1 else 0)
                if best_cost is None or cost < best_cost:
                    best, best_cost = (tm, tn, tk), cost
    assert best is not None, (m, n, k)
    return best


def _mm_call(name, grid, operands, in_specs, out_shape, out_spec, contract, nk, scale, has_res, tile, exchange=None):
    n_reg = len(operands)
    n_xin = len(exchange.inputs) if exchange else 0
    n_xout = len(exchange.out_shapes) if exchange else 0

    def body(*refs):
        a_ref, b_ref = refs[0], refs[1]
        res_ref = refs[2] if has_res else None
        o_ref = refs[n_reg + n_xin]
        acc_ref = refs[n_reg + n_xin + 1 + n_xout] if nk > 1 else None
        if exchange:
            x_ins = refs[n_reg:n_reg + n_xin]
            x_outs = refs[n_reg + n_xin + 1:n_reg + n_xin + 1 + n_xout]
            sems = refs[-2:]
            step = [pl.program_id(k) for k in range(3)]
            at_first = jnp.logical_and(jnp.logical_and(step[0] == 0, step[1] == 0), step[2] == 0)
            at_last = jnp.logical_and(jnp.logical_and(step[0] == grid[0] - 1, step[1] == grid[1] - 1),
                                      step[2] == grid[2] - 1)

            @pl.when(at_first)
            def _():
                exchange.start(x_ins, x_outs, sems)

        def finish(v):
            if scale != 1.0:
                v = v * scale
            if has_res:
                v = res_ref[...] + v
            o_ref[...] = v.astype(o_ref.dtype)

        b = b_ref[...]
        if b.ndim == 3:
            b = b.reshape(b.shape[0] * b.shape[1], b.shape[2])
        part = lax.dot_general(a_ref[...], b, contract, preferred_element_type=F32)
        if nk == 1:
            finish(part)
        else:
            kk = pl.program_id(2)

            @pl.when(kk == 0)
            def _():
                acc_ref[...] = part

            @pl.when(jnp.logical_and(kk > 0, kk < nk - 1))
            def _():
                acc_ref[...] += part

            @pl.when(kk == nk - 1)
            def _():
                finish(acc_ref[...] + part)

        if exchange:
            @pl.when(at_last)
            def _():
                exchange.finish(x_ins, x_outs, sems)

    scratch = [pltpu.VMEM(tile, F32)] if nk > 1 else []
    if not exchange:
        return pl.pallas_call(
            body, name=name, grid=grid, in_specs=in_specs, out_specs=out_spec, out_shape=out_shape,
            scratch_shapes=scratch,
            compiler_params=_params(dimension_semantics=("arbitrary", "arbitrary", "arbitrary")),
        )(*operands)
    scratch += [pltpu.SemaphoreType.DMA((exchange.n_sems,)), pltpu.SemaphoreType.DMA((exchange.n_sems,))]
    got = pl.pallas_call(
        body, name=name, grid=grid, in_specs=in_specs + [ANY] * n_xin, out_specs=[out_spec] + [ANY] * n_xout,
        out_shape=[out_shape] + list(exchange.out_shapes), scratch_shapes=scratch,
        input_output_aliases={n_reg + i: 1 + o for i, o in exchange.aliases.items()},
        compiler_params=_params(dimension_semantics=("arbitrary", "arbitrary", "arbitrary")),
    )(*operands, *exchange.inputs)
    return got[0], list(got[1:])


def mm_nn(name, a, w, kind, out_dtype, res=None, scale=1.0, exchange=None):
    m, k = a.shape
    p, r, c = w.shape
    n = p * c if kind == 'col' else c
    assert k == (r if kind == 'col' else p * r), (name, a.shape, w.shape)
    n_tiles = _divisors(c, LANE, 2816)
    k_tiles = _divisors(r, LANE, 4096)
    if kind == 'row':
        k_tiles = k_tiles + [q * r for q in (2, 4) if p % q == 0]
    tm, tn, tk = _plan_mm(m, n_tiles, k_tiles, n, k, a.dtype.itemsize, jnp.dtype(out_dtype).itemsize, res is not None)
    nk = k // tk
    if kind == 'col':
        cpt = c // tn
        w_spec = pl.BlockSpec((None, tk, tn), lambda j, i, kk: (j // cpt, kk, j % cpt))
    elif tk > r:
        w_spec = pl.BlockSpec((tk // r, r, tn), lambda j, i, kk: (kk, 0, j))
    else:
        rpt = r // tk
        w_spec = pl.BlockSpec((None, tk, tn), lambda j, i, kk: (kk // rpt, kk % rpt, j))
    in_specs = [pl.BlockSpec((tm, tk), lambda j, i, kk: (i, kk)), w_spec]
    operands = [a, w]
    if res is not None:
        in_specs.append(pl.BlockSpec((tm, tn), lambda j, i, kk: (i, j)))
        operands.append(res)
    return _mm_call(name, (n // tn, m // tm, nk), operands, in_specs, jax.ShapeDtypeStruct((m, n), out_dtype),
                    pl.BlockSpec((tm, tn), lambda j, i, kk: (i, j)), (((1,), (0,)), ((), ())), nk, scale,
                    res is not None, (tm, tn), exchange)


def mm_nt(name, a, w, kind, out_dtype, scale=1.0, exchange=None):
    m, kc = a.shape
    p, r, c = w.shape
    n = r if kind == 'col' else p * r
    assert kc == (p * c if kind == 'col' else c), (name, a.shape, w.shape)
    n_tiles = _divisors(r, LANE, 2816)
    k_tiles = _divisors(c, LANE, 4096)
    if kind == 'row':
        n_tiles = n_tiles + [q * r for q in (2, 4) if p % q == 0 and q * r <= 2816]
    tm, tn, tk = _plan_mm(m, n_tiles, k_tiles, n, kc, a.dtype.itemsize, jnp.dtype(out_dtype).itemsize, False)
    nk = kc // tk
    if kind == 'col':
        cpt = c // tk
        w_spec = pl.BlockSpec((None, tn, tk), lambda j, i, kk: (kk // cpt, j, kk % cpt))
    elif tn > r:
        w_spec = pl.BlockSpec((tn // r, r, tk), lambda j, i, kk: (j, 0, kk))
    else:
        rpt = r // tn
        w_spec = pl.BlockSpec((None, tn, tk), lambda j, i, kk: (j // rpt, j % rpt, kk))
    in_specs = [pl.BlockSpec((tm, tk), lambda j, i, kk: (i, kk)), w_spec]
    return _mm_call(name, (n // tn, m // tm, nk), [a, w], in_specs, jax.ShapeDtypeStruct((m, n), out_dtype),
                    pl.BlockSpec((tm, tn), lambda j, i, kk: (i, j)), (((1,), (1,)), ((), ())), nk, scale, False,
                    (tm, tn), exchange)


def _plan_tn(s, ka, nd, r_tiles, n_tiles):
    best, best_cost = None, None
    for ts in _divisors(s, 16, 2048):
        for tr in r_tiles:
            for tn in n_tiles:
                ni, nj, ns = ka // tr, nd // tn, s // ts
                vmem = 2 * (ts * tr * 2 + ts * tn * 2 + tr * tn * 2) + tr * tn * 4 * (2 if ns > 1 else 1)
                if vmem > VMEM_PLAN_BYTES:
                    continue
                traffic = nj * s * ka * 2 + ni * s * nd * 2 + ka * nd * 2
                cost = _tile_time(2 * s * ka * nd, _mxu_fill(ts) * _mxu_fill(tn), traffic, ni * nj * ns,
                                  tr * tn // 1024 if ns > 1 else 0)
                if best_cost is None or cost < best_cost:
                    best, best_cost = (ts, tr, tn), cost
    assert best is not None, (s, ka, nd)
    return best


def mm_tn(name, a, dy, kind, scale=1.0, exchange=None):
    s, ka = a.shape
    s2, nd = dy.shape
    assert s == s2
    p = N_CHIPS
    r, c = (ka, nd // p) if kind == 'col' else (ka // p, nd)
    ts, tr, tn = _plan_tn(s, ka, nd, _divisors(r, LANE, 2048), _divisors(c, LANE, 2816))
    ns = s // ts
    if kind == 'col':
        cpt = c // tn
        o_spec = pl.BlockSpec((None, tr, tn), lambda j, i, kk: (j // cpt, i, j % cpt))
    else:
        rpt = r // tr
        o_spec = pl.BlockSpec((None, tr, tn), lambda j, i, kk: (i // rpt, i % rpt, j))
    in_specs = [pl.BlockSpec((ts, tr), lambda j, i, kk: (kk, i)), pl.BlockSpec((ts, tn), lambda j, i, kk: (kk, j))]
    return _mm_call(name, (nd // tn, ka // tr, ns), [a, dy], in_specs, jax.ShapeDtypeStruct((p, r, c), BF16), o_spec,
                    (((0,), (0,)), ((), ())), ns, scale, False, (tr, tn), exchange)


def _rms_rows(x, g):
    r = lax.rsqrt(jnp.mean(x * x, axis=-1, keepdims=True) + RMS_EPS)
    xhat = x * r
    return xhat, r, xhat * g


def rms_fwd(name, x, g):
    s, d = x.shape
    tm = _row_tile(s, d * 4)

    def body(x_ref, g_ref, o_ref):
        o_ref[...] = _rms_rows(x_ref[...], g_ref[...])[2].astype(BF16)

    return pl.pallas_call(
        body, name=name, grid=(s // tm,),
        in_specs=[pl.BlockSpec((tm, d), lambda i: (i, 0)), pl.BlockSpec((1, d), lambda i: (0, 0))],
        out_specs=pl.BlockSpec((tm, d), lambda i: (i, 0)), out_shape=jax.ShapeDtypeStruct((s, d), BF16),
        compiler_params=_params(dimension_semantics=("arbitrary",)),
    )(x, g.reshape(1, d))


def _rms_bwd_rows(x, g, dh):
    xhat, r, _ = _rms_rows(x, g)
    u = dh * g
    dx = r * (u - xhat * jnp.mean(u * xhat, axis=-1, keepdims=True))
    return dx, jnp.sum(dh * xhat, axis=0, keepdims=True)


def rms_bwd(name, x, g, dh, dres):
    s, d = x.shape
    tm = _row_tile(s, d * 4, 2 * 2**20)
    has_res = dres is not None

    def body(*refs):
        x_ref, g_ref, dh_ref = refs[:3]
        dres_ref = refs[3] if has_res else None
        dx_ref, dxb_ref, dg_ref = refs[-3:]
        dx, dg = _rms_bwd_rows(x_ref[...], g_ref[...], dh_ref[...].astype(F32))
        if has_res:
            dx = dx + dres_ref[...]
        dx_ref[...] = dx
        dxb_ref[...] = dx.astype(BF16)

        @pl.when(pl.program_id(0) == 0)
        def _():
            dg_ref[...] = dg

        @pl.when(pl.program_id(0) > 0)
        def _():
            dg_ref[...] += dg

    row = pl.BlockSpec((tm, d), lambda i: (i, 0))
    vec = pl.BlockSpec((1, d), lambda i: (0, 0))
    return pl.pallas_call(
        body, name=name, grid=(s // tm,),
        in_specs=[row, vec, row] + ([row] if has_res else []),
        out_specs=[row, row, vec],
        out_shape=[jax.ShapeDtypeStruct((s, d), F32), jax.ShapeDtypeStruct((s, d), BF16),
                   jax.ShapeDtypeStruct((1, d), F32)],
        compiler_params=_params(dimension_semantics=("arbitrary",)),
    )(x, g.reshape(1, d), dh, *([dres] if has_res else []))


def loss_head(name, x, g, target):
    s, d = x.shape
    tm = _row_tile(s, d * 4, 2 * 2**20)

    def body(x_ref, g_ref, t_ref, dx_ref, dxb_ref, dg_ref, loss_ref):
        x = x_ref[...]
        gain = g_ref[...]
        y = _rms_rows(x, gain)[2]
        diff = y - t_ref[...]
        dx, dg = _rms_bwd_rows(x, gain, diff * (1.0 / d))
        dx_ref[...] = dx
        dxb_ref[...] = dx.astype(BF16)
        sq = jnp.sum(diff * diff, axis=0, keepdims=True)

        @pl.when(pl.program_id(0) == 0)
        def _():
            dg_ref[...] = dg
            loss_ref[...] = sq

        @pl.when(pl.program_id(0) > 0)
        def _():
            dg_ref[...] += dg
            loss_ref[...] += sq

    row = pl.BlockSpec((tm, d), lambda i: (i, 0))
    vec = pl.BlockSpec((1, d), lambda i: (0, 0))
    return pl.pallas_call(
        body, name=name, grid=(s // tm,), in_specs=[row, vec, row], out_specs=[row, row, vec, vec],
        out_shape=[jax.ShapeDtypeStruct((s, d), F32), jax.ShapeDtypeStruct((s, d), BF16),
                   jax.ShapeDtypeStruct((1, d), F32), jax.ShapeDtypeStruct((1, d), F32)],
        compiler_params=_params(dimension_semantics=("arbitrary",)),
    )(x, g.reshape(1, d), target)


def _sigmoid(x):
    return 0.5 * jnp.tanh(0.5 * x) + 0.5


def swiglu_fwd(name, gu):
    s, f2 = gu.shape
    f = f2 // 2
    tf = _divisors(f, LANE, 1408)[-1]
    tm = _row_tile(s, tf * 4, 2 * 2**20)
    nf = f // tf

    def body(g_ref, u_ref, o_ref):
        g = g_ref[...].astype(F32)
        o_ref[...] = (g * _sigmoid(g) * u_ref[...].astype(F32)).astype(BF16)

    return pl.pallas_call(
        body, name=name, grid=(s // tm, nf),
        in_specs=[pl.BlockSpec((tm, tf), lambda i, j: (i, j)), pl.BlockSpec((tm, tf), lambda i, j: (i, j + nf))],
        out_specs=pl.BlockSpec((tm, tf), lambda i, j: (i, j)), out_shape=jax.ShapeDtypeStruct((s, f), BF16),
        compiler_params=_params(dimension_semantics=("arbitrary", "arbitrary")),
    )(gu, gu)


def swiglu_bwd(name, da, gu):
    s, f2 = gu.shape
    f = f2 // 2
    tf = _divisors(f, LANE, 1408)[-1]
    tm = _row_tile(s, tf * 4, 2 * 2**20)
    nf = f // tf

    def body(da_ref, g_ref, u_ref, o_ref):
        g = g_ref[...].astype(F32)
        d = da_ref[...].astype(F32)
        sg = _sigmoid(g)
        is_gate = pl.program_id(1) < nf

        @pl.when(is_gate)
        def _():
            o_ref[...] = (d * u_ref[...].astype(F32) * (sg * (1.0 + g * (1.0 - sg)))).astype(BF16)

        @pl.when(jnp.logical_not(is_gate))
        def _():
            o_ref[...] = (d * (g * sg)).astype(BF16)

    return pl.pallas_call(
        body, name=name, grid=(s // tm, 2 * nf),
        in_specs=[pl.BlockSpec((tm, tf), lambda i, j: (i, j % nf)), pl.BlockSpec((tm, tf), lambda i, j: (i, j % nf)),
                  pl.BlockSpec((tm, tf), lambda i, j: (i, j % nf + nf))],
        out_specs=pl.BlockSpec((tm, tf), lambda i, j: (i, j)), out_shape=jax.ShapeDtypeStruct((s, f2), BF16),
        compiler_params=_params(dimension_semantics=("arbitrary", "arbitrary")),
    )(da, gu, gu)


_INV_SQRT2 = 0.7071067811865476
_INV_SQRT_2PI = 0.3989422804014327


def _gelu(z):
    return 0.5 * z * (1.0 + lax.erf(z * _INV_SQRT2))


def _gelu_grad(z):
    return 0.5 * (1.0 + lax.erf(z * _INV_SQRT2)) + z * (_INV_SQRT_2PI * jnp.exp(-0.5 * z * z))


def _causal_weights(ws_ref, g):
    t = ws_ref.shape[-1]
    keep = lax.broadcasted_iota(jnp.int32, (t, t), 0) >= lax.broadcasted_iota(jnp.int32, (t, t), 1)
    return jnp.where(keep, ws_ref[g], 0.0).astype(BF16), keep


def _gmlp_gate_rows(z_ref, lg_ref, lb_ref, e):
    z = z_ref[...].astype(F32)
    gz = _gelu(z)
    u, v = gz[:, :e], gz[:, e:]
    mu = jnp.mean(v, axis=-1, keepdims=True)
    xc = v - mu
    rs = lax.rsqrt(jnp.mean(xc * xc, axis=-1, keepdims=True) + LN_EPS)
    vhat = xc * rs
    return z, u, vhat, rs, vhat * lg_ref[...] + lb_ref[...]


def gmlp_fwd(name, z, ln_g, ln_b, w_s, bias):
    s, e2 = z.shape
    e = e2 // 2
    eg = e // GMLP_GROUPS

    def body(z_ref, lg_ref, lb_ref, ws_ref, b_ref, o_ref):
        _, u, _, _, vln = _gmlp_gate_rows(z_ref, lg_ref, lb_ref, e)
        vb = vln.astype(BF16)
        for g in range(GMLP_GROUPS):
            cols = slice(g * eg, (g + 1) * eg)
            wm, _ = _causal_weights(ws_ref, g)
            f = jnp.dot(wm, vb[:, cols], preferred_element_type=F32) + b_ref[:, cols]
            o_ref[:, cols] = (u[:, cols] * f).astype(BF16)

    full = lambda shape: pl.BlockSpec(shape, lambda i: (0,) * len(shape))
    return pl.pallas_call(
        body, name=name, grid=(s // CHUNK,),
        in_specs=[pl.BlockSpec((CHUNK, e2), lambda i: (i, 0)), full((1, e)), full((1, e)),
                  full((GMLP_GROUPS, CHUNK, CHUNK)), full((CHUNK, e))],
        out_specs=pl.BlockSpec((CHUNK, e), lambda i: (i, 0)), out_shape=jax.ShapeDtypeStruct((s, e), BF16),
        compiler_params=_params(dimension_semantics=("arbitrary",)),
    )(z, ln_g.reshape(1, e), ln_b.reshape(1, e), w_s, bias)


def gmlp_bwd(name, z, dp, ln_g, ln_b, w_s, bias):
    s, e2 = z.shape
    e = e2 // 2
    eg = e // GMLP_GROUPS
    t = CHUNK

    def body(z_ref, dp_ref, lg_ref, lb_ref, ws_ref, b_ref, dz_ref, dws_ref, dbs_ref, dlg_ref, dlb_ref):
        first = pl.program_id(0) == 0
        zf, u, vhat, rs, vln = _gmlp_gate_rows(z_ref, lg_ref, lb_ref, e)
        vb = vln.astype(BF16)
        dp = dp_ref[...].astype(F32)
        lane = lax.broadcasted_iota(jnp.int32, (t, LANE), 1)
        dbs = jnp.zeros((t, LANE), F32)
        dvln_parts = []
        for g in range(GMLP_GROUPS):
            cols = slice(g * eg, (g + 1) * eg)
            wm, keep = _causal_weights(ws_ref, g)
            f = jnp.dot(wm, vb[:, cols], preferred_element_type=F32) + b_ref[:, cols]
            dz_ref[:, cols] = (dp[:, cols] * f * _gelu_grad(zf[:, cols])).astype(BF16)
            df = dp[:, cols] * u[:, cols]
            dfb = df.astype(BF16)
            dbs = dbs + jnp.where(lane == g, jnp.sum(df, axis=-1, keepdims=True), 0.0)
            dw = lax.dot_general(dfb, vb[:, cols], (((1,), (1,)), ((), ())), preferred_element_type=F32)
            dw = jnp.where(keep, dw, 0.0)

            @pl.when(first)
            def _():
                dws_ref[g] = dw

            @pl.when(jnp.logical_not(first))
            def _():
                dws_ref[g] += dw

            dvln_parts.append(lax.dot_general(wm, dfb, (((0,), (0,)), ((), ())), preferred_element_type=F32))
        dvln = jnp.concatenate(dvln_parts, axis=-1)
        dvhat = dvln * lg_ref[...]
        dv = rs * (dvhat - jnp.mean(dvhat, axis=-1, keepdims=True)
                   - vhat * jnp.mean(dvhat * vhat, axis=-1, keepdims=True))
        dz_ref[:, e:] = (dv * _gelu_grad(zf[:, e:])).astype(BF16)
        dlg = jnp.sum(dvln * vhat, axis=0, keepdims=True)
        dlb = jnp.sum(dvln, axis=0, keepdims=True)

        @pl.when(first)
        def _():
            dbs_ref[...] = dbs
            dlg_ref[...] = dlg
            dlb_ref[...] = dlb

        @pl.when(jnp.logical_not(first))
        def _():
            dbs_ref[...] += dbs
            dlg_ref[...] += dlg
            dlb_ref[...] += dlb

    full = lambda shape: pl.BlockSpec(shape, lambda i: (0,) * len(shape))
    return pl.pallas_call(
        body, name=name, grid=(s // t,),
        in_specs=[pl.BlockSpec((t, e2), lambda i: (i, 0)), pl.BlockSpec((t, e), lambda i: (i, 0)), full((1, e)),
                  full((1, e)), full((GMLP_GROUPS, t, t)), full((t, e))],
        out_specs=[pl.BlockSpec((t, e2), lambda i: (i, 0)), full((GMLP_GROUPS, t, t)), full((t, LANE)), full((1, e)),
                   full((1, e))],
        out_shape=[jax.ShapeDtypeStruct((s, e2), BF16), jax.ShapeDtypeStruct((GMLP_GROUPS, t, t), F32),
                   jax.ShapeDtypeStruct((t, LANE), F32), jax.ShapeDtypeStruct((1, e), F32),
                   jax.ShapeDtypeStruct((1, e), F32)],
        compiler_params=_params(dimension_semantics=("arbitrary",)),
    )(z, dp, ln_g.reshape(1, e), ln_b.reshape(1, e), w_s, bias)


EDGE = 16


def _shift_down(zc, prev, k):
    tm = zc.shape[0]
    row = lax.broadcasted_iota(jnp.int32, (tm, 1), 0)
    out = pltpu.roll(zc, k, 0)
    for j in range(k):
        out = jnp.where(row == j, prev[EDGE - k + j:EDGE - k + j + 1, :], out)
    return out


def _shift_up(dc, nxt, k):
    tm = dc.shape[0]
    row = lax.broadcasted_iota(jnp.int32, (tm, 1), 0)
    out = pltpu.roll(dc, tm - k, 0)
    for j in range(k):
        out = jnp.where(row == tm - k + j, nxt[j:j + 1, :], out)
    return out


def conv_fwd(name, bcv, cw):
    s, d3 = bcv.shape
    d = d3 // 3
    tm = _row_tile(s, d * 4, 2 * 2**20)
    per = tm // EDGE

    def body(b_ref, c_ref, v_ref, cp_ref, vp_ref, w_ref, o_ref):
        i = pl.program_id(0)
        zc = c_ref[...].astype(F32) * v_ref[...].astype(F32)
        prev = jnp.where(i > 0, cp_ref[...].astype(F32) * vp_ref[...].astype(F32), 0.0)
        conv = w_ref[2:3, :] * zc + w_ref[1:2, :] * _shift_down(zc, prev, 1) + w_ref[0:1, :] * _shift_down(zc, prev, 2)
        o_ref[...] = (b_ref[...].astype(F32) * conv).astype(BF16)

    blk = lambda col: pl.BlockSpec((tm, d), lambda i: (i, col))
    edge = lambda col: pl.BlockSpec((EDGE, d), lambda i: (jnp.maximum(i * per - 1, 0), col))
    return pl.pallas_call(
        body, name=name, grid=(s // tm,),
        in_specs=[blk(0), blk(1), blk(2), edge(1), edge(2), pl.BlockSpec((3, d), lambda i: (0, 0))],
        out_specs=pl.BlockSpec((tm, d), lambda i: (i, 0)), out_shape=jax.ShapeDtypeStruct((s, d), BF16),
        compiler_params=_params(dimension_semantics=("arbitrary",)),
    )(bcv, bcv, bcv, bcv, bcv, cw)


def conv_bwd(name, bcv, dq, cw):
    s, d3 = bcv.shape
    d = d3 // 3
    tm = _row_tile(s, d * 4, 2**20)
    per = tm // EDGE
    n_tiles = s // tm
    last_edge = s // EDGE - 1

    def body(b_ref, c_ref, v_ref, cp_ref, vp_ref, bn_ref, dq_ref, dqn_ref, w_ref, o_ref, dw_ref):
        i = pl.program_id(0)
        b = b_ref[...].astype(F32)
        c = c_ref[...].astype(F32)
        v = v_ref[...].astype(F32)
        dq = dq_ref[...].astype(F32)
        zc = c * v
        prev = jnp.where(i > 0, cp_ref[...].astype(F32) * vp_ref[...].astype(F32), 0.0)
        z1 = _shift_down(zc, prev, 1)
        z2 = _shift_down(zc, prev, 2)
        w0, w1, w2 = w_ref[0:1, :], w_ref[1:2, :], w_ref[2:3, :]
        conv = w2 * zc + w1 * z1 + w0 * z2
        dconv = dq * b
        nxt = jnp.where(i < n_tiles - 1, dqn_ref[...].astype(F32) * bn_ref[...].astype(F32), 0.0)
        dz = w2 * dconv + w1 * _shift_up(dconv, nxt, 1) + w0 * _shift_up(dconv, nxt, 2)
        o_ref[:, :d] = (dq * conv).astype(BF16)
        o_ref[:, d:2 * d] = (dz * v).astype(BF16)
        o_ref[:, 2 * d:] = (dz * c).astype(BF16)
        dw = jnp.concatenate([jnp.sum(dconv * z2, axis=0, keepdims=True), jnp.sum(dconv * z1, axis=0, keepdims=True),
                              jnp.sum(dconv * zc, axis=0, keepdims=True), jnp.zeros((5, d), F32)], axis=0)

        @pl.when(i == 0)
        def _():
            dw_ref[...] = dw

        @pl.when(i > 0)
        def _():
            dw_ref[...] += dw

    blk = lambda col: pl.BlockSpec((tm, d), lambda i: (i, col))
    before = lambda col: pl.BlockSpec((EDGE, d), lambda i: (jnp.maximum(i * per - 1, 0), col))
    after = lambda col: pl.BlockSpec((EDGE, d), lambda i: (jnp.minimum((i + 1) * per, last_edge), col))
    return pl.pallas_call(
        body, name=name, grid=(n_tiles,),
        in_specs=[blk(0), blk(1), blk(2), before(1), before(2), after(0), blk(0), after(0),
                  pl.BlockSpec((3, d), lambda i: (0, 0))],
        out_specs=[pl.BlockSpec((tm, d3), lambda i: (i, 0)), pl.BlockSpec((8, d), lambda i: (0, 0))],
        out_shape=[jax.ShapeDtypeStruct((s, d3), BF16), jax.ShapeDtypeStruct((8, d), F32)],
        compiler_params=_params(dimension_semantics=("arbitrary",)),
    )(bcv, bcv, bcv, bcv, bcv, bcv, dq, dq, cw)


def _attn_probs(qh, kh, scale):
    sc = lax.dot_general(qh, kh, (((1,), (1,)), ((), ())), preferred_element_type=F32) * scale
    ex = jnp.exp(sc - jnp.max(sc, axis=-1, keepdims=True))
    return ex / jnp.sum(ex, axis=-1, keepdims=True)


def attn_fwd(name, q, kv):
    s, d = q.shape
    mlen = kv.shape[0]
    dh = d // XATTN_HEADS
    scale = dh ** -0.5
    tm = _row_tile(s, d * 4, 2 * 2**20)

    def body(q_ref, kv_ref, o_ref):
        for h in range(XATTN_HEADS):
            cols = slice(h * dh, (h + 1) * dh)
            p = _attn_probs(q_ref[:, cols], kv_ref[:, cols], scale)
            o_ref[:, cols] = jnp.dot(p.astype(BF16), kv_ref[:, d + h * dh:d + (h + 1) * dh],
                                     preferred_element_type=F32).astype(BF16)

    return pl.pallas_call(
        body, name=name, grid=(s // tm,),
        in_specs=[pl.BlockSpec((tm, d), lambda i: (i, 0)), pl.BlockSpec((mlen, 2 * d), lambda i: (0, 0))],
        out_specs=pl.BlockSpec((tm, d), lambda i: (i, 0)), out_shape=jax.ShapeDtypeStruct((s, d), BF16),
        compiler_params=_params(dimension_semantics=("arbitrary",)),
    )(q, kv)


def attn_bwd(name, q, kv, do):
    s, d = q.shape
    mlen = kv.shape[0]
    dh = d // XATTN_HEADS
    scale = dh ** -0.5
    tm = _row_tile(s, d * 4, 2 * 2**20)

    def body(q_ref, kv_ref, do_ref, dq_ref, dkv_ref):
        first = pl.program_id(0) == 0
        for h in range(XATTN_HEADS):
            cols = slice(h * dh, (h + 1) * dh)
            vcols = slice(d + h * dh, d + (h + 1) * dh)
            qh, kh, vh, doh = q_ref[:, cols], kv_ref[:, cols], kv_ref[:, vcols], do_ref[:, cols]
            p = _attn_probs(qh, kh, scale)
            dp = lax.dot_general(doh, vh, (((1,), (1,)), ((), ())), preferred_element_type=F32)
            ds = (p * (dp - jnp.sum(dp * p, axis=-1, keepdims=True)) * scale).astype(BF16)
            dq_ref[:, cols] = jnp.dot(ds, kh, preferred_element_type=F32).astype(BF16)
            dk = lax.dot_general(ds, qh, (((0,), (0,)), ((), ())), preferred_element_type=F32)
            dv = lax.dot_general(p.astype(BF16), doh, (((0,), (0,)), ((), ())), preferred_element_type=F32)

            @pl.when(first)
            def _():
                dkv_ref[:, cols] = dk
                dkv_ref[:, vcols] = dv

            @pl.when(jnp.logical_not(first))
            def _():
                dkv_ref[:, cols] += dk
                dkv_ref[:, vcols] += dv

    row = pl.BlockSpec((tm, d), lambda i: (i, 0))
    whole = pl.BlockSpec((mlen, 2 * d), lambda i: (0, 0))
    return pl.pallas_call(
        body, name=name, grid=(s // tm,), in_specs=[row, whole, row], out_specs=[row, whole],
        out_shape=[jax.ShapeDtypeStruct((s, d), BF16), jax.ShapeDtypeStruct((mlen, 2 * d), F32)],
        compiler_params=_params(dimension_semantics=("arbitrary",)),
    )(q, kv, do)


def _as_rows(a):
    if a.ndim >= 2 and a.shape[-1] % LANE == 0:
        return a.reshape(-1, a.shape[-1])
    return a.reshape(-1, LANE) if a.size % LANE == 0 else a.reshape(1, -1)


def add_halves(name, dw, other, core):
    p, r, c = dw.shape
    h = r // 2
    th = _row_tile(h, c * 2, 2 * 2**20)

    def body(core_ref, a_ref, b_ref, o_ref):
        o_ref[...] = (a_ref[...].astype(F32) + b_ref[...].astype(F32)).astype(BF16)

    grid_spec = pltpu.PrefetchScalarGridSpec(
        num_scalar_prefetch=1, grid=(p, h // th),
        in_specs=[pl.BlockSpec((None, None, th, c), lambda pi, i, core_ref: (pi, core_ref[0], i, 0)),
                  pl.BlockSpec((None, th, c), lambda pi, i, core_ref: (pi, i, 0))],
        out_specs=pl.BlockSpec((None, th, c), lambda pi, i, core_ref: (pi, i, 0)))
    return pl.pallas_call(
        body, name=name, grid_spec=grid_spec, out_shape=jax.ShapeDtypeStruct((p, h, c), BF16),
        compiler_params=_params(dimension_semantics=("arbitrary", "arbitrary")),
    )(core, dw.reshape(p, 2, h, c), other)


def sum_leading(name, parts):
    n, r, c = parts.shape
    tr = _row_tile(r, c * 4 * 2, 2 * 2**20)

    def body(p_ref, o_ref):
        acc = p_ref[0].astype(F32)
        for k in range(1, n):
            acc = acc + p_ref[k].astype(F32)
        o_ref[...] = acc

    return pl.pallas_call(
        body, name=name, grid=(r // tr,), in_specs=[pl.BlockSpec((n, tr, c), lambda i: (0, i, 0))],
        out_specs=pl.BlockSpec((tr, c), lambda i: (i, 0)), out_shape=jax.ShapeDtypeStruct((r, c), F32),
        compiler_params=_params(dimension_semantics=("arbitrary",)),
    )(parts)


def _adamw_rows(w, g, m, v):
    m = ADAM_B1 * m + (1.0 - ADAM_B1) * g
    v = ADAM_B2 * v + (1.0 - ADAM_B2) * (g * g)
    m_hat = m / (1.0 - ADAM_B1 ** ADAM_STEP)
    v_hat = v / (1.0 - ADAM_B2 ** ADAM_STEP)
    delta = -ADAM_LR * (m_hat / (jnp.sqrt(v_hat) + ADAM_EPS) + ADAM_WD * w)
    return delta, m, v


def adamw_layer(name, w, m, v, g, layer, carried):
    nl, r, c = w.shape
    tr = _row_tile(r, c * 4, 2**20)
    n_carried = 4 if carried is not None else 0

    def body(*refs):
        w_ref, m_ref, v_ref, g_ref = refs[:4]
        go_ref, d_ref, mo_ref, vo_ref = refs[4 + n_carried:]
        g = g_ref[...]
        delta, m_new, v_new = _adamw_rows(w_ref[...], g, m_ref[...], v_ref[...])
        go_ref[...] = g
        d_ref[...] = delta
        mo_ref[...] = m_new
        vo_ref[...] = v_new

    stacked = pl.BlockSpec((None, tr, c), lambda i: (layer, i, 0))
    in_specs = [stacked, stacked, stacked, pl.BlockSpec((tr, c), lambda i: (i, 0))]
    in_specs += [pl.BlockSpec(memory_space=pl.ANY)] * n_carried
    shape = jax.ShapeDtypeStruct((nl, r, c), F32)
    return pl.pallas_call(
        body, name=name, grid=(r // tr,), in_specs=in_specs, out_specs=[stacked] * 4, out_shape=[shape] * 4,
        input_output_aliases={4 + k: k for k in range(n_carried)},
        compiler_params=_params(dimension_semantics=("arbitrary",)),
    )(w, m, v, g, *(carried or ()))


def adamw_flat(name, w, m, v, g):
    r, c = w.shape

    def body(w_ref, m_ref, v_ref, g_ref, d_ref, mo_ref, vo_ref):
        delta, m_new, v_new = _adamw_rows(w_ref[...], g_ref[...], m_ref[...], v_ref[...])
        d_ref[...] = delta
        mo_ref[...] = m_new
        vo_ref[...] = v_new

    shape = jax.ShapeDtypeStruct((r, c), F32)
    return pl.pallas_call(body, name=name, out_shape=[shape] * 3, compiler_params=_params())(w, m, v, g)


def cast_place(name, w, layer, place):
    nl, r, c = w.shape
    tr = _row_tile(r, c * 4, 2 * 2**20)

    def body(x_ref, y_ref, c_ref, w_ref, o_ref):
        o_ref[...] = w_ref[...].astype(BF16)

    grid_spec = pltpu.PrefetchScalarGridSpec(
        num_scalar_prefetch=3, grid=(r // tr,),
        in_specs=[pl.BlockSpec((None, tr, c), lambda i, x_ref, y_ref, c_ref: (layer, i, 0))],
        out_specs=pl.BlockSpec((None, tr, c), lambda i, x_ref, y_ref, c_ref: (2 * x_ref[0] + y_ref[0], i, 0)))
    return pl.pallas_call(
        body, name=name, grid_spec=grid_spec, out_shape=jax.ShapeDtypeStruct((N_CHIPS, r, c), BF16),
        compiler_params=_params(dimension_semantics=("arbitrary",)),
    )(*place, w)


def reduce_sum4(name, own, landed, place):
    p, h, c = own.shape
    tr = _row_tile(h, c * 4, 2**20)

    def body(x_ref, y_ref, c_ref, t_ref, y1_ref, y2_ref, y3_ref, o_ref):
        acc = t_ref[...].astype(F32)
        for part_ref in (y1_ref, y2_ref, y3_ref):
            acc = acc + part_ref[...].astype(F32)
        o_ref[...] = acc

    def panel(fx, fy):
        return pl.BlockSpec((None, tr, c), lambda i, x_ref, y_ref, c_ref: (
            2 * (1 - x_ref[0] if fx else x_ref[0]) + (1 - y_ref[0] if fy else y_ref[0]), i, 0))

    grid_spec = pltpu.PrefetchScalarGridSpec(
        num_scalar_prefetch=3, grid=(h // tr,),
        in_specs=[panel(0, 0), panel(1, 0), panel(0, 1), panel(1, 1)],
        out_specs=pl.BlockSpec((None, tr, c), lambda i, x_ref, y_ref, c_ref: (c_ref[0], i, 0)))
    return pl.pallas_call(
        body, name=name, grid_spec=grid_spec, out_shape=jax.ShapeDtypeStruct((2, h, c), F32),
        compiler_params=_params(dimension_semantics=("arbitrary",)),
    )(*place, own, landed, landed, landed)


def run_exchange(name, exchange):
    n_in, n_out = len(exchange.inputs), len(exchange.out_shapes)

    def body(*refs):
        ins, outs, sems = refs[:n_in], refs[n_in:n_in + n_out], refs[n_in + n_out:]
        exchange.start(ins, outs, sems)
        exchange.finish(ins, outs, sems)

    return pl.pallas_call(
        body, name=name, in_specs=[ANY] * n_in, out_specs=[ANY] * n_out, out_shape=list(exchange.out_shapes),
        scratch_shapes=[pltpu.SemaphoreType.DMA((exchange.n_sems,)), pltpu.SemaphoreType.DMA((exchange.n_sems,))],
        input_output_aliases=dict(exchange.aliases),
    )(*exchange.inputs)


def _row_halves(ref, c):
    h = ref.shape[1] // 2
    return pl.ds(pl.multiple_of(c * h, 16), h), pl.ds(pl.multiple_of((1 - c) * h, 16), h)


def gather_exchange(fulls):
    n = len(fulls)

    def start(ins, outs, sems):
        x, y, c, mine, chips = _place()
        for a in range(n):
            rows = outs[a].at[mine, _row_halves(outs[a], c)[0]]
            for j, chip in enumerate(chips):
                _remote(rows, rows, sems, 6 * a + j, (*chip, c)).start()

    def finish(ins, outs, sems):
        x, y, c, mine, chips = _place()
        sibling = (x, y, 1 - c)
        for a in range(n):
            half = _row_halves(outs[a], c)[0]
            for j, chip in enumerate(chips):
                rows = outs[a].at[2 * chip[0] + chip[1], half]
                _remote(rows, rows, sems, 6 * a + j, sibling).wait_recv()
                _remote(rows, rows, sems, 6 * a + 3 + j, sibling).start()
        for a in range(n):
            half, other = _row_halves(outs[a], c)
            for j, chip in enumerate(chips):
                rows = outs[a].at[2 * chip[0] + chip[1], other]
                _remote(rows, rows, sems, 6 * a + 3 + j, sibling).wait_recv()
            for j, chip in enumerate(chips):
                rows = outs[a].at[mine, half]
                _remote(rows, rows, sems, 6 * a + j, (*chip, c)).wait_send()
                rows = outs[a].at[2 * chip[0] + chip[1], half]
                _remote(rows, rows, sems, 6 * a + 3 + j, sibling).wait_send()

    return Exchange(list(fulls), [jax.ShapeDtypeStruct(f.shape, f.dtype) for f in fulls], {a: a for a in range(n)},
                    6 * n, start, finish)


def swap_exchange(grads):
    n = len(grads)

    def copies(ins, outs, sems):
        x, y, c, _, _ = _place()
        return [_remote(ins[a].at[:, _row_halves(ins[a], c)[1]], outs[a], sems, a, (x, y, 1 - c)) for a in range(n)]

    def start(ins, outs, sems):
        for cp in copies(ins, outs, sems):
            cp.start()

    def finish(ins, outs, sems):
        for cp in copies(ins, outs, sems):
            cp.wait()

    shapes = [jax.ShapeDtypeStruct((g.shape[0], g.shape[1] // 2, g.shape[2]), g.dtype) for g in grads]
    return Exchange(list(grads), shapes, {}, n, start, finish)


def scatter_exchange(parts):
    n = len(parts)

    def sends(ins, outs, sems):
        x, y, c, mine, chips = _place()
        return [_remote(ins[a].at[2 * chip[0] + chip[1]], outs[a].at[mine], sems, 3 * a + j, (*chip, c))
                for a in range(n) for j, chip in enumerate(chips)]

    def start(ins, outs, sems):
        for cp in sends(ins, outs, sems):
            cp.start()

    def finish(ins, outs, sems):
        x, y, c, mine, chips = _place()
        for a in range(n):
            for j, chip in enumerate(chips):
                landing = outs[a].at[2 * chip[0] + chip[1]]
                _remote(landing, landing, sems, 3 * a + j, (*chip, c)).wait_recv()
        for cp in sends(ins, outs, sems):
            cp.wait_send()

    return Exchange(list(parts), [jax.ShapeDtypeStruct(g.shape, g.dtype) for g in parts], {}, 3 * n, start, finish)


def join_exchange(halves):
    n = len(halves)

    def start(ins, outs, sems):
        x, y, c, _, _ = _place()
        for a in range(n):
            _remote(outs[a].at[c], outs[a].at[c], sems, a, (x, y, 1 - c)).start()

    def finish(ins, outs, sems):
        x, y, c, _, _ = _place()
        for a in range(n):
            _remote(outs[a].at[1 - c], outs[a].at[1 - c], sems, a, (x, y, 1 - c)).wait_recv()
        for a in range(n):
            _remote(outs[a].at[c], outs[a].at[c], sems, a, (x, y, 1 - c)).wait_send()

    return Exchange(list(halves), [jax.ShapeDtypeStruct(g.shape, g.dtype) for g in halves], {a: a for a in range(n)},
                    n, start, finish)


def gather_all(name, rows):
    def body(in_ref, out_ref, send_sems, recv_sems, local_sem):
        sems = (send_sems, recv_sems)
        x, y, c, _, _ = _place()
        me = 4 * x + 2 * y + c
        local = pltpu.make_async_copy(in_ref, out_ref.at[me], local_sem)
        local.start()
        peers = [(1 - x if k & 4 else x, 1 - y if k & 2 else y, 1 - c if k & 1 else c) for k in range(1, N_DEV)]
        sent = []
        for k, peer in enumerate(peers):
            cp = _remote(in_ref, out_ref.at[me], sems, k, peer)
            cp.start()
            sent.append(cp)
        for k, peer in enumerate(peers):
            landing = out_ref.at[4 * peer[0] + 2 * peer[1] + peer[2]]
            _remote(landing, landing, sems, k, peer).wait_recv()
        for cp in sent:
            cp.wait_send()
        local.wait()

    return pl.pallas_call(
        body, name=name, in_specs=[ANY], out_specs=ANY,
        out_shape=jax.ShapeDtypeStruct((N_DEV,) + rows.shape, rows.dtype),
        scratch_shapes=[pltpu.SemaphoreType.DMA((N_DEV - 1,)), pltpu.SemaphoreType.DMA((N_DEV - 1,)),
                        pltpu.SemaphoreType.DMA],
    )(rows)


class _Step:
    def __init__(self, p):
        self.p = p
        xi, yi, ci = lax.axis_index("x"), lax.axis_index("y"), lax.axis_index("c")
        self.chip = 2 * xi + yi
        self.place_refs = tuple(v.astype(jnp.int32).reshape(1) for v in (xi, yi, ci))
        self.core_ref = self.place_refs[2]
        self.depth = p['ffn1_norm'].shape[0]
        self.placed, self.w, self.big_g = {}, {}, {}
        self.waiting_joins = []
        self.waiting_scatter = None

    def block_keys(self, tag, l):
        if l >= self.depth:
            return []
        mixer = ['gmlp_w_in', 'gmlp_w_out'] if l % 2 == 0 else ['conv_w_in', 'conv_w_out']
        names = {"ffn1": ['ffn1_w13', 'ffn1_w2'], "mix": mixer, "xattn": ['xattn_wq', 'xattn_wkv', 'xattn_wo'],
                 "ffn2": ['ffn2_w13', 'ffn2_w2']}[tag]
        return [(n, l // 2 if tag == "mix" else l) for n in names]

    def place(self, keys):
        for n, idx in keys:
            self.placed[(n, idx)] = cast_place(f"place_{n}{idx}", self.p[n], idx, self.place_refs)

    def gather_alone(self, name, keys):
        got = run_exchange(name, gather_exchange([self.placed[k] for k in keys]))
        self.w.update(zip(keys, got, strict=True))

    def carrying_gather(self, mm, keys, *args, **kw):
        keys = [k for k in keys if k in self.placed]
        if not keys:
            return mm(*args, **kw)
        out, got = mm(*args, exchange=gather_exchange([self.placed[k] for k in keys]), **kw)
        self.w.update(zip(keys, got, strict=True))
        return out

    def reduce_begin(self, tag, keys, dws):
        theirs = run_exchange(tag + "_swap", swap_exchange(dws))
        parts = [add_halves(f"{tag}_add{i}", dw, t, self.core_ref) for i, (dw, t) in enumerate(zip(dws, theirs, strict=True))]
        assert self.waiting_scatter is None
        self.waiting_scatter = (tag, keys, parts)

    def carrying_scatter(self, mm, *args, **kw):
        tag, keys, parts = self.waiting_scatter
        self.waiting_scatter = None
        out, landed = mm(*args, exchange=scatter_exchange(parts), **kw)
        halves = [reduce_sum4(f"{tag}_sum{i}", t, y, self.place_refs) for i, (t, y) in enumerate(zip(parts, landed, strict=True))]
        self.waiting_joins += list(zip(keys, halves, strict=True))
        return out

    def take_joined(self, keys, joined):
        for k, g in zip(keys, joined, strict=True):
            self.big_g[k] = g.reshape(-1, g.shape[-1])

    def carrying_joins(self, mm, *args, **kw):
        if not self.waiting_joins:
            return mm(*args, **kw)
        keys, halves = zip(*self.waiting_joins, strict=True)
        self.waiting_joins = []
        out, joined = mm(*args, exchange=join_exchange(list(halves)), **kw)
        self.take_joined(keys, joined)
        return out

    def joins_alone(self, name):
        keys, halves = zip(*self.waiting_joins, strict=True)
        self.waiting_joins = []
        self.take_joined(keys, run_exchange(name, join_exchange(list(halves))))

    def ffn_fwd(self, tag, l, x, gain, carry13, carry2):
        name = f"l{l}_{tag}"
        h = rms_fwd(name + "_norm", x, gain)
        gu = self.carrying_gather(mm_nn, carry13, name + "_w13", h, self.w[(tag + '_w13', l)], 'col', BF16)
        act = swiglu_fwd(name + "_act", gu)
        out = self.carrying_gather(mm_nn, carry2, name + "_w2", act, self.w[(tag + '_w2', l)], 'row', F32, res=x,
                                   scale=0.5)
        return out, (x, h, gu, act)

    def ffn_bwd(self, tag, l, dx, dxb, saved, gain):
        w13, w2 = self.w[(tag + '_w13', l)], self.w[(tag + '_w2', l)]
        name = f"l{l}_{tag}"
        x, h, gu, act = saved
        d_act = self.carrying_joins(mm_nt, name + "_dact", dxb, w2, 'row', BF16, scale=0.5)
        d_w2 = mm_tn(name + "_dw2", act, dxb, 'row', scale=0.5)
        d_gu = swiglu_bwd(name + "_dgu", d_act, gu)
        d_w13 = mm_tn(name + "_dw13", h, d_gu, 'col')
        self.reduce_begin(name, [(tag + '_w13', l), (tag + '_w2', l)], [d_w13, d_w2])
        dh = self.carrying_scatter(mm_nt, name + "_dh", d_gu, w13, 'col', F32)
        return rms_bwd(name + "_dnorm", x, gain, dh, dx)


def kernel(x, mem, ffn1_norm, ffn1_w13, ffn1_w2, mix_norm, gmlp_w_in, gmlp_ln_g, gmlp_ln_b, gmlp_w_s, gmlp_b_s, gmlp_w_out, conv_w_in, conv_w, conv_w_out, xattn_norm, mem_norm, xattn_wq, xattn_wkv, xattn_wo, ffn2_norm, ffn2_w13, ffn2_w2, final_norm, loss_target, m_ffn1_norm, m_ffn1_w13, m_ffn1_w2, m_mix_norm, m_gmlp_w_in, m_gmlp_ln_g, m_gmlp_ln_b, m_gmlp_w_s, m_gmlp_b_s, m_gmlp_w_out, m_conv_w_in, m_conv_w, m_conv_w_out, m_xattn_norm, m_mem_norm, m_xattn_wq, m_xattn_wkv, m_xattn_wo, m_ffn2_norm, m_ffn2_w13, m_ffn2_w2, m_final_norm, v_ffn1_norm, v_ffn1_w13, v_ffn1_w2, v_mix_norm, v_gmlp_w_in, v_gmlp_ln_g, v_gmlp_ln_b, v_gmlp_w_s, v_gmlp_b_s, v_gmlp_w_out, v_conv_w_in, v_conv_w, v_conv_w_out, v_xattn_norm, v_mem_norm, v_xattn_wq, v_xattn_wkv, v_xattn_wo, v_ffn2_norm, v_ffn2_w13, v_ffn2_w2, v_final_norm):
    return _step(dict(locals()))


def _step(p):
    assert sorted(p) == sorted(ARG_NAMES)
    st = _Step(p)
    x = p['x'][0]
    mem = p['mem'][0]
    target = p['loss_target'][0]
    s, d = x.shape
    depth = st.depth

    for l in range(depth):
        for tag in ("ffn1", "mix", "xattn", "ffn2"):
            st.place(st.block_keys(tag, l))
    st.gather_alone("gather_first", [('ffn1_w13', 0)])

    cw_local = p['conv_w']
    n_conv, cwid, dq4 = cw_local.shape
    cw_rows = jnp.pad(cw_local.reshape(-1, LANE), ((0, (-cw_local.size // LANE) % 8), (0, 0)))
    cw_all = gather_all("gather_conv_w", cw_rows)[0::2, :cw_local.size // LANE]
    conv_w_full = cw_all.reshape(N_CHIPS, n_conv, cwid, dq4).transpose(1, 2, 0, 3).reshape(n_conv, cwid, N_CHIPS * dq4)

    saved = []
    for l in range(depth):
        j = l // 2
        rec = {}
        first_w2 = [('ffn1_w2', 0)] if l == 0 else []
        x, rec['ffn1'] = st.ffn_fwd("ffn1", l, x, p['ffn1_norm'][l],
                                    first_w2 + st.block_keys("mix", l) + st.block_keys("xattn", l), [('ffn2_w13', l)])
        h = rms_fwd(f"l{l}_mix_norm", x, p['mix_norm'][l])
        if l % 2 == 0:
            e = p['gmlp_ln_g'].shape[-1]
            bias = jnp.repeat(p['gmlp_b_s'][j].T, e // GMLP_GROUPS, axis=1)
            z = st.carrying_gather(mm_nn, [('ffn2_w2', l)], f"l{l}_gmlp_in", h, st.w[('gmlp_w_in', j)], 'col', BF16)
            gate = gmlp_fwd(f"l{l}_gmlp_gate", z, p['gmlp_ln_g'][j], p['gmlp_ln_b'][j], p['gmlp_w_s'][j], bias)
            x_new = mm_nn(f"l{l}_gmlp_out", gate, st.w[('gmlp_w_out', j)], 'row', F32, res=x)
            rec['mix'] = (x, h, z, gate, bias)
        else:
            bcv = st.carrying_gather(mm_nn, [('ffn2_w2', l)], f"l{l}_conv_in", h, st.w[('conv_w_in', j)], 'col', BF16)
            gate = conv_fwd(f"l{l}_conv_gate", bcv, conv_w_full[j])
            x_new = mm_nn(f"l{l}_conv_out", gate, st.w[('conv_w_out', j)], 'row', F32, res=x)
            rec['mix'] = (x, h, bcv, gate)
        x = x_new
        hq = rms_fwd(f"l{l}_xattn_norm", x, p['xattn_norm'][l])
        q = mm_nn(f"l{l}_xattn_q", hq, st.w[('xattn_wq', l)], 'row', BF16)
        mem_n = rms_fwd(f"l{l}_mem_norm", mem, p['mem_norm'][l])
        kv = mm_nn(f"l{l}_xattn_kv", mem_n, st.w[('xattn_wkv', l)], 'col', BF16)
        o = attn_fwd(f"l{l}_xattn_core", q, kv)
        x_new = mm_nn(f"l{l}_xattn_o", o, st.w[('xattn_wo', l)], 'row', F32, res=x)
        rec['xattn'] = (x, hq, q, mem_n, kv, o)
        x = x_new
        x, rec['ffn2'] = st.ffn_fwd("ffn2", l, x, p['ffn2_norm'][l], st.block_keys("ffn1", l + 1), [])
        saved.append(rec)

    dx, dxb, d_final, loss_lanes = loss_head("loss_head", x, p['final_norm'], target)
    loss = lax.psum(0.5 * jnp.sum(loss_lanes) / d, ("x", "y", "c"))

    small = {n: [None] * p[n].shape[0] for n in ('ffn1_norm', 'mix_norm', 'xattn_norm', 'mem_norm', 'ffn2_norm',
                                                  'gmlp_ln_g', 'gmlp_ln_b', 'gmlp_w_s', 'gmlp_b_s', 'conv_w')}
    for l in reversed(range(depth)):
        j = l // 2
        rec = saved[l]
        dx, dxb, small['ffn2_norm'][l] = st.ffn_bwd("ffn2", l, dx, dxb, rec['ffn2'], p['ffn2_norm'][l])

        x_in, hq, q, mem_n, kv, o = rec['xattn']
        name = f"l{l}_xattn"
        do = st.carrying_joins(mm_nt, name + "_do", dxb, st.w[('xattn_wo', l)], 'row', BF16)
        d_wo = mm_tn(name + "_dwo", o, dxb, 'row')
        dq, dkv = attn_bwd(name + "_dcore", q, kv, do)
        d_wq = mm_tn(name + "_dwq", hq, dq, 'row')
        dkvb = dkv.astype(BF16)
        d_wkv = mm_tn(name + "_dwkv", mem_n, dkvb, 'col')
        st.reduce_begin(name, [('xattn_wq', l), ('xattn_wkv', l), ('xattn_wo', l)], [d_wq, d_wkv, d_wo])
        dh = mm_nt(name + "_dh", dq, st.w[('xattn_wq', l)], 'row', F32)
        dx, dxb, small['xattn_norm'][l] = rms_bwd(name + "_dnorm", x_in, p['xattn_norm'][l], dh, dx)
        dmem_n = mm_nt(name + "_dmem", dkvb, st.w[('xattn_wkv', l)], 'col', F32)
        small['mem_norm'][l] = rms_bwd(f"l{l}_mem_dnorm", mem, p['mem_norm'][l], dmem_n, None)[2]

        if l % 2 == 0:
            x_in, h, z, gate, bias = rec['mix']
            name = f"l{l}_gmlp"
            w_in, w_out = st.w[('gmlp_w_in', j)], st.w[('gmlp_w_out', j)]
            dgate = mm_nt(name + "_dgate", dxb, w_out, 'row', BF16)
            d_wout = mm_tn(name + "_dwout", gate, dxb, 'row')
            dmix, dws, dbs, dlg, dlb = gmlp_bwd(name + "_dgate_core", z, dgate, p['gmlp_ln_g'][j], p['gmlp_ln_b'][j],
                                                p['gmlp_w_s'][j], bias)
            small['gmlp_w_s'][j], small['gmlp_b_s'][j] = dws, dbs[:, :GMLP_GROUPS].T
            small['gmlp_ln_g'][j], small['gmlp_ln_b'][j] = dlg, dlb
            keys = [('gmlp_w_in', j), ('gmlp_w_out', j)]
        else:
            x_in, h, bcv, gate = rec['mix']
            name = f"l{l}_conv"
            w_in, w_out = st.w[('conv_w_in', j)], st.w[('conv_w_out', j)]
            dgate = mm_nt(name + "_dgate", dxb, w_out, 'row', BF16)
            d_wout = mm_tn(name + "_dwout", gate, dxb, 'row')
            dmix, dcw = conv_bwd(name + "_dgate_core", bcv, dgate, conv_w_full[j])
            small['conv_w'][j] = dcw[:cwid]
            keys = [('conv_w_in', j), ('conv_w_out', j)]
        d_win = st.carrying_scatter(mm_tn, name + "_dwin", h, dmix, 'col')
        st.reduce_begin(name, keys, [d_win, d_wout])
        dh = st.carrying_scatter(mm_nt, name + "_dh", dmix, w_in, 'col', F32)
        dx, dxb, small['mix_norm'][l] = rms_bwd(f"l{l}_mix_dnorm", x_in, p['mix_norm'][l], dh, dx)

        dx, dxb, small['ffn1_norm'][l] = st.ffn_bwd("ffn1", l, dx, dxb, rec['ffn1'], p['ffn1_norm'][l])
    st.joins_alone("join_last")

    small_names = ['ffn1_norm', 'mix_norm', 'xattn_norm', 'mem_norm', 'ffn2_norm', 'gmlp_ln_g', 'gmlp_ln_b', 'gmlp_w_s',
                   'gmlp_b_s', 'final_norm', 'conv_w']
    small_full = {n: jnp.stack([g.reshape(p[n].shape[1:]) for g in small[n]]) for n in small_names
                  if n not in ('final_norm', 'conv_w')}
    small_full['final_norm'] = d_final.reshape(p['final_norm'].shape)
    small_full['conv_w'] = jnp.stack(small['conv_w'])
    packed = jnp.concatenate([small_full[n].reshape(-1, LANE) for n in small_names], axis=0)
    total = sum_leading("small_sum", gather_all("small_gather", packed))
    small_g, at = {}, 0
    for n in small_names:
        rows = small_full[n].size // LANE
        small_g[n] = total[at:at + rows].reshape(small_full[n].shape)
        at += rows
    small_g['conv_w'] = lax.dynamic_slice_in_dim(small_g['conv_w'], st.chip * dq4, dq4, axis=2)

    grads, deltas, new_m, new_v = {}, {}, {}, {}
    for n in WEIGHTS:
        w, m, v = p[n], p['m_' + n], p['v_' + n]
        if n in BIG:
            carried = None
            for i in range(w.shape[0]):
                carried = adamw_layer(f"adamw_{n}{i}", w, m, v, st.big_g[(n, i)], i, carried)
            grads[n], deltas[n], new_m[n], new_v[n] = carried
        else:
            g = small_g[n]
            out = adamw_flat(f"adamw_{n}", _as_rows(w), _as_rows(m), _as_rows(v), _as_rows(g))
            grads[n] = g
            deltas[n], new_m[n], new_v[n] = (o.reshape(w.shape) for o in out)

    grad_x = dx.reshape(p['x'].shape)
    return (loss, grad_x, *[grads[n] for n in WEIGHTS], *[deltas[n] for n in WEIGHTS], *[new_m[n] for n in WEIGHTS],
            *[new_v[n] for n in WEIGHTS])
```

```python
from typing import Callable, NamedTuple

import jax
import jax.numpy as jnp
from jax import lax
from jax.experimental import pallas as pl
from jax.experimental.pallas import tpu as pltpu

F32 = jnp.float32
BF16 = jnp.bfloat16
MESH = pl.DeviceIdType.MESH

CHUNK = 128
GMLP_GROUPS = 8
XATTN_HEADS = 4
RMS_EPS = 1e-6
LN_EPS = 1e-5
ADAM_LR = 0.001
ADAM_B1 = 0.9
ADAM_B2 = 0.999
ADAM_EPS = 1e-08
ADAM_WD = 0.01
ADAM_STEP = 10

N_CHIPS = 4
N_DEV = 8

VMEM_LIMIT_BYTES = 58 * 2**20
VMEM_PLAN_BYTES = 46 * 2**20
LANE = 128
MXU_DIM = 256
MXU_FLOPS_PER_US = 996e6
HBM_BYTES_PER_US = 3.3e6
STEP_US = 0.35
ACC_US_PER_VREG = 0.58e-3

WEIGHTS = ['ffn1_norm', 'ffn1_w13', 'ffn1_w2', 'mix_norm', 'gmlp_w_in', 'gmlp_ln_g', 'gmlp_ln_b', 'gmlp_w_s',
           'gmlp_b_s', 'gmlp_w_out', 'conv_w_in', 'conv_w', 'conv_w_out', 'xattn_norm', 'mem_norm', 'xattn_wq',
           'xattn_wkv', 'xattn_wo', 'ffn2_norm', 'ffn2_w13', 'ffn2_w2', 'final_norm']
BIG = {'ffn1_w13': 'col', 'ffn1_w2': 'row', 'gmlp_w_in': 'col', 'gmlp_w_out': 'row', 'conv_w_in': 'col',
       'conv_w_out': 'row', 'xattn_wq': 'row', 'xattn_wkv': 'col', 'xattn_wo': 'row', 'ffn2_w13': 'col',
       'ffn2_w2': 'row'}
ARG_NAMES = (['x', 'mem'] + WEIGHTS + ['loss_target'] + ['m_' + n for n in WEIGHTS] + ['v_' + n for n in WEIGHTS])


def _params(**kw):
    return pltpu.CompilerParams(vmem_limit_bytes=VMEM_LIMIT_BYTES, **kw)


def _divisors(n, mult, cap):
    return [d for d in range(mult, min(n, cap) + 1, mult) if n % d == 0] or [n]


def _row_tile(rows, width_bytes, budget=4 * 2**20):
    best = None
    for d in _divisors(rows, 16, 1024):
        if d * width_bytes <= budget:
            best = d
    return best or _divisors(rows, 16, 1024)[0]


ANY = pl.BlockSpec(memory_space=pl.ANY)


class Exchange(NamedTuple):
    inputs: list
    out_shapes: list
    aliases: dict
    n_sems: int
    start: Callable
    finish: Callable


def _place():
    x, y, c = lax.axis_index("x"), lax.axis_index("y"), lax.axis_index("c")
    chips = [(1 - x, y), (x, 1 - y), (1 - x, 1 - y)]
    return x, y, c, 2 * x + y, chips


def _remote(src, dst, sems, k, device):
    return pltpu.make_async_remote_copy(src_ref=src, dst_ref=dst, send_sem=sems[0].at[k], recv_sem=sems[1].at[k],
                                        device_id=device, device_id_type=MESH)


def _mxu_fill(dim):
    return dim / (-(-dim // MXU_DIM) * MXU_DIM)


def _tile_time(flops, fill, traffic, steps, acc_vregs):
    return (max(flops / (MXU_FLOPS_PER_US * fill), traffic / HBM_BYTES_PER_US) + steps * STEP_US
            + steps * acc_vregs * ACC_US_PER_VREG)


def _plan_mm(m, n_tiles_of, k_tiles_of, n, k, a_item, o_item, has_res, a_arrays=1):
    best, best_cost = None, None
    for tm in _divisors(m, 16, 1024):
        for tn in n_tiles_of:
            for tk in k_tiles_of:
                ni, nj, nk = m // tm, n // tn, k // tk
                blocks = a_arrays * tm * tk * a_item + tk * tn * 2 + tm * tn * o_item + (tm * tn * 4 if has_res else 0)
                vmem = 2 * blocks + tm * tn * 4 * (2 if nk > 1 else 1)
                if vmem > VMEM_PLAN_BYTES:
                    continue
                traffic = nj * m * k * a_item + (k * n * 2 if nk == 1 else ni * k * n * 2)
                traffic += m * n * (o_item + (4 if has_res else 0))
                cost = _tile_time(2 * m * n * k, _mxu_fill(tk) * _mxu_fill(tn), traffic, ni * nj * nk,
                                  tm * tn // 1024 if nk > 1 else 0)
                if best_cost is None or cost < best_cost:
                    best, best_cost = (tm, tn, tk), cost
    assert best is not None, (m, n, k)
    return best


def _tiled_call(name, grid, operands, in_specs, out_shapes, out_specs, scratch, compute, exchange=None, aliases=None):
    n_reg, n_out, n_scr = len(operands), len(out_shapes), len(scratch)
    n_xin = len(exchange.inputs) if exchange else 0
    n_xout = len(exchange.out_shapes) if exchange else 0
    semantics = ("arbitrary",) * len(grid)

    def body(*refs):
        ins = refs[:n_reg]
        outs = refs[n_reg + n_xin:n_reg + n_xin + n_out]
        scr = refs[n_reg + n_xin + n_out + n_xout:n_reg + n_xin + n_out + n_xout + n_scr]
        if not exchange:
            compute(ins, outs, scr)
            return
        x_ins = refs[n_reg:n_reg + n_xin]
        x_outs = refs[n_reg + n_xin + n_out:n_reg + n_xin + n_out + n_xout]
        sems = refs[-2:]
        at_first, at_last = True, True
        for k, extent in enumerate(grid):
            at_first = jnp.logical_and(at_first, pl.program_id(k) == 0)
            at_last = jnp.logical_and(at_last, pl.program_id(k) == extent - 1)

        @pl.when(at_first)
        def _():
            exchange.start(x_ins, x_outs, sems)

        compute(ins, outs, scr)

        @pl.when(at_last)
        def _():
            exchange.finish(x_ins, x_outs, sems)

    if not exchange:
        return pl.pallas_call(
            body, name=name, grid=grid, in_specs=in_specs, out_specs=out_specs, out_shape=out_shapes,
            scratch_shapes=scratch, input_output_aliases=dict(aliases or {}),
            compiler_params=_params(dimension_semantics=semantics),
        )(*operands)
    assert not aliases
    sems = [pltpu.SemaphoreType.DMA((exchange.n_sems,)), pltpu.SemaphoreType.DMA((exchange.n_sems,))]
    got = pl.pallas_call(
        body, name=name, grid=grid, in_specs=in_specs + [ANY] * n_xin, out_specs=out_specs + [ANY] * n_xout,
        out_shape=out_shapes + list(exchange.out_shapes), scratch_shapes=scratch + sems,
        input_output_aliases={n_reg + i: n_out + o for i, o in exchange.aliases.items()},
        compiler_params=_params(dimension_semantics=semantics),
    )(*operands, *exchange.inputs)
    return list(got[:n_out]), list(got[n_out:])


class Split(NamedTuple):
    slot: int
    other: jax.Array
    spec: pl.BlockSpec
    use_other: Callable


def _mm_call(name, grid, operands, in_specs, out_shape, out_spec, contract, nk, scale, has_res, tile, exchange=None,
             split=None, into=None):
    n_main = len(operands)

    def compute(ins, outs, scr):
        res_ref = ins[2] if has_res else None
        o_ref = outs[0]
        acc_ref = scr[0] if nk > 1 else None

        def finish(v):
            if scale != 1.0:
                v = v * scale
            if has_res:
                v = res_ref[...] + v
            o_ref[...] = v.astype(o_ref.dtype)

        def contribute(a_ref, b_ref):
            b = b_ref[...]
            if b.ndim == 3:
                b = b.reshape(b.shape[0] * b.shape[1], b.shape[2])
            part = lax.dot_general(a_ref[...], b, contract, preferred_element_type=F32)
            if nk == 1:
                finish(part)
                return
            kk = pl.program_id(2)

            @pl.when(kk == 0)
            def _():
                acc_ref[...] = part

            @pl.when(jnp.logical_and(kk > 0, kk < nk - 1))
            def _():
                acc_ref[...] += part

            @pl.when(kk == nk - 1)
            def _():
                finish(acc_ref[...] + part)

        if split is None:
            contribute(ins[0], ins[1])
            return
        use_other = split.use_other(pl.program_id(0), pl.program_id(1), pl.program_id(2))
        pair = [ins[0], ins[1]]
        other = list(pair)
        other[split.slot] = ins[n_main]

        @pl.when(jnp.logical_not(use_other))
        def _():
            contribute(*pair)

        @pl.when(use_other)
        def _():
            contribute(*other)

    aliases = None
    if split is not None:
        operands, in_specs = operands + [split.other], in_specs + [split.spec]
    if into is not None:
        aliases = {len(operands): 0}
        operands, in_specs = operands + [into], in_specs + [ANY]
    got = _tiled_call(name, grid, operands, in_specs, [out_shape], [out_spec],
                      [pltpu.VMEM(tile, F32)] if nk > 1 else [], compute, exchange, aliases)
    if exchange:
        return got[0][0], got[1]
    return got[0]


def mm_nn(name, a, w, kind, out_dtype, res=None, scale=1.0, exchange=None):
    m, k = a.shape
    p, r, c = w.shape
    n = p * c if kind == 'col' else c
    assert k == (r if kind == 'col' else p * r), (name, a.shape, w.shape)
    n_tiles = _divisors(c, LANE, 2816)
    k_tiles = _divisors(r, LANE, 4096)
    if kind == 'row':
        k_tiles = k_tiles + [q * r for q in (2, 4) if p % q == 0]
    tm, tn, tk = _plan_mm(m, n_tiles, k_tiles, n, k, a.dtype.itemsize, jnp.dtype(out_dtype).itemsize, res is not None)
    nk = k // tk
    if kind == 'col':
        cpt = c // tn
        w_spec = pl.BlockSpec((None, tk, tn), lambda j, i, kk: (j // cpt, kk, j % cpt))
    elif tk > r:
        w_spec = pl.BlockSpec((tk // r, r, tn), lambda j, i, kk: (kk, 0, j))
    else:
        rpt = r // tk
        w_spec = pl.BlockSpec((None, tk, tn), lambda j, i, kk: (kk // rpt, kk % rpt, j))
    in_specs = [pl.BlockSpec((tm, tk), lambda j, i, kk: (i, kk)), w_spec]
    operands = [a, w]
    if res is not None:
        in_specs.append(pl.BlockSpec((tm, tn), lambda j, i, kk: (i, j)))
        operands.append(res)
    return _mm_call(name, (n // tn, m // tm, nk), operands, in_specs, jax.ShapeDtypeStruct((m, n), out_dtype),
                    pl.BlockSpec((tm, tn), lambda j, i, kk: (i, j)), (((1,), (0,)), ((), ())), nk, scale,
                    res is not None, (tm, tn), exchange)


def mm_nt(name, a, w, kind, out_dtype, scale=1.0, exchange=None, a_hi=None):
    m, kc = a.shape
    if a_hi is not None:
        assert a_hi.shape == a.shape
        kc = 2 * kc
    p, r, c = w.shape
    n = r if kind == 'col' else p * r
    assert kc == (p * c if kind == 'col' else c), (name, a.shape, w.shape)
    n_tiles = _divisors(r, LANE, 2816)
    k_tiles = _divisors(c, LANE, 4096)
    if kind == 'row':
        n_tiles = n_tiles + [q * r for q in (2, 4) if p % q == 0 and q * r <= 2816]
    tm, tn, tk = _plan_mm(m, n_tiles, k_tiles, n, kc, a.dtype.itemsize, jnp.dtype(out_dtype).itemsize, False,
                          1 if a_hi is None else 2)
    nk = kc // tk
    if kind == 'col':
        cpt = c // tk
        w_spec = pl.BlockSpec((None, tn, tk), lambda j, i, kk: (kk // cpt, j, kk % cpt))
    elif tn > r:
        w_spec = pl.BlockSpec((tn // r, r, tk), lambda j, i, kk: (j, 0, kk))
    else:
        rpt = r // tn
        w_spec = pl.BlockSpec((None, tn, tk), lambda j, i, kk: (j // rpt, j % rpt, kk))
    split = None
    a_spec = pl.BlockSpec((tm, tk), lambda j, i, kk: (i, kk))
    if a_hi is not None:
        half = nk // 2
        assert nk % 2 == 0
        a_spec = pl.BlockSpec((tm, tk), lambda j, i, kk: (i, jnp.minimum(kk, half - 1)))
        split = Split(0, a_hi, pl.BlockSpec((tm, tk), lambda j, i, kk: (i, jnp.maximum(kk - half, 0))),
                      lambda j, i, kk: kk >= half)
    return _mm_call(name, (n // tn, m // tm, nk), [a, w], [a_spec, w_spec], jax.ShapeDtypeStruct((m, n), out_dtype),
                    pl.BlockSpec((tm, tn), lambda j, i, kk: (i, j)), (((1,), (1,)), ((), ())), nk, scale, False,
                    (tm, tn), exchange, split)


def _plan_tn(s, ka, nd, r_tiles, n_tiles):
    best, best_cost = None, None
    for ts in _divisors(s, 16, 2048):
        for tr in r_tiles:
            for tn in n_tiles:
                ni, nj, ns = ka // tr, nd // tn, s // ts
                vmem = 2 * (ts * tr * 2 + ts * tn * 2 + tr * tn * 2) + tr * tn * 4 * (2 if ns > 1 else 1)
                if vmem > VMEM_PLAN_BYTES:
                    continue
                traffic = nj * s * ka * 2 + ni * s * nd * 2 + ka * nd * 2
                cost = _tile_time(2 * s * ka * nd, _mxu_fill(ts) * _mxu_fill(tn), traffic, ni * nj * ns,
                                  tr * tn // 1024 if ns > 1 else 0)
                if best_cost is None or cost < best_cost:
                    best, best_cost = (ts, tr, tn), cost
    assert best is not None, (s, ka, nd)
    return best


def mm_tn(name, a, dy, kind, scale=1.0, exchange=None, panels=(0, N_CHIPS), into=None):
    s, ka = a.shape
    s2, nd = dy.shape
    assert s == s2
    p = N_CHIPS
    first_panel, n_panels = panels
    assert kind == 'col' or panels == (0, p)
    r, c = (ka, nd // n_panels) if kind == 'col' else (ka // p, nd)
    ts, tr, tn = _plan_tn(s, ka, nd, _divisors(r, LANE, 2048), _divisors(c, LANE, 2816))
    ns = s // ts
    if kind == 'col':
        cpt = c // tn
        o_spec = pl.BlockSpec((None, tr, tn), lambda j, i, kk: (first_panel + j // cpt, i, j % cpt))
    else:
        rpt = r // tr
        o_spec = pl.BlockSpec((None, tr, tn), lambda j, i, kk: (i // rpt, i % rpt, j))
    in_specs = [pl.BlockSpec((ts, tr), lambda j, i, kk: (kk, i)), pl.BlockSpec((ts, tn), lambda j, i, kk: (kk, j))]
    return _mm_call(name, (nd // tn, ka // tr, ns), [a, dy], in_specs, jax.ShapeDtypeStruct((p, r, c), BF16), o_spec,
                    (((0,), (0,)), ((), ())), ns, scale, False, (tr, tn), exchange, None, into)


def _plan_fused(m, k, f, tiles, n_w, n_io):
    best, best_cost = None, None
    for tm in _divisors(m, 16, 1024):
        for tn in tiles:
            vmem = 2 * (tm * k * 2 + n_w * k * tn * 2 + n_io * tm * tn * 2) + 4 * tm * tn * 4
            if vmem > VMEM_PLAN_BYTES:
                continue
            traffic = (f // tn) * m * k * 2 + n_w * k * f * 2 + n_io * m * f * 2
            cost = _tile_time(2 * m * k * f * n_w, _mxu_fill(tn), traffic, (f // tn) * (m // tm), 0)
            if best_cost is None or cost < best_cost:
                best, best_cost = (tm, tn), cost
    assert best is not None, (m, k, f)
    return best


def mm_swiglu(name, h, w13, exchange=None):
    m, k = h.shape
    p, r, c = w13.shape
    assert r == k and p % 2 == 0
    f = p * c // 2
    tm, tn = _plan_fused(m, k, f, _divisors(c, LANE, 2816), 2, 3)
    cpt = c // tn

    def compute(ins, outs, scr):
        a = ins[0][...]
        g = jnp.dot(a, ins[1][...], preferred_element_type=F32)
        u = jnp.dot(a, ins[2][...], preferred_element_type=F32)
        outs[0][...] = g.astype(BF16)
        outs[1][...] = u.astype(BF16)
        outs[2][...] = (g * _sigmoid(g) * u).astype(BF16)

    tile = pl.BlockSpec((tm, tn), lambda j, i: (i, j))
    in_specs = [pl.BlockSpec((tm, k), lambda j, i: (i, 0)),
                pl.BlockSpec((None, k, tn), lambda j, i: (j // cpt, 0, j % cpt)),
                pl.BlockSpec((None, k, tn), lambda j, i: (j // cpt + p // 2, 0, j % cpt))]
    shape = jax.ShapeDtypeStruct((m, f), BF16)
    got = _tiled_call(name, (f // tn, m // tm), [h, w13, w13], in_specs, [shape] * 3, [tile] * 3, [], compute, exchange)
    return got


def mm_dswiglu(name, dy, w2, gate, up, scale, exchange=None):
    m, k = dy.shape
    p, r, c = w2.shape
    assert c == k
    f = p * r
    tm, tn = _plan_fused(m, k, f, _divisors(r, LANE, 2816), 1, 4)
    rpt = r // tn

    def compute(ins, outs, scr):
        d = lax.dot_general(ins[0][...], ins[1][...], (((1,), (1,)), ((), ())), preferred_element_type=F32) * scale
        g = ins[2][...].astype(F32)
        sg = _sigmoid(g)
        outs[0][...] = (d * ins[3][...].astype(F32) * (sg * (1.0 + g * (1.0 - sg)))).astype(BF16)
        outs[1][...] = (d * (g * sg)).astype(BF16)

    tile = pl.BlockSpec((tm, tn), lambda j, i: (i, j))
    in_specs = [pl.BlockSpec((tm, k), lambda j, i: (i, 0)),
                pl.BlockSpec((None, tn, k), lambda j, i: (j // rpt, j % rpt, 0)), tile, tile]
    shape = jax.ShapeDtypeStruct((m, f), BF16)
    return _tiled_call(name, (f // tn, m // tm), [dy, w2, gate, up], in_specs, [shape] * 2, [tile] * 2, [], compute,
                       exchange)


def _rms_rows(x, g):
    r = lax.rsqrt(jnp.mean(x * x, axis=-1, keepdims=True) + RMS_EPS)
    xhat = x * r
    return xhat, r, xhat * g


def rms_fwd(name, x, g):
    s, d = x.shape
    tm = _row_tile(s, d * 4)

    def body(x_ref, g_ref, o_ref):
        o_ref[...] = _rms_rows(x_ref[...], g_ref[...])[2].astype(BF16)

    return pl.pallas_call(
        body, name=name, grid=(s // tm,),
        in_specs=[pl.BlockSpec((tm, d), lambda i: (i, 0)), pl.BlockSpec((1, d), lambda i: (0, 0))],
        out_specs=pl.BlockSpec((tm, d), lambda i: (i, 0)), out_shape=jax.ShapeDtypeStruct((s, d), BF16),
        compiler_params=_params(dimension_semantics=("arbitrary",)),
    )(x, g.reshape(1, d))


def _rms_bwd_rows(x, g, dh):
    xhat, r, _ = _rms_rows(x, g)
    u = dh * g
    dx = r * (u - xhat * jnp.mean(u * xhat, axis=-1, keepdims=True))
    return dx, jnp.sum(dh * xhat, axis=0, keepdims=True)


def rms_bwd(name, x, g, dh, dres):
    s, d = x.shape
    tm = _row_tile(s, d * 4, 2 * 2**20)
    has_res = dres is not None

    def body(*refs):
        x_ref, g_ref, dh_ref = refs[:3]
        dres_ref = refs[3] if has_res else None
        dx_ref, dxb_ref, dg_ref = refs[-3:]
        dx, dg = _rms_bwd_rows(x_ref[...], g_ref[...], dh_ref[...].astype(F32))
        if has_res:
            dx = dx + dres_ref[...]
        dx_ref[...] = dx
        dxb_ref[...] = dx.astype(BF16)

        @pl.when(pl.program_id(0) == 0)
        def _():
            dg_ref[...] = dg

        @pl.when(pl.program_id(0) > 0)
        def _():
            dg_ref[...] += dg

    row = pl.BlockSpec((tm, d), lambda i: (i, 0))
    vec = pl.BlockSpec((1, d), lambda i: (0, 0))
    return pl.pallas_call(
        body, name=name, grid=(s // tm,),
        in_specs=[row, vec, row] + ([row] if has_res else []),
        out_specs=[row, row, vec],
        out_shape=[jax.ShapeDtypeStruct((s, d), F32), jax.ShapeDtypeStruct((s, d), BF16),
                   jax.ShapeDtypeStruct((1, d), F32)],
        compiler_params=_params(dimension_semantics=("arbitrary",)),
    )(x, g.reshape(1, d), dh, *([dres] if has_res else []))


def loss_head(name, x, g, target):
    s, d = x.shape
    tm = _row_tile(s, d * 4, 2 * 2**20)

    def body(x_ref, g_ref, t_ref, dx_ref, dxb_ref, dg_ref, loss_ref):
        x = x_ref[...]
        gain = g_ref[...]
        y = _rms_rows(x, gain)[2]
        diff = y - t_ref[...]
        dx, dg = _rms_bwd_rows(x, gain, diff * (1.0 / d))
        dx_ref[...] = dx
        dxb_ref[...] = dx.astype(BF16)
        sq = jnp.sum(diff * diff, axis=0, keepdims=True)

        @pl.when(pl.program_id(0) == 0)
        def _():
            dg_ref[...] = dg
            loss_ref[...] = sq

        @pl.when(pl.program_id(0) > 0)
        def _():
            dg_ref[...] += dg
            loss_ref[...] += sq

    row = pl.BlockSpec((tm, d), lambda i: (i, 0))
    vec = pl.BlockSpec((1, d), lambda i: (0, 0))
    return pl.pallas_call(
        body, name=name, grid=(s // tm,), in_specs=[row, vec, row], out_specs=[row, row, vec, vec],
        out_shape=[jax.ShapeDtypeStruct((s, d), F32), jax.ShapeDtypeStruct((s, d), BF16),
                   jax.ShapeDtypeStruct((1, d), F32), jax.ShapeDtypeStruct((1, d), F32)],
        compiler_params=_params(dimension_semantics=("arbitrary",)),
    )(x, g.reshape(1, d), target)


def _sigmoid(x):
    return 0.5 * jnp.tanh(0.5 * x) + 0.5


_INV_SQRT2 = 0.7071067811865476
_INV_SQRT_2PI = 0.3989422804014327


def _normal_cdf(z):
    return 0.5 * (1.0 + lax.erf(z * _INV_SQRT2))


def _gelu_grad(z, cdf):
    return cdf + z * (_INV_SQRT_2PI * jnp.exp(-0.5 * z * z))


def _causal_weights(ws_ref, g):
    t = ws_ref.shape[-1]
    keep = lax.broadcasted_iota(jnp.int32, (t, t), 0) >= lax.broadcasted_iota(jnp.int32, (t, t), 1)
    return jnp.where(keep, ws_ref[g], 0.0).astype(BF16), keep


def _gmlp_gate_rows(z_ref, lg_ref, lb_ref, e):
    z = z_ref[...].astype(F32)
    cdf = _normal_cdf(z)
    gz = z * cdf
    u, v = gz[:, :e], gz[:, e:]
    mu = jnp.mean(v, axis=-1, keepdims=True)
    xc = v - mu
    rs = lax.rsqrt(jnp.mean(xc * xc, axis=-1, keepdims=True) + LN_EPS)
    vhat = xc * rs
    return (z, cdf), u, vhat, rs, vhat * lg_ref[...] + lb_ref[...]


def gmlp_fwd(name, z, ln_g, ln_b, w_s, bias):
    s, e2 = z.shape
    e = e2 // 2
    eg = e // GMLP_GROUPS

    def body(z_ref, lg_ref, lb_ref, ws_ref, b_ref, o_ref):
        _, u, _, _, vln = _gmlp_gate_rows(z_ref, lg_ref, lb_ref, e)
        vb = vln.astype(BF16)
        for g in range(GMLP_GROUPS):
            cols = slice(g * eg, (g + 1) * eg)
            wm, _ = _causal_weights(ws_ref, g)
            f = jnp.dot(wm, vb[:, cols], preferred_element_type=F32) + b_ref[:, cols]
            o_ref[:, cols] = (u[:, cols] * f).astype(BF16)

    full = lambda shape: pl.BlockSpec(shape, lambda i: (0,) * len(shape))
    return pl.pallas_call(
        body, name=name, grid=(s // CHUNK,),
        in_specs=[pl.BlockSpec((CHUNK, e2), lambda i: (i, 0)), full((1, e)), full((1, e)),
                  full((GMLP_GROUPS, CHUNK, CHUNK)), full((CHUNK, e))],
        out_specs=pl.BlockSpec((CHUNK, e), lambda i: (i, 0)), out_shape=jax.ShapeDtypeStruct((s, e), BF16),
        compiler_params=_params(dimension_semantics=("arbitrary",)),
    )(z, ln_g.reshape(1, e), ln_b.reshape(1, e), w_s, bias)


def gmlp_bwd(name, z, dp, ln_g, ln_b, w_s, bias):
    s, e2 = z.shape
    e = e2 // 2
    eg = e // GMLP_GROUPS
    t = CHUNK

    def body(z_ref, dp_ref, lg_ref, lb_ref, ws_ref, b_ref, dz_ref, dws_ref, dbs_ref, dlg_ref, dlb_ref):
        first = pl.program_id(0) == 0
        (zf, cdf), u, vhat, rs, vln = _gmlp_gate_rows(z_ref, lg_ref, lb_ref, e)
        vb = vln.astype(BF16)
        dp = dp_ref[...].astype(F32)
        lane = lax.broadcasted_iota(jnp.int32, (t, LANE), 1)
        dbs = jnp.zeros((t, LANE), F32)
        dvln_parts = []
        for g in range(GMLP_GROUPS):
            cols = slice(g * eg, (g + 1) * eg)
            wm, keep = _causal_weights(ws_ref, g)
            f = jnp.dot(wm, vb[:, cols], preferred_element_type=F32) + b_ref[:, cols]
            dz_ref[:, cols] = (dp[:, cols] * f * _gelu_grad(zf[:, cols], cdf[:, cols])).astype(BF16)
            df = dp[:, cols] * u[:, cols]
            dfb = df.astype(BF16)
            dbs = dbs + jnp.where(lane == g, jnp.sum(df, axis=-1, keepdims=True), 0.0)
            dw = lax.dot_general(dfb, vb[:, cols], (((1,), (1,)), ((), ())), preferred_element_type=F32)
            dw = jnp.where(keep, dw, 0.0)

            @pl.when(first)
            def _():
                dws_ref[g] = dw

            @pl.when(jnp.logical_not(first))
            def _():
                dws_ref[g] += dw

            dvln_parts.append(lax.dot_general(wm, dfb, (((0,), (0,)), ((), ())), preferred_element_type=F32))
        dvln = jnp.concatenate(dvln_parts, axis=-1)
        dvhat = dvln * lg_ref[...]
        dv = rs * (dvhat - jnp.mean(dvhat, axis=-1, keepdims=True)
                   - vhat * jnp.mean(dvhat * vhat, axis=-1, keepdims=True))
        dz_ref[:, e:] = (dv * _gelu_grad(zf[:, e:], cdf[:, e:])).astype(BF16)
        dlg = jnp.sum(dvln * vhat, axis=0, keepdims=True)
        dlb = jnp.sum(dvln, axis=0, keepdims=True)

        @pl.when(first)
        def _():
            dbs_ref[...] = dbs
            dlg_ref[...] = dlg
            dlb_ref[...] = dlb

        @pl.when(jnp.logical_not(first))
        def _():
            dbs_ref[...] += dbs
            dlg_ref[...] += dlg
            dlb_ref[...] += dlb

    full = lambda shape: pl.BlockSpec(shape, lambda i: (0,) * len(shape))
    return pl.pallas_call(
        body, name=name, grid=(s // t,),
        in_specs=[pl.BlockSpec((t, e2), lambda i: (i, 0)), pl.BlockSpec((t, e), lambda i: (i, 0)), full((1, e)),
                  full((1, e)), full((GMLP_GROUPS, t, t)), full((t, e))],
        out_specs=[pl.BlockSpec((t, e2), lambda i: (i, 0)), full((GMLP_GROUPS, t, t)), full((t, LANE)), full((1, e)),
                   full((1, e))],
        out_shape=[jax.ShapeDtypeStruct((s, e2), BF16), jax.ShapeDtypeStruct((GMLP_GROUPS, t, t), F32),
                   jax.ShapeDtypeStruct((t, LANE), F32), jax.ShapeDtypeStruct((1, e), F32),
                   jax.ShapeDtypeStruct((1, e), F32)],
        compiler_params=_params(dimension_semantics=("arbitrary",)),
    )(z, dp, ln_g.reshape(1, e), ln_b.reshape(1, e), w_s, bias)


EDGE = 16


def _shift_down(zc, prev, k):
    tm = zc.shape[0]
    row = lax.broadcasted_iota(jnp.int32, (tm, 1), 0)
    out = pltpu.roll(zc, k, 0)
    for j in range(k):
        out = jnp.where(row == j, prev[EDGE - k + j:EDGE - k + j + 1, :], out)
    return out


def _shift_up(dc, nxt, k):
    tm = dc.shape[0]
    row = lax.broadcasted_iota(jnp.int32, (tm, 1), 0)
    out = pltpu.roll(dc, tm - k, 0)
    for j in range(k):
        out = jnp.where(row == tm - k + j, nxt[j:j + 1, :], out)
    return out


def conv_fwd(name, bcv, cw):
    s, d3 = bcv.shape
    d = d3 // 3
    tm = _row_tile(s, d * 4, 2 * 2**20)
    per = tm // EDGE

    def body(b_ref, c_ref, v_ref, cp_ref, vp_ref, w_ref, o_ref):
        i = pl.program_id(0)
        zc = c_ref[...].astype(F32) * v_ref[...].astype(F32)
        prev = jnp.where(i > 0, cp_ref[...].astype(F32) * vp_ref[...].astype(F32), 0.0)
        conv = w_ref[2:3, :] * zc + w_ref[1:2, :] * _shift_down(zc, prev, 1) + w_ref[0:1, :] * _shift_down(zc, prev, 2)
        o_ref[...] = (b_ref[...].astype(F32) * conv).astype(BF16)

    blk = lambda col: pl.BlockSpec((tm, d), lambda i: (i, col))
    edge = lambda col: pl.BlockSpec((EDGE, d), lambda i: (jnp.maximum(i * per - 1, 0), col))
    return pl.pallas_call(
        body, name=name, grid=(s // tm,),
        in_specs=[blk(0), blk(1), blk(2), edge(1), edge(2), pl.BlockSpec((3, d), lambda i: (0, 0))],
        out_specs=pl.BlockSpec((tm, d), lambda i: (i, 0)), out_shape=jax.ShapeDtypeStruct((s, d), BF16),
        compiler_params=_params(dimension_semantics=("arbitrary",)),
    )(bcv, bcv, bcv, bcv, bcv, cw)


def conv_bwd(name, bcv, dq, cw):
    s, d3 = bcv.shape
    d = d3 // 3
    tm = _row_tile(s, d * 4, 2**20)
    per = tm // EDGE
    n_tiles = s // tm
    last_edge = s // EDGE - 1

    def body(b_ref, c_ref, v_ref, cp_ref, vp_ref, bn_ref, dq_ref, dqn_ref, w_ref, o_ref, dw_ref):
        i = pl.program_id(0)
        b = b_ref[...].astype(F32)
        c = c_ref[...].astype(F32)
        v = v_ref[...].astype(F32)
        dq = dq_ref[...].astype(F32)
        zc = c * v
        prev = jnp.where(i > 0, cp_ref[...].astype(F32) * vp_ref[...].astype(F32), 0.0)
        z1 = _shift_down(zc, prev, 1)
        z2 = _shift_down(zc, prev, 2)
        w0, w1, w2 = w_ref[0:1, :], w_ref[1:2, :], w_ref[2:3, :]
        conv = w2 * zc + w1 * z1 + w0 * z2
        dconv = dq * b
        nxt = jnp.where(i < n_tiles - 1, dqn_ref[...].astype(F32) * bn_ref[...].astype(F32), 0.0)
        dz = w2 * dconv + w1 * _shift_up(dconv, nxt, 1) + w0 * _shift_up(dconv, nxt, 2)
        o_ref[:, :d] = (dq * conv).astype(BF16)
        o_ref[:, d:2 * d] = (dz * v).astype(BF16)
        o_ref[:, 2 * d:] = (dz * c).astype(BF16)
        dw = jnp.concatenate([jnp.sum(dconv * z2, axis=0, keepdims=True), jnp.sum(dconv * z1, axis=0, keepdims=True),
                              jnp.sum(dconv * zc, axis=0, keepdims=True), jnp.zeros((5, d), F32)], axis=0)

        @pl.when(i == 0)
        def _():
            dw_ref[...] = dw

        @pl.when(i > 0)
        def _():
            dw_ref[...] += dw

    blk = lambda col: pl.BlockSpec((tm, d), lambda i: (i, col))
    before = lambda col: pl.BlockSpec((EDGE, d), lambda i: (jnp.maximum(i * per - 1, 0), col))
    after = lambda col: pl.BlockSpec((EDGE, d), lambda i: (jnp.minimum((i + 1) * per, last_edge), col))
    return pl.pallas_call(
        body, name=name, grid=(n_tiles,),
        in_specs=[blk(0), blk(1), blk(2), before(1), before(2), after(0), blk(0), after(0),
                  pl.BlockSpec((3, d), lambda i: (0, 0))],
        out_specs=[pl.BlockSpec((tm, d3), lambda i: (i, 0)), pl.BlockSpec((8, d), lambda i: (0, 0))],
        out_shape=[jax.ShapeDtypeStruct((s, d3), BF16), jax.ShapeDtypeStruct((8, d), F32)],
        compiler_params=_params(dimension_semantics=("arbitrary",)),
    )(bcv, bcv, bcv, bcv, bcv, bcv, dq, dq, cw)


def _attn_probs(qh, kh, scale):
    sc = lax.dot_general(qh, kh, (((1,), (1,)), ((), ())), preferred_element_type=F32) * scale
    ex = jnp.exp(sc - jnp.max(sc, axis=-1, keepdims=True))
    return ex / jnp.sum(ex, axis=-1, keepdims=True)


def attn_fwd(name, q, kv):
    s, d = q.shape
    mlen = kv.shape[0]
    dh = d // XATTN_HEADS
    scale = dh ** -0.5
    tm = _row_tile(s, d * 4, 2 * 2**20)

    def body(q_ref, kv_ref, o_ref):
        for h in range(XATTN_HEADS):
            cols = slice(h * dh, (h + 1) * dh)
            p = _attn_probs(q_ref[:, cols], kv_ref[:, cols], scale)
            o_ref[:, cols] = jnp.dot(p.astype(BF16), kv_ref[:, d + h * dh:d + (h + 1) * dh],
                                     preferred_element_type=F32).astype(BF16)

    return pl.pallas_call(
        body, name=name, grid=(s // tm,),
        in_specs=[pl.BlockSpec((tm, d), lambda i: (i, 0)), pl.BlockSpec((mlen, 2 * d), lambda i: (0, 0))],
        out_specs=pl.BlockSpec((tm, d), lambda i: (i, 0)), out_shape=jax.ShapeDtypeStruct((s, d), BF16),
        compiler_params=_params(dimension_semantics=("arbitrary",)),
    )(q, kv)


def attn_bwd(name, q, kv, do):
    s, d = q.shape
    mlen = kv.shape[0]
    dh = d // XATTN_HEADS
    scale = dh ** -0.5
    tm = _row_tile(s, d * 4, 2 * 2**20)

    def body(q_ref, kv_ref, do_ref, dq_ref, dkv_ref):
        first = pl.program_id(0) == 0
        for h in range(XATTN_HEADS):
            cols = slice(h * dh, (h + 1) * dh)
            vcols = slice(d + h * dh, d + (h + 1) * dh)
            qh, kh, vh, doh = q_ref[:, cols], kv_ref[:, cols], kv_ref[:, vcols], do_ref[:, cols]
            p = _attn_probs(qh, kh, scale)
            dp = lax.dot_general(doh, vh, (((1,), (1,)), ((), ())), preferred_element_type=F32)
            ds = (p * (dp - jnp.sum(dp * p, axis=-1, keepdims=True)) * scale).astype(BF16)
            dq_ref[:, cols] = jnp.dot(ds, kh, preferred_element_type=F32).astype(BF16)
            dk = lax.dot_general(ds, qh, (((0,), (0,)), ((), ())), preferred_element_type=F32)
            dv = lax.dot_general(p.astype(BF16), doh, (((0,), (0,)), ((), ())), preferred_element_type=F32)

            @pl.when(first)
            def _():
                dkv_ref[:, cols] = dk
                dkv_ref[:, vcols] = dv

            @pl.when(jnp.logical_not(first))
            def _():
                dkv_ref[:, cols] += dk
                dkv_ref[:, vcols] += dv

    row = pl.BlockSpec((tm, d), lambda i: (i, 0))
    whole = pl.BlockSpec((mlen, 2 * d), lambda i: (0, 0))
    return pl.pallas_call(
        body, name=name, grid=(s // tm,), in_specs=[row, whole, row], out_specs=[row, whole],
        out_shape=[jax.ShapeDtypeStruct((s, d), BF16), jax.ShapeDtypeStruct((mlen, 2 * d), F32)],
        compiler_params=_params(dimension_semantics=("arbitrary",)),
    )(q, kv, do)


def _as_rows(a):
    if a.ndim >= 2 and a.shape[-1] % LANE == 0:
        return a.reshape(-1, a.shape[-1])
    return a.reshape(-1, LANE) if a.size % LANE == 0 else a.reshape(1, -1)


def add_halves(name, dw, other, core):
    p, r, c = dw.shape
    h = r // 2
    th = _row_tile(h, c * 2, 2 * 2**20)

    def body(core_ref, a_ref, b_ref, o_ref):
        o_ref[...] = (a_ref[...].astype(F32) + b_ref[...].astype(F32)).astype(BF16)

    grid_spec = pltpu.PrefetchScalarGridSpec(
        num_scalar_prefetch=1, grid=(p, h // th),
        in_specs=[pl.BlockSpec((None, None, th, c), lambda pi, i, core_ref: (pi, core_ref[0], i, 0)),
                  pl.BlockSpec((None, th, c), lambda pi, i, core_ref: (pi, i, 0))],
        out_specs=pl.BlockSpec((None, th, c), lambda pi, i, core_ref: (pi, i, 0)))
    return pl.pallas_call(
        body, name=name, grid_spec=grid_spec, out_shape=jax.ShapeDtypeStruct((p, h, c), BF16),
        compiler_params=_params(dimension_semantics=("arbitrary", "arbitrary")),
    )(core, dw.reshape(p, 2, h, c), other)


def sum_leading(name, parts):
    n, r, c = parts.shape
    tr = _row_tile(r, c * 4 * 2, 2 * 2**20)

    def body(p_ref, o_ref):
        acc = p_ref[0].astype(F32)
        for k in range(1, n):
            acc = acc + p_ref[k].astype(F32)
        o_ref[...] = acc

    return pl.pallas_call(
        body, name=name, grid=(r // tr,), in_specs=[pl.BlockSpec((n, tr, c), lambda i: (0, i, 0))],
        out_specs=pl.BlockSpec((tr, c), lambda i: (i, 0)), out_shape=jax.ShapeDtypeStruct((r, c), F32),
        compiler_params=_params(dimension_semantics=("arbitrary",)),
    )(parts)


def _adamw_rows(w, g, m, v):
    m = ADAM_B1 * m + (1.0 - ADAM_B1) * g
    v = ADAM_B2 * v + (1.0 - ADAM_B2) * (g * g)
    m_hat = m / (1.0 - ADAM_B1 ** ADAM_STEP)
    v_hat = v / (1.0 - ADAM_B2 ** ADAM_STEP)
    delta = -ADAM_LR * (m_hat / (jnp.sqrt(v_hat) + ADAM_EPS) + ADAM_WD * w)
    return delta, m, v


def adamw_layer(name, w, m, v, g, layer, carried):
    nl, r, c = w.shape
    tr = _row_tile(r, c * 4, 2**20)
    n_carried = 4 if carried is not None else 0

    def body(*refs):
        w_ref, m_ref, v_ref, g_ref = refs[:4]
        go_ref, d_ref, mo_ref, vo_ref = refs[4 + n_carried:]
        g = g_ref[...]
        delta, m_new, v_new = _adamw_rows(w_ref[...], g, m_ref[...], v_ref[...])
        go_ref[...] = g
        d_ref[...] = delta
        mo_ref[...] = m_new
        vo_ref[...] = v_new

    stacked = pl.BlockSpec((None, tr, c), lambda i: (layer, i, 0))
    in_specs = [stacked, stacked, stacked, pl.BlockSpec((tr, c), lambda i: (i, 0))]
    in_specs += [pl.BlockSpec(memory_space=pl.ANY)] * n_carried
    shape = jax.ShapeDtypeStruct((nl, r, c), F32)
    return pl.pallas_call(
        body, name=name, grid=(r // tr,), in_specs=in_specs, out_specs=[stacked] * 4, out_shape=[shape] * 4,
        input_output_aliases={4 + k: k for k in range(n_carried)},
        compiler_params=_params(dimension_semantics=("arbitrary",)),
    )(w, m, v, g, *(carried or ()))


def adamw_flat(name, w, m, v, g):
    r, c = w.shape

    def body(w_ref, m_ref, v_ref, g_ref, d_ref, mo_ref, vo_ref):
        delta, m_new, v_new = _adamw_rows(w_ref[...], g_ref[...], m_ref[...], v_ref[...])
        d_ref[...] = delta
        mo_ref[...] = m_new
        vo_ref[...] = v_new

    shape = jax.ShapeDtypeStruct((r, c), F32)
    return pl.pallas_call(body, name=name, out_shape=[shape] * 3, compiler_params=_params())(w, m, v, g)


def cast_place(name, w, layer, place):
    nl, r, c = w.shape
    tr = _row_tile(r, c * 4, 2 * 2**20)

    def body(x_ref, y_ref, c_ref, w_ref, o_ref):
        o_ref[...] = w_ref[...].astype(BF16)

    grid_spec = pltpu.PrefetchScalarGridSpec(
        num_scalar_prefetch=3, grid=(r // tr,),
        in_specs=[pl.BlockSpec((None, tr, c), lambda i, x_ref, y_ref, c_ref: (layer, i, 0))],
        out_specs=pl.BlockSpec((None, tr, c), lambda i, x_ref, y_ref, c_ref: (2 * x_ref[0] + y_ref[0], i, 0)))
    return pl.pallas_call(
        body, name=name, grid_spec=grid_spec, out_shape=jax.ShapeDtypeStruct((N_CHIPS, r, c), BF16),
        compiler_params=_params(dimension_semantics=("arbitrary",)),
    )(*place, w)


def reduce_sum4(name, own, landed, place):
    p, h, c = own.shape
    tr = _row_tile(h, c * 4, 2**20)

    def body(x_ref, y_ref, c_ref, t_ref, y1_ref, y2_ref, y3_ref, o_ref):
        acc = t_ref[...].astype(F32)
        for part_ref in (y1_ref, y2_ref, y3_ref):
            acc = acc + part_ref[...].astype(F32)
        o_ref[...] = acc

    def panel(fx, fy):
        return pl.BlockSpec((None, tr, c), lambda i, x_ref, y_ref, c_ref: (
            2 * (1 - x_ref[0] if fx else x_ref[0]) + (1 - y_ref[0] if fy else y_ref[0]), i, 0))

    grid_spec = pltpu.PrefetchScalarGridSpec(
        num_scalar_prefetch=3, grid=(h // tr,),
        in_specs=[panel(0, 0), panel(1, 0), panel(0, 1), panel(1, 1)],
        out_specs=pl.BlockSpec((None, tr, c), lambda i, x_ref, y_ref, c_ref: (c_ref[0], i, 0)))
    return pl.pallas_call(
        body, name=name, grid_spec=grid_spec, out_shape=jax.ShapeDtypeStruct((2, h, c), F32),
        compiler_params=_params(dimension_semantics=("arbitrary",)),
    )(*place, own, landed, landed, landed)


def run_exchange(name, exchange):
    n_in, n_out = len(exchange.inputs), len(exchange.out_shapes)

    def body(*refs):
        ins, outs, sems = refs[:n_in], refs[n_in:n_in + n_out], refs[n_in + n_out:]
        exchange.start(ins, outs, sems)
        exchange.finish(ins, outs, sems)

    return pl.pallas_call(
        body, name=name, in_specs=[ANY] * n_in, out_specs=[ANY] * n_out, out_shape=list(exchange.out_shapes),
        scratch_shapes=[pltpu.SemaphoreType.DMA((exchange.n_sems,)), pltpu.SemaphoreType.DMA((exchange.n_sems,))],
        input_output_aliases=dict(exchange.aliases),
    )(*exchange.inputs)


def _row_halves(ref, c):
    h = ref.shape[1] // 2
    return pl.ds(pl.multiple_of(c * h, 16), h), pl.ds(pl.multiple_of((1 - c) * h, 16), h)


def gather_exchange(fulls):
    n = len(fulls)

    def start(ins, outs, sems):
        x, y, c, mine, chips = _place()
        for a in range(n):
            rows = outs[a].at[mine, _row_halves(outs[a], c)[0]]
            for j, chip in enumerate(chips):
                _remote(rows, rows, sems, 6 * a + j, (*chip, c)).start()

    def finish(ins, outs, sems):
        x, y, c, mine, chips = _place()
        sibling = (x, y, 1 - c)
        for a in range(n):
            half = _row_halves(outs[a], c)[0]
            for j, chip in enumerate(chips):
                rows = outs[a].at[2 * chip[0] + chip[1], half]
                _remote(rows, rows, sems, 6 * a + j, sibling).wait_recv()
                _remote(rows, rows, sems, 6 * a + 3 + j, sibling).start()
        for a in range(n):
            half, other = _row_halves(outs[a], c)
            for j, chip in enumerate(chips):
                rows = outs[a].at[2 * chip[0] + chip[1], other]
                _remote(rows, rows, sems, 6 * a + 3 + j, sibling).wait_recv()
            for j, chip in enumerate(chips):
                rows = outs[a].at[mine, half]
                _remote(rows, rows, sems, 6 * a + j, (*chip, c)).wait_send()
                rows = outs[a].at[2 * chip[0] + chip[1], half]
                _remote(rows, rows, sems, 6 * a + 3 + j, sibling).wait_send()

    return Exchange(list(fulls), [jax.ShapeDtypeStruct(f.shape, f.dtype) for f in fulls], {a: a for a in range(n)},
                    6 * n, start, finish)


def swap_exchange(grads):
    n = len(grads)

    def copies(ins, outs, sems):
        x, y, c, _, _ = _place()
        return [_remote(ins[a].at[:, _row_halves(ins[a], c)[1]], outs[a], sems, a, (x, y, 1 - c)) for a in range(n)]

    def start(ins, outs, sems):
        for cp in copies(ins, outs, sems):
            cp.start()

    def finish(ins, outs, sems):
        for cp in copies(ins, outs, sems):
            cp.wait()

    shapes = [jax.ShapeDtypeStruct((g.shape[0], g.shape[1] // 2, g.shape[2]), g.dtype) for g in grads]
    return Exchange(list(grads), shapes, {}, n, start, finish)


def scatter_exchange(parts):
    n = len(parts)

    def sends(ins, outs, sems):
        x, y, c, mine, chips = _place()
        return [_remote(ins[a].at[2 * chip[0] + chip[1]], outs[a].at[mine], sems, 3 * a + j, (*chip, c))
                for a in range(n) for j, chip in enumerate(chips)]

    def start(ins, outs, sems):
        for cp in sends(ins, outs, sems):
            cp.start()

    def finish(ins, outs, sems):
        x, y, c, mine, chips = _place()
        for a in range(n):
            for j, chip in enumerate(chips):
                landing = outs[a].at[2 * chip[0] + chip[1]]
                _remote(landing, landing, sems, 3 * a + j, (*chip, c)).wait_recv()
        for cp in sends(ins, outs, sems):
            cp.wait_send()

    return Exchange(list(parts), [jax.ShapeDtypeStruct(g.shape, g.dtype) for g in parts], {}, 3 * n, start, finish)


def join_exchange(halves):
    n = len(halves)

    def start(ins, outs, sems):
        x, y, c, _, _ = _place()
        for a in range(n):
            _remote(outs[a].at[c], outs[a].at[c], sems, a, (x, y, 1 - c)).start()

    def finish(ins, outs, sems):
        x, y, c, _, _ = _place()
        for a in range(n):
            _remote(outs[a].at[1 - c], outs[a].at[1 - c], sems, a, (x, y, 1 - c)).wait_recv()
        for a in range(n):
            _remote(outs[a].at[c], outs[a].at[c], sems, a, (x, y, 1 - c)).wait_send()

    return Exchange(list(halves), [jax.ShapeDtypeStruct(g.shape, g.dtype) for g in halves], {a: a for a in range(n)},
                    n, start, finish)


def gather_all(name, rows):
    def body(in_ref, out_ref, send_sems, recv_sems, local_sem):
        sems = (send_sems, recv_sems)
        x, y, c, _, _ = _place()
        me = 4 * x + 2 * y + c
        local = pltpu.make_async_copy(in_ref, out_ref.at[me], local_sem)
        local.start()
        peers = [(1 - x if k & 4 else x, 1 - y if k & 2 else y, 1 - c if k & 1 else c) for k in range(1, N_DEV)]
        sent = []
        for k, peer in enumerate(peers):
            cp = _remote(in_ref, out_ref.at[me], sems, k, peer)
            cp.start()
            sent.append(cp)
        for k, peer in enumerate(peers):
            landing = out_ref.at[4 * peer[0] + 2 * peer[1] + peer[2]]
            _remote(landing, landing, sems, k, peer).wait_recv()
        for cp in sent:
            cp.wait_send()
        local.wait()

    return pl.pallas_call(
        body, name=name, in_specs=[ANY], out_specs=ANY,
        out_shape=jax.ShapeDtypeStruct((N_DEV,) + rows.shape, rows.dtype),
        scratch_shapes=[pltpu.SemaphoreType.DMA((N_DEV - 1,)), pltpu.SemaphoreType.DMA((N_DEV - 1,)),
                        pltpu.SemaphoreType.DMA],
    )(rows)


class _Step:
    def __init__(self, p):
        self.p = p
        xi, yi, ci = lax.axis_index("x"), lax.axis_index("y"), lax.axis_index("c")
        self.chip = 2 * xi + yi
        self.place_refs = tuple(v.astype(jnp.int32).reshape(1) for v in (xi, yi, ci))
        self.core_ref = self.place_refs[2]
        self.depth = p['ffn1_norm'].shape[0]
        self.placed, self.w, self.big_g = {}, {}, {}
        self.waiting_joins = []
        self.waiting_scatter = None

    def block_keys(self, tag, l):
        if l >= self.depth:
            return []
        mixer = ['gmlp_w_in', 'gmlp_w_out'] if l % 2 == 0 else ['conv_w_in', 'conv_w_out']
        names = {"ffn1": ['ffn1_w13', 'ffn1_w2'], "mix": mixer, "xattn": ['xattn_wq', 'xattn_wkv', 'xattn_wo'],
                 "ffn2": ['ffn2_w13', 'ffn2_w2']}[tag]
        return [(n, l // 2 if tag == "mix" else l) for n in names]

    def place(self, keys):
        for n, idx in keys:
            self.placed[(n, idx)] = cast_place(f"place_{n}{idx}", self.p[n], idx, self.place_refs)

    def gather_alone(self, name, keys):
        got = run_exchange(name, gather_exchange([self.placed[k] for k in keys]))
        self.w.update(zip(keys, got, strict=True))

    def carrying_gather(self, mm, keys, *args, **kw):
        keys = [k for k in keys if k in self.placed]
        if not keys:
            return mm(*args, **kw)
        out, got = mm(*args, exchange=gather_exchange([self.placed[k] for k in keys]), **kw)
        self.w.update(zip(keys, got, strict=True))
        return out

    def reduce_begin(self, tag, keys, dws):
        theirs = run_exchange(tag + "_swap", swap_exchange(dws))
        parts = [add_halves(f"{tag}_add{i}", dw, t, self.core_ref) for i, (dw, t) in enumerate(zip(dws, theirs, strict=True))]
        assert self.waiting_scatter is None
        self.waiting_scatter = (tag, keys, parts)

    def carrying_scatter(self, mm, *args, **kw):
        tag, keys, parts = self.waiting_scatter
        self.waiting_scatter = None
        out, landed = mm(*args, exchange=scatter_exchange(parts), **kw)
        halves = [reduce_sum4(f"{tag}_sum{i}", t, y, self.place_refs) for i, (t, y) in enumerate(zip(parts, landed, strict=True))]
        self.waiting_joins += list(zip(keys, halves, strict=True))
        return out

    def take_joined(self, keys, joined):
        for k, g in zip(keys, joined, strict=True):
            self.big_g[k] = g.reshape(-1, g.shape[-1])

    def carrying_joins(self, mm, *args, **kw):
        if not self.waiting_joins:
            return mm(*args, **kw)
        keys, halves = zip(*self.waiting_joins, strict=True)
        self.waiting_joins = []
        out, joined = mm(*args, exchange=join_exchange(list(halves)), **kw)
        self.take_joined(keys, joined)
        return out

    def joins_alone(self, name):
        keys, halves = zip(*self.waiting_joins, strict=True)
        self.waiting_joins = []
        self.take_joined(keys, run_exchange(name, join_exchange(list(halves))))

    def ffn_fwd(self, tag, l, x, gain, carry13, carry2):
        name = f"l{l}_{tag}"
        h = rms_fwd(name + "_norm", x, gain)
        gate, up, act = self.carrying_gather(mm_swiglu, carry13, name + "_w13", h, self.w[(tag + '_w13', l)])
        out = self.carrying_gather(mm_nn, carry2, name + "_w2", act, self.w[(tag + '_w2', l)], 'row', F32, res=x,
                                   scale=0.5)
        return out, (x, h, gate, up, act)

    def ffn_bwd(self, tag, l, dx, dxb, saved, gain):
        w13, w2 = self.w[(tag + '_w13', l)], self.w[(tag + '_w2', l)]
        name = f"l{l}_{tag}"
        x, h, gate, up, act = saved
        d_gate, d_up = self.carrying_joins(mm_dswiglu, name + "_dact", dxb, w2, gate, up, 0.5)
        d_w2 = mm_tn(name + "_dw2", act, dxb, 'row', scale=0.5)
        half = N_CHIPS // 2
        d_w13 = mm_tn(name + "_dw13g", h, d_gate, 'col', panels=(0, half))
        d_w13 = mm_tn(name + "_dw13u", h, d_up, 'col', panels=(half, half), into=d_w13)
        self.reduce_begin(name, [(tag + '_w13', l), (tag + '_w2', l)], [d_w13, d_w2])
        dh = self.carrying_scatter(mm_nt, name + "_dh", d_gate, w13, 'col', F32, a_hi=d_up)
        return rms_bwd(name + "_dnorm", x, gain, dh, dx)


def kernel(x, mem, ffn1_norm, ffn1_w13, ffn1_w2, mix_norm, gmlp_w_in, gmlp_ln_g, gmlp_ln_b, gmlp_w_s, gmlp_b_s, gmlp_w_out, conv_w_in, conv_w, conv_w_out, xattn_norm, mem_norm, xattn_wq, xattn_wkv, xattn_wo, ffn2_norm, ffn2_w13, ffn2_w2, final_norm, loss_target, m_ffn1_norm, m_ffn1_w13, m_ffn1_w2, m_mix_norm, m_gmlp_w_in, m_gmlp_ln_g, m_gmlp_ln_b, m_gmlp_w_s, m_gmlp_b_s, m_gmlp_w_out, m_conv_w_in, m_conv_w, m_conv_w_out, m_xattn_norm, m_mem_norm, m_xattn_wq, m_xattn_wkv, m_xattn_wo, m_ffn2_norm, m_ffn2_w13, m_ffn2_w2, m_final_norm, v_ffn1_norm, v_ffn1_w13, v_ffn1_w2, v_mix_norm, v_gmlp_w_in, v_gmlp_ln_g, v_gmlp_ln_b, v_gmlp_w_s, v_gmlp_b_s, v_gmlp_w_out, v_conv_w_in, v_conv_w, v_conv_w_out, v_xattn_norm, v_mem_norm, v_xattn_wq, v_xattn_wkv, v_xattn_wo, v_ffn2_norm, v_ffn2_w13, v_ffn2_w2, v_final_norm):
    return _step(dict(locals()))


def _step(p):
    assert sorted(p) == sorted(ARG_NAMES)
    st = _Step(p)
    x = p['x'][0]
    mem = p['mem'][0]
    target = p['loss_target'][0]
    s, d = x.shape
    depth = st.depth

    for l in range(depth):
        for tag in ("ffn1", "mix", "xattn", "ffn2"):
            st.place(st.block_keys(tag, l))
    st.gather_alone("gather_first", [('ffn1_w13', 0)])

    cw_local = p['conv_w']
    n_conv, cwid, dq4 = cw_local.shape
    cw_rows = jnp.pad(cw_local.reshape(-1, LANE), ((0, (-cw_local.size // LANE) % 8), (0, 0)))
    cw_all = gather_all("gather_conv_w", cw_rows)[0::2, :cw_local.size // LANE]
    conv_w_full = cw_all.reshape(N_CHIPS, n_conv, cwid, dq4).transpose(1, 2, 0, 3).reshape(n_conv, cwid, N_CHIPS * dq4)

    saved = []
    for l in range(depth):
        j = l // 2
        rec = {}
        first_w2 = [('ffn1_w2', 0)] if l == 0 else []
        x, rec['ffn1'] = st.ffn_fwd("ffn1", l, x, p['ffn1_norm'][l],
                                    first_w2 + st.block_keys("mix", l) + st.block_keys("xattn", l), [('ffn2_w13', l)])
        h = rms_fwd(f"l{l}_mix_norm", x, p['mix_norm'][l])
        if l % 2 == 0:
            e = p['gmlp_ln_g'].shape[-1]
            bias = jnp.repeat(p['gmlp_b_s'][j].T, e // GMLP_GROUPS, axis=1)
            z = st.carrying_gather(mm_nn, [('ffn2_w2', l)], f"l{l}_gmlp_in", h, st.w[('gmlp_w_in', j)], 'col', BF16)
            gate = gmlp_fwd(f"l{l}_gmlp_gate", z, p['gmlp_ln_g'][j], p['gmlp_ln_b'][j], p['gmlp_w_s'][j], bias)
            x_new = mm_nn(f"l{l}_gmlp_out", gate, st.w[('gmlp_w_out', j)], 'row', F32, res=x)
            rec['mix'] = (x, h, z, gate, bias)
        else:
            bcv = st.carrying_gather(mm_nn, [('ffn2_w2', l)], f"l{l}_conv_in", h, st.w[('conv_w_in', j)], 'col', BF16)
            gate = conv_fwd(f"l{l}_conv_gate", bcv, conv_w_full[j])
            x_new = mm_nn(f"l{l}_conv_out", gate, st.w[('conv_w_out', j)], 'row', F32, res=x)
            rec['mix'] = (x, h, bcv, gate)
        x = x_new
        hq = rms_fwd(f"l{l}_xattn_norm", x, p['xattn_norm'][l])
        q = mm_nn(f"l{l}_xattn_q", hq, st.w[('xattn_wq', l)], 'row', BF16)
        mem_n = rms_fwd(f"l{l}_mem_norm", mem, p['mem_norm'][l])
        kv = mm_nn(f"l{l}_xattn_kv", mem_n, st.w[('xattn_wkv', l)], 'col', BF16)
        o = attn_fwd(f"l{l}_xattn_core", q, kv)
        x_new = mm_nn(f"l{l}_xattn_o", o, st.w[('xattn_wo', l)], 'row', F32, res=x)
        rec['xattn'] = (x, hq, q, mem_n, kv, o)
        x = x_new
        x, rec['ffn2'] = st.ffn_fwd("ffn2", l, x, p['ffn2_norm'][l], st.block_keys("ffn1", l + 1), [])
        saved.append(rec)

    dx, dxb, d_final, loss_lanes = loss_head("loss_head", x, p['final_norm'], target)
    loss = lax.psum(0.5 * jnp.sum(loss_lanes) / d, ("x", "y", "c"))

    small = {n: [None] * p[n].shape[0] for n in ('ffn1_norm', 'mix_norm', 'xattn_norm', 'mem_norm', 'ffn2_norm',
                                                  'gmlp_ln_g', 'gmlp_ln_b', 'gmlp_w_s', 'gmlp_b_s', 'conv_w')}
    for l in reversed(range(depth)):
        j = l // 2
        rec = saved[l]
        dx, dxb, small['ffn2_norm'][l] = st.ffn_bwd("ffn2", l, dx, dxb, rec['ffn2'], p['ffn2_norm'][l])

        x_in, hq, q, mem_n, kv, o = rec['xattn']
        name = f"l{l}_xattn"
        do = st.carrying_joins(mm_nt, name + "_do", dxb, st.w[('xattn_wo', l)], 'row', BF16)
        d_wo = mm_tn(name + "_dwo", o, dxb, 'row')
        dq, dkv = attn_bwd(name + "_dcore", q, kv, do)
        d_wq = mm_tn(name + "_dwq", hq, dq, 'row')
        dkvb = dkv.astype(BF16)
        d_wkv = mm_tn(name + "_dwkv", mem_n, dkvb, 'col')
        st.reduce_begin(name, [('xattn_wq', l), ('xattn_wkv', l), ('xattn_wo', l)], [d_wq, d_wkv, d_wo])
        dh = mm_nt(name + "_dh", dq, st.w[('xattn_wq', l)], 'row', F32)
        dx, dxb, small['xattn_norm'][l] = rms_bwd(name + "_dnorm", x_in, p['xattn_norm'][l], dh, dx)
        dmem_n = mm_nt(name + "_dmem", dkvb, st.w[('xattn_wkv', l)], 'col', F32)
        small['mem_norm'][l] = rms_bwd(f"l{l}_mem_dnorm", mem, p['mem_norm'][l], dmem_n, None)[2]

        if l % 2 == 0:
            x_in, h, z, gate, bias = rec['mix']
            name = f"l{l}_gmlp"
            w_in, w_out = st.w[('gmlp_w_in', j)], st.w[('gmlp_w_out', j)]
            dgate = mm_nt(name + "_dgate", dxb, w_out, 'row', BF16)
            d_wout = mm_tn(name + "_dwout", gate, dxb, 'row')
            dmix, dws, dbs, dlg, dlb = gmlp_bwd(name + "_dgate_core", z, dgate, p['gmlp_ln_g'][j], p['gmlp_ln_b'][j],
                                                p['gmlp_w_s'][j], bias)
            small['gmlp_w_s'][j], small['gmlp_b_s'][j] = dws, dbs[:, :GMLP_GROUPS].T
            small['gmlp_ln_g'][j], small['gmlp_ln_b'][j] = dlg, dlb
            keys = [('gmlp_w_in', j), ('gmlp_w_out', j)]
        else:
            x_in, h, bcv, gate = rec['mix']
            name = f"l{l}_conv"
            w_in, w_out = st.w[('conv_w_in', j)], st.w[('conv_w_out', j)]
            dgate = mm_nt(name + "_dgate", dxb, w_out, 'row', BF16)
            d_wout = mm_tn(name + "_dwout", gate, dxb, 'row')
            dmix, dcw = conv_bwd(name + "_dgate_core", bcv, dgate, conv_w_full[j])
            small['conv_w'][j] = dcw[:cwid]
            keys = [('conv_w_in', j), ('conv_w_out', j)]
        d_win = st.carrying_scatter(mm_tn, name + "_dwin", h, dmix, 'col')
        st.reduce_begin(name, keys, [d_win, d_wout])
        dh = st.carrying_scatter(mm_nt, name + "_dh", dmix, w_in, 'col', F32)
        dx, dxb, small['mix_norm'][l] = rms_bwd(f"l{l}_mix_dnorm", x_in, p['mix_norm'][l], dh, dx)

        dx, dxb, small['ffn1_norm'][l] = st.ffn_bwd("ffn1", l, dx, dxb, rec['ffn1'], p['ffn1_norm'][l])
    st.joins_alone("join_last")

    small_names = ['ffn1_norm', 'mix_norm', 'xattn_norm', 'mem_norm', 'ffn2_norm', 'gmlp_ln_g', 'gmlp_ln_b', 'gmlp_w_s',
                   'gmlp_b_s', 'final_norm', 'conv_w']
    small_full = {n: jnp.stack([g.reshape(p[n].shape[1:]) for g in small[n]]) for n in small_names
                  if n not in ('final_norm', 'conv_w')}
    small_full['final_norm'] = d_final.reshape(p['final_norm'].shape)
    small_full['conv_w'] = jnp.stack(small['conv_w'])
    packed = jnp.concatenate([small_full[n].reshape(-1, LANE) for n in small_names], axis=0)
    total = sum_leading("small_sum", gather_all("small_gather", packed))
    small_g, at = {}, 0
    for n in small_names:
        rows = small_full[n].size // LANE
        small_g[n] = total[at:at + rows].reshape(small_full[n].shape)
        at += rows
    small_g['conv_w'] = lax.dynamic_slice_in_dim(small_g['conv_w'], st.chip * dq4, dq4, axis=2)

    grads, deltas, new_m, new_v = {}, {}, {}, {}
    for n in WEIGHTS:
        w, m, v = p[n], p['m_' + n], p['v_' + n]
        if n in BIG:
            carried = None
            for i in range(w.shape[0]):
                carried = adamw_layer(f"adamw_{n}{i}", w, m, v, st.big_g[(n, i)], i, carried)
            grads[n], deltas[n], new_m[n], new_v[n] = carried
        else:
            g = small_g[n]
            out = adamw_flat(f"adamw_{n}", _as_rows(w), _as_rows(m), _as_rows(v), _as_rows(g))
            grads[n] = g
            deltas[n], new_m[n], new_v[n] = (o.reshape(w.shape) for o in out)

    grad_x = dx.reshape(p['x'].shape)
    return (loss, grad_x, *[grads[n] for n in WEIGHTS], *[deltas[n] for n in WEIGHTS], *[new_m[n] for n in WEIGHTS],
            *[new_v[n] for n in WEIGHTS])
```

```python
from typing import Callable, NamedTuple

import jax
import jax.numpy as jnp
from jax import lax
from jax.experimental import pallas as pl
from jax.experimental.pallas import tpu as pltpu

F32 = jnp.float32
BF16 = jnp.bfloat16
MESH = pl.DeviceIdType.MESH

CHUNK = 128
GMLP_GROUPS = 8
XATTN_HEADS = 4
RMS_EPS = 1e-6
LN_EPS = 1e-5
ADAM_LR = 0.001
ADAM_B1 = 0.9
ADAM_B2 = 0.999
ADAM_EPS = 1e-08
ADAM_WD = 0.01
ADAM_STEP = 10

N_CHIPS = 4
N_DEV = 8

VMEM_LIMIT_BYTES = 58 * 2**20
VMEM_PLAN_BYTES = 48 * 2**20
LANE = 128
MXU_DIM = 256
MXU_FLOPS_PER_US = 996e6
HBM_BYTES_PER_US = 3.3e6
STEP_US = 0.35
ACC_US_PER_VREG = 0.58e-3

WEIGHTS = ['ffn1_norm', 'ffn1_w13', 'ffn1_w2', 'mix_norm', 'gmlp_w_in', 'gmlp_ln_g', 'gmlp_ln_b', 'gmlp_w_s',
           'gmlp_b_s', 'gmlp_w_out', 'conv_w_in', 'conv_w', 'conv_w_out', 'xattn_norm', 'mem_norm', 'xattn_wq',
           'xattn_wkv', 'xattn_wo', 'ffn2_norm', 'ffn2_w13', 'ffn2_w2', 'final_norm']
BIG = {'ffn1_w13': 'col', 'ffn1_w2': 'row', 'gmlp_w_in': 'col', 'gmlp_w_out': 'row', 'conv_w_in': 'col',
       'conv_w_out': 'row', 'xattn_wq': 'row', 'xattn_wkv': 'col', 'xattn_wo': 'row', 'ffn2_w13': 'col',
       'ffn2_w2': 'row'}
ARG_NAMES = (['x', 'mem'] + WEIGHTS + ['loss_target'] + ['m_' + n for n in WEIGHTS] + ['v_' + n for n in WEIGHTS])


def _params(**kw):
    return pltpu.CompilerParams(vmem_limit_bytes=VMEM_LIMIT_BYTES, **kw)


def _divisors(n, mult, cap):
    return [d for d in range(mult, min(n, cap) + 1, mult) if n % d == 0] or [n]


def _row_tile(rows, width_bytes, budget=4 * 2**20):
    best = None
    for d in _divisors(rows, 16, 1024):
        if d * width_bytes <= budget:
            best = d
    return best or _divisors(rows, 16, 1024)[0]


ANY = pl.BlockSpec(memory_space=pl.ANY)


class Exchange(NamedTuple):
    inputs: list
    out_shapes: list
    aliases: dict
    n_sems: int
    start: Callable
    finish: Callable


def _place():
    x, y, c = lax.axis_index("x"), lax.axis_index("y"), lax.axis_index("c")
    chips = [(1 - x, y), (x, 1 - y), (1 - x, 1 - y)]
    return x, y, c, 2 * x + y, chips


def _remote(src, dst, sems, k, device):
    return pltpu.make_async_remote_copy(src_ref=src, dst_ref=dst, send_sem=sems[0].at[k], recv_sem=sems[1].at[k],
                                        device_id=device, device_id_type=MESH)


def _mxu_fill(dim):
    return dim / (-(-dim // MXU_DIM) * MXU_DIM)


def _tile_time(flops, fill, traffic, steps, acc_vregs):
    return (max(flops / (MXU_FLOPS_PER_US * fill), traffic / HBM_BYTES_PER_US) + steps * STEP_US
            + steps * acc_vregs * ACC_US_PER_VREG)


def _plan_mm(m, n_tiles_of, k_tiles_of, n, k, a_item, o_item, has_res, a_arrays=1):
    best, best_cost = None, None
    for tm in _divisors(m, 16, 1024):
        for tn in n_tiles_of:
            for tk in k_tiles_of:
                ni, nj, nk = m // tm, n // tn, k // tk
                blocks = a_arrays * tm * tk * a_item + tk * tn * 2 + tm * tn * o_item + (tm * tn * 4 if has_res else 0)
                vmem = 2 * blocks + tm * tn * 4 * (2 if nk > 1 else 1)
                if vmem > VMEM_PLAN_BYTES:
                    continue
                traffic = nj * m * k * a_item + (k * n * 2 if nk == 1 else ni * k * n * 2)
                traffic += m * n * (o_item + (4 if has_res else 0))
                cost = _tile_time(2 * m * n * k, _mxu_fill(tk) * _mxu_fill(tn), traffic, ni * nj * nk,
                                  tm * tn // 1024 if nk > 1 else 0)
                if best_cost is None or cost < best_cost:
                    best, best_cost = (tm, tn, tk), cost
    assert best is not None, (m, n, k)
    return best


def _tiled_call(name, grid, operands, in_specs, out_shapes, out_specs, scratch, compute, exchange=None, aliases=None):
    n_reg, n_out, n_scr = len(operands), len(out_shapes), len(scratch)
    n_xin = len(exchange.inputs) if exchange else 0
    n_xout = len(exchange.out_shapes) if exchange else 0
    semantics = ("arbitrary",) * len(grid)

    def body(*refs):
        ins = refs[:n_reg]
        outs = refs[n_reg + n_xin:n_reg + n_xin + n_out]
        scr = refs[n_reg + n_xin + n_out + n_xout:n_reg + n_xin + n_out + n_xout + n_scr]
        if not exchange:
            compute(ins, outs, scr)
            return
        x_ins = refs[n_reg:n_reg + n_xin]
        x_outs = refs[n_reg + n_xin + n_out:n_reg + n_xin + n_out + n_xout]
        sems = refs[-2:]
        at_first, at_last = True, True
        for k, extent in enumerate(grid):
            at_first = jnp.logical_and(at_first, pl.program_id(k) == 0)
            at_last = jnp.logical_and(at_last, pl.program_id(k) == extent - 1)

        @pl.when(at_first)
        def _():
            exchange.start(x_ins, x_outs, sems)

        compute(ins, outs, scr)

        @pl.when(at_last)
        def _():
            exchange.finish(x_ins, x_outs, sems)

    if not exchange:
        return pl.pallas_call(
            body, name=name, grid=grid, in_specs=in_specs, out_specs=out_specs, out_shape=out_shapes,
            scratch_shapes=scratch, input_output_aliases=dict(aliases or {}),
            compiler_params=_params(dimension_semantics=semantics),
        )(*operands)
    assert not aliases
    sems = [pltpu.SemaphoreType.DMA((exchange.n_sems,)), pltpu.SemaphoreType.DMA((exchange.n_sems,))]
    got = pl.pallas_call(
        body, name=name, grid=grid, in_specs=in_specs + [ANY] * n_xin, out_specs=out_specs + [ANY] * n_xout,
        out_shape=out_shapes + list(exchange.out_shapes), scratch_shapes=scratch + sems,
        input_output_aliases={n_reg + i: n_out + o for i, o in exchange.aliases.items()},
        compiler_params=_params(dimension_semantics=semantics),
    )(*operands, *exchange.inputs)
    return list(got[:n_out]), list(got[n_out:])


class Split(NamedTuple):
    slot: int
    other: jax.Array
    spec: pl.BlockSpec
    use_other: Callable


def _mm_call(name, grid, operands, in_specs, out_shape, out_spec, contract, nk, scale, has_res, tile, exchange=None,
             split=None, into=None, normed=False):
    n_main = len(operands)
    out_shapes, out_specs = [out_shape], [out_spec]
    if normed:
        out_shapes, out_specs = out_shapes + [jax.ShapeDtypeStruct(out_shape.shape, BF16)], out_specs + [out_spec]

    def compute(ins, outs, scr):
        res_ref = ins[2] if has_res else None
        o_ref = outs[0]
        acc_ref = scr[0] if nk > 1 else None

        def finish(v):
            if scale != 1.0:
                v = v * scale
            if has_res:
                v = res_ref[...] + v
            o_ref[...] = v.astype(o_ref.dtype)
            if normed:
                outs[1][...] = _rms_rows(v, ins[2 + has_res][...])[2].astype(BF16)

        def contribute(a_ref, b_ref):
            b = b_ref[...]
            if b.ndim == 3:
                b = b.reshape(b.shape[0] * b.shape[1], b.shape[2])
            part = lax.dot_general(a_ref[...], b, contract, preferred_element_type=F32)
            if nk == 1:
                finish(part)
                return
            kk = pl.program_id(2)

            @pl.when(kk == 0)
            def _():
                acc_ref[...] = part

            @pl.when(jnp.logical_and(kk > 0, kk < nk - 1))
            def _():
                acc_ref[...] += part

            @pl.when(kk == nk - 1)
            def _():
                finish(acc_ref[...] + part)

        if split is None:
            contribute(ins[0], ins[1])
            return
        use_other = split.use_other(pl.program_id(0), pl.program_id(1), pl.program_id(2))
        pair = [ins[0], ins[1]]
        other = list(pair)
        other[split.slot] = ins[n_main]

        @pl.when(jnp.logical_not(use_other))
        def _():
            contribute(*pair)

        @pl.when(use_other)
        def _():
            contribute(*other)

    aliases = None
    if split is not None:
        operands, in_specs = operands + [split.other], in_specs + [split.spec]
    if into is not None:
        aliases = {len(operands): 0}
        operands, in_specs = operands + [into], in_specs + [ANY]
    got = _tiled_call(name, grid, operands, in_specs, out_shapes, out_specs,
                      [pltpu.VMEM(tile, F32)] if nk > 1 else [], compute, exchange, aliases)
    results, carried = (got[0], got[1]) if exchange else (got, None)
    out = tuple(results) if normed else results[0]
    return (out, carried) if exchange else out


def mm_nn(name, a, w, kind, out_dtype, res=None, scale=1.0, exchange=None, norm_gain=None):
    m, k = a.shape
    p, r, c = w.shape
    n = p * c if kind == 'col' else c
    assert k == (r if kind == 'col' else p * r), (name, a.shape, w.shape)
    n_tiles = _divisors(c, LANE, 2816)
    if norm_gain is not None:
        assert kind == 'row'
        n_tiles = [n]
    k_tiles = _divisors(r, LANE, 4096)
    if kind == 'row':
        k_tiles = k_tiles + [q * r for q in (2, 4) if p % q == 0]
    o_item = jnp.dtype(out_dtype).itemsize + (2 if norm_gain is not None else 0)
    tm, tn, tk = _plan_mm(m, n_tiles, k_tiles, n, k, a.dtype.itemsize, o_item, res is not None)
    nk = k // tk
    if kind == 'col':
        cpt = c // tn
        w_spec = pl.BlockSpec((None, tk, tn), lambda j, i, kk: (j // cpt, kk, j % cpt))
    elif tk > r:
        w_spec = pl.BlockSpec((tk // r, r, tn), lambda j, i, kk: (kk, 0, j))
    else:
        rpt = r // tk
        w_spec = pl.BlockSpec((None, tk, tn), lambda j, i, kk: (kk // rpt, kk % rpt, j))
    in_specs = [pl.BlockSpec((tm, tk), lambda j, i, kk: (i, kk)), w_spec]
    operands = [a, w]
    if res is not None:
        in_specs.append(pl.BlockSpec((tm, tn), lambda j, i, kk: (i, j)))
        operands.append(res)
    if norm_gain is not None:
        in_specs.append(pl.BlockSpec((1, tn), lambda j, i, kk: (0, 0)))
        operands.append(norm_gain.reshape(1, n))
    return _mm_call(name, (n // tn, m // tm, nk), operands, in_specs, jax.ShapeDtypeStruct((m, n), out_dtype),
                    pl.BlockSpec((tm, tn), lambda j, i, kk: (i, j)), (((1,), (0,)), ((), ())), nk, scale,
                    res is not None, (tm, tn), exchange, normed=norm_gain is not None)


def mm_nt(name, a, w, kind, out_dtype, scale=1.0, exchange=None, a_hi=None):
    m, kc = a.shape
    if a_hi is not None:
        assert a_hi.shape == a.shape
        kc = 2 * kc
    p, r, c = w.shape
    n = r if kind == 'col' else p * r
    assert kc == (p * c if kind == 'col' else c), (name, a.shape, w.shape)
    n_tiles = _divisors(r, LANE, 2816)
    k_tiles = _divisors(c, LANE, 4096)
    if kind == 'row':
        n_tiles = n_tiles + [q * r for q in (2, 4) if p % q == 0 and q * r <= 2816]
    tm, tn, tk = _plan_mm(m, n_tiles, k_tiles, n, kc, a.dtype.itemsize, jnp.dtype(out_dtype).itemsize, False,
                          1 if a_hi is None else 2)
    nk = kc // tk
    if kind == 'col':
        cpt = c // tk
        w_spec = pl.BlockSpec((None, tn, tk), lambda j, i, kk: (kk // cpt, j, kk % cpt))
    elif tn > r:
        w_spec = pl.BlockSpec((tn // r, r, tk), lambda j, i, kk: (j, 0, kk))
    else:
        rpt = r // tn
        w_spec = pl.BlockSpec((None, tn, tk), lambda j, i, kk: (j // rpt, j % rpt, kk))
    split = None
    a_spec = pl.BlockSpec((tm, tk), lambda j, i, kk: (i, kk))
    if a_hi is not None:
        half = nk // 2
        assert nk % 2 == 0
        a_spec = pl.BlockSpec((tm, tk), lambda j, i, kk: (i, jnp.minimum(kk, half - 1)))
        split = Split(0, a_hi, pl.BlockSpec((tm, tk), lambda j, i, kk: (i, jnp.maximum(kk - half, 0))),
                      lambda j, i, kk: kk >= half)
    return _mm_call(name, (n // tn, m // tm, nk), [a, w], [a_spec, w_spec], jax.ShapeDtypeStruct((m, n), out_dtype),
                    pl.BlockSpec((tm, tn), lambda j, i, kk: (i, j)), (((1,), (1,)), ((), ())), nk, scale, False,
                    (tm, tn), exchange, split)


def _plan_tn(s, ka, nd, r_tiles, n_tiles):
    best, best_cost = None, None
    for ts in _divisors(s, 16, 2048):
        for tr in r_tiles:
            for tn in n_tiles:
                ni, nj, ns = ka // tr, nd // tn, s // ts
                vmem = 2 * (ts * tr * 2 + ts * tn * 2 + tr * tn * 2) + tr * tn * 4 * (2 if ns > 1 else 1)
                if vmem > VMEM_PLAN_BYTES:
                    continue
                traffic = nj * s * ka * 2 + ni * s * nd * 2 + ka * nd * 2
                cost = _tile_time(2 * s * ka * nd, _mxu_fill(ts) * _mxu_fill(tn), traffic, ni * nj * ns,
                                  tr * tn // 1024 if ns > 1 else 0)
                if best_cost is None or cost < best_cost:
                    best, best_cost = (ts, tr, tn), cost
    assert best is not None, (s, ka, nd)
    return best


def mm_tn(name, a, dy, kind, scale=1.0, exchange=None, panels=(0, N_CHIPS), into=None):
    s, ka = a.shape
    s2, nd = dy.shape
    assert s == s2
    p = N_CHIPS
    first_panel, n_panels = panels
    assert kind == 'col' or panels == (0, p)
    r, c = (ka, nd // n_panels) if kind == 'col' else (ka // p, nd)
    ts, tr, tn = _plan_tn(s, ka, nd, _divisors(r, LANE, 2048), _divisors(c, LANE, 2816))
    ns = s // ts
    if kind == 'col':
        cpt = c // tn
        o_spec = pl.BlockSpec((None, tr, tn), lambda j, i, kk: (first_panel + j // cpt, i, j % cpt))
    else:
        rpt = r // tr
        o_spec = pl.BlockSpec((None, tr, tn), lambda j, i, kk: (i // rpt, i % rpt, j))
    in_specs = [pl.BlockSpec((ts, tr), lambda j, i, kk: (kk, i)), pl.BlockSpec((ts, tn), lambda j, i, kk: (kk, j))]
    return _mm_call(name, (nd // tn, ka // tr, ns), [a, dy], in_specs, jax.ShapeDtypeStruct((p, r, c), BF16), o_spec,
                    (((0,), (0,)), ((), ())), ns, scale, False, (tr, tn), exchange, None, into)


def _plan_fused(m, k, f, tiles, n_w, n_io):
    best, best_cost = None, None
    for tm in _divisors(m, 16, 1024):
        for tn in tiles:
            vmem = 2 * (tm * k * 2 + n_w * k * tn * 2 + n_io * tm * tn * 2) + 4 * tm * tn * 4
            if vmem > VMEM_PLAN_BYTES:
                continue
            traffic = (f // tn) * m * k * 2 + n_w * k * f * 2 + n_io * m * f * 2
            cost = _tile_time(2 * m * k * f * n_w, _mxu_fill(tn), traffic, (f // tn) * (m // tm), 0)
            if best_cost is None or cost < best_cost:
                best, best_cost = (tm, tn), cost
    assert best is not None, (m, k, f)
    return best


def mm_swiglu(name, h, w13, exchange=None):
    m, k = h.shape
    p, r, c = w13.shape
    assert r == k and p % 2 == 0
    f = p * c // 2
    tm, tn = _plan_fused(m, k, f, _divisors(c, LANE, 2816), 2, 3)
    cpt = c // tn

    def compute(ins, outs, scr):
        a = ins[0][...]
        g = jnp.dot(a, ins[1][...], preferred_element_type=F32)
        u = jnp.dot(a, ins[2][...], preferred_element_type=F32)
        outs[0][...] = g.astype(BF16)
        outs[1][...] = u.astype(BF16)
        outs[2][...] = (g * _sigmoid(g) * u).astype(BF16)

    tile = pl.BlockSpec((tm, tn), lambda j, i: (i, j))
    in_specs = [pl.BlockSpec((tm, k), lambda j, i: (i, 0)),
                pl.BlockSpec((None, k, tn), lambda j, i: (j // cpt, 0, j % cpt)),
                pl.BlockSpec((None, k, tn), lambda j, i: (j // cpt + p // 2, 0, j % cpt))]
    shape = jax.ShapeDtypeStruct((m, f), BF16)
    got = _tiled_call(name, (f // tn, m // tm), [h, w13, w13], in_specs, [shape] * 3, [tile] * 3, [], compute, exchange)
    return got


def mm_dswiglu(name, dy, w2, gate, up, scale, exchange=None):
    m, k = dy.shape
    p, r, c = w2.shape
    assert c == k
    f = p * r
    tm, tn = _plan_fused(m, k, f, _divisors(r, LANE, 2816), 1, 4)
    rpt = r // tn

    def compute(ins, outs, scr):
        d = lax.dot_general(ins[0][...], ins[1][...], (((1,), (1,)), ((), ())), preferred_element_type=F32) * scale
        g = ins[2][...].astype(F32)
        sg = _sigmoid(g)
        outs[0][...] = (d * ins[3][...].astype(F32) * (sg * (1.0 + g * (1.0 - sg)))).astype(BF16)
        outs[1][...] = (d * (g * sg)).astype(BF16)

    tile = pl.BlockSpec((tm, tn), lambda j, i: (i, j))
    in_specs = [pl.BlockSpec((tm, k), lambda j, i: (i, 0)),
                pl.BlockSpec((None, tn, k), lambda j, i: (j // rpt, j % rpt, 0)), tile, tile]
    shape = jax.ShapeDtypeStruct((m, f), BF16)
    return _tiled_call(name, (f // tn, m // tm), [dy, w2, gate, up], in_specs, [shape] * 2, [tile] * 2, [], compute,
                       exchange)


def _rms_rows(x, g):
    r = lax.rsqrt(jnp.mean(x * x, axis=-1, keepdims=True) + RMS_EPS)
    xhat = x * r
    return xhat, r, xhat * g


def rms_fwd(name, x, g):
    s, d = x.shape
    tm = _row_tile(s, d * 4)

    def body(x_ref, g_ref, o_ref):
        o_ref[...] = _rms_rows(x_ref[...], g_ref[...])[2].astype(BF16)

    return pl.pallas_call(
        body, name=name, grid=(s // tm,),
        in_specs=[pl.BlockSpec((tm, d), lambda i: (i, 0)), pl.BlockSpec((1, d), lambda i: (0, 0))],
        out_specs=pl.BlockSpec((tm, d), lambda i: (i, 0)), out_shape=jax.ShapeDtypeStruct((s, d), BF16),
        compiler_params=_params(dimension_semantics=("arbitrary",)),
    )(x, g.reshape(1, d))


def _rms_bwd_rows(x, g, dh):
    xhat, r, _ = _rms_rows(x, g)
    u = dh * g
    dx = r * (u - xhat * jnp.mean(u * xhat, axis=-1, keepdims=True))
    return dx, jnp.sum(dh * xhat, axis=0, keepdims=True)


def rms_bwd(name, x, g, dh, dres):
    s, d = x.shape
    tm = _row_tile(s, d * 4, 2 * 2**20)
    has_res = dres is not None

    def body(*refs):
        x_ref, g_ref, dh_ref = refs[:3]
        dres_ref = refs[3] if has_res else None
        dx_ref, dxb_ref, dg_ref = refs[-3:]
        dx, dg = _rms_bwd_rows(x_ref[...], g_ref[...], dh_ref[...].astype(F32))
        if has_res:
            dx = dx + dres_ref[...]
        dx_ref[...] = dx
        dxb_ref[...] = dx.astype(BF16)

        @pl.when(pl.program_id(0) == 0)
        def _():
            dg_ref[...] = dg

        @pl.when(pl.program_id(0) > 0)
        def _():
            dg_ref[...] += dg

    row = pl.BlockSpec((tm, d), lambda i: (i, 0))
    vec = pl.BlockSpec((1, d), lambda i: (0, 0))
    return pl.pallas_call(
        body, name=name, grid=(s // tm,),
        in_specs=[row, vec, row] + ([row] if has_res else []),
        out_specs=[row, row, vec],
        out_shape=[jax.ShapeDtypeStruct((s, d), F32), jax.ShapeDtypeStruct((s, d), BF16),
                   jax.ShapeDtypeStruct((1, d), F32)],
        compiler_params=_params(dimension_semantics=("arbitrary",)),
    )(x, g.reshape(1, d), dh, *([dres] if has_res else []))


def loss_head(name, x, g, target):
    s, d = x.shape
    tm = _row_tile(s, d * 4, 2 * 2**20)

    def body(x_ref, g_ref, t_ref, dx_ref, dxb_ref, dg_ref, loss_ref):
        x = x_ref[...]
        gain = g_ref[...]
        y = _rms_rows(x, gain)[2]
        diff = y - t_ref[...]
        dx, dg = _rms_bwd_rows(x, gain, diff * (1.0 / d))
        dx_ref[...] = dx
        dxb_ref[...] = dx.astype(BF16)
        sq = jnp.sum(diff * diff, axis=0, keepdims=True)

        @pl.when(pl.program_id(0) == 0)
        def _():
            dg_ref[...] = dg
            loss_ref[...] = sq

        @pl.when(pl.program_id(0) > 0)
        def _():
            dg_ref[...] += dg
            loss_ref[...] += sq

    row = pl.BlockSpec((tm, d), lambda i: (i, 0))
    vec = pl.BlockSpec((1, d), lambda i: (0, 0))
    return pl.pallas_call(
        body, name=name, grid=(s // tm,), in_specs=[row, vec, row], out_specs=[row, row, vec, vec],
        out_shape=[jax.ShapeDtypeStruct((s, d), F32), jax.ShapeDtypeStruct((s, d), BF16),
                   jax.ShapeDtypeStruct((1, d), F32), jax.ShapeDtypeStruct((1, d), F32)],
        compiler_params=_params(dimension_semantics=("arbitrary",)),
    )(x, g.reshape(1, d), target)


def _sigmoid(x):
    return 0.5 * jnp.tanh(0.5 * x) + 0.5


_INV_SQRT2 = 0.7071067811865476
_INV_SQRT_2PI = 0.3989422804014327


def _normal_cdf(z):
    return 0.5 * (1.0 + lax.erf(z * _INV_SQRT2))


def _gelu_grad(z, cdf):
    return cdf + z * (_INV_SQRT_2PI * jnp.exp(-0.5 * z * z))


def _causal_weights(ws_ref, g):
    t = ws_ref.shape[-1]
    keep = lax.broadcasted_iota(jnp.int32, (t, t), 0) >= lax.broadcasted_iota(jnp.int32, (t, t), 1)
    return jnp.where(keep, ws_ref[g], 0.0).astype(BF16), keep


def _gmlp_gate_rows(z_ref, lg_ref, lb_ref, e):
    z = z_ref[...].astype(F32)
    cdf = _normal_cdf(z)
    gz = z * cdf
    u, v = gz[:, :e], gz[:, e:]
    mu = jnp.mean(v, axis=-1, keepdims=True)
    xc = v - mu
    rs = lax.rsqrt(jnp.mean(xc * xc, axis=-1, keepdims=True) + LN_EPS)
    vhat = xc * rs
    return (z, cdf), u, vhat, rs, vhat * lg_ref[...] + lb_ref[...]


def gmlp_fwd(name, z, ln_g, ln_b, w_s, bias):
    s, e2 = z.shape
    e = e2 // 2
    eg = e // GMLP_GROUPS

    def body(z_ref, lg_ref, lb_ref, ws_ref, b_ref, o_ref):
        _, u, _, _, vln = _gmlp_gate_rows(z_ref, lg_ref, lb_ref, e)
        vb = vln.astype(BF16)
        for g in range(GMLP_GROUPS):
            cols = slice(g * eg, (g + 1) * eg)
            wm, _ = _causal_weights(ws_ref, g)
            f = jnp.dot(wm, vb[:, cols], preferred_element_type=F32) + b_ref[:, cols]
            o_ref[:, cols] = (u[:, cols] * f).astype(BF16)

    full = lambda shape: pl.BlockSpec(shape, lambda i: (0,) * len(shape))
    return pl.pallas_call(
        body, name=name, grid=(s // CHUNK,),
        in_specs=[pl.BlockSpec((CHUNK, e2), lambda i: (i, 0)), full((1, e)), full((1, e)),
                  full((GMLP_GROUPS, CHUNK, CHUNK)), full((CHUNK, e))],
        out_specs=pl.BlockSpec((CHUNK, e), lambda i: (i, 0)), out_shape=jax.ShapeDtypeStruct((s, e), BF16),
        compiler_params=_params(dimension_semantics=("arbitrary",)),
    )(z, ln_g.reshape(1, e), ln_b.reshape(1, e), w_s, bias)


def gmlp_bwd(name, z, dp, ln_g, ln_b, w_s, bias):
    s, e2 = z.shape
    e = e2 // 2
    eg = e // GMLP_GROUPS
    t = CHUNK

    def body(z_ref, dp_ref, lg_ref, lb_ref, ws_ref, b_ref, dz_ref, dws_ref, dbs_ref, dlg_ref, dlb_ref):
        first = pl.program_id(0) == 0
        (zf, cdf), u, vhat, rs, vln = _gmlp_gate_rows(z_ref, lg_ref, lb_ref, e)
        vb = vln.astype(BF16)
        dp = dp_ref[...].astype(F32)
        lane = lax.broadcasted_iota(jnp.int32, (t, LANE), 1)
        dbs = jnp.zeros((t, LANE), F32)
        dvln_parts = []
        for g in range(GMLP_GROUPS):
            cols = slice(g * eg, (g + 1) * eg)
            wm, keep = _causal_weights(ws_ref, g)
            f = jnp.dot(wm, vb[:, cols], preferred_element_type=F32) + b_ref[:, cols]
            dz_ref[:, cols] = (dp[:, cols] * f * _gelu_grad(zf[:, cols], cdf[:, cols])).astype(BF16)
            df = dp[:, cols] * u[:, cols]
            dfb = df.astype(BF16)
            dbs = dbs + jnp.where(lane == g, jnp.sum(df, axis=-1, keepdims=True), 0.0)
            dw = lax.dot_general(dfb, vb[:, cols], (((1,), (1,)), ((), ())), preferred_element_type=F32)
            dw = jnp.where(keep, dw, 0.0)

            @pl.when(first)
            def _():
                dws_ref[g] = dw

            @pl.when(jnp.logical_not(first))
            def _():
                dws_ref[g] += dw

            dvln_parts.append(lax.dot_general(wm, dfb, (((0,), (0,)), ((), ())), preferred_element_type=F32))
        dvln = jnp.concatenate(dvln_parts, axis=-1)
        dvhat = dvln * lg_ref[...]
        dv = rs * (dvhat - jnp.mean(dvhat, axis=-1, keepdims=True)
                   - vhat * jnp.mean(dvhat * vhat, axis=-1, keepdims=True))
        dz_ref[:, e:] = (dv * _gelu_grad(zf[:, e:], cdf[:, e:])).astype(BF16)
        dlg = jnp.sum(dvln * vhat, axis=0, keepdims=True)
        dlb = jnp.sum(dvln, axis=0, keepdims=True)

        @pl.when(first)
        def _():
            dbs_ref[...] = dbs
            dlg_ref[...] = dlg
            dlb_ref[...] = dlb

        @pl.when(jnp.logical_not(first))
        def _():
            dbs_ref[...] += dbs
            dlg_ref[...] += dlg
            dlb_ref[...] += dlb

    full = lambda shape: pl.BlockSpec(shape, lambda i: (0,) * len(shape))
    return pl.pallas_call(
        body, name=name, grid=(s // t,),
        in_specs=[pl.BlockSpec((t, e2), lambda i: (i, 0)), pl.BlockSpec((t, e), lambda i: (i, 0)), full((1, e)),
                  full((1, e)), full((GMLP_GROUPS, t, t)), full((t, e))],
        out_specs=[pl.BlockSpec((t, e2), lambda i: (i, 0)), full((GMLP_GROUPS, t, t)), full((t, LANE)), full((1, e)),
                   full((1, e))],
        out_shape=[jax.ShapeDtypeStruct((s, e2), BF16), jax.ShapeDtypeStruct((GMLP_GROUPS, t, t), F32),
                   jax.ShapeDtypeStruct((t, LANE), F32), jax.ShapeDtypeStruct((1, e), F32),
                   jax.ShapeDtypeStruct((1, e), F32)],
        compiler_params=_params(dimension_semantics=("arbitrary",)),
    )(z, dp, ln_g.reshape(1, e), ln_b.reshape(1, e), w_s, bias)


EDGE = 16


def _shift_down(zc, prev, k):
    tm = zc.shape[0]
    row = lax.broadcasted_iota(jnp.int32, (tm, 1), 0)
    out = pltpu.roll(zc, k, 0)
    for j in range(k):
        out = jnp.where(row == j, prev[EDGE - k + j:EDGE - k + j + 1, :], out)
    return out


def _shift_up(dc, nxt, k):
    tm = dc.shape[0]
    row = lax.broadcasted_iota(jnp.int32, (tm, 1), 0)
    out = pltpu.roll(dc, tm - k, 0)
    for j in range(k):
        out = jnp.where(row == tm - k + j, nxt[j:j + 1, :], out)
    return out


def conv_fwd(name, bcv, cw):
    s, d3 = bcv.shape
    d = d3 // 3
    tm = _row_tile(s, d * 4, 2 * 2**20)
    per = tm // EDGE

    def body(b_ref, c_ref, v_ref, cp_ref, vp_ref, w_ref, o_ref):
        i = pl.program_id(0)
        zc = c_ref[...].astype(F32) * v_ref[...].astype(F32)
        prev = jnp.where(i > 0, cp_ref[...].astype(F32) * vp_ref[...].astype(F32), 0.0)
        conv = w_ref[2:3, :] * zc + w_ref[1:2, :] * _shift_down(zc, prev, 1) + w_ref[0:1, :] * _shift_down(zc, prev, 2)
        o_ref[...] = (b_ref[...].astype(F32) * conv).astype(BF16)

    blk = lambda col: pl.BlockSpec((tm, d), lambda i: (i, col))
    edge = lambda col: pl.BlockSpec((EDGE, d), lambda i: (jnp.maximum(i * per - 1, 0), col))
    return pl.pallas_call(
        body, name=name, grid=(s // tm,),
        in_specs=[blk(0), blk(1), blk(2), edge(1), edge(2), pl.BlockSpec((3, d), lambda i: (0, 0))],
        out_specs=pl.BlockSpec((tm, d), lambda i: (i, 0)), out_shape=jax.ShapeDtypeStruct((s, d), BF16),
        compiler_params=_params(dimension_semantics=("arbitrary",)),
    )(bcv, bcv, bcv, bcv, bcv, cw)


def conv_bwd(name, bcv, dq, cw):
    s, d3 = bcv.shape
    d = d3 // 3
    tm = _row_tile(s, d * 4, 2**20)
    per = tm // EDGE
    n_tiles = s // tm
    last_edge = s // EDGE - 1

    def body(b_ref, c_ref, v_ref, cp_ref, vp_ref, bn_ref, dq_ref, dqn_ref, w_ref, o_ref, dw_ref):
        i = pl.program_id(0)
        b = b_ref[...].astype(F32)
        c = c_ref[...].astype(F32)
        v = v_ref[...].astype(F32)
        dq = dq_ref[...].astype(F32)
        zc = c * v
        prev = jnp.where(i > 0, cp_ref[...].astype(F32) * vp_ref[...].astype(F32), 0.0)
        z1 = _shift_down(zc, prev, 1)
        z2 = _shift_down(zc, prev, 2)
        w0, w1, w2 = w_ref[0:1, :], w_ref[1:2, :], w_ref[2:3, :]
        conv = w2 * zc + w1 * z1 + w0 * z2
        dconv = dq * b
        nxt = jnp.where(i < n_tiles - 1, dqn_ref[...].astype(F32) * bn_ref[...].astype(F32), 0.0)
        dz = w2 * dconv + w1 * _shift_up(dconv, nxt, 1) + w0 * _shift_up(dconv, nxt, 2)
        o_ref[:, :d] = (dq * conv).astype(BF16)
        o_ref[:, d:2 * d] = (dz * v).astype(BF16)
        o_ref[:, 2 * d:] = (dz * c).astype(BF16)
        dw = jnp.concatenate([jnp.sum(dconv * z2, axis=0, keepdims=True), jnp.sum(dconv * z1, axis=0, keepdims=True),
                              jnp.sum(dconv * zc, axis=0, keepdims=True), jnp.zeros((5, d), F32)], axis=0)

        @pl.when(i == 0)
        def _():
            dw_ref[...] = dw

        @pl.when(i > 0)
        def _():
            dw_ref[...] += dw

    blk = lambda col: pl.BlockSpec((tm, d), lambda i: (i, col))
    before = lambda col: pl.BlockSpec((EDGE, d), lambda i: (jnp.maximum(i * per - 1, 0), col))
    after = lambda col: pl.BlockSpec((EDGE, d), lambda i: (jnp.minimum((i + 1) * per, last_edge), col))
    return pl.pallas_call(
        body, name=name, grid=(n_tiles,),
        in_specs=[blk(0), blk(1), blk(2), before(1), before(2), after(0), blk(0), after(0),
                  pl.BlockSpec((3, d), lambda i: (0, 0))],
        out_specs=[pl.BlockSpec((tm, d3), lambda i: (i, 0)), pl.BlockSpec((8, d), lambda i: (0, 0))],
        out_shape=[jax.ShapeDtypeStruct((s, d3), BF16), jax.ShapeDtypeStruct((8, d), F32)],
        compiler_params=_params(dimension_semantics=("arbitrary",)),
    )(bcv, bcv, bcv, bcv, bcv, bcv, dq, dq, cw)


def _attn_probs(qh, kh, scale):
    sc = lax.dot_general(qh, kh, (((1,), (1,)), ((), ())), preferred_element_type=F32) * scale
    ex = jnp.exp(sc - jnp.max(sc, axis=-1, keepdims=True))
    return ex / jnp.sum(ex, axis=-1, keepdims=True)


def attn_fwd(name, q, kv):
    s, d = q.shape
    mlen = kv.shape[0]
    dh = d // XATTN_HEADS
    scale = dh ** -0.5
    tm = _row_tile(s, d * 4, 4 * 2**20)

    def body(q_ref, kv_ref, o_ref):
        for h in range(XATTN_HEADS):
            cols = slice(h * dh, (h + 1) * dh)
            p = _attn_probs(q_ref[:, cols], kv_ref[:, cols], scale)
            o_ref[:, cols] = jnp.dot(p.astype(BF16), kv_ref[:, d + h * dh:d + (h + 1) * dh],
                                     preferred_element_type=F32).astype(BF16)

    return pl.pallas_call(
        body, name=name, grid=(s // tm,),
        in_specs=[pl.BlockSpec((tm, d), lambda i: (i, 0)), pl.BlockSpec((mlen, 2 * d), lambda i: (0, 0))],
        out_specs=pl.BlockSpec((tm, d), lambda i: (i, 0)), out_shape=jax.ShapeDtypeStruct((s, d), BF16),
        compiler_params=_params(dimension_semantics=("arbitrary",)),
    )(q, kv)


def attn_bwd(name, q, kv, do):
    s, d = q.shape
    mlen = kv.shape[0]
    dh = d // XATTN_HEADS
    scale = dh ** -0.5
    tm = _row_tile(s, d * 4, 4 * 2**20)

    def body(q_ref, kv_ref, do_ref, dq_ref, dkv_ref):
        first = pl.program_id(0) == 0
        for h in range(XATTN_HEADS):
            cols = slice(h * dh, (h + 1) * dh)
            vcols = slice(d + h * dh, d + (h + 1) * dh)
            qh, kh, vh, doh = q_ref[:, cols], kv_ref[:, cols], kv_ref[:, vcols], do_ref[:, cols]
            p = _attn_probs(qh, kh, scale)
            dp = lax.dot_general(doh, vh, (((1,), (1,)), ((), ())), preferred_element_type=F32)
            ds = (p * (dp - jnp.sum(dp * p, axis=-1, keepdims=True)) * scale).astype(BF16)
            dq_ref[:, cols] = jnp.dot(ds, kh, preferred_element_type=F32).astype(BF16)
            dk = lax.dot_general(ds, qh, (((0,), (0,)), ((), ())), preferred_element_type=F32)
            dv = lax.dot_general(p.astype(BF16), doh, (((0,), (0,)), ((), ())), preferred_element_type=F32)

            @pl.when(first)
            def _():
                dkv_ref[:, cols] = dk
                dkv_ref[:, vcols] = dv

            @pl.when(jnp.logical_not(first))
            def _():
                dkv_ref[:, cols] += dk
                dkv_ref[:, vcols] += dv

    row = pl.BlockSpec((tm, d), lambda i: (i, 0))
    whole = pl.BlockSpec((mlen, 2 * d), lambda i: (0, 0))
    return pl.pallas_call(
        body, name=name, grid=(s // tm,), in_specs=[row, whole, row], out_specs=[row, whole],
        out_shape=[jax.ShapeDtypeStruct((s, d), BF16), jax.ShapeDtypeStruct((mlen, 2 * d), F32)],
        compiler_params=_params(dimension_semantics=("arbitrary",)),
    )(q, kv, do)


def _as_rows(a):
    if a.ndim >= 2 and a.shape[-1] % LANE == 0:
        return a.reshape(-1, a.shape[-1])
    return a.reshape(-1, LANE) if a.size % LANE == 0 else a.reshape(1, -1)


def add_halves(name, dw, other, core):
    p, r, c = dw.shape
    h = r // 2
    th = _row_tile(h, c * 2, 2 * 2**20)

    def body(core_ref, a_ref, b_ref, o_ref):
        o_ref[...] = (a_ref[...].astype(F32) + b_ref[...].astype(F32)).astype(BF16)

    grid_spec = pltpu.PrefetchScalarGridSpec(
        num_scalar_prefetch=1, grid=(p, h // th),
        in_specs=[pl.BlockSpec((None, None, th, c), lambda pi, i, core_ref: (pi, core_ref[0], i, 0)),
                  pl.BlockSpec((None, th, c), lambda pi, i, core_ref: (pi, i, 0))],
        out_specs=pl.BlockSpec((None, th, c), lambda pi, i, core_ref: (pi, i, 0)))
    return pl.pallas_call(
        body, name=name, grid_spec=grid_spec, out_shape=jax.ShapeDtypeStruct((p, h, c), BF16),
        compiler_params=_params(dimension_semantics=("arbitrary", "arbitrary")),
    )(core, dw.reshape(p, 2, h, c), other)


def sum_leading(name, parts):
    n, r, c = parts.shape
    tr = _row_tile(r, c * 4 * 2, 2 * 2**20)

    def body(p_ref, o_ref):
        acc = p_ref[0].astype(F32)
        for k in range(1, n):
            acc = acc + p_ref[k].astype(F32)
        o_ref[...] = acc

    return pl.pallas_call(
        body, name=name, grid=(r // tr,), in_specs=[pl.BlockSpec((n, tr, c), lambda i: (0, i, 0))],
        out_specs=pl.BlockSpec((tr, c), lambda i: (i, 0)), out_shape=jax.ShapeDtypeStruct((r, c), F32),
        compiler_params=_params(dimension_semantics=("arbitrary",)),
    )(parts)


def _adamw_rows(w, g, m, v):
    m = ADAM_B1 * m + (1.0 - ADAM_B1) * g
    v = ADAM_B2 * v + (1.0 - ADAM_B2) * (g * g)
    m_hat = m / (1.0 - ADAM_B1 ** ADAM_STEP)
    v_hat = v / (1.0 - ADAM_B2 ** ADAM_STEP)
    delta = -ADAM_LR * (m_hat / (jnp.sqrt(v_hat) + ADAM_EPS) + ADAM_WD * w)
    return delta, m, v


def adamw_layer(name, w, m, v, g, layer, carried):
    nl, r, c = w.shape
    tr = _row_tile(r, c * 4, 3 * 2**19)
    n_carried = 4 if carried is not None else 0

    def body(*refs):
        w_ref, m_ref, v_ref, g_ref = refs[:4]
        go_ref, d_ref, mo_ref, vo_ref = refs[4 + n_carried:]
        g = g_ref[...]
        delta, m_new, v_new = _adamw_rows(w_ref[...], g, m_ref[...], v_ref[...])
        go_ref[...] = g
        d_ref[...] = delta
        mo_ref[...] = m_new
        vo_ref[...] = v_new

    stacked = pl.BlockSpec((None, tr, c), lambda i: (layer, i, 0))
    in_specs = [stacked, stacked, stacked, pl.BlockSpec((tr, c), lambda i: (i, 0))]
    in_specs += [pl.BlockSpec(memory_space=pl.ANY)] * n_carried
    shape = jax.ShapeDtypeStruct((nl, r, c), F32)
    return pl.pallas_call(
        body, name=name, grid=(r // tr,), in_specs=in_specs, out_specs=[stacked] * 4, out_shape=[shape] * 4,
        input_output_aliases={4 + k: k for k in range(n_carried)},
        compiler_params=_params(dimension_semantics=("arbitrary",)),
    )(w, m, v, g, *(carried or ()))


def adamw_flat(name, w, m, v, g):
    r, c = w.shape

    def body(w_ref, m_ref, v_ref, g_ref, d_ref, mo_ref, vo_ref):
        delta, m_new, v_new = _adamw_rows(w_ref[...], g_ref[...], m_ref[...], v_ref[...])
        d_ref[...] = delta
        mo_ref[...] = m_new
        vo_ref[...] = v_new

    shape = jax.ShapeDtypeStruct((r, c), F32)
    return pl.pallas_call(body, name=name, out_shape=[shape] * 3, compiler_params=_params())(w, m, v, g)


def cast_place(name, w, layer, place):
    nl, r, c = w.shape
    tr = _row_tile(r, c * 4, 2 * 2**20)

    def body(x_ref, y_ref, c_ref, w_ref, o_ref):
        o_ref[...] = w_ref[...].astype(BF16)

    grid_spec = pltpu.PrefetchScalarGridSpec(
        num_scalar_prefetch=3, grid=(r // tr,),
        in_specs=[pl.BlockSpec((None, tr, c), lambda i, x_ref, y_ref, c_ref: (layer, i, 0))],
        out_specs=pl.BlockSpec((None, tr, c), lambda i, x_ref, y_ref, c_ref: (2 * x_ref[0] + y_ref[0], i, 0)))
    return pl.pallas_call(
        body, name=name, grid_spec=grid_spec, out_shape=jax.ShapeDtypeStruct((N_CHIPS, r, c), BF16),
        compiler_params=_params(dimension_semantics=("arbitrary",)),
    )(*place, w)


def reduce_sum4(name, own, landed, place):
    p, h, c = own.shape
    tr = _row_tile(h, c * 4, 2**20)

    def body(x_ref, y_ref, c_ref, t_ref, y1_ref, y2_ref, y3_ref, o_ref):
        acc = t_ref[...].astype(F32)
        for part_ref in (y1_ref, y2_ref, y3_ref):
            acc = acc + part_ref[...].astype(F32)
        o_ref[...] = acc

    def panel(fx, fy):
        return pl.BlockSpec((None, tr, c), lambda i, x_ref, y_ref, c_ref: (
            2 * (1 - x_ref[0] if fx else x_ref[0]) + (1 - y_ref[0] if fy else y_ref[0]), i, 0))

    grid_spec = pltpu.PrefetchScalarGridSpec(
        num_scalar_prefetch=3, grid=(h // tr,),
        in_specs=[panel(0, 0), panel(1, 0), panel(0, 1), panel(1, 1)],
        out_specs=pl.BlockSpec((None, tr, c), lambda i, x_ref, y_ref, c_ref: (c_ref[0], i, 0)))
    return pl.pallas_call(
        body, name=name, grid_spec=grid_spec, out_shape=jax.ShapeDtypeStruct((2, h, c), F32),
        compiler_params=_params(dimension_semantics=("arbitrary",)),
    )(*place, own, landed, landed, landed)


def run_exchange(name, exchange):
    n_in, n_out = len(exchange.inputs), len(exchange.out_shapes)

    def body(*refs):
        ins, outs, sems = refs[:n_in], refs[n_in:n_in + n_out], refs[n_in + n_out:]
        exchange.start(ins, outs, sems)
        exchange.finish(ins, outs, sems)

    return pl.pallas_call(
        body, name=name, in_specs=[ANY] * n_in, out_specs=[ANY] * n_out, out_shape=list(exchange.out_shapes),
        scratch_shapes=[pltpu.SemaphoreType.DMA((exchange.n_sems,)), pltpu.SemaphoreType.DMA((exchange.n_sems,))],
        input_output_aliases=dict(exchange.aliases),
    )(*exchange.inputs)


def _row_halves(ref, c):
    h = ref.shape[1] // 2
    return pl.ds(pl.multiple_of(c * h, 16), h), pl.ds(pl.multiple_of((1 - c) * h, 16), h)


def gather_exchange(fulls):
    n = len(fulls)

    def start(ins, outs, sems):
        x, y, c, mine, chips = _place()
        for a in range(n):
            rows = outs[a].at[mine, _row_halves(outs[a], c)[0]]
            for j, chip in enumerate(chips):
                _remote(rows, rows, sems, 6 * a + j, (*chip, c)).start()

    def finish(ins, outs, sems):
        x, y, c, mine, chips = _place()
        sibling = (x, y, 1 - c)
        for a in range(n):
            half = _row_halves(outs[a], c)[0]
            for j, chip in enumerate(chips):
                rows = outs[a].at[2 * chip[0] + chip[1], half]
                _remote(rows, rows, sems, 6 * a + j, sibling).wait_recv()
                _remote(rows, rows, sems, 6 * a + 3 + j, sibling).start()
        for a in range(n):
            half, other = _row_halves(outs[a], c)
            for j, chip in enumerate(chips):
                rows = outs[a].at[2 * chip[0] + chip[1], other]
                _remote(rows, rows, sems, 6 * a + 3 + j, sibling).wait_recv()
            for j, chip in enumerate(chips):
                rows = outs[a].at[mine, half]
                _remote(rows, rows, sems, 6 * a + j, (*chip, c)).wait_send()
                rows = outs[a].at[2 * chip[0] + chip[1], half]
                _remote(rows, rows, sems, 6 * a + 3 + j, sibling).wait_send()

    return Exchange(list(fulls), [jax.ShapeDtypeStruct(f.shape, f.dtype) for f in fulls], {a: a for a in range(n)},
                    6 * n, start, finish)


def swap_exchange(grads):
    n = len(grads)

    def copies(ins, outs, sems):
        x, y, c, _, _ = _place()
        return [_remote(ins[a].at[:, _row_halves(ins[a], c)[1]], outs[a], sems, a, (x, y, 1 - c)) for a in range(n)]

    def start(ins, outs, sems):
        for cp in copies(ins, outs, sems):
            cp.start()

    def finish(ins, outs, sems):
        for cp in copies(ins, outs, sems):
            cp.wait()

    shapes = [jax.ShapeDtypeStruct((g.shape[0], g.shape[1] // 2, g.shape[2]), g.dtype) for g in grads]
    return Exchange(list(grads), shapes, {}, n, start, finish)


def scatter_exchange(parts):
    n = len(parts)

    def sends(ins, outs, sems):
        x, y, c, mine, chips = _place()
        return [_remote(ins[a].at[2 * chip[0] + chip[1]], outs[a].at[mine], sems, 3 * a + j, (*chip, c))
                for a in range(n) for j, chip in enumerate(chips)]

    def start(ins, outs, sems):
        for cp in sends(ins, outs, sems):
            cp.start()

    def finish(ins, outs, sems):
        x, y, c, mine, chips = _place()
        for a in range(n):
            for j, chip in enumerate(chips):
                landing = outs[a].at[2 * chip[0] + chip[1]]
                _remote(landing, landing, sems, 3 * a + j, (*chip, c)).wait_recv()
        for cp in sends(ins, outs, sems):
            cp.wait_send()

    return Exchange(list(parts), [jax.ShapeDtypeStruct(g.shape, g.dtype) for g in parts], {}, 3 * n, start, finish)


def join_exchange(halves):
    n = len(halves)

    def start(ins, outs, sems):
        x, y, c, _, _ = _place()
        for a in range(n):
            _remote(outs[a].at[c], outs[a].at[c], sems, a, (x, y, 1 - c)).start()

    def finish(ins, outs, sems):
        x, y, c, _, _ = _place()
        for a in range(n):
            _remote(outs[a].at[1 - c], outs[a].at[1 - c], sems, a, (x, y, 1 - c)).wait_recv()
        for a in range(n):
            _remote(outs[a].at[c], outs[a].at[c], sems, a, (x, y, 1 - c)).wait_send()

    return Exchange(list(halves), [jax.ShapeDtypeStruct(g.shape, g.dtype) for g in halves], {a: a for a in range(n)},
                    n, start, finish)


def gather_all(name, rows):
    def body(in_ref, out_ref, send_sems, recv_sems, local_sem):
        sems = (send_sems, recv_sems)
        x, y, c, _, _ = _place()
        me = 4 * x + 2 * y + c
        local = pltpu.make_async_copy(in_ref, out_ref.at[me], local_sem)
        local.start()
        peers = [(1 - x if k & 4 else x, 1 - y if k & 2 else y, 1 - c if k & 1 else c) for k in range(1, N_DEV)]
        sent = []
        for k, peer in enumerate(peers):
            cp = _remote(in_ref, out_ref.at[me], sems, k, peer)
            cp.start()
            sent.append(cp)
        for k, peer in enumerate(peers):
            landing = out_ref.at[4 * peer[0] + 2 * peer[1] + peer[2]]
            _remote(landing, landing, sems, k, peer).wait_recv()
        for cp in sent:
            cp.wait_send()
        local.wait()

    return pl.pallas_call(
        body, name=name, in_specs=[ANY], out_specs=ANY,
        out_shape=jax.ShapeDtypeStruct((N_DEV,) + rows.shape, rows.dtype),
        scratch_shapes=[pltpu.SemaphoreType.DMA((N_DEV - 1,)), pltpu.SemaphoreType.DMA((N_DEV - 1,)),
                        pltpu.SemaphoreType.DMA],
    )(rows)


class _Step:
    def __init__(self, p):
        self.p = p
        xi, yi, ci = lax.axis_index("x"), lax.axis_index("y"), lax.axis_index("c")
        self.chip = 2 * xi + yi
        self.place_refs = tuple(v.astype(jnp.int32).reshape(1) for v in (xi, yi, ci))
        self.core_ref = self.place_refs[2]
        self.depth = p['ffn1_norm'].shape[0]
        self.placed, self.w, self.big_g = {}, {}, {}
        self.waiting_joins = []
        self.waiting_scatter = None

    def block_keys(self, tag, l):
        if l >= self.depth:
            return []
        mixer = ['gmlp_w_in', 'gmlp_w_out'] if l % 2 == 0 else ['conv_w_in', 'conv_w_out']
        names = {"ffn1": ['ffn1_w13', 'ffn1_w2'], "mix": mixer, "xattn": ['xattn_wq', 'xattn_wkv', 'xattn_wo'],
                 "ffn2": ['ffn2_w13', 'ffn2_w2']}[tag]
        return [(n, l // 2 if tag == "mix" else l) for n in names]

    def place(self, keys):
        for n, idx in keys:
            self.placed[(n, idx)] = cast_place(f"place_{n}{idx}", self.p[n], idx, self.place_refs)

    def gather_alone(self, name, keys):
        got = run_exchange(name, gather_exchange([self.placed[k] for k in keys]))
        self.w.update(zip(keys, got, strict=True))

    def carrying_gather(self, mm, keys, *args, **kw):
        keys = [k for k in keys if k in self.placed]
        if not keys:
            return mm(*args, **kw)
        out, got = mm(*args, exchange=gather_exchange([self.placed[k] for k in keys]), **kw)
        self.w.update(zip(keys, got, strict=True))
        return out

    def reduce_begin(self, tag, keys, dws):
        theirs = run_exchange(tag + "_swap", swap_exchange(dws))
        parts = [add_halves(f"{tag}_add{i}", dw, t, self.core_ref) for i, (dw, t) in enumerate(zip(dws, theirs, strict=True))]
        assert self.waiting_scatter is None
        self.waiting_scatter = (tag, keys, parts)

    def carrying_scatter(self, mm, *args, **kw):
        tag, keys, parts = self.waiting_scatter
        self.waiting_scatter = None
        out, landed = mm(*args, exchange=scatter_exchange(parts), **kw)
        halves = [reduce_sum4(f"{tag}_sum{i}", t, y, self.place_refs) for i, (t, y) in enumerate(zip(parts, landed, strict=True))]
        self.waiting_joins += list(zip(keys, halves, strict=True))
        return out

    def take_joined(self, keys, joined):
        for k, g in zip(keys, joined, strict=True):
            self.big_g[k] = g.reshape(-1, g.shape[-1])

    def carrying_joins(self, mm, *args, **kw):
        if not self.waiting_joins:
            return mm(*args, **kw)
        keys, halves = zip(*self.waiting_joins, strict=True)
        self.waiting_joins = []
        out, joined = mm(*args, exchange=join_exchange(list(halves)), **kw)
        self.take_joined(keys, joined)
        return out

    def joins_alone(self, name):
        keys, halves = zip(*self.waiting_joins, strict=True)
        self.waiting_joins = []
        self.take_joined(keys, run_exchange(name, join_exchange(list(halves))))

    def ffn_fwd(self, tag, l, x, gain, carry13, carry2, h=None):
        name = f"l{l}_{tag}"
        if h is None:
            h = rms_fwd(name + "_norm", x, gain)
        gate, up, act = self.carrying_gather(mm_swiglu, carry13, name + "_w13", h, self.w[(tag + '_w13', l)])
        out = self.carrying_gather(mm_nn, carry2, name + "_w2", act, self.w[(tag + '_w2', l)], 'row', F32, res=x,
                                   scale=0.5)
        return out, (x, h, gate, up, act)

    def ffn_bwd(self, tag, l, dx, dxb, saved, gain):
        w13, w2 = self.w[(tag + '_w13', l)], self.w[(tag + '_w2', l)]
        name = f"l{l}_{tag}"
        x, h, gate, up, act = saved
        d_gate, d_up = self.carrying_joins(mm_dswiglu, name + "_dact", dxb, w2, gate, up, 0.5)
        d_w2 = mm_tn(name + "_dw2", act, dxb, 'row', scale=0.5)
        half = N_CHIPS // 2
        d_w13 = mm_tn(name + "_dw13g", h, d_gate, 'col', panels=(0, half))
        d_w13 = mm_tn(name + "_dw13u", h, d_up, 'col', panels=(half, half), into=d_w13)
        self.reduce_begin(name, [(tag + '_w13', l), (tag + '_w2', l)], [d_w13, d_w2])
        dh = self.carrying_scatter(mm_nt, name + "_dh", d_gate, w13, 'col', BF16, a_hi=d_up)
        return rms_bwd(name + "_dnorm", x, gain, dh, dx)


def kernel(x, mem, ffn1_norm, ffn1_w13, ffn1_w2, mix_norm, gmlp_w_in, gmlp_ln_g, gmlp_ln_b, gmlp_w_s, gmlp_b_s, gmlp_w_out, conv_w_in, conv_w, conv_w_out, xattn_norm, mem_norm, xattn_wq, xattn_wkv, xattn_wo, ffn2_norm, ffn2_w13, ffn2_w2, final_norm, loss_target, m_ffn1_norm, m_ffn1_w13, m_ffn1_w2, m_mix_norm, m_gmlp_w_in, m_gmlp_ln_g, m_gmlp_ln_b, m_gmlp_w_s, m_gmlp_b_s, m_gmlp_w_out, m_conv_w_in, m_conv_w, m_conv_w_out, m_xattn_norm, m_mem_norm, m_xattn_wq, m_xattn_wkv, m_xattn_wo, m_ffn2_norm, m_ffn2_w13, m_ffn2_w2, m_final_norm, v_ffn1_norm, v_ffn1_w13, v_ffn1_w2, v_mix_norm, v_gmlp_w_in, v_gmlp_ln_g, v_gmlp_ln_b, v_gmlp_w_s, v_gmlp_b_s, v_gmlp_w_out, v_conv_w_in, v_conv_w, v_conv_w_out, v_xattn_norm, v_mem_norm, v_xattn_wq, v_xattn_wkv, v_xattn_wo, v_ffn2_norm, v_ffn2_w13, v_ffn2_w2, v_final_norm):
    return _step(dict(locals()))


def _step(p):
    assert sorted(p) == sorted(ARG_NAMES)
    st = _Step(p)
    x = p['x'][0]
    mem = p['mem'][0]
    target = p['loss_target'][0]
    s, d = x.shape
    depth = st.depth

    for l in range(depth):
        for tag in ("ffn1", "mix", "xattn", "ffn2"):
            st.place(st.block_keys(tag, l))
    st.gather_alone("gather_first", [('ffn1_w13', 0)])

    cw_local = p['conv_w']
    n_conv, cwid, dq4 = cw_local.shape
    cw_rows = jnp.pad(cw_local.reshape(-1, LANE), ((0, (-cw_local.size // LANE) % 8), (0, 0)))
    cw_all = gather_all("gather_conv_w", cw_rows)[0::2, :cw_local.size // LANE]
    conv_w_full = cw_all.reshape(N_CHIPS, n_conv, cwid, dq4).transpose(1, 2, 0, 3).reshape(n_conv, cwid, N_CHIPS * dq4)

    saved = []
    for l in range(depth):
        j = l // 2
        rec = {}
        first_w2 = [('ffn1_w2', 0)] if l == 0 else []
        x, rec['ffn1'] = st.ffn_fwd("ffn1", l, x, p['ffn1_norm'][l],
                                    first_w2 + st.block_keys("mix", l) + st.block_keys("xattn", l), [('ffn2_w13', l)])
        h = rms_fwd(f"l{l}_mix_norm", x, p['mix_norm'][l])
        if l % 2 == 0:
            e = p['gmlp_ln_g'].shape[-1]
            bias = jnp.repeat(p['gmlp_b_s'][j].T, e // GMLP_GROUPS, axis=1)
            z = st.carrying_gather(mm_nn, [('ffn2_w2', l)], f"l{l}_gmlp_in", h, st.w[('gmlp_w_in', j)], 'col', BF16)
            gate = gmlp_fwd(f"l{l}_gmlp_gate", z, p['gmlp_ln_g'][j], p['gmlp_ln_b'][j], p['gmlp_w_s'][j], bias)
            x_new, hq = mm_nn(f"l{l}_gmlp_out", gate, st.w[('gmlp_w_out', j)], 'row', F32, res=x,
                              norm_gain=p['xattn_norm'][l])
            rec['mix'] = (x, h, z, gate, bias)
        else:
            bcv = st.carrying_gather(mm_nn, [('ffn2_w2', l)], f"l{l}_conv_in", h, st.w[('conv_w_in', j)], 'col', BF16)
            gate = conv_fwd(f"l{l}_conv_gate", bcv, conv_w_full[j])
            x_new, hq = mm_nn(f"l{l}_conv_out", gate, st.w[('conv_w_out', j)], 'row', F32, res=x,
                              norm_gain=p['xattn_norm'][l])
            rec['mix'] = (x, h, bcv, gate)
        x = x_new
        q = mm_nn(f"l{l}_xattn_q", hq, st.w[('xattn_wq', l)], 'row', BF16)
        mem_n = rms_fwd(f"l{l}_mem_norm", mem, p['mem_norm'][l])
        kv = mm_nn(f"l{l}_xattn_kv", mem_n, st.w[('xattn_wkv', l)], 'col', BF16)
        o = attn_fwd(f"l{l}_xattn_core", q, kv)
        x_new, h_ffn2 = mm_nn(f"l{l}_xattn_o", o, st.w[('xattn_wo', l)], 'row', F32, res=x,
                              norm_gain=p['ffn2_norm'][l])
        rec['xattn'] = (x, hq, q, mem_n, kv, o)
        x = x_new
        x, rec['ffn2'] = st.ffn_fwd("ffn2", l, x, p['ffn2_norm'][l], st.block_keys("ffn1", l + 1), [], h=h_ffn2)
        saved.append(rec)

    dx, dxb, d_final, loss_lanes = loss_head("loss_head", x, p['final_norm'], target)
    loss = lax.psum(0.5 * jnp.sum(loss_lanes) / d, ("x", "y", "c"))

    small = {n: [None] * p[n].shape[0] for n in ('ffn1_norm', 'mix_norm', 'xattn_norm', 'mem_norm', 'ffn2_norm',
                                                  'gmlp_ln_g', 'gmlp_ln_b', 'gmlp_w_s', 'gmlp_b_s', 'conv_w')}
    for l in reversed(range(depth)):
        j = l // 2
        rec = saved[l]
        dx, dxb, small['ffn2_norm'][l] = st.ffn_bwd("ffn2", l, dx, dxb, rec['ffn2'], p['ffn2_norm'][l])

        x_in, hq, q, mem_n, kv, o = rec['xattn']
        name = f"l{l}_xattn"
        do = st.carrying_joins(mm_nt, name + "_do", dxb, st.w[('xattn_wo', l)], 'row', BF16)
        d_wo = mm_tn(name + "_dwo", o, dxb, 'row')
        dq, dkv = attn_bwd(name + "_dcore", q, kv, do)
        d_wq = mm_tn(name + "_dwq", hq, dq, 'row')
        dkvb = dkv.astype(BF16)
        d_wkv = mm_tn(name + "_dwkv", mem_n, dkvb, 'col')
        st.reduce_begin(name, [('xattn_wq', l), ('xattn_wkv', l), ('xattn_wo', l)], [d_wq, d_wkv, d_wo])
        dh = mm_nt(name + "_dh", dq, st.w[('xattn_wq', l)], 'row', BF16)
        dx, dxb, small['xattn_norm'][l] = rms_bwd(name + "_dnorm", x_in, p['xattn_norm'][l], dh, dx)
        dmem_n = mm_nt(name + "_dmem", dkvb, st.w[('xattn_wkv', l)], 'col', F32)
        small['mem_norm'][l] = rms_bwd(f"l{l}_mem_dnorm", mem, p['mem_norm'][l], dmem_n, None)[2]

        if l % 2 == 0:
            x_in, h, z, gate, bias = rec['mix']
            name = f"l{l}_gmlp"
            w_in, w_out = st.w[('gmlp_w_in', j)], st.w[('gmlp_w_out', j)]
            dgate = mm_nt(name + "_dgate", dxb, w_out, 'row', BF16)
            d_wout = mm_tn(name + "_dwout", gate, dxb, 'row')
            dmix, dws, dbs, dlg, dlb = gmlp_bwd(name + "_dgate_core", z, dgate, p['gmlp_ln_g'][j], p['gmlp_ln_b'][j],
                                                p['gmlp_w_s'][j], bias)
            small['gmlp_w_s'][j], small['gmlp_b_s'][j] = dws, dbs[:, :GMLP_GROUPS].T
            small['gmlp_ln_g'][j], small['gmlp_ln_b'][j] = dlg, dlb
            keys = [('gmlp_w_in', j), ('gmlp_w_out', j)]
        else:
            x_in, h, bcv, gate = rec['mix']
            name = f"l{l}_conv"
            w_in, w_out = st.w[('conv_w_in', j)], st.w[('conv_w_out', j)]
            dgate = mm_nt(name + "_dgate", dxb, w_out, 'row', BF16)
            d_wout = mm_tn(name + "_dwout", gate, dxb, 'row')
            dmix, dcw = conv_bwd(name + "_dgate_core", bcv, dgate, conv_w_full[j])
            small['conv_w'][j] = dcw[:cwid]
            keys = [('conv_w_in', j), ('conv_w_out', j)]
        d_win = st.carrying_scatter(mm_tn, name + "_dwin", h, dmix, 'col')
        st.reduce_begin(name, keys, [d_win, d_wout])
        dh = st.carrying_scatter(mm_nt, name + "_dh", dmix, w_in, 'col', BF16)
        dx, dxb, small['mix_norm'][l] = rms_bwd(f"l{l}_mix_dnorm", x_in, p['mix_norm'][l], dh, dx)

        dx, dxb, small['ffn1_norm'][l] = st.ffn_bwd("ffn1", l, dx, dxb, rec['ffn1'], p['ffn1_norm'][l])
    st.joins_alone("join_last")

    small_names = ['ffn1_norm', 'mix_norm', 'xattn_norm', 'mem_norm', 'ffn2_norm', 'gmlp_ln_g', 'gmlp_ln_b', 'gmlp_w_s',
                   'gmlp_b_s', 'final_norm', 'conv_w']
    small_full = {n: jnp.stack([g.reshape(p[n].shape[1:]) for g in small[n]]) for n in small_names
                  if n not in ('final_norm', 'conv_w')}
    small_full['final_norm'] = d_final.reshape(p['final_norm'].shape)
    small_full['conv_w'] = jnp.stack(small['conv_w'])
    packed = jnp.concatenate([small_full[n].reshape(-1, LANE) for n in small_names], axis=0)
    total = sum_leading("small_sum", gather_all("small_gather", packed))
    small_g, at = {}, 0
    for n in small_names:
        rows = small_full[n].size // LANE
        small_g[n] = total[at:at + rows].reshape(small_full[n].shape)
        at += rows
    small_g['conv_w'] = lax.dynamic_slice_in_dim(small_g['conv_w'], st.chip * dq4, dq4, axis=2)

    grads, deltas, new_m, new_v = {}, {}, {}, {}
    for n in WEIGHTS:
        w, m, v = p[n], p['m_' + n], p['v_' + n]
        if n in BIG:
            carried = None
            for i in range(w.shape[0]):
                carried = adamw_layer(f"adamw_{n}{i}", w, m, v, st.big_g[(n, i)], i, carried)
            grads[n], deltas[n], new_m[n], new_v[n] = carried
        else:
            g = small_g[n]
            out = adamw_flat(f"adamw_{n}", _as_rows(w), _as_rows(m), _as_rows(v), _as_rows(g))
            grads[n] = g
            deltas[n], new_m[n], new_v[n] = (o.reshape(w.shape) for o in out)

    grad_x = dx.reshape(p['x'].shape)
    return (loss, grad_x, *[grads[n] for n in WEIGHTS], *[deltas[n] for n in WEIGHTS], *[new_m[n] for n in WEIGHTS],
            *[new_v[n] for n in WEIGHTS])
```

```python
from typing import Callable, NamedTuple

import jax
import jax.numpy as jnp
from jax import lax
from jax.experimental import pallas as pl
from jax.experimental.pallas import tpu as pltpu

F32 = jnp.float32
BF16 = jnp.bfloat16
MESH = pl.DeviceIdType.MESH

CHUNK = 128
GMLP_GROUPS = 8
XATTN_HEADS = 4
RMS_EPS = 1e-6
LN_EPS = 1e-5
ADAM_LR = 0.001
ADAM_B1 = 0.9
ADAM_B2 = 0.999
ADAM_EPS = 1e-08
ADAM_WD = 0.01
ADAM_STEP = 10

N_CHIPS = 4
N_DEV = 8

VMEM_LIMIT_BYTES = 58 * 2**20
VMEM_PLAN_BYTES = 48 * 2**20
LANE = 128
MXU_DIM = 256
ACC_CHUNK = 2 * MXU_DIM
MXU_FLOPS_PER_US = 996e6
HBM_BYTES_PER_US = 3.3e6
STEP_US = 0.35
ACC_US_PER_VREG = 0.58e-3

WEIGHTS = ['ffn1_norm', 'ffn1_w13', 'ffn1_w2', 'mix_norm', 'gmlp_w_in', 'gmlp_ln_g', 'gmlp_ln_b', 'gmlp_w_s',
           'gmlp_b_s', 'gmlp_w_out', 'conv_w_in', 'conv_w', 'conv_w_out', 'xattn_norm', 'mem_norm', 'xattn_wq',
           'xattn_wkv', 'xattn_wo', 'ffn2_norm', 'ffn2_w13', 'ffn2_w2', 'final_norm']
BIG = {'ffn1_w13': 'col', 'ffn1_w2': 'row', 'gmlp_w_in': 'col', 'gmlp_w_out': 'row', 'conv_w_in': 'col',
       'conv_w_out': 'row', 'xattn_wq': 'row', 'xattn_wkv': 'col', 'xattn_wo': 'row', 'ffn2_w13': 'col',
       'ffn2_w2': 'row'}
ARG_NAMES = (['x', 'mem'] + WEIGHTS + ['loss_target'] + ['m_' + n for n in WEIGHTS] + ['v_' + n for n in WEIGHTS])


def _params(**kw):
    return pltpu.CompilerParams(vmem_limit_bytes=VMEM_LIMIT_BYTES, **kw)


def _divisors(n, mult, cap):
    return [d for d in range(mult, min(n, cap) + 1, mult) if n % d == 0] or [n]


def _row_tile(rows, width_bytes, budget=4 * 2**20):
    best = None
    for d in _divisors(rows, 16, 1024):
        if d * width_bytes <= budget:
            best = d
    return best or _divisors(rows, 16, 1024)[0]


ANY = pl.BlockSpec(memory_space=pl.ANY)


class Exchange(NamedTuple):
    inputs: list
    out_shapes: list
    aliases: dict
    n_sems: int
    start: Callable
    finish: Callable


def _place():
    x, y, c = lax.axis_index("x"), lax.axis_index("y"), lax.axis_index("c")
    chips = [(1 - x, y), (x, 1 - y), (1 - x, 1 - y)]
    return x, y, c, 2 * x + y, chips


def _remote(src, dst, sems, k, device):
    return pltpu.make_async_remote_copy(src_ref=src, dst_ref=dst, send_sem=sems[0].at[k], recv_sem=sems[1].at[k],
                                        device_id=device, device_id_type=MESH)


def _mxu_fill(dim):
    return dim / (-(-dim // MXU_DIM) * MXU_DIM)


def _tile_time(flops, fill, traffic, steps, acc_vregs):
    return (max(flops / (MXU_FLOPS_PER_US * fill), traffic / HBM_BYTES_PER_US) + steps * STEP_US
            + steps * acc_vregs * ACC_US_PER_VREG)


def _plan_mm(m, n_tiles_of, k_tiles_of, n, k, a_item, o_item, has_res, a_arrays=1):
    best, best_cost = None, None
    for tm in _divisors(m, 16, 1024):
        for tn in n_tiles_of:
            for tk in k_tiles_of:
                ni, nj, nk = m // tm, n // tn, k // tk
                blocks = a_arrays * tm * tk * a_item + tk * tn * 2 + tm * tn * o_item + (tm * tn * 4 if has_res else 0)
                vmem = 2 * blocks + tm * tn * 4 * (2 if nk > 1 else 1)
                if vmem > VMEM_PLAN_BYTES:
                    continue
                traffic = nj * m * k * a_item + (k * n * 2 if nk == 1 else ni * k * n * 2)
                traffic += m * n * (o_item + (4 if has_res else 0))
                cost = _tile_time(2 * m * n * k, _mxu_fill(tk) * _mxu_fill(tn), traffic, ni * nj * nk,
                                  tm * tn // 1024 if nk > 1 else 0)
                if best_cost is None or cost < best_cost:
                    best, best_cost = (tm, tn, tk), cost
    assert best is not None, (m, n, k)
    return best


def _tiled_call(name, grid, operands, in_specs, out_shapes, out_specs, scratch, compute, exchange=None, aliases=None):
    n_reg, n_out, n_scr = len(operands), len(out_shapes), len(scratch)
    n_xin = len(exchange.inputs) if exchange else 0
    n_xout = len(exchange.out_shapes) if exchange else 0
    semantics = ("arbitrary",) * len(grid)

    def body(*refs):
        ins = refs[:n_reg]
        outs = refs[n_reg + n_xin:n_reg + n_xin + n_out]
        scr = refs[n_reg + n_xin + n_out + n_xout:n_reg + n_xin + n_out + n_xout + n_scr]
        if not exchange:
            compute(ins, outs, scr)
            return
        x_ins = refs[n_reg:n_reg + n_xin]
        x_outs = refs[n_reg + n_xin + n_out:n_reg + n_xin + n_out + n_xout]
        sems = refs[-2:]
        at_first, at_last = True, True
        for k, extent in enumerate(grid):
            at_first = jnp.logical_and(at_first, pl.program_id(k) == 0)
            at_last = jnp.logical_and(at_last, pl.program_id(k) == extent - 1)

        @pl.when(at_first)
        def _():
            exchange.start(x_ins, x_outs, sems)

        compute(ins, outs, scr)

        @pl.when(at_last)
        def _():
            exchange.finish(x_ins, x_outs, sems)

    if not exchange:
        return pl.pallas_call(
            body, name=name, grid=grid, in_specs=in_specs, out_specs=out_specs, out_shape=out_shapes,
            scratch_shapes=scratch, input_output_aliases=dict(aliases or {}),
            compiler_params=_params(dimension_semantics=semantics),
        )(*operands)
    assert not aliases
    sems = [pltpu.SemaphoreType.DMA((exchange.n_sems,)), pltpu.SemaphoreType.DMA((exchange.n_sems,))]
    got = pl.pallas_call(
        body, name=name, grid=grid, in_specs=in_specs + [ANY] * n_xin, out_specs=out_specs + [ANY] * n_xout,
        out_shape=out_shapes + list(exchange.out_shapes), scratch_shapes=scratch + sems,
        input_output_aliases={n_reg + i: n_out + o for i, o in exchange.aliases.items()},
        compiler_params=_params(dimension_semantics=semantics),
    )(*operands, *exchange.inputs)
    return list(got[:n_out]), list(got[n_out:])


class Split(NamedTuple):
    slot: int
    other: jax.Array
    spec: pl.BlockSpec
    use_other: Callable


def _mm_call(name, grid, operands, in_specs, out_shape, out_spec, contract, nk, scale, has_res, tile, exchange=None,
             split=None, into=None, normed=False):
    n_main = len(operands)
    out_shapes, out_specs = [out_shape], [out_spec]
    if normed:
        out_shapes, out_specs = out_shapes + [jax.ShapeDtypeStruct(out_shape.shape, BF16)], out_specs + [out_spec]

    def compute(ins, outs, scr):
        res_ref = ins[2] if has_res else None
        o_ref = outs[0]
        acc_ref = scr[0] if nk > 1 else None

        def finish(v):
            if scale != 1.0:
                v = v * scale
            if has_res:
                v = res_ref[...] + v
            o_ref[...] = v.astype(o_ref.dtype)
            if normed:
                outs[1][...] = _rms_rows(v, ins[2 + has_res][...])[2].astype(BF16)

        def b_block(cols):
            ref = b_ref_of[0]
            stacked = len(ref.shape) == 3
            if contract[0][1] == (1,):
                if not stacked:
                    return ref[cols, :]
                return ref[...].reshape(ref.shape[0] * ref.shape[1], ref.shape[2])[cols, :]
            if not stacked:
                return ref[:, cols]
            b = ref[:, :, cols]
            return b.reshape(b.shape[0] * b.shape[1], b.shape[2])

        b_ref_of = [None]

        def contribute(a_ref, b_ref):
            b_ref_of[0] = b_ref
            if nk == 1:
                finish(lax.dot_general(a_ref[...], b_block(slice(None)), contract, preferred_element_type=F32))
                return
            kk = pl.program_id(2)

            @pl.when(kk == 0)
            def _():
                acc_ref[...] = jnp.zeros(tile, F32)

            a = a_ref[...]
            for start in range(0, tile[1], ACC_CHUNK):
                cols = slice(start, min(start + ACC_CHUNK, tile[1]))
                acc_ref[:, cols] += lax.dot_general(a, b_block(cols), contract, preferred_element_type=F32)

            @pl.when(kk == nk - 1)
            def _():
                finish(acc_ref[...])

        if split is None:
            contribute(ins[0], ins[1])
            return
        use_other = split.use_other(pl.program_id(0), pl.program_id(1), pl.program_id(2))
        pair = [ins[0], ins[1]]
        other = list(pair)
        other[split.slot] = ins[n_main]

        @pl.when(jnp.logical_not(use_other))
        def _():
            contribute(*pair)

        @pl.when(use_other)
        def _():
            contribute(*other)

    aliases = None
    if split is not None:
        operands, in_specs = operands + [split.other], in_specs + [split.spec]
    if into is not None:
        aliases = {len(operands): 0}
        operands, in_specs = operands + [into], in_specs + [ANY]
    got = _tiled_call(name, grid, operands, in_specs, out_shapes, out_specs,
                      [pltpu.VMEM(tile, F32)] if nk > 1 else [], compute, exchange, aliases)
    results, carried = (got[0], got[1]) if exchange else (got, None)
    out = tuple(results) if normed else results[0]
    return (out, carried) if exchange else out


def mm_nn(name, a, w, kind, out_dtype, res=None, scale=1.0, exchange=None, norm_gain=None):
    m, k = a.shape
    p, r, c = w.shape
    n = p * c if kind == 'col' else c
    assert k == (r if kind == 'col' else p * r), (name, a.shape, w.shape)
    n_tiles = _divisors(c, LANE, 2816)
    if norm_gain is not None:
        assert kind == 'row'
        n_tiles = [n]
    k_tiles = _divisors(r, LANE, 4096)
    if kind == 'row':
        k_tiles = k_tiles + [q * r for q in (2, 4) if p % q == 0]
    o_item = jnp.dtype(out_dtype).itemsize + (2 if norm_gain is not None else 0)
    tm, tn, tk = _plan_mm(m, n_tiles, k_tiles, n, k, a.dtype.itemsize, o_item, res is not None)
    nk = k // tk
    if kind == 'col':
        cpt = c // tn
        w_spec = pl.BlockSpec((None, tk, tn), lambda j, i, kk: (j // cpt, kk, j % cpt))
    elif tk > r:
        w_spec = pl.BlockSpec((tk // r, r, tn), lambda j, i, kk: (kk, 0, j))
    else:
        rpt = r // tk
        w_spec = pl.BlockSpec((None, tk, tn), lambda j, i, kk: (kk // rpt, kk % rpt, j))
    in_specs = [pl.BlockSpec((tm, tk), lambda j, i, kk: (i, kk)), w_spec]
    operands = [a, w]
    if res is not None:
        in_specs.append(pl.BlockSpec((tm, tn), lambda j, i, kk: (i, j)))
        operands.append(res)
    if norm_gain is not None:
        in_specs.append(pl.BlockSpec((1, tn), lambda j, i, kk: (0, 0)))
        operands.append(norm_gain.reshape(1, n))
    return _mm_call(name, (n // tn, m // tm, nk), operands, in_specs, jax.ShapeDtypeStruct((m, n), out_dtype),
                    pl.BlockSpec((tm, tn), lambda j, i, kk: (i, j)), (((1,), (0,)), ((), ())), nk, scale,
                    res is not None, (tm, tn), exchange, normed=norm_gain is not None)


def mm_nt(name, a, w, kind, out_dtype, scale=1.0, exchange=None, a_hi=None):
    m, kc = a.shape
    if a_hi is not None:
        assert a_hi.shape == a.shape
        kc = 2 * kc
    p, r, c = w.shape
    n = r if kind == 'col' else p * r
    assert kc == (p * c if kind == 'col' else c), (name, a.shape, w.shape)
    n_tiles = _divisors(r, LANE, 2816)
    k_tiles = _divisors(c, LANE, 4096)
    if kind == 'row':
        n_tiles = n_tiles + [q * r for q in (2, 4) if p % q == 0 and q * r <= 2816]
    tm, tn, tk = _plan_mm(m, n_tiles, k_tiles, n, kc, a.dtype.itemsize, jnp.dtype(out_dtype).itemsize, False,
                          1 if a_hi is None else 2)
    nk = kc // tk
    if kind == 'col':
        cpt = c // tk
        w_spec = pl.BlockSpec((None, tn, tk), lambda j, i, kk: (kk // cpt, j, kk % cpt))
    elif tn > r:
        w_spec = pl.BlockSpec((tn // r, r, tk), lambda j, i, kk: (j, 0, kk))
    else:
        rpt = r // tn
        w_spec = pl.BlockSpec((None, tn, tk), lambda j, i, kk: (j // rpt, j % rpt, kk))
    split = None
    a_spec = pl.BlockSpec((tm, tk), lambda j, i, kk: (i, kk))
    if a_hi is not None:
        half = nk // 2
        assert nk % 2 == 0
        a_spec = pl.BlockSpec((tm, tk), lambda j, i, kk: (i, jnp.minimum(kk, half - 1)))
        split = Split(0, a_hi, pl.BlockSpec((tm, tk), lambda j, i, kk: (i, jnp.maximum(kk - half, 0))),
                      lambda j, i, kk: kk >= half)
    return _mm_call(name, (n // tn, m // tm, nk), [a, w], [a_spec, w_spec], jax.ShapeDtypeStruct((m, n), out_dtype),
                    pl.BlockSpec((tm, tn), lambda j, i, kk: (i, j)), (((1,), (1,)), ((), ())), nk, scale, False,
                    (tm, tn), exchange, split)


def _plan_tn(s, ka, nd, r_tiles, n_tiles):
    best, best_cost = None, None
    for ts in _divisors(s, 16, 2048):
        for tr in r_tiles:
            for tn in n_tiles:
                ni, nj, ns = ka // tr, nd // tn, s // ts
                vmem = 2 * (ts * tr * 2 + ts * tn * 2 + tr * tn * 2) + tr * tn * 4 * (2 if ns > 1 else 1)
                if vmem > VMEM_PLAN_BYTES:
                    continue
                traffic = nj * s * ka * 2 + ni * s * nd * 2 + ka * nd * 2
                cost = _tile_time(2 * s * ka * nd, _mxu_fill(ts) * _mxu_fill(tn), traffic, ni * nj * ns,
                                  tr * tn // 1024 if ns > 1 else 0)
                if best_cost is None or cost < best_cost:
                    best, best_cost = (ts, tr, tn), cost
    assert best is not None, (s, ka, nd)
    return best


def mm_tn(name, a, dy, kind, scale=1.0, exchange=None, panels=(0, N_CHIPS), into=None):
    s, ka = a.shape
    s2, nd = dy.shape
    assert s == s2
    p = N_CHIPS
    first_panel, n_panels = panels
    assert kind == 'col' or panels == (0, p)
    r, c = (ka, nd // n_panels) if kind == 'col' else (ka // p, nd)
    ts, tr, tn = _plan_tn(s, ka, nd, _divisors(r, LANE, 2048), _divisors(c, LANE, 2816))
    ns = s // ts
    if kind == 'col':
        cpt = c // tn
        o_spec = pl.BlockSpec((None, tr, tn), lambda j, i, kk: (first_panel + j // cpt, i, j % cpt))
    else:
        rpt = r // tr
        o_spec = pl.BlockSpec((None, tr, tn), lambda j, i, kk: (i // rpt, i % rpt, j))
    in_specs = [pl.BlockSpec((ts, tr), lambda j, i, kk: (kk, i)), pl.BlockSpec((ts, tn), lambda j, i, kk: (kk, j))]
    return _mm_call(name, (nd // tn, ka // tr, ns), [a, dy], in_specs, jax.ShapeDtypeStruct((p, r, c), BF16), o_spec,
                    (((0,), (0,)), ((), ())), ns, scale, False, (tr, tn), exchange, None, into)


def _plan_fused(m, k, f, tiles, n_w, n_io):
    best, best_cost = None, None
    for tm in _divisors(m, 16, 1024):
        for tn in tiles:
            vmem = 2 * (tm * k * 2 + n_w * k * tn * 2 + n_io * tm * tn * 2) + 4 * tm * tn * 4
            if vmem > VMEM_PLAN_BYTES:
                continue
            traffic = (f // tn) * m * k * 2 + n_w * k * f * 2 + n_io * m * f * 2
            cost = _tile_time(2 * m * k * f * n_w, _mxu_fill(tn), traffic, (f // tn) * (m // tm), 0)
            if best_cost is None or cost < best_cost:
                best, best_cost = (tm, tn), cost
    assert best is not None, (m, k, f)
    return best


def mm_swiglu(name, h, w13, exchange=None):
    m, k = h.shape
    p, r, c = w13.shape
    assert r == k and p % 2 == 0
    f = p * c // 2
    tm, tn = _plan_fused(m, k, f, _divisors(c, LANE, 2816), 2, 3)
    cpt = c // tn

    def compute(ins, outs, scr):
        a = ins[0][...]
        g = jnp.dot(a, ins[1][...], preferred_element_type=F32)
        u = jnp.dot(a, ins[2][...], preferred_element_type=F32)
        outs[0][...] = g.astype(BF16)
        outs[1][...] = u.astype(BF16)
        outs[2][...] = (g * _sigmoid(g) * u).astype(BF16)

    tile = pl.BlockSpec((tm, tn), lambda j, i: (i, j))
    in_specs = [pl.BlockSpec((tm, k), lambda j, i: (i, 0)),
                pl.BlockSpec((None, k, tn), lambda j, i: (j // cpt, 0, j % cpt)),
                pl.BlockSpec((None, k, tn), lambda j, i: (j // cpt + p // 2, 0, j % cpt))]
    shape = jax.ShapeDtypeStruct((m, f), BF16)
    got = _tiled_call(name, (f // tn, m // tm), [h, w13, w13], in_specs, [shape] * 3, [tile] * 3, [], compute, exchange)
    return got


def mm_dswiglu(name, dy, w2, gate, up, scale, exchange=None):
    m, k = dy.shape
    p, r, c = w2.shape
    assert c == k
    f = p * r
    tm, tn = _plan_fused(m, k, f, _divisors(r, LANE, 2816), 1, 4)
    rpt = r // tn

    def compute(ins, outs, scr):
        d = lax.dot_general(ins[0][...], ins[1][...], (((1,), (1,)), ((), ())), preferred_element_type=F32) * scale
        g = ins[2][...].astype(F32)
        sg = _sigmoid(g)
        outs[0][...] = (d * ins[3][...].astype(F32) * (sg * (1.0 + g * (1.0 - sg)))).astype(BF16)
        outs[1][...] = (d * (g * sg)).astype(BF16)

    tile = pl.BlockSpec((tm, tn), lambda j, i: (i, j))
    in_specs = [pl.BlockSpec((tm, k), lambda j, i: (i, 0)),
                pl.BlockSpec((None, tn, k), lambda j, i: (j // rpt, j % rpt, 0)), tile, tile]
    shape = jax.ShapeDtypeStruct((m, f), BF16)
    return _tiled_call(name, (f // tn, m // tm), [dy, w2, gate, up], in_specs, [shape] * 2, [tile] * 2, [], compute,
                       exchange)


def _rms_rows(x, g):
    r = lax.rsqrt(jnp.mean(x * x, axis=-1, keepdims=True) + RMS_EPS)
    xhat = x * r
    return xhat, r, xhat * g


def rms_fwd(name, x, g):
    s, d = x.shape
    tm = _row_tile(s, d * 4)

    def body(x_ref, g_ref, o_ref):
        o_ref[...] = _rms_rows(x_ref[...], g_ref[...])[2].astype(BF16)

    return pl.pallas_call(
        body, name=name, grid=(s // tm,),
        in_specs=[pl.BlockSpec((tm, d), lambda i: (i, 0)), pl.BlockSpec((1, d), lambda i: (0, 0))],
        out_specs=pl.BlockSpec((tm, d), lambda i: (i, 0)), out_shape=jax.ShapeDtypeStruct((s, d), BF16),
        compiler_params=_params(dimension_semantics=("arbitrary",)),
    )(x, g.reshape(1, d))


def _rms_bwd_rows(x, g, dh):
    xhat, r, _ = _rms_rows(x, g)
    u = dh * g
    dx = r * (u - xhat * jnp.mean(u * xhat, axis=-1, keepdims=True))
    return dx, jnp.sum(dh * xhat, axis=0, keepdims=True)


def rms_bwd(name, x, g, dh, dres):
    s, d = x.shape
    tm = _row_tile(s, d * 4, 2 * 2**20)
    has_res = dres is not None

    def body(*refs):
        x_ref, g_ref, dh_ref = refs[:3]
        dres_ref = refs[3] if has_res else None
        dx_ref, dxb_ref, dg_ref = refs[-3:]
        dx, dg = _rms_bwd_rows(x_ref[...], g_ref[...], dh_ref[...].astype(F32))
        if has_res:
            dx = dx + dres_ref[...]
        dx_ref[...] = dx
        dxb_ref[...] = dx.astype(BF16)

        @pl.when(pl.program_id(0) == 0)
        def _():
            dg_ref[...] = dg

        @pl.when(pl.program_id(0) > 0)
        def _():
            dg_ref[...] += dg

    row = pl.BlockSpec((tm, d), lambda i: (i, 0))
    vec = pl.BlockSpec((1, d), lambda i: (0, 0))
    return pl.pallas_call(
        body, name=name, grid=(s // tm,),
        in_specs=[row, vec, row] + ([row] if has_res else []),
        out_specs=[row, row, vec],
        out_shape=[jax.ShapeDtypeStruct((s, d), F32), jax.ShapeDtypeStruct((s, d), BF16),
                   jax.ShapeDtypeStruct((1, d), F32)],
        compiler_params=_params(dimension_semantics=("arbitrary",)),
    )(x, g.reshape(1, d), dh, *([dres] if has_res else []))


def loss_head(name, x, g, target):
    s, d = x.shape
    tm = _row_tile(s, d * 4, 2 * 2**20)

    def body(x_ref, g_ref, t_ref, dx_ref, dxb_ref, dg_ref, loss_ref):
        x = x_ref[...]
        gain = g_ref[...]
        y = _rms_rows(x, gain)[2]
        diff = y - t_ref[...]
        dx, dg = _rms_bwd_rows(x, gain, diff * (1.0 / d))
        dx_ref[...] = dx
        dxb_ref[...] = dx.astype(BF16)
        sq = jnp.sum(diff * diff, axis=0, keepdims=True)

        @pl.when(pl.program_id(0) == 0)
        def _():
            dg_ref[...] = dg
            loss_ref[...] = sq

        @pl.when(pl.program_id(0) > 0)
        def _():
            dg_ref[...] += dg
            loss_ref[...] += sq

    row = pl.BlockSpec((tm, d), lambda i: (i, 0))
    vec = pl.BlockSpec((1, d), lambda i: (0, 0))
    return pl.pallas_call(
        body, name=name, grid=(s // tm,), in_specs=[row, vec, row], out_specs=[row, row, vec, vec],
        out_shape=[jax.ShapeDtypeStruct((s, d), F32), jax.ShapeDtypeStruct((s, d), BF16),
                   jax.ShapeDtypeStruct((1, d), F32), jax.ShapeDtypeStruct((1, d), F32)],
        compiler_params=_params(dimension_semantics=("arbitrary",)),
    )(x, g.reshape(1, d), target)


def _sigmoid(x):
    return 0.5 * jnp.tanh(0.5 * x) + 0.5


_INV_SQRT2 = 0.7071067811865476
_INV_SQRT_2PI = 0.3989422804014327


def _normal_cdf(z):
    return 0.5 * (1.0 + lax.erf(z * _INV_SQRT2))


def _gelu_grad(z, cdf):
    return cdf + z * (_INV_SQRT_2PI * jnp.exp(-0.5 * z * z))


def _causal_weights(ws_ref, g):
    t = ws_ref.shape[-1]
    keep = lax.broadcasted_iota(jnp.int32, (t, t), 0) >= lax.broadcasted_iota(jnp.int32, (t, t), 1)
    return jnp.where(keep, ws_ref[g], 0.0).astype(BF16), keep


def _gmlp_gate_rows(z_ref, lg_ref, lb_ref, e):
    z = z_ref[...].astype(F32)
    cdf = _normal_cdf(z)
    gz = z * cdf
    u, v = gz[:, :e], gz[:, e:]
    mu = jnp.mean(v, axis=-1, keepdims=True)
    xc = v - mu
    rs = lax.rsqrt(jnp.mean(xc * xc, axis=-1, keepdims=True) + LN_EPS)
    vhat = xc * rs
    return (z, cdf), u, vhat, rs, vhat * lg_ref[...] + lb_ref[...]


def gmlp_fwd(name, z, ln_g, ln_b, w_s, bias):
    s, e2 = z.shape
    e = e2 // 2
    eg = e // GMLP_GROUPS

    def body(z_ref, lg_ref, lb_ref, ws_ref, b_ref, o_ref):
        _, u, _, _, vln = _gmlp_gate_rows(z_ref, lg_ref, lb_ref, e)
        vb = vln.astype(BF16)
        for g in range(GMLP_GROUPS):
            cols = slice(g * eg, (g + 1) * eg)
            wm, _ = _causal_weights(ws_ref, g)
            f = jnp.dot(wm, vb[:, cols], preferred_element_type=F32) + b_ref[:, cols]
            o_ref[:, cols] = (u[:, cols] * f).astype(BF16)

    full = lambda shape: pl.BlockSpec(shape, lambda i: (0,) * len(shape))
    return pl.pallas_call(
        body, name=name, grid=(s // CHUNK,),
        in_specs=[pl.BlockSpec((CHUNK, e2), lambda i: (i, 0)), full((1, e)), full((1, e)),
                  full((GMLP_GROUPS, CHUNK, CHUNK)), full((CHUNK, e))],
        out_specs=pl.BlockSpec((CHUNK, e), lambda i: (i, 0)), out_shape=jax.ShapeDtypeStruct((s, e), BF16),
        compiler_params=_params(dimension_semantics=("arbitrary",)),
    )(z, ln_g.reshape(1, e), ln_b.reshape(1, e), w_s, bias)


def gmlp_bwd(name, z, dp, ln_g, ln_b, w_s, bias):
    s, e2 = z.shape
    e = e2 // 2
    eg = e // GMLP_GROUPS
    t = CHUNK

    def body(z_ref, dp_ref, lg_ref, lb_ref, ws_ref, b_ref, dz_ref, dws_ref, dbs_ref, dlg_ref, dlb_ref):
        first = pl.program_id(0) == 0
        (zf, cdf), u, vhat, rs, vln = _gmlp_gate_rows(z_ref, lg_ref, lb_ref, e)
        vb = vln.astype(BF16)
        dp = dp_ref[...].astype(F32)
        lane = lax.broadcasted_iota(jnp.int32, (t, LANE), 1)
        dbs = jnp.zeros((t, LANE), F32)
        dvln_parts = []
        for g in range(GMLP_GROUPS):
            cols = slice(g * eg, (g + 1) * eg)
            wm, keep = _causal_weights(ws_ref, g)
            f = jnp.dot(wm, vb[:, cols], preferred_element_type=F32) + b_ref[:, cols]
            dz_ref[:, cols] = (dp[:, cols] * f * _gelu_grad(zf[:, cols], cdf[:, cols])).astype(BF16)
            df = dp[:, cols] * u[:, cols]
            dfb = df.astype(BF16)
            dbs = dbs + jnp.where(lane == g, jnp.sum(df, axis=-1, keepdims=True), 0.0)
            dw = lax.dot_general(dfb, vb[:, cols], (((1,), (1,)), ((), ())), preferred_element_type=F32)
            dw = jnp.where(keep, dw, 0.0)

            @pl.when(first)
            def _():
                dws_ref[g] = dw

            @pl.when(jnp.logical_not(first))
            def _():
                dws_ref[g] += dw

            dvln_parts.append(lax.dot_general(wm, dfb, (((0,), (0,)), ((), ())), preferred_element_type=F32))
        dvln = jnp.concatenate(dvln_parts, axis=-1)
        dvhat = dvln * lg_ref[...]
        dv = rs * (dvhat - jnp.mean(dvhat, axis=-1, keepdims=True)
                   - vhat * jnp.mean(dvhat * vhat, axis=-1, keepdims=True))
        dz_ref[:, e:] = (dv * _gelu_grad(zf[:, e:], cdf[:, e:])).astype(BF16)
        dlg = jnp.sum(dvln * vhat, axis=0, keepdims=True)
        dlb = jnp.sum(dvln, axis=0, keepdims=True)

        @pl.when(first)
        def _():
            dbs_ref[...] = dbs
            dlg_ref[...] = dlg
            dlb_ref[...] = dlb

        @pl.when(jnp.logical_not(first))
        def _():
            dbs_ref[...] += dbs
            dlg_ref[...] += dlg
            dlb_ref[...] += dlb

    full = lambda shape: pl.BlockSpec(shape, lambda i: (0,) * len(shape))
    return pl.pallas_call(
        body, name=name, grid=(s // t,),
        in_specs=[pl.BlockSpec((t, e2), lambda i: (i, 0)), pl.BlockSpec((t, e), lambda i: (i, 0)), full((1, e)),
                  full((1, e)), full((GMLP_GROUPS, t, t)), full((t, e))],
        out_specs=[pl.BlockSpec((t, e2), lambda i: (i, 0)), full((GMLP_GROUPS, t, t)), full((t, LANE)), full((1, e)),
                   full((1, e))],
        out_shape=[jax.ShapeDtypeStruct((s, e2), BF16), jax.ShapeDtypeStruct((GMLP_GROUPS, t, t), F32),
                   jax.ShapeDtypeStruct((t, LANE), F32), jax.ShapeDtypeStruct((1, e), F32),
                   jax.ShapeDtypeStruct((1, e), F32)],
        compiler_params=_params(dimension_semantics=("arbitrary",)),
    )(z, dp, ln_g.reshape(1, e), ln_b.reshape(1, e), w_s, bias)


EDGE = 16


def _shift_down(zc, prev, k):
    tm = zc.shape[0]
    row = lax.broadcasted_iota(jnp.int32, (tm, 1), 0)
    out = pltpu.roll(zc, k, 0)
    for j in range(k):
        out = jnp.where(row == j, prev[EDGE - k + j:EDGE - k + j + 1, :], out)
    return out


def _shift_up(dc, nxt, k):
    tm = dc.shape[0]
    row = lax.broadcasted_iota(jnp.int32, (tm, 1), 0)
    out = pltpu.roll(dc, tm - k, 0)
    for j in range(k):
        out = jnp.where(row == tm - k + j, nxt[j:j + 1, :], out)
    return out


def conv_fwd(name, bcv, cw):
    s, d3 = bcv.shape
    d = d3 // 3
    tm = _row_tile(s, d * 4, 2 * 2**20)
    per = tm // EDGE

    def body(b_ref, c_ref, v_ref, cp_ref, vp_ref, w_ref, o_ref):
        i = pl.program_id(0)
        zc = c_ref[...].astype(F32) * v_ref[...].astype(F32)
        prev = jnp.where(i > 0, cp_ref[...].astype(F32) * vp_ref[...].astype(F32), 0.0)
        conv = w_ref[2:3, :] * zc + w_ref[1:2, :] * _shift_down(zc, prev, 1) + w_ref[0:1, :] * _shift_down(zc, prev, 2)
        o_ref[...] = (b_ref[...].astype(F32) * conv).astype(BF16)

    blk = lambda col: pl.BlockSpec((tm, d), lambda i: (i, col))
    edge = lambda col: pl.BlockSpec((EDGE, d), lambda i: (jnp.maximum(i * per - 1, 0), col))
    return pl.pallas_call(
        body, name=name, grid=(s // tm,),
        in_specs=[blk(0), blk(1), blk(2), edge(1), edge(2), pl.BlockSpec((3, d), lambda i: (0, 0))],
        out_specs=pl.BlockSpec((tm, d), lambda i: (i, 0)), out_shape=jax.ShapeDtypeStruct((s, d), BF16),
        compiler_params=_params(dimension_semantics=("arbitrary",)),
    )(bcv, bcv, bcv, bcv, bcv, cw)


def conv_bwd(name, bcv, dq, cw):
    s, d3 = bcv.shape
    d = d3 // 3
    tm = _row_tile(s, d * 4, 2**20)
    per = tm // EDGE
    n_tiles = s // tm
    last_edge = s // EDGE - 1

    def body(b_ref, c_ref, v_ref, cp_ref, vp_ref, bn_ref, dq_ref, dqn_ref, w_ref, o_ref, dw_ref):
        i = pl.program_id(0)
        b = b_ref[...].astype(F32)
        c = c_ref[...].astype(F32)
        v = v_ref[...].astype(F32)
        dq = dq_ref[...].astype(F32)
        zc = c * v
        prev = jnp.where(i > 0, cp_ref[...].astype(F32) * vp_ref[...].astype(F32), 0.0)
        z1 = _shift_down(zc, prev, 1)
        z2 = _shift_down(zc, prev, 2)
        w0, w1, w2 = w_ref[0:1, :], w_ref[1:2, :], w_ref[2:3, :]
        conv = w2 * zc + w1 * z1 + w0 * z2
        dconv = dq * b
        nxt = jnp.where(i < n_tiles - 1, dqn_ref[...].astype(F32) * bn_ref[...].astype(F32), 0.0)
        dz = w2 * dconv + w1 * _shift_up(dconv, nxt, 1) + w0 * _shift_up(dconv, nxt, 2)
        o_ref[:, :d] = (dq * conv).astype(BF16)
        o_ref[:, d:2 * d] = (dz * v).astype(BF16)
        o_ref[:, 2 * d:] = (dz * c).astype(BF16)
        dw = jnp.concatenate([jnp.sum(dconv * z2, axis=0, keepdims=True), jnp.sum(dconv * z1, axis=0, keepdims=True),
                              jnp.sum(dconv * zc, axis=0, keepdims=True), jnp.zeros((5, d), F32)], axis=0)

        @pl.when(i == 0)
        def _():
            dw_ref[...] = dw

        @pl.when(i > 0)
        def _():
            dw_ref[...] += dw

    blk = lambda col: pl.BlockSpec((tm, d), lambda i: (i, col))
    before = lambda col: pl.BlockSpec((EDGE, d), lambda i: (jnp.maximum(i * per - 1, 0), col))
    after = lambda col: pl.BlockSpec((EDGE, d), lambda i: (jnp.minimum((i + 1) * per, last_edge), col))
    return pl.pallas_call(
        body, name=name, grid=(n_tiles,),
        in_specs=[blk(0), blk(1), blk(2), before(1), before(2), after(0), blk(0), after(0),
                  pl.BlockSpec((3, d), lambda i: (0, 0))],
        out_specs=[pl.BlockSpec((tm, d3), lambda i: (i, 0)), pl.BlockSpec((8, d), lambda i: (0, 0))],
        out_shape=[jax.ShapeDtypeStruct((s, d3), BF16), jax.ShapeDtypeStruct((8, d), F32)],
        compiler_params=_params(dimension_semantics=("arbitrary",)),
    )(bcv, bcv, bcv, bcv, bcv, bcv, dq, dq, cw)


def _attn_probs(qh, kh, scale):
    sc = lax.dot_general(qh, kh, (((1,), (1,)), ((), ())), preferred_element_type=F32) * scale
    ex = jnp.exp(sc - jnp.max(sc, axis=-1, keepdims=True))
    return ex / jnp.sum(ex, axis=-1, keepdims=True)


def attn_fwd(name, q, kv):
    s, d = q.shape
    mlen = kv.shape[0]
    dh = d // XATTN_HEADS
    scale = dh ** -0.5
    tm = _row_tile(s, d * 4, 4 * 2**20)

    def body(q_ref, kv_ref, o_ref):
        for h in range(XATTN_HEADS):
            cols = slice(h * dh, (h + 1) * dh)
            p = _attn_probs(q_ref[:, cols], kv_ref[:, cols], scale)
            o_ref[:, cols] = jnp.dot(p.astype(BF16), kv_ref[:, d + h * dh:d + (h + 1) * dh],
                                     preferred_element_type=F32).astype(BF16)

    return pl.pallas_call(
        body, name=name, grid=(s // tm,),
        in_specs=[pl.BlockSpec((tm, d), lambda i: (i, 0)), pl.BlockSpec((mlen, 2 * d), lambda i: (0, 0))],
        out_specs=pl.BlockSpec((tm, d), lambda i: (i, 0)), out_shape=jax.ShapeDtypeStruct((s, d), BF16),
        compiler_params=_params(dimension_semantics=("arbitrary",)),
    )(q, kv)


def attn_bwd(name, q, kv, do):
    s, d = q.shape
    mlen = kv.shape[0]
    dh = d // XATTN_HEADS
    scale = dh ** -0.5
    tm = _row_tile(s, d * 4, 4 * 2**20)

    def body(q_ref, kv_ref, do_ref, dq_ref, dkv_ref):
        first = pl.program_id(0) == 0
        for h in range(XATTN_HEADS):
            cols = slice(h * dh, (h + 1) * dh)
            vcols = slice(d + h * dh, d + (h + 1) * dh)
            qh, kh, vh, doh = q_ref[:, cols], kv_ref[:, cols], kv_ref[:, vcols], do_ref[:, cols]
            p = _attn_probs(qh, kh, scale)
            dp = lax.dot_general(doh, vh, (((1,), (1,)), ((), ())), preferred_element_type=F32)
            ds = (p * (dp - jnp.sum(dp * p, axis=-1, keepdims=True)) * scale).astype(BF16)
            dq_ref[:, cols] = jnp.dot(ds, kh, preferred_element_type=F32).astype(BF16)
            dk = lax.dot_general(ds, qh, (((0,), (0,)), ((), ())), preferred_element_type=F32)
            dv = lax.dot_general(p.astype(BF16), doh, (((0,), (0,)), ((), ())), preferred_element_type=F32)

            @pl.when(first)
            def _():
                dkv_ref[:, cols] = dk
                dkv_ref[:, vcols] = dv

            @pl.when(jnp.logical_not(first))
            def _():
                dkv_ref[:, cols] += dk
                dkv_ref[:, vcols] += dv

    row = pl.BlockSpec((tm, d), lambda i: (i, 0))
    whole = pl.BlockSpec((mlen, 2 * d), lambda i: (0, 0))
    return pl.pallas_call(
        body, name=name, grid=(s // tm,), in_specs=[row, whole, row], out_specs=[row, whole],
        out_shape=[jax.ShapeDtypeStruct((s, d), BF16), jax.ShapeDtypeStruct((mlen, 2 * d), F32)],
        compiler_params=_params(dimension_semantics=("arbitrary",)),
    )(q, kv, do)


def _as_rows(a):
    if a.ndim >= 2 and a.shape[-1] % LANE == 0:
        return a.reshape(-1, a.shape[-1])
    return a.reshape(-1, LANE) if a.size % LANE == 0 else a.reshape(1, -1)


def add_halves(name, dw, other, core):
    p, r, c = dw.shape
    h = r // 2
    th = _row_tile(h, c * 2, 4 * 2**20)

    def body(core_ref, a_ref, b_ref, o_ref):
        o_ref[...] = (a_ref[...].astype(F32) + b_ref[...].astype(F32)).astype(BF16)

    grid_spec = pltpu.PrefetchScalarGridSpec(
        num_scalar_prefetch=1, grid=(p, h // th),
        in_specs=[pl.BlockSpec((None, None, th, c), lambda pi, i, core_ref: (pi, core_ref[0], i, 0)),
                  pl.BlockSpec((None, th, c), lambda pi, i, core_ref: (pi, i, 0))],
        out_specs=pl.BlockSpec((None, th, c), lambda pi, i, core_ref: (pi, i, 0)))
    return pl.pallas_call(
        body, name=name, grid_spec=grid_spec, out_shape=jax.ShapeDtypeStruct((p, h, c), BF16),
        compiler_params=_params(dimension_semantics=("arbitrary", "arbitrary")),
    )(core, dw.reshape(p, 2, h, c), other)


def sum_leading(name, parts):
    n, r, c = parts.shape
    tr = _row_tile(r, c * 4 * 2, 2 * 2**20)

    def body(p_ref, o_ref):
        acc = p_ref[0].astype(F32)
        for k in range(1, n):
            acc = acc + p_ref[k].astype(F32)
        o_ref[...] = acc

    return pl.pallas_call(
        body, name=name, grid=(r // tr,), in_specs=[pl.BlockSpec((n, tr, c), lambda i: (0, i, 0))],
        out_specs=pl.BlockSpec((tr, c), lambda i: (i, 0)), out_shape=jax.ShapeDtypeStruct((r, c), F32),
        compiler_params=_params(dimension_semantics=("arbitrary",)),
    )(parts)


def _adamw_rows(w, g, m, v):
    m = ADAM_B1 * m + (1.0 - ADAM_B1) * g
    v = ADAM_B2 * v + (1.0 - ADAM_B2) * (g * g)
    m_hat = m / (1.0 - ADAM_B1 ** ADAM_STEP)
    v_hat = v / (1.0 - ADAM_B2 ** ADAM_STEP)
    delta = -ADAM_LR * (m_hat / (jnp.sqrt(v_hat) + ADAM_EPS) + ADAM_WD * w)
    return delta, m, v


def adamw_layer(name, w, m, v, g, layer, carried):
    nl, r, c = w.shape
    tr = _row_tile(r, c * 4, 3 * 2**19)
    n_carried = 4 if carried is not None else 0

    def body(*refs):
        w_ref, m_ref, v_ref, g_ref = refs[:4]
        go_ref, d_ref, mo_ref, vo_ref = refs[4 + n_carried:]
        g = g_ref[...]
        delta, m_new, v_new = _adamw_rows(w_ref[...], g, m_ref[...], v_ref[...])
        go_ref[...] = g
        d_ref[...] = delta
        mo_ref[...] = m_new
        vo_ref[...] = v_new

    stacked = pl.BlockSpec((None, tr, c), lambda i: (layer, i, 0))
    in_specs = [stacked, stacked, stacked, pl.BlockSpec((tr, c), lambda i: (i, 0))]
    in_specs += [pl.BlockSpec(memory_space=pl.ANY)] * n_carried
    shape = jax.ShapeDtypeStruct((nl, r, c), F32)
    return pl.pallas_call(
        body, name=name, grid=(r // tr,), in_specs=in_specs, out_specs=[stacked] * 4, out_shape=[shape] * 4,
        input_output_aliases={4 + k: k for k in range(n_carried)},
        compiler_params=_params(dimension_semantics=("arbitrary",)),
    )(w, m, v, g, *(carried or ()))


def adamw_flat(name, w, m, v, g):
    r, c = w.shape

    def body(w_ref, m_ref, v_ref, g_ref, d_ref, mo_ref, vo_ref):
        delta, m_new, v_new = _adamw_rows(w_ref[...], g_ref[...], m_ref[...], v_ref[...])
        d_ref[...] = delta
        mo_ref[...] = m_new
        vo_ref[...] = v_new

    shape = jax.ShapeDtypeStruct((r, c), F32)
    return pl.pallas_call(body, name=name, out_shape=[shape] * 3, compiler_params=_params())(w, m, v, g)


def cast_place(name, w, layer, place):
    nl, r, c = w.shape
    tr = _row_tile(r, c * 4, 8 * 2**20)

    def body(x_ref, y_ref, c_ref, w_ref, o_ref):
        o_ref[...] = w_ref[...].astype(BF16)

    grid_spec = pltpu.PrefetchScalarGridSpec(
        num_scalar_prefetch=3, grid=(r // tr,),
        in_specs=[pl.BlockSpec((None, tr, c), lambda i, x_ref, y_ref, c_ref: (layer, i, 0))],
        out_specs=pl.BlockSpec((None, tr, c), lambda i, x_ref, y_ref, c_ref: (2 * x_ref[0] + y_ref[0], i, 0)))
    return pl.pallas_call(
        body, name=name, grid_spec=grid_spec, out_shape=jax.ShapeDtypeStruct((N_CHIPS, r, c), BF16),
        compiler_params=_params(dimension_semantics=("arbitrary",)),
    )(*place, w)


def reduce_sum4(name, own, landed, place):
    p, h, c = own.shape
    tr = _row_tile(h, c * 4, 4 * 2**20)

    def body(x_ref, y_ref, c_ref, t_ref, y1_ref, y2_ref, y3_ref, o_ref):
        acc = t_ref[...].astype(F32)
        for part_ref in (y1_ref, y2_ref, y3_ref):
            acc = acc + part_ref[...].astype(F32)
        o_ref[...] = acc

    def panel(fx, fy):
        return pl.BlockSpec((None, tr, c), lambda i, x_ref, y_ref, c_ref: (
            2 * (1 - x_ref[0] if fx else x_ref[0]) + (1 - y_ref[0] if fy else y_ref[0]), i, 0))

    grid_spec = pltpu.PrefetchScalarGridSpec(
        num_scalar_prefetch=3, grid=(h // tr,),
        in_specs=[panel(0, 0), panel(1, 0), panel(0, 1), panel(1, 1)],
        out_specs=pl.BlockSpec((None, tr, c), lambda i, x_ref, y_ref, c_ref: (c_ref[0], i, 0)))
    return pl.pallas_call(
        body, name=name, grid_spec=grid_spec, out_shape=jax.ShapeDtypeStruct((2, h, c), F32),
        compiler_params=_params(dimension_semantics=("arbitrary",)),
    )(*place, own, landed, landed, landed)


def run_exchange(name, exchange):
    n_in, n_out = len(exchange.inputs), len(exchange.out_shapes)

    def body(*refs):
        ins, outs, sems = refs[:n_in], refs[n_in:n_in + n_out], refs[n_in + n_out:]
        exchange.start(ins, outs, sems)
        exchange.finish(ins, outs, sems)

    return pl.pallas_call(
        body, name=name, in_specs=[ANY] * n_in, out_specs=[ANY] * n_out, out_shape=list(exchange.out_shapes),
        scratch_shapes=[pltpu.SemaphoreType.DMA((exchange.n_sems,)), pltpu.SemaphoreType.DMA((exchange.n_sems,))],
        input_output_aliases=dict(exchange.aliases),
    )(*exchange.inputs)


def _row_halves(ref, c):
    h = ref.shape[1] // 2
    return pl.ds(pl.multiple_of(c * h, 16), h), pl.ds(pl.multiple_of((1 - c) * h, 16), h)


def gather_exchange(fulls):
    n = len(fulls)

    def start(ins, outs, sems):
        x, y, c, mine, chips = _place()
        for a in range(n):
            rows = outs[a].at[mine, _row_halves(outs[a], c)[0]]
            for j, chip in enumerate(chips):
                _remote(rows, rows, sems, 6 * a + j, (*chip, c)).start()

    def finish(ins, outs, sems):
        x, y, c, mine, chips = _place()
        sibling = (x, y, 1 - c)
        for a in range(n):
            half = _row_halves(outs[a], c)[0]
            for j, chip in enumerate(chips):
                rows = outs[a].at[2 * chip[0] + chip[1], half]
                _remote(rows, rows, sems, 6 * a + j, sibling).wait_recv()
                _remote(rows, rows, sems, 6 * a + 3 + j, sibling).start()
        for a in range(n):
            half, other = _row_halves(outs[a], c)
            for j, chip in enumerate(chips):
                rows = outs[a].at[2 * chip[0] + chip[1], other]
                _remote(rows, rows, sems, 6 * a + 3 + j, sibling).wait_recv()
            for j, chip in enumerate(chips):
                rows = outs[a].at[mine, half]
                _remote(rows, rows, sems, 6 * a + j, (*chip, c)).wait_send()
                rows = outs[a].at[2 * chip[0] + chip[1], half]
                _remote(rows, rows, sems, 6 * a + 3 + j, sibling).wait_send()

    return Exchange(list(fulls), [jax.ShapeDtypeStruct(f.shape, f.dtype) for f in fulls], {a: a for a in range(n)},
                    6 * n, start, finish)


def swap_exchange(grads):
    n = len(grads)

    def copies(ins, outs, sems):
        x, y, c, _, _ = _place()
        return [_remote(ins[a].at[:, _row_halves(ins[a], c)[1]], outs[a], sems, a, (x, y, 1 - c)) for a in range(n)]

    def start(ins, outs, sems):
        for cp in copies(ins, outs, sems):
            cp.start()

    def finish(ins, outs, sems):
        for cp in copies(ins, outs, sems):
            cp.wait()

    shapes = [jax.ShapeDtypeStruct((g.shape[0], g.shape[1] // 2, g.shape[2]), g.dtype) for g in grads]
    return Exchange(list(grads), shapes, {}, n, start, finish)


def scatter_exchange(parts):
    n = len(parts)

    def sends(ins, outs, sems):
        x, y, c, mine, chips = _place()
        return [_remote(ins[a].at[2 * chip[0] + chip[1]], outs[a].at[mine], sems, 3 * a + j, (*chip, c))
                for a in range(n) for j, chip in enumerate(chips)]

    def start(ins, outs, sems):
        for cp in sends(ins, outs, sems):
            cp.start()

    def finish(ins, outs, sems):
        x, y, c, mine, chips = _place()
        for a in range(n):
            for j, chip in enumerate(chips):
                landing = outs[a].at[2 * chip[0] + chip[1]]
                _remote(landing, landing, sems, 3 * a + j, (*chip, c)).wait_recv()
        for cp in sends(ins, outs, sems):
            cp.wait_send()

    return Exchange(list(parts), [jax.ShapeDtypeStruct(g.shape, g.dtype) for g in parts], {}, 3 * n, start, finish)


def join_exchange(halves):
    n = len(halves)

    def start(ins, outs, sems):
        x, y, c, _, _ = _place()
        for a in range(n):
            _remote(outs[a].at[c], outs[a].at[c], sems, a, (x, y, 1 - c)).start()

    def finish(ins, outs, sems):
        x, y, c, _, _ = _place()
        for a in range(n):
            _remote(outs[a].at[1 - c], outs[a].at[1 - c], sems, a, (x, y, 1 - c)).wait_recv()
        for a in range(n):
            _remote(outs[a].at[c], outs[a].at[c], sems, a, (x, y, 1 - c)).wait_send()

    return Exchange(list(halves), [jax.ShapeDtypeStruct(g.shape, g.dtype) for g in halves], {a: a for a in range(n)},
                    n, start, finish)


def gather_all(name, rows):
    def body(in_ref, out_ref, send_sems, recv_sems, local_sem):
        sems = (send_sems, recv_sems)
        x, y, c, _, _ = _place()
        me = 4 * x + 2 * y + c
        local = pltpu.make_async_copy(in_ref, out_ref.at[me], local_sem)
        local.start()
        peers = [(1 - x if k & 4 else x, 1 - y if k & 2 else y, 1 - c if k & 1 else c) for k in range(1, N_DEV)]
        sent = []
        for k, peer in enumerate(peers):
            cp = _remote(in_ref, out_ref.at[me], sems, k, peer)
            cp.start()
            sent.append(cp)
        for k, peer in enumerate(peers):
            landing = out_ref.at[4 * peer[0] + 2 * peer[1] + peer[2]]
            _remote(landing, landing, sems, k, peer).wait_recv()
        for cp in sent:
            cp.wait_send()
        local.wait()

    return pl.pallas_call(
        body, name=name, in_specs=[ANY], out_specs=ANY,
        out_shape=jax.ShapeDtypeStruct((N_DEV,) + rows.shape, rows.dtype),
        scratch_shapes=[pltpu.SemaphoreType.DMA((N_DEV - 1,)), pltpu.SemaphoreType.DMA((N_DEV - 1,)),
                        pltpu.SemaphoreType.DMA],
    )(rows)


class _Step:
    def __init__(self, p):
        self.p = p
        xi, yi, ci = lax.axis_index("x"), lax.axis_index("y"), lax.axis_index("c")
        self.chip = 2 * xi + yi
        self.place_refs = tuple(v.astype(jnp.int32).reshape(1) for v in (xi, yi, ci))
        self.core_ref = self.place_refs[2]
        self.depth = p['ffn1_norm'].shape[0]
        self.placed, self.w, self.big_g = {}, {}, {}
        self.waiting_joins = []
        self.waiting_scatter = None

    def block_keys(self, tag, l):
        if l >= self.depth:
            return []
        mixer = ['gmlp_w_in', 'gmlp_w_out'] if l % 2 == 0 else ['conv_w_in', 'conv_w_out']
        names = {"ffn1": ['ffn1_w13', 'ffn1_w2'], "mix": mixer, "xattn": ['xattn_wq', 'xattn_wkv', 'xattn_wo'],
                 "ffn2": ['ffn2_w13', 'ffn2_w2']}[tag]
        return [(n, l // 2 if tag == "mix" else l) for n in names]

    def place(self, keys):
        for n, idx in keys:
            self.placed[(n, idx)] = cast_place(f"place_{n}{idx}", self.p[n], idx, self.place_refs)

    def gather_alone(self, name, keys):
        got = run_exchange(name, gather_exchange([self.placed[k] for k in keys]))
        self.w.update(zip(keys, got, strict=True))

    def carrying_gather(self, mm, keys, *args, **kw):
        keys = [k for k in keys if k in self.placed]
        if not keys:
            return mm(*args, **kw)
        out, got = mm(*args, exchange=gather_exchange([self.placed[k] for k in keys]), **kw)
        self.w.update(zip(keys, got, strict=True))
        return out

    def reduce_begin(self, tag, keys, dws):
        theirs = run_exchange(tag + "_swap", swap_exchange(dws))
        parts = [add_halves(f"{tag}_add{i}", dw, t, self.core_ref) for i, (dw, t) in enumerate(zip(dws, theirs, strict=True))]
        assert self.waiting_scatter is None
        self.waiting_scatter = (tag, keys, parts)

    def carrying_scatter(self, mm, *args, **kw):
        tag, keys, parts = self.waiting_scatter
        self.waiting_scatter = None
        out, landed = mm(*args, exchange=scatter_exchange(parts), **kw)
        halves = [reduce_sum4(f"{tag}_sum{i}", t, y, self.place_refs) for i, (t, y) in enumerate(zip(parts, landed, strict=True))]
        self.waiting_joins += list(zip(keys, halves, strict=True))
        return out

    def take_joined(self, keys, joined):
        for k, g in zip(keys, joined, strict=True):
            self.big_g[k] = g.reshape(-1, g.shape[-1])

    def carrying_joins(self, mm, *args, **kw):
        if not self.waiting_joins:
            return mm(*args, **kw)
        keys, halves = zip(*self.waiting_joins, strict=True)
        self.waiting_joins = []
        out, joined = mm(*args, exchange=join_exchange(list(halves)), **kw)
        self.take_joined(keys, joined)
        return out

    def joins_alone(self, name):
        keys, halves = zip(*self.waiting_joins, strict=True)
        self.waiting_joins = []
        self.take_joined(keys, run_exchange(name, join_exchange(list(halves))))

    def ffn_fwd(self, tag, l, x, gain, carry13, carry2, h=None):
        name = f"l{l}_{tag}"
        if h is None:
            h = rms_fwd(name + "_norm", x, gain)
        gate, up, act = self.carrying_gather(mm_swiglu, carry13, name + "_w13", h, self.w[(tag + '_w13', l)])
        out = self.carrying_gather(mm_nn, carry2, name + "_w2", act, self.w[(tag + '_w2', l)], 'row', F32, res=x,
                                   scale=0.5)
        return out, (x, h, gate, up, act)

    def ffn_bwd(self, tag, l, dx, dxb, saved, gain):
        w13, w2 = self.w[(tag + '_w13', l)], self.w[(tag + '_w2', l)]
        name = f"l{l}_{tag}"
        x, h, gate, up, act = saved
        d_gate, d_up = self.carrying_joins(mm_dswiglu, name + "_dact", dxb, w2, gate, up, 0.5)
        d_w2 = mm_tn(name + "_dw2", act, dxb, 'row', scale=0.5)
        half = N_CHIPS // 2
        d_w13 = mm_tn(name + "_dw13g", h, d_gate, 'col', panels=(0, half))
        d_w13 = mm_tn(name + "_dw13u", h, d_up, 'col', panels=(half, half), into=d_w13)
        self.reduce_begin(name, [(tag + '_w13', l), (tag + '_w2', l)], [d_w13, d_w2])
        dh = self.carrying_scatter(mm_nt, name + "_dh", d_gate, w13, 'col', BF16, a_hi=d_up)
        return rms_bwd(name + "_dnorm", x, gain, dh, dx)


def kernel(x, mem, ffn1_norm, ffn1_w13, ffn1_w2, mix_norm, gmlp_w_in, gmlp_ln_g, gmlp_ln_b, gmlp_w_s, gmlp_b_s, gmlp_w_out, conv_w_in, conv_w, conv_w_out, xattn_norm, mem_norm, xattn_wq, xattn_wkv, xattn_wo, ffn2_norm, ffn2_w13, ffn2_w2, final_norm, loss_target, m_ffn1_norm, m_ffn1_w13, m_ffn1_w2, m_mix_norm, m_gmlp_w_in, m_gmlp_ln_g, m_gmlp_ln_b, m_gmlp_w_s, m_gmlp_b_s, m_gmlp_w_out, m_conv_w_in, m_conv_w, m_conv_w_out, m_xattn_norm, m_mem_norm, m_xattn_wq, m_xattn_wkv, m_xattn_wo, m_ffn2_norm, m_ffn2_w13, m_ffn2_w2, m_final_norm, v_ffn1_norm, v_ffn1_w13, v_ffn1_w2, v_mix_norm, v_gmlp_w_in, v_gmlp_ln_g, v_gmlp_ln_b, v_gmlp_w_s, v_gmlp_b_s, v_gmlp_w_out, v_conv_w_in, v_conv_w, v_conv_w_out, v_xattn_norm, v_mem_norm, v_xattn_wq, v_xattn_wkv, v_xattn_wo, v_ffn2_norm, v_ffn2_w13, v_ffn2_w2, v_final_norm):
    return _step(dict(locals()))


def _step(p):
    assert sorted(p) == sorted(ARG_NAMES)
    st = _Step(p)
    x = p['x'][0]
    mem = p['mem'][0]
    target = p['loss_target'][0]
    s, d = x.shape
    depth = st.depth

    for l in range(depth):
        for tag in ("ffn1", "mix", "xattn", "ffn2"):
            st.place(st.block_keys(tag, l))
    st.gather_alone("gather_first", [('ffn1_w13', 0)])

    cw_local = p['conv_w']
    n_conv, cwid, dq4 = cw_local.shape
    cw_rows = jnp.pad(cw_local.reshape(-1, LANE), ((0, (-cw_local.size // LANE) % 8), (0, 0)))
    cw_all = gather_all("gather_conv_w", cw_rows)[0::2, :cw_local.size // LANE]
    conv_w_full = cw_all.reshape(N_CHIPS, n_conv, cwid, dq4).transpose(1, 2, 0, 3).reshape(n_conv, cwid, N_CHIPS * dq4)

    saved = []
    for l in range(depth):
        j = l // 2
        rec = {}
        first_w2 = [('ffn1_w2', 0)] if l == 0 else []
        x, rec['ffn1'] = st.ffn_fwd("ffn1", l, x, p['ffn1_norm'][l],
                                    first_w2 + st.block_keys("mix", l) + st.block_keys("xattn", l), [('ffn2_w13', l)])
        h = rms_fwd(f"l{l}_mix_norm", x, p['mix_norm'][l])
        if l % 2 == 0:
            e = p['gmlp_ln_g'].shape[-1]
            bias = jnp.repeat(p['gmlp_b_s'][j].T, e // GMLP_GROUPS, axis=1)
            z = st.carrying_gather(mm_nn, [('ffn2_w2', l)], f"l{l}_gmlp_in", h, st.w[('gmlp_w_in', j)], 'col', BF16)
            gate = gmlp_fwd(f"l{l}_gmlp_gate", z, p['gmlp_ln_g'][j], p['gmlp_ln_b'][j], p['gmlp_w_s'][j], bias)
            x_new, hq = mm_nn(f"l{l}_gmlp_out", gate, st.w[('gmlp_w_out', j)], 'row', F32, res=x,
                              norm_gain=p['xattn_norm'][l])
            rec['mix'] = (x, h, z, gate, bias)
        else:
            bcv = st.carrying_gather(mm_nn, [('ffn2_w2', l)], f"l{l}_conv_in", h, st.w[('conv_w_in', j)], 'col', BF16)
            gate = conv_fwd(f"l{l}_conv_gate", bcv, conv_w_full[j])
            x_new, hq = mm_nn(f"l{l}_conv_out", gate, st.w[('conv_w_out', j)], 'row', F32, res=x,
                              norm_gain=p['xattn_norm'][l])
            rec['mix'] = (x, h, bcv, gate)
        x = x_new
        q = mm_nn(f"l{l}_xattn_q", hq, st.w[('xattn_wq', l)], 'row', BF16)
        mem_n = rms_fwd(f"l{l}_mem_norm", mem, p['mem_norm'][l])
        kv = mm_nn(f"l{l}_xattn_kv", mem_n, st.w[('xattn_wkv', l)], 'col', BF16)
        o = attn_fwd(f"l{l}_xattn_core", q, kv)
        x_new, h_ffn2 = mm_nn(f"l{l}_xattn_o", o, st.w[('xattn_wo', l)], 'row', F32, res=x,
                              norm_gain=p['ffn2_norm'][l])
        rec['xattn'] = (x, hq, q, mem_n, kv, o)
        x = x_new
        x, rec['ffn2'] = st.ffn_fwd("ffn2", l, x, p['ffn2_norm'][l], st.block_keys("ffn1", l + 1), [], h=h_ffn2)
        saved.append(rec)

    dx, dxb, d_final, loss_lanes = loss_head("loss_head", x, p['final_norm'], target)
    loss = lax.psum(0.5 * jnp.sum(loss_lanes) / d, ("x", "y", "c"))

    small = {n: [None] * p[n].shape[0] for n in ('ffn1_norm', 'mix_norm', 'xattn_norm', 'mem_norm', 'ffn2_norm',
                                                  'gmlp_ln_g', 'gmlp_ln_b', 'gmlp_w_s', 'gmlp_b_s', 'conv_w')}
    for l in reversed(range(depth)):
        j = l // 2
        rec = saved[l]
        dx, dxb, small['ffn2_norm'][l] = st.ffn_bwd("ffn2", l, dx, dxb, rec['ffn2'], p['ffn2_norm'][l])

        x_in, hq, q, mem_n, kv, o = rec['xattn']
        name = f"l{l}_xattn"
        do = st.carrying_joins(mm_nt, name + "_do", dxb, st.w[('xattn_wo', l)], 'row', BF16)
        d_wo = mm_tn(name + "_dwo", o, dxb, 'row')
        dq, dkv = attn_bwd(name + "_dcore", q, kv, do)
        d_wq = mm_tn(name + "_dwq", hq, dq, 'row')
        dkvb = dkv.astype(BF16)
        d_wkv = mm_tn(name + "_dwkv", mem_n, dkvb, 'col')
        st.reduce_begin(name, [('xattn_wq', l), ('xattn_wkv', l), ('xattn_wo', l)], [d_wq, d_wkv, d_wo])
        dh = mm_nt(name + "_dh", dq, st.w[('xattn_wq', l)], 'row', BF16)
        dx, dxb, small['xattn_norm'][l] = rms_bwd(name + "_dnorm", x_in, p['xattn_norm'][l], dh, dx)
        dmem_n = mm_nt(name + "_dmem", dkvb, st.w[('xattn_wkv', l)], 'col', F32)
        small['mem_norm'][l] = rms_bwd(f"l{l}_mem_dnorm", mem, p['mem_norm'][l], dmem_n, None)[2]

        if l % 2 == 0:
            x_in, h, z, gate, bias = rec['mix']
            name = f"l{l}_gmlp"
            w_in, w_out = st.w[('gmlp_w_in', j)], st.w[('gmlp_w_out', j)]
            dgate = mm_nt(name + "_dgate", dxb, w_out, 'row', BF16)
            d_wout = mm_tn(name + "_dwout", gate, dxb, 'row')
            dmix, dws, dbs, dlg, dlb = gmlp_bwd(name + "_dgate_core", z, dgate, p['gmlp_ln_g'][j], p['gmlp_ln_b'][j],
                                                p['gmlp_w_s'][j], bias)
            small['gmlp_w_s'][j], small['gmlp_b_s'][j] = dws, dbs[:, :GMLP_GROUPS].T
            small['gmlp_ln_g'][j], small['gmlp_ln_b'][j] = dlg, dlb
            keys = [('gmlp_w_in', j), ('gmlp_w_out', j)]
        else:
            x_in, h, bcv, gate = rec['mix']
            name = f"l{l}_conv"
            w_in, w_out = st.w[('conv_w_in', j)], st.w[('conv_w_out', j)]
            dgate = mm_nt(name + "_dgate", dxb, w_out, 'row', BF16)
            d_wout = mm_tn(name + "_dwout", gate, dxb, 'row')
            dmix, dcw = conv_bwd(name + "_dgate_core", bcv, dgate, conv_w_full[j])
            small['conv_w'][j] = dcw[:cwid]
            keys = [('conv_w_in', j), ('conv_w_out', j)]
        d_win = st.carrying_scatter(mm_tn, name + "_dwin", h, dmix, 'col')
        st.reduce_begin(name, keys, [d_win, d_wout])
        dh = st.carrying_scatter(mm_nt, name + "_dh", dmix, w_in, 'col', BF16)
        dx, dxb, small['mix_norm'][l] = rms_bwd(f"l{l}_mix_dnorm", x_in, p['mix_norm'][l], dh, dx)

        dx, dxb, small['ffn1_norm'][l] = st.ffn_bwd("ffn1", l, dx, dxb, rec['ffn1'], p['ffn1_norm'][l])
    st.joins_alone("join_last")

    small_names = ['ffn1_norm', 'mix_norm', 'xattn_norm', 'mem_norm', 'ffn2_norm', 'gmlp_ln_g', 'gmlp_ln_b', 'gmlp_w_s',
                   'gmlp_b_s', 'final_norm', 'conv_w']
    small_full = {n: jnp.stack([g.reshape(p[n].shape[1:]) for g in small[n]]) for n in small_names
                  if n not in ('final_norm', 'conv_w')}
    small_full['final_norm'] = d_final.reshape(p['final_norm'].shape)
    small_full['conv_w'] = jnp.stack(small['conv_w'])
    packed = jnp.concatenate([small_full[n].reshape(-1, LANE) for n in small_names], axis=0)
    total = sum_leading("small_sum", gather_all("small_gather", packed))
    small_g, at = {}, 0
    for n in small_names:
        rows = small_full[n].size // LANE
        small_g[n] = total[at:at + rows].reshape(small_full[n].shape)
        at += rows
    small_g['conv_w'] = lax.dynamic_slice_in_dim(small_g['conv_w'], st.chip * dq4, dq4, axis=2)

    grads, deltas, new_m, new_v = {}, {}, {}, {}
    for n in WEIGHTS:
        w, m, v = p[n], p['m_' + n], p['v_' + n]
        if n in BIG:
            carried = None
            for i in range(w.shape[0]):
                carried = adamw_layer(f"adamw_{n}{i}", w, m, v, st.big_g[(n, i)], i, carried)
            grads[n], deltas[n], new_m[n], new_v[n] = carried
        else:
            g = small_g[n]
            out = adamw_flat(f"adamw_{n}", _as_rows(w), _as_rows(m), _as_rows(v), _as_rows(g))
            grads[n] = g
            deltas[n], new_m[n], new_v[n] = (o.reshape(w.shape) for o in out)

    grad_x = dx.reshape(p['x'].shape)
    return (loss, grad_x, *[grads[n] for n in WEIGHTS], *[deltas[n] for n in WEIGHTS], *[new_m[n] for n in WEIGHTS],
            *[new_v[n] for n in WEIGHTS])
```

```python
from typing import Callable, NamedTuple

import jax
import jax.numpy as jnp
from jax import lax
from jax.experimental import pallas as pl
from jax.experimental.pallas import tpu as pltpu

F32 = jnp.float32
BF16 = jnp.bfloat16
MESH = pl.DeviceIdType.MESH

CHUNK = 128
GMLP_GROUPS = 8
XATTN_HEADS = 4
RMS_EPS = 1e-6
LN_EPS = 1e-5
ADAM_LR = 0.001
ADAM_B1 = 0.9
ADAM_B2 = 0.999
ADAM_EPS = 1e-08
ADAM_WD = 0.01
ADAM_STEP = 10

N_CHIPS = 4
N_DEV = 8

VMEM_LIMIT_BYTES = 58 * 2**20
VMEM_PLAN_BYTES = 48 * 2**20
LANE = 128
MXU_DIM = 256
ACC_CHUNK = 2 * MXU_DIM
MXU_FLOPS_PER_US = 996e6
HBM_BYTES_PER_US = 3.3e6
STEP_US = 0.35
ACC_US_PER_VREG = 0.58e-3

WEIGHTS = ['ffn1_norm', 'ffn1_w13', 'ffn1_w2', 'mix_norm', 'gmlp_w_in', 'gmlp_ln_g', 'gmlp_ln_b', 'gmlp_w_s',
           'gmlp_b_s', 'gmlp_w_out', 'conv_w_in', 'conv_w', 'conv_w_out', 'xattn_norm', 'mem_norm', 'xattn_wq',
           'xattn_wkv', 'xattn_wo', 'ffn2_norm', 'ffn2_w13', 'ffn2_w2', 'final_norm']
BIG = {'ffn1_w13': 'col', 'ffn1_w2': 'row', 'gmlp_w_in': 'col', 'gmlp_w_out': 'row', 'conv_w_in': 'col',
       'conv_w_out': 'row', 'xattn_wq': 'row', 'xattn_wkv': 'col', 'xattn_wo': 'row', 'ffn2_w13': 'col',
       'ffn2_w2': 'row'}
ARG_NAMES = (['x', 'mem'] + WEIGHTS + ['loss_target'] + ['m_' + n for n in WEIGHTS] + ['v_' + n for n in WEIGHTS])


def _params(**kw):
    return pltpu.CompilerParams(vmem_limit_bytes=VMEM_LIMIT_BYTES, **kw)


def _divisors(n, mult, cap):
    return [d for d in range(mult, min(n, cap) + 1, mult) if n % d == 0] or [n]


def _row_tile(rows, width_bytes, budget=4 * 2**20):
    best = None
    for d in _divisors(rows, 16, 1024):
        if d * width_bytes <= budget:
            best = d
    return best or _divisors(rows, 16, 1024)[0]


ANY = pl.BlockSpec(memory_space=pl.ANY)


class Exchange(NamedTuple):
    inputs: list
    out_shapes: list
    aliases: dict
    n_sems: int
    start: Callable
    finish: Callable


def _place():
    x, y, c = lax.axis_index("x"), lax.axis_index("y"), lax.axis_index("c")
    chips = [(1 - x, y), (x, 1 - y), (1 - x, 1 - y)]
    return x, y, c, 2 * x + y, chips


def _remote(src, dst, sems, k, device):
    return pltpu.make_async_remote_copy(src_ref=src, dst_ref=dst, send_sem=sems[0].at[k], recv_sem=sems[1].at[k],
                                        device_id=device, device_id_type=MESH)


def _mxu_fill(dim):
    return dim / (-(-dim // MXU_DIM) * MXU_DIM)


def _tile_time(flops, fill, traffic, steps, acc_vregs):
    return (max(flops / (MXU_FLOPS_PER_US * fill), traffic / HBM_BYTES_PER_US) + steps * STEP_US
            + steps * acc_vregs * ACC_US_PER_VREG)


def _plan_mm(m, n_tiles_of, k_tiles_of, n, k, a_item, o_item, has_res, a_arrays=1):
    best, best_cost = None, None
    for tm in _divisors(m, 16, 1024):
        for tn in n_tiles_of:
            for tk in k_tiles_of:
                ni, nj, nk = m // tm, n // tn, k // tk
                blocks = a_arrays * tm * tk * a_item + tk * tn * 2 + tm * tn * o_item + (tm * tn * 4 if has_res else 0)
                vmem = 2 * blocks + tm * tn * 4 * (2 if nk > 1 else 1)
                if vmem > VMEM_PLAN_BYTES:
                    continue
                traffic = nj * m * k * a_item + (k * n * 2 if nk == 1 else ni * k * n * 2)
                traffic += m * n * (o_item + (4 if has_res else 0))
                cost = _tile_time(2 * m * n * k, _mxu_fill(tk) * _mxu_fill(tn), traffic, ni * nj * nk,
                                  tm * tn // 1024 if nk > 1 else 0)
                if best_cost is None or cost < best_cost:
                    best, best_cost = (tm, tn, tk), cost
    assert best is not None, (m, n, k)
    return best


def _tiled_call(name, grid, operands, in_specs, out_shapes, out_specs, scratch, compute, exchange=None, aliases=None):
    n_reg, n_out, n_scr = len(operands), len(out_shapes), len(scratch)
    n_xin = len(exchange.inputs) if exchange else 0
    n_xout = len(exchange.out_shapes) if exchange else 0
    semantics = ("arbitrary",) * len(grid)

    def body(*refs):
        ins = refs[:n_reg]
        outs = refs[n_reg + n_xin:n_reg + n_xin + n_out]
        scr = refs[n_reg + n_xin + n_out + n_xout:n_reg + n_xin + n_out + n_xout + n_scr]
        if not exchange:
            compute(ins, outs, scr)
            return
        x_ins = refs[n_reg:n_reg + n_xin]
        x_outs = refs[n_reg + n_xin + n_out:n_reg + n_xin + n_out + n_xout]
        sems = refs[-2:]
        at_first, at_last = True, True
        for k, extent in enumerate(grid):
            at_first = jnp.logical_and(at_first, pl.program_id(k) == 0)
            at_last = jnp.logical_and(at_last, pl.program_id(k) == extent - 1)

        @pl.when(at_first)
        def _():
            exchange.start(x_ins, x_outs, sems)

        compute(ins, outs, scr)

        @pl.when(at_last)
        def _():
            exchange.finish(x_ins, x_outs, sems)

    if not exchange:
        return pl.pallas_call(
            body, name=name, grid=grid, in_specs=in_specs, out_specs=out_specs, out_shape=out_shapes,
            scratch_shapes=scratch, input_output_aliases=dict(aliases or {}),
            compiler_params=_params(dimension_semantics=semantics),
        )(*operands)
    assert not aliases
    sems = [pltpu.SemaphoreType.DMA((exchange.n_sems,)), pltpu.SemaphoreType.DMA((exchange.n_sems,))]
    got = pl.pallas_call(
        body, name=name, grid=grid, in_specs=in_specs + [ANY] * n_xin, out_specs=out_specs + [ANY] * n_xout,
        out_shape=out_shapes + list(exchange.out_shapes), scratch_shapes=scratch + sems,
        input_output_aliases={n_reg + i: n_out + o for i, o in exchange.aliases.items()},
        compiler_params=_params(dimension_semantics=semantics),
    )(*operands, *exchange.inputs)
    return list(got[:n_out]), list(got[n_out:])


class Split(NamedTuple):
    slot: int
    other: jax.Array
    spec: pl.BlockSpec
    use_other: Callable


def _mm_call(name, grid, operands, in_specs, out_shape, out_spec, contract, nk, scale, has_res, tile, exchange=None,
             split=None, into=None, normed=False):
    n_main = len(operands)
    out_shapes, out_specs = [out_shape], [out_spec]
    if normed:
        out_shapes, out_specs = out_shapes + [jax.ShapeDtypeStruct(out_shape.shape, BF16)], out_specs + [out_spec]

    def compute(ins, outs, scr):
        res_ref = ins[2] if has_res else None
        o_ref = outs[0]
        acc_ref = scr[0] if nk > 1 else None

        def finish(v):
            if scale != 1.0:
                v = v * scale
            if has_res:
                v = res_ref[...] + v
            o_ref[...] = v.astype(o_ref.dtype)
            if normed:
                outs[1][...] = _rms_rows(v, ins[2 + has_res][...])[2].astype(BF16)

        def b_block(cols):
            ref = b_ref_of[0]
            stacked = len(ref.shape) == 3
            if contract[0][1] == (1,):
                if not stacked:
                    return ref[cols, :]
                return ref[...].reshape(ref.shape[0] * ref.shape[1], ref.shape[2])[cols, :]
            if not stacked:
                return ref[:, cols]
            b = ref[:, :, cols]
            return b.reshape(b.shape[0] * b.shape[1], b.shape[2])

        b_ref_of = [None]

        def contribute(a_ref, b_ref):
            b_ref_of[0] = b_ref
            if nk == 1:
                finish(lax.dot_general(a_ref[...], b_block(slice(None)), contract, preferred_element_type=F32))
                return
            kk = pl.program_id(2)

            @pl.when(kk == 0)
            def _():
                acc_ref[...] = jnp.zeros(tile, F32)

            a = a_ref[...]
            for start in range(0, tile[1], ACC_CHUNK):
                cols = slice(start, min(start + ACC_CHUNK, tile[1]))
                acc_ref[:, cols] += lax.dot_general(a, b_block(cols), contract, preferred_element_type=F32)

            @pl.when(kk == nk - 1)
            def _():
                finish(acc_ref[...])

        if split is None:
            contribute(ins[0], ins[1])
            return
        use_other = split.use_other(pl.program_id(0), pl.program_id(1), pl.program_id(2))
        pair = [ins[0], ins[1]]
        other = list(pair)
        other[split.slot] = ins[n_main]

        @pl.when(jnp.logical_not(use_other))
        def _():
            contribute(*pair)

        @pl.when(use_other)
        def _():
            contribute(*other)

    aliases = None
    if split is not None:
        operands, in_specs = operands + [split.other], in_specs + [split.spec]
    if into is not None:
        aliases = {len(operands): 0}
        operands, in_specs = operands + [into], in_specs + [ANY]
    got = _tiled_call(name, grid, operands, in_specs, out_shapes, out_specs,
                      [pltpu.VMEM(tile, F32)] if nk > 1 else [], compute, exchange, aliases)
    results, carried = (got[0], got[1]) if exchange else (got, None)
    out = tuple(results) if normed else results[0]
    return (out, carried) if exchange else out


def mm_nn(name, a, w, kind, out_dtype, res=None, scale=1.0, exchange=None, norm_gain=None):
    m, k = a.shape
    p, r, c = w.shape
    n = p * c if kind == 'col' else c
    assert k == (r if kind == 'col' else p * r), (name, a.shape, w.shape)
    n_tiles = _divisors(c, LANE, 2816)
    if norm_gain is not None:
        assert kind == 'row'
        n_tiles = [n]
    k_tiles = _divisors(r, LANE, 4096)
    if kind == 'row':
        k_tiles = k_tiles + [q * r for q in (2, 4) if p % q == 0]
    o_item = jnp.dtype(out_dtype).itemsize + (2 if norm_gain is not None else 0)
    tm, tn, tk = _plan_mm(m, n_tiles, k_tiles, n, k, a.dtype.itemsize, o_item, res is not None)
    nk = k // tk
    if kind == 'col':
        cpt = c // tn
        w_spec = pl.BlockSpec((None, tk, tn), lambda j, i, kk: (j // cpt, kk, j % cpt))
    elif tk > r:
        w_spec = pl.BlockSpec((tk // r, r, tn), lambda j, i, kk: (kk, 0, j))
    else:
        rpt = r // tk
        w_spec = pl.BlockSpec((None, tk, tn), lambda j, i, kk: (kk // rpt, kk % rpt, j))
    in_specs = [pl.BlockSpec((tm, tk), lambda j, i, kk: (i, kk)), w_spec]
    operands = [a, w]
    if res is not None:
        in_specs.append(pl.BlockSpec((tm, tn), lambda j, i, kk: (i, j)))
        operands.append(res)
    if norm_gain is not None:
        in_specs.append(pl.BlockSpec((1, tn), lambda j, i, kk: (0, 0)))
        operands.append(norm_gain.reshape(1, n))
    return _mm_call(name, (n // tn, m // tm, nk), operands, in_specs, jax.ShapeDtypeStruct((m, n), out_dtype),
                    pl.BlockSpec((tm, tn), lambda j, i, kk: (i, j)), (((1,), (0,)), ((), ())), nk, scale,
                    res is not None, (tm, tn), exchange, normed=norm_gain is not None)


def mm_nt(name, a, w, kind, out_dtype, scale=1.0, exchange=None, a_hi=None):
    m, kc = a.shape
    if a_hi is not None:
        assert a_hi.shape == a.shape
        kc = 2 * kc
    p, r, c = w.shape
    n = r if kind == 'col' else p * r
    assert kc == (p * c if kind == 'col' else c), (name, a.shape, w.shape)
    n_tiles = _divisors(r, LANE, 2816)
    k_tiles = _divisors(c, LANE, 4096)
    if kind == 'row':
        n_tiles = n_tiles + [q * r for q in (2, 4) if p % q == 0 and q * r <= 2816]
    tm, tn, tk = _plan_mm(m, n_tiles, k_tiles, n, kc, a.dtype.itemsize, jnp.dtype(out_dtype).itemsize, False,
                          1 if a_hi is None else 2)
    nk = kc // tk
    if kind == 'col':
        cpt = c // tk
        w_spec = pl.BlockSpec((None, tn, tk), lambda j, i, kk: (kk // cpt, j, kk % cpt))
    elif tn > r:
        w_spec = pl.BlockSpec((tn // r, r, tk), lambda j, i, kk: (j, 0, kk))
    else:
        rpt = r // tn
        w_spec = pl.BlockSpec((None, tn, tk), lambda j, i, kk: (j // rpt, j % rpt, kk))
    split = None
    a_spec = pl.BlockSpec((tm, tk), lambda j, i, kk: (i, kk))
    if a_hi is not None:
        half = nk // 2
        assert nk % 2 == 0
        a_spec = pl.BlockSpec((tm, tk), lambda j, i, kk: (i, jnp.minimum(kk, half - 1)))
        split = Split(0, a_hi, pl.BlockSpec((tm, tk), lambda j, i, kk: (i, jnp.maximum(kk - half, 0))),
                      lambda j, i, kk: kk >= half)
    return _mm_call(name, (n // tn, m // tm, nk), [a, w], [a_spec, w_spec], jax.ShapeDtypeStruct((m, n), out_dtype),
                    pl.BlockSpec((tm, tn), lambda j, i, kk: (i, j)), (((1,), (1,)), ((), ())), nk, scale, False,
                    (tm, tn), exchange, split)


def _plan_tn(s, ka, nd, r_tiles, n_tiles):
    best, best_cost = None, None
    for ts in _divisors(s, 16, 2048):
        for tr in r_tiles:
            for tn in n_tiles:
                ni, nj, ns = ka // tr, nd // tn, s // ts
                vmem = 2 * (ts * tr * 2 + ts * tn * 2 + tr * tn * 2) + tr * tn * 4 * (2 if ns > 1 else 1)
                if vmem > VMEM_PLAN_BYTES:
                    continue
                traffic = nj * s * ka * 2 + ni * s * nd * 2 + ka * nd * 2
                cost = _tile_time(2 * s * ka * nd, _mxu_fill(ts) * _mxu_fill(tn), traffic, ni * nj * ns,
                                  tr * tn // 1024 if ns > 1 else 0)
                if best_cost is None or cost < best_cost:
                    best, best_cost = (ts, tr, tn), cost
    assert best is not None, (s, ka, nd)
    return best


def mm_tn(name, a, dy, kind, scale=1.0, exchange=None, panels=(0, N_CHIPS), into=None):
    s, ka = a.shape
    s2, nd = dy.shape
    assert s == s2
    p = N_CHIPS
    first_panel, n_panels = panels
    assert kind == 'col' or panels == (0, p)
    r, c = (ka, nd // n_panels) if kind == 'col' else (ka // p, nd)
    ts, tr, tn = _plan_tn(s, ka, nd, _divisors(r, LANE, 2048), _divisors(c, LANE, 2816))
    ns = s // ts
    if kind == 'col':
        cpt = c // tn
        o_spec = pl.BlockSpec((None, tr, tn), lambda j, i, kk: (first_panel + j // cpt, i, j % cpt))
    else:
        rpt = r // tr
        o_spec = pl.BlockSpec((None, tr, tn), lambda j, i, kk: (i // rpt, i % rpt, j))
    in_specs = [pl.BlockSpec((ts, tr), lambda j, i, kk: (kk, i)), pl.BlockSpec((ts, tn), lambda j, i, kk: (kk, j))]
    return _mm_call(name, (nd // tn, ka // tr, ns), [a, dy], in_specs, jax.ShapeDtypeStruct((p, r, c), BF16), o_spec,
                    (((0,), (0,)), ((), ())), ns, scale, False, (tr, tn), exchange, None, into)


def _plan_fused(m, k, f, tiles, n_w, n_io):
    best, best_cost = None, None
    for tm in _divisors(m, 16, 1024):
        for tn in tiles:
            vmem = 2 * (tm * k * 2 + n_w * k * tn * 2 + n_io * tm * tn * 2) + 4 * tm * tn * 4
            if vmem > VMEM_PLAN_BYTES:
                continue
            traffic = (f // tn) * m * k * 2 + n_w * k * f * 2 + n_io * m * f * 2
            cost = _tile_time(2 * m * k * f * n_w, _mxu_fill(tn), traffic, (f // tn) * (m // tm), 0)
            if best_cost is None or cost < best_cost:
                best, best_cost = (tm, tn), cost
    assert best is not None, (m, k, f)
    return best


def mm_swiglu(name, h, w13, exchange=None):
    m, k = h.shape
    p, r, c = w13.shape
    assert r == k and p % 2 == 0
    f = p * c // 2
    tm, tn = _plan_fused(m, k, f, _divisors(c, LANE, 2816), 2, 3)
    cpt = c // tn

    def compute(ins, outs, scr):
        a = ins[0][...]
        g = jnp.dot(a, ins[1][...], preferred_element_type=F32)
        u = jnp.dot(a, ins[2][...], preferred_element_type=F32)
        sg = _sigmoid(g)
        silu = g * sg
        outs[0][...] = (u * (sg * (1.0 + g * (1.0 - sg)))).astype(BF16)
        outs[1][...] = silu.astype(BF16)
        outs[2][...] = (silu * u).astype(BF16)

    tile = pl.BlockSpec((tm, tn), lambda j, i: (i, j))
    in_specs = [pl.BlockSpec((tm, k), lambda j, i: (i, 0)),
                pl.BlockSpec((None, k, tn), lambda j, i: (j // cpt, 0, j % cpt)),
                pl.BlockSpec((None, k, tn), lambda j, i: (j // cpt + p // 2, 0, j % cpt))]
    shape = jax.ShapeDtypeStruct((m, f), BF16)
    got = _tiled_call(name, (f // tn, m // tm), [h, w13, w13], in_specs, [shape] * 3, [tile] * 3, [], compute, exchange)
    return got


def mm_dswiglu(name, dy, w2, by_gate, by_up, scale, exchange=None):
    m, k = dy.shape
    p, r, c = w2.shape
    assert c == k
    f = p * r
    tm, tn = _plan_fused(m, k, f, _divisors(r, LANE, 2816), 1, 4)
    rpt = r // tn

    def compute(ins, outs, scr):
        d = lax.dot_general(ins[0][...], ins[1][...], (((1,), (1,)), ((), ())), preferred_element_type=F32) * scale
        outs[0][...] = (d * ins[2][...].astype(F32)).astype(BF16)
        outs[1][...] = (d * ins[3][...].astype(F32)).astype(BF16)

    tile = pl.BlockSpec((tm, tn), lambda j, i: (i, j))
    in_specs = [pl.BlockSpec((tm, k), lambda j, i: (i, 0)),
                pl.BlockSpec((None, tn, k), lambda j, i: (j // rpt, j % rpt, 0)), tile, tile]
    shape = jax.ShapeDtypeStruct((m, f), BF16)
    return _tiled_call(name, (f // tn, m // tm), [dy, w2, by_gate, by_up], in_specs, [shape] * 2, [tile] * 2, [], compute,
                       exchange)


def _rms_rows(x, g):
    r = lax.rsqrt(jnp.mean(x * x, axis=-1, keepdims=True) + RMS_EPS)
    xhat = x * r
    return xhat, r, xhat * g


def rms_fwd(name, x, g):
    s, d = x.shape
    tm = _row_tile(s, d * 4)

    def body(x_ref, g_ref, o_ref):
        o_ref[...] = _rms_rows(x_ref[...], g_ref[...])[2].astype(BF16)

    return pl.pallas_call(
        body, name=name, grid=(s // tm,),
        in_specs=[pl.BlockSpec((tm, d), lambda i: (i, 0)), pl.BlockSpec((1, d), lambda i: (0, 0))],
        out_specs=pl.BlockSpec((tm, d), lambda i: (i, 0)), out_shape=jax.ShapeDtypeStruct((s, d), BF16),
        compiler_params=_params(dimension_semantics=("arbitrary",)),
    )(x, g.reshape(1, d))


def _rms_bwd_rows(x, g, dh):
    xhat, r, _ = _rms_rows(x, g)
    u = dh * g
    dx = r * (u - xhat * jnp.mean(u * xhat, axis=-1, keepdims=True))
    return dx, jnp.sum(dh * xhat, axis=0, keepdims=True)


def rms_bwd(name, x, g, dh, dres):
    s, d = x.shape
    tm = _row_tile(s, d * 4, 2 * 2**20)
    has_res = dres is not None

    def body(*refs):
        x_ref, g_ref, dh_ref = refs[:3]
        dres_ref = refs[3] if has_res else None
        dx_ref, dxb_ref, dg_ref = refs[-3:]
        dx, dg = _rms_bwd_rows(x_ref[...], g_ref[...], dh_ref[...].astype(F32))
        if has_res:
            dx = dx + dres_ref[...]
        dx_ref[...] = dx
        dxb_ref[...] = dx.astype(BF16)

        @pl.when(pl.program_id(0) == 0)
        def _():
            dg_ref[...] = dg

        @pl.when(pl.program_id(0) > 0)
        def _():
            dg_ref[...] += dg

    row = pl.BlockSpec((tm, d), lambda i: (i, 0))
    vec = pl.BlockSpec((1, d), lambda i: (0, 0))
    return pl.pallas_call(
        body, name=name, grid=(s // tm,),
        in_specs=[row, vec, row] + ([row] if has_res else []),
        out_specs=[row, row, vec],
        out_shape=[jax.ShapeDtypeStruct((s, d), F32), jax.ShapeDtypeStruct((s, d), BF16),
                   jax.ShapeDtypeStruct((1, d), F32)],
        compiler_params=_params(dimension_semantics=("arbitrary",)),
    )(x, g.reshape(1, d), dh, *([dres] if has_res else []))


def loss_head(name, x, g, target):
    s, d = x.shape
    tm = _row_tile(s, d * 4, 2 * 2**20)

    def body(x_ref, g_ref, t_ref, dx_ref, dxb_ref, dg_ref, loss_ref):
        x = x_ref[...]
        gain = g_ref[...]
        y = _rms_rows(x, gain)[2]
        diff = y - t_ref[...]
        dx, dg = _rms_bwd_rows(x, gain, diff * (1.0 / d))
        dx_ref[...] = dx
        dxb_ref[...] = dx.astype(BF16)
        sq = jnp.sum(diff * diff, axis=0, keepdims=True)

        @pl.when(pl.program_id(0) == 0)
        def _():
            dg_ref[...] = dg
            loss_ref[...] = sq

        @pl.when(pl.program_id(0) > 0)
        def _():
            dg_ref[...] += dg
            loss_ref[...] += sq

    row = pl.BlockSpec((tm, d), lambda i: (i, 0))
    vec = pl.BlockSpec((1, d), lambda i: (0, 0))
    return pl.pallas_call(
        body, name=name, grid=(s // tm,), in_specs=[row, vec, row], out_specs=[row, row, vec, vec],
        out_shape=[jax.ShapeDtypeStruct((s, d), F32), jax.ShapeDtypeStruct((s, d), BF16),
                   jax.ShapeDtypeStruct((1, d), F32), jax.ShapeDtypeStruct((1, d), F32)],
        compiler_params=_params(dimension_semantics=("arbitrary",)),
    )(x, g.reshape(1, d), target)


def _sigmoid(x):
    return 0.5 * jnp.tanh(0.5 * x) + 0.5


_INV_SQRT2 = 0.7071067811865476
_INV_SQRT_2PI = 0.3989422804014327


def _normal_cdf(z):
    return 0.5 * (1.0 + lax.erf(z * _INV_SQRT2))


def _gelu_grad(z, cdf):
    return cdf + z * (_INV_SQRT_2PI * jnp.exp(-0.5 * z * z))


def _causal_weights(ws_ref, g):
    t = ws_ref.shape[-1]
    keep = lax.broadcasted_iota(jnp.int32, (t, t), 0) >= lax.broadcasted_iota(jnp.int32, (t, t), 1)
    return jnp.where(keep, ws_ref[g], 0.0).astype(BF16), keep


def _gmlp_gate_rows(z_ref, lg_ref, lb_ref, e):
    z = z_ref[...].astype(F32)
    cdf = _normal_cdf(z)
    gz = z * cdf
    u, v = gz[:, :e], gz[:, e:]
    mu = jnp.mean(v, axis=-1, keepdims=True)
    xc = v - mu
    rs = lax.rsqrt(jnp.mean(xc * xc, axis=-1, keepdims=True) + LN_EPS)
    vhat = xc * rs
    return (z, cdf), u, vhat, rs, vhat * lg_ref[...] + lb_ref[...]


def gmlp_fwd(name, z, ln_g, ln_b, w_s, bias):
    s, e2 = z.shape
    e = e2 // 2
    eg = e // GMLP_GROUPS

    def body(z_ref, lg_ref, lb_ref, ws_ref, b_ref, o_ref):
        _, u, _, _, vln = _gmlp_gate_rows(z_ref, lg_ref, lb_ref, e)
        vb = vln.astype(BF16)
        for g in range(GMLP_GROUPS):
            cols = slice(g * eg, (g + 1) * eg)
            wm, _ = _causal_weights(ws_ref, g)
            f = jnp.dot(wm, vb[:, cols], preferred_element_type=F32) + b_ref[:, cols]
            o_ref[:, cols] = (u[:, cols] * f).astype(BF16)

    full = lambda shape: pl.BlockSpec(shape, lambda i: (0,) * len(shape))
    return pl.pallas_call(
        body, name=name, grid=(s // CHUNK,),
        in_specs=[pl.BlockSpec((CHUNK, e2), lambda i: (i, 0)), full((1, e)), full((1, e)),
                  full((GMLP_GROUPS, CHUNK, CHUNK)), full((CHUNK, e))],
        out_specs=pl.BlockSpec((CHUNK, e), lambda i: (i, 0)), out_shape=jax.ShapeDtypeStruct((s, e), BF16),
        compiler_params=_params(dimension_semantics=("arbitrary",)),
    )(z, ln_g.reshape(1, e), ln_b.reshape(1, e), w_s, bias)


def gmlp_bwd(name, z, dp, ln_g, ln_b, w_s, bias):
    s, e2 = z.shape
    e = e2 // 2
    eg = e // GMLP_GROUPS
    t = CHUNK

    def body(z_ref, dp_ref, lg_ref, lb_ref, ws_ref, b_ref, dz_ref, dws_ref, dbs_ref, dlg_ref, dlb_ref):
        first = pl.program_id(0) == 0
        (zf, cdf), u, vhat, rs, vln = _gmlp_gate_rows(z_ref, lg_ref, lb_ref, e)
        vb = vln.astype(BF16)
        dp = dp_ref[...].astype(F32)
        lane = lax.broadcasted_iota(jnp.int32, (t, LANE), 1)
        dbs = jnp.zeros((t, LANE), F32)
        dvln_parts = []
        for g in range(GMLP_GROUPS):
            cols = slice(g * eg, (g + 1) * eg)
            wm, keep = _causal_weights(ws_ref, g)
            f = jnp.dot(wm, vb[:, cols], preferred_element_type=F32) + b_ref[:, cols]
            dz_ref[:, cols] = (dp[:, cols] * f * _gelu_grad(zf[:, cols], cdf[:, cols])).astype(BF16)
            df = dp[:, cols] * u[:, cols]
            dfb = df.astype(BF16)
            dbs = dbs + jnp.where(lane == g, jnp.sum(df, axis=-1, keepdims=True), 0.0)
            dw = lax.dot_general(dfb, vb[:, cols], (((1,), (1,)), ((), ())), preferred_element_type=F32)
            dw = jnp.where(keep, dw, 0.0)

            @pl.when(first)
            def _():
                dws_ref[g] = dw

            @pl.when(jnp.logical_not(first))
            def _():
                dws_ref[g] += dw

            dvln_parts.append(lax.dot_general(wm, dfb, (((0,), (0,)), ((), ())), preferred_element_type=F32))
        dvln = jnp.concatenate(dvln_parts, axis=-1)
        dvhat = dvln * lg_ref[...]
        dv = rs * (dvhat - jnp.mean(dvhat, axis=-1, keepdims=True)
                   - vhat * jnp.mean(dvhat * vhat, axis=-1, keepdims=True))
        dz_ref[:, e:] = (dv * _gelu_grad(zf[:, e:], cdf[:, e:])).astype(BF16)
        dlg = jnp.sum(dvln * vhat, axis=0, keepdims=True)
        dlb = jnp.sum(dvln, axis=0, keepdims=True)

        @pl.when(first)
        def _():
            dbs_ref[...] = dbs
            dlg_ref[...] = dlg
            dlb_ref[...] = dlb

        @pl.when(jnp.logical_not(first))
        def _():
            dbs_ref[...] += dbs
            dlg_ref[...] += dlg
            dlb_ref[...] += dlb

    full = lambda shape: pl.BlockSpec(shape, lambda i: (0,) * len(shape))
    return pl.pallas_call(
        body, name=name, grid=(s // t,),
        in_specs=[pl.BlockSpec((t, e2), lambda i: (i, 0)), pl.BlockSpec((t, e), lambda i: (i, 0)), full((1, e)),
                  full((1, e)), full((GMLP_GROUPS, t, t)), full((t, e))],
        out_specs=[pl.BlockSpec((t, e2), lambda i: (i, 0)), full((GMLP_GROUPS, t, t)), full((t, LANE)), full((1, e)),
                   full((1, e))],
        out_shape=[jax.ShapeDtypeStruct((s, e2), BF16), jax.ShapeDtypeStruct((GMLP_GROUPS, t, t), F32),
                   jax.ShapeDtypeStruct((t, LANE), F32), jax.ShapeDtypeStruct((1, e), F32),
                   jax.ShapeDtypeStruct((1, e), F32)],
        compiler_params=_params(dimension_semantics=("arbitrary",)),
    )(z, dp, ln_g.reshape(1, e), ln_b.reshape(1, e), w_s, bias)


EDGE = 16


def _shift_down(zc, prev, k):
    tm = zc.shape[0]
    row = lax.broadcasted_iota(jnp.int32, (tm, 1), 0)
    out = pltpu.roll(zc, k, 0)
    for j in range(k):
        out = jnp.where(row == j, prev[EDGE - k + j:EDGE - k + j + 1, :], out)
    return out


def _shift_up(dc, nxt, k):
    tm = dc.shape[0]
    row = lax.broadcasted_iota(jnp.int32, (tm, 1), 0)
    out = pltpu.roll(dc, tm - k, 0)
    for j in range(k):
        out = jnp.where(row == tm - k + j, nxt[j:j + 1, :], out)
    return out


def conv_fwd(name, bcv, cw):
    s, d3 = bcv.shape
    d = d3 // 3
    tm = _row_tile(s, d * 4, 2 * 2**20)
    per = tm // EDGE

    def body(b_ref, c_ref, v_ref, cp_ref, vp_ref, w_ref, o_ref):
        i = pl.program_id(0)
        zc = c_ref[...].astype(F32) * v_ref[...].astype(F32)
        prev = jnp.where(i > 0, cp_ref[...].astype(F32) * vp_ref[...].astype(F32), 0.0)
        conv = w_ref[2:3, :] * zc + w_ref[1:2, :] * _shift_down(zc, prev, 1) + w_ref[0:1, :] * _shift_down(zc, prev, 2)
        o_ref[...] = (b_ref[...].astype(F32) * conv).astype(BF16)

    blk = lambda col: pl.BlockSpec((tm, d), lambda i: (i, col))
    edge = lambda col: pl.BlockSpec((EDGE, d), lambda i: (jnp.maximum(i * per - 1, 0), col))
    return pl.pallas_call(
        body, name=name, grid=(s // tm,),
        in_specs=[blk(0), blk(1), blk(2), edge(1), edge(2), pl.BlockSpec((3, d), lambda i: (0, 0))],
        out_specs=pl.BlockSpec((tm, d), lambda i: (i, 0)), out_shape=jax.ShapeDtypeStruct((s, d), BF16),
        compiler_params=_params(dimension_semantics=("arbitrary",)),
    )(bcv, bcv, bcv, bcv, bcv, cw)


def conv_bwd(name, bcv, dq, cw):
    s, d3 = bcv.shape
    d = d3 // 3
    tm = _row_tile(s, d * 4, 2**20)
    per = tm // EDGE
    n_tiles = s // tm
    last_edge = s // EDGE - 1

    def body(b_ref, c_ref, v_ref, cp_ref, vp_ref, bn_ref, dq_ref, dqn_ref, w_ref, o_ref, dw_ref):
        i = pl.program_id(0)
        b = b_ref[...].astype(F32)
        c = c_ref[...].astype(F32)
        v = v_ref[...].astype(F32)
        dq = dq_ref[...].astype(F32)
        zc = c * v
        prev = jnp.where(i > 0, cp_ref[...].astype(F32) * vp_ref[...].astype(F32), 0.0)
        z1 = _shift_down(zc, prev, 1)
        z2 = _shift_down(zc, prev, 2)
        w0, w1, w2 = w_ref[0:1, :], w_ref[1:2, :], w_ref[2:3, :]
        conv = w2 * zc + w1 * z1 + w0 * z2
        dconv = dq * b
        nxt = jnp.where(i < n_tiles - 1, dqn_ref[...].astype(F32) * bn_ref[...].astype(F32), 0.0)
        dz = w2 * dconv + w1 * _shift_up(dconv, nxt, 1) + w0 * _shift_up(dconv, nxt, 2)
        o_ref[:, :d] = (dq * conv).astype(BF16)
        o_ref[:, d:2 * d] = (dz * v).astype(BF16)
        o_ref[:, 2 * d:] = (dz * c).astype(BF16)
        dw = jnp.concatenate([jnp.sum(dconv * z2, axis=0, keepdims=True), jnp.sum(dconv * z1, axis=0, keepdims=True),
                              jnp.sum(dconv * zc, axis=0, keepdims=True), jnp.zeros((5, d), F32)], axis=0)

        @pl.when(i == 0)
        def _():
            dw_ref[...] = dw

        @pl.when(i > 0)
        def _():
            dw_ref[...] += dw

    blk = lambda col: pl.BlockSpec((tm, d), lambda i: (i, col))
    before = lambda col: pl.BlockSpec((EDGE, d), lambda i: (jnp.maximum(i * per - 1, 0), col))
    after = lambda col: pl.BlockSpec((EDGE, d), lambda i: (jnp.minimum((i + 1) * per, last_edge), col))
    return pl.pallas_call(
        body, name=name, grid=(n_tiles,),
        in_specs=[blk(0), blk(1), blk(2), before(1), before(2), after(0), blk(0), after(0),
                  pl.BlockSpec((3, d), lambda i: (0, 0))],
        out_specs=[pl.BlockSpec((tm, d3), lambda i: (i, 0)), pl.BlockSpec((8, d), lambda i: (0, 0))],
        out_shape=[jax.ShapeDtypeStruct((s, d3), BF16), jax.ShapeDtypeStruct((8, d), F32)],
        compiler_params=_params(dimension_semantics=("arbitrary",)),
    )(bcv, bcv, bcv, bcv, bcv, bcv, dq, dq, cw)


def _attn_probs(qh, kh, scale):
    sc = lax.dot_general(qh, kh, (((1,), (1,)), ((), ())), preferred_element_type=F32) * scale
    ex = jnp.exp(sc - jnp.max(sc, axis=-1, keepdims=True))
    return ex / jnp.sum(ex, axis=-1, keepdims=True)


def attn_fwd(name, q, kv):
    s, d = q.shape
    mlen = kv.shape[0]
    dh = d // XATTN_HEADS
    scale = dh ** -0.5
    tm = _row_tile(s, d * 4, 4 * 2**20)

    def body(q_ref, kv_ref, o_ref):
        for h in range(XATTN_HEADS):
            cols = slice(h * dh, (h + 1) * dh)
            p = _attn_probs(q_ref[:, cols], kv_ref[:, cols], scale)
            o_ref[:, cols] = jnp.dot(p.astype(BF16), kv_ref[:, d + h * dh:d + (h + 1) * dh],
                                     preferred_element_type=F32).astype(BF16)

    return pl.pallas_call(
        body, name=name, grid=(s // tm,),
        in_specs=[pl.BlockSpec((tm, d), lambda i: (i, 0)), pl.BlockSpec((mlen, 2 * d), lambda i: (0, 0))],
        out_specs=pl.BlockSpec((tm, d), lambda i: (i, 0)), out_shape=jax.ShapeDtypeStruct((s, d), BF16),
        compiler_params=_params(dimension_semantics=("arbitrary",)),
    )(q, kv)


def attn_bwd(name, q, kv, do):
    s, d = q.shape
    mlen = kv.shape[0]
    dh = d // XATTN_HEADS
    scale = dh ** -0.5
    tm = _row_tile(s, d * 4, 4 * 2**20)

    def body(q_ref, kv_ref, do_ref, dq_ref, dkv_ref):
        first = pl.program_id(0) == 0
        for h in range(XATTN_HEADS):
            cols = slice(h * dh, (h + 1) * dh)
            vcols = slice(d + h * dh, d + (h + 1) * dh)
            qh, kh, vh, doh = q_ref[:, cols], kv_ref[:, cols], kv_ref[:, vcols], do_ref[:, cols]
            p = _attn_probs(qh, kh, scale)
            dp = lax.dot_general(doh, vh, (((1,), (1,)), ((), ())), preferred_element_type=F32)
            ds = (p * (dp - jnp.sum(dp * p, axis=-1, keepdims=True)) * scale).astype(BF16)
            dq_ref[:, cols] = jnp.dot(ds, kh, preferred_element_type=F32).astype(BF16)
            dk = lax.dot_general(ds, qh, (((0,), (0,)), ((), ())), preferred_element_type=F32)
            dv = lax.dot_general(p.astype(BF16), doh, (((0,), (0,)), ((), ())), preferred_element_type=F32)

            @pl.when(first)
            def _():
                dkv_ref[:, cols] = dk
                dkv_ref[:, vcols] = dv

            @pl.when(jnp.logical_not(first))
            def _():
                dkv_ref[:, cols] += dk
                dkv_ref[:, vcols] += dv

    row = pl.BlockSpec((tm, d), lambda i: (i, 0))
    whole = pl.BlockSpec((mlen, 2 * d), lambda i: (0, 0))
    return pl.pallas_call(
        body, name=name, grid=(s // tm,), in_specs=[row, whole, row], out_specs=[row, whole],
        out_shape=[jax.ShapeDtypeStruct((s, d), BF16), jax.ShapeDtypeStruct((mlen, 2 * d), F32)],
        compiler_params=_params(dimension_semantics=("arbitrary",)),
    )(q, kv, do)


def _as_rows(a):
    if a.ndim >= 2 and a.shape[-1] % LANE == 0:
        return a.reshape(-1, a.shape[-1])
    return a.reshape(-1, LANE) if a.size % LANE == 0 else a.reshape(1, -1)


def add_halves(name, dw, other, core):
    p, r, c = dw.shape
    h = r // 2
    th = _row_tile(h, c * 2, 4 * 2**20)

    def body(core_ref, a_ref, b_ref, o_ref):
        o_ref[...] = (a_ref[...].astype(F32) + b_ref[...].astype(F32)).astype(BF16)

    grid_spec = pltpu.PrefetchScalarGridSpec(
        num_scalar_prefetch=1, grid=(p, h // th),
        in_specs=[pl.BlockSpec((None, None, th, c), lambda pi, i, core_ref: (pi, core_ref[0], i, 0)),
                  pl.BlockSpec((None, th, c), lambda pi, i, core_ref: (pi, i, 0))],
        out_specs=pl.BlockSpec((None, th, c), lambda pi, i, core_ref: (pi, i, 0)))
    return pl.pallas_call(
        body, name=name, grid_spec=grid_spec, out_shape=jax.ShapeDtypeStruct((p, h, c), BF16),
        compiler_params=_params(dimension_semantics=("arbitrary", "arbitrary")),
    )(core, dw.reshape(p, 2, h, c), other)


def sum_leading(name, parts):
    n, r, c = parts.shape
    tr = _row_tile(r, c * 4 * 2, 2 * 2**20)

    def body(p_ref, o_ref):
        acc = p_ref[0].astype(F32)
        for k in range(1, n):
            acc = acc + p_ref[k].astype(F32)
        o_ref[...] = acc

    return pl.pallas_call(
        body, name=name, grid=(r // tr,), in_specs=[pl.BlockSpec((n, tr, c), lambda i: (0, i, 0))],
        out_specs=pl.BlockSpec((tr, c), lambda i: (i, 0)), out_shape=jax.ShapeDtypeStruct((r, c), F32),
        compiler_params=_params(dimension_semantics=("arbitrary",)),
    )(parts)


def _adamw_rows(w, g, m, v):
    m = ADAM_B1 * m + (1.0 - ADAM_B1) * g
    v = ADAM_B2 * v + (1.0 - ADAM_B2) * (g * g)
    m_hat = m / (1.0 - ADAM_B1 ** ADAM_STEP)
    v_hat = v / (1.0 - ADAM_B2 ** ADAM_STEP)
    delta = -ADAM_LR * (m_hat / (jnp.sqrt(v_hat) + ADAM_EPS) + ADAM_WD * w)
    return delta, m, v


def adamw_layer(name, w, m, v, g, layer, carried):
    nl, r, c = w.shape
    tr = _row_tile(r, c * 4, 3 * 2**19)
    n_carried = 4 if carried is not None else 0

    def body(*refs):
        w_ref, m_ref, v_ref, g_ref = refs[:4]
        go_ref, d_ref, mo_ref, vo_ref = refs[4 + n_carried:]
        g = g_ref[...]
        delta, m_new, v_new = _adamw_rows(w_ref[...], g, m_ref[...], v_ref[...])
        go_ref[...] = g
        d_ref[...] = delta
        mo_ref[...] = m_new
        vo_ref[...] = v_new

    stacked = pl.BlockSpec((None, tr, c), lambda i: (layer, i, 0))
    in_specs = [stacked, stacked, stacked, pl.BlockSpec((tr, c), lambda i: (i, 0))]
    in_specs += [pl.BlockSpec(memory_space=pl.ANY)] * n_carried
    shape = jax.ShapeDtypeStruct((nl, r, c), F32)
    return pl.pallas_call(
        body, name=name, grid=(r // tr,), in_specs=in_specs, out_specs=[stacked] * 4, out_shape=[shape] * 4,
        input_output_aliases={4 + k: k for k in range(n_carried)},
        compiler_params=_params(dimension_semantics=("arbitrary",)),
    )(w, m, v, g, *(carried or ()))


def adamw_flat(name, w, m, v, g):
    r, c = w.shape

    def body(w_ref, m_ref, v_ref, g_ref, d_ref, mo_ref, vo_ref):
        delta, m_new, v_new = _adamw_rows(w_ref[...], g_ref[...], m_ref[...], v_ref[...])
        d_ref[...] = delta
        mo_ref[...] = m_new
        vo_ref[...] = v_new

    shape = jax.ShapeDtypeStruct((r, c), F32)
    return pl.pallas_call(body, name=name, out_shape=[shape] * 3, compiler_params=_params())(w, m, v, g)


def cast_place(name, w, layer, place):
    nl, r, c = w.shape
    tr = _row_tile(r, c * 4, 8 * 2**20)

    def body(x_ref, y_ref, c_ref, w_ref, o_ref):
        o_ref[...] = w_ref[...].astype(BF16)

    grid_spec = pltpu.PrefetchScalarGridSpec(
        num_scalar_prefetch=3, grid=(r // tr,),
        in_specs=[pl.BlockSpec((None, tr, c), lambda i, x_ref, y_ref, c_ref: (layer, i, 0))],
        out_specs=pl.BlockSpec((None, tr, c), lambda i, x_ref, y_ref, c_ref: (2 * x_ref[0] + y_ref[0], i, 0)))
    return pl.pallas_call(
        body, name=name, grid_spec=grid_spec, out_shape=jax.ShapeDtypeStruct((N_CHIPS, r, c), BF16),
        compiler_params=_params(dimension_semantics=("arbitrary",)),
    )(*place, w)


def reduce_sum4(name, own, landed, place):
    p, h, c = own.shape
    tr = _row_tile(h, c * 4, 4 * 2**20)

    def body(x_ref, y_ref, c_ref, t_ref, y1_ref, y2_ref, y3_ref, o_ref):
        acc = t_ref[...].astype(F32)
        for part_ref in (y1_ref, y2_ref, y3_ref):
            acc = acc + part_ref[...].astype(F32)
        o_ref[...] = acc

    def panel(fx, fy):
        return pl.BlockSpec((None, tr, c), lambda i, x_ref, y_ref, c_ref: (
            2 * (1 - x_ref[0] if fx else x_ref[0]) + (1 - y_ref[0] if fy else y_ref[0]), i, 0))

    grid_spec = pltpu.PrefetchScalarGridSpec(
        num_scalar_prefetch=3, grid=(h // tr,),
        in_specs=[panel(0, 0), panel(1, 0), panel(0, 1), panel(1, 1)],
        out_specs=pl.BlockSpec((None, tr, c), lambda i, x_ref, y_ref, c_ref: (c_ref[0], i, 0)))
    return pl.pallas_call(
        body, name=name, grid_spec=grid_spec, out_shape=jax.ShapeDtypeStruct((2, h, c), F32),
        compiler_params=_params(dimension_semantics=("arbitrary",)),
    )(*place, own, landed, landed, landed)


def run_exchange(name, exchange):
    n_in, n_out = len(exchange.inputs), len(exchange.out_shapes)

    def body(*refs):
        ins, outs, sems = refs[:n_in], refs[n_in:n_in + n_out], refs[n_in + n_out:]
        exchange.start(ins, outs, sems)
        exchange.finish(ins, outs, sems)

    return pl.pallas_call(
        body, name=name, in_specs=[ANY] * n_in, out_specs=[ANY] * n_out, out_shape=list(exchange.out_shapes),
        scratch_shapes=[pltpu.SemaphoreType.DMA((exchange.n_sems,)), pltpu.SemaphoreType.DMA((exchange.n_sems,))],
        input_output_aliases=dict(exchange.aliases),
    )(*exchange.inputs)


def _row_halves(ref, c):
    h = ref.shape[1] // 2
    return pl.ds(pl.multiple_of(c * h, 16), h), pl.ds(pl.multiple_of((1 - c) * h, 16), h)


def gather_exchange(fulls):
    n = len(fulls)

    def start(ins, outs, sems):
        x, y, c, mine, chips = _place()
        for a in range(n):
            rows = outs[a].at[mine, _row_halves(outs[a], c)[0]]
            for j, chip in enumerate(chips):
                _remote(rows, rows, sems, 6 * a + j, (*chip, c)).start()

    def finish(ins, outs, sems):
        x, y, c, mine, chips = _place()
        sibling = (x, y, 1 - c)
        for a in range(n):
            half = _row_halves(outs[a], c)[0]
            for j, chip in enumerate(chips):
                rows = outs[a].at[2 * chip[0] + chip[1], half]
                _remote(rows, rows, sems, 6 * a + j, sibling).wait_recv()
                _remote(rows, rows, sems, 6 * a + 3 + j, sibling).start()
        for a in range(n):
            half, other = _row_halves(outs[a], c)
            for j, chip in enumerate(chips):
                rows = outs[a].at[2 * chip[0] + chip[1], other]
                _remote(rows, rows, sems, 6 * a + 3 + j, sibling).wait_recv()
            for j, chip in enumerate(chips):
                rows = outs[a].at[mine, half]
                _remote(rows, rows, sems, 6 * a + j, (*chip, c)).wait_send()
                rows = outs[a].at[2 * chip[0] + chip[1], half]
                _remote(rows, rows, sems, 6 * a + 3 + j, sibling).wait_send()

    return Exchange(list(fulls), [jax.ShapeDtypeStruct(f.shape, f.dtype) for f in fulls], {a: a for a in range(n)},
                    6 * n, start, finish)


def swap_exchange(grads):
    n = len(grads)

    def copies(ins, outs, sems):
        x, y, c, _, _ = _place()
        return [_remote(ins[a].at[:, _row_halves(ins[a], c)[1]], outs[a], sems, a, (x, y, 1 - c)) for a in range(n)]

    def start(ins, outs, sems):
        for cp in copies(ins, outs, sems):
            cp.start()

    def finish(ins, outs, sems):
        for cp in copies(ins, outs, sems):
            cp.wait()

    shapes = [jax.ShapeDtypeStruct((g.shape[0], g.shape[1] // 2, g.shape[2]), g.dtype) for g in grads]
    return Exchange(list(grads), shapes, {}, n, start, finish)


def scatter_exchange(parts):
    n = len(parts)

    def sends(ins, outs, sems):
        x, y, c, mine, chips = _place()
        return [_remote(ins[a].at[2 * chip[0] + chip[1]], outs[a].at[mine], sems, 3 * a + j, (*chip, c))
                for a in range(n) for j, chip in enumerate(chips)]

    def start(ins, outs, sems):
        for cp in sends(ins, outs, sems):
            cp.start()

    def finish(ins, outs, sems):
        x, y, c, mine, chips = _place()
        for a in range(n):
            for j, chip in enumerate(chips):
                landing = outs[a].at[2 * chip[0] + chip[1]]
                _remote(landing, landing, sems, 3 * a + j, (*chip, c)).wait_recv()
        for cp in sends(ins, outs, sems):
            cp.wait_send()

    return Exchange(list(parts), [jax.ShapeDtypeStruct(g.shape, g.dtype) for g in parts], {}, 3 * n, start, finish)


def join_exchange(halves):
    n = len(halves)

    def start(ins, outs, sems):
        x, y, c, _, _ = _place()
        for a in range(n):
            _remote(outs[a].at[c], outs[a].at[c], sems, a, (x, y, 1 - c)).start()

    def finish(ins, outs, sems):
        x, y, c, _, _ = _place()
        for a in range(n):
            _remote(outs[a].at[1 - c], outs[a].at[1 - c], sems, a, (x, y, 1 - c)).wait_recv()
        for a in range(n):
            _remote(outs[a].at[c], outs[a].at[c], sems, a, (x, y, 1 - c)).wait_send()

    return Exchange(list(halves), [jax.ShapeDtypeStruct(g.shape, g.dtype) for g in halves], {a: a for a in range(n)},
                    n, start, finish)


def gather_all(name, rows):
    def body(in_ref, out_ref, send_sems, recv_sems, local_sem):
        sems = (send_sems, recv_sems)
        x, y, c, _, _ = _place()
        me = 4 * x + 2 * y + c
        local = pltpu.make_async_copy(in_ref, out_ref.at[me], local_sem)
        local.start()
        peers = [(1 - x if k & 4 else x, 1 - y if k & 2 else y, 1 - c if k & 1 else c) for k in range(1, N_DEV)]
        sent = []
        for k, peer in enumerate(peers):
            cp = _remote(in_ref, out_ref.at[me], sems, k, peer)
            cp.start()
            sent.append(cp)
        for k, peer in enumerate(peers):
            landing = out_ref.at[4 * peer[0] + 2 * peer[1] + peer[2]]
            _remote(landing, landing, sems, k, peer).wait_recv()
        for cp in sent:
            cp.wait_send()
        local.wait()

    return pl.pallas_call(
        body, name=name, in_specs=[ANY], out_specs=ANY,
        out_shape=jax.ShapeDtypeStruct((N_DEV,) + rows.shape, rows.dtype),
        scratch_shapes=[pltpu.SemaphoreType.DMA((N_DEV - 1,)), pltpu.SemaphoreType.DMA((N_DEV - 1,)),
                        pltpu.SemaphoreType.DMA],
    )(rows)


class _Step:
    def __init__(self, p):
        self.p = p
        xi, yi, ci = lax.axis_index("x"), lax.axis_index("y"), lax.axis_index("c")
        self.chip = 2 * xi + yi
        self.place_refs = tuple(v.astype(jnp.int32).reshape(1) for v in (xi, yi, ci))
        self.core_ref = self.place_refs[2]
        self.depth = p['ffn1_norm'].shape[0]
        self.placed, self.w, self.big_g = {}, {}, {}
        self.waiting_joins = []
        self.waiting_scatter = None

    def block_keys(self, tag, l):
        if l >= self.depth:
            return []
        mixer = ['gmlp_w_in', 'gmlp_w_out'] if l % 2 == 0 else ['conv_w_in', 'conv_w_out']
        names = {"ffn1": ['ffn1_w13', 'ffn1_w2'], "mix": mixer, "xattn": ['xattn_wq', 'xattn_wkv', 'xattn_wo'],
                 "ffn2": ['ffn2_w13', 'ffn2_w2']}[tag]
        return [(n, l // 2 if tag == "mix" else l) for n in names]

    def place(self, keys):
        for n, idx in keys:
            self.placed[(n, idx)] = cast_place(f"place_{n}{idx}", self.p[n], idx, self.place_refs)

    def gather_alone(self, name, keys):
        got = run_exchange(name, gather_exchange([self.placed[k] for k in keys]))
        self.w.update(zip(keys, got, strict=True))

    def carrying_gather(self, mm, keys, *args, **kw):
        keys = [k for k in keys if k in self.placed]
        if not keys:
            return mm(*args, **kw)
        out, got = mm(*args, exchange=gather_exchange([self.placed[k] for k in keys]), **kw)
        self.w.update(zip(keys, got, strict=True))
        return out

    def reduce_begin(self, tag, keys, dws, theirs=None):
        theirs = list(theirs or [None] * len(dws))
        todo = [i for i, t in enumerate(theirs) if t is None]
        for i, t in zip(todo, run_exchange(tag + "_swap", swap_exchange([dws[i] for i in todo])), strict=True):
            theirs[i] = t
        parts = [add_halves(f"{tag}_add{i}", dw, t, self.core_ref) for i, (dw, t) in enumerate(zip(dws, theirs, strict=True))]
        assert self.waiting_scatter is None
        self.waiting_scatter = (tag, keys, parts)

    def carrying_scatter(self, mm, *args, **kw):
        tag, keys, parts = self.waiting_scatter
        self.waiting_scatter = None
        out, landed = mm(*args, exchange=scatter_exchange(parts), **kw)
        halves = [reduce_sum4(f"{tag}_sum{i}", t, y, self.place_refs) for i, (t, y) in enumerate(zip(parts, landed, strict=True))]
        self.waiting_joins += list(zip(keys, halves, strict=True))
        return out

    def take_joined(self, keys, joined):
        for k, g in zip(keys, joined, strict=True):
            self.big_g[k] = g.reshape(-1, g.shape[-1])

    def carrying_joins(self, mm, *args, **kw):
        if not self.waiting_joins:
            return mm(*args, **kw)
        keys, halves = zip(*self.waiting_joins, strict=True)
        self.waiting_joins = []
        out, joined = mm(*args, exchange=join_exchange(list(halves)), **kw)
        self.take_joined(keys, joined)
        return out

    def joins_alone(self, name):
        keys, halves = zip(*self.waiting_joins, strict=True)
        self.waiting_joins = []
        self.take_joined(keys, run_exchange(name, join_exchange(list(halves))))

    def ffn_fwd(self, tag, l, x, gain, carry13, carry2, h=None):
        name = f"l{l}_{tag}"
        if h is None:
            h = rms_fwd(name + "_norm", x, gain)
        by_gate, by_up, act = self.carrying_gather(mm_swiglu, carry13, name + "_w13", h, self.w[(tag + '_w13', l)])
        out = self.carrying_gather(mm_nn, carry2, name + "_w2", act, self.w[(tag + '_w2', l)], 'row', F32, res=x,
                                   scale=0.5)
        return out, (x, h, by_gate, by_up, act)

    def ffn_bwd(self, tag, l, dx, dxb, saved, gain):
        w13, w2 = self.w[(tag + '_w13', l)], self.w[(tag + '_w2', l)]
        name = f"l{l}_{tag}"
        x, h, by_gate, by_up, act = saved
        d_gate, d_up = self.carrying_joins(mm_dswiglu, name + "_dact", dxb, w2, by_gate, by_up, 0.5)
        d_w2 = mm_tn(name + "_dw2", act, dxb, 'row', scale=0.5)
        half = N_CHIPS // 2
        d_w13, their_w2 = mm_tn(name + "_dw13g", h, d_gate, 'col', panels=(0, half), exchange=swap_exchange([d_w2]))
        d_w13 = mm_tn(name + "_dw13u", h, d_up, 'col', panels=(half, half), into=d_w13)
        self.reduce_begin(name, [(tag + '_w13', l), (tag + '_w2', l)], [d_w13, d_w2], [None] + their_w2)
        dh = self.carrying_scatter(mm_nt, name + "_dh", d_gate, w13, 'col', BF16, a_hi=d_up)
        return rms_bwd(name + "_dnorm", x, gain, dh, dx)


def kernel(x, mem, ffn1_norm, ffn1_w13, ffn1_w2, mix_norm, gmlp_w_in, gmlp_ln_g, gmlp_ln_b, gmlp_w_s, gmlp_b_s, gmlp_w_out, conv_w_in, conv_w, conv_w_out, xattn_norm, mem_norm, xattn_wq, xattn_wkv, xattn_wo, ffn2_norm, ffn2_w13, ffn2_w2, final_norm, loss_target, m_ffn1_norm, m_ffn1_w13, m_ffn1_w2, m_mix_norm, m_gmlp_w_in, m_gmlp_ln_g, m_gmlp_ln_b, m_gmlp_w_s, m_gmlp_b_s, m_gmlp_w_out, m_conv_w_in, m_conv_w, m_conv_w_out, m_xattn_norm, m_mem_norm, m_xattn_wq, m_xattn_wkv, m_xattn_wo, m_ffn2_norm, m_ffn2_w13, m_ffn2_w2, m_final_norm, v_ffn1_norm, v_ffn1_w13, v_ffn1_w2, v_mix_norm, v_gmlp_w_in, v_gmlp_ln_g, v_gmlp_ln_b, v_gmlp_w_s, v_gmlp_b_s, v_gmlp_w_out, v_conv_w_in, v_conv_w, v_conv_w_out, v_xattn_norm, v_mem_norm, v_xattn_wq, v_xattn_wkv, v_xattn_wo, v_ffn2_norm, v_ffn2_w13, v_ffn2_w2, v_final_norm):
    return _step(dict(locals()))


def _step(p):
    assert sorted(p) == sorted(ARG_NAMES)
    st = _Step(p)
    x = p['x'][0]
    mem = p['mem'][0]
    target = p['loss_target'][0]
    s, d = x.shape
    depth = st.depth

    for l in range(depth):
        for tag in ("ffn1", "mix", "xattn", "ffn2"):
            st.place(st.block_keys(tag, l))
    st.gather_alone("gather_first", [('ffn1_w13', 0)])

    cw_local = p['conv_w']
    n_conv, cwid, dq4 = cw_local.shape
    cw_rows = jnp.pad(cw_local.reshape(-1, LANE), ((0, (-cw_local.size // LANE) % 8), (0, 0)))
    cw_all = gather_all("gather_conv_w", cw_rows)[0::2, :cw_local.size // LANE]
    conv_w_full = cw_all.reshape(N_CHIPS, n_conv, cwid, dq4).transpose(1, 2, 0, 3).reshape(n_conv, cwid, N_CHIPS * dq4)

    saved = []
    for l in range(depth):
        j = l // 2
        rec = {}
        first_w2 = [('ffn1_w2', 0)] if l == 0 else []
        x, rec['ffn1'] = st.ffn_fwd("ffn1", l, x, p['ffn1_norm'][l],
                                    first_w2 + st.block_keys("mix", l) + st.block_keys("xattn", l), [('ffn2_w13', l)])
        h = rms_fwd(f"l{l}_mix_norm", x, p['mix_norm'][l])
        if l % 2 == 0:
            e = p['gmlp_ln_g'].shape[-1]
            bias = jnp.repeat(p['gmlp_b_s'][j].T, e // GMLP_GROUPS, axis=1)
            z = st.carrying_gather(mm_nn, [('ffn2_w2', l)], f"l{l}_gmlp_in", h, st.w[('gmlp_w_in', j)], 'col', BF16)
            gate = gmlp_fwd(f"l{l}_gmlp_gate", z, p['gmlp_ln_g'][j], p['gmlp_ln_b'][j], p['gmlp_w_s'][j], bias)
            x_new, hq = mm_nn(f"l{l}_gmlp_out", gate, st.w[('gmlp_w_out', j)], 'row', F32, res=x,
                              norm_gain=p['xattn_norm'][l])
            rec['mix'] = (x, h, z, gate, bias)
        else:
            bcv = st.carrying_gather(mm_nn, [('ffn2_w2', l)], f"l{l}_conv_in", h, st.w[('conv_w_in', j)], 'col', BF16)
            gate = conv_fwd(f"l{l}_conv_gate", bcv, conv_w_full[j])
            x_new, hq = mm_nn(f"l{l}_conv_out", gate, st.w[('conv_w_out', j)], 'row', F32, res=x,
                              norm_gain=p['xattn_norm'][l])
            rec['mix'] = (x, h, bcv, gate)
        x = x_new
        q = mm_nn(f"l{l}_xattn_q", hq, st.w[('xattn_wq', l)], 'row', BF16)
        mem_n = rms_fwd(f"l{l}_mem_norm", mem, p['mem_norm'][l])
        kv = mm_nn(f"l{l}_xattn_kv", mem_n, st.w[('xattn_wkv', l)], 'col', BF16)
        o = attn_fwd(f"l{l}_xattn_core", q, kv)
        x_new, h_ffn2 = mm_nn(f"l{l}_xattn_o", o, st.w[('xattn_wo', l)], 'row', F32, res=x,
                              norm_gain=p['ffn2_norm'][l])
        rec['xattn'] = (x, hq, q, mem_n, kv, o)
        x = x_new
        x, rec['ffn2'] = st.ffn_fwd("ffn2", l, x, p['ffn2_norm'][l], st.block_keys("ffn1", l + 1), [], h=h_ffn2)
        saved.append(rec)

    dx, dxb, d_final, loss_lanes = loss_head("loss_head", x, p['final_norm'], target)
    loss = lax.psum(0.5 * jnp.sum(loss_lanes) / d, ("x", "y", "c"))

    small = {n: [None] * p[n].shape[0] for n in ('ffn1_norm', 'mix_norm', 'xattn_norm', 'mem_norm', 'ffn2_norm',
                                                  'gmlp_ln_g', 'gmlp_ln_b', 'gmlp_w_s', 'gmlp_b_s', 'conv_w')}
    for l in reversed(range(depth)):
        j = l // 2
        rec = saved[l]
        dx, dxb, small['ffn2_norm'][l] = st.ffn_bwd("ffn2", l, dx, dxb, rec['ffn2'], p['ffn2_norm'][l])

        x_in, hq, q, mem_n, kv, o = rec['xattn']
        name = f"l{l}_xattn"
        do = st.carrying_joins(mm_nt, name + "_do", dxb, st.w[('xattn_wo', l)], 'row', BF16)
        d_wo = mm_tn(name + "_dwo", o, dxb, 'row')
        dq, dkv = attn_bwd(name + "_dcore", q, kv, do)
        d_wq = mm_tn(name + "_dwq", hq, dq, 'row')
        dkvb = dkv.astype(BF16)
        d_wkv = mm_tn(name + "_dwkv", mem_n, dkvb, 'col')
        st.reduce_begin(name, [('xattn_wq', l), ('xattn_wkv', l), ('xattn_wo', l)], [d_wq, d_wkv, d_wo])
        dh = mm_nt(name + "_dh", dq, st.w[('xattn_wq', l)], 'row', BF16)
        dx, dxb, small['xattn_norm'][l] = rms_bwd(name + "_dnorm", x_in, p['xattn_norm'][l], dh, dx)
        dmem_n = mm_nt(name + "_dmem", dkvb, st.w[('xattn_wkv', l)], 'col', F32)
        small['mem_norm'][l] = rms_bwd(f"l{l}_mem_dnorm", mem, p['mem_norm'][l], dmem_n, None)[2]

        if l % 2 == 0:
            x_in, h, z, gate, bias = rec['mix']
            name = f"l{l}_gmlp"
            w_in, w_out = st.w[('gmlp_w_in', j)], st.w[('gmlp_w_out', j)]
            dgate = mm_nt(name + "_dgate", dxb, w_out, 'row', BF16)
            d_wout = mm_tn(name + "_dwout", gate, dxb, 'row')
            dmix, dws, dbs, dlg, dlb = gmlp_bwd(name + "_dgate_core", z, dgate, p['gmlp_ln_g'][j], p['gmlp_ln_b'][j],
                                                p['gmlp_w_s'][j], bias)
            small['gmlp_w_s'][j], small['gmlp_b_s'][j] = dws, dbs[:, :GMLP_GROUPS].T
            small['gmlp_ln_g'][j], small['gmlp_ln_b'][j] = dlg, dlb
            keys = [('gmlp_w_in', j), ('gmlp_w_out', j)]
        else:
            x_in, h, bcv, gate = rec['mix']
            name = f"l{l}_conv"
            w_in, w_out = st.w[('conv_w_in', j)], st.w[('conv_w_out', j)]
            dgate = mm_nt(name + "_dgate", dxb, w_out, 'row', BF16)
            d_wout = mm_tn(name + "_dwout", gate, dxb, 'row')
            dmix, dcw = conv_bwd(name + "_dgate_core", bcv, dgate, conv_w_full[j])
            small['conv_w'][j] = dcw[:cwid]
            keys = [('conv_w_in', j), ('conv_w_out', j)]
        d_win = st.carrying_scatter(mm_tn, name + "_dwin", h, dmix, 'col')
        st.reduce_begin(name, keys, [d_win, d_wout])
        dh = st.carrying_scatter(mm_nt, name + "_dh", dmix, w_in, 'col', BF16)
        dx, dxb, small['mix_norm'][l] = rms_bwd(f"l{l}_mix_dnorm", x_in, p['mix_norm'][l], dh, dx)

        dx, dxb, small['ffn1_norm'][l] = st.ffn_bwd("ffn1", l, dx, dxb, rec['ffn1'], p['ffn1_norm'][l])
    st.joins_alone("join_last")

    small_names = ['ffn1_norm', 'mix_norm', 'xattn_norm', 'mem_norm', 'ffn2_norm', 'gmlp_ln_g', 'gmlp_ln_b', 'gmlp_w_s',
                   'gmlp_b_s', 'final_norm', 'conv_w']
    small_full = {n: jnp.stack([g.reshape(p[n].shape[1:]) for g in small[n]]) for n in small_names
                  if n not in ('final_norm', 'conv_w')}
    small_full['final_norm'] = d_final.reshape(p['final_norm'].shape)
    small_full['conv_w'] = jnp.stack(small['conv_w'])
    packed = jnp.concatenate([small_full[n].reshape(-1, LANE) for n in small_names], axis=0)
    total = sum_leading("small_sum", gather_all("small_gather", packed))
    small_g, at = {}, 0
    for n in small_names:
        rows = small_full[n].size // LANE
        small_g[n] = total[at:at + rows].reshape(small_full[n].shape)
        at += rows
    small_g['conv_w'] = lax.dynamic_slice_in_dim(small_g['conv_w'], st.chip * dq4, dq4, axis=2)

    grads, deltas, new_m, new_v = {}, {}, {}, {}
    for n in WEIGHTS:
        w, m, v = p[n], p['m_' + n], p['v_' + n]
        if n in BIG:
            carried = None
            for i in range(w.shape[0]):
                carried = adamw_layer(f"adamw_{n}{i}", w, m, v, st.big_g[(n, i)], i, carried)
            grads[n], deltas[n], new_m[n], new_v[n] = carried
        else:
            g = small_g[n]
            out = adamw_flat(f"adamw_{n}", _as_rows(w), _as_rows(m), _as_rows(v), _as_rows(g))
            grads[n] = g
            deltas[n], new_m[n], new_v[n] = (o.reshape(w.shape) for o in out)

    grad_x = dx.reshape(p['x'].shape)
    return (loss, grad_x, *[grads[n] for n in WEIGHTS], *[deltas[n] for n in WEIGHTS], *[new_m[n] for n in WEIGHTS],
            *[new_v[n] for n in WEIGHTS])
```

```python
from typing import Callable, NamedTuple

import jax
import jax.numpy as jnp
from jax import lax
from jax.experimental import pallas as pl
from jax.experimental.pallas import tpu as pltpu

F32 = jnp.float32
BF16 = jnp.bfloat16
MESH = pl.DeviceIdType.MESH

CHUNK = 128
GMLP_GROUPS = 8
XATTN_HEADS = 4
RMS_EPS = 1e-6
LN_EPS = 1e-5
ADAM_LR = 0.001
ADAM_B1 = 0.9
ADAM_B2 = 0.999
ADAM_EPS = 1e-08
ADAM_WD = 0.01
ADAM_STEP = 10

N_CHIPS = 4
N_DEV = 8

VMEM_LIMIT_BYTES = 58 * 2**20
VMEM_PLAN_BYTES = 48 * 2**20
LANE = 128
MXU_DIM = 256
ACC_CHUNK = 2 * MXU_DIM
MXU_FLOPS_PER_US = 996e6
HBM_BYTES_PER_US = 3.3e6
STEP_US = 0.35
ACC_US_PER_VREG = 0.58e-3

WEIGHTS = ['ffn1_norm', 'ffn1_w13', 'ffn1_w2', 'mix_norm', 'gmlp_w_in', 'gmlp_ln_g', 'gmlp_ln_b', 'gmlp_w_s',
           'gmlp_b_s', 'gmlp_w_out', 'conv_w_in', 'conv_w', 'conv_w_out', 'xattn_norm', 'mem_norm', 'xattn_wq',
           'xattn_wkv', 'xattn_wo', 'ffn2_norm', 'ffn2_w13', 'ffn2_w2', 'final_norm']
BIG = {'ffn1_w13': 'col', 'ffn1_w2': 'row', 'gmlp_w_in': 'col', 'gmlp_w_out': 'row', 'conv_w_in': 'col',
       'conv_w_out': 'row', 'xattn_wq': 'row', 'xattn_wkv': 'col', 'xattn_wo': 'row', 'ffn2_w13': 'col',
       'ffn2_w2': 'row'}
ARG_NAMES = (['x', 'mem'] + WEIGHTS + ['loss_target'] + ['m_' + n for n in WEIGHTS] + ['v_' + n for n in WEIGHTS])


def _params(**kw):
    return pltpu.CompilerParams(vmem_limit_bytes=VMEM_LIMIT_BYTES, **kw)


def _divisors(n, mult, cap):
    return [d for d in range(mult, min(n, cap) + 1, mult) if n % d == 0] or [n]


def _row_tile(rows, width_bytes, budget=4 * 2**20):
    best = None
    for d in _divisors(rows, 16, 1024):
        if d * width_bytes <= budget:
            best = d
    return best or _divisors(rows, 16, 1024)[0]


ANY = pl.BlockSpec(memory_space=pl.ANY)


class Exchange(NamedTuple):
    inputs: list
    out_shapes: list
    aliases: dict
    n_sems: int
    start: Callable
    finish: Callable


def _place():
    x, y, c = lax.axis_index("x"), lax.axis_index("y"), lax.axis_index("c")
    chips = [(1 - x, y), (x, 1 - y), (1 - x, 1 - y)]
    return x, y, c, 2 * x + y, chips


def _remote(src, dst, sems, k, device):
    return pltpu.make_async_remote_copy(src_ref=src, dst_ref=dst, send_sem=sems[0].at[k], recv_sem=sems[1].at[k],
                                        device_id=device, device_id_type=MESH)


def _mxu_fill(dim):
    return dim / (-(-dim // MXU_DIM) * MXU_DIM)


def _tile_time(flops, fill, traffic, steps, acc_vregs):
    return (max(flops / (MXU_FLOPS_PER_US * fill), traffic / HBM_BYTES_PER_US) + steps * STEP_US
            + steps * acc_vregs * ACC_US_PER_VREG)


def _plan_mm(m, n_tiles_of, k_tiles_of, n, k, a_item, o_item, has_res, a_arrays=1):
    best, best_cost = None, None
    for tm in _divisors(m, 16, 1024):
        for tn in n_tiles_of:
            for tk in k_tiles_of:
                ni, nj, nk = m // tm, n // tn, k // tk
                blocks = a_arrays * tm * tk * a_item + tk * tn * 2 + tm * tn * o_item + (tm * tn * 4 if has_res else 0)
                vmem = 2 * blocks + tm * tn * 4 * (2 if nk > 1 else 1)
                if vmem > VMEM_PLAN_BYTES:
                    continue
                traffic = nj * m * k * a_item + (k * n * 2 if nk == 1 else ni * k * n * 2)
                traffic += m * n * (o_item + (4 if has_res else 0))
                cost = _tile_time(2 * m * n * k, _mxu_fill(tk) * _mxu_fill(tn), traffic, ni * nj * nk,
                                  tm * tn // 1024 if nk > 1 else 0)
                if best_cost is None or cost < best_cost:
                    best, best_cost = (tm, tn, tk), cost
    assert best is not None, (m, n, k)
    return best


def _tiled_call(name, grid, operands, in_specs, out_shapes, out_specs, scratch, compute, exchange=None, aliases=None):
    n_reg, n_out, n_scr = len(operands), len(out_shapes), len(scratch)
    n_xin = len(exchange.inputs) if exchange else 0
    n_xout = len(exchange.out_shapes) if exchange else 0
    semantics = ("arbitrary",) * len(grid)

    def body(*refs):
        ins = refs[:n_reg]
        outs = refs[n_reg + n_xin:n_reg + n_xin + n_out]
        scr = refs[n_reg + n_xin + n_out + n_xout:n_reg + n_xin + n_out + n_xout + n_scr]
        if not exchange:
            compute(ins, outs, scr)
            return
        x_ins = refs[n_reg:n_reg + n_xin]
        x_outs = refs[n_reg + n_xin + n_out:n_reg + n_xin + n_out + n_xout]
        sems = refs[-2:]
        at_first, at_last = True, True
        for k, extent in enumerate(grid):
            at_first = jnp.logical_and(at_first, pl.program_id(k) == 0)
            at_last = jnp.logical_and(at_last, pl.program_id(k) == extent - 1)

        @pl.when(at_first)
        def _():
            exchange.start(x_ins, x_outs, sems)

        compute(ins, outs, scr)

        @pl.when(at_last)
        def _():
            exchange.finish(x_ins, x_outs, sems)

    if not exchange:
        return pl.pallas_call(
            body, name=name, grid=grid, in_specs=in_specs, out_specs=out_specs, out_shape=out_shapes,
            scratch_shapes=scratch, input_output_aliases=dict(aliases or {}),
            compiler_params=_params(dimension_semantics=semantics),
        )(*operands)
    assert not aliases
    sems = [pltpu.SemaphoreType.DMA((exchange.n_sems,)), pltpu.SemaphoreType.DMA((exchange.n_sems,))]
    got = pl.pallas_call(
        body, name=name, grid=grid, in_specs=in_specs + [ANY] * n_xin, out_specs=out_specs + [ANY] * n_xout,
        out_shape=out_shapes + list(exchange.out_shapes), scratch_shapes=scratch + sems,
        input_output_aliases={n_reg + i: n_out + o for i, o in exchange.aliases.items()},
        compiler_params=_params(dimension_semantics=semantics),
    )(*operands, *exchange.inputs)
    return list(got[:n_out]), list(got[n_out:])


class Split(NamedTuple):
    slot: int
    other: jax.Array
    spec: pl.BlockSpec
    use_other: Callable


def _mm_call(name, grid, operands, in_specs, out_shape, out_spec, contract, nk, scale, has_res, tile, exchange=None,
             split=None, into=None, normed=False):
    n_main = len(operands)
    out_shapes, out_specs = [out_shape], [out_spec]
    if normed:
        out_shapes, out_specs = out_shapes + [jax.ShapeDtypeStruct(out_shape.shape, BF16)], out_specs + [out_spec]

    def compute(ins, outs, scr):
        res_ref = ins[2] if has_res else None
        o_ref = outs[0]
        acc_ref = scr[0] if nk > 1 else None

        def finish(v):
            if scale != 1.0:
                v = v * scale
            if has_res:
                v = res_ref[...] + v
            o_ref[...] = v.astype(o_ref.dtype)
            if normed:
                outs[1][...] = _rms_rows(v, ins[2 + has_res][...])[2].astype(BF16)

        def b_block(cols):
            ref = b_ref_of[0]
            stacked = len(ref.shape) == 3
            if contract[0][1] == (1,):
                if not stacked:
                    return ref[cols, :]
                return ref[...].reshape(ref.shape[0] * ref.shape[1], ref.shape[2])[cols, :]
            if not stacked:
                return ref[:, cols]
            b = ref[:, :, cols]
            return b.reshape(b.shape[0] * b.shape[1], b.shape[2])

        b_ref_of = [None]

        def contribute(a_ref, b_ref):
            b_ref_of[0] = b_ref
            if nk == 1:
                finish(lax.dot_general(a_ref[...], b_block(slice(None)), contract, preferred_element_type=F32))
                return
            kk = pl.program_id(2)

            @pl.when(kk == 0)
            def _():
                acc_ref[...] = jnp.zeros(tile, F32)

            a = a_ref[...]
            for start in range(0, tile[1], ACC_CHUNK):
                cols = slice(start, min(start + ACC_CHUNK, tile[1]))
                acc_ref[:, cols] += lax.dot_general(a, b_block(cols), contract, preferred_element_type=F32)

            @pl.when(kk == nk - 1)
            def _():
                finish(acc_ref[...])

        if split is None:
            contribute(ins[0], ins[1])
            return
        use_other = split.use_other(pl.program_id(0), pl.program_id(1), pl.program_id(2))
        pair = [ins[0], ins[1]]
        other = list(pair)
        other[split.slot] = ins[n_main]

        @pl.when(jnp.logical_not(use_other))
        def _():
            contribute(*pair)

        @pl.when(use_other)
        def _():
            contribute(*other)

    aliases = None
    if split is not None:
        operands, in_specs = operands + [split.other], in_specs + [split.spec]
    if into is not None:
        aliases = {len(operands): 0}
        operands, in_specs = operands + [into], in_specs + [ANY]
    got = _tiled_call(name, grid, operands, in_specs, out_shapes, out_specs,
                      [pltpu.VMEM(tile, F32)] if nk > 1 else [], compute, exchange, aliases)
    results, carried = (got[0], got[1]) if exchange else (got, None)
    out = tuple(results) if normed else results[0]
    return (out, carried) if exchange else out


def mm_nn(name, a, w, kind, out_dtype, res=None, scale=1.0, exchange=None, norm_gain=None):
    m, k = a.shape
    p, r, c = w.shape
    n = p * c if kind == 'col' else c
    assert k == (r if kind == 'col' else p * r), (name, a.shape, w.shape)
    n_tiles = _divisors(c, LANE, 2816)
    if norm_gain is not None:
        assert kind == 'row'
        n_tiles = [n]
    k_tiles = _divisors(r, LANE, 4096)
    if kind == 'row':
        k_tiles = k_tiles + [q * r for q in (2, 4) if p % q == 0]
    o_item = jnp.dtype(out_dtype).itemsize + (2 if norm_gain is not None else 0)
    tm, tn, tk = _plan_mm(m, n_tiles, k_tiles, n, k, a.dtype.itemsize, o_item, res is not None)
    nk = k // tk
    if kind == 'col':
        cpt = c // tn
        w_spec = pl.BlockSpec((None, tk, tn), lambda j, i, kk: (j // cpt, kk, j % cpt))
    elif tk > r:
        w_spec = pl.BlockSpec((tk // r, r, tn), lambda j, i, kk: (kk, 0, j))
    else:
        rpt = r // tk
        w_spec = pl.BlockSpec((None, tk, tn), lambda j, i, kk: (kk // rpt, kk % rpt, j))
    in_specs = [pl.BlockSpec((tm, tk), lambda j, i, kk: (i, kk)), w_spec]
    operands = [a, w]
    if res is not None:
        in_specs.append(pl.BlockSpec((tm, tn), lambda j, i, kk: (i, j)))
        operands.append(res)
    if norm_gain is not None:
        in_specs.append(pl.BlockSpec((1, tn), lambda j, i, kk: (0, 0)))
        operands.append(norm_gain.reshape(1, n))
    return _mm_call(name, (n // tn, m // tm, nk), operands, in_specs, jax.ShapeDtypeStruct((m, n), out_dtype),
                    pl.BlockSpec((tm, tn), lambda j, i, kk: (i, j)), (((1,), (0,)), ((), ())), nk, scale,
                    res is not None, (tm, tn), exchange, normed=norm_gain is not None)


def mm_nt(name, a, w, kind, out_dtype, scale=1.0, exchange=None, a_hi=None):
    m, kc = a.shape
    if a_hi is not None:
        assert a_hi.shape == a.shape
        kc = 2 * kc
    p, r, c = w.shape
    n = r if kind == 'col' else p * r
    assert kc == (p * c if kind == 'col' else c), (name, a.shape, w.shape)
    n_tiles = _divisors(r, LANE, 2816)
    k_tiles = _divisors(c, LANE, 4096)
    if kind == 'row':
        n_tiles = n_tiles + [q * r for q in (2, 4) if p % q == 0 and q * r <= 2816]
    tm, tn, tk = _plan_mm(m, n_tiles, k_tiles, n, kc, a.dtype.itemsize, jnp.dtype(out_dtype).itemsize, False,
                          1 if a_hi is None else 2)
    nk = kc // tk
    if kind == 'col':
        cpt = c // tk
        w_spec = pl.BlockSpec((None, tn, tk), lambda j, i, kk: (kk // cpt, j, kk % cpt))
    elif tn > r:
        w_spec = pl.BlockSpec((tn // r, r, tk), lambda j, i, kk: (j, 0, kk))
    else:
        rpt = r // tn
        w_spec = pl.BlockSpec((None, tn, tk), lambda j, i, kk: (j // rpt, j % rpt, kk))
    split = None
    a_spec = pl.BlockSpec((tm, tk), lambda j, i, kk: (i, kk))
    if a_hi is not None:
        half = nk // 2
        assert nk % 2 == 0
        a_spec = pl.BlockSpec((tm, tk), lambda j, i, kk: (i, jnp.minimum(kk, half - 1)))
        split = Split(0, a_hi, pl.BlockSpec((tm, tk), lambda j, i, kk: (i, jnp.maximum(kk - half, 0))),
                      lambda j, i, kk: kk >= half)
    return _mm_call(name, (n // tn, m // tm, nk), [a, w], [a_spec, w_spec], jax.ShapeDtypeStruct((m, n), out_dtype),
                    pl.BlockSpec((tm, tn), lambda j, i, kk: (i, j)), (((1,), (1,)), ((), ())), nk, scale, False,
                    (tm, tn), exchange, split)


def _plan_tn(s, ka, nd, r_tiles, n_tiles):
    best, best_cost = None, None
    for ts in _divisors(s, 16, 2048):
        for tr in r_tiles:
            for tn in n_tiles:
                ni, nj, ns = ka // tr, nd // tn, s // ts
                vmem = 2 * (ts * tr * 2 + ts * tn * 2 + tr * tn * 2) + tr * tn * 4 * (2 if ns > 1 else 1)
                if vmem > VMEM_PLAN_BYTES:
                    continue
                traffic = nj * s * ka * 2 + ni * s * nd * 2 + ka * nd * 2
                cost = _tile_time(2 * s * ka * nd, _mxu_fill(ts) * _mxu_fill(tn), traffic, ni * nj * ns,
                                  tr * tn // 1024 if ns > 1 else 0)
                if best_cost is None or cost < best_cost:
                    best, best_cost = (ts, tr, tn), cost
    assert best is not None, (s, ka, nd)
    return best


def mm_tn(name, a, dy, kind, scale=1.0, exchange=None, panels=(0, N_CHIPS), into=None):
    s, ka = a.shape
    s2, nd = dy.shape
    assert s == s2
    p = N_CHIPS
    first_panel, n_panels = panels
    assert kind == 'col' or panels == (0, p)
    r, c = (ka, nd // n_panels) if kind == 'col' else (ka // p, nd)
    ts, tr, tn = _plan_tn(s, ka, nd, _divisors(r, LANE, 2048), _divisors(c, LANE, 2816))
    ns = s // ts
    if kind == 'col':
        cpt = c // tn
        o_spec = pl.BlockSpec((None, tr, tn), lambda j, i, kk: (first_panel + j // cpt, i, j % cpt))
    else:
        rpt = r // tr
        o_spec = pl.BlockSpec((None, tr, tn), lambda j, i, kk: (i // rpt, i % rpt, j))
    in_specs = [pl.BlockSpec((ts, tr), lambda j, i, kk: (kk, i)), pl.BlockSpec((ts, tn), lambda j, i, kk: (kk, j))]
    return _mm_call(name, (nd // tn, ka // tr, ns), [a, dy], in_specs, jax.ShapeDtypeStruct((p, r, c), BF16), o_spec,
                    (((0,), (0,)), ((), ())), ns, scale, False, (tr, tn), exchange, None, into)


def _plan_fused(m, k, f, tiles, n_w, n_io):
    best, best_cost = None, None
    for tm in _divisors(m, 16, 1024):
        for tn in tiles:
            vmem = 2 * (tm * k * 2 + n_io * tm * tn * 2) + n_w * k * tn * 2 + 4 * tm * tn * 4
            if vmem > VMEM_PLAN_BYTES:
                continue
            traffic = (f // tn) * m * k * 2 + n_w * k * f * 2 + n_io * m * f * 2
            cost = _tile_time(2 * m * k * f * n_w, _mxu_fill(tn), traffic, (f // tn) * (m // tm), 0)
            if best_cost is None or cost < best_cost:
                best, best_cost = (tm, tn), cost
    assert best is not None, (m, k, f)
    return best


def mm_swiglu(name, h, w13, exchange=None):
    m, k = h.shape
    p, r, c = w13.shape
    assert r == k and p % 2 == 0
    f = p * c // 2
    tm, tn = _plan_fused(m, k, f, _divisors(c, LANE, 2816), 2, 3)
    cpt = c // tn

    def compute(ins, outs, scr):
        a = ins[0][...]
        g = jnp.dot(a, ins[1][...], preferred_element_type=F32)
        u = jnp.dot(a, ins[2][...], preferred_element_type=F32)
        sg = _sigmoid(g)
        silu = g * sg
        outs[0][...] = (u * (sg * (1.0 + g * (1.0 - sg)))).astype(BF16)
        outs[1][...] = silu.astype(BF16)
        outs[2][...] = (silu * u).astype(BF16)

    tile = pl.BlockSpec((tm, tn), lambda j, i: (i, j))
    in_specs = [pl.BlockSpec((tm, k), lambda j, i: (i, 0)),
                pl.BlockSpec((None, k, tn), lambda j, i: (j // cpt, 0, j % cpt), pipeline_mode=pl.Buffered(1)),
                pl.BlockSpec((None, k, tn), lambda j, i: (j // cpt + p // 2, 0, j % cpt),
                             pipeline_mode=pl.Buffered(1))]
    shape = jax.ShapeDtypeStruct((m, f), BF16)
    got = _tiled_call(name, (f // tn, m // tm), [h, w13, w13], in_specs, [shape] * 3, [tile] * 3, [], compute, exchange)
    return got


def mm_dswiglu(name, dy, w2, by_gate, by_up, scale, exchange=None):
    m, k = dy.shape
    p, r, c = w2.shape
    assert c == k
    f = p * r
    tiles = _divisors(r, LANE, 2816) + [q * r for q in (2, 4) if p % q == 0 and q * r <= 2816]
    tm, tn = _plan_fused(m, k, f, tiles, 1, 4)

    def compute(ins, outs, scr):
        b = ins[1][...]
        if b.ndim == 3:
            b = b.reshape(b.shape[0] * b.shape[1], b.shape[2])
        d = lax.dot_general(ins[0][...], b, (((1,), (1,)), ((), ())), preferred_element_type=F32) * scale
        outs[0][...] = (d * ins[2][...].astype(F32)).astype(BF16)
        outs[1][...] = (d * ins[3][...].astype(F32)).astype(BF16)

    tile = pl.BlockSpec((tm, tn), lambda j, i: (i, j))
    if tn > r:
        w_spec = pl.BlockSpec((tn // r, r, k), lambda j, i: (j, 0, 0), pipeline_mode=pl.Buffered(1))
    else:
        rpt = r // tn
        w_spec = pl.BlockSpec((None, tn, k), lambda j, i: (j // rpt, j % rpt, 0), pipeline_mode=pl.Buffered(1))
    in_specs = [pl.BlockSpec((tm, k), lambda j, i: (i, 0)), w_spec, tile, tile]
    shape = jax.ShapeDtypeStruct((m, f), BF16)
    return _tiled_call(name, (f // tn, m // tm), [dy, w2, by_gate, by_up], in_specs, [shape] * 2, [tile] * 2, [], compute,
                       exchange)


def _rms_rows(x, g):
    r = lax.rsqrt(jnp.mean(x * x, axis=-1, keepdims=True) + RMS_EPS)
    xhat = x * r
    return xhat, r, xhat * g


def rms_fwd(name, x, g):
    s, d = x.shape
    tm = _row_tile(s, d * 4)

    def body(x_ref, g_ref, o_ref):
        o_ref[...] = _rms_rows(x_ref[...], g_ref[...])[2].astype(BF16)

    return pl.pallas_call(
        body, name=name, grid=(s // tm,),
        in_specs=[pl.BlockSpec((tm, d), lambda i: (i, 0)), pl.BlockSpec((1, d), lambda i: (0, 0))],
        out_specs=pl.BlockSpec((tm, d), lambda i: (i, 0)), out_shape=jax.ShapeDtypeStruct((s, d), BF16),
        compiler_params=_params(dimension_semantics=("arbitrary",)),
    )(x, g.reshape(1, d))


def _rms_bwd_rows(x, g, dh):
    xhat, r, _ = _rms_rows(x, g)
    u = dh * g
    dx = r * (u - xhat * jnp.mean(u * xhat, axis=-1, keepdims=True))
    return dx, jnp.sum(dh * xhat, axis=0, keepdims=True)


def rms_bwd(name, x, g, dh, dres):
    s, d = x.shape
    tm = _row_tile(s, d * 4, 2 * 2**20)
    has_res = dres is not None

    def body(*refs):
        x_ref, g_ref, dh_ref = refs[:3]
        dres_ref = refs[3] if has_res else None
        dx_ref, dxb_ref, dg_ref = refs[-3:]
        dx, dg = _rms_bwd_rows(x_ref[...], g_ref[...], dh_ref[...].astype(F32))
        if has_res:
            dx = dx + dres_ref[...]
        dx_ref[...] = dx
        dxb_ref[...] = dx.astype(BF16)

        @pl.when(pl.program_id(0) == 0)
        def _():
            dg_ref[...] = dg

        @pl.when(pl.program_id(0) > 0)
        def _():
            dg_ref[...] += dg

    row = pl.BlockSpec((tm, d), lambda i: (i, 0))
    vec = pl.BlockSpec((1, d), lambda i: (0, 0))
    return pl.pallas_call(
        body, name=name, grid=(s // tm,),
        in_specs=[row, vec, row] + ([row] if has_res else []),
        out_specs=[row, row, vec],
        out_shape=[jax.ShapeDtypeStruct((s, d), F32), jax.ShapeDtypeStruct((s, d), BF16),
                   jax.ShapeDtypeStruct((1, d), F32)],
        compiler_params=_params(dimension_semantics=("arbitrary",)),
    )(x, g.reshape(1, d), dh, *([dres] if has_res else []))


def loss_head(name, x, g, target):
    s, d = x.shape
    tm = _row_tile(s, d * 4, 2 * 2**20)

    def body(x_ref, g_ref, t_ref, dx_ref, dxb_ref, dg_ref, loss_ref):
        x = x_ref[...]
        gain = g_ref[...]
        y = _rms_rows(x, gain)[2]
        diff = y - t_ref[...]
        dx, dg = _rms_bwd_rows(x, gain, diff * (1.0 / d))
        dx_ref[...] = dx
        dxb_ref[...] = dx.astype(BF16)
        sq = jnp.sum(diff * diff, axis=0, keepdims=True)

        @pl.when(pl.program_id(0) == 0)
        def _():
            dg_ref[...] = dg
            loss_ref[...] = sq

        @pl.when(pl.program_id(0) > 0)
        def _():
            dg_ref[...] += dg
            loss_ref[...] += sq

    row = pl.BlockSpec((tm, d), lambda i: (i, 0))
    vec = pl.BlockSpec((1, d), lambda i: (0, 0))
    return pl.pallas_call(
        body, name=name, grid=(s // tm,), in_specs=[row, vec, row], out_specs=[row, row, vec, vec],
        out_shape=[jax.ShapeDtypeStruct((s, d), F32), jax.ShapeDtypeStruct((s, d), BF16),
                   jax.ShapeDtypeStruct((1, d), F32), jax.ShapeDtypeStruct((1, d), F32)],
        compiler_params=_params(dimension_semantics=("arbitrary",)),
    )(x, g.reshape(1, d), target)


def _sigmoid(x):
    return 0.5 * jnp.tanh(0.5 * x) + 0.5


_INV_SQRT2 = 0.7071067811865476
_INV_SQRT_2PI = 0.3989422804014327


def _normal_cdf(z):
    return 0.5 * (1.0 + lax.erf(z * _INV_SQRT2))


def _gelu_grad(z, cdf):
    return cdf + z * (_INV_SQRT_2PI * jnp.exp(-0.5 * z * z))


def _causal_weights(ws_ref, g):
    t = ws_ref.shape[-1]
    keep = lax.broadcasted_iota(jnp.int32, (t, t), 0) >= lax.broadcasted_iota(jnp.int32, (t, t), 1)
    return jnp.where(keep, ws_ref[g], 0.0).astype(BF16), keep


def _gmlp_gate_rows(z_ref, lg_ref, lb_ref, e):
    z = z_ref[...].astype(F32)
    cdf = _normal_cdf(z)
    gz = z * cdf
    u, v = gz[:, :e], gz[:, e:]
    mu = jnp.mean(v, axis=-1, keepdims=True)
    xc = v - mu
    rs = lax.rsqrt(jnp.mean(xc * xc, axis=-1, keepdims=True) + LN_EPS)
    vhat = xc * rs
    return (z, cdf), u, vhat, rs, vhat * lg_ref[...] + lb_ref[...]


def gmlp_fwd(name, z, ln_g, ln_b, w_s, bias):
    s, e2 = z.shape
    e = e2 // 2
    eg = e // GMLP_GROUPS

    def body(z_ref, lg_ref, lb_ref, ws_ref, b_ref, o_ref):
        _, u, _, _, vln = _gmlp_gate_rows(z_ref, lg_ref, lb_ref, e)
        vb = vln.astype(BF16)
        for g in range(GMLP_GROUPS):
            cols = slice(g * eg, (g + 1) * eg)
            wm, _ = _causal_weights(ws_ref, g)
            f = jnp.dot(wm, vb[:, cols], preferred_element_type=F32) + b_ref[:, cols]
            o_ref[:, cols] = (u[:, cols] * f).astype(BF16)

    full = lambda shape: pl.BlockSpec(shape, lambda i: (0,) * len(shape))
    return pl.pallas_call(
        body, name=name, grid=(s // CHUNK,),
        in_specs=[pl.BlockSpec((CHUNK, e2), lambda i: (i, 0)), full((1, e)), full((1, e)),
                  full((GMLP_GROUPS, CHUNK, CHUNK)), full((CHUNK, e))],
        out_specs=pl.BlockSpec((CHUNK, e), lambda i: (i, 0)), out_shape=jax.ShapeDtypeStruct((s, e), BF16),
        compiler_params=_params(dimension_semantics=("arbitrary",)),
    )(z, ln_g.reshape(1, e), ln_b.reshape(1, e), w_s, bias)


def gmlp_bwd(name, z, dp, ln_g, ln_b, w_s, bias):
    s, e2 = z.shape
    e = e2 // 2
    eg = e // GMLP_GROUPS
    t = CHUNK

    def body(z_ref, dp_ref, lg_ref, lb_ref, ws_ref, b_ref, dz_ref, dws_ref, dbs_ref, dlg_ref, dlb_ref):
        first = pl.program_id(0) == 0
        (zf, cdf), u, vhat, rs, vln = _gmlp_gate_rows(z_ref, lg_ref, lb_ref, e)
        vb = vln.astype(BF16)
        dp = dp_ref[...].astype(F32)
        lane = lax.broadcasted_iota(jnp.int32, (t, LANE), 1)
        dbs = jnp.zeros((t, LANE), F32)
        dvln_parts = []
        for g in range(GMLP_GROUPS):
            cols = slice(g * eg, (g + 1) * eg)
            wm, keep = _causal_weights(ws_ref, g)
            f = jnp.dot(wm, vb[:, cols], preferred_element_type=F32) + b_ref[:, cols]
            dz_ref[:, cols] = (dp[:, cols] * f * _gelu_grad(zf[:, cols], cdf[:, cols])).astype(BF16)
            df = dp[:, cols] * u[:, cols]
            dfb = df.astype(BF16)
            dbs = dbs + jnp.where(lane == g, jnp.sum(df, axis=-1, keepdims=True), 0.0)
            dw = lax.dot_general(dfb, vb[:, cols], (((1,), (1,)), ((), ())), preferred_element_type=F32)
            dw = jnp.where(keep, dw, 0.0)

            @pl.when(first)
            def _():
                dws_ref[g] = dw

            @pl.when(jnp.logical_not(first))
            def _():
                dws_ref[g] += dw

            dvln_parts.append(lax.dot_general(wm, dfb, (((0,), (0,)), ((), ())), preferred_element_type=F32))
        dvln = jnp.concatenate(dvln_parts, axis=-1)
        dvhat = dvln * lg_ref[...]
        dv = rs * (dvhat - jnp.mean(dvhat, axis=-1, keepdims=True)
                   - vhat * jnp.mean(dvhat * vhat, axis=-1, keepdims=True))
        dz_ref[:, e:] = (dv * _gelu_grad(zf[:, e:], cdf[:, e:])).astype(BF16)
        dlg = jnp.sum(dvln * vhat, axis=0, keepdims=True)
        dlb = jnp.sum(dvln, axis=0, keepdims=True)

        @pl.when(first)
        def _():
            dbs_ref[...] = dbs
            dlg_ref[...] = dlg
            dlb_ref[...] = dlb

        @pl.when(jnp.logical_not(first))
        def _():
            dbs_ref[...] += dbs
            dlg_ref[...] += dlg
            dlb_ref[...] += dlb

    full = lambda shape: pl.BlockSpec(shape, lambda i: (0,) * len(shape))
    return pl.pallas_call(
        body, name=name, grid=(s // t,),
        in_specs=[pl.BlockSpec((t, e2), lambda i: (i, 0)), pl.BlockSpec((t, e), lambda i: (i, 0)), full((1, e)),
                  full((1, e)), full((GMLP_GROUPS, t, t)), full((t, e))],
        out_specs=[pl.BlockSpec((t, e2), lambda i: (i, 0)), full((GMLP_GROUPS, t, t)), full((t, LANE)), full((1, e)),
                   full((1, e))],
        out_shape=[jax.ShapeDtypeStruct((s, e2), BF16), jax.ShapeDtypeStruct((GMLP_GROUPS, t, t), F32),
                   jax.ShapeDtypeStruct((t, LANE), F32), jax.ShapeDtypeStruct((1, e), F32),
                   jax.ShapeDtypeStruct((1, e), F32)],
        compiler_params=_params(dimension_semantics=("arbitrary",)),
    )(z, dp, ln_g.reshape(1, e), ln_b.reshape(1, e), w_s, bias)


EDGE = 16


def _shift_down(zc, prev, k):
    tm = zc.shape[0]
    row = lax.broadcasted_iota(jnp.int32, (tm, 1), 0)
    out = pltpu.roll(zc, k, 0)
    for j in range(k):
        out = jnp.where(row == j, prev[EDGE - k + j:EDGE - k + j + 1, :], out)
    return out


def _shift_up(dc, nxt, k):
    tm = dc.shape[0]
    row = lax.broadcasted_iota(jnp.int32, (tm, 1), 0)
    out = pltpu.roll(dc, tm - k, 0)
    for j in range(k):
        out = jnp.where(row == tm - k + j, nxt[j:j + 1, :], out)
    return out


def conv_fwd(name, bcv, cw):
    s, d3 = bcv.shape
    d = d3 // 3
    tm = _row_tile(s, d * 4, 2 * 2**20)
    per = tm // EDGE

    def body(b_ref, c_ref, v_ref, cp_ref, vp_ref, w_ref, o_ref):
        i = pl.program_id(0)
        zc = c_ref[...].astype(F32) * v_ref[...].astype(F32)
        prev = jnp.where(i > 0, cp_ref[...].astype(F32) * vp_ref[...].astype(F32), 0.0)
        conv = w_ref[2:3, :] * zc + w_ref[1:2, :] * _shift_down(zc, prev, 1) + w_ref[0:1, :] * _shift_down(zc, prev, 2)
        o_ref[...] = (b_ref[...].astype(F32) * conv).astype(BF16)

    blk = lambda col: pl.BlockSpec((tm, d), lambda i: (i, col))
    edge = lambda col: pl.BlockSpec((EDGE, d), lambda i: (jnp.maximum(i * per - 1, 0), col))
    return pl.pallas_call(
        body, name=name, grid=(s // tm,),
        in_specs=[blk(0), blk(1), blk(2), edge(1), edge(2), pl.BlockSpec((3, d), lambda i: (0, 0))],
        out_specs=pl.BlockSpec((tm, d), lambda i: (i, 0)), out_shape=jax.ShapeDtypeStruct((s, d), BF16),
        compiler_params=_params(dimension_semantics=("arbitrary",)),
    )(bcv, bcv, bcv, bcv, bcv, cw)


def conv_bwd(name, bcv, dq, cw):
    s, d3 = bcv.shape
    d = d3 // 3
    tm = _row_tile(s, d * 4, 2**20)
    per = tm // EDGE
    n_tiles = s // tm
    last_edge = s // EDGE - 1

    def body(b_ref, c_ref, v_ref, cp_ref, vp_ref, bn_ref, dq_ref, dqn_ref, w_ref, o_ref, dw_ref):
        i = pl.program_id(0)
        b = b_ref[...].astype(F32)
        c = c_ref[...].astype(F32)
        v = v_ref[...].astype(F32)
        dq = dq_ref[...].astype(F32)
        zc = c * v
        prev = jnp.where(i > 0, cp_ref[...].astype(F32) * vp_ref[...].astype(F32), 0.0)
        z1 = _shift_down(zc, prev, 1)
        z2 = _shift_down(zc, prev, 2)
        w0, w1, w2 = w_ref[0:1, :], w_ref[1:2, :], w_ref[2:3, :]
        conv = w2 * zc + w1 * z1 + w0 * z2
        dconv = dq * b
        nxt = jnp.where(i < n_tiles - 1, dqn_ref[...].astype(F32) * bn_ref[...].astype(F32), 0.0)
        dz = w2 * dconv + w1 * _shift_up(dconv, nxt, 1) + w0 * _shift_up(dconv, nxt, 2)
        o_ref[:, :d] = (dq * conv).astype(BF16)
        o_ref[:, d:2 * d] = (dz * v).astype(BF16)
        o_ref[:, 2 * d:] = (dz * c).astype(BF16)
        dw = jnp.concatenate([jnp.sum(dconv * z2, axis=0, keepdims=True), jnp.sum(dconv * z1, axis=0, keepdims=True),
                              jnp.sum(dconv * zc, axis=0, keepdims=True), jnp.zeros((5, d), F32)], axis=0)

        @pl.when(i == 0)
        def _():
            dw_ref[...] = dw

        @pl.when(i > 0)
        def _():
            dw_ref[...] += dw

    blk = lambda col: pl.BlockSpec((tm, d), lambda i: (i, col))
    before = lambda col: pl.BlockSpec((EDGE, d), lambda i: (jnp.maximum(i * per - 1, 0), col))
    after = lambda col: pl.BlockSpec((EDGE, d), lambda i: (jnp.minimum((i + 1) * per, last_edge), col))
    return pl.pallas_call(
        body, name=name, grid=(n_tiles,),
        in_specs=[blk(0), blk(1), blk(2), before(1), before(2), after(0), blk(0), after(0),
                  pl.BlockSpec((3, d), lambda i: (0, 0))],
        out_specs=[pl.BlockSpec((tm, d3), lambda i: (i, 0)), pl.BlockSpec((8, d), lambda i: (0, 0))],
        out_shape=[jax.ShapeDtypeStruct((s, d3), BF16), jax.ShapeDtypeStruct((8, d), F32)],
        compiler_params=_params(dimension_semantics=("arbitrary",)),
    )(bcv, bcv, bcv, bcv, bcv, bcv, dq, dq, cw)


def _attn_probs(qh, kh, scale):
    sc = lax.dot_general(qh, kh, (((1,), (1,)), ((), ())), preferred_element_type=F32) * scale
    ex = jnp.exp(sc - jnp.max(sc, axis=-1, keepdims=True))
    return ex / jnp.sum(ex, axis=-1, keepdims=True)


def attn_fwd(name, q, kv):
    s, d = q.shape
    mlen = kv.shape[0]
    dh = d // XATTN_HEADS
    scale = dh ** -0.5
    tm = _row_tile(s, d * 4, 4 * 2**20)

    def body(q_ref, kv_ref, o_ref):
        for h in range(XATTN_HEADS):
            cols = slice(h * dh, (h + 1) * dh)
            p = _attn_probs(q_ref[:, cols], kv_ref[:, cols], scale)
            o_ref[:, cols] = jnp.dot(p.astype(BF16), kv_ref[:, d + h * dh:d + (h + 1) * dh],
                                     preferred_element_type=F32).astype(BF16)

    return pl.pallas_call(
        body, name=name, grid=(s // tm,),
        in_specs=[pl.BlockSpec((tm, d), lambda i: (i, 0)), pl.BlockSpec((mlen, 2 * d), lambda i: (0, 0))],
        out_specs=pl.BlockSpec((tm, d), lambda i: (i, 0)), out_shape=jax.ShapeDtypeStruct((s, d), BF16),
        compiler_params=_params(dimension_semantics=("arbitrary",)),
    )(q, kv)


def attn_bwd(name, q, kv, do):
    s, d = q.shape
    mlen = kv.shape[0]
    dh = d // XATTN_HEADS
    scale = dh ** -0.5
    tm = _row_tile(s, d * 4, 4 * 2**20)

    def body(q_ref, kv_ref, do_ref, dq_ref, dkv_ref):
        first = pl.program_id(0) == 0
        for h in range(XATTN_HEADS):
            cols = slice(h * dh, (h + 1) * dh)
            vcols = slice(d + h * dh, d + (h + 1) * dh)
            qh, kh, vh, doh = q_ref[:, cols], kv_ref[:, cols], kv_ref[:, vcols], do_ref[:, cols]
            p = _attn_probs(qh, kh, scale)
            dp = lax.dot_general(doh, vh, (((1,), (1,)), ((), ())), preferred_element_type=F32)
            ds = (p * (dp - jnp.sum(dp * p, axis=-1, keepdims=True)) * scale).astype(BF16)
            dq_ref[:, cols] = jnp.dot(ds, kh, preferred_element_type=F32).astype(BF16)
            dk = lax.dot_general(ds, qh, (((0,), (0,)), ((), ())), preferred_element_type=F32)
            dv = lax.dot_general(p.astype(BF16), doh, (((0,), (0,)), ((), ())), preferred_element_type=F32)

            @pl.when(first)
            def _():
                dkv_ref[:, cols] = dk
                dkv_ref[:, vcols] = dv

            @pl.when(jnp.logical_not(first))
            def _():
                dkv_ref[:, cols] += dk
                dkv_ref[:, vcols] += dv

    row = pl.BlockSpec((tm, d), lambda i: (i, 0))
    whole = pl.BlockSpec((mlen, 2 * d), lambda i: (0, 0))
    return pl.pallas_call(
        body, name=name, grid=(s // tm,), in_specs=[row, whole, row], out_specs=[row, whole],
        out_shape=[jax.ShapeDtypeStruct((s, d), BF16), jax.ShapeDtypeStruct((mlen, 2 * d), F32)],
        compiler_params=_params(dimension_semantics=("arbitrary",)),
    )(q, kv, do)


def _as_rows(a):
    if a.ndim >= 2 and a.shape[-1] % LANE == 0:
        return a.reshape(-1, a.shape[-1])
    return a.reshape(-1, LANE) if a.size % LANE == 0 else a.reshape(1, -1)


def add_halves(name, dw, other, core):
    p, r, c = dw.shape
    h = r // 2
    th = _row_tile(h, c * 2, 4 * 2**20)

    def body(core_ref, a_ref, b_ref, o_ref):
        o_ref[...] = (a_ref[...].astype(F32) + b_ref[...].astype(F32)).astype(BF16)

    grid_spec = pltpu.PrefetchScalarGridSpec(
        num_scalar_prefetch=1, grid=(p, h // th),
        in_specs=[pl.BlockSpec((None, None, th, c), lambda pi, i, core_ref: (pi, core_ref[0], i, 0)),
                  pl.BlockSpec((None, th, c), lambda pi, i, core_ref: (pi, i, 0))],
        out_specs=pl.BlockSpec((None, th, c), lambda pi, i, core_ref: (pi, i, 0)))
    return pl.pallas_call(
        body, name=name, grid_spec=grid_spec, out_shape=jax.ShapeDtypeStruct((p, h, c), BF16),
        compiler_params=_params(dimension_semantics=("arbitrary", "arbitrary")),
    )(core, dw.reshape(p, 2, h, c), other)


def sum_leading(name, parts):
    n, r, c = parts.shape
    tr = _row_tile(r, c * 4 * 2, 2 * 2**20)

    def body(p_ref, o_ref):
        acc = p_ref[0].astype(F32)
        for k in range(1, n):
            acc = acc + p_ref[k].astype(F32)
        o_ref[...] = acc

    return pl.pallas_call(
        body, name=name, grid=(r // tr,), in_specs=[pl.BlockSpec((n, tr, c), lambda i: (0, i, 0))],
        out_specs=pl.BlockSpec((tr, c), lambda i: (i, 0)), out_shape=jax.ShapeDtypeStruct((r, c), F32),
        compiler_params=_params(dimension_semantics=("arbitrary",)),
    )(parts)


def _adamw_rows(w, g, m, v):
    m = ADAM_B1 * m + (1.0 - ADAM_B1) * g
    v = ADAM_B2 * v + (1.0 - ADAM_B2) * (g * g)
    m_hat = m / (1.0 - ADAM_B1 ** ADAM_STEP)
    v_hat = v / (1.0 - ADAM_B2 ** ADAM_STEP)
    delta = -ADAM_LR * (m_hat / (jnp.sqrt(v_hat) + ADAM_EPS) + ADAM_WD * w)
    return delta, m, v


def adamw_layer(name, w, m, v, g, layer, carried):
    nl, r, c = w.shape
    tr = _row_tile(r, c * 4, 3 * 2**19)
    n_carried = 4 if carried is not None else 0

    def body(*refs):
        w_ref, m_ref, v_ref, g_ref = refs[:4]
        go_ref, d_ref, mo_ref, vo_ref = refs[4 + n_carried:]
        g = g_ref[...]
        delta, m_new, v_new = _adamw_rows(w_ref[...], g, m_ref[...], v_ref[...])
        go_ref[...] = g
        d_ref[...] = delta
        mo_ref[...] = m_new
        vo_ref[...] = v_new

    stacked = pl.BlockSpec((None, tr, c), lambda i: (layer, i, 0))
    in_specs = [stacked, stacked, stacked, pl.BlockSpec((tr, c), lambda i: (i, 0))]
    in_specs += [pl.BlockSpec(memory_space=pl.ANY)] * n_carried
    shape = jax.ShapeDtypeStruct((nl, r, c), F32)
    return pl.pallas_call(
        body, name=name, grid=(r // tr,), in_specs=in_specs, out_specs=[stacked] * 4, out_shape=[shape] * 4,
        input_output_aliases={4 + k: k for k in range(n_carried)},
        compiler_params=_params(dimension_semantics=("arbitrary",)),
    )(w, m, v, g, *(carried or ()))


def adamw_flat(name, w, m, v, g):
    r, c = w.shape

    def body(w_ref, m_ref, v_ref, g_ref, d_ref, mo_ref, vo_ref):
        delta, m_new, v_new = _adamw_rows(w_ref[...], g_ref[...], m_ref[...], v_ref[...])
        d_ref[...] = delta
        mo_ref[...] = m_new
        vo_ref[...] = v_new

    shape = jax.ShapeDtypeStruct((r, c), F32)
    return pl.pallas_call(body, name=name, out_shape=[shape] * 3, compiler_params=_params())(w, m, v, g)


def cast_place(name, w, layer, place):
    nl, r, c = w.shape
    tr = _row_tile(r, c * 4, 8 * 2**20)

    def body(x_ref, y_ref, c_ref, w_ref, o_ref):
        o_ref[...] = w_ref[...].astype(BF16)

    grid_spec = pltpu.PrefetchScalarGridSpec(
        num_scalar_prefetch=3, grid=(r // tr,),
        in_specs=[pl.BlockSpec((None, tr, c), lambda i, x_ref, y_ref, c_ref: (layer, i, 0))],
        out_specs=pl.BlockSpec((None, tr, c), lambda i, x_ref, y_ref, c_ref: (2 * x_ref[0] + y_ref[0], i, 0)))
    return pl.pallas_call(
        body, name=name, grid_spec=grid_spec, out_shape=jax.ShapeDtypeStruct((N_CHIPS, r, c), BF16),
        compiler_params=_params(dimension_semantics=("arbitrary",)),
    )(*place, w)


def reduce_sum4(name, own, landed, place):
    p, h, c = own.shape
    tr = _row_tile(h, c * 4, 4 * 2**20)

    def body(x_ref, y_ref, c_ref, t_ref, y1_ref, y2_ref, y3_ref, o_ref):
        acc = t_ref[...].astype(F32)
        for part_ref in (y1_ref, y2_ref, y3_ref):
            acc = acc + part_ref[...].astype(F32)
        o_ref[...] = acc

    def panel(fx, fy):
        return pl.BlockSpec((None, tr, c), lambda i, x_ref, y_ref, c_ref: (
            2 * (1 - x_ref[0] if fx else x_ref[0]) + (1 - y_ref[0] if fy else y_ref[0]), i, 0))

    grid_spec = pltpu.PrefetchScalarGridSpec(
        num_scalar_prefetch=3, grid=(h // tr,),
        in_specs=[panel(0, 0), panel(1, 0), panel(0, 1), panel(1, 1)],
        out_specs=pl.BlockSpec((None, tr, c), lambda i, x_ref, y_ref, c_ref: (c_ref[0], i, 0)))
    return pl.pallas_call(
        body, name=name, grid_spec=grid_spec, out_shape=jax.ShapeDtypeStruct((2, h, c), F32),
        compiler_params=_params(dimension_semantics=("arbitrary",)),
    )(*place, own, landed, landed, landed)


def run_exchange(name, exchange):
    n_in, n_out = len(exchange.inputs), len(exchange.out_shapes)

    def body(*refs):
        ins, outs, sems = refs[:n_in], refs[n_in:n_in + n_out], refs[n_in + n_out:]
        exchange.start(ins, outs, sems)
        exchange.finish(ins, outs, sems)

    return pl.pallas_call(
        body, name=name, in_specs=[ANY] * n_in, out_specs=[ANY] * n_out, out_shape=list(exchange.out_shapes),
        scratch_shapes=[pltpu.SemaphoreType.DMA((exchange.n_sems,)), pltpu.SemaphoreType.DMA((exchange.n_sems,))],
        input_output_aliases=dict(exchange.aliases),
    )(*exchange.inputs)


def _row_halves(ref, c):
    h = ref.shape[1] // 2
    return pl.ds(pl.multiple_of(c * h, 16), h), pl.ds(pl.multiple_of((1 - c) * h, 16), h)


def gather_exchange(fulls):
    n = len(fulls)

    def start(ins, outs, sems):
        x, y, c, mine, chips = _place()
        for a in range(n):
            rows = outs[a].at[mine, _row_halves(outs[a], c)[0]]
            for j, chip in enumerate(chips):
                _remote(rows, rows, sems, 6 * a + j, (*chip, c)).start()

    def finish(ins, outs, sems):
        x, y, c, mine, chips = _place()
        sibling = (x, y, 1 - c)
        for a in range(n):
            half = _row_halves(outs[a], c)[0]
            for j, chip in enumerate(chips):
                rows = outs[a].at[2 * chip[0] + chip[1], half]
                _remote(rows, rows, sems, 6 * a + j, sibling).wait_recv()
                _remote(rows, rows, sems, 6 * a + 3 + j, sibling).start()
        for a in range(n):
            half, other = _row_halves(outs[a], c)
            for j, chip in enumerate(chips):
                rows = outs[a].at[2 * chip[0] + chip[1], other]
                _remote(rows, rows, sems, 6 * a + 3 + j, sibling).wait_recv()
            for j, chip in enumerate(chips):
                rows = outs[a].at[mine, half]
                _remote(rows, rows, sems, 6 * a + j, (*chip, c)).wait_send()
                rows = outs[a].at[2 * chip[0] + chip[1], half]
                _remote(rows, rows, sems, 6 * a + 3 + j, sibling).wait_send()

    return Exchange(list(fulls), [jax.ShapeDtypeStruct(f.shape, f.dtype) for f in fulls], {a: a for a in range(n)},
                    6 * n, start, finish)


def swap_exchange(grads):
    n = len(grads)

    def copies(ins, outs, sems):
        x, y, c, _, _ = _place()
        return [_remote(ins[a].at[:, _row_halves(ins[a], c)[1]], outs[a], sems, a, (x, y, 1 - c)) for a in range(n)]

    def start(ins, outs, sems):
        for cp in copies(ins, outs, sems):
            cp.start()

    def finish(ins, outs, sems):
        for cp in copies(ins, outs, sems):
            cp.wait()

    shapes = [jax.ShapeDtypeStruct((g.shape[0], g.shape[1] // 2, g.shape[2]), g.dtype) for g in grads]
    return Exchange(list(grads), shapes, {}, n, start, finish)


def scatter_exchange(parts):
    n = len(parts)

    def sends(ins, outs, sems):
        x, y, c, mine, chips = _place()
        return [_remote(ins[a].at[2 * chip[0] + chip[1]], outs[a].at[mine], sems, 3 * a + j, (*chip, c))
                for a in range(n) for j, chip in enumerate(chips)]

    def start(ins, outs, sems):
        for cp in sends(ins, outs, sems):
            cp.start()

    def finish(ins, outs, sems):
        x, y, c, mine, chips = _place()
        for a in range(n):
            for j, chip in enumerate(chips):
                landing = outs[a].at[2 * chip[0] + chip[1]]
                _remote(landing, landing, sems, 3 * a + j, (*chip, c)).wait_recv()
        for cp in sends(ins, outs, sems):
            cp.wait_send()

    return Exchange(list(parts), [jax.ShapeDtypeStruct(g.shape, g.dtype) for g in parts], {}, 3 * n, start, finish)


def join_exchange(halves):
    n = len(halves)

    def start(ins, outs, sems):
        x, y, c, _, _ = _place()
        for a in range(n):
            _remote(outs[a].at[c], outs[a].at[c], sems, a, (x, y, 1 - c)).start()

    def finish(ins, outs, sems):
        x, y, c, _, _ = _place()
        for a in range(n):
            _remote(outs[a].at[1 - c], outs[a].at[1 - c], sems, a, (x, y, 1 - c)).wait_recv()
        for a in range(n):
            _remote(outs[a].at[c], outs[a].at[c], sems, a, (x, y, 1 - c)).wait_send()

    return Exchange(list(halves), [jax.ShapeDtypeStruct(g.shape, g.dtype) for g in halves], {a: a for a in range(n)},
                    n, start, finish)


def gather_all(name, rows):
    def body(in_ref, out_ref, send_sems, recv_sems, local_sem):
        sems = (send_sems, recv_sems)
        x, y, c, _, _ = _place()
        me = 4 * x + 2 * y + c
        local = pltpu.make_async_copy(in_ref, out_ref.at[me], local_sem)
        local.start()
        peers = [(1 - x if k & 4 else x, 1 - y if k & 2 else y, 1 - c if k & 1 else c) for k in range(1, N_DEV)]
        sent = []
        for k, peer in enumerate(peers):
            cp = _remote(in_ref, out_ref.at[me], sems, k, peer)
            cp.start()
            sent.append(cp)
        for k, peer in enumerate(peers):
            landing = out_ref.at[4 * peer[0] + 2 * peer[1] + peer[2]]
            _remote(landing, landing, sems, k, peer).wait_recv()
        for cp in sent:
            cp.wait_send()
        local.wait()

    return pl.pallas_call(
        body, name=name, in_specs=[ANY], out_specs=ANY,
        out_shape=jax.ShapeDtypeStruct((N_DEV,) + rows.shape, rows.dtype),
        scratch_shapes=[pltpu.SemaphoreType.DMA((N_DEV - 1,)), pltpu.SemaphoreType.DMA((N_DEV - 1,)),
                        pltpu.SemaphoreType.DMA],
    )(rows)


class _Step:
    def __init__(self, p):
        self.p = p
        xi, yi, ci = lax.axis_index("x"), lax.axis_index("y"), lax.axis_index("c")
        self.chip = 2 * xi + yi
        self.place_refs = tuple(v.astype(jnp.int32).reshape(1) for v in (xi, yi, ci))
        self.core_ref = self.place_refs[2]
        self.depth = p['ffn1_norm'].shape[0]
        self.placed, self.w, self.big_g = {}, {}, {}
        self.waiting_joins = []
        self.waiting_scatter = None

    def block_keys(self, tag, l):
        if l >= self.depth:
            return []
        mixer = ['gmlp_w_in', 'gmlp_w_out'] if l % 2 == 0 else ['conv_w_in', 'conv_w_out']
        names = {"ffn1": ['ffn1_w13', 'ffn1_w2'], "mix": mixer, "xattn": ['xattn_wq', 'xattn_wkv', 'xattn_wo'],
                 "ffn2": ['ffn2_w13', 'ffn2_w2']}[tag]
        return [(n, l // 2 if tag == "mix" else l) for n in names]

    def place(self, keys):
        for n, idx in keys:
            self.placed[(n, idx)] = cast_place(f"place_{n}{idx}", self.p[n], idx, self.place_refs)

    def gather_alone(self, name, keys):
        got = run_exchange(name, gather_exchange([self.placed[k] for k in keys]))
        self.w.update(zip(keys, got, strict=True))

    def carrying_gather(self, mm, keys, *args, **kw):
        keys = [k for k in keys if k in self.placed]
        if not keys:
            return mm(*args, **kw)
        out, got = mm(*args, exchange=gather_exchange([self.placed[k] for k in keys]), **kw)
        self.w.update(zip(keys, got, strict=True))
        return out

    def reduce_begin(self, tag, keys, dws, theirs=None):
        theirs = list(theirs or [None] * len(dws))
        todo = [i for i, t in enumerate(theirs) if t is None]
        for i, t in zip(todo, run_exchange(tag + "_swap", swap_exchange([dws[i] for i in todo])), strict=True):
            theirs[i] = t
        parts = [add_halves(f"{tag}_add{i}", dw, t, self.core_ref) for i, (dw, t) in enumerate(zip(dws, theirs, strict=True))]
        assert self.waiting_scatter is None
        self.waiting_scatter = (tag, keys, parts)

    def carrying_scatter(self, mm, *args, **kw):
        tag, keys, parts = self.waiting_scatter
        self.waiting_scatter = None
        out, landed = mm(*args, exchange=scatter_exchange(parts), **kw)
        halves = [reduce_sum4(f"{tag}_sum{i}", t, y, self.place_refs) for i, (t, y) in enumerate(zip(parts, landed, strict=True))]
        self.waiting_joins += list(zip(keys, halves, strict=True))
        return out

    def take_joined(self, keys, joined):
        for k, g in zip(keys, joined, strict=True):
            self.big_g[k] = g.reshape(-1, g.shape[-1])

    def carrying_joins(self, mm, *args, **kw):
        if not self.waiting_joins:
            return mm(*args, **kw)
        keys, halves = zip(*self.waiting_joins, strict=True)
        self.waiting_joins = []
        out, joined = mm(*args, exchange=join_exchange(list(halves)), **kw)
        self.take_joined(keys, joined)
        return out

    def joins_alone(self, name):
        keys, halves = zip(*self.waiting_joins, strict=True)
        self.waiting_joins = []
        self.take_joined(keys, run_exchange(name, join_exchange(list(halves))))

    def ffn_fwd(self, tag, l, x, gain, carry13, carry2, h=None):
        name = f"l{l}_{tag}"
        if h is None:
            h = rms_fwd(name + "_norm", x, gain)
        by_gate, by_up, act = self.carrying_gather(mm_swiglu, carry13, name + "_w13", h, self.w[(tag + '_w13', l)])
        out = self.carrying_gather(mm_nn, carry2, name + "_w2", act, self.w[(tag + '_w2', l)], 'row', F32, res=x,
                                   scale=0.5)
        return out, (x, h, by_gate, by_up, act)

    def ffn_bwd(self, tag, l, dx, dxb, saved, gain):
        w13, w2 = self.w[(tag + '_w13', l)], self.w[(tag + '_w2', l)]
        name = f"l{l}_{tag}"
        x, h, by_gate, by_up, act = saved
        d_gate, d_up = self.carrying_joins(mm_dswiglu, name + "_dact", dxb, w2, by_gate, by_up, 0.5)
        d_w2 = mm_tn(name + "_dw2", act, dxb, 'row', scale=0.5)
        half = N_CHIPS // 2
        d_w13, their_w2 = mm_tn(name + "_dw13g", h, d_gate, 'col', panels=(0, half), exchange=swap_exchange([d_w2]))
        d_w13 = mm_tn(name + "_dw13u", h, d_up, 'col', panels=(half, half), into=d_w13)
        self.reduce_begin(name, [(tag + '_w13', l), (tag + '_w2', l)], [d_w13, d_w2], [None] + their_w2)
        dh = self.carrying_scatter(mm_nt, name + "_dh", d_gate, w13, 'col', BF16, a_hi=d_up)
        return rms_bwd(name + "_dnorm", x, gain, dh, dx)


def kernel(x, mem, ffn1_norm, ffn1_w13, ffn1_w2, mix_norm, gmlp_w_in, gmlp_ln_g, gmlp_ln_b, gmlp_w_s, gmlp_b_s, gmlp_w_out, conv_w_in, conv_w, conv_w_out, xattn_norm, mem_norm, xattn_wq, xattn_wkv, xattn_wo, ffn2_norm, ffn2_w13, ffn2_w2, final_norm, loss_target, m_ffn1_norm, m_ffn1_w13, m_ffn1_w2, m_mix_norm, m_gmlp_w_in, m_gmlp_ln_g, m_gmlp_ln_b, m_gmlp_w_s, m_gmlp_b_s, m_gmlp_w_out, m_conv_w_in, m_conv_w, m_conv_w_out, m_xattn_norm, m_mem_norm, m_xattn_wq, m_xattn_wkv, m_xattn_wo, m_ffn2_norm, m_ffn2_w13, m_ffn2_w2, m_final_norm, v_ffn1_norm, v_ffn1_w13, v_ffn1_w2, v_mix_norm, v_gmlp_w_in, v_gmlp_ln_g, v_gmlp_ln_b, v_gmlp_w_s, v_gmlp_b_s, v_gmlp_w_out, v_conv_w_in, v_conv_w, v_conv_w_out, v_xattn_norm, v_mem_norm, v_xattn_wq, v_xattn_wkv, v_xattn_wo, v_ffn2_norm, v_ffn2_w13, v_ffn2_w2, v_final_norm):
    return _step(dict(locals()))


def _step(p):
    assert sorted(p) == sorted(ARG_NAMES)
    st = _Step(p)
    x = p['x'][0]
    mem = p['mem'][0]
    target = p['loss_target'][0]
    s, d = x.shape
    depth = st.depth

    for l in range(depth):
        for tag in ("ffn1", "mix", "xattn", "ffn2"):
            st.place(st.block_keys(tag, l))
    st.gather_alone("gather_first", [('ffn1_w13', 0)])

    cw_local = p['conv_w']
    n_conv, cwid, dq4 = cw_local.shape
    cw_rows = jnp.pad(cw_local.reshape(-1, LANE), ((0, (-cw_local.size // LANE) % 8), (0, 0)))
    cw_all = gather_all("gather_conv_w", cw_rows)[0::2, :cw_local.size // LANE]
    conv_w_full = cw_all.reshape(N_CHIPS, n_conv, cwid, dq4).transpose(1, 2, 0, 3).reshape(n_conv, cwid, N_CHIPS * dq4)

    saved = []
    for l in range(depth):
        j = l // 2
        rec = {}
        first_w2 = [('ffn1_w2', 0)] if l == 0 else []
        x, rec['ffn1'] = st.ffn_fwd("ffn1", l, x, p['ffn1_norm'][l],
                                    first_w2 + st.block_keys("mix", l) + st.block_keys("xattn", l), [('ffn2_w13', l)])
        h = rms_fwd(f"l{l}_mix_norm", x, p['mix_norm'][l])
        if l % 2 == 0:
            e = p['gmlp_ln_g'].shape[-1]
            bias = jnp.repeat(p['gmlp_b_s'][j].T, e // GMLP_GROUPS, axis=1)
            z = st.carrying_gather(mm_nn, [('ffn2_w2', l)], f"l{l}_gmlp_in", h, st.w[('gmlp_w_in', j)], 'col', BF16)
            gate = gmlp_fwd(f"l{l}_gmlp_gate", z, p['gmlp_ln_g'][j], p['gmlp_ln_b'][j], p['gmlp_w_s'][j], bias)
            x_new, hq = mm_nn(f"l{l}_gmlp_out", gate, st.w[('gmlp_w_out', j)], 'row', F32, res=x,
                              norm_gain=p['xattn_norm'][l])
            rec['mix'] = (x, h, z, gate, bias)
        else:
            bcv = st.carrying_gather(mm_nn, [('ffn2_w2', l)], f"l{l}_conv_in", h, st.w[('conv_w_in', j)], 'col', BF16)
            gate = conv_fwd(f"l{l}_conv_gate", bcv, conv_w_full[j])
            x_new, hq = mm_nn(f"l{l}_conv_out", gate, st.w[('conv_w_out', j)], 'row', F32, res=x,
                              norm_gain=p['xattn_norm'][l])
            rec['mix'] = (x, h, bcv, gate)
        x = x_new
        q = mm_nn(f"l{l}_xattn_q", hq, st.w[('xattn_wq', l)], 'row', BF16)
        mem_n = rms_fwd(f"l{l}_mem_norm", mem, p['mem_norm'][l])
        kv = mm_nn(f"l{l}_xattn_kv", mem_n, st.w[('xattn_wkv', l)], 'col', BF16)
        o = attn_fwd(f"l{l}_xattn_core", q, kv)
        x_new, h_ffn2 = mm_nn(f"l{l}_xattn_o", o, st.w[('xattn_wo', l)], 'row', F32, res=x,
                              norm_gain=p['ffn2_norm'][l])
        rec['xattn'] = (x, hq, q, mem_n, kv, o)
        x = x_new
        x, rec['ffn2'] = st.ffn_fwd("ffn2", l, x, p['ffn2_norm'][l], st.block_keys("ffn1", l + 1), [], h=h_ffn2)
        saved.append(rec)

    dx, dxb, d_final, loss_lanes = loss_head("loss_head", x, p['final_norm'], target)
    loss = lax.psum(0.5 * jnp.sum(loss_lanes) / d, ("x", "y", "c"))

    small = {n: [None] * p[n].shape[0] for n in ('ffn1_norm', 'mix_norm', 'xattn_norm', 'mem_norm', 'ffn2_norm',
                                                  'gmlp_ln_g', 'gmlp_ln_b', 'gmlp_w_s', 'gmlp_b_s', 'conv_w')}
    for l in reversed(range(depth)):
        j = l // 2
        rec = saved[l]
        dx, dxb, small['ffn2_norm'][l] = st.ffn_bwd("ffn2", l, dx, dxb, rec['ffn2'], p['ffn2_norm'][l])

        x_in, hq, q, mem_n, kv, o = rec['xattn']
        name = f"l{l}_xattn"
        do = st.carrying_joins(mm_nt, name + "_do", dxb, st.w[('xattn_wo', l)], 'row', BF16)
        d_wo = mm_tn(name + "_dwo", o, dxb, 'row')
        dq, dkv = attn_bwd(name + "_dcore", q, kv, do)
        d_wq = mm_tn(name + "_dwq", hq, dq, 'row')
        dkvb = dkv.astype(BF16)
        d_wkv = mm_tn(name + "_dwkv", mem_n, dkvb, 'col')
        st.reduce_begin(name, [('xattn_wq', l), ('xattn_wkv', l), ('xattn_wo', l)], [d_wq, d_wkv, d_wo])
        dh = mm_nt(name + "_dh", dq, st.w[('xattn_wq', l)], 'row', BF16)
        dx, dxb, small['xattn_norm'][l] = rms_bwd(name + "_dnorm", x_in, p['xattn_norm'][l], dh, dx)
        dmem_n = mm_nt(name + "_dmem", dkvb, st.w[('xattn_wkv', l)], 'col', F32)
        small['mem_norm'][l] = rms_bwd(f"l{l}_mem_dnorm", mem, p['mem_norm'][l], dmem_n, None)[2]

        if l % 2 == 0:
            x_in, h, z, gate, bias = rec['mix']
            name = f"l{l}_gmlp"
            w_in, w_out = st.w[('gmlp_w_in', j)], st.w[('gmlp_w_out', j)]
            dgate = mm_nt(name + "_dgate", dxb, w_out, 'row', BF16)
            d_wout = mm_tn(name + "_dwout", gate, dxb, 'row')
            dmix, dws, dbs, dlg, dlb = gmlp_bwd(name + "_dgate_core", z, dgate, p['gmlp_ln_g'][j], p['gmlp_ln_b'][j],
                                                p['gmlp_w_s'][j], bias)
            small['gmlp_w_s'][j], small['gmlp_b_s'][j] = dws, dbs[:, :GMLP_GROUPS].T
            small['gmlp_ln_g'][j], small['gmlp_ln_b'][j] = dlg, dlb
            keys = [('gmlp_w_in', j), ('gmlp_w_out', j)]
        else:
            x_in, h, bcv, gate = rec['mix']
            name = f"l{l}_conv"
            w_in, w_out = st.w[('conv_w_in', j)], st.w[('conv_w_out', j)]
            dgate = mm_nt(name + "_dgate", dxb, w_out, 'row', BF16)
            d_wout = mm_tn(name + "_dwout", gate, dxb, 'row')
            dmix, dcw = conv_bwd(name + "_dgate_core", bcv, dgate, conv_w_full[j])
            small['conv_w'][j] = dcw[:cwid]
            keys = [('conv_w_in', j), ('conv_w_out', j)]
        d_win = st.carrying_scatter(mm_tn, name + "_dwin", h, dmix, 'col')
        st.reduce_begin(name, keys, [d_win, d_wout])
        dh = st.carrying_scatter(mm_nt, name + "_dh", dmix, w_in, 'col', BF16)
        dx, dxb, small['mix_norm'][l] = rms_bwd(f"l{l}_mix_dnorm", x_in, p['mix_norm'][l], dh, dx)

        dx, dxb, small['ffn1_norm'][l] = st.ffn_bwd("ffn1", l, dx, dxb, rec['ffn1'], p['ffn1_norm'][l])
    st.joins_alone("join_last")

    small_names = ['ffn1_norm', 'mix_norm', 'xattn_norm', 'mem_norm', 'ffn2_norm', 'gmlp_ln_g', 'gmlp_ln_b', 'gmlp_w_s',
                   'gmlp_b_s', 'final_norm', 'conv_w']
    small_full = {n: jnp.stack([g.reshape(p[n].shape[1:]) for g in small[n]]) for n in small_names
                  if n not in ('final_norm', 'conv_w')}
    small_full['final_norm'] = d_final.reshape(p['final_norm'].shape)
    small_full['conv_w'] = jnp.stack(small['conv_w'])
    packed = jnp.concatenate([small_full[n].reshape(-1, LANE) for n in small_names], axis=0)
    total = sum_leading("small_sum", gather_all("small_gather", packed))
    small_g, at = {}, 0
    for n in small_names:
        rows = small_full[n].size // LANE
        small_g[n] = total[at:at + rows].reshape(small_full[n].shape)
        at += rows
    small_g['conv_w'] = lax.dynamic_slice_in_dim(small_g['conv_w'], st.chip * dq4, dq4, axis=2)

    grads, deltas, new_m, new_v = {}, {}, {}, {}
    for n in WEIGHTS:
        w, m, v = p[n], p['m_' + n], p['v_' + n]
        if n in BIG:
            carried = None
            for i in range(w.shape[0]):
                carried = adamw_layer(f"adamw_{n}{i}", w, m, v, st.big_g[(n, i)], i, carried)
            grads[n], deltas[n], new_m[n], new_v[n] = carried
        else:
            g = small_g[n]
            out = adamw_flat(f"adamw_{n}", _as_rows(w), _as_rows(m), _as_rows(v), _as_rows(g))
            grads[n] = g
            deltas[n], new_m[n], new_v[n] = (o.reshape(w.shape) for o in out)

    grad_x = dx.reshape(p['x'].shape)
    return (loss, grad_x, *[grads[n] for n in WEIGHTS], *[deltas[n] for n in WEIGHTS], *[new_m[n] for n in WEIGHTS],
            *[new_v[n] for n in WEIGHTS])
```

```python
from typing import Callable, NamedTuple

import jax
import jax.numpy as jnp
from jax import lax
from jax.experimental import pallas as pl
from jax.experimental.pallas import tpu as pltpu

F32 = jnp.float32
BF16 = jnp.bfloat16
MESH = pl.DeviceIdType.MESH

CHUNK = 128
GMLP_GROUPS = 8
XATTN_HEADS = 4
RMS_EPS = 1e-6
LN_EPS = 1e-5
ADAM_LR = 0.001
ADAM_B1 = 0.9
ADAM_B2 = 0.999
ADAM_EPS = 1e-08
ADAM_WD = 0.01
ADAM_STEP = 10

N_CHIPS = 4
N_DEV = 8

VMEM_LIMIT_BYTES = 58 * 2**20
VMEM_PLAN_BYTES = 48 * 2**20
LANE = 128
MXU_DIM = 256
ACC_CHUNK = 2 * MXU_DIM
MXU_FLOPS_PER_US = 996e6
HBM_BYTES_PER_US = 3.3e6
STEP_US = 0.35
ACC_US_PER_VREG = 0.58e-3

WEIGHTS = ['ffn1_norm', 'ffn1_w13', 'ffn1_w2', 'mix_norm', 'gmlp_w_in', 'gmlp_ln_g', 'gmlp_ln_b', 'gmlp_w_s',
           'gmlp_b_s', 'gmlp_w_out', 'conv_w_in', 'conv_w', 'conv_w_out', 'xattn_norm', 'mem_norm', 'xattn_wq',
           'xattn_wkv', 'xattn_wo', 'ffn2_norm', 'ffn2_w13', 'ffn2_w2', 'final_norm']
BIG = {'ffn1_w13': 'col', 'ffn1_w2': 'row', 'gmlp_w_in': 'col', 'gmlp_w_out': 'row', 'conv_w_in': 'col',
       'conv_w_out': 'row', 'xattn_wq': 'row', 'xattn_wkv': 'col', 'xattn_wo': 'row', 'ffn2_w13': 'col',
       'ffn2_w2': 'row'}
ARG_NAMES = (['x', 'mem'] + WEIGHTS + ['loss_target'] + ['m_' + n for n in WEIGHTS] + ['v_' + n for n in WEIGHTS])


def _params(**kw):
    return pltpu.CompilerParams(vmem_limit_bytes=VMEM_LIMIT_BYTES, **kw)


def _divisors(n, mult, cap):
    return [d for d in range(mult, min(n, cap) + 1, mult) if n % d == 0] or [n]


def _row_tile(rows, width_bytes, budget=4 * 2**20):
    best = None
    for d in _divisors(rows, 16, 1024):
        if d * width_bytes <= budget:
            best = d
    return best or _divisors(rows, 16, 1024)[0]


ANY = pl.BlockSpec(memory_space=pl.ANY)


class Exchange(NamedTuple):
    inputs: list
    out_shapes: list
    aliases: dict
    n_sems: int
    start: Callable
    finish: Callable


def _place():
    x, y, c = lax.axis_index("x"), lax.axis_index("y"), lax.axis_index("c")
    chips = [(1 - x, y), (x, 1 - y), (1 - x, 1 - y)]
    return x, y, c, 2 * x + y, chips


def _remote(src, dst, sems, k, device):
    return pltpu.make_async_remote_copy(src_ref=src, dst_ref=dst, send_sem=sems[0].at[k], recv_sem=sems[1].at[k],
                                        device_id=device, device_id_type=MESH)


def _mxu_fill(dim):
    return dim / (-(-dim // MXU_DIM) * MXU_DIM)


def _tile_time(flops, fill, traffic, steps, acc_vregs):
    return (max(flops / (MXU_FLOPS_PER_US * fill), traffic / HBM_BYTES_PER_US) + steps * STEP_US
            + steps * acc_vregs * ACC_US_PER_VREG)


def _plan_mm(m, n_tiles_of, k_tiles_of, n, k, a_item, o_item, has_res, a_arrays=1):
    best, best_cost = None, None
    for tm in _divisors(m, 16, 1024):
        for tn in n_tiles_of:
            for tk in k_tiles_of:
                ni, nj, nk = m // tm, n // tn, k // tk
                blocks = a_arrays * tm * tk * a_item + tk * tn * 2 + tm * tn * o_item + (tm * tn * 4 if has_res else 0)
                vmem = 2 * blocks + tm * tn * 4 * (2 if nk > 1 else 1)
                if vmem > VMEM_PLAN_BYTES:
                    continue
                traffic = nj * m * k * a_item + (k * n * 2 if nk == 1 else ni * k * n * 2)
                traffic += m * n * (o_item + (4 if has_res else 0))
                cost = _tile_time(2 * m * n * k, _mxu_fill(tk) * _mxu_fill(tn), traffic, ni * nj * nk,
                                  tm * tn // 1024 if nk > 1 else 0)
                if best_cost is None or cost < best_cost:
                    best, best_cost = (tm, tn, tk), cost
    assert best is not None, (m, n, k)
    return best


def _tiled_call(name, grid, operands, in_specs, out_shapes, out_specs, scratch, compute, exchange=None, aliases=None):
    n_reg, n_out, n_scr = len(operands), len(out_shapes), len(scratch)
    n_xin = len(exchange.inputs) if exchange else 0
    n_xout = len(exchange.out_shapes) if exchange else 0
    semantics = ("arbitrary",) * len(grid)

    def body(*refs):
        ins = refs[:n_reg]
        outs = refs[n_reg + n_xin:n_reg + n_xin + n_out]
        scr = refs[n_reg + n_xin + n_out + n_xout:n_reg + n_xin + n_out + n_xout + n_scr]
        if not exchange:
            compute(ins, outs, scr)
            return
        x_ins = refs[n_reg:n_reg + n_xin]
        x_outs = refs[n_reg + n_xin + n_out:n_reg + n_xin + n_out + n_xout]
        sems = refs[-2:]
        at_first, at_last = True, True
        for k, extent in enumerate(grid):
            at_first = jnp.logical_and(at_first, pl.program_id(k) == 0)
            at_last = jnp.logical_and(at_last, pl.program_id(k) == extent - 1)

        @pl.when(at_first)
        def _():
            exchange.start(x_ins, x_outs, sems)

        compute(ins, outs, scr)

        @pl.when(at_last)
        def _():
            exchange.finish(x_ins, x_outs, sems)

    if not exchange:
        return pl.pallas_call(
            body, name=name, grid=grid, in_specs=in_specs, out_specs=out_specs, out_shape=out_shapes,
            scratch_shapes=scratch, input_output_aliases=dict(aliases or {}),
            compiler_params=_params(dimension_semantics=semantics),
        )(*operands)
    assert not aliases
    sems = [pltpu.SemaphoreType.DMA((exchange.n_sems,)), pltpu.SemaphoreType.DMA((exchange.n_sems,))]
    got = pl.pallas_call(
        body, name=name, grid=grid, in_specs=in_specs + [ANY] * n_xin, out_specs=out_specs + [ANY] * n_xout,
        out_shape=out_shapes + list(exchange.out_shapes), scratch_shapes=scratch + sems,
        input_output_aliases={n_reg + i: n_out + o for i, o in exchange.aliases.items()},
        compiler_params=_params(dimension_semantics=semantics),
    )(*operands, *exchange.inputs)
    return list(got[:n_out]), list(got[n_out:])


class Split(NamedTuple):
    slot: int
    other: jax.Array
    spec: pl.BlockSpec
    use_other: Callable


def _mm_call(name, grid, operands, in_specs, out_shape, out_spec, contract, nk, scale, has_res, tile, exchange=None,
             split=None, into=None, normed=False):
    n_main = len(operands)
    out_shapes, out_specs = [out_shape], [out_spec]
    if normed:
        out_shapes, out_specs = out_shapes + [jax.ShapeDtypeStruct(out_shape.shape, BF16)], out_specs + [out_spec]

    def compute(ins, outs, scr):
        res_ref = ins[2] if has_res else None
        o_ref = outs[0]
        acc_ref = scr[0] if nk > 1 else None

        def finish(v):
            if scale != 1.0:
                v = v * scale
            if has_res:
                v = res_ref[...] + v
            o_ref[...] = v.astype(o_ref.dtype)
            if normed:
                outs[1][...] = _rms_rows(v, ins[2 + has_res][...])[2].astype(BF16)

        def b_block(cols):
            ref = b_ref_of[0]
            stacked = len(ref.shape) == 3
            if contract[0][1] == (1,):
                if not stacked:
                    return ref[cols, :]
                return ref[...].reshape(ref.shape[0] * ref.shape[1], ref.shape[2])[cols, :]
            if not stacked:
                return ref[:, cols]
            b = ref[:, :, cols]
            return b.reshape(b.shape[0] * b.shape[1], b.shape[2])

        b_ref_of = [None]

        def contribute(a_ref, b_ref):
            b_ref_of[0] = b_ref
            if nk == 1:
                finish(lax.dot_general(a_ref[...], b_block(slice(None)), contract, preferred_element_type=F32))
                return
            kk = pl.program_id(2)

            @pl.when(kk == 0)
            def _():
                acc_ref[...] = jnp.zeros(tile, F32)

            a = a_ref[...]
            for start in range(0, tile[1], ACC_CHUNK):
                cols = slice(start, min(start + ACC_CHUNK, tile[1]))
                acc_ref[:, cols] += lax.dot_general(a, b_block(cols), contract, preferred_element_type=F32)

            @pl.when(kk == nk - 1)
            def _():
                finish(acc_ref[...])

        if split is None:
            contribute(ins[0], ins[1])
            return
        use_other = split.use_other(pl.program_id(0), pl.program_id(1), pl.program_id(2))
        pair = [ins[0], ins[1]]
        other = list(pair)
        other[split.slot] = ins[n_main]

        @pl.when(jnp.logical_not(use_other))
        def _():
            contribute(*pair)

        @pl.when(use_other)
        def _():
            contribute(*other)

    aliases = None
    if split is not None:
        operands, in_specs = operands + [split.other], in_specs + [split.spec]
    if into is not None:
        aliases = {len(operands): 0}
        operands, in_specs = operands + [into], in_specs + [ANY]
    got = _tiled_call(name, grid, operands, in_specs, out_shapes, out_specs,
                      [pltpu.VMEM(tile, F32)] if nk > 1 else [], compute, exchange, aliases)
    results, carried = (got[0], got[1]) if exchange else (got, None)
    out = tuple(results) if normed else results[0]
    return (out, carried) if exchange else out


def mm_nn(name, a, w, kind, out_dtype, res=None, scale=1.0, exchange=None, norm_gain=None):
    m, k = a.shape
    p, r, c = w.shape
    n = p * c if kind == 'col' else c
    assert k == (r if kind == 'col' else p * r), (name, a.shape, w.shape)
    n_tiles = _divisors(c, LANE, 2816)
    if norm_gain is not None:
        assert kind == 'row'
        n_tiles = [n]
    k_tiles = _divisors(r, LANE, 4096)
    if kind == 'row':
        k_tiles = k_tiles + [q * r for q in (2, 4) if p % q == 0]
    o_item = jnp.dtype(out_dtype).itemsize + (2 if norm_gain is not None else 0)
    tm, tn, tk = _plan_mm(m, n_tiles, k_tiles, n, k, a.dtype.itemsize, o_item, res is not None)
    nk = k // tk
    if kind == 'col':
        cpt = c // tn
        w_spec = pl.BlockSpec((None, tk, tn), lambda j, i, kk: (j // cpt, kk, j % cpt))
    elif tk > r:
        w_spec = pl.BlockSpec((tk // r, r, tn), lambda j, i, kk: (kk, 0, j))
    else:
        rpt = r // tk
        w_spec = pl.BlockSpec((None, tk, tn), lambda j, i, kk: (kk // rpt, kk % rpt, j))
    in_specs = [pl.BlockSpec((tm, tk), lambda j, i, kk: (i, kk)), w_spec]
    operands = [a, w]
    if res is not None:
        in_specs.append(pl.BlockSpec((tm, tn), lambda j, i, kk: (i, j)))
        operands.append(res)
    if norm_gain is not None:
        in_specs.append(pl.BlockSpec((1, tn), lambda j, i, kk: (0, 0)))
        operands.append(norm_gain.reshape(1, n))
    return _mm_call(name, (n // tn, m // tm, nk), operands, in_specs, jax.ShapeDtypeStruct((m, n), out_dtype),
                    pl.BlockSpec((tm, tn), lambda j, i, kk: (i, j)), (((1,), (0,)), ((), ())), nk, scale,
                    res is not None, (tm, tn), exchange, normed=norm_gain is not None)


def mm_nt(name, a, w, kind, out_dtype, scale=1.0, exchange=None, a_hi=None):
    m, kc = a.shape
    if a_hi is not None:
        assert a_hi.shape == a.shape
        kc = 2 * kc
    p, r, c = w.shape
    n = r if kind == 'col' else p * r
    assert kc == (p * c if kind == 'col' else c), (name, a.shape, w.shape)
    n_tiles = _divisors(r, LANE, 2816)
    k_tiles = _divisors(c, LANE, 4096)
    if kind == 'row':
        n_tiles = n_tiles + [q * r for q in (2, 4) if p % q == 0 and q * r <= 2816]
    tm, tn, tk = _plan_mm(m, n_tiles, k_tiles, n, kc, a.dtype.itemsize, jnp.dtype(out_dtype).itemsize, False,
                          1 if a_hi is None else 2)
    nk = kc // tk
    if kind == 'col':
        cpt = c // tk
        w_spec = pl.BlockSpec((None, tn, tk), lambda j, i, kk: (kk // cpt, j, kk % cpt))
    elif tn > r:
        w_spec = pl.BlockSpec((tn // r, r, tk), lambda j, i, kk: (j, 0, kk))
    else:
        rpt = r // tn
        w_spec = pl.BlockSpec((None, tn, tk), lambda j, i, kk: (j // rpt, j % rpt, kk))
    split = None
    a_spec = pl.BlockSpec((tm, tk), lambda j, i, kk: (i, kk))
    if a_hi is not None:
        half = nk // 2
        assert nk % 2 == 0
        a_spec = pl.BlockSpec((tm, tk), lambda j, i, kk: (i, jnp.minimum(kk, half - 1)))
        split = Split(0, a_hi, pl.BlockSpec((tm, tk), lambda j, i, kk: (i, jnp.maximum(kk - half, 0))),
                      lambda j, i, kk: kk >= half)
    return _mm_call(name, (n // tn, m // tm, nk), [a, w], [a_spec, w_spec], jax.ShapeDtypeStruct((m, n), out_dtype),
                    pl.BlockSpec((tm, tn), lambda j, i, kk: (i, j)), (((1,), (1,)), ((), ())), nk, scale, False,
                    (tm, tn), exchange, split)


def _plan_tn(s, ka, nd, r_tiles, n_tiles):
    best, best_cost = None, None
    for ts in _divisors(s, 16, 2048):
        for tr in r_tiles:
            for tn in n_tiles:
                ni, nj, ns = ka // tr, nd // tn, s // ts
                vmem = 2 * (ts * tr * 2 + ts * tn * 2 + tr * tn * 2) + tr * tn * 4 * (2 if ns > 1 else 1)
                if vmem > VMEM_PLAN_BYTES:
                    continue
                traffic = nj * s * ka * 2 + ni * s * nd * 2 + ka * nd * 2
                cost = _tile_time(2 * s * ka * nd, _mxu_fill(ts) * _mxu_fill(tn), traffic, ni * nj * ns,
                                  tr * tn // 1024 if ns > 1 else 0)
                if best_cost is None or cost < best_cost:
                    best, best_cost = (ts, tr, tn), cost
    assert best is not None, (s, ka, nd)
    return best


def mm_tn(name, a, dy, kind, scale=1.0, exchange=None, panels=(0, N_CHIPS), into=None):
    s, ka = a.shape
    s2, nd = dy.shape
    assert s == s2
    p = N_CHIPS
    first_panel, n_panels = panels
    assert kind == 'col' or panels == (0, p)
    r, c = (ka, nd // n_panels) if kind == 'col' else (ka // p, nd)
    ts, tr, tn = _plan_tn(s, ka, nd, _divisors(r, LANE, 2048), _divisors(c, LANE, 2816))
    ns = s // ts
    if kind == 'col':
        cpt = c // tn
        o_spec = pl.BlockSpec((None, tr, tn), lambda j, i, kk: (first_panel + j // cpt, i, j % cpt))
    else:
        rpt = r // tr
        o_spec = pl.BlockSpec((None, tr, tn), lambda j, i, kk: (i // rpt, i % rpt, j))
    in_specs = [pl.BlockSpec((ts, tr), lambda j, i, kk: (kk, i)), pl.BlockSpec((ts, tn), lambda j, i, kk: (kk, j))]
    return _mm_call(name, (nd // tn, ka // tr, ns), [a, dy], in_specs, jax.ShapeDtypeStruct((p, r, c), BF16), o_spec,
                    (((0,), (0,)), ((), ())), ns, scale, False, (tr, tn), exchange, None, into)


def _plan_fused(m, k, f, tiles, n_w, n_io):
    best, best_cost = None, None
    for tm in _divisors(m, 16, 1024):
        for tn in tiles:
            vmem = 2 * (tm * k * 2 + n_io * tm * tn * 2) + n_w * k * tn * 2 + 4 * tm * tn * 4
            if vmem > VMEM_PLAN_BYTES:
                continue
            traffic = (f // tn) * m * k * 2 + n_w * k * f * 2 + n_io * m * f * 2
            cost = _tile_time(2 * m * k * f * n_w, _mxu_fill(tn), traffic, (f // tn) * (m // tm), 0)
            if best_cost is None or cost < best_cost:
                best, best_cost = (tm, tn), cost
    assert best is not None, (m, k, f)
    return best


def mm_swiglu(name, h, w13, exchange=None):
    m, k = h.shape
    p, r, c = w13.shape
    assert r == k and p % 2 == 0
    f = p * c // 2
    tm, tn = _plan_fused(m, k, f, _divisors(c, LANE, 2816), 2, 3)
    cpt = c // tn

    def compute(ins, outs, scr):
        a = ins[0][...]
        g = jnp.dot(a, ins[1][...], preferred_element_type=F32)
        u = jnp.dot(a, ins[2][...], preferred_element_type=F32)
        sg = _sigmoid(g)
        silu = g * sg
        outs[0][...] = (u * (sg * (1.0 + g * (1.0 - sg)))).astype(BF16)
        outs[1][...] = silu.astype(BF16)
        outs[2][...] = (silu * u).astype(BF16)

    tile = pl.BlockSpec((tm, tn), lambda j, i: (i, j))
    in_specs = [pl.BlockSpec((tm, k), lambda j, i: (i, 0)),
                pl.BlockSpec((None, k, tn), lambda j, i: (j // cpt, 0, j % cpt), pipeline_mode=pl.Buffered(1)),
                pl.BlockSpec((None, k, tn), lambda j, i: (j // cpt + p // 2, 0, j % cpt),
                             pipeline_mode=pl.Buffered(1))]
    shape = jax.ShapeDtypeStruct((m, f), BF16)
    got = _tiled_call(name, (f // tn, m // tm), [h, w13, w13], in_specs, [shape] * 3, [tile] * 3, [], compute, exchange)
    return got


def mm_dswiglu(name, dy, w2, by_gate, by_up, scale, exchange=None):
    m, k = dy.shape
    p, r, c = w2.shape
    assert c == k
    f = p * r
    tiles = _divisors(r, LANE, 2816) + [q * r for q in (2, 4) if p % q == 0 and q * r <= 2816]
    tm, tn = _plan_fused(m, k, f, tiles, 1, 4)

    def compute(ins, outs, scr):
        b = ins[1][...]
        if b.ndim == 3:
            b = b.reshape(b.shape[0] * b.shape[1], b.shape[2])
        d = lax.dot_general(ins[0][...], b, (((1,), (1,)), ((), ())), preferred_element_type=F32) * scale
        outs[0][...] = (d * ins[2][...].astype(F32)).astype(BF16)
        outs[1][...] = (d * ins[3][...].astype(F32)).astype(BF16)

    tile = pl.BlockSpec((tm, tn), lambda j, i: (i, j))
    if tn > r:
        w_spec = pl.BlockSpec((tn // r, r, k), lambda j, i: (j, 0, 0), pipeline_mode=pl.Buffered(1))
    else:
        rpt = r // tn
        w_spec = pl.BlockSpec((None, tn, k), lambda j, i: (j // rpt, j % rpt, 0), pipeline_mode=pl.Buffered(1))
    in_specs = [pl.BlockSpec((tm, k), lambda j, i: (i, 0)), w_spec, tile, tile]
    shape = jax.ShapeDtypeStruct((m, f), BF16)
    return _tiled_call(name, (f // tn, m // tm), [dy, w2, by_gate, by_up], in_specs, [shape] * 2, [tile] * 2, [], compute,
                       exchange)


def _rms_rows(x, g):
    r = lax.rsqrt(jnp.mean(x * x, axis=-1, keepdims=True) + RMS_EPS)
    xhat = x * r
    return xhat, r, xhat * g


def rms_fwd(name, x, g, exchange=None):
    s, d = x.shape
    tm = _row_tile(s, d * 4)

    def compute(ins, outs, scr):
        outs[0][...] = _rms_rows(ins[0][...], ins[1][...])[2].astype(BF16)

    got = _tiled_call(name, (s // tm,), [x, g.reshape(1, d)],
                      [pl.BlockSpec((tm, d), lambda i: (i, 0)), pl.BlockSpec((1, d), lambda i: (0, 0))],
                      [jax.ShapeDtypeStruct((s, d), BF16)], [pl.BlockSpec((tm, d), lambda i: (i, 0))], [], compute,
                      exchange)
    return (got[0][0], got[1]) if exchange else got[0]


def _rms_bwd_rows(x, g, dh):
    xhat, r, _ = _rms_rows(x, g)
    u = dh * g
    dx = r * (u - xhat * jnp.mean(u * xhat, axis=-1, keepdims=True))
    return dx, jnp.sum(dh * xhat, axis=0, keepdims=True)


def rms_bwd(name, x, g, dh, dres):
    s, d = x.shape
    tm = _row_tile(s, d * 4, 2 * 2**20)
    has_res = dres is not None

    def body(*refs):
        x_ref, g_ref, dh_ref = refs[:3]
        dres_ref = refs[3] if has_res else None
        dx_ref, dxb_ref, dg_ref = refs[-3:]
        dx, dg = _rms_bwd_rows(x_ref[...], g_ref[...], dh_ref[...].astype(F32))
        if has_res:
            dx = dx + dres_ref[...]
        dx_ref[...] = dx
        dxb_ref[...] = dx.astype(BF16)

        @pl.when(pl.program_id(0) == 0)
        def _():
            dg_ref[...] = dg

        @pl.when(pl.program_id(0) > 0)
        def _():
            dg_ref[...] += dg

    row = pl.BlockSpec((tm, d), lambda i: (i, 0))
    vec = pl.BlockSpec((1, d), lambda i: (0, 0))
    return pl.pallas_call(
        body, name=name, grid=(s // tm,),
        in_specs=[row, vec, row] + ([row] if has_res else []),
        out_specs=[row, row, vec],
        out_shape=[jax.ShapeDtypeStruct((s, d), F32), jax.ShapeDtypeStruct((s, d), BF16),
                   jax.ShapeDtypeStruct((1, d), F32)],
        compiler_params=_params(dimension_semantics=("arbitrary",)),
    )(x, g.reshape(1, d), dh, *([dres] if has_res else []))


def loss_head(name, x, g, target):
    s, d = x.shape
    tm = _row_tile(s, d * 4, 2 * 2**20)

    def body(x_ref, g_ref, t_ref, dx_ref, dxb_ref, dg_ref, loss_ref):
        x = x_ref[...]
        gain = g_ref[...]
        y = _rms_rows(x, gain)[2]
        diff = y - t_ref[...]
        dx, dg = _rms_bwd_rows(x, gain, diff * (1.0 / d))
        dx_ref[...] = dx
        dxb_ref[...] = dx.astype(BF16)
        sq = jnp.sum(diff * diff, axis=0, keepdims=True)

        @pl.when(pl.program_id(0) == 0)
        def _():
            dg_ref[...] = dg
            loss_ref[...] = sq

        @pl.when(pl.program_id(0) > 0)
        def _():
            dg_ref[...] += dg
            loss_ref[...] += sq

    row = pl.BlockSpec((tm, d), lambda i: (i, 0))
    vec = pl.BlockSpec((1, d), lambda i: (0, 0))
    return pl.pallas_call(
        body, name=name, grid=(s // tm,), in_specs=[row, vec, row], out_specs=[row, row, vec, vec],
        out_shape=[jax.ShapeDtypeStruct((s, d), F32), jax.ShapeDtypeStruct((s, d), BF16),
                   jax.ShapeDtypeStruct((1, d), F32), jax.ShapeDtypeStruct((1, d), F32)],
        compiler_params=_params(dimension_semantics=("arbitrary",)),
    )(x, g.reshape(1, d), target)


def _sigmoid(x):
    return 0.5 * jnp.tanh(0.5 * x) + 0.5


_INV_SQRT2 = 0.7071067811865476
_INV_SQRT_2PI = 0.3989422804014327


def _normal_cdf(z):
    return 0.5 * (1.0 + lax.erf(z * _INV_SQRT2))


def _gelu_grad(z, cdf):
    return cdf + z * (_INV_SQRT_2PI * jnp.exp(-0.5 * z * z))


def _causal_weights(ws_ref, g):
    t = ws_ref.shape[-1]
    keep = lax.broadcasted_iota(jnp.int32, (t, t), 0) >= lax.broadcasted_iota(jnp.int32, (t, t), 1)
    return jnp.where(keep, ws_ref[g], 0.0).astype(BF16), keep


def _gmlp_gate_rows(z_ref, lg_ref, lb_ref, e):
    z = z_ref[...].astype(F32)
    cdf = _normal_cdf(z)
    gz = z * cdf
    u, v = gz[:, :e], gz[:, e:]
    mu = jnp.mean(v, axis=-1, keepdims=True)
    xc = v - mu
    rs = lax.rsqrt(jnp.mean(xc * xc, axis=-1, keepdims=True) + LN_EPS)
    vhat = xc * rs
    return (z, cdf), u, vhat, rs, vhat * lg_ref[...] + lb_ref[...]


def gmlp_fwd(name, z, ln_g, ln_b, w_s, bias):
    s, e2 = z.shape
    e = e2 // 2
    eg = e // GMLP_GROUPS

    def body(z_ref, lg_ref, lb_ref, ws_ref, b_ref, o_ref):
        _, u, _, _, vln = _gmlp_gate_rows(z_ref, lg_ref, lb_ref, e)
        vb = vln.astype(BF16)
        for g in range(GMLP_GROUPS):
            cols = slice(g * eg, (g + 1) * eg)
            wm, _ = _causal_weights(ws_ref, g)
            f = jnp.dot(wm, vb[:, cols], preferred_element_type=F32) + b_ref[:, cols]
            o_ref[:, cols] = (u[:, cols] * f).astype(BF16)

    full = lambda shape: pl.BlockSpec(shape, lambda i: (0,) * len(shape))
    return pl.pallas_call(
        body, name=name, grid=(s // CHUNK,),
        in_specs=[pl.BlockSpec((CHUNK, e2), lambda i: (i, 0)), full((1, e)), full((1, e)),
                  full((GMLP_GROUPS, CHUNK, CHUNK)), full((CHUNK, e))],
        out_specs=pl.BlockSpec((CHUNK, e), lambda i: (i, 0)), out_shape=jax.ShapeDtypeStruct((s, e), BF16),
        compiler_params=_params(dimension_semantics=("arbitrary",)),
    )(z, ln_g.reshape(1, e), ln_b.reshape(1, e), w_s, bias)


def gmlp_bwd(name, z, dp, ln_g, ln_b, w_s, bias):
    s, e2 = z.shape
    e = e2 // 2
    eg = e // GMLP_GROUPS
    t = CHUNK

    def body(z_ref, dp_ref, lg_ref, lb_ref, ws_ref, b_ref, dz_ref, dws_ref, dbs_ref, dlg_ref, dlb_ref):
        first = pl.program_id(0) == 0
        (zf, cdf), u, vhat, rs, vln = _gmlp_gate_rows(z_ref, lg_ref, lb_ref, e)
        vb = vln.astype(BF16)
        dp = dp_ref[...].astype(F32)
        lane = lax.broadcasted_iota(jnp.int32, (t, LANE), 1)
        dbs = jnp.zeros((t, LANE), F32)
        dvln_parts = []
        for g in range(GMLP_GROUPS):
            cols = slice(g * eg, (g + 1) * eg)
            wm, keep = _causal_weights(ws_ref, g)
            f = jnp.dot(wm, vb[:, cols], preferred_element_type=F32) + b_ref[:, cols]
            dz_ref[:, cols] = (dp[:, cols] * f * _gelu_grad(zf[:, cols], cdf[:, cols])).astype(BF16)
            df = dp[:, cols] * u[:, cols]
            dfb = df.astype(BF16)
            dbs = dbs + jnp.where(lane == g, jnp.sum(df, axis=-1, keepdims=True), 0.0)
            dw = lax.dot_general(dfb, vb[:, cols], (((1,), (1,)), ((), ())), preferred_element_type=F32)
            dw = jnp.where(keep, dw, 0.0)

            @pl.when(first)
            def _():
                dws_ref[g] = dw

            @pl.when(jnp.logical_not(first))
            def _():
                dws_ref[g] += dw

            dvln_parts.append(lax.dot_general(wm, dfb, (((0,), (0,)), ((), ())), preferred_element_type=F32))
        dvln = jnp.concatenate(dvln_parts, axis=-1)
        dvhat = dvln * lg_ref[...]
        dv = rs * (dvhat - jnp.mean(dvhat, axis=-1, keepdims=True)
                   - vhat * jnp.mean(dvhat * vhat, axis=-1, keepdims=True))
        dz_ref[:, e:] = (dv * _gelu_grad(zf[:, e:], cdf[:, e:])).astype(BF16)
        dlg = jnp.sum(dvln * vhat, axis=0, keepdims=True)
        dlb = jnp.sum(dvln, axis=0, keepdims=True)

        @pl.when(first)
        def _():
            dbs_ref[...] = dbs
            dlg_ref[...] = dlg
            dlb_ref[...] = dlb

        @pl.when(jnp.logical_not(first))
        def _():
            dbs_ref[...] += dbs
            dlg_ref[...] += dlg
            dlb_ref[...] += dlb

    full = lambda shape: pl.BlockSpec(shape, lambda i: (0,) * len(shape))
    return pl.pallas_call(
        body, name=name, grid=(s // t,),
        in_specs=[pl.BlockSpec((t, e2), lambda i: (i, 0)), pl.BlockSpec((t, e), lambda i: (i, 0)), full((1, e)),
                  full((1, e)), full((GMLP_GROUPS, t, t)), full((t, e))],
        out_specs=[pl.BlockSpec((t, e2), lambda i: (i, 0)), full((GMLP_GROUPS, t, t)), full((t, LANE)), full((1, e)),
                   full((1, e))],
        out_shape=[jax.ShapeDtypeStruct((s, e2), BF16), jax.ShapeDtypeStruct((GMLP_GROUPS, t, t), F32),
                   jax.ShapeDtypeStruct((t, LANE), F32), jax.ShapeDtypeStruct((1, e), F32),
                   jax.ShapeDtypeStruct((1, e), F32)],
        compiler_params=_params(dimension_semantics=("arbitrary",)),
    )(z, dp, ln_g.reshape(1, e), ln_b.reshape(1, e), w_s, bias)


EDGE = 16


def _shift_down(zc, prev, k):
    tm = zc.shape[0]
    row = lax.broadcasted_iota(jnp.int32, (tm, 1), 0)
    out = pltpu.roll(zc, k, 0)
    for j in range(k):
        out = jnp.where(row == j, prev[EDGE - k + j:EDGE - k + j + 1, :], out)
    return out


def _shift_up(dc, nxt, k):
    tm = dc.shape[0]
    row = lax.broadcasted_iota(jnp.int32, (tm, 1), 0)
    out = pltpu.roll(dc, tm - k, 0)
    for j in range(k):
        out = jnp.where(row == tm - k + j, nxt[j:j + 1, :], out)
    return out


def conv_fwd(name, bcv, cw):
    s, d3 = bcv.shape
    d = d3 // 3
    tm = _row_tile(s, d * 4, 2 * 2**20)
    per = tm // EDGE

    def body(b_ref, c_ref, v_ref, cp_ref, vp_ref, w_ref, o_ref):
        i = pl.program_id(0)
        zc = c_ref[...].astype(F32) * v_ref[...].astype(F32)
        prev = jnp.where(i > 0, cp_ref[...].astype(F32) * vp_ref[...].astype(F32), 0.0)
        conv = w_ref[2:3, :] * zc + w_ref[1:2, :] * _shift_down(zc, prev, 1) + w_ref[0:1, :] * _shift_down(zc, prev, 2)
        o_ref[...] = (b_ref[...].astype(F32) * conv).astype(BF16)

    blk = lambda col: pl.BlockSpec((tm, d), lambda i: (i, col))
    edge = lambda col: pl.BlockSpec((EDGE, d), lambda i: (jnp.maximum(i * per - 1, 0), col))
    return pl.pallas_call(
        body, name=name, grid=(s // tm,),
        in_specs=[blk(0), blk(1), blk(2), edge(1), edge(2), pl.BlockSpec((3, d), lambda i: (0, 0))],
        out_specs=pl.BlockSpec((tm, d), lambda i: (i, 0)), out_shape=jax.ShapeDtypeStruct((s, d), BF16),
        compiler_params=_params(dimension_semantics=("arbitrary",)),
    )(bcv, bcv, bcv, bcv, bcv, cw)


def conv_bwd(name, bcv, dq, cw):
    s, d3 = bcv.shape
    d = d3 // 3
    tm = _row_tile(s, d * 4, 2**20)
    per = tm // EDGE
    n_tiles = s // tm
    last_edge = s // EDGE - 1

    def body(b_ref, c_ref, v_ref, cp_ref, vp_ref, bn_ref, dq_ref, dqn_ref, w_ref, o_ref, dw_ref):
        i = pl.program_id(0)
        b = b_ref[...].astype(F32)
        c = c_ref[...].astype(F32)
        v = v_ref[...].astype(F32)
        dq = dq_ref[...].astype(F32)
        zc = c * v
        prev = jnp.where(i > 0, cp_ref[...].astype(F32) * vp_ref[...].astype(F32), 0.0)
        z1 = _shift_down(zc, prev, 1)
        z2 = _shift_down(zc, prev, 2)
        w0, w1, w2 = w_ref[0:1, :], w_ref[1:2, :], w_ref[2:3, :]
        conv = w2 * zc + w1 * z1 + w0 * z2
        dconv = dq * b
        nxt = jnp.where(i < n_tiles - 1, dqn_ref[...].astype(F32) * bn_ref[...].astype(F32), 0.0)
        dz = w2 * dconv + w1 * _shift_up(dconv, nxt, 1) + w0 * _shift_up(dconv, nxt, 2)
        o_ref[:, :d] = (dq * conv).astype(BF16)
        o_ref[:, d:2 * d] = (dz * v).astype(BF16)
        o_ref[:, 2 * d:] = (dz * c).astype(BF16)
        dw = jnp.concatenate([jnp.sum(dconv * z2, axis=0, keepdims=True), jnp.sum(dconv * z1, axis=0, keepdims=True),
                              jnp.sum(dconv * zc, axis=0, keepdims=True), jnp.zeros((5, d), F32)], axis=0)

        @pl.when(i == 0)
        def _():
            dw_ref[...] = dw

        @pl.when(i > 0)
        def _():
            dw_ref[...] += dw

    blk = lambda col: pl.BlockSpec((tm, d), lambda i: (i, col))
    before = lambda col: pl.BlockSpec((EDGE, d), lambda i: (jnp.maximum(i * per - 1, 0), col))
    after = lambda col: pl.BlockSpec((EDGE, d), lambda i: (jnp.minimum((i + 1) * per, last_edge), col))
    return pl.pallas_call(
        body, name=name, grid=(n_tiles,),
        in_specs=[blk(0), blk(1), blk(2), before(1), before(2), after(0), blk(0), after(0),
                  pl.BlockSpec((3, d), lambda i: (0, 0))],
        out_specs=[pl.BlockSpec((tm, d3), lambda i: (i, 0)), pl.BlockSpec((8, d), lambda i: (0, 0))],
        out_shape=[jax.ShapeDtypeStruct((s, d3), BF16), jax.ShapeDtypeStruct((8, d), F32)],
        compiler_params=_params(dimension_semantics=("arbitrary",)),
    )(bcv, bcv, bcv, bcv, bcv, bcv, dq, dq, cw)


def _attn_probs(qh, kh, scale):
    sc = lax.dot_general(qh, kh, (((1,), (1,)), ((), ())), preferred_element_type=F32) * scale
    ex = jnp.exp(sc - jnp.max(sc, axis=-1, keepdims=True))
    return ex / jnp.sum(ex, axis=-1, keepdims=True)


def attn_fwd(name, q, kv):
    s, d = q.shape
    mlen = kv.shape[0]
    dh = d // XATTN_HEADS
    scale = dh ** -0.5
    tm = _row_tile(s, d * 4, 4 * 2**20)

    def body(q_ref, kv_ref, o_ref):
        for h in range(XATTN_HEADS):
            cols = slice(h * dh, (h + 1) * dh)
            p = _attn_probs(q_ref[:, cols], kv_ref[:, cols], scale)
            o_ref[:, cols] = jnp.dot(p.astype(BF16), kv_ref[:, d + h * dh:d + (h + 1) * dh],
                                     preferred_element_type=F32).astype(BF16)

    return pl.pallas_call(
        body, name=name, grid=(s // tm,),
        in_specs=[pl.BlockSpec((tm, d), lambda i: (i, 0)), pl.BlockSpec((mlen, 2 * d), lambda i: (0, 0))],
        out_specs=pl.BlockSpec((tm, d), lambda i: (i, 0)), out_shape=jax.ShapeDtypeStruct((s, d), BF16),
        compiler_params=_params(dimension_semantics=("arbitrary",)),
    )(q, kv)


def attn_bwd(name, q, kv, do):
    s, d = q.shape
    mlen = kv.shape[0]
    dh = d // XATTN_HEADS
    scale = dh ** -0.5
    tm = _row_tile(s, d * 4, 4 * 2**20)

    def body(q_ref, kv_ref, do_ref, dq_ref, dkv_ref):
        first = pl.program_id(0) == 0
        for h in range(XATTN_HEADS):
            cols = slice(h * dh, (h + 1) * dh)
            vcols = slice(d + h * dh, d + (h + 1) * dh)
            qh, kh, vh, doh = q_ref[:, cols], kv_ref[:, cols], kv_ref[:, vcols], do_ref[:, cols]
            p = _attn_probs(qh, kh, scale)
            dp = lax.dot_general(doh, vh, (((1,), (1,)), ((), ())), preferred_element_type=F32)
            ds = (p * (dp - jnp.sum(dp * p, axis=-1, keepdims=True)) * scale).astype(BF16)
            dq_ref[:, cols] = jnp.dot(ds, kh, preferred_element_type=F32).astype(BF16)
            dk = lax.dot_general(ds, qh, (((0,), (0,)), ((), ())), preferred_element_type=F32)
            dv = lax.dot_general(p.astype(BF16), doh, (((0,), (0,)), ((), ())), preferred_element_type=F32)

            @pl.when(first)
            def _():
                dkv_ref[:, cols] = dk
                dkv_ref[:, vcols] = dv

            @pl.when(jnp.logical_not(first))
            def _():
                dkv_ref[:, cols] += dk
                dkv_ref[:, vcols] += dv

    row = pl.BlockSpec((tm, d), lambda i: (i, 0))
    whole = pl.BlockSpec((mlen, 2 * d), lambda i: (0, 0))
    return pl.pallas_call(
        body, name=name, grid=(s // tm,), in_specs=[row, whole, row], out_specs=[row, whole],
        out_shape=[jax.ShapeDtypeStruct((s, d), BF16), jax.ShapeDtypeStruct((mlen, 2 * d), F32)],
        compiler_params=_params(dimension_semantics=("arbitrary",)),
    )(q, kv, do)


def _as_rows(a):
    if a.ndim >= 2 and a.shape[-1] % LANE == 0:
        return a.reshape(-1, a.shape[-1])
    return a.reshape(-1, LANE) if a.size % LANE == 0 else a.reshape(1, -1)


def add_halves(name, dw, other, core):
    p, r, c = dw.shape
    h = r // 2
    th = _row_tile(h, c * 2, 4 * 2**20)

    def body(core_ref, a_ref, b_ref, o_ref):
        o_ref[...] = (a_ref[...].astype(F32) + b_ref[...].astype(F32)).astype(BF16)

    grid_spec = pltpu.PrefetchScalarGridSpec(
        num_scalar_prefetch=1, grid=(p, h // th),
        in_specs=[pl.BlockSpec((None, None, th, c), lambda pi, i, core_ref: (pi, core_ref[0], i, 0)),
                  pl.BlockSpec((None, th, c), lambda pi, i, core_ref: (pi, i, 0))],
        out_specs=pl.BlockSpec((None, th, c), lambda pi, i, core_ref: (pi, i, 0)))
    return pl.pallas_call(
        body, name=name, grid_spec=grid_spec, out_shape=jax.ShapeDtypeStruct((p, h, c), BF16),
        compiler_params=_params(dimension_semantics=("arbitrary", "arbitrary")),
    )(core, dw.reshape(p, 2, h, c), other)


def sum_leading(name, parts):
    n, r, c = parts.shape
    tr = _row_tile(r, c * 4 * 2, 2 * 2**20)

    def body(p_ref, o_ref):
        acc = p_ref[0].astype(F32)
        for k in range(1, n):
            acc = acc + p_ref[k].astype(F32)
        o_ref[...] = acc

    return pl.pallas_call(
        body, name=name, grid=(r // tr,), in_specs=[pl.BlockSpec((n, tr, c), lambda i: (0, i, 0))],
        out_specs=pl.BlockSpec((tr, c), lambda i: (i, 0)), out_shape=jax.ShapeDtypeStruct((r, c), F32),
        compiler_params=_params(dimension_semantics=("arbitrary",)),
    )(parts)


def _adamw_rows(w, g, m, v):
    m = ADAM_B1 * m + (1.0 - ADAM_B1) * g
    v = ADAM_B2 * v + (1.0 - ADAM_B2) * (g * g)
    m_hat = m / (1.0 - ADAM_B1 ** ADAM_STEP)
    v_hat = v / (1.0 - ADAM_B2 ** ADAM_STEP)
    delta = -ADAM_LR * (m_hat / (jnp.sqrt(v_hat) + ADAM_EPS) + ADAM_WD * w)
    return delta, m, v


def adamw_layer(name, w, m, v, g, layer, carried):
    nl, r, c = w.shape
    tr = _row_tile(r, c * 4, 3 * 2**19)
    n_carried = 4 if carried is not None else 0

    def body(*refs):
        w_ref, m_ref, v_ref, g_ref = refs[:4]
        go_ref, d_ref, mo_ref, vo_ref = refs[4 + n_carried:]
        g = g_ref[...]
        delta, m_new, v_new = _adamw_rows(w_ref[...], g, m_ref[...], v_ref[...])
        go_ref[...] = g
        d_ref[...] = delta
        mo_ref[...] = m_new
        vo_ref[...] = v_new

    stacked = pl.BlockSpec((None, tr, c), lambda i: (layer, i, 0))
    in_specs = [stacked, stacked, stacked, pl.BlockSpec((tr, c), lambda i: (i, 0))]
    in_specs += [pl.BlockSpec(memory_space=pl.ANY)] * n_carried
    shape = jax.ShapeDtypeStruct((nl, r, c), F32)
    return pl.pallas_call(
        body, name=name, grid=(r // tr,), in_specs=in_specs, out_specs=[stacked] * 4, out_shape=[shape] * 4,
        input_output_aliases={4 + k: k for k in range(n_carried)},
        compiler_params=_params(dimension_semantics=("arbitrary",)),
    )(w, m, v, g, *(carried or ()))


def adamw_flat(name, w, m, v, g):
    r, c = w.shape

    def body(w_ref, m_ref, v_ref, g_ref, d_ref, mo_ref, vo_ref):
        delta, m_new, v_new = _adamw_rows(w_ref[...], g_ref[...], m_ref[...], v_ref[...])
        d_ref[...] = delta
        mo_ref[...] = m_new
        vo_ref[...] = v_new

    shape = jax.ShapeDtypeStruct((r, c), F32)
    return pl.pallas_call(body, name=name, out_shape=[shape] * 3, compiler_params=_params())(w, m, v, g)


def cast_place(name, w, layer, place):
    nl, r, c = w.shape
    tr = _row_tile(r, c * 4, 8 * 2**20)

    def body(x_ref, y_ref, c_ref, w_ref, o_ref):
        o_ref[...] = w_ref[...].astype(BF16)

    grid_spec = pltpu.PrefetchScalarGridSpec(
        num_scalar_prefetch=3, grid=(r // tr,),
        in_specs=[pl.BlockSpec((None, tr, c), lambda i, x_ref, y_ref, c_ref: (layer, i, 0))],
        out_specs=pl.BlockSpec((None, tr, c), lambda i, x_ref, y_ref, c_ref: (2 * x_ref[0] + y_ref[0], i, 0)))
    return pl.pallas_call(
        body, name=name, grid_spec=grid_spec, out_shape=jax.ShapeDtypeStruct((N_CHIPS, r, c), BF16),
        compiler_params=_params(dimension_semantics=("arbitrary",)),
    )(*place, w)


def reduce_sum4(name, own, landed, place):
    p, h, c = own.shape
    tr = _row_tile(h, c * 4, 4 * 2**20)

    def body(x_ref, y_ref, c_ref, t_ref, y1_ref, y2_ref, y3_ref, o_ref):
        acc = t_ref[...].astype(F32)
        for part_ref in (y1_ref, y2_ref, y3_ref):
            acc = acc + part_ref[...].astype(F32)
        o_ref[...] = acc

    def panel(fx, fy):
        return pl.BlockSpec((None, tr, c), lambda i, x_ref, y_ref, c_ref: (
            2 * (1 - x_ref[0] if fx else x_ref[0]) + (1 - y_ref[0] if fy else y_ref[0]), i, 0))

    grid_spec = pltpu.PrefetchScalarGridSpec(
        num_scalar_prefetch=3, grid=(h // tr,),
        in_specs=[panel(0, 0), panel(1, 0), panel(0, 1), panel(1, 1)],
        out_specs=pl.BlockSpec((None, tr, c), lambda i, x_ref, y_ref, c_ref: (c_ref[0], i, 0)))
    return pl.pallas_call(
        body, name=name, grid_spec=grid_spec, out_shape=jax.ShapeDtypeStruct((2, h, c), F32),
        compiler_params=_params(dimension_semantics=("arbitrary",)),
    )(*place, own, landed, landed, landed)


def run_exchange(name, exchange):
    n_in, n_out = len(exchange.inputs), len(exchange.out_shapes)

    def body(*refs):
        ins, outs, sems = refs[:n_in], refs[n_in:n_in + n_out], refs[n_in + n_out:]
        exchange.start(ins, outs, sems)
        exchange.finish(ins, outs, sems)

    return pl.pallas_call(
        body, name=name, in_specs=[ANY] * n_in, out_specs=[ANY] * n_out, out_shape=list(exchange.out_shapes),
        scratch_shapes=[pltpu.SemaphoreType.DMA((exchange.n_sems,)), pltpu.SemaphoreType.DMA((exchange.n_sems,))],
        input_output_aliases=dict(exchange.aliases),
    )(*exchange.inputs)


def _row_halves(ref, c):
    h = ref.shape[1] // 2
    return pl.ds(pl.multiple_of(c * h, 16), h), pl.ds(pl.multiple_of((1 - c) * h, 16), h)


def _in_place(arrays, n_sems, start, finish):
    return Exchange(list(arrays), [jax.ShapeDtypeStruct(f.shape, f.dtype) for f in arrays],
                    {a: a for a in range(len(arrays))}, n_sems, start, finish)


def gather_over_ici(fulls):
    n = len(fulls)

    def sends(outs, sems):
        x, y, c, mine, chips = _place()
        return [_remote(rows, rows, sems, 3 * a + j, (*chip, c)) for a in range(n)
                for rows in [outs[a].at[mine, _row_halves(outs[a], c)[0]]] for j, chip in enumerate(chips)]

    def start(ins, outs, sems):
        for cp in sends(outs, sems):
            cp.start()

    def finish(ins, outs, sems):
        x, y, c, mine, chips = _place()
        for a in range(n):
            for j, chip in enumerate(chips):
                rows = outs[a].at[2 * chip[0] + chip[1], _row_halves(outs[a], c)[0]]
                _remote(rows, rows, sems, 3 * a + j, (*chip, c)).wait_recv()
        for cp in sends(outs, sems):
            cp.wait_send()

    return _in_place(fulls, 3 * n, start, finish)


def gather_over_d2d(fulls):
    n = len(fulls)

    def copies(outs, sems, which):
        x, y, c, mine, chips = _place()
        return [_remote(rows, rows, sems, 3 * a + j, (x, y, 1 - c)) for a in range(n) for j, chip in enumerate(chips)
                for rows in [outs[a].at[2 * chip[0] + chip[1], _row_halves(outs[a], c)[which]]]]

    def start(ins, outs, sems):
        for cp in copies(outs, sems, 0):
            cp.start()

    def finish(ins, outs, sems):
        for cp in copies(outs, sems, 1):
            cp.wait_recv()
        for cp in copies(outs, sems, 0):
            cp.wait_send()

    return _in_place(fulls, 3 * n, start, finish)


def gather_whole(fulls):
    ici, d2d = gather_over_ici(fulls), gather_over_d2d(fulls)

    def finish(ins, outs, sems):
        ici.finish(ins, outs, sems)
        later = tuple(_SemaphoresFrom(s, ici.n_sems) for s in sems)
        d2d.start(ins, outs, later)
        d2d.finish(ins, outs, later)

    return _in_place(fulls, ici.n_sems + d2d.n_sems, ici.start, finish)


class _SemaphoresFrom:
    def __init__(self, ref, offset):
        self.ref, self.offset = ref, offset

    @property
    def at(self):
        return self

    def __getitem__(self, k):
        return self.ref.at[self.offset + k]


def combine(exchanges):
    exchanges = [e for e in exchanges if e is not None]
    if len(exchanges) <= 1:
        return exchanges[0] if exchanges else None
    inputs, out_shapes, aliases, spans, n_sems = [], [], {}, [], 0
    for e in exchanges:
        aliases.update({len(inputs) + i: len(out_shapes) + o for i, o in e.aliases.items()})
        spans.append((len(inputs), len(e.inputs), len(out_shapes), len(e.out_shapes), n_sems))
        inputs, out_shapes, n_sems = inputs + list(e.inputs), out_shapes + list(e.out_shapes), n_sems + e.n_sems

    def each(method):
        def run(ins, outs, sems):
            for e, (i0, ni, o0, no, s0) in zip(exchanges, spans, strict=True):
                getattr(e, method)(ins[i0:i0 + ni], outs[o0:o0 + no], tuple(_SemaphoresFrom(s, s0) for s in sems))
        return run

    return Exchange(inputs, out_shapes, aliases, n_sems, each("start"), each("finish"))


def swap_exchange(grads):
    n = len(grads)

    def copies(ins, outs, sems):
        x, y, c, _, _ = _place()
        return [_remote(ins[a].at[:, _row_halves(ins[a], c)[1]], outs[a], sems, a, (x, y, 1 - c)) for a in range(n)]

    def start(ins, outs, sems):
        for cp in copies(ins, outs, sems):
            cp.start()

    def finish(ins, outs, sems):
        for cp in copies(ins, outs, sems):
            cp.wait()

    shapes = [jax.ShapeDtypeStruct((g.shape[0], g.shape[1] // 2, g.shape[2]), g.dtype) for g in grads]
    return Exchange(list(grads), shapes, {}, n, start, finish)


def scatter_exchange(parts):
    n = len(parts)

    def sends(ins, outs, sems):
        x, y, c, mine, chips = _place()
        return [_remote(ins[a].at[2 * chip[0] + chip[1]], outs[a].at[mine], sems, 3 * a + j, (*chip, c))
                for a in range(n) for j, chip in enumerate(chips)]

    def start(ins, outs, sems):
        for cp in sends(ins, outs, sems):
            cp.start()

    def finish(ins, outs, sems):
        x, y, c, mine, chips = _place()
        for a in range(n):
            for j, chip in enumerate(chips):
                landing = outs[a].at[2 * chip[0] + chip[1]]
                _remote(landing, landing, sems, 3 * a + j, (*chip, c)).wait_recv()
        for cp in sends(ins, outs, sems):
            cp.wait_send()

    return Exchange(list(parts), [jax.ShapeDtypeStruct(g.shape, g.dtype) for g in parts], {}, 3 * n, start, finish)


def join_exchange(halves):
    n = len(halves)

    def start(ins, outs, sems):
        x, y, c, _, _ = _place()
        for a in range(n):
            _remote(outs[a].at[c], outs[a].at[c], sems, a, (x, y, 1 - c)).start()

    def finish(ins, outs, sems):
        x, y, c, _, _ = _place()
        for a in range(n):
            _remote(outs[a].at[1 - c], outs[a].at[1 - c], sems, a, (x, y, 1 - c)).wait_recv()
        for a in range(n):
            _remote(outs[a].at[c], outs[a].at[c], sems, a, (x, y, 1 - c)).wait_send()

    return Exchange(list(halves), [jax.ShapeDtypeStruct(g.shape, g.dtype) for g in halves], {a: a for a in range(n)},
                    n, start, finish)


def gather_all(name, rows):
    def body(in_ref, out_ref, send_sems, recv_sems, local_sem):
        sems = (send_sems, recv_sems)
        x, y, c, _, _ = _place()
        me = 4 * x + 2 * y + c
        local = pltpu.make_async_copy(in_ref, out_ref.at[me], local_sem)
        local.start()
        peers = [(1 - x if k & 4 else x, 1 - y if k & 2 else y, 1 - c if k & 1 else c) for k in range(1, N_DEV)]
        sent = []
        for k, peer in enumerate(peers):
            cp = _remote(in_ref, out_ref.at[me], sems, k, peer)
            cp.start()
            sent.append(cp)
        for k, peer in enumerate(peers):
            landing = out_ref.at[4 * peer[0] + 2 * peer[1] + peer[2]]
            _remote(landing, landing, sems, k, peer).wait_recv()
        for cp in sent:
            cp.wait_send()
        local.wait()

    return pl.pallas_call(
        body, name=name, in_specs=[ANY], out_specs=ANY,
        out_shape=jax.ShapeDtypeStruct((N_DEV,) + rows.shape, rows.dtype),
        scratch_shapes=[pltpu.SemaphoreType.DMA((N_DEV - 1,)), pltpu.SemaphoreType.DMA((N_DEV - 1,)),
                        pltpu.SemaphoreType.DMA],
    )(rows)


class _Step:
    def __init__(self, p):
        self.p = p
        xi, yi, ci = lax.axis_index("x"), lax.axis_index("y"), lax.axis_index("c")
        self.chip = 2 * xi + yi
        self.place_refs = tuple(v.astype(jnp.int32).reshape(1) for v in (xi, yi, ci))
        self.core_ref = self.place_refs[2]
        self.depth = p['ffn1_norm'].shape[0]
        self.placed, self.landed, self.w = {}, {}, {}
        self.big_g = {}
        self.waiting_joins = []
        self.waiting_scatter = None

    def block_keys(self, tag, l):
        if l >= self.depth:
            return []
        mixer = ['gmlp_w_in', 'gmlp_w_out'] if l % 2 == 0 else ['conv_w_in', 'conv_w_out']
        names = {"ffn1": ['ffn1_w13', 'ffn1_w2'], "mix": mixer, "xattn": ['xattn_wq', 'xattn_wkv', 'xattn_wo'],
                 "ffn2": ['ffn2_w13', 'ffn2_w2']}[tag]
        return [(n, l // 2 if tag == "mix" else l) for n in names]

    def place(self, keys):
        for n, idx in keys:
            self.placed[(n, idx)] = cast_place(f"place_{n}{idx}", self.p[n], idx, self.place_refs)

    def carrying_gather(self, call, over_ici, over_d2d, whole, *args, **kw):
        over_ici = [k for k in over_ici if k in self.placed]
        over_d2d = [k for k in over_d2d if k in self.landed]
        parts = [gather_over_ici([self.placed.pop(k) for k in over_ici]) if over_ici else None,
                 gather_over_d2d([self.landed.pop(k) for k in over_d2d]) if over_d2d else None,
                 gather_whole([self.placed.pop(k) for k in whole]) if whole else None]
        exchange = combine(parts)
        if exchange is None:
            return call(*args, **kw)
        out, got = call(*args, exchange=exchange, **kw)
        self.landed.update(zip(over_ici, got[:len(over_ici)], strict=True))
        self.w.update(zip(over_d2d + whole, got[len(over_ici):], strict=True))
        return out

    def reduce_begin(self, tag, keys, dws, theirs=None):
        theirs = list(theirs or [None] * len(dws))
        todo = [i for i, t in enumerate(theirs) if t is None]
        for i, t in zip(todo, run_exchange(tag + "_swap", swap_exchange([dws[i] for i in todo])), strict=True):
            theirs[i] = t
        parts = [add_halves(f"{tag}_add{i}", dw, t, self.core_ref) for i, (dw, t) in enumerate(zip(dws, theirs, strict=True))]
        assert self.waiting_scatter is None
        self.waiting_scatter = (tag, keys, parts)

    def carrying_scatter(self, mm, *args, **kw):
        tag, keys, parts = self.waiting_scatter
        self.waiting_scatter = None
        out, landed = mm(*args, exchange=scatter_exchange(parts), **kw)
        halves = [reduce_sum4(f"{tag}_sum{i}", t, y, self.place_refs) for i, (t, y) in enumerate(zip(parts, landed, strict=True))]
        self.waiting_joins += list(zip(keys, halves, strict=True))
        return out

    def take_joined(self, keys, joined):
        for k, g in zip(keys, joined, strict=True):
            self.big_g[k] = g.reshape(-1, g.shape[-1])

    def carrying_joins(self, mm, *args, **kw):
        if not self.waiting_joins:
            return mm(*args, **kw)
        keys, halves = zip(*self.waiting_joins, strict=True)
        self.waiting_joins = []
        out, joined = mm(*args, exchange=join_exchange(list(halves)), **kw)
        self.take_joined(keys, joined)
        return out

    def joins_alone(self, name):
        keys, halves = zip(*self.waiting_joins, strict=True)
        self.waiting_joins = []
        self.take_joined(keys, run_exchange(name, join_exchange(list(halves))))

    def ffn_fwd(self, tag, l, x, gain, carry_norm, carry13, carry2, h=None):
        name = f"l{l}_{tag}"
        if h is None:
            h = self.carrying_gather(rms_fwd, *carry_norm, name + "_norm", x, gain)
        by_gate, by_up, act = self.carrying_gather(mm_swiglu, *carry13, name + "_w13", h, self.w[(tag + '_w13', l)])
        out = self.carrying_gather(mm_nn, *carry2, name + "_w2", act, self.w[(tag + '_w2', l)], 'row', F32, res=x,
                                   scale=0.5)
        return out, (x, h, by_gate, by_up, act)

    def ffn_bwd(self, tag, l, dx, dxb, saved, gain):
        w13, w2 = self.w[(tag + '_w13', l)], self.w[(tag + '_w2', l)]
        name = f"l{l}_{tag}"
        x, h, by_gate, by_up, act = saved
        d_gate, d_up = self.carrying_joins(mm_dswiglu, name + "_dact", dxb, w2, by_gate, by_up, 0.5)
        d_w2 = mm_tn(name + "_dw2", act, dxb, 'row', scale=0.5)
        half = N_CHIPS // 2
        d_w13, their_w2 = mm_tn(name + "_dw13g", h, d_gate, 'col', panels=(0, half), exchange=swap_exchange([d_w2]))
        d_w13 = mm_tn(name + "_dw13u", h, d_up, 'col', panels=(half, half), into=d_w13)
        self.reduce_begin(name, [(tag + '_w13', l), (tag + '_w2', l)], [d_w13, d_w2], [None] + their_w2)
        dh = self.carrying_scatter(mm_nt, name + "_dh", d_gate, w13, 'col', BF16, a_hi=d_up)
        return rms_bwd(name + "_dnorm", x, gain, dh, dx)


def kernel(x, mem, ffn1_norm, ffn1_w13, ffn1_w2, mix_norm, gmlp_w_in, gmlp_ln_g, gmlp_ln_b, gmlp_w_s, gmlp_b_s, gmlp_w_out, conv_w_in, conv_w, conv_w_out, xattn_norm, mem_norm, xattn_wq, xattn_wkv, xattn_wo, ffn2_norm, ffn2_w13, ffn2_w2, final_norm, loss_target, m_ffn1_norm, m_ffn1_w13, m_ffn1_w2, m_mix_norm, m_gmlp_w_in, m_gmlp_ln_g, m_gmlp_ln_b, m_gmlp_w_s, m_gmlp_b_s, m_gmlp_w_out, m_conv_w_in, m_conv_w, m_conv_w_out, m_xattn_norm, m_mem_norm, m_xattn_wq, m_xattn_wkv, m_xattn_wo, m_ffn2_norm, m_ffn2_w13, m_ffn2_w2, m_final_norm, v_ffn1_norm, v_ffn1_w13, v_ffn1_w2, v_mix_norm, v_gmlp_w_in, v_gmlp_ln_g, v_gmlp_ln_b, v_gmlp_w_s, v_gmlp_b_s, v_gmlp_w_out, v_conv_w_in, v_conv_w, v_conv_w_out, v_xattn_norm, v_mem_norm, v_xattn_wq, v_xattn_wkv, v_xattn_wo, v_ffn2_norm, v_ffn2_w13, v_ffn2_w2, v_final_norm):
    return _step(dict(locals()))


def _step(p):
    assert sorted(p) == sorted(ARG_NAMES)
    st = _Step(p)
    x = p['x'][0]
    mem = p['mem'][0]
    target = p['loss_target'][0]
    s, d = x.shape
    depth = st.depth

    for l in range(depth):
        for tag in ("ffn1", "mix", "xattn", "ffn2"):
            st.place(st.block_keys(tag, l))
    first = ('ffn1_w13', 0)
    st.landed[first] = run_exchange("gather_first", gather_over_ici([st.placed.pop(first)]))[0]

    cw_local = p['conv_w']
    n_conv, cwid, dq4 = cw_local.shape
    cw_rows = jnp.pad(cw_local.reshape(-1, LANE), ((0, (-cw_local.size // LANE) % 8), (0, 0)))
    cw_all = gather_all("gather_conv_w", cw_rows)[0::2, :cw_local.size // LANE]
    conv_w_full = cw_all.reshape(N_CHIPS, n_conv, cwid, dq4).transpose(1, 2, 0, 3).reshape(n_conv, cwid, N_CHIPS * dq4)

    saved = []
    for l in range(depth):
        j = l // 2
        rec = {}
        soon = st.block_keys("mix", l) + st.block_keys("xattn", l)
        x, rec['ffn1'] = st.ffn_fwd("ffn1", l, x, p['ffn1_norm'][l],
                                    ([], [('ffn1_w13', 0)] if l == 0 else [], []),
                                    (soon, [], [('ffn1_w2', 0)] if l == 0 else []),
                                    ([('ffn2_w13', l)], soon, []))
        h = rms_fwd(f"l{l}_mix_norm", x, p['mix_norm'][l])
        if l % 2 == 0:
            e = p['gmlp_ln_g'].shape[-1]
            bias = jnp.repeat(p['gmlp_b_s'][j].T, e // GMLP_GROUPS, axis=1)
            z = st.carrying_gather(mm_nn, [('ffn2_w2', l)], [('ffn2_w13', l)], [], f"l{l}_gmlp_in", h,
                                   st.w[('gmlp_w_in', j)], 'col', BF16)
            gate = gmlp_fwd(f"l{l}_gmlp_gate", z, p['gmlp_ln_g'][j], p['gmlp_ln_b'][j], p['gmlp_w_s'][j], bias)
            x_new, hq = st.carrying_gather(mm_nn, [], [('ffn2_w2', l)], [], f"l{l}_gmlp_out", gate,
                                           st.w[('gmlp_w_out', j)], 'row', F32, res=x, norm_gain=p['xattn_norm'][l])
            rec['mix'] = (x, h, z, gate, bias)
        else:
            bcv = st.carrying_gather(mm_nn, [('ffn2_w2', l)], [('ffn2_w13', l)], [], f"l{l}_conv_in", h,
                                     st.w[('conv_w_in', j)], 'col', BF16)
            gate = conv_fwd(f"l{l}_conv_gate", bcv, conv_w_full[j])
            x_new, hq = st.carrying_gather(mm_nn, [], [('ffn2_w2', l)], [], f"l{l}_conv_out", gate,
                                           st.w[('conv_w_out', j)], 'row', F32, res=x, norm_gain=p['xattn_norm'][l])
            rec['mix'] = (x, h, bcv, gate)
        x = x_new
        q = mm_nn(f"l{l}_xattn_q", hq, st.w[('xattn_wq', l)], 'row', BF16)
        mem_n = rms_fwd(f"l{l}_mem_norm", mem, p['mem_norm'][l])
        kv = mm_nn(f"l{l}_xattn_kv", mem_n, st.w[('xattn_wkv', l)], 'col', BF16)
        o = attn_fwd(f"l{l}_xattn_core", q, kv)
        x_new, h_ffn2 = mm_nn(f"l{l}_xattn_o", o, st.w[('xattn_wo', l)], 'row', F32, res=x,
                              norm_gain=p['ffn2_norm'][l])
        rec['xattn'] = (x, hq, q, mem_n, kv, o)
        x = x_new
        ahead = st.block_keys("ffn1", l + 1)
        x, rec['ffn2'] = st.ffn_fwd("ffn2", l, x, p['ffn2_norm'][l], None, (ahead, [], []), ([], ahead, []), h=h_ffn2)
        saved.append(rec)

    dx, dxb, d_final, loss_lanes = loss_head("loss_head", x, p['final_norm'], target)
    loss = lax.psum(0.5 * jnp.sum(loss_lanes) / d, ("x", "y", "c"))

    small = {n: [None] * p[n].shape[0] for n in ('ffn1_norm', 'mix_norm', 'xattn_norm', 'mem_norm', 'ffn2_norm',
                                                  'gmlp_ln_g', 'gmlp_ln_b', 'gmlp_w_s', 'gmlp_b_s', 'conv_w')}
    for l in reversed(range(depth)):
        j = l // 2
        rec = saved[l]
        dx, dxb, small['ffn2_norm'][l] = st.ffn_bwd("ffn2", l, dx, dxb, rec['ffn2'], p['ffn2_norm'][l])

        x_in, hq, q, mem_n, kv, o = rec['xattn']
        name = f"l{l}_xattn"
        do = st.carrying_joins(mm_nt, name + "_do", dxb, st.w[('xattn_wo', l)], 'row', BF16)
        d_wo = mm_tn(name + "_dwo", o, dxb, 'row')
        dq, dkv = attn_bwd(name + "_dcore", q, kv, do)
        d_wq = mm_tn(name + "_dwq", hq, dq, 'row')
        dkvb = dkv.astype(BF16)
        d_wkv = mm_tn(name + "_dwkv", mem_n, dkvb, 'col')
        st.reduce_begin(name, [('xattn_wq', l), ('xattn_wkv', l), ('xattn_wo', l)], [d_wq, d_wkv, d_wo])
        dh = mm_nt(name + "_dh", dq, st.w[('xattn_wq', l)], 'row', BF16)
        dx, dxb, small['xattn_norm'][l] = rms_bwd(name + "_dnorm", x_in, p['xattn_norm'][l], dh, dx)
        dmem_n = mm_nt(name + "_dmem", dkvb, st.w[('xattn_wkv', l)], 'col', F32)
        small['mem_norm'][l] = rms_bwd(f"l{l}_mem_dnorm", mem, p['mem_norm'][l], dmem_n, None)[2]

        if l % 2 == 0:
            x_in, h, z, gate, bias = rec['mix']
            name = f"l{l}_gmlp"
            w_in, w_out = st.w[('gmlp_w_in', j)], st.w[('gmlp_w_out', j)]
            dgate = mm_nt(name + "_dgate", dxb, w_out, 'row', BF16)
            d_wout = mm_tn(name + "_dwout", gate, dxb, 'row')
            dmix, dws, dbs, dlg, dlb = gmlp_bwd(name + "_dgate_core", z, dgate, p['gmlp_ln_g'][j], p['gmlp_ln_b'][j],
                                                p['gmlp_w_s'][j], bias)
            small['gmlp_w_s'][j], small['gmlp_b_s'][j] = dws, dbs[:, :GMLP_GROUPS].T
            small['gmlp_ln_g'][j], small['gmlp_ln_b'][j] = dlg, dlb
            keys = [('gmlp_w_in', j), ('gmlp_w_out', j)]
        else:
            x_in, h, bcv, gate = rec['mix']
            name = f"l{l}_conv"
            w_in, w_out = st.w[('conv_w_in', j)], st.w[('conv_w_out', j)]
            dgate = mm_nt(name + "_dgate", dxb, w_out, 'row', BF16)
            d_wout = mm_tn(name + "_dwout", gate, dxb, 'row')
            dmix, dcw = conv_bwd(name + "_dgate_core", bcv, dgate, conv_w_full[j])
            small['conv_w'][j] = dcw[:cwid]
            keys = [('conv_w_in', j), ('conv_w_out', j)]
        d_win = st.carrying_scatter(mm_tn, name + "_dwin", h, dmix, 'col')
        st.reduce_begin(name, keys, [d_win, d_wout])
        dh = st.carrying_scatter(mm_nt, name + "_dh", dmix, w_in, 'col', BF16)
        dx, dxb, small['mix_norm'][l] = rms_bwd(f"l{l}_mix_dnorm", x_in, p['mix_norm'][l], dh, dx)

        dx, dxb, small['ffn1_norm'][l] = st.ffn_bwd("ffn1", l, dx, dxb, rec['ffn1'], p['ffn1_norm'][l])
    st.joins_alone("join_last")

    small_names = ['ffn1_norm', 'mix_norm', 'xattn_norm', 'mem_norm', 'ffn2_norm', 'gmlp_ln_g', 'gmlp_ln_b', 'gmlp_w_s',
                   'gmlp_b_s', 'final_norm', 'conv_w']
    small_full = {n: jnp.stack([g.reshape(p[n].shape[1:]) for g in small[n]]) for n in small_names
                  if n not in ('final_norm', 'conv_w')}
    small_full['final_norm'] = d_final.reshape(p['final_norm'].shape)
    small_full['conv_w'] = jnp.stack(small['conv_w'])
    packed = jnp.concatenate([small_full[n].reshape(-1, LANE) for n in small_names], axis=0)
    total = sum_leading("small_sum", gather_all("small_gather", packed))
    small_g, at = {}, 0
    for n in small_names:
        rows = small_full[n].size // LANE
        small_g[n] = total[at:at + rows].reshape(small_full[n].shape)
        at += rows
    small_g['conv_w'] = lax.dynamic_slice_in_dim(small_g['conv_w'], st.chip * dq4, dq4, axis=2)

    grads, deltas, new_m, new_v = {}, {}, {}, {}
    for n in WEIGHTS:
        w, m, v = p[n], p['m_' + n], p['v_' + n]
        if n in BIG:
            carried = None
            for i in range(w.shape[0]):
                carried = adamw_layer(f"adamw_{n}{i}", w, m, v, st.big_g[(n, i)], i, carried)
            grads[n], deltas[n], new_m[n], new_v[n] = carried
        else:
            g = small_g[n]
            out = adamw_flat(f"adamw_{n}", _as_rows(w), _as_rows(m), _as_rows(v), _as_rows(g))
            grads[n] = g
            deltas[n], new_m[n], new_v[n] = (o.reshape(w.shape) for o in out)

    grad_x = dx.reshape(p['x'].shape)
    return (loss, grad_x, *[grads[n] for n in WEIGHTS], *[deltas[n] for n in WEIGHTS], *[new_m[n] for n in WEIGHTS],
            *[new_v[n] for n in WEIGHTS])
```

```python
from typing import Callable, NamedTuple

import jax
import jax.numpy as jnp
from jax import lax
from jax.experimental import pallas as pl
from jax.experimental.pallas import tpu as pltpu

F32 = jnp.float32
BF16 = jnp.bfloat16
MESH = pl.DeviceIdType.MESH

CHUNK = 128
GMLP_GROUPS = 8
XATTN_HEADS = 4
RMS_EPS = 1e-6
LN_EPS = 1e-5
ADAM_LR = 0.001
ADAM_B1 = 0.9
ADAM_B2 = 0.999
ADAM_EPS = 1e-08
ADAM_WD = 0.01
ADAM_STEP = 10

N_CHIPS = 4
N_DEV = 8

VMEM_LIMIT_BYTES = 58 * 2**20
VMEM_PLAN_BYTES = 48 * 2**20
LANE = 128
MXU_DIM = 256
ACC_CHUNK = 2 * MXU_DIM
MXU_FLOPS_PER_US = 996e6
HBM_BYTES_PER_US = 3.3e6
STEP_US = 0.35
ACC_US_PER_VREG = 0.58e-3

WEIGHTS = ['ffn1_norm', 'ffn1_w13', 'ffn1_w2', 'mix_norm', 'gmlp_w_in', 'gmlp_ln_g', 'gmlp_ln_b', 'gmlp_w_s',
           'gmlp_b_s', 'gmlp_w_out', 'conv_w_in', 'conv_w', 'conv_w_out', 'xattn_norm', 'mem_norm', 'xattn_wq',
           'xattn_wkv', 'xattn_wo', 'ffn2_norm', 'ffn2_w13', 'ffn2_w2', 'final_norm']
BIG = {'ffn1_w13': 'col', 'ffn1_w2': 'row', 'gmlp_w_in': 'col', 'gmlp_w_out': 'row', 'conv_w_in': 'col',
       'conv_w_out': 'row', 'xattn_wq': 'row', 'xattn_wkv': 'col', 'xattn_wo': 'row', 'ffn2_w13': 'col',
       'ffn2_w2': 'row'}
ARG_NAMES = (['x', 'mem'] + WEIGHTS + ['loss_target'] + ['m_' + n for n in WEIGHTS] + ['v_' + n for n in WEIGHTS])


def _params(**kw):
    return pltpu.CompilerParams(vmem_limit_bytes=VMEM_LIMIT_BYTES, **kw)


def _divisors(n, mult, cap):
    return [d for d in range(mult, min(n, cap) + 1, mult) if n % d == 0] or [n]


def _row_tile(rows, width_bytes, budget=4 * 2**20):
    best = None
    for d in _divisors(rows, 16, 1024):
        if d * width_bytes <= budget:
            best = d
    return best or _divisors(rows, 16, 1024)[0]


ANY = pl.BlockSpec(memory_space=pl.ANY)


class Exchange(NamedTuple):
    inputs: list
    out_shapes: list
    aliases: dict
    n_sems: int
    start: Callable
    finish: Callable


def _place():
    x, y, c = lax.axis_index("x"), lax.axis_index("y"), lax.axis_index("c")
    chips = [(1 - x, y), (x, 1 - y), (1 - x, 1 - y)]
    return x, y, c, 2 * x + y, chips


def _remote(src, dst, sems, k, device):
    return pltpu.make_async_remote_copy(src_ref=src, dst_ref=dst, send_sem=sems[0].at[k], recv_sem=sems[1].at[k],
                                        device_id=device, device_id_type=MESH)


def _mxu_fill(dim):
    return dim / (-(-dim // MXU_DIM) * MXU_DIM)


def _tile_time(flops, fill, traffic, steps, acc_vregs):
    return (max(flops / (MXU_FLOPS_PER_US * fill), traffic / HBM_BYTES_PER_US) + steps * STEP_US
            + steps * acc_vregs * ACC_US_PER_VREG)


def _plan_mm(m, n_tiles_of, k_tiles_of, n, k, a_item, o_item, has_res, a_arrays=1):
    best, best_cost = None, None
    for tm in _divisors(m, 16, 1024):
        for tn in n_tiles_of:
            for tk in k_tiles_of:
                ni, nj, nk = m // tm, n // tn, k // tk
                blocks = a_arrays * tm * tk * a_item + tk * tn * 2 + tm * tn * o_item + (tm * tn * 4 if has_res else 0)
                vmem = 2 * blocks + tm * tn * 4 * (2 if nk > 1 else 1)
                if vmem > VMEM_PLAN_BYTES:
                    continue
                traffic = nj * m * k * a_item + (k * n * 2 if nk == 1 else ni * k * n * 2)
                traffic += m * n * (o_item + (4 if has_res else 0))
                cost = _tile_time(2 * m * n * k, _mxu_fill(tk) * _mxu_fill(tn), traffic, ni * nj * nk,
                                  tm * tn // 1024 if nk > 1 else 0)
                if best_cost is None or cost < best_cost:
                    best, best_cost = (tm, tn, tk), cost
    assert best is not None, (m, n, k)
    return best


def _tiled_call(name, grid, operands, in_specs, out_shapes, out_specs, scratch, compute, exchange=None, aliases=None):
    n_reg, n_out, n_scr = len(operands), len(out_shapes), len(scratch)
    n_xin = len(exchange.inputs) if exchange else 0
    n_xout = len(exchange.out_shapes) if exchange else 0
    semantics = ("arbitrary",) * len(grid)

    def body(*refs):
        ins = refs[:n_reg]
        outs = refs[n_reg + n_xin:n_reg + n_xin + n_out]
        scr = refs[n_reg + n_xin + n_out + n_xout:n_reg + n_xin + n_out + n_xout + n_scr]
        if not exchange:
            compute(ins, outs, scr)
            return
        x_ins = refs[n_reg:n_reg + n_xin]
        x_outs = refs[n_reg + n_xin + n_out:n_reg + n_xin + n_out + n_xout]
        sems = refs[-2:]
        at_first, at_last = True, True
        for k, extent in enumerate(grid):
            at_first = jnp.logical_and(at_first, pl.program_id(k) == 0)
            at_last = jnp.logical_and(at_last, pl.program_id(k) == extent - 1)

        @pl.when(at_first)
        def _():
            exchange.start(x_ins, x_outs, sems)

        compute(ins, outs, scr)

        @pl.when(at_last)
        def _():
            exchange.finish(x_ins, x_outs, sems)

    if not exchange:
        return pl.pallas_call(
            body, name=name, grid=grid, in_specs=in_specs, out_specs=out_specs, out_shape=out_shapes,
            scratch_shapes=scratch, input_output_aliases=dict(aliases or {}),
            compiler_params=_params(dimension_semantics=semantics),
        )(*operands)
    assert not aliases
    sems = [pltpu.SemaphoreType.DMA((exchange.n_sems,)), pltpu.SemaphoreType.DMA((exchange.n_sems,))]
    got = pl.pallas_call(
        body, name=name, grid=grid, in_specs=in_specs + [ANY] * n_xin, out_specs=out_specs + [ANY] * n_xout,
        out_shape=out_shapes + list(exchange.out_shapes), scratch_shapes=scratch + sems,
        input_output_aliases={n_reg + i: n_out + o for i, o in exchange.aliases.items()},
        compiler_params=_params(dimension_semantics=semantics),
    )(*operands, *exchange.inputs)
    return list(got[:n_out]), list(got[n_out:])


class Split(NamedTuple):
    slot: int
    other: jax.Array
    spec: pl.BlockSpec
    use_other: Callable


def _mm_call(name, grid, operands, in_specs, out_shape, out_spec, contract, nk, scale, has_res, tile, exchange=None,
             split=None, into=None, normed=False):
    n_main = len(operands)
    out_shapes, out_specs = [out_shape], [out_spec]
    if normed:
        out_shapes, out_specs = out_shapes + [jax.ShapeDtypeStruct(out_shape.shape, BF16)], out_specs + [out_spec]

    def compute(ins, outs, scr):
        res_ref = ins[2] if has_res else None
        o_ref = outs[0]
        acc_ref = scr[0] if nk > 1 else None

        def finish(v):
            if scale != 1.0:
                v = v * scale
            if has_res:
                v = res_ref[...] + v
            o_ref[...] = v.astype(o_ref.dtype)
            if normed:
                outs[1][...] = _rms_rows(v, ins[2 + has_res][...])[2].astype(BF16)

        def b_block(cols):
            ref = b_ref_of[0]
            stacked = len(ref.shape) == 3
            if contract[0][1] == (1,):
                if not stacked:
                    return ref[cols, :]
                return ref[...].reshape(ref.shape[0] * ref.shape[1], ref.shape[2])[cols, :]
            if not stacked:
                return ref[:, cols]
            b = ref[:, :, cols]
            return b.reshape(b.shape[0] * b.shape[1], b.shape[2])

        b_ref_of = [None]

        def contribute(a_ref, b_ref):
            b_ref_of[0] = b_ref
            if nk == 1:
                finish(lax.dot_general(a_ref[...], b_block(slice(None)), contract, preferred_element_type=F32))
                return
            kk = pl.program_id(2)

            @pl.when(kk == 0)
            def _():
                acc_ref[...] = jnp.zeros(tile, F32)

            a = a_ref[...]
            for start in range(0, tile[1], ACC_CHUNK):
                cols = slice(start, min(start + ACC_CHUNK, tile[1]))
                acc_ref[:, cols] += lax.dot_general(a, b_block(cols), contract, preferred_element_type=F32)

            @pl.when(kk == nk - 1)
            def _():
                finish(acc_ref[...])

        if split is None:
            contribute(ins[0], ins[1])
            return
        use_other = split.use_other(pl.program_id(0), pl.program_id(1), pl.program_id(2))
        pair = [ins[0], ins[1]]
        other = list(pair)
        other[split.slot] = ins[n_main]

        @pl.when(jnp.logical_not(use_other))
        def _():
            contribute(*pair)

        @pl.when(use_other)
        def _():
            contribute(*other)

    aliases = None
    if split is not None:
        operands, in_specs = operands + [split.other], in_specs + [split.spec]
    if into is not None:
        aliases = {len(operands): 0}
        operands, in_specs = operands + [into], in_specs + [ANY]
    got = _tiled_call(name, grid, operands, in_specs, out_shapes, out_specs,
                      [pltpu.VMEM(tile, F32)] if nk > 1 else [], compute, exchange, aliases)
    results, carried = (got[0], got[1]) if exchange else (got, None)
    out = tuple(results) if normed else results[0]
    return (out, carried) if exchange else out


def mm_nn(name, a, w, kind, out_dtype, res=None, scale=1.0, exchange=None, norm_gain=None):
    m, k = a.shape
    p, r, c = w.shape
    n = p * c if kind == 'col' else c
    assert k == (r if kind == 'col' else p * r), (name, a.shape, w.shape)
    n_tiles = _divisors(c, LANE, 2816)
    if norm_gain is not None:
        assert kind == 'row'
        n_tiles = [n]
    k_tiles = _divisors(r, LANE, 4096)
    if kind == 'row':
        k_tiles = k_tiles + [q * r for q in (2, 4) if p % q == 0]
    o_item = jnp.dtype(out_dtype).itemsize + (2 if norm_gain is not None else 0)
    tm, tn, tk = _plan_mm(m, n_tiles, k_tiles, n, k, a.dtype.itemsize, o_item, res is not None)
    nk = k // tk
    if kind == 'col':
        cpt = c // tn
        w_spec = pl.BlockSpec((None, tk, tn), lambda j, i, kk: (j // cpt, kk, j % cpt))
    elif tk > r:
        w_spec = pl.BlockSpec((tk // r, r, tn), lambda j, i, kk: (kk, 0, j))
    else:
        rpt = r // tk
        w_spec = pl.BlockSpec((None, tk, tn), lambda j, i, kk: (kk // rpt, kk % rpt, j))
    in_specs = [pl.BlockSpec((tm, tk), lambda j, i, kk: (i, kk)), w_spec]
    operands = [a, w]
    if res is not None:
        in_specs.append(pl.BlockSpec((tm, tn), lambda j, i, kk: (i, j)))
        operands.append(res)
    if norm_gain is not None:
        in_specs.append(pl.BlockSpec((1, tn), lambda j, i, kk: (0, 0)))
        operands.append(norm_gain.reshape(1, n))
    return _mm_call(name, (n // tn, m // tm, nk), operands, in_specs, jax.ShapeDtypeStruct((m, n), out_dtype),
                    pl.BlockSpec((tm, tn), lambda j, i, kk: (i, j)), (((1,), (0,)), ((), ())), nk, scale,
                    res is not None, (tm, tn), exchange, normed=norm_gain is not None)


def mm_nt(name, a, w, kind, out_dtype, scale=1.0, exchange=None, a_hi=None):
    m, kc = a.shape
    if a_hi is not None:
        assert a_hi.shape == a.shape
        kc = 2 * kc
    p, r, c = w.shape
    n = r if kind == 'col' else p * r
    assert kc == (p * c if kind == 'col' else c), (name, a.shape, w.shape)
    n_tiles = _divisors(r, LANE, 2816)
    k_tiles = _divisors(c, LANE, 4096)
    if kind == 'row':
        n_tiles = n_tiles + [q * r for q in (2, 4) if p % q == 0 and q * r <= 2816]
    tm, tn, tk = _plan_mm(m, n_tiles, k_tiles, n, kc, a.dtype.itemsize, jnp.dtype(out_dtype).itemsize, False,
                          1 if a_hi is None else 2)
    nk = kc // tk
    if kind == 'col':
        cpt = c // tk
        w_spec = pl.BlockSpec((None, tn, tk), lambda j, i, kk: (kk // cpt, j, kk % cpt))
    elif tn > r:
        w_spec = pl.BlockSpec((tn // r, r, tk), lambda j, i, kk: (j, 0, kk))
    else:
        rpt = r // tn
        w_spec = pl.BlockSpec((None, tn, tk), lambda j, i, kk: (j // rpt, j % rpt, kk))
    split = None
    a_spec = pl.BlockSpec((tm, tk), lambda j, i, kk: (i, kk))
    if a_hi is not None:
        half = nk // 2
        assert nk % 2 == 0
        a_spec = pl.BlockSpec((tm, tk), lambda j, i, kk: (i, jnp.minimum(kk, half - 1)))
        split = Split(0, a_hi, pl.BlockSpec((tm, tk), lambda j, i, kk: (i, jnp.maximum(kk - half, 0))),
                      lambda j, i, kk: kk >= half)
    return _mm_call(name, (n // tn, m // tm, nk), [a, w], [a_spec, w_spec], jax.ShapeDtypeStruct((m, n), out_dtype),
                    pl.BlockSpec((tm, tn), lambda j, i, kk: (i, j)), (((1,), (1,)), ((), ())), nk, scale, False,
                    (tm, tn), exchange, split)


def _plan_tn(s, ka, nd, r_tiles, n_tiles):
    best, best_cost = None, None
    for ts in _divisors(s, 16, 2048):
        for tr in r_tiles:
            for tn in n_tiles:
                ni, nj, ns = ka // tr, nd // tn, s // ts
                vmem = 2 * (ts * tr * 2 + ts * tn * 2 + tr * tn * 2) + tr * tn * 4 * (2 if ns > 1 else 1)
                if vmem > VMEM_PLAN_BYTES:
                    continue
                traffic = nj * s * ka * 2 + ni * s * nd * 2 + ka * nd * 2
                cost = _tile_time(2 * s * ka * nd, _mxu_fill(ts) * _mxu_fill(tn), traffic, ni * nj * ns,
                                  tr * tn // 1024 if ns > 1 else 0)
                if best_cost is None or cost < best_cost:
                    best, best_cost = (ts, tr, tn), cost
    assert best is not None, (s, ka, nd)
    return best


def mm_tn(name, a, dy, kind, scale=1.0, exchange=None, panels=(0, N_CHIPS), into=None):
    s, ka = a.shape
    s2, nd = dy.shape
    assert s == s2
    p = N_CHIPS
    first_panel, n_panels = panels
    assert kind == 'col' or panels == (0, p)
    r, c = (ka, nd // n_panels) if kind == 'col' else (ka // p, nd)
    ts, tr, tn = _plan_tn(s, ka, nd, _divisors(r, LANE, 2048), _divisors(c, LANE, 2816))
    ns = s // ts
    if kind == 'col':
        cpt = c // tn
        o_spec = pl.BlockSpec((None, tr, tn), lambda j, i, kk: (first_panel + j // cpt, i, j % cpt))
    else:
        rpt = r // tr
        o_spec = pl.BlockSpec((None, tr, tn), lambda j, i, kk: (i // rpt, i % rpt, j))
    in_specs = [pl.BlockSpec((ts, tr), lambda j, i, kk: (kk, i)), pl.BlockSpec((ts, tn), lambda j, i, kk: (kk, j))]
    return _mm_call(name, (nd // tn, ka // tr, ns), [a, dy], in_specs, jax.ShapeDtypeStruct((p, r, c), BF16), o_spec,
                    (((0,), (0,)), ((), ())), ns, scale, False, (tr, tn), exchange, None, into)


def _plan_fused(m, k, f, tiles, n_w, n_io):
    best, best_cost = None, None
    for tm in _divisors(m, 16, 1024):
        for tn in tiles:
            vmem = 2 * (tm * k * 2 + n_io * tm * tn * 2) + n_w * k * tn * 2 + 4 * tm * tn * 4
            if vmem > VMEM_PLAN_BYTES:
                continue
            traffic = (f // tn) * m * k * 2 + n_w * k * f * 2 + n_io * m * f * 2
            cost = _tile_time(2 * m * k * f * n_w, _mxu_fill(tn), traffic, (f // tn) * (m // tm), 0)
            if best_cost is None or cost < best_cost:
                best, best_cost = (tm, tn), cost
    assert best is not None, (m, k, f)
    return best


def mm_swiglu(name, h, w13, exchange=None):
    m, k = h.shape
    p, r, c = w13.shape
    assert r == k and p % 2 == 0
    f = p * c // 2
    tm, tn = _plan_fused(m, k, f, _divisors(c, LANE, 2816), 2, 3)
    cpt = c // tn

    def compute(ins, outs, scr):
        a = ins[0][...]
        g = jnp.dot(a, ins[1][...], preferred_element_type=F32)
        u = jnp.dot(a, ins[2][...], preferred_element_type=F32)
        sg = _sigmoid(g)
        silu = g * sg
        outs[0][...] = (u * (sg * (1.0 + g * (1.0 - sg)))).astype(BF16)
        outs[1][...] = silu.astype(BF16)
        outs[2][...] = (silu * u).astype(BF16)

    tile = pl.BlockSpec((tm, tn), lambda j, i: (i, j))
    in_specs = [pl.BlockSpec((tm, k), lambda j, i: (i, 0)),
                pl.BlockSpec((None, k, tn), lambda j, i: (j // cpt, 0, j % cpt), pipeline_mode=pl.Buffered(1)),
                pl.BlockSpec((None, k, tn), lambda j, i: (j // cpt + p // 2, 0, j % cpt),
                             pipeline_mode=pl.Buffered(1))]
    shape = jax.ShapeDtypeStruct((m, f), BF16)
    got = _tiled_call(name, (f // tn, m // tm), [h, w13, w13], in_specs, [shape] * 3, [tile] * 3, [], compute, exchange)
    return got


def mm_dswiglu(name, dy, w2, by_gate, by_up, scale, exchange=None):
    m, k = dy.shape
    p, r, c = w2.shape
    assert c == k
    f = p * r
    tiles = _divisors(r, LANE, 2816) + [q * r for q in (2, 4) if p % q == 0 and q * r <= 2816]
    tm, tn = _plan_fused(m, k, f, tiles, 1, 4)

    def compute(ins, outs, scr):
        b = ins[1][...]
        if b.ndim == 3:
            b = b.reshape(b.shape[0] * b.shape[1], b.shape[2])
        d = lax.dot_general(ins[0][...], b, (((1,), (1,)), ((), ())), preferred_element_type=F32) * scale
        outs[0][...] = (d * ins[2][...].astype(F32)).astype(BF16)
        outs[1][...] = (d * ins[3][...].astype(F32)).astype(BF16)

    tile = pl.BlockSpec((tm, tn), lambda j, i: (i, j))
    if tn > r:
        w_spec = pl.BlockSpec((tn // r, r, k), lambda j, i: (j, 0, 0), pipeline_mode=pl.Buffered(1))
    else:
        rpt = r // tn
        w_spec = pl.BlockSpec((None, tn, k), lambda j, i: (j // rpt, j % rpt, 0), pipeline_mode=pl.Buffered(1))
    in_specs = [pl.BlockSpec((tm, k), lambda j, i: (i, 0)), w_spec, tile, tile]
    shape = jax.ShapeDtypeStruct((m, f), BF16)
    return _tiled_call(name, (f // tn, m // tm), [dy, w2, by_gate, by_up], in_specs, [shape] * 2, [tile] * 2, [], compute,
                       exchange)


def _rms_rows(x, g):
    r = lax.rsqrt(jnp.mean(x * x, axis=-1, keepdims=True) + RMS_EPS)
    xhat = x * r
    return xhat, r, xhat * g


def rms_fwd(name, x, g, exchange=None):
    s, d = x.shape
    tm = _row_tile(s, d * 4)

    def compute(ins, outs, scr):
        outs[0][...] = _rms_rows(ins[0][...], ins[1][...])[2].astype(BF16)

    got = _tiled_call(name, (s // tm,), [x, g.reshape(1, d)],
                      [pl.BlockSpec((tm, d), lambda i: (i, 0)), pl.BlockSpec((1, d), lambda i: (0, 0))],
                      [jax.ShapeDtypeStruct((s, d), BF16)], [pl.BlockSpec((tm, d), lambda i: (i, 0))], [], compute,
                      exchange)
    return (got[0][0], got[1]) if exchange else got[0]


def _rms_bwd_rows(x, g, dh):
    xhat, r, _ = _rms_rows(x, g)
    u = dh * g
    dx = r * (u - xhat * jnp.mean(u * xhat, axis=-1, keepdims=True))
    return dx, jnp.sum(dh * xhat, axis=0, keepdims=True)


def rms_bwd(name, x, g, dh, dres):
    s, d = x.shape
    tm = _row_tile(s, d * 4, 2 * 2**20)
    has_res = dres is not None

    def body(*refs):
        x_ref, g_ref, dh_ref = refs[:3]
        dres_ref = refs[3] if has_res else None
        dx_ref, dxb_ref, dg_ref = refs[-3:]
        dx, dg = _rms_bwd_rows(x_ref[...], g_ref[...], dh_ref[...].astype(F32))
        if has_res:
            dx = dx + dres_ref[...]
        dx_ref[...] = dx
        dxb_ref[...] = dx.astype(BF16)

        @pl.when(pl.program_id(0) == 0)
        def _():
            dg_ref[...] = dg

        @pl.when(pl.program_id(0) > 0)
        def _():
            dg_ref[...] += dg

    row = pl.BlockSpec((tm, d), lambda i: (i, 0))
    vec = pl.BlockSpec((1, d), lambda i: (0, 0))
    return pl.pallas_call(
        body, name=name, grid=(s // tm,),
        in_specs=[row, vec, row] + ([row] if has_res else []),
        out_specs=[row, row, vec],
        out_shape=[jax.ShapeDtypeStruct((s, d), F32), jax.ShapeDtypeStruct((s, d), BF16),
                   jax.ShapeDtypeStruct((1, d), F32)],
        compiler_params=_params(dimension_semantics=("arbitrary",)),
    )(x, g.reshape(1, d), dh, *([dres] if has_res else []))


def loss_head(name, x, g, target):
    s, d = x.shape
    tm = _row_tile(s, d * 4, 2 * 2**20)

    def body(x_ref, g_ref, t_ref, dx_ref, dxb_ref, dg_ref, loss_ref):
        x = x_ref[...]
        gain = g_ref[...]
        y = _rms_rows(x, gain)[2]
        diff = y - t_ref[...]
        dx, dg = _rms_bwd_rows(x, gain, diff * (1.0 / d))
        dx_ref[...] = dx
        dxb_ref[...] = dx.astype(BF16)
        sq = jnp.sum(diff * diff, axis=0, keepdims=True)

        @pl.when(pl.program_id(0) == 0)
        def _():
            dg_ref[...] = dg
            loss_ref[...] = sq

        @pl.when(pl.program_id(0) > 0)
        def _():
            dg_ref[...] += dg
            loss_ref[...] += sq

    row = pl.BlockSpec((tm, d), lambda i: (i, 0))
    vec = pl.BlockSpec((1, d), lambda i: (0, 0))
    return pl.pallas_call(
        body, name=name, grid=(s // tm,), in_specs=[row, vec, row], out_specs=[row, row, vec, vec],
        out_shape=[jax.ShapeDtypeStruct((s, d), F32), jax.ShapeDtypeStruct((s, d), BF16),
                   jax.ShapeDtypeStruct((1, d), F32), jax.ShapeDtypeStruct((1, d), F32)],
        compiler_params=_params(dimension_semantics=("arbitrary",)),
    )(x, g.reshape(1, d), target)


def _sigmoid(x):
    return 0.5 * jnp.tanh(0.5 * x) + 0.5


_INV_SQRT2 = 0.7071067811865476
_INV_SQRT_2PI = 0.3989422804014327


def _normal_cdf(z):
    return 0.5 * (1.0 + lax.erf(z * _INV_SQRT2))


def _gelu_grad(z, cdf):
    return cdf + z * (_INV_SQRT_2PI * jnp.exp(-0.5 * z * z))


def _causal_weights(ws_ref, g):
    t = ws_ref.shape[-1]
    keep = lax.broadcasted_iota(jnp.int32, (t, t), 0) >= lax.broadcasted_iota(jnp.int32, (t, t), 1)
    return jnp.where(keep, ws_ref[g], 0.0).astype(BF16), keep


def _gmlp_gate_rows(z_ref, lg_ref, lb_ref, e):
    z = z_ref[...].astype(F32)
    cdf = _normal_cdf(z)
    gz = z * cdf
    u, v = gz[:, :e], gz[:, e:]
    mu = jnp.mean(v, axis=-1, keepdims=True)
    xc = v - mu
    rs = lax.rsqrt(jnp.mean(xc * xc, axis=-1, keepdims=True) + LN_EPS)
    vhat = xc * rs
    return (z, cdf), u, vhat, rs, vhat * lg_ref[...] + lb_ref[...]


def gmlp_fwd(name, z, ln_g, ln_b, w_s, bias):
    s, e2 = z.shape
    e = e2 // 2
    eg = e // GMLP_GROUPS

    def body(z_ref, lg_ref, lb_ref, ws_ref, b_ref, o_ref):
        _, u, _, _, vln = _gmlp_gate_rows(z_ref, lg_ref, lb_ref, e)
        vb = vln.astype(BF16)
        for g in range(GMLP_GROUPS):
            cols = slice(g * eg, (g + 1) * eg)
            wm, _ = _causal_weights(ws_ref, g)
            f = jnp.dot(wm, vb[:, cols], preferred_element_type=F32) + b_ref[:, cols]
            o_ref[:, cols] = (u[:, cols] * f).astype(BF16)

    full = lambda shape: pl.BlockSpec(shape, lambda i: (0,) * len(shape))
    return pl.pallas_call(
        body, name=name, grid=(s // CHUNK,),
        in_specs=[pl.BlockSpec((CHUNK, e2), lambda i: (i, 0)), full((1, e)), full((1, e)),
                  full((GMLP_GROUPS, CHUNK, CHUNK)), full((CHUNK, e))],
        out_specs=pl.BlockSpec((CHUNK, e), lambda i: (i, 0)), out_shape=jax.ShapeDtypeStruct((s, e), BF16),
        compiler_params=_params(dimension_semantics=("arbitrary",)),
    )(z, ln_g.reshape(1, e), ln_b.reshape(1, e), w_s, bias)


def gmlp_bwd(name, z, dp, ln_g, ln_b, w_s, bias):
    s, e2 = z.shape
    e = e2 // 2
    eg = e // GMLP_GROUPS
    t = CHUNK

    def body(z_ref, dp_ref, lg_ref, lb_ref, ws_ref, b_ref, dz_ref, dws_ref, dbs_ref, dlg_ref, dlb_ref):
        first = pl.program_id(0) == 0
        (zf, cdf), u, vhat, rs, vln = _gmlp_gate_rows(z_ref, lg_ref, lb_ref, e)
        vb = vln.astype(BF16)
        dp = dp_ref[...].astype(F32)
        lane = lax.broadcasted_iota(jnp.int32, (t, LANE), 1)
        dbs = jnp.zeros((t, LANE), F32)
        dvln_parts = []
        for g in range(GMLP_GROUPS):
            cols = slice(g * eg, (g + 1) * eg)
            wm, keep = _causal_weights(ws_ref, g)
            f = jnp.dot(wm, vb[:, cols], preferred_element_type=F32) + b_ref[:, cols]
            dz_ref[:, cols] = (dp[:, cols] * f * _gelu_grad(zf[:, cols], cdf[:, cols])).astype(BF16)
            df = dp[:, cols] * u[:, cols]
            dfb = df.astype(BF16)
            dbs = dbs + jnp.where(lane == g, jnp.sum(df, axis=-1, keepdims=True), 0.0)
            dw = lax.dot_general(dfb, vb[:, cols], (((1,), (1,)), ((), ())), preferred_element_type=F32)
            dw = jnp.where(keep, dw, 0.0)

            @pl.when(first)
            def _():
                dws_ref[g] = dw

            @pl.when(jnp.logical_not(first))
            def _():
                dws_ref[g] += dw

            dvln_parts.append(lax.dot_general(wm, dfb, (((0,), (0,)), ((), ())), preferred_element_type=F32))
        dvln = jnp.concatenate(dvln_parts, axis=-1)
        dvhat = dvln * lg_ref[...]
        dv = rs * (dvhat - jnp.mean(dvhat, axis=-1, keepdims=True)
                   - vhat * jnp.mean(dvhat * vhat, axis=-1, keepdims=True))
        dz_ref[:, e:] = (dv * _gelu_grad(zf[:, e:], cdf[:, e:])).astype(BF16)
        dlg = jnp.sum(dvln * vhat, axis=0, keepdims=True)
        dlb = jnp.sum(dvln, axis=0, keepdims=True)

        @pl.when(first)
        def _():
            dbs_ref[...] = dbs
            dlg_ref[...] = dlg
            dlb_ref[...] = dlb

        @pl.when(jnp.logical_not(first))
        def _():
            dbs_ref[...] += dbs
            dlg_ref[...] += dlg
            dlb_ref[...] += dlb

    full = lambda shape: pl.BlockSpec(shape, lambda i: (0,) * len(shape))
    return pl.pallas_call(
        body, name=name, grid=(s // t,),
        in_specs=[pl.BlockSpec((t, e2), lambda i: (i, 0)), pl.BlockSpec((t, e), lambda i: (i, 0)), full((1, e)),
                  full((1, e)), full((GMLP_GROUPS, t, t)), full((t, e))],
        out_specs=[pl.BlockSpec((t, e2), lambda i: (i, 0)), full((GMLP_GROUPS, t, t)), full((t, LANE)), full((1, e)),
                   full((1, e))],
        out_shape=[jax.ShapeDtypeStruct((s, e2), BF16), jax.ShapeDtypeStruct((GMLP_GROUPS, t, t), F32),
                   jax.ShapeDtypeStruct((t, LANE), F32), jax.ShapeDtypeStruct((1, e), F32),
                   jax.ShapeDtypeStruct((1, e), F32)],
        compiler_params=_params(dimension_semantics=("arbitrary",)),
    )(z, dp, ln_g.reshape(1, e), ln_b.reshape(1, e), w_s, bias)


EDGE = 16


def _shift_down(zc, prev, k):
    tm = zc.shape[0]
    row = lax.broadcasted_iota(jnp.int32, (tm, 1), 0)
    out = pltpu.roll(zc, k, 0)
    for j in range(k):
        out = jnp.where(row == j, prev[EDGE - k + j:EDGE - k + j + 1, :], out)
    return out


def _shift_up(dc, nxt, k):
    tm = dc.shape[0]
    row = lax.broadcasted_iota(jnp.int32, (tm, 1), 0)
    out = pltpu.roll(dc, tm - k, 0)
    for j in range(k):
        out = jnp.where(row == tm - k + j, nxt[j:j + 1, :], out)
    return out


def conv_fwd(name, bcv, cw):
    s, d3 = bcv.shape
    d = d3 // 3
    tm = _row_tile(s, d * 4, 2 * 2**20)
    per = tm // EDGE

    def body(b_ref, c_ref, v_ref, cp_ref, vp_ref, w_ref, o_ref):
        i = pl.program_id(0)
        zc = c_ref[...].astype(F32) * v_ref[...].astype(F32)
        prev = jnp.where(i > 0, cp_ref[...].astype(F32) * vp_ref[...].astype(F32), 0.0)
        conv = w_ref[2:3, :] * zc + w_ref[1:2, :] * _shift_down(zc, prev, 1) + w_ref[0:1, :] * _shift_down(zc, prev, 2)
        o_ref[...] = (b_ref[...].astype(F32) * conv).astype(BF16)

    blk = lambda col: pl.BlockSpec((tm, d), lambda i: (i, col))
    edge = lambda col: pl.BlockSpec((EDGE, d), lambda i: (jnp.maximum(i * per - 1, 0), col))
    return pl.pallas_call(
        body, name=name, grid=(s // tm,),
        in_specs=[blk(0), blk(1), blk(2), edge(1), edge(2), pl.BlockSpec((3, d), lambda i: (0, 0))],
        out_specs=pl.BlockSpec((tm, d), lambda i: (i, 0)), out_shape=jax.ShapeDtypeStruct((s, d), BF16),
        compiler_params=_params(dimension_semantics=("arbitrary",)),
    )(bcv, bcv, bcv, bcv, bcv, cw)


def conv_bwd(name, bcv, dq, cw):
    s, d3 = bcv.shape
    d = d3 // 3
    tm = _row_tile(s, d * 4, 2**20)
    per = tm // EDGE
    n_tiles = s // tm
    last_edge = s // EDGE - 1

    def body(b_ref, c_ref, v_ref, cp_ref, vp_ref, bn_ref, dq_ref, dqn_ref, w_ref, o_ref, dw_ref):
        i = pl.program_id(0)
        b = b_ref[...].astype(F32)
        c = c_ref[...].astype(F32)
        v = v_ref[...].astype(F32)
        dq = dq_ref[...].astype(F32)
        zc = c * v
        prev = jnp.where(i > 0, cp_ref[...].astype(F32) * vp_ref[...].astype(F32), 0.0)
        z1 = _shift_down(zc, prev, 1)
        z2 = _shift_down(zc, prev, 2)
        w0, w1, w2 = w_ref[0:1, :], w_ref[1:2, :], w_ref[2:3, :]
        conv = w2 * zc + w1 * z1 + w0 * z2
        dconv = dq * b
        nxt = jnp.where(i < n_tiles - 1, dqn_ref[...].astype(F32) * bn_ref[...].astype(F32), 0.0)
        dz = w2 * dconv + w1 * _shift_up(dconv, nxt, 1) + w0 * _shift_up(dconv, nxt, 2)
        o_ref[:, :d] = (dq * conv).astype(BF16)
        o_ref[:, d:2 * d] = (dz * v).astype(BF16)
        o_ref[:, 2 * d:] = (dz * c).astype(BF16)
        dw = jnp.concatenate([jnp.sum(dconv * z2, axis=0, keepdims=True), jnp.sum(dconv * z1, axis=0, keepdims=True),
                              jnp.sum(dconv * zc, axis=0, keepdims=True), jnp.zeros((5, d), F32)], axis=0)

        @pl.when(i == 0)
        def _():
            dw_ref[...] = dw

        @pl.when(i > 0)
        def _():
            dw_ref[...] += dw

    blk = lambda col: pl.BlockSpec((tm, d), lambda i: (i, col))
    before = lambda col: pl.BlockSpec((EDGE, d), lambda i: (jnp.maximum(i * per - 1, 0), col))
    after = lambda col: pl.BlockSpec((EDGE, d), lambda i: (jnp.minimum((i + 1) * per, last_edge), col))
    return pl.pallas_call(
        body, name=name, grid=(n_tiles,),
        in_specs=[blk(0), blk(1), blk(2), before(1), before(2), after(0), blk(0), after(0),
                  pl.BlockSpec((3, d), lambda i: (0, 0))],
        out_specs=[pl.BlockSpec((tm, d3), lambda i: (i, 0)), pl.BlockSpec((8, d), lambda i: (0, 0))],
        out_shape=[jax.ShapeDtypeStruct((s, d3), BF16), jax.ShapeDtypeStruct((8, d), F32)],
        compiler_params=_params(dimension_semantics=("arbitrary",)),
    )(bcv, bcv, bcv, bcv, bcv, bcv, dq, dq, cw)


def _attn_probs(qh, kh, scale):
    sc = lax.dot_general(qh, kh, (((1,), (1,)), ((), ())), preferred_element_type=F32) * scale
    ex = jnp.exp(sc - jnp.max(sc, axis=-1, keepdims=True))
    return ex / jnp.sum(ex, axis=-1, keepdims=True)


def attn_fwd(name, q, kv):
    s, d = q.shape
    mlen = kv.shape[0]
    dh = d // XATTN_HEADS
    scale = dh ** -0.5
    tm = _row_tile(s, d * 4, 4 * 2**20)

    def body(q_ref, kv_ref, o_ref):
        for h in range(XATTN_HEADS):
            cols = slice(h * dh, (h + 1) * dh)
            p = _attn_probs(q_ref[:, cols], kv_ref[:, cols], scale)
            o_ref[:, cols] = jnp.dot(p.astype(BF16), kv_ref[:, d + h * dh:d + (h + 1) * dh],
                                     preferred_element_type=F32).astype(BF16)

    return pl.pallas_call(
        body, name=name, grid=(s // tm,),
        in_specs=[pl.BlockSpec((tm, d), lambda i: (i, 0)), pl.BlockSpec((mlen, 2 * d), lambda i: (0, 0))],
        out_specs=pl.BlockSpec((tm, d), lambda i: (i, 0)), out_shape=jax.ShapeDtypeStruct((s, d), BF16),
        compiler_params=_params(dimension_semantics=("arbitrary",)),
    )(q, kv)


def attn_bwd(name, q, kv, do):
    s, d = q.shape
    mlen = kv.shape[0]
    dh = d // XATTN_HEADS
    scale = dh ** -0.5
    tm = _row_tile(s, d * 4, 4 * 2**20)

    def body(q_ref, kv_ref, do_ref, dq_ref, dkv_ref):
        first = pl.program_id(0) == 0
        for h in range(XATTN_HEADS):
            cols = slice(h * dh, (h + 1) * dh)
            vcols = slice(d + h * dh, d + (h + 1) * dh)
            qh, kh, vh, doh = q_ref[:, cols], kv_ref[:, cols], kv_ref[:, vcols], do_ref[:, cols]
            p = _attn_probs(qh, kh, scale)
            dp = lax.dot_general(doh, vh, (((1,), (1,)), ((), ())), preferred_element_type=F32)
            ds = (p * (dp - jnp.sum(dp * p, axis=-1, keepdims=True)) * scale).astype(BF16)
            dq_ref[:, cols] = jnp.dot(ds, kh, preferred_element_type=F32).astype(BF16)
            dk = lax.dot_general(ds, qh, (((0,), (0,)), ((), ())), preferred_element_type=F32)
            dv = lax.dot_general(p.astype(BF16), doh, (((0,), (0,)), ((), ())), preferred_element_type=F32)

            @pl.when(first)
            def _():
                dkv_ref[:, cols] = dk
                dkv_ref[:, vcols] = dv

            @pl.when(jnp.logical_not(first))
            def _():
                dkv_ref[:, cols] += dk
                dkv_ref[:, vcols] += dv

    row = pl.BlockSpec((tm, d), lambda i: (i, 0))
    whole = pl.BlockSpec((mlen, 2 * d), lambda i: (0, 0))
    return pl.pallas_call(
        body, name=name, grid=(s // tm,), in_specs=[row, whole, row], out_specs=[row, whole],
        out_shape=[jax.ShapeDtypeStruct((s, d), BF16), jax.ShapeDtypeStruct((mlen, 2 * d), F32)],
        compiler_params=_params(dimension_semantics=("arbitrary",)),
    )(q, kv, do)


def _as_rows(a):
    if a.ndim >= 2 and a.shape[-1] % LANE == 0:
        return a.reshape(-1, a.shape[-1])
    return a.reshape(-1, LANE) if a.size % LANE == 0 else a.reshape(1, -1)


def add_halves(name, dw, other, core):
    p, r, c = dw.shape
    h = r // 2
    th = _row_tile(h, c * 2, 4 * 2**20)

    def body(core_ref, a_ref, b_ref, o_ref):
        o_ref[...] = (a_ref[...].astype(F32) + b_ref[...].astype(F32)).astype(BF16)

    grid_spec = pltpu.PrefetchScalarGridSpec(
        num_scalar_prefetch=1, grid=(p, h // th),
        in_specs=[pl.BlockSpec((None, None, th, c), lambda pi, i, core_ref: (pi, core_ref[0], i, 0)),
                  pl.BlockSpec((None, th, c), lambda pi, i, core_ref: (pi, i, 0))],
        out_specs=pl.BlockSpec((None, th, c), lambda pi, i, core_ref: (pi, i, 0)))
    return pl.pallas_call(
        body, name=name, grid_spec=grid_spec, out_shape=jax.ShapeDtypeStruct((p, h, c), BF16),
        compiler_params=_params(dimension_semantics=("arbitrary", "arbitrary")),
    )(core, dw.reshape(p, 2, h, c), other)


def sum_leading(name, parts):
    n, r, c = parts.shape
    tr = _row_tile(r, c * 4 * 2, 2 * 2**20)

    def body(p_ref, o_ref):
        acc = p_ref[0].astype(F32)
        for k in range(1, n):
            acc = acc + p_ref[k].astype(F32)
        o_ref[...] = acc

    return pl.pallas_call(
        body, name=name, grid=(r // tr,), in_specs=[pl.BlockSpec((n, tr, c), lambda i: (0, i, 0))],
        out_specs=pl.BlockSpec((tr, c), lambda i: (i, 0)), out_shape=jax.ShapeDtypeStruct((r, c), F32),
        compiler_params=_params(dimension_semantics=("arbitrary",)),
    )(parts)


def _adamw_rows(w, g, m, v):
    m = ADAM_B1 * m + (1.0 - ADAM_B1) * g
    v = ADAM_B2 * v + (1.0 - ADAM_B2) * (g * g)
    m_hat = m / (1.0 - ADAM_B1 ** ADAM_STEP)
    v_hat = v / (1.0 - ADAM_B2 ** ADAM_STEP)
    delta = -ADAM_LR * (m_hat / (jnp.sqrt(v_hat) + ADAM_EPS) + ADAM_WD * w)
    return delta, m, v


def adamw_layer(name, w, m, v, g, layer, carried):
    nl, r, c = w.shape
    tr = _row_tile(r, c * 4, 3 * 2**19)
    n_carried = 4 if carried is not None else 0

    def body(*refs):
        w_ref, m_ref, v_ref, g_ref = refs[:4]
        go_ref, d_ref, mo_ref, vo_ref = refs[4 + n_carried:]
        g = g_ref[...]
        delta, m_new, v_new = _adamw_rows(w_ref[...], g, m_ref[...], v_ref[...])
        go_ref[...] = g
        d_ref[...] = delta
        mo_ref[...] = m_new
        vo_ref[...] = v_new

    stacked = pl.BlockSpec((None, tr, c), lambda i: (layer, i, 0))
    in_specs = [stacked, stacked, stacked, pl.BlockSpec((tr, c), lambda i: (i, 0))]
    in_specs += [pl.BlockSpec(memory_space=pl.ANY)] * n_carried
    shape = jax.ShapeDtypeStruct((nl, r, c), F32)
    return pl.pallas_call(
        body, name=name, grid=(r // tr,), in_specs=in_specs, out_specs=[stacked] * 4, out_shape=[shape] * 4,
        input_output_aliases={4 + k: k for k in range(n_carried)},
        compiler_params=_params(dimension_semantics=("arbitrary",)),
    )(w, m, v, g, *(carried or ()))


def adamw_flat(name, w, m, v, g):
    r, c = w.shape

    def body(w_ref, m_ref, v_ref, g_ref, d_ref, mo_ref, vo_ref):
        delta, m_new, v_new = _adamw_rows(w_ref[...], g_ref[...], m_ref[...], v_ref[...])
        d_ref[...] = delta
        mo_ref[...] = m_new
        vo_ref[...] = v_new

    shape = jax.ShapeDtypeStruct((r, c), F32)
    return pl.pallas_call(body, name=name, out_shape=[shape] * 3, compiler_params=_params())(w, m, v, g)


def cast_place(name, w, layer, place):
    nl, r, c = w.shape
    tr = _row_tile(r, c * 4, 8 * 2**20)

    def body(x_ref, y_ref, c_ref, w_ref, o_ref):
        o_ref[...] = w_ref[...].astype(BF16)

    grid_spec = pltpu.PrefetchScalarGridSpec(
        num_scalar_prefetch=3, grid=(r // tr,),
        in_specs=[pl.BlockSpec((None, tr, c), lambda i, x_ref, y_ref, c_ref: (layer, i, 0))],
        out_specs=pl.BlockSpec((None, tr, c), lambda i, x_ref, y_ref, c_ref: (2 * x_ref[0] + y_ref[0], i, 0)))
    return pl.pallas_call(
        body, name=name, grid_spec=grid_spec, out_shape=jax.ShapeDtypeStruct((N_CHIPS, r, c), BF16),
        compiler_params=_params(dimension_semantics=("arbitrary",)),
    )(*place, w)


def reduce_sum4(name, own, landed, place):
    p, h, c = own.shape
    tr = _row_tile(h, c * 4, 4 * 2**20)

    def body(x_ref, y_ref, c_ref, t_ref, y1_ref, y2_ref, y3_ref, o_ref):
        acc = t_ref[...].astype(F32)
        for part_ref in (y1_ref, y2_ref, y3_ref):
            acc = acc + part_ref[...].astype(F32)
        o_ref[...] = acc

    def panel(fx, fy):
        return pl.BlockSpec((None, tr, c), lambda i, x_ref, y_ref, c_ref: (
            2 * (1 - x_ref[0] if fx else x_ref[0]) + (1 - y_ref[0] if fy else y_ref[0]), i, 0))

    grid_spec = pltpu.PrefetchScalarGridSpec(
        num_scalar_prefetch=3, grid=(h // tr,),
        in_specs=[panel(0, 0), panel(1, 0), panel(0, 1), panel(1, 1)],
        out_specs=pl.BlockSpec((None, tr, c), lambda i, x_ref, y_ref, c_ref: (c_ref[0], i, 0)))
    return pl.pallas_call(
        body, name=name, grid_spec=grid_spec, out_shape=jax.ShapeDtypeStruct((2, h, c), F32),
        compiler_params=_params(dimension_semantics=("arbitrary",)),
    )(*place, own, landed, landed, landed)


def run_exchange(name, exchange):
    n_in, n_out = len(exchange.inputs), len(exchange.out_shapes)

    def body(*refs):
        ins, outs, sems = refs[:n_in], refs[n_in:n_in + n_out], refs[n_in + n_out:]
        exchange.start(ins, outs, sems)
        exchange.finish(ins, outs, sems)

    return pl.pallas_call(
        body, name=name, in_specs=[ANY] * n_in, out_specs=[ANY] * n_out, out_shape=list(exchange.out_shapes),
        scratch_shapes=[pltpu.SemaphoreType.DMA((exchange.n_sems,)), pltpu.SemaphoreType.DMA((exchange.n_sems,))],
        input_output_aliases=dict(exchange.aliases),
    )(*exchange.inputs)


def _row_halves(ref, c):
    h = ref.shape[1] // 2
    return pl.ds(pl.multiple_of(c * h, 16), h), pl.ds(pl.multiple_of((1 - c) * h, 16), h)


def _in_place(arrays, n_sems, start, finish):
    return Exchange(list(arrays), [jax.ShapeDtypeStruct(f.shape, f.dtype) for f in arrays],
                    {a: a for a in range(len(arrays))}, n_sems, start, finish)


def gather_over_ici(fulls):
    n = len(fulls)

    def sends(outs, sems):
        x, y, c, mine, chips = _place()
        return [_remote(rows, rows, sems, 3 * a + j, (*chip, c)) for a in range(n)
                for rows in [outs[a].at[mine, _row_halves(outs[a], c)[0]]] for j, chip in enumerate(chips)]

    def start(ins, outs, sems):
        for cp in sends(outs, sems):
            cp.start()

    def finish(ins, outs, sems):
        x, y, c, mine, chips = _place()
        for a in range(n):
            for j, chip in enumerate(chips):
                rows = outs[a].at[2 * chip[0] + chip[1], _row_halves(outs[a], c)[0]]
                _remote(rows, rows, sems, 3 * a + j, (*chip, c)).wait_recv()
        for cp in sends(outs, sems):
            cp.wait_send()

    return _in_place(fulls, 3 * n, start, finish)


def gather_over_ici_by_neighbours(fulls):
    n = len(fulls)

    def plan(outs, a):
        x, y, c, mine, (nbr_x, nbr_y, far) = _place()
        h = outs[a].shape[1] // 2
        first = pl.ds(pl.multiple_of(c * h, 16), h // 2)
        second = pl.ds(pl.multiple_of(c * h + h // 2, 16), h // 2)
        index = lambda chip: 2 * chip[0] + chip[1]
        return c, mine, nbr_x, nbr_y, index, _row_halves(outs[a], c)[0], first, second, index(far)

    def direct(outs, sems, a):
        c, mine, nbr_x, nbr_y, _, half, _, _, _ = plan(outs, a)
        rows = outs[a].at[mine, half]
        return [_remote(rows, rows, sems, 4 * a, (*nbr_x, c)), _remote(rows, rows, sems, 4 * a + 1, (*nbr_y, c))]

    def passed_on(outs, sems, a):
        c, _, nbr_x, nbr_y, index, _, first, second, _ = plan(outs, a)
        from_x, from_y = outs[a].at[index(nbr_x), first], outs[a].at[index(nbr_y), second]
        return [_remote(from_x, from_x, sems, 4 * a + 2, (*nbr_y, c)), _remote(from_y, from_y, sems, 4 * a + 3, (*nbr_x, c))]

    def start(ins, outs, sems):
        for a in range(n):
            for cp in direct(outs, sems, a):
                cp.start()

    def finish(ins, outs, sems):
        for a in range(n):
            c, _, nbr_x, nbr_y, index, half, first, second, far = plan(outs, a)
            for k, nbr in enumerate((nbr_x, nbr_y)):
                rows = outs[a].at[index(nbr), half]
                _remote(rows, rows, sems, 4 * a + k, (*nbr, c)).wait_recv()
            for cp in passed_on(outs, sems, a):
                cp.start()
        for a in range(n):
            c, _, nbr_x, nbr_y, index, half, first, second, far = plan(outs, a)
            for k, (rows, nbr) in enumerate(((outs[a].at[far, first], nbr_y), (outs[a].at[far, second], nbr_x))):
                _remote(rows, rows, sems, 4 * a + 2 + k, (*nbr, c)).wait_recv()
            for cp in direct(outs, sems, a) + passed_on(outs, sems, a):
                cp.wait_send()

    return _in_place(fulls, 4 * n, start, finish)


def gather_over_d2d(fulls):
    n = len(fulls)

    def copies(outs, sems, which):
        x, y, c, mine, chips = _place()
        return [_remote(rows, rows, sems, 3 * a + j, (x, y, 1 - c)) for a in range(n) for j, chip in enumerate(chips)
                for rows in [outs[a].at[2 * chip[0] + chip[1], _row_halves(outs[a], c)[which]]]]

    def start(ins, outs, sems):
        for cp in copies(outs, sems, 0):
            cp.start()

    def finish(ins, outs, sems):
        for cp in copies(outs, sems, 1):
            cp.wait_recv()
        for cp in copies(outs, sems, 0):
            cp.wait_send()

    return _in_place(fulls, 3 * n, start, finish)


def gather_whole(fulls):
    ici, d2d = gather_over_ici(fulls), gather_over_d2d(fulls)

    def finish(ins, outs, sems):
        ici.finish(ins, outs, sems)
        later = tuple(_SemaphoresFrom(s, ici.n_sems) for s in sems)
        d2d.start(ins, outs, later)
        d2d.finish(ins, outs, later)

    return _in_place(fulls, ici.n_sems + d2d.n_sems, ici.start, finish)


class _SemaphoresFrom:
    def __init__(self, ref, offset):
        self.ref, self.offset = ref, offset

    @property
    def at(self):
        return self

    def __getitem__(self, k):
        return self.ref.at[self.offset + k]


def combine(exchanges):
    exchanges = [e for e in exchanges if e is not None]
    if len(exchanges) <= 1:
        return exchanges[0] if exchanges else None
    inputs, out_shapes, aliases, spans, n_sems = [], [], {}, [], 0
    for e in exchanges:
        aliases.update({len(inputs) + i: len(out_shapes) + o for i, o in e.aliases.items()})
        spans.append((len(inputs), len(e.inputs), len(out_shapes), len(e.out_shapes), n_sems))
        inputs, out_shapes, n_sems = inputs + list(e.inputs), out_shapes + list(e.out_shapes), n_sems + e.n_sems

    def each(method):
        def run(ins, outs, sems):
            for e, (i0, ni, o0, no, s0) in zip(exchanges, spans, strict=True):
                getattr(e, method)(ins[i0:i0 + ni], outs[o0:o0 + no], tuple(_SemaphoresFrom(s, s0) for s in sems))
        return run

    return Exchange(inputs, out_shapes, aliases, n_sems, each("start"), each("finish"))


def swap_exchange(grads):
    n = len(grads)

    def copies(ins, outs, sems):
        x, y, c, _, _ = _place()
        return [_remote(ins[a].at[:, _row_halves(ins[a], c)[1]], outs[a], sems, a, (x, y, 1 - c)) for a in range(n)]

    def start(ins, outs, sems):
        for cp in copies(ins, outs, sems):
            cp.start()

    def finish(ins, outs, sems):
        for cp in copies(ins, outs, sems):
            cp.wait()

    shapes = [jax.ShapeDtypeStruct((g.shape[0], g.shape[1] // 2, g.shape[2]), g.dtype) for g in grads]
    return Exchange(list(grads), shapes, {}, n, start, finish)


def scatter_exchange(parts):
    n = len(parts)

    def sends(ins, outs, sems):
        x, y, c, mine, chips = _place()
        return [_remote(ins[a].at[2 * chip[0] + chip[1]], outs[a].at[mine], sems, 3 * a + j, (*chip, c))
                for a in range(n) for j, chip in enumerate(chips)]

    def start(ins, outs, sems):
        for cp in sends(ins, outs, sems):
            cp.start()

    def finish(ins, outs, sems):
        x, y, c, mine, chips = _place()
        for a in range(n):
            for j, chip in enumerate(chips):
                landing = outs[a].at[2 * chip[0] + chip[1]]
                _remote(landing, landing, sems, 3 * a + j, (*chip, c)).wait_recv()
        for cp in sends(ins, outs, sems):
            cp.wait_send()

    return Exchange(list(parts), [jax.ShapeDtypeStruct(g.shape, g.dtype) for g in parts], {}, 3 * n, start, finish)


def join_exchange(halves):
    n = len(halves)

    def start(ins, outs, sems):
        x, y, c, _, _ = _place()
        for a in range(n):
            _remote(outs[a].at[c], outs[a].at[c], sems, a, (x, y, 1 - c)).start()

    def finish(ins, outs, sems):
        x, y, c, _, _ = _place()
        for a in range(n):
            _remote(outs[a].at[1 - c], outs[a].at[1 - c], sems, a, (x, y, 1 - c)).wait_recv()
        for a in range(n):
            _remote(outs[a].at[c], outs[a].at[c], sems, a, (x, y, 1 - c)).wait_send()

    return Exchange(list(halves), [jax.ShapeDtypeStruct(g.shape, g.dtype) for g in halves], {a: a for a in range(n)},
                    n, start, finish)


def gather_all(name, rows):
    def body(in_ref, out_ref, send_sems, recv_sems, local_sem):
        sems = (send_sems, recv_sems)
        x, y, c, _, _ = _place()
        me = 4 * x + 2 * y + c
        local = pltpu.make_async_copy(in_ref, out_ref.at[me], local_sem)
        local.start()
        peers = [(1 - x if k & 4 else x, 1 - y if k & 2 else y, 1 - c if k & 1 else c) for k in range(1, N_DEV)]
        sent = []
        for k, peer in enumerate(peers):
            cp = _remote(in_ref, out_ref.at[me], sems, k, peer)
            cp.start()
            sent.append(cp)
        for k, peer in enumerate(peers):
            landing = out_ref.at[4 * peer[0] + 2 * peer[1] + peer[2]]
            _remote(landing, landing, sems, k, peer).wait_recv()
        for cp in sent:
            cp.wait_send()
        local.wait()

    return pl.pallas_call(
        body, name=name, in_specs=[ANY], out_specs=ANY,
        out_shape=jax.ShapeDtypeStruct((N_DEV,) + rows.shape, rows.dtype),
        scratch_shapes=[pltpu.SemaphoreType.DMA((N_DEV - 1,)), pltpu.SemaphoreType.DMA((N_DEV - 1,)),
                        pltpu.SemaphoreType.DMA],
    )(rows)


class _Step:
    def __init__(self, p):
        self.p = p
        xi, yi, ci = lax.axis_index("x"), lax.axis_index("y"), lax.axis_index("c")
        self.chip = 2 * xi + yi
        self.place_refs = tuple(v.astype(jnp.int32).reshape(1) for v in (xi, yi, ci))
        self.core_ref = self.place_refs[2]
        self.depth = p['ffn1_norm'].shape[0]
        self.placed, self.landed, self.w = {}, {}, {}
        self.big_g = {}
        self.waiting_joins = []
        self.waiting_scatter = None

    def block_keys(self, tag, l):
        if l >= self.depth:
            return []
        mixer = ['gmlp_w_in', 'gmlp_w_out'] if l % 2 == 0 else ['conv_w_in', 'conv_w_out']
        names = {"ffn1": ['ffn1_w13', 'ffn1_w2'], "mix": mixer, "xattn": ['xattn_wq', 'xattn_wkv', 'xattn_wo'],
                 "ffn2": ['ffn2_w13', 'ffn2_w2']}[tag]
        return [(n, l // 2 if tag == "mix" else l) for n in names]

    def place(self, keys):
        for n, idx in keys:
            self.placed[(n, idx)] = cast_place(f"place_{n}{idx}", self.p[n], idx, self.place_refs)

    def carrying_gather(self, call, over_ici, over_d2d, whole, *args, **kw):
        over_ici = [k for k in over_ici if k in self.placed]
        over_d2d = [k for k in over_d2d if k in self.landed]
        parts = [gather_over_ici([self.placed.pop(k) for k in over_ici]) if over_ici else None,
                 gather_over_d2d([self.landed.pop(k) for k in over_d2d]) if over_d2d else None,
                 gather_whole([self.placed.pop(k) for k in whole]) if whole else None]
        exchange = combine(parts)
        if exchange is None:
            return call(*args, **kw)
        out, got = call(*args, exchange=exchange, **kw)
        self.landed.update(zip(over_ici, got[:len(over_ici)], strict=True))
        self.w.update(zip(over_d2d + whole, got[len(over_ici):], strict=True))
        return out

    def reduce_begin(self, tag, keys, dws, theirs=None):
        theirs = list(theirs or [None] * len(dws))
        todo = [i for i, t in enumerate(theirs) if t is None]
        for i, t in zip(todo, run_exchange(tag + "_swap", swap_exchange([dws[i] for i in todo])), strict=True):
            theirs[i] = t
        parts = [add_halves(f"{tag}_add{i}", dw, t, self.core_ref) for i, (dw, t) in enumerate(zip(dws, theirs, strict=True))]
        assert self.waiting_scatter is None
        self.waiting_scatter = (tag, keys, parts)

    def carrying_scatter(self, mm, *args, **kw):
        tag, keys, parts = self.waiting_scatter
        self.waiting_scatter = None
        out, landed = mm(*args, exchange=scatter_exchange(parts), **kw)
        halves = [reduce_sum4(f"{tag}_sum{i}", t, y, self.place_refs) for i, (t, y) in enumerate(zip(parts, landed, strict=True))]
        self.waiting_joins += list(zip(keys, halves, strict=True))
        return out

    def take_joined(self, keys, joined):
        for k, g in zip(keys, joined, strict=True):
            self.big_g[k] = g.reshape(-1, g.shape[-1])

    def carrying_joins(self, mm, *args, **kw):
        if not self.waiting_joins:
            return mm(*args, **kw)
        keys, halves = zip(*self.waiting_joins, strict=True)
        self.waiting_joins = []
        out, joined = mm(*args, exchange=join_exchange(list(halves)), **kw)
        self.take_joined(keys, joined)
        return out

    def joins_alone(self, name):
        keys, halves = zip(*self.waiting_joins, strict=True)
        self.waiting_joins = []
        self.take_joined(keys, run_exchange(name, join_exchange(list(halves))))

    def ffn_fwd(self, tag, l, x, gain, carry_norm, carry13, carry2, h=None):
        name = f"l{l}_{tag}"
        if h is None:
            h = self.carrying_gather(rms_fwd, *carry_norm, name + "_norm", x, gain)
        by_gate, by_up, act = self.carrying_gather(mm_swiglu, *carry13, name + "_w13", h, self.w[(tag + '_w13', l)])
        out = self.carrying_gather(mm_nn, *carry2, name + "_w2", act, self.w[(tag + '_w2', l)], 'row', F32, res=x,
                                   scale=0.5)
        return out, (x, h, by_gate, by_up, act)

    def ffn_bwd(self, tag, l, dx, dxb, saved, gain):
        w13, w2 = self.w[(tag + '_w13', l)], self.w[(tag + '_w2', l)]
        name = f"l{l}_{tag}"
        x, h, by_gate, by_up, act = saved
        d_gate, d_up = self.carrying_joins(mm_dswiglu, name + "_dact", dxb, w2, by_gate, by_up, 0.5)
        d_w2 = mm_tn(name + "_dw2", act, dxb, 'row', scale=0.5)
        half = N_CHIPS // 2
        d_w13, their_w2 = mm_tn(name + "_dw13g", h, d_gate, 'col', panels=(0, half), exchange=swap_exchange([d_w2]))
        d_w13 = mm_tn(name + "_dw13u", h, d_up, 'col', panels=(half, half), into=d_w13)
        self.reduce_begin(name, [(tag + '_w13', l), (tag + '_w2', l)], [d_w13, d_w2], [None] + their_w2)
        dh = self.carrying_scatter(mm_nt, name + "_dh", d_gate, w13, 'col', BF16, a_hi=d_up)
        return rms_bwd(name + "_dnorm", x, gain, dh, dx)


def kernel(x, mem, ffn1_norm, ffn1_w13, ffn1_w2, mix_norm, gmlp_w_in, gmlp_ln_g, gmlp_ln_b, gmlp_w_s, gmlp_b_s, gmlp_w_out, conv_w_in, conv_w, conv_w_out, xattn_norm, mem_norm, xattn_wq, xattn_wkv, xattn_wo, ffn2_norm, ffn2_w13, ffn2_w2, final_norm, loss_target, m_ffn1_norm, m_ffn1_w13, m_ffn1_w2, m_mix_norm, m_gmlp_w_in, m_gmlp_ln_g, m_gmlp_ln_b, m_gmlp_w_s, m_gmlp_b_s, m_gmlp_w_out, m_conv_w_in, m_conv_w, m_conv_w_out, m_xattn_norm, m_mem_norm, m_xattn_wq, m_xattn_wkv, m_xattn_wo, m_ffn2_norm, m_ffn2_w13, m_ffn2_w2, m_final_norm, v_ffn1_norm, v_ffn1_w13, v_ffn1_w2, v_mix_norm, v_gmlp_w_in, v_gmlp_ln_g, v_gmlp_ln_b, v_gmlp_w_s, v_gmlp_b_s, v_gmlp_w_out, v_conv_w_in, v_conv_w, v_conv_w_out, v_xattn_norm, v_mem_norm, v_xattn_wq, v_xattn_wkv, v_xattn_wo, v_ffn2_norm, v_ffn2_w13, v_ffn2_w2, v_final_norm):
    return _step(dict(locals()))


def _step(p):
    assert sorted(p) == sorted(ARG_NAMES)
    st = _Step(p)
    x = p['x'][0]
    mem = p['mem'][0]
    target = p['loss_target'][0]
    s, d = x.shape
    depth = st.depth

    for l in range(depth):
        for tag in ("ffn1", "mix", "xattn", "ffn2"):
            st.place(st.block_keys(tag, l))
    first = ('ffn1_w13', 0)
    st.landed[first] = run_exchange("gather_first", gather_over_ici_by_neighbours([st.placed.pop(first)]))[0]

    cw_local = p['conv_w']
    n_conv, cwid, dq4 = cw_local.shape
    cw_rows = jnp.pad(cw_local.reshape(-1, LANE), ((0, (-cw_local.size // LANE) % 8), (0, 0)))
    cw_all = gather_all("gather_conv_w", cw_rows)[0::2, :cw_local.size // LANE]
    conv_w_full = cw_all.reshape(N_CHIPS, n_conv, cwid, dq4).transpose(1, 2, 0, 3).reshape(n_conv, cwid, N_CHIPS * dq4)

    saved = []
    for l in range(depth):
        j = l // 2
        rec = {}
        soon = st.block_keys("mix", l) + st.block_keys("xattn", l)
        x, rec['ffn1'] = st.ffn_fwd("ffn1", l, x, p['ffn1_norm'][l],
                                    ([], [('ffn1_w13', 0)] if l == 0 else [], []),
                                    (soon, [], [('ffn1_w2', 0)] if l == 0 else []),
                                    ([('ffn2_w13', l)], soon, []))
        h = rms_fwd(f"l{l}_mix_norm", x, p['mix_norm'][l])
        if l % 2 == 0:
            e = p['gmlp_ln_g'].shape[-1]
            bias = jnp.repeat(p['gmlp_b_s'][j].T, e // GMLP_GROUPS, axis=1)
            z = st.carrying_gather(mm_nn, [('ffn2_w2', l)], [('ffn2_w13', l)], [], f"l{l}_gmlp_in", h,
                                   st.w[('gmlp_w_in', j)], 'col', BF16)
            gate = gmlp_fwd(f"l{l}_gmlp_gate", z, p['gmlp_ln_g'][j], p['gmlp_ln_b'][j], p['gmlp_w_s'][j], bias)
            x_new, hq = st.carrying_gather(mm_nn, [], [('ffn2_w2', l)], [], f"l{l}_gmlp_out", gate,
                                           st.w[('gmlp_w_out', j)], 'row', F32, res=x, norm_gain=p['xattn_norm'][l])
            rec['mix'] = (x, h, z, gate, bias)
        else:
            bcv = st.carrying_gather(mm_nn, [('ffn2_w2', l)], [('ffn2_w13', l)], [], f"l{l}_conv_in", h,
                                     st.w[('conv_w_in', j)], 'col', BF16)
            gate = conv_fwd(f"l{l}_conv_gate", bcv, conv_w_full[j])
            x_new, hq = st.carrying_gather(mm_nn, [], [('ffn2_w2', l)], [], f"l{l}_conv_out", gate,
                                           st.w[('conv_w_out', j)], 'row', F32, res=x, norm_gain=p['xattn_norm'][l])
            rec['mix'] = (x, h, bcv, gate)
        x = x_new
        q = mm_nn(f"l{l}_xattn_q", hq, st.w[('xattn_wq', l)], 'row', BF16)
        mem_n = rms_fwd(f"l{l}_mem_norm", mem, p['mem_norm'][l])
        kv = mm_nn(f"l{l}_xattn_kv", mem_n, st.w[('xattn_wkv', l)], 'col', BF16)
        o = attn_fwd(f"l{l}_xattn_core", q, kv)
        x_new, h_ffn2 = mm_nn(f"l{l}_xattn_o", o, st.w[('xattn_wo', l)], 'row', F32, res=x,
                              norm_gain=p['ffn2_norm'][l])
        rec['xattn'] = (x, hq, q, mem_n, kv, o)
        x = x_new
        ahead = st.block_keys("ffn1", l + 1)
        x, rec['ffn2'] = st.ffn_fwd("ffn2", l, x, p['ffn2_norm'][l], None, (ahead, [], []), ([], ahead, []), h=h_ffn2)
        saved.append(rec)

    dx, dxb, d_final, loss_lanes = loss_head("loss_head", x, p['final_norm'], target)
    loss = lax.psum(0.5 * jnp.sum(loss_lanes) / d, ("x", "y", "c"))

    small = {n: [None] * p[n].shape[0] for n in ('ffn1_norm', 'mix_norm', 'xattn_norm', 'mem_norm', 'ffn2_norm',
                                                  'gmlp_ln_g', 'gmlp_ln_b', 'gmlp_w_s', 'gmlp_b_s', 'conv_w')}
    for l in reversed(range(depth)):
        j = l // 2
        rec = saved[l]
        dx, dxb, small['ffn2_norm'][l] = st.ffn_bwd("ffn2", l, dx, dxb, rec['ffn2'], p['ffn2_norm'][l])

        x_in, hq, q, mem_n, kv, o = rec['xattn']
        name = f"l{l}_xattn"
        do = st.carrying_joins(mm_nt, name + "_do", dxb, st.w[('xattn_wo', l)], 'row', BF16)
        d_wo = mm_tn(name + "_dwo", o, dxb, 'row')
        dq, dkv = attn_bwd(name + "_dcore", q, kv, do)
        d_wq = mm_tn(name + "_dwq", hq, dq, 'row')
        dkvb = dkv.astype(BF16)
        d_wkv = mm_tn(name + "_dwkv", mem_n, dkvb, 'col')
        st.reduce_begin(name, [('xattn_wq', l), ('xattn_wkv', l), ('xattn_wo', l)], [d_wq, d_wkv, d_wo])
        dh = mm_nt(name + "_dh", dq, st.w[('xattn_wq', l)], 'row', BF16)
        dx, dxb, small['xattn_norm'][l] = rms_bwd(name + "_dnorm", x_in, p['xattn_norm'][l], dh, dx)
        dmem_n = mm_nt(name + "_dmem", dkvb, st.w[('xattn_wkv', l)], 'col', F32)
        small['mem_norm'][l] = rms_bwd(f"l{l}_mem_dnorm", mem, p['mem_norm'][l], dmem_n, None)[2]

        if l % 2 == 0:
            x_in, h, z, gate, bias = rec['mix']
            name = f"l{l}_gmlp"
            w_in, w_out = st.w[('gmlp_w_in', j)], st.w[('gmlp_w_out', j)]
            dgate = mm_nt(name + "_dgate", dxb, w_out, 'row', BF16)
            d_wout = mm_tn(name + "_dwout", gate, dxb, 'row')
            dmix, dws, dbs, dlg, dlb = gmlp_bwd(name + "_dgate_core", z, dgate, p['gmlp_ln_g'][j], p['gmlp_ln_b'][j],
                                                p['gmlp_w_s'][j], bias)
            small['gmlp_w_s'][j], small['gmlp_b_s'][j] = dws, dbs[:, :GMLP_GROUPS].T
            small['gmlp_ln_g'][j], small['gmlp_ln_b'][j] = dlg, dlb
            keys = [('gmlp_w_in', j), ('gmlp_w_out', j)]
        else:
            x_in, h, bcv, gate = rec['mix']
            name = f"l{l}_conv"
            w_in, w_out = st.w[('conv_w_in', j)], st.w[('conv_w_out', j)]
            dgate = mm_nt(name + "_dgate", dxb, w_out, 'row', BF16)
            d_wout = mm_tn(name + "_dwout", gate, dxb, 'row')
            dmix, dcw = conv_bwd(name + "_dgate_core", bcv, dgate, conv_w_full[j])
            small['conv_w'][j] = dcw[:cwid]
            keys = [('conv_w_in', j), ('conv_w_out', j)]
        d_win = st.carrying_scatter(mm_tn, name + "_dwin", h, dmix, 'col')
        st.reduce_begin(name, keys, [d_win, d_wout])
        dh = st.carrying_scatter(mm_nt, name + "_dh", dmix, w_in, 'col', BF16)
        dx, dxb, small['mix_norm'][l] = rms_bwd(f"l{l}_mix_dnorm", x_in, p['mix_norm'][l], dh, dx)

        dx, dxb, small['ffn1_norm'][l] = st.ffn_bwd("ffn1", l, dx, dxb, rec['ffn1'], p['ffn1_norm'][l])
    st.joins_alone("join_last")

    small_names = ['ffn1_norm', 'mix_norm', 'xattn_norm', 'mem_norm', 'ffn2_norm', 'gmlp_ln_g', 'gmlp_ln_b', 'gmlp_w_s',
                   'gmlp_b_s', 'final_norm', 'conv_w']
    small_full = {n: jnp.stack([g.reshape(p[n].shape[1:]) for g in small[n]]) for n in small_names
                  if n not in ('final_norm', 'conv_w')}
    small_full['final_norm'] = d_final.reshape(p['final_norm'].shape)
    small_full['conv_w'] = jnp.stack(small['conv_w'])
    packed = jnp.concatenate([small_full[n].reshape(-1, LANE) for n in small_names], axis=0)
    total = sum_leading("small_sum", gather_all("small_gather", packed))
    small_g, at = {}, 0
    for n in small_names:
        rows = small_full[n].size // LANE
        small_g[n] = total[at:at + rows].reshape(small_full[n].shape)
        at += rows
    small_g['conv_w'] = lax.dynamic_slice_in_dim(small_g['conv_w'], st.chip * dq4, dq4, axis=2)

    grads, deltas, new_m, new_v = {}, {}, {}, {}
    for n in WEIGHTS:
        w, m, v = p[n], p['m_' + n], p['v_' + n]
        if n in BIG:
            carried = None
            for i in range(w.shape[0]):
                carried = adamw_layer(f"adamw_{n}{i}", w, m, v, st.big_g[(n, i)], i, carried)
            grads[n], deltas[n], new_m[n], new_v[n] = carried
        else:
            g = small_g[n]
            out = adamw_flat(f"adamw_{n}", _as_rows(w), _as_rows(m), _as_rows(v), _as_rows(g))
            grads[n] = g
            deltas[n], new_m[n], new_v[n] = (o.reshape(w.shape) for o in out)

    grad_x = dx.reshape(p['x'].shape)
    return (loss, grad_x, *[grads[n] for n in WEIGHTS], *[deltas[n] for n in WEIGHTS], *[new_m[n] for n in WEIGHTS],
            *[new_v[n] for n in WEIGHTS])
```

```python
from typing import Callable, NamedTuple

import jax
import jax.numpy as jnp
from jax import lax
from jax.experimental import pallas as pl
from jax.experimental.pallas import tpu as pltpu

F32 = jnp.float32
BF16 = jnp.bfloat16
MESH = pl.DeviceIdType.MESH

CHUNK = 128
GMLP_GROUPS = 8
XATTN_HEADS = 4
RMS_EPS = 1e-6
LN_EPS = 1e-5
ADAM_LR = 0.001
ADAM_B1 = 0.9
ADAM_B2 = 0.999
ADAM_EPS = 1e-08
ADAM_WD = 0.01
ADAM_STEP = 10

N_CHIPS = 4
N_DEV = 8

VMEM_LIMIT_BYTES = 60 * 2**20
VMEM_PLAN_BYTES = 52 * 2**20
LANE = 128
MXU_DIM = 256
ACC_CHUNK = 2 * MXU_DIM
MXU_FLOPS_PER_US = 996e6
HBM_BYTES_PER_US = 3.3e6
STEP_US = 0.35
ACC_US_PER_VREG = 0.58e-3

WEIGHTS = ['ffn1_norm', 'ffn1_w13', 'ffn1_w2', 'mix_norm', 'gmlp_w_in', 'gmlp_ln_g', 'gmlp_ln_b', 'gmlp_w_s',
           'gmlp_b_s', 'gmlp_w_out', 'conv_w_in', 'conv_w', 'conv_w_out', 'xattn_norm', 'mem_norm', 'xattn_wq',
           'xattn_wkv', 'xattn_wo', 'ffn2_norm', 'ffn2_w13', 'ffn2_w2', 'final_norm']
BIG = {'ffn1_w13': 'col', 'ffn1_w2': 'row', 'gmlp_w_in': 'col', 'gmlp_w_out': 'row', 'conv_w_in': 'col',
       'conv_w_out': 'row', 'xattn_wq': 'row', 'xattn_wkv': 'col', 'xattn_wo': 'row', 'ffn2_w13': 'col',
       'ffn2_w2': 'row'}
ARG_NAMES = (['x', 'mem'] + WEIGHTS + ['loss_target'] + ['m_' + n for n in WEIGHTS] + ['v_' + n for n in WEIGHTS])


def _params(**kw):
    return pltpu.CompilerParams(vmem_limit_bytes=VMEM_LIMIT_BYTES, **kw)


def _divisors(n, mult, cap):
    return [d for d in range(mult, min(n, cap) + 1, mult) if n % d == 0] or [n]


def _row_tile(rows, width_bytes, budget=4 * 2**20):
    best = None
    for d in _divisors(rows, 16, 1024):
        if d * width_bytes <= budget:
            best = d
    return best or _divisors(rows, 16, 1024)[0]


ANY = pl.BlockSpec(memory_space=pl.ANY)


class Exchange(NamedTuple):
    inputs: list
    out_shapes: list
    aliases: dict
    n_sems: int
    start: Callable
    finish: Callable


def _place():
    x, y, c = lax.axis_index("x"), lax.axis_index("y"), lax.axis_index("c")
    chips = [(1 - x, y), (x, 1 - y), (1 - x, 1 - y)]
    return x, y, c, 2 * x + y, chips


def _remote(src, dst, sems, k, device):
    return pltpu.make_async_remote_copy(src_ref=src, dst_ref=dst, send_sem=sems[0].at[k], recv_sem=sems[1].at[k],
                                        device_id=device, device_id_type=MESH)


def _mxu_fill(dim):
    return dim / (-(-dim // MXU_DIM) * MXU_DIM)


def _tile_time(flops, fill, traffic, steps, acc_vregs):
    return (max(flops / (MXU_FLOPS_PER_US * fill), traffic / HBM_BYTES_PER_US) + steps * STEP_US
            + steps * acc_vregs * ACC_US_PER_VREG)


def _plan_mm(m, n_tiles_of, k_tiles_of, n, k, a_item, o_item, has_res, a_arrays=1):
    best, best_cost = None, None
    for tm in _divisors(m, 16, 1024):
        for tn in n_tiles_of:
            for tk in k_tiles_of:
                ni, nj, nk = m // tm, n // tn, k // tk
                blocks = a_arrays * tm * tk * a_item + tk * tn * 2 + tm * tn * o_item + (tm * tn * 4 if has_res else 0)
                vmem = 2 * blocks + tm * tn * 4 * (2 if nk > 1 else 1)
                if vmem > VMEM_PLAN_BYTES:
                    continue
                traffic = nj * m * k * a_item + (k * n * 2 if nk == 1 else ni * k * n * 2)
                traffic += m * n * (o_item + (4 if has_res else 0))
                cost = _tile_time(2 * m * n * k, _mxu_fill(tk) * _mxu_fill(tn), traffic, ni * nj * nk,
                                  tm * tn // 1024 if nk > 1 else 0)
                if best_cost is None or cost < best_cost:
                    best, best_cost = (tm, tn, tk), cost
    assert best is not None, (m, n, k)
    return best


def _tiled_call(name, grid, operands, in_specs, out_shapes, out_specs, scratch, compute, exchange=None, aliases=None):
    n_reg, n_out, n_scr = len(operands), len(out_shapes), len(scratch)
    n_xin = len(exchange.inputs) if exchange else 0
    n_xout = len(exchange.out_shapes) if exchange else 0
    semantics = ("arbitrary",) * len(grid)

    def body(*refs):
        ins = refs[:n_reg]
        outs = refs[n_reg + n_xin:n_reg + n_xin + n_out]
        scr = refs[n_reg + n_xin + n_out + n_xout:n_reg + n_xin + n_out + n_xout + n_scr]
        if not exchange:
            compute(ins, outs, scr)
            return
        x_ins = refs[n_reg:n_reg + n_xin]
        x_outs = refs[n_reg + n_xin + n_out:n_reg + n_xin + n_out + n_xout]
        sems = refs[-2:]
        at_first, at_last = True, True
        for k, extent in enumerate(grid):
            at_first = jnp.logical_and(at_first, pl.program_id(k) == 0)
            at_last = jnp.logical_and(at_last, pl.program_id(k) == extent - 1)

        @pl.when(at_first)
        def _():
            exchange.start(x_ins, x_outs, sems)

        compute(ins, outs, scr)

        @pl.when(at_last)
        def _():
            exchange.finish(x_ins, x_outs, sems)

    if not exchange:
        return pl.pallas_call(
            body, name=name, grid=grid, in_specs=in_specs, out_specs=out_specs, out_shape=out_shapes,
            scratch_shapes=scratch, input_output_aliases=dict(aliases or {}),
            compiler_params=_params(dimension_semantics=semantics),
        )(*operands)
    assert not aliases
    sems = [pltpu.SemaphoreType.DMA((exchange.n_sems,)), pltpu.SemaphoreType.DMA((exchange.n_sems,))]
    got = pl.pallas_call(
        body, name=name, grid=grid, in_specs=in_specs + [ANY] * n_xin, out_specs=out_specs + [ANY] * n_xout,
        out_shape=out_shapes + list(exchange.out_shapes), scratch_shapes=scratch + sems,
        input_output_aliases={n_reg + i: n_out + o for i, o in exchange.aliases.items()},
        compiler_params=_params(dimension_semantics=semantics),
    )(*operands, *exchange.inputs)
    return list(got[:n_out]), list(got[n_out:])


class Split(NamedTuple):
    slot: int
    other: jax.Array
    spec: pl.BlockSpec
    use_other: Callable


def _mm_call(name, grid, operands, in_specs, out_shape, out_spec, contract, nk, scale, has_res, tile, exchange=None,
             split=None, into=None, normed=False, norm_back=False):
    n_main = len(operands)
    out_shapes, out_specs = [out_shape], [out_spec]
    if normed:
        out_shapes, out_specs = out_shapes + [jax.ShapeDtypeStruct(out_shape.shape, BF16)], out_specs + [out_spec]
    if norm_back:
        m, n = out_shape.shape
        out_shapes = [jax.ShapeDtypeStruct((m, n), F32), jax.ShapeDtypeStruct((m, n), BF16),
                      jax.ShapeDtypeStruct((1, n), F32)]
        out_specs = [out_spec, out_spec, pl.BlockSpec((1, n), lambda j, i, kk: (0, 0))]

    def compute(ins, outs, scr):
        res_ref = ins[2] if has_res else None
        o_ref = outs[0]
        acc_ref = scr[0] if nk > 1 else None

        def finish_norm_back(v):
            dx, dg = _rms_bwd_rows(ins[2][...], ins[3][...], v)
            dx = dx + ins[4][...]
            outs[0][...] = dx
            outs[1][...] = dx.astype(BF16)
            first_rows = pl.program_id(1) == 0

            @pl.when(first_rows)
            def _():
                outs[2][...] = dg

            @pl.when(jnp.logical_not(first_rows))
            def _():
                outs[2][...] += dg

        def finish(v):
            if scale != 1.0:
                v = v * scale
            if norm_back:
                finish_norm_back(v)
                return
            if has_res:
                v = res_ref[...] + v
            o_ref[...] = v.astype(o_ref.dtype)
            if normed:
                outs[1][...] = _rms_rows(v, ins[2 + has_res][...])[2].astype(BF16)

        def b_block(cols):
            ref = b_ref_of[0]
            stacked = len(ref.shape) == 3
            if contract[0][1] == (1,):
                if not stacked:
                    return ref[cols, :]
                return ref[...].reshape(ref.shape[0] * ref.shape[1], ref.shape[2])[cols, :]
            if not stacked:
                return ref[:, cols]
            b = ref[:, :, cols]
            return b.reshape(b.shape[0] * b.shape[1], b.shape[2])

        b_ref_of = [None]

        def contribute(a_ref, b_ref):
            b_ref_of[0] = b_ref
            if nk == 1:
                finish(lax.dot_general(a_ref[...], b_block(slice(None)), contract, preferred_element_type=F32))
                return
            kk = pl.program_id(2)

            @pl.when(kk == 0)
            def _():
                acc_ref[...] = jnp.zeros(tile, F32)

            a = a_ref[...]
            for start in range(0, tile[1], ACC_CHUNK):
                cols = slice(start, min(start + ACC_CHUNK, tile[1]))
                acc_ref[:, cols] += lax.dot_general(a, b_block(cols), contract, preferred_element_type=F32)

            @pl.when(kk == nk - 1)
            def _():
                finish(acc_ref[...])

        if split is None:
            contribute(ins[0], ins[1])
            return
        use_other = split.use_other(pl.program_id(0), pl.program_id(1), pl.program_id(2))
        pair = [ins[0], ins[1]]
        other = list(pair)
        other[split.slot] = ins[n_main]

        @pl.when(jnp.logical_not(use_other))
        def _():
            contribute(*pair)

        @pl.when(use_other)
        def _():
            contribute(*other)

    aliases = None
    if split is not None:
        operands, in_specs = operands + [split.other], in_specs + [split.spec]
    if into is not None:
        aliases = {len(operands): 0}
        operands, in_specs = operands + [into], in_specs + [ANY]
    got = _tiled_call(name, grid, operands, in_specs, out_shapes, out_specs,
                      [pltpu.VMEM(tile, F32)] if nk > 1 else [], compute, exchange, aliases)
    results, carried = (got[0], got[1]) if exchange else (got, None)
    out = tuple(results) if normed or norm_back else results[0]
    return (out, carried) if exchange else out


def mm_nn(name, a, w, kind, out_dtype, res=None, scale=1.0, exchange=None, norm_gain=None):
    m, k = a.shape
    p, r, c = w.shape
    n = p * c if kind == 'col' else c
    assert k == (r if kind == 'col' else p * r), (name, a.shape, w.shape)
    n_tiles = _divisors(c, LANE, 2816)
    if norm_gain is not None:
        assert kind == 'row'
        n_tiles = [n]
    k_tiles = _divisors(r, LANE, 4096)
    if kind == 'row':
        k_tiles = k_tiles + [q * r for q in (2, 4) if p % q == 0]
    o_item = jnp.dtype(out_dtype).itemsize + (2 if norm_gain is not None else 0)
    tm, tn, tk = _plan_mm(m, n_tiles, k_tiles, n, k, a.dtype.itemsize, o_item, res is not None)
    nk = k // tk
    if kind == 'col':
        cpt = c // tn
        w_spec = pl.BlockSpec((None, tk, tn), lambda j, i, kk: (j // cpt, kk, j % cpt))
    elif tk > r:
        w_spec = pl.BlockSpec((tk // r, r, tn), lambda j, i, kk: (kk, 0, j))
    else:
        rpt = r // tk
        w_spec = pl.BlockSpec((None, tk, tn), lambda j, i, kk: (kk // rpt, kk % rpt, j))
    in_specs = [pl.BlockSpec((tm, tk), lambda j, i, kk: (i, kk)), w_spec]
    operands = [a, w]
    if res is not None:
        in_specs.append(pl.BlockSpec((tm, tn), lambda j, i, kk: (i, j)))
        operands.append(res)
    if norm_gain is not None:
        in_specs.append(pl.BlockSpec((1, tn), lambda j, i, kk: (0, 0)))
        operands.append(norm_gain.reshape(1, n))
    return _mm_call(name, (n // tn, m // tm, nk), operands, in_specs, jax.ShapeDtypeStruct((m, n), out_dtype),
                    pl.BlockSpec((tm, tn), lambda j, i, kk: (i, j)), (((1,), (0,)), ((), ())), nk, scale,
                    res is not None, (tm, tn), exchange, normed=norm_gain is not None)


def mm_nt(name, a, w, kind, out_dtype, scale=1.0, exchange=None, a_hi=None, norm_back=None):
    m, kc = a.shape
    if a_hi is not None:
        assert a_hi.shape == a.shape
        kc = 2 * kc
    p, r, c = w.shape
    n = r if kind == 'col' else p * r
    assert kc == (p * c if kind == 'col' else c), (name, a.shape, w.shape)
    n_tiles = _divisors(r, LANE, 2816)
    k_tiles = _divisors(c, LANE, 4096)
    if kind == 'row':
        n_tiles = n_tiles + [q * r for q in (2, 4) if p % q == 0 and q * r <= 2816]
    o_item = jnp.dtype(out_dtype).itemsize
    if norm_back is not None:
        assert a_hi is None and n in n_tiles
        n_tiles, o_item = [n], 4 + 2 + 4
    tm, tn, tk = _plan_mm(m, n_tiles, k_tiles, n, kc, a.dtype.itemsize, o_item, norm_back is not None,
                          1 if a_hi is None else 2)
    nk = kc // tk
    if kind == 'col':
        cpt = c // tk
        w_spec = pl.BlockSpec((None, tn, tk), lambda j, i, kk: (kk // cpt, j, kk % cpt))
    elif tn > r:
        w_spec = pl.BlockSpec((tn // r, r, tk), lambda j, i, kk: (j, 0, kk))
    else:
        rpt = r // tn
        w_spec = pl.BlockSpec((None, tn, tk), lambda j, i, kk: (j // rpt, j % rpt, kk))
    split = None
    a_spec = pl.BlockSpec((tm, tk), lambda j, i, kk: (i, kk))
    if a_hi is not None:
        half = nk // 2
        assert nk % 2 == 0
        a_spec = pl.BlockSpec((tm, tk), lambda j, i, kk: (i, jnp.minimum(kk, half - 1)))
        split = Split(0, a_hi, pl.BlockSpec((tm, tk), lambda j, i, kk: (i, jnp.maximum(kk - half, 0))),
                      lambda j, i, kk: kk >= half)
    operands, in_specs = [a, w], [a_spec, w_spec]
    if norm_back is not None:
        x, gain, dres = norm_back
        rows = pl.BlockSpec((tm, n), lambda j, i, kk: (i, 0))
        operands += [x, gain.reshape(1, n), dres]
        in_specs += [rows, pl.BlockSpec((1, n), lambda j, i, kk: (0, 0)), rows]
    return _mm_call(name, (n // tn, m // tm, nk), operands, in_specs, jax.ShapeDtypeStruct((m, n), out_dtype),
                    pl.BlockSpec((tm, tn), lambda j, i, kk: (i, j)), (((1,), (1,)), ((), ())), nk, scale, False,
                    (tm, tn), exchange, split, norm_back=norm_back is not None)


def _plan_tn(s, ka, nd, r_tiles, n_tiles):
    best, best_cost = None, None
    for ts in _divisors(s, 16, 2048):
        for tr in r_tiles:
            for tn in n_tiles:
                ni, nj, ns = ka // tr, nd // tn, s // ts
                vmem = 2 * (ts * tr * 2 + ts * tn * 2 + tr * tn * 2) + tr * tn * 4 * (2 if ns > 1 else 1)
                if vmem > VMEM_PLAN_BYTES:
                    continue
                traffic = nj * s * ka * 2 + ni * s * nd * 2 + ka * nd * 2
                cost = _tile_time(2 * s * ka * nd, _mxu_fill(ts) * _mxu_fill(tn), traffic, ni * nj * ns,
                                  tr * tn // 1024 if ns > 1 else 0)
                if best_cost is None or cost < best_cost:
                    best, best_cost = (ts, tr, tn), cost
    assert best is not None, (s, ka, nd)
    return best


def mm_tn(name, a, dy, kind, scale=1.0, exchange=None, panels=(0, N_CHIPS), into=None):
    s, ka = a.shape
    s2, nd = dy.shape
    assert s == s2
    p = N_CHIPS
    first_panel, n_panels = panels
    assert kind == 'col' or panels == (0, p)
    r, c = (ka, nd // n_panels) if kind == 'col' else (ka // p, nd)
    ts, tr, tn = _plan_tn(s, ka, nd, _divisors(r, LANE, 2048), _divisors(c, LANE, 2816))
    ns = s // ts
    if kind == 'col':
        cpt = c // tn
        o_spec = pl.BlockSpec((None, tr, tn), lambda j, i, kk: (first_panel + j // cpt, i, j % cpt))
    else:
        rpt = r // tr
        o_spec = pl.BlockSpec((None, tr, tn), lambda j, i, kk: (i // rpt, i % rpt, j))
    in_specs = [pl.BlockSpec((ts, tr), lambda j, i, kk: (kk, i)), pl.BlockSpec((ts, tn), lambda j, i, kk: (kk, j))]
    return _mm_call(name, (nd // tn, ka // tr, ns), [a, dy], in_specs, jax.ShapeDtypeStruct((p, r, c), BF16), o_spec,
                    (((0,), (0,)), ((), ())), ns, scale, False, (tr, tn), exchange, None, into)


def _plan_fused(m, k, f, tiles, n_w, n_io):
    best, best_cost = None, None
    for tm in _divisors(m, 16, 1024):
        for tn in tiles:
            vmem = 2 * (tm * k * 2 + n_io * tm * tn * 2) + n_w * k * tn * 2 + 4 * tm * tn * 4
            if vmem > VMEM_PLAN_BYTES:
                continue
            traffic = (f // tn) * m * k * 2 + n_w * k * f * 2 + n_io * m * f * 2
            cost = _tile_time(2 * m * k * f * n_w, _mxu_fill(tn), traffic, (f // tn) * (m // tm), 0)
            if best_cost is None or cost < best_cost:
                best, best_cost = (tm, tn), cost
    assert best is not None, (m, k, f)
    return best


def mm_swiglu(name, h, w13, exchange=None):
    m, k = h.shape
    p, r, c = w13.shape
    assert r == k and p % 2 == 0
    f = p * c // 2
    tm, tn = _plan_fused(m, k, f, _divisors(c, LANE, 2816), 2, 3)
    cpt = c // tn

    def compute(ins, outs, scr):
        a = ins[0][...]
        g = jnp.dot(a, ins[1][...], preferred_element_type=F32)
        u = jnp.dot(a, ins[2][...], preferred_element_type=F32)
        sg = _sigmoid(g)
        silu = g * sg
        outs[0][...] = (u * (sg * (1.0 + g * (1.0 - sg)))).astype(BF16)
        outs[1][...] = silu.astype(BF16)
        outs[2][...] = (silu * u).astype(BF16)

    tile = pl.BlockSpec((tm, tn), lambda j, i: (i, j))
    in_specs = [pl.BlockSpec((tm, k), lambda j, i: (i, 0)),
                pl.BlockSpec((None, k, tn), lambda j, i: (j // cpt, 0, j % cpt), pipeline_mode=pl.Buffered(1)),
                pl.BlockSpec((None, k, tn), lambda j, i: (j // cpt + p // 2, 0, j % cpt),
                             pipeline_mode=pl.Buffered(1))]
    shape = jax.ShapeDtypeStruct((m, f), BF16)
    got = _tiled_call(name, (f // tn, m // tm), [h, w13, w13], in_specs, [shape] * 3, [tile] * 3, [], compute, exchange)
    return got


def mm_dswiglu(name, dy, w2, by_gate, by_up, scale, exchange=None):
    m, k = dy.shape
    p, r, c = w2.shape
    assert c == k
    f = p * r
    tiles = _divisors(r, LANE, 2816) + [q * r for q in (2, 4) if p % q == 0 and q * r <= 2816]
    tm, tn = _plan_fused(m, k, f, tiles, 1, 4)

    def compute(ins, outs, scr):
        b = ins[1][...]
        if b.ndim == 3:
            b = b.reshape(b.shape[0] * b.shape[1], b.shape[2])
        d = lax.dot_general(ins[0][...], b, (((1,), (1,)), ((), ())), preferred_element_type=F32) * scale
        outs[0][...] = (d * ins[2][...].astype(F32)).astype(BF16)
        outs[1][...] = (d * ins[3][...].astype(F32)).astype(BF16)

    tile = pl.BlockSpec((tm, tn), lambda j, i: (i, j))
    if tn > r:
        w_spec = pl.BlockSpec((tn // r, r, k), lambda j, i: (j, 0, 0), pipeline_mode=pl.Buffered(1))
    else:
        rpt = r // tn
        w_spec = pl.BlockSpec((None, tn, k), lambda j, i: (j // rpt, j % rpt, 0), pipeline_mode=pl.Buffered(1))
    in_specs = [pl.BlockSpec((tm, k), lambda j, i: (i, 0)), w_spec, tile, tile]
    shape = jax.ShapeDtypeStruct((m, f), BF16)
    return _tiled_call(name, (f // tn, m // tm), [dy, w2, by_gate, by_up], in_specs, [shape] * 2, [tile] * 2, [], compute,
                       exchange)


def _rms_rows(x, g):
    r = lax.rsqrt(jnp.mean(x * x, axis=-1, keepdims=True) + RMS_EPS)
    xhat = x * r
    return xhat, r, xhat * g


def rms_fwd(name, x, g, exchange=None):
    s, d = x.shape
    tm = _row_tile(s, d * 4)

    def compute(ins, outs, scr):
        outs[0][...] = _rms_rows(ins[0][...], ins[1][...])[2].astype(BF16)

    got = _tiled_call(name, (s // tm,), [x, g.reshape(1, d)],
                      [pl.BlockSpec((tm, d), lambda i: (i, 0)), pl.BlockSpec((1, d), lambda i: (0, 0))],
                      [jax.ShapeDtypeStruct((s, d), BF16)], [pl.BlockSpec((tm, d), lambda i: (i, 0))], [], compute,
                      exchange)
    return (got[0][0], got[1]) if exchange else got[0]


def _rms_bwd_rows(x, g, dh):
    xhat, r, _ = _rms_rows(x, g)
    u = dh * g
    dx = r * (u - xhat * jnp.mean(u * xhat, axis=-1, keepdims=True))
    return dx, jnp.sum(dh * xhat, axis=0, keepdims=True)


def rms_bwd(name, x, g, dh, dres):
    s, d = x.shape
    tm = _row_tile(s, d * 4, 2 * 2**20)
    has_res = dres is not None

    def body(*refs):
        x_ref, g_ref, dh_ref = refs[:3]
        dres_ref = refs[3] if has_res else None
        dx_ref, dxb_ref, dg_ref = refs[-3:]
        dx, dg = _rms_bwd_rows(x_ref[...], g_ref[...], dh_ref[...].astype(F32))
        if has_res:
            dx = dx + dres_ref[...]
        dx_ref[...] = dx
        dxb_ref[...] = dx.astype(BF16)

        @pl.when(pl.program_id(0) == 0)
        def _():
            dg_ref[...] = dg

        @pl.when(pl.program_id(0) > 0)
        def _():
            dg_ref[...] += dg

    row = pl.BlockSpec((tm, d), lambda i: (i, 0))
    vec = pl.BlockSpec((1, d), lambda i: (0, 0))
    return pl.pallas_call(
        body, name=name, grid=(s // tm,),
        in_specs=[row, vec, row] + ([row] if has_res else []),
        out_specs=[row, row, vec],
        out_shape=[jax.ShapeDtypeStruct((s, d), F32), jax.ShapeDtypeStruct((s, d), BF16),
                   jax.ShapeDtypeStruct((1, d), F32)],
        compiler_params=_params(dimension_semantics=("arbitrary",)),
    )(x, g.reshape(1, d), dh, *([dres] if has_res else []))


def loss_head(name, x, g, target):
    s, d = x.shape
    tm = _row_tile(s, d * 4, 2 * 2**20)

    def body(x_ref, g_ref, t_ref, dx_ref, dxb_ref, dg_ref, loss_ref):
        x = x_ref[...]
        gain = g_ref[...]
        y = _rms_rows(x, gain)[2]
        diff = y - t_ref[...]
        dx, dg = _rms_bwd_rows(x, gain, diff * (1.0 / d))
        dx_ref[...] = dx
        dxb_ref[...] = dx.astype(BF16)
        sq = jnp.sum(diff * diff, axis=0, keepdims=True)

        @pl.when(pl.program_id(0) == 0)
        def _():
            dg_ref[...] = dg
            loss_ref[...] = sq

        @pl.when(pl.program_id(0) > 0)
        def _():
            dg_ref[...] += dg
            loss_ref[...] += sq

    row = pl.BlockSpec((tm, d), lambda i: (i, 0))
    vec = pl.BlockSpec((1, d), lambda i: (0, 0))
    return pl.pallas_call(
        body, name=name, grid=(s // tm,), in_specs=[row, vec, row], out_specs=[row, row, vec, vec],
        out_shape=[jax.ShapeDtypeStruct((s, d), F32), jax.ShapeDtypeStruct((s, d), BF16),
                   jax.ShapeDtypeStruct((1, d), F32), jax.ShapeDtypeStruct((1, d), F32)],
        compiler_params=_params(dimension_semantics=("arbitrary",)),
    )(x, g.reshape(1, d), target)


def _sigmoid(x):
    return 0.5 * jnp.tanh(0.5 * x) + 0.5


_INV_SQRT2 = 0.7071067811865476
_INV_SQRT_2PI = 0.3989422804014327


def _normal_cdf(z):
    return 0.5 * (1.0 + lax.erf(z * _INV_SQRT2))


def _gelu_grad(z, cdf):
    return cdf + z * (_INV_SQRT_2PI * jnp.exp(-0.5 * z * z))


def _causal_weights(ws_ref, g):
    t = ws_ref.shape[-1]
    keep = lax.broadcasted_iota(jnp.int32, (t, t), 0) >= lax.broadcasted_iota(jnp.int32, (t, t), 1)
    return jnp.where(keep, ws_ref[g], 0.0).astype(BF16), keep


def _gmlp_gate_rows(z_ref, lg_ref, lb_ref, e):
    z = z_ref[...].astype(F32)
    cdf = _normal_cdf(z)
    gz = z * cdf
    u, v = gz[:, :e], gz[:, e:]
    mu = jnp.mean(v, axis=-1, keepdims=True)
    xc = v - mu
    rs = lax.rsqrt(jnp.mean(xc * xc, axis=-1, keepdims=True) + LN_EPS)
    vhat = xc * rs
    return (z, cdf), u, vhat, rs, vhat * lg_ref[...] + lb_ref[...]


def gmlp_fwd(name, z, ln_g, ln_b, w_s, bias):
    s, e2 = z.shape
    e = e2 // 2
    eg = e // GMLP_GROUPS

    def body(z_ref, lg_ref, lb_ref, ws_ref, b_ref, o_ref):
        _, u, _, _, vln = _gmlp_gate_rows(z_ref, lg_ref, lb_ref, e)
        vb = vln.astype(BF16)
        for g in range(GMLP_GROUPS):
            cols = slice(g * eg, (g + 1) * eg)
            wm, _ = _causal_weights(ws_ref, g)
            f = jnp.dot(wm, vb[:, cols], preferred_element_type=F32) + b_ref[:, cols]
            o_ref[:, cols] = (u[:, cols] * f).astype(BF16)

    full = lambda shape: pl.BlockSpec(shape, lambda i: (0,) * len(shape))
    return pl.pallas_call(
        body, name=name, grid=(s // CHUNK,),
        in_specs=[pl.BlockSpec((CHUNK, e2), lambda i: (i, 0)), full((1, e)), full((1, e)),
                  full((GMLP_GROUPS, CHUNK, CHUNK)), full((CHUNK, e))],
        out_specs=pl.BlockSpec((CHUNK, e), lambda i: (i, 0)), out_shape=jax.ShapeDtypeStruct((s, e), BF16),
        compiler_params=_params(dimension_semantics=("arbitrary",)),
    )(z, ln_g.reshape(1, e), ln_b.reshape(1, e), w_s, bias)


def gmlp_bwd(name, z, dp, ln_g, ln_b, w_s, bias):
    s, e2 = z.shape
    e = e2 // 2
    eg = e // GMLP_GROUPS
    t = CHUNK

    def body(z_ref, dp_ref, lg_ref, lb_ref, ws_ref, b_ref, dz_ref, dws_ref, dbs_ref, dlg_ref, dlb_ref):
        first = pl.program_id(0) == 0
        (zf, cdf), u, vhat, rs, vln = _gmlp_gate_rows(z_ref, lg_ref, lb_ref, e)
        vb = vln.astype(BF16)
        dp = dp_ref[...].astype(F32)
        lane = lax.broadcasted_iota(jnp.int32, (t, LANE), 1)
        dbs = jnp.zeros((t, LANE), F32)
        dvln_parts = []
        for g in range(GMLP_GROUPS):
            cols = slice(g * eg, (g + 1) * eg)
            wm, keep = _causal_weights(ws_ref, g)
            f = jnp.dot(wm, vb[:, cols], preferred_element_type=F32) + b_ref[:, cols]
            dz_ref[:, cols] = (dp[:, cols] * f * _gelu_grad(zf[:, cols], cdf[:, cols])).astype(BF16)
            df = dp[:, cols] * u[:, cols]
            dfb = df.astype(BF16)
            dbs = dbs + jnp.where(lane == g, jnp.sum(df, axis=-1, keepdims=True), 0.0)
            dw = lax.dot_general(dfb, vb[:, cols], (((1,), (1,)), ((), ())), preferred_element_type=F32)
            dw = jnp.where(keep, dw, 0.0)

            @pl.when(first)
            def _():
                dws_ref[g] = dw

            @pl.when(jnp.logical_not(first))
            def _():
                dws_ref[g] += dw

            dvln_parts.append(lax.dot_general(wm, dfb, (((0,), (0,)), ((), ())), preferred_element_type=F32))
        dvln = jnp.concatenate(dvln_parts, axis=-1)
        dvhat = dvln * lg_ref[...]
        dv = rs * (dvhat - jnp.mean(dvhat, axis=-1, keepdims=True)
                   - vhat * jnp.mean(dvhat * vhat, axis=-1, keepdims=True))
        dz_ref[:, e:] = (dv * _gelu_grad(zf[:, e:], cdf[:, e:])).astype(BF16)
        dlg = jnp.sum(dvln * vhat, axis=0, keepdims=True)
        dlb = jnp.sum(dvln, axis=0, keepdims=True)

        @pl.when(first)
        def _():
            dbs_ref[...] = dbs
            dlg_ref[...] = dlg
            dlb_ref[...] = dlb

        @pl.when(jnp.logical_not(first))
        def _():
            dbs_ref[...] += dbs
            dlg_ref[...] += dlg
            dlb_ref[...] += dlb

    full = lambda shape: pl.BlockSpec(shape, lambda i: (0,) * len(shape))
    return pl.pallas_call(
        body, name=name, grid=(s // t,),
        in_specs=[pl.BlockSpec((t, e2), lambda i: (i, 0)), pl.BlockSpec((t, e), lambda i: (i, 0)), full((1, e)),
                  full((1, e)), full((GMLP_GROUPS, t, t)), full((t, e))],
        out_specs=[pl.BlockSpec((t, e2), lambda i: (i, 0)), full((GMLP_GROUPS, t, t)), full((t, LANE)), full((1, e)),
                   full((1, e))],
        out_shape=[jax.ShapeDtypeStruct((s, e2), BF16), jax.ShapeDtypeStruct((GMLP_GROUPS, t, t), F32),
                   jax.ShapeDtypeStruct((t, LANE), F32), jax.ShapeDtypeStruct((1, e), F32),
                   jax.ShapeDtypeStruct((1, e), F32)],
        compiler_params=_params(dimension_semantics=("arbitrary",)),
    )(z, dp, ln_g.reshape(1, e), ln_b.reshape(1, e), w_s, bias)


EDGE = 16


def _shift_down(zc, prev, k):
    tm = zc.shape[0]
    row = lax.broadcasted_iota(jnp.int32, (tm, 1), 0)
    out = pltpu.roll(zc, k, 0)
    for j in range(k):
        out = jnp.where(row == j, prev[EDGE - k + j:EDGE - k + j + 1, :], out)
    return out


def _shift_up(dc, nxt, k):
    tm = dc.shape[0]
    row = lax.broadcasted_iota(jnp.int32, (tm, 1), 0)
    out = pltpu.roll(dc, tm - k, 0)
    for j in range(k):
        out = jnp.where(row == tm - k + j, nxt[j:j + 1, :], out)
    return out


def conv_fwd(name, bcv, cw):
    s, d3 = bcv.shape
    d = d3 // 3
    tm = _row_tile(s, d * 4, 2 * 2**20)
    per = tm // EDGE

    def body(b_ref, c_ref, v_ref, cp_ref, vp_ref, w_ref, o_ref):
        i = pl.program_id(0)
        zc = c_ref[...].astype(F32) * v_ref[...].astype(F32)
        prev = jnp.where(i > 0, cp_ref[...].astype(F32) * vp_ref[...].astype(F32), 0.0)
        conv = w_ref[2:3, :] * zc + w_ref[1:2, :] * _shift_down(zc, prev, 1) + w_ref[0:1, :] * _shift_down(zc, prev, 2)
        o_ref[...] = (b_ref[...].astype(F32) * conv).astype(BF16)

    blk = lambda col: pl.BlockSpec((tm, d), lambda i: (i, col))
    edge = lambda col: pl.BlockSpec((EDGE, d), lambda i: (jnp.maximum(i * per - 1, 0), col))
    return pl.pallas_call(
        body, name=name, grid=(s // tm,),
        in_specs=[blk(0), blk(1), blk(2), edge(1), edge(2), pl.BlockSpec((3, d), lambda i: (0, 0))],
        out_specs=pl.BlockSpec((tm, d), lambda i: (i, 0)), out_shape=jax.ShapeDtypeStruct((s, d), BF16),
        compiler_params=_params(dimension_semantics=("arbitrary",)),
    )(bcv, bcv, bcv, bcv, bcv, cw)


def conv_bwd(name, bcv, dq, cw):
    s, d3 = bcv.shape
    d = d3 // 3
    tm = _row_tile(s, d * 4, 2**20)
    per = tm // EDGE
    n_tiles = s // tm
    last_edge = s // EDGE - 1

    def body(b_ref, c_ref, v_ref, cp_ref, vp_ref, bn_ref, dq_ref, dqn_ref, w_ref, o_ref, dw_ref):
        i = pl.program_id(0)
        b = b_ref[...].astype(F32)
        c = c_ref[...].astype(F32)
        v = v_ref[...].astype(F32)
        dq = dq_ref[...].astype(F32)
        zc = c * v
        prev = jnp.where(i > 0, cp_ref[...].astype(F32) * vp_ref[...].astype(F32), 0.0)
        z1 = _shift_down(zc, prev, 1)
        z2 = _shift_down(zc, prev, 2)
        w0, w1, w2 = w_ref[0:1, :], w_ref[1:2, :], w_ref[2:3, :]
        conv = w2 * zc + w1 * z1 + w0 * z2
        dconv = dq * b
        nxt = jnp.where(i < n_tiles - 1, dqn_ref[...].astype(F32) * bn_ref[...].astype(F32), 0.0)
        dz = w2 * dconv + w1 * _shift_up(dconv, nxt, 1) + w0 * _shift_up(dconv, nxt, 2)
        o_ref[:, :d] = (dq * conv).astype(BF16)
        o_ref[:, d:2 * d] = (dz * v).astype(BF16)
        o_ref[:, 2 * d:] = (dz * c).astype(BF16)
        dw = jnp.concatenate([jnp.sum(dconv * z2, axis=0, keepdims=True), jnp.sum(dconv * z1, axis=0, keepdims=True),
                              jnp.sum(dconv * zc, axis=0, keepdims=True), jnp.zeros((5, d), F32)], axis=0)

        @pl.when(i == 0)
        def _():
            dw_ref[...] = dw

        @pl.when(i > 0)
        def _():
            dw_ref[...] += dw

    blk = lambda col: pl.BlockSpec((tm, d), lambda i: (i, col))
    before = lambda col: pl.BlockSpec((EDGE, d), lambda i: (jnp.maximum(i * per - 1, 0), col))
    after = lambda col: pl.BlockSpec((EDGE, d), lambda i: (jnp.minimum((i + 1) * per, last_edge), col))
    return pl.pallas_call(
        body, name=name, grid=(n_tiles,),
        in_specs=[blk(0), blk(1), blk(2), before(1), before(2), after(0), blk(0), after(0),
                  pl.BlockSpec((3, d), lambda i: (0, 0))],
        out_specs=[pl.BlockSpec((tm, d3), lambda i: (i, 0)), pl.BlockSpec((8, d), lambda i: (0, 0))],
        out_shape=[jax.ShapeDtypeStruct((s, d3), BF16), jax.ShapeDtypeStruct((8, d), F32)],
        compiler_params=_params(dimension_semantics=("arbitrary",)),
    )(bcv, bcv, bcv, bcv, bcv, bcv, dq, dq, cw)


def _attn_probs(qh, kh, scale):
    sc = lax.dot_general(qh, kh, (((1,), (1,)), ((), ())), preferred_element_type=F32) * scale
    ex = jnp.exp(sc - jnp.max(sc, axis=-1, keepdims=True))
    return ex / jnp.sum(ex, axis=-1, keepdims=True)


def attn_fwd(name, q, kv):
    s, d = q.shape
    mlen = kv.shape[0]
    dh = d // XATTN_HEADS
    scale = dh ** -0.5
    tm = _row_tile(s, d * 4, 4 * 2**20)

    def body(q_ref, kv_ref, o_ref):
        for h in range(XATTN_HEADS):
            cols = slice(h * dh, (h + 1) * dh)
            p = _attn_probs(q_ref[:, cols], kv_ref[:, cols], scale)
            o_ref[:, cols] = jnp.dot(p.astype(BF16), kv_ref[:, d + h * dh:d + (h + 1) * dh],
                                     preferred_element_type=F32).astype(BF16)

    return pl.pallas_call(
        body, name=name, grid=(s // tm,),
        in_specs=[pl.BlockSpec((tm, d), lambda i: (i, 0)), pl.BlockSpec((mlen, 2 * d), lambda i: (0, 0))],
        out_specs=pl.BlockSpec((tm, d), lambda i: (i, 0)), out_shape=jax.ShapeDtypeStruct((s, d), BF16),
        compiler_params=_params(dimension_semantics=("arbitrary",)),
    )(q, kv)


def attn_bwd(name, q, kv, do):
    s, d = q.shape
    mlen = kv.shape[0]
    dh = d // XATTN_HEADS
    scale = dh ** -0.5
    tm = _row_tile(s, d * 4, 4 * 2**20)

    def body(q_ref, kv_ref, do_ref, dq_ref, dkv_ref):
        first = pl.program_id(0) == 0
        for h in range(XATTN_HEADS):
            cols = slice(h * dh, (h + 1) * dh)
            vcols = slice(d + h * dh, d + (h + 1) * dh)
            qh, kh, vh, doh = q_ref[:, cols], kv_ref[:, cols], kv_ref[:, vcols], do_ref[:, cols]
            p = _attn_probs(qh, kh, scale)
            dp = lax.dot_general(doh, vh, (((1,), (1,)), ((), ())), preferred_element_type=F32)
            ds = (p * (dp - jnp.sum(dp * p, axis=-1, keepdims=True)) * scale).astype(BF16)
            dq_ref[:, cols] = jnp.dot(ds, kh, preferred_element_type=F32).astype(BF16)
            dk = lax.dot_general(ds, qh, (((0,), (0,)), ((), ())), preferred_element_type=F32)
            dv = lax.dot_general(p.astype(BF16), doh, (((0,), (0,)), ((), ())), preferred_element_type=F32)

            @pl.when(first)
            def _():
                dkv_ref[:, cols] = dk
                dkv_ref[:, vcols] = dv

            @pl.when(jnp.logical_not(first))
            def _():
                dkv_ref[:, cols] += dk
                dkv_ref[:, vcols] += dv

    row = pl.BlockSpec((tm, d), lambda i: (i, 0))
    whole = pl.BlockSpec((mlen, 2 * d), lambda i: (0, 0))
    return pl.pallas_call(
        body, name=name, grid=(s // tm,), in_specs=[row, whole, row], out_specs=[row, whole],
        out_shape=[jax.ShapeDtypeStruct((s, d), BF16), jax.ShapeDtypeStruct((mlen, 2 * d), F32)],
        compiler_params=_params(dimension_semantics=("arbitrary",)),
    )(q, kv, do)


def _as_rows(a):
    if a.ndim >= 2 and a.shape[-1] % LANE == 0:
        return a.reshape(-1, a.shape[-1])
    return a.reshape(-1, LANE) if a.size % LANE == 0 else a.reshape(1, -1)


def add_halves(name, dw, other, core):
    p, r, c = dw.shape
    h = r // 2
    th = _row_tile(h, c * 2, 4 * 2**20)

    def body(core_ref, a_ref, b_ref, o_ref):
        o_ref[...] = (a_ref[...].astype(F32) + b_ref[...].astype(F32)).astype(BF16)

    grid_spec = pltpu.PrefetchScalarGridSpec(
        num_scalar_prefetch=1, grid=(p, h // th),
        in_specs=[pl.BlockSpec((None, None, th, c), lambda pi, i, core_ref: (pi, core_ref[0], i, 0)),
                  pl.BlockSpec((None, th, c), lambda pi, i, core_ref: (pi, i, 0))],
        out_specs=pl.BlockSpec((None, th, c), lambda pi, i, core_ref: (pi, i, 0)))
    return pl.pallas_call(
        body, name=name, grid_spec=grid_spec, out_shape=jax.ShapeDtypeStruct((p, h, c), BF16),
        compiler_params=_params(dimension_semantics=("arbitrary", "arbitrary")),
    )(core, dw.reshape(p, 2, h, c), other)


def sum_leading(name, parts):
    n, r, c = parts.shape
    tr = _row_tile(r, c * 4 * 2, 2 * 2**20)

    def body(p_ref, o_ref):
        acc = p_ref[0].astype(F32)
        for k in range(1, n):
            acc = acc + p_ref[k].astype(F32)
        o_ref[...] = acc

    return pl.pallas_call(
        body, name=name, grid=(r // tr,), in_specs=[pl.BlockSpec((n, tr, c), lambda i: (0, i, 0))],
        out_specs=pl.BlockSpec((tr, c), lambda i: (i, 0)), out_shape=jax.ShapeDtypeStruct((r, c), F32),
        compiler_params=_params(dimension_semantics=("arbitrary",)),
    )(parts)


def _adamw_rows(w, g, m, v):
    m = ADAM_B1 * m + (1.0 - ADAM_B1) * g
    v = ADAM_B2 * v + (1.0 - ADAM_B2) * (g * g)
    m_hat = m / (1.0 - ADAM_B1 ** ADAM_STEP)
    v_hat = v / (1.0 - ADAM_B2 ** ADAM_STEP)
    delta = -ADAM_LR * (m_hat / (jnp.sqrt(v_hat) + ADAM_EPS) + ADAM_WD * w)
    return delta, m, v


def adamw_layer(name, w, m, v, g, layer, carried):
    nl, r, c = w.shape
    tr = _row_tile(r, c * 4, 3 * 2**19)
    n_carried = 4 if carried is not None else 0

    def body(*refs):
        w_ref, m_ref, v_ref, g_ref = refs[:4]
        go_ref, d_ref, mo_ref, vo_ref = refs[4 + n_carried:]
        g = g_ref[...]
        delta, m_new, v_new = _adamw_rows(w_ref[...], g, m_ref[...], v_ref[...])
        go_ref[...] = g
        d_ref[...] = delta
        mo_ref[...] = m_new
        vo_ref[...] = v_new

    stacked = pl.BlockSpec((None, tr, c), lambda i: (layer, i, 0))
    in_specs = [stacked, stacked, stacked, pl.BlockSpec((tr, c), lambda i: (i, 0))]
    in_specs += [pl.BlockSpec(memory_space=pl.ANY)] * n_carried
    shape = jax.ShapeDtypeStruct((nl, r, c), F32)
    return pl.pallas_call(
        body, name=name, grid=(r // tr,), in_specs=in_specs, out_specs=[stacked] * 4, out_shape=[shape] * 4,
        input_output_aliases={4 + k: k for k in range(n_carried)},
        compiler_params=_params(dimension_semantics=("arbitrary",)),
    )(w, m, v, g, *(carried or ()))


def adamw_flat(name, w, m, v, g):
    r, c = w.shape

    def body(w_ref, m_ref, v_ref, g_ref, d_ref, mo_ref, vo_ref):
        delta, m_new, v_new = _adamw_rows(w_ref[...], g_ref[...], m_ref[...], v_ref[...])
        d_ref[...] = delta
        mo_ref[...] = m_new
        vo_ref[...] = v_new

    shape = jax.ShapeDtypeStruct((r, c), F32)
    return pl.pallas_call(body, name=name, out_shape=[shape] * 3, compiler_params=_params())(w, m, v, g)


def cast_place(name, w, layer, place):
    nl, r, c = w.shape
    tr = _row_tile(r, c * 4, 8 * 2**20)

    def body(x_ref, y_ref, c_ref, w_ref, o_ref):
        o_ref[...] = w_ref[...].astype(BF16)

    grid_spec = pltpu.PrefetchScalarGridSpec(
        num_scalar_prefetch=3, grid=(r // tr,),
        in_specs=[pl.BlockSpec((None, tr, c), lambda i, x_ref, y_ref, c_ref: (layer, i, 0))],
        out_specs=pl.BlockSpec((None, tr, c), lambda i, x_ref, y_ref, c_ref: (2 * x_ref[0] + y_ref[0], i, 0)))
    return pl.pallas_call(
        body, name=name, grid_spec=grid_spec, out_shape=jax.ShapeDtypeStruct((N_CHIPS, r, c), BF16),
        compiler_params=_params(dimension_semantics=("arbitrary",)),
    )(*place, w)


def reduce_sum4(name, own, landed, place):
    p, h, c = own.shape
    tr = _row_tile(h, c * 4, 4 * 2**20)

    def body(x_ref, y_ref, c_ref, t_ref, y1_ref, y2_ref, y3_ref, o_ref):
        acc = t_ref[...].astype(F32)
        for part_ref in (y1_ref, y2_ref, y3_ref):
            acc = acc + part_ref[...].astype(F32)
        o_ref[...] = acc

    def panel(fx, fy):
        return pl.BlockSpec((None, tr, c), lambda i, x_ref, y_ref, c_ref: (
            2 * (1 - x_ref[0] if fx else x_ref[0]) + (1 - y_ref[0] if fy else y_ref[0]), i, 0))

    grid_spec = pltpu.PrefetchScalarGridSpec(
        num_scalar_prefetch=3, grid=(h // tr,),
        in_specs=[panel(0, 0), panel(1, 0), panel(0, 1), panel(1, 1)],
        out_specs=pl.BlockSpec((None, tr, c), lambda i, x_ref, y_ref, c_ref: (c_ref[0], i, 0)))
    return pl.pallas_call(
        body, name=name, grid_spec=grid_spec, out_shape=jax.ShapeDtypeStruct((2, h, c), F32),
        compiler_params=_params(dimension_semantics=("arbitrary",)),
    )(*place, own, landed, landed, landed)


def run_exchange(name, exchange):
    n_in, n_out = len(exchange.inputs), len(exchange.out_shapes)

    def body(*refs):
        ins, outs, sems = refs[:n_in], refs[n_in:n_in + n_out], refs[n_in + n_out:]
        exchange.start(ins, outs, sems)
        exchange.finish(ins, outs, sems)

    return pl.pallas_call(
        body, name=name, in_specs=[ANY] * n_in, out_specs=[ANY] * n_out, out_shape=list(exchange.out_shapes),
        scratch_shapes=[pltpu.SemaphoreType.DMA((exchange.n_sems,)), pltpu.SemaphoreType.DMA((exchange.n_sems,))],
        input_output_aliases=dict(exchange.aliases),
    )(*exchange.inputs)


def _row_halves(ref, c):
    h = ref.shape[1] // 2
    return pl.ds(pl.multiple_of(c * h, 16), h), pl.ds(pl.multiple_of((1 - c) * h, 16), h)


def _in_place(arrays, n_sems, start, finish):
    return Exchange(list(arrays), [jax.ShapeDtypeStruct(f.shape, f.dtype) for f in arrays],
                    {a: a for a in range(len(arrays))}, n_sems, start, finish)


def gather_over_ici(fulls):
    n = len(fulls)

    def sends(outs, sems):
        x, y, c, mine, chips = _place()
        return [_remote(rows, rows, sems, 3 * a + j, (*chip, c)) for a in range(n)
                for rows in [outs[a].at[mine, _row_halves(outs[a], c)[0]]] for j, chip in enumerate(chips)]

    def start(ins, outs, sems):
        for cp in sends(outs, sems):
            cp.start()

    def finish(ins, outs, sems):
        x, y, c, mine, chips = _place()
        for a in range(n):
            for j, chip in enumerate(chips):
                rows = outs[a].at[2 * chip[0] + chip[1], _row_halves(outs[a], c)[0]]
                _remote(rows, rows, sems, 3 * a + j, (*chip, c)).wait_recv()
        for cp in sends(outs, sems):
            cp.wait_send()

    return _in_place(fulls, 3 * n, start, finish)


def gather_over_ici_by_neighbours(fulls):
    n = len(fulls)

    def plan(outs, a):
        x, y, c, mine, (nbr_x, nbr_y, far) = _place()
        h = outs[a].shape[1] // 2
        first = pl.ds(pl.multiple_of(c * h, 16), h // 2)
        second = pl.ds(pl.multiple_of(c * h + h // 2, 16), h // 2)
        index = lambda chip: 2 * chip[0] + chip[1]
        return c, mine, nbr_x, nbr_y, index, _row_halves(outs[a], c)[0], first, second, index(far)

    def direct(outs, sems, a):
        c, mine, nbr_x, nbr_y, _, half, _, _, _ = plan(outs, a)
        rows = outs[a].at[mine, half]
        return [_remote(rows, rows, sems, 4 * a, (*nbr_x, c)), _remote(rows, rows, sems, 4 * a + 1, (*nbr_y, c))]

    def passed_on(outs, sems, a):
        c, _, nbr_x, nbr_y, index, _, first, second, _ = plan(outs, a)
        from_x, from_y = outs[a].at[index(nbr_x), first], outs[a].at[index(nbr_y), second]
        return [_remote(from_x, from_x, sems, 4 * a + 2, (*nbr_y, c)), _remote(from_y, from_y, sems, 4 * a + 3, (*nbr_x, c))]

    def start(ins, outs, sems):
        for a in range(n):
            for cp in direct(outs, sems, a):
                cp.start()

    def finish(ins, outs, sems):
        for a in range(n):
            c, _, nbr_x, nbr_y, index, half, first, second, far = plan(outs, a)
            for k, nbr in enumerate((nbr_x, nbr_y)):
                rows = outs[a].at[index(nbr), half]
                _remote(rows, rows, sems, 4 * a + k, (*nbr, c)).wait_recv()
            for cp in passed_on(outs, sems, a):
                cp.start()
        for a in range(n):
            c, _, nbr_x, nbr_y, index, half, first, second, far = plan(outs, a)
            for k, (rows, nbr) in enumerate(((outs[a].at[far, first], nbr_y), (outs[a].at[far, second], nbr_x))):
                _remote(rows, rows, sems, 4 * a + 2 + k, (*nbr, c)).wait_recv()
            for cp in direct(outs, sems, a) + passed_on(outs, sems, a):
                cp.wait_send()

    return _in_place(fulls, 4 * n, start, finish)


def gather_over_d2d(fulls):
    n = len(fulls)

    def copies(outs, sems, which):
        x, y, c, mine, chips = _place()
        return [_remote(rows, rows, sems, 3 * a + j, (x, y, 1 - c)) for a in range(n) for j, chip in enumerate(chips)
                for rows in [outs[a].at[2 * chip[0] + chip[1], _row_halves(outs[a], c)[which]]]]

    def start(ins, outs, sems):
        for cp in copies(outs, sems, 0):
            cp.start()

    def finish(ins, outs, sems):
        for cp in copies(outs, sems, 1):
            cp.wait_recv()
        for cp in copies(outs, sems, 0):
            cp.wait_send()

    return _in_place(fulls, 3 * n, start, finish)


def gather_whole(fulls):
    ici, d2d = gather_over_ici(fulls), gather_over_d2d(fulls)

    def finish(ins, outs, sems):
        ici.finish(ins, outs, sems)
        later = tuple(_SemaphoresFrom(s, ici.n_sems) for s in sems)
        d2d.start(ins, outs, later)
        d2d.finish(ins, outs, later)

    return _in_place(fulls, ici.n_sems + d2d.n_sems, ici.start, finish)


class _SemaphoresFrom:
    def __init__(self, ref, offset):
        self.ref, self.offset = ref, offset

    @property
    def at(self):
        return self

    def __getitem__(self, k):
        return self.ref.at[self.offset + k]


def combine(exchanges):
    exchanges = [e for e in exchanges if e is not None]
    if len(exchanges) <= 1:
        return exchanges[0] if exchanges else None
    inputs, out_shapes, aliases, spans, n_sems = [], [], {}, [], 0
    for e in exchanges:
        aliases.update({len(inputs) + i: len(out_shapes) + o for i, o in e.aliases.items()})
        spans.append((len(inputs), len(e.inputs), len(out_shapes), len(e.out_shapes), n_sems))
        inputs, out_shapes, n_sems = inputs + list(e.inputs), out_shapes + list(e.out_shapes), n_sems + e.n_sems

    def each(method):
        def run(ins, outs, sems):
            for e, (i0, ni, o0, no, s0) in zip(exchanges, spans, strict=True):
                getattr(e, method)(ins[i0:i0 + ni], outs[o0:o0 + no], tuple(_SemaphoresFrom(s, s0) for s in sems))
        return run

    return Exchange(inputs, out_shapes, aliases, n_sems, each("start"), each("finish"))


def swap_exchange(grads):
    n = len(grads)

    def copies(ins, outs, sems):
        x, y, c, _, _ = _place()
        return [_remote(ins[a].at[:, _row_halves(ins[a], c)[1]], outs[a], sems, a, (x, y, 1 - c)) for a in range(n)]

    def start(ins, outs, sems):
        for cp in copies(ins, outs, sems):
            cp.start()

    def finish(ins, outs, sems):
        for cp in copies(ins, outs, sems):
            cp.wait()

    shapes = [jax.ShapeDtypeStruct((g.shape[0], g.shape[1] // 2, g.shape[2]), g.dtype) for g in grads]
    return Exchange(list(grads), shapes, {}, n, start, finish)


def scatter_exchange(parts):
    n = len(parts)

    def sends(ins, outs, sems):
        x, y, c, mine, chips = _place()
        return [_remote(ins[a].at[2 * chip[0] + chip[1]], outs[a].at[mine], sems, 3 * a + j, (*chip, c))
                for a in range(n) for j, chip in enumerate(chips)]

    def start(ins, outs, sems):
        for cp in sends(ins, outs, sems):
            cp.start()

    def finish(ins, outs, sems):
        x, y, c, mine, chips = _place()
        for a in range(n):
            for j, chip in enumerate(chips):
                landing = outs[a].at[2 * chip[0] + chip[1]]
                _remote(landing, landing, sems, 3 * a + j, (*chip, c)).wait_recv()
        for cp in sends(ins, outs, sems):
            cp.wait_send()

    return Exchange(list(parts), [jax.ShapeDtypeStruct(g.shape, g.dtype) for g in parts], {}, 3 * n, start, finish)


def join_exchange(halves):
    n = len(halves)

    def start(ins, outs, sems):
        x, y, c, _, _ = _place()
        for a in range(n):
            _remote(outs[a].at[c], outs[a].at[c], sems, a, (x, y, 1 - c)).start()

    def finish(ins, outs, sems):
        x, y, c, _, _ = _place()
        for a in range(n):
            _remote(outs[a].at[1 - c], outs[a].at[1 - c], sems, a, (x, y, 1 - c)).wait_recv()
        for a in range(n):
            _remote(outs[a].at[c], outs[a].at[c], sems, a, (x, y, 1 - c)).wait_send()

    return Exchange(list(halves), [jax.ShapeDtypeStruct(g.shape, g.dtype) for g in halves], {a: a for a in range(n)},
                    n, start, finish)


def gather_all(name, rows):
    def body(in_ref, out_ref, send_sems, recv_sems, local_sem):
        sems = (send_sems, recv_sems)
        x, y, c, _, _ = _place()
        me = 4 * x + 2 * y + c
        local = pltpu.make_async_copy(in_ref, out_ref.at[me], local_sem)
        local.start()
        peers = [(1 - x if k & 4 else x, 1 - y if k & 2 else y, 1 - c if k & 1 else c) for k in range(1, N_DEV)]
        sent = []
        for k, peer in enumerate(peers):
            cp = _remote(in_ref, out_ref.at[me], sems, k, peer)
            cp.start()
            sent.append(cp)
        for k, peer in enumerate(peers):
            landing = out_ref.at[4 * peer[0] + 2 * peer[1] + peer[2]]
            _remote(landing, landing, sems, k, peer).wait_recv()
        for cp in sent:
            cp.wait_send()
        local.wait()

    return pl.pallas_call(
        body, name=name, in_specs=[ANY], out_specs=ANY,
        out_shape=jax.ShapeDtypeStruct((N_DEV,) + rows.shape, rows.dtype),
        scratch_shapes=[pltpu.SemaphoreType.DMA((N_DEV - 1,)), pltpu.SemaphoreType.DMA((N_DEV - 1,)),
                        pltpu.SemaphoreType.DMA],
    )(rows)


class _Step:
    def __init__(self, p):
        self.p = p
        xi, yi, ci = lax.axis_index("x"), lax.axis_index("y"), lax.axis_index("c")
        self.chip = 2 * xi + yi
        self.place_refs = tuple(v.astype(jnp.int32).reshape(1) for v in (xi, yi, ci))
        self.core_ref = self.place_refs[2]
        self.depth = p['ffn1_norm'].shape[0]
        self.placed, self.landed, self.w = {}, {}, {}
        self.big_g = {}
        self.waiting_joins = []
        self.waiting_scatter = None

    def block_keys(self, tag, l):
        if l >= self.depth:
            return []
        mixer = ['gmlp_w_in', 'gmlp_w_out'] if l % 2 == 0 else ['conv_w_in', 'conv_w_out']
        names = {"ffn1": ['ffn1_w13', 'ffn1_w2'], "mix": mixer, "xattn": ['xattn_wq', 'xattn_wkv', 'xattn_wo'],
                 "ffn2": ['ffn2_w13', 'ffn2_w2']}[tag]
        return [(n, l // 2 if tag == "mix" else l) for n in names]

    def place(self, keys):
        for n, idx in keys:
            self.placed[(n, idx)] = cast_place(f"place_{n}{idx}", self.p[n], idx, self.place_refs)

    def carrying_gather(self, call, over_ici, over_d2d, whole, *args, **kw):
        over_ici = [k for k in over_ici if k in self.placed]
        over_d2d = [k for k in over_d2d if k in self.landed]
        parts = [gather_over_ici([self.placed.pop(k) for k in over_ici]) if over_ici else None,
                 gather_over_d2d([self.landed.pop(k) for k in over_d2d]) if over_d2d else None,
                 gather_whole([self.placed.pop(k) for k in whole]) if whole else None]
        exchange = combine(parts)
        if exchange is None:
            return call(*args, **kw)
        out, got = call(*args, exchange=exchange, **kw)
        self.landed.update(zip(over_ici, got[:len(over_ici)], strict=True))
        self.w.update(zip(over_d2d + whole, got[len(over_ici):], strict=True))
        return out

    def reduce_begin(self, tag, keys, dws, theirs=None):
        theirs = list(theirs or [None] * len(dws))
        todo = [i for i, t in enumerate(theirs) if t is None]
        for i, t in zip(todo, run_exchange(tag + "_swap", swap_exchange([dws[i] for i in todo])), strict=True):
            theirs[i] = t
        parts = [add_halves(f"{tag}_add{i}", dw, t, self.core_ref) for i, (dw, t) in enumerate(zip(dws, theirs, strict=True))]
        assert self.waiting_scatter is None
        self.waiting_scatter = (tag, keys, parts)

    def carrying_scatter(self, mm, *args, **kw):
        tag, keys, parts = self.waiting_scatter
        self.waiting_scatter = None
        out, landed = mm(*args, exchange=scatter_exchange(parts), **kw)
        halves = [reduce_sum4(f"{tag}_sum{i}", t, y, self.place_refs) for i, (t, y) in enumerate(zip(parts, landed, strict=True))]
        self.waiting_joins += list(zip(keys, halves, strict=True))
        return out

    def take_joined(self, keys, joined):
        for k, g in zip(keys, joined, strict=True):
            self.big_g[k] = g.reshape(-1, g.shape[-1])

    def carrying_joins(self, mm, *args, **kw):
        if not self.waiting_joins:
            return mm(*args, **kw)
        keys, halves = zip(*self.waiting_joins, strict=True)
        self.waiting_joins = []
        out, joined = mm(*args, exchange=join_exchange(list(halves)), **kw)
        self.take_joined(keys, joined)
        return out

    def joins_alone(self, name):
        keys, halves = zip(*self.waiting_joins, strict=True)
        self.waiting_joins = []
        self.take_joined(keys, run_exchange(name, join_exchange(list(halves))))

    def ffn_fwd(self, tag, l, x, gain, carry_norm, carry13, carry2, h=None):
        name = f"l{l}_{tag}"
        if h is None:
            h = self.carrying_gather(rms_fwd, *carry_norm, name + "_norm", x, gain)
        by_gate, by_up, act = self.carrying_gather(mm_swiglu, *carry13, name + "_w13", h, self.w[(tag + '_w13', l)])
        out = self.carrying_gather(mm_nn, *carry2, name + "_w2", act, self.w[(tag + '_w2', l)], 'row', F32, res=x,
                                   scale=0.5)
        return out, (x, h, by_gate, by_up, act)

    def ffn_bwd(self, tag, l, dx, dxb, saved, gain):
        w13, w2 = self.w[(tag + '_w13', l)], self.w[(tag + '_w2', l)]
        name = f"l{l}_{tag}"
        x, h, by_gate, by_up, act = saved
        d_gate, d_up = self.carrying_joins(mm_dswiglu, name + "_dact", dxb, w2, by_gate, by_up, 0.5)
        d_w2 = mm_tn(name + "_dw2", act, dxb, 'row', scale=0.5)
        half = N_CHIPS // 2
        d_w13, their_w2 = mm_tn(name + "_dw13g", h, d_gate, 'col', panels=(0, half), exchange=swap_exchange([d_w2]))
        d_w13 = mm_tn(name + "_dw13u", h, d_up, 'col', panels=(half, half), into=d_w13)
        self.reduce_begin(name, [(tag + '_w13', l), (tag + '_w2', l)], [d_w13, d_w2], [None] + their_w2)
        dh = self.carrying_scatter(mm_nt, name + "_dh", d_gate, w13, 'col', BF16, a_hi=d_up)
        return rms_bwd(name + "_dnorm", x, gain, dh, dx)


def kernel(x, mem, ffn1_norm, ffn1_w13, ffn1_w2, mix_norm, gmlp_w_in, gmlp_ln_g, gmlp_ln_b, gmlp_w_s, gmlp_b_s, gmlp_w_out, conv_w_in, conv_w, conv_w_out, xattn_norm, mem_norm, xattn_wq, xattn_wkv, xattn_wo, ffn2_norm, ffn2_w13, ffn2_w2, final_norm, loss_target, m_ffn1_norm, m_ffn1_w13, m_ffn1_w2, m_mix_norm, m_gmlp_w_in, m_gmlp_ln_g, m_gmlp_ln_b, m_gmlp_w_s, m_gmlp_b_s, m_gmlp_w_out, m_conv_w_in, m_conv_w, m_conv_w_out, m_xattn_norm, m_mem_norm, m_xattn_wq, m_xattn_wkv, m_xattn_wo, m_ffn2_norm, m_ffn2_w13, m_ffn2_w2, m_final_norm, v_ffn1_norm, v_ffn1_w13, v_ffn1_w2, v_mix_norm, v_gmlp_w_in, v_gmlp_ln_g, v_gmlp_ln_b, v_gmlp_w_s, v_gmlp_b_s, v_gmlp_w_out, v_conv_w_in, v_conv_w, v_conv_w_out, v_xattn_norm, v_mem_norm, v_xattn_wq, v_xattn_wkv, v_xattn_wo, v_ffn2_norm, v_ffn2_w13, v_ffn2_w2, v_final_norm):
    return _step(dict(locals()))


def _step(p):
    assert sorted(p) == sorted(ARG_NAMES)
    st = _Step(p)
    x = p['x'][0]
    mem = p['mem'][0]
    target = p['loss_target'][0]
    s, d = x.shape
    depth = st.depth

    for l in range(depth):
        for tag in ("ffn1", "mix", "xattn", "ffn2"):
            st.place(st.block_keys(tag, l))
    first = ('ffn1_w13', 0)
    st.landed[first] = run_exchange("gather_first", gather_over_ici_by_neighbours([st.placed.pop(first)]))[0]

    cw_local = p['conv_w']
    n_conv, cwid, dq4 = cw_local.shape
    cw_rows = jnp.pad(cw_local.reshape(-1, LANE), ((0, (-cw_local.size // LANE) % 8), (0, 0)))
    cw_all = gather_all("gather_conv_w", cw_rows)[0::2, :cw_local.size // LANE]
    conv_w_full = cw_all.reshape(N_CHIPS, n_conv, cwid, dq4).transpose(1, 2, 0, 3).reshape(n_conv, cwid, N_CHIPS * dq4)

    saved = []
    for l in range(depth):
        j = l // 2
        rec = {}
        soon = st.block_keys("mix", l) + st.block_keys("xattn", l)
        x, rec['ffn1'] = st.ffn_fwd("ffn1", l, x, p['ffn1_norm'][l],
                                    ([], [('ffn1_w13', 0)] if l == 0 else [], []),
                                    (soon, [], [('ffn1_w2', 0)] if l == 0 else []),
                                    ([('ffn2_w13', l)], soon, []))
        h = rms_fwd(f"l{l}_mix_norm", x, p['mix_norm'][l])
        if l % 2 == 0:
            e = p['gmlp_ln_g'].shape[-1]
            bias = jnp.repeat(p['gmlp_b_s'][j].T, e // GMLP_GROUPS, axis=1)
            z = st.carrying_gather(mm_nn, [('ffn2_w2', l)], [('ffn2_w13', l)], [], f"l{l}_gmlp_in", h,
                                   st.w[('gmlp_w_in', j)], 'col', BF16)
            gate = gmlp_fwd(f"l{l}_gmlp_gate", z, p['gmlp_ln_g'][j], p['gmlp_ln_b'][j], p['gmlp_w_s'][j], bias)
            x_new, hq = st.carrying_gather(mm_nn, [], [('ffn2_w2', l)], [], f"l{l}_gmlp_out", gate,
                                           st.w[('gmlp_w_out', j)], 'row', F32, res=x, norm_gain=p['xattn_norm'][l])
            rec['mix'] = (x, h, z, gate, bias)
        else:
            bcv = st.carrying_gather(mm_nn, [('ffn2_w2', l)], [('ffn2_w13', l)], [], f"l{l}_conv_in", h,
                                     st.w[('conv_w_in', j)], 'col', BF16)
            gate = conv_fwd(f"l{l}_conv_gate", bcv, conv_w_full[j])
            x_new, hq = st.carrying_gather(mm_nn, [], [('ffn2_w2', l)], [], f"l{l}_conv_out", gate,
                                           st.w[('conv_w_out', j)], 'row', F32, res=x, norm_gain=p['xattn_norm'][l])
            rec['mix'] = (x, h, bcv, gate)
        x = x_new
        q = mm_nn(f"l{l}_xattn_q", hq, st.w[('xattn_wq', l)], 'row', BF16)
        mem_n = rms_fwd(f"l{l}_mem_norm", mem, p['mem_norm'][l])
        kv = mm_nn(f"l{l}_xattn_kv", mem_n, st.w[('xattn_wkv', l)], 'col', BF16)
        o = attn_fwd(f"l{l}_xattn_core", q, kv)
        x_new, h_ffn2 = mm_nn(f"l{l}_xattn_o", o, st.w[('xattn_wo', l)], 'row', F32, res=x,
                              norm_gain=p['ffn2_norm'][l])
        rec['xattn'] = (x, hq, q, mem_n, kv, o)
        x = x_new
        ahead = st.block_keys("ffn1", l + 1)
        x, rec['ffn2'] = st.ffn_fwd("ffn2", l, x, p['ffn2_norm'][l], None, (ahead, [], []), ([], ahead, []), h=h_ffn2)
        saved.append(rec)

    dx, dxb, d_final, loss_lanes = loss_head("loss_head", x, p['final_norm'], target)
    loss = lax.psum(0.5 * jnp.sum(loss_lanes) / d, ("x", "y", "c"))

    small = {n: [None] * p[n].shape[0] for n in ('ffn1_norm', 'mix_norm', 'xattn_norm', 'mem_norm', 'ffn2_norm',
                                                  'gmlp_ln_g', 'gmlp_ln_b', 'gmlp_w_s', 'gmlp_b_s', 'conv_w')}
    for l in reversed(range(depth)):
        j = l // 2
        rec = saved[l]
        dx, dxb, small['ffn2_norm'][l] = st.ffn_bwd("ffn2", l, dx, dxb, rec['ffn2'], p['ffn2_norm'][l])

        x_in, hq, q, mem_n, kv, o = rec['xattn']
        name = f"l{l}_xattn"
        do = st.carrying_joins(mm_nt, name + "_do", dxb, st.w[('xattn_wo', l)], 'row', BF16)
        d_wo = mm_tn(name + "_dwo", o, dxb, 'row')
        dq, dkv = attn_bwd(name + "_dcore", q, kv, do)
        d_wq = mm_tn(name + "_dwq", hq, dq, 'row')
        dkvb = dkv.astype(BF16)
        d_wkv = mm_tn(name + "_dwkv", mem_n, dkvb, 'col')
        st.reduce_begin(name, [('xattn_wq', l), ('xattn_wkv', l), ('xattn_wo', l)], [d_wq, d_wkv, d_wo])
        dx, dxb, small['xattn_norm'][l] = mm_nt(name + "_dh", dq, st.w[('xattn_wq', l)], 'row', F32,
                                                norm_back=(x_in, p['xattn_norm'][l], dx))
        dmem_n = mm_nt(name + "_dmem", dkvb, st.w[('xattn_wkv', l)], 'col', F32)
        small['mem_norm'][l] = rms_bwd(f"l{l}_mem_dnorm", mem, p['mem_norm'][l], dmem_n, None)[2]

        if l % 2 == 0:
            x_in, h, z, gate, bias = rec['mix']
            name = f"l{l}_gmlp"
            w_in, w_out = st.w[('gmlp_w_in', j)], st.w[('gmlp_w_out', j)]
            dgate = mm_nt(name + "_dgate", dxb, w_out, 'row', BF16)
            d_wout = mm_tn(name + "_dwout", gate, dxb, 'row')
            dmix, dws, dbs, dlg, dlb = gmlp_bwd(name + "_dgate_core", z, dgate, p['gmlp_ln_g'][j], p['gmlp_ln_b'][j],
                                                p['gmlp_w_s'][j], bias)
            small['gmlp_w_s'][j], small['gmlp_b_s'][j] = dws, dbs[:, :GMLP_GROUPS].T
            small['gmlp_ln_g'][j], small['gmlp_ln_b'][j] = dlg, dlb
            keys = [('gmlp_w_in', j), ('gmlp_w_out', j)]
        else:
            x_in, h, bcv, gate = rec['mix']
            name = f"l{l}_conv"
            w_in, w_out = st.w[('conv_w_in', j)], st.w[('conv_w_out', j)]
            dgate = mm_nt(name + "_dgate", dxb, w_out, 'row', BF16)
            d_wout = mm_tn(name + "_dwout", gate, dxb, 'row')
            dmix, dcw = conv_bwd(name + "_dgate_core", bcv, dgate, conv_w_full[j])
            small['conv_w'][j] = dcw[:cwid]
            keys = [('conv_w_in', j), ('conv_w_out', j)]
        d_win = st.carrying_scatter(mm_tn, name + "_dwin", h, dmix, 'col')
        st.reduce_begin(name, keys, [d_win, d_wout])
        dx, dxb, small['mix_norm'][l] = st.carrying_scatter(mm_nt, name + "_dh", dmix, w_in, 'col', F32,
                                                            norm_back=(x_in, p['mix_norm'][l], dx))

        dx, dxb, small['ffn1_norm'][l] = st.ffn_bwd("ffn1", l, dx, dxb, rec['ffn1'], p['ffn1_norm'][l])
    st.joins_alone("join_last")

    small_names = ['ffn1_norm', 'mix_norm', 'xattn_norm', 'mem_norm', 'ffn2_norm', 'gmlp_ln_g', 'gmlp_ln_b', 'gmlp_w_s',
                   'gmlp_b_s', 'final_norm', 'conv_w']
    small_full = {n: jnp.stack([g.reshape(p[n].shape[1:]) for g in small[n]]) for n in small_names
                  if n not in ('final_norm', 'conv_w')}
    small_full['final_norm'] = d_final.reshape(p['final_norm'].shape)
    small_full['conv_w'] = jnp.stack(small['conv_w'])
    packed = jnp.concatenate([small_full[n].reshape(-1, LANE) for n in small_names], axis=0)
    total = sum_leading("small_sum", gather_all("small_gather", packed))
    small_g, at = {}, 0
    for n in small_names:
        rows = small_full[n].size // LANE
        small_g[n] = total[at:at + rows].reshape(small_full[n].shape)
        at += rows
    small_g['conv_w'] = lax.dynamic_slice_in_dim(small_g['conv_w'], st.chip * dq4, dq4, axis=2)

    grads, deltas, new_m, new_v = {}, {}, {}, {}
    for n in WEIGHTS:
        w, m, v = p[n], p['m_' + n], p['v_' + n]
        if n in BIG:
            carried = None
            for i in range(w.shape[0]):
                carried = adamw_layer(f"adamw_{n}{i}", w, m, v, st.big_g[(n, i)], i, carried)
            grads[n], deltas[n], new_m[n], new_v[n] = carried
        else:
            g = small_g[n]
            out = adamw_flat(f"adamw_{n}", _as_rows(w), _as_rows(m), _as_rows(v), _as_rows(g))
            grads[n] = g
            deltas[n], new_m[n], new_v[n] = (o.reshape(w.shape) for o in out)

    grad_x = dx.reshape(p['x'].shape)
    return (loss, grad_x, *[grads[n] for n in WEIGHTS], *[deltas[n] for n in WEIGHTS], *[new_m[n] for n in WEIGHTS],
            *[new_v[n] for n in WEIGHTS])
```

```python
from typing import Callable, NamedTuple

import jax
import jax.numpy as jnp
from jax import lax
from jax.experimental import pallas as pl
from jax.experimental.pallas import tpu as pltpu

F32 = jnp.float32
BF16 = jnp.bfloat16
MESH = pl.DeviceIdType.MESH

CHUNK = 128
GMLP_GROUPS = 8
XATTN_HEADS = 4
RMS_EPS = 1e-6
LN_EPS = 1e-5
ADAM_LR = 0.001
ADAM_B1 = 0.9
ADAM_B2 = 0.999
ADAM_EPS = 1e-08
ADAM_WD = 0.01
ADAM_STEP = 10

N_CHIPS = 4
N_DEV = 8

VMEM_LIMIT_BYTES = 60 * 2**20
VMEM_PLAN_BYTES = 52 * 2**20
LANE = 128
MXU_DIM = 256
ACC_CHUNK = 2 * MXU_DIM
MXU_FLOPS_PER_US = 996e6
HBM_BYTES_PER_US = 3.3e6
STEP_US = 0.35
ACC_US_PER_VREG = 0.58e-3

WEIGHTS = ['ffn1_norm', 'ffn1_w13', 'ffn1_w2', 'mix_norm', 'gmlp_w_in', 'gmlp_ln_g', 'gmlp_ln_b', 'gmlp_w_s',
           'gmlp_b_s', 'gmlp_w_out', 'conv_w_in', 'conv_w', 'conv_w_out', 'xattn_norm', 'mem_norm', 'xattn_wq',
           'xattn_wkv', 'xattn_wo', 'ffn2_norm', 'ffn2_w13', 'ffn2_w2', 'final_norm']
BIG = {'ffn1_w13': 'col', 'ffn1_w2': 'row', 'gmlp_w_in': 'col', 'gmlp_w_out': 'row', 'conv_w_in': 'col',
       'conv_w_out': 'row', 'xattn_wq': 'row', 'xattn_wkv': 'col', 'xattn_wo': 'row', 'ffn2_w13': 'col',
       'ffn2_w2': 'row'}
ARG_NAMES = (['x', 'mem'] + WEIGHTS + ['loss_target'] + ['m_' + n for n in WEIGHTS] + ['v_' + n for n in WEIGHTS])


def _params(**kw):
    return pltpu.CompilerParams(vmem_limit_bytes=VMEM_LIMIT_BYTES, **kw)


def _divisors(n, mult, cap):
    return [d for d in range(mult, min(n, cap) + 1, mult) if n % d == 0] or [n]


def _row_tile(rows, width_bytes, budget=4 * 2**20):
    best = None
    for d in _divisors(rows, 16, 1024):
        if d * width_bytes <= budget:
            best = d
    return best or _divisors(rows, 16, 1024)[0]


ANY = pl.BlockSpec(memory_space=pl.ANY)


class Exchange(NamedTuple):
    inputs: list
    out_shapes: list
    aliases: dict
    n_sems: int
    start: Callable
    finish: Callable


def _place():
    x, y, c = lax.axis_index("x"), lax.axis_index("y"), lax.axis_index("c")
    chips = [(1 - x, y), (x, 1 - y), (1 - x, 1 - y)]
    return x, y, c, 2 * x + y, chips


def _remote(src, dst, sems, k, device):
    return pltpu.make_async_remote_copy(src_ref=src, dst_ref=dst, send_sem=sems[0].at[k], recv_sem=sems[1].at[k],
                                        device_id=device, device_id_type=MESH)


def _mxu_fill(dim):
    return dim / (-(-dim // MXU_DIM) * MXU_DIM)


def _tile_time(flops, fill, traffic, steps, acc_vregs):
    return (max(flops / (MXU_FLOPS_PER_US * fill), traffic / HBM_BYTES_PER_US) + steps * STEP_US
            + steps * acc_vregs * ACC_US_PER_VREG)


def _plan_mm(m, n_tiles_of, k_tiles_of, n, k, a_item, o_item, has_res, a_arrays=1):
    best, best_cost = None, None
    for tm in _divisors(m, 16, 1024):
        for tn in n_tiles_of:
            for tk in k_tiles_of:
                ni, nj, nk = m // tm, n // tn, k // tk
                blocks = a_arrays * tm * tk * a_item + tk * tn * 2 + tm * tn * o_item + (tm * tn * 4 if has_res else 0)
                vmem = 2 * blocks + tm * tn * 4 * (2 if nk > 1 else 1)
                if vmem > VMEM_PLAN_BYTES:
                    continue
                traffic = nj * m * k * a_item + (k * n * 2 if nk == 1 else ni * k * n * 2)
                traffic += m * n * (o_item + (4 if has_res else 0))
                cost = _tile_time(2 * m * n * k, _mxu_fill(tk) * _mxu_fill(tn), traffic, ni * nj * nk,
                                  tm * tn // 1024 if nk > 1 else 0)
                if best_cost is None or cost < best_cost:
                    best, best_cost = (tm, tn, tk), cost
    assert best is not None, (m, n, k)
    return best


def _tiled_call(name, grid, operands, in_specs, out_shapes, out_specs, scratch, compute, exchange=None, aliases=None):
    n_reg, n_out, n_scr = len(operands), len(out_shapes), len(scratch)
    n_xin = len(exchange.inputs) if exchange else 0
    n_xout = len(exchange.out_shapes) if exchange else 0
    semantics = ("arbitrary",) * len(grid)

    def body(*refs):
        ins = refs[:n_reg]
        outs = refs[n_reg + n_xin:n_reg + n_xin + n_out]
        scr = refs[n_reg + n_xin + n_out + n_xout:n_reg + n_xin + n_out + n_xout + n_scr]
        if not exchange:
            compute(ins, outs, scr)
            return
        x_ins = refs[n_reg:n_reg + n_xin]
        x_outs = refs[n_reg + n_xin + n_out:n_reg + n_xin + n_out + n_xout]
        sems = refs[-2:]
        at_first, at_last = True, True
        for k, extent in enumerate(grid):
            at_first = jnp.logical_and(at_first, pl.program_id(k) == 0)
            at_last = jnp.logical_and(at_last, pl.program_id(k) == extent - 1)

        @pl.when(at_first)
        def _():
            exchange.start(x_ins, x_outs, sems)

        compute(ins, outs, scr)

        @pl.when(at_last)
        def _():
            exchange.finish(x_ins, x_outs, sems)

    if not exchange:
        return pl.pallas_call(
            body, name=name, grid=grid, in_specs=in_specs, out_specs=out_specs, out_shape=out_shapes,
            scratch_shapes=scratch, input_output_aliases=dict(aliases or {}),
            compiler_params=_params(dimension_semantics=semantics),
        )(*operands)
    assert not aliases
    sems = [pltpu.SemaphoreType.DMA((exchange.n_sems,)), pltpu.SemaphoreType.DMA((exchange.n_sems,))]
    got = pl.pallas_call(
        body, name=name, grid=grid, in_specs=in_specs + [ANY] * n_xin, out_specs=out_specs + [ANY] * n_xout,
        out_shape=out_shapes + list(exchange.out_shapes), scratch_shapes=scratch + sems,
        input_output_aliases={n_reg + i: n_out + o for i, o in exchange.aliases.items()},
        compiler_params=_params(dimension_semantics=semantics),
    )(*operands, *exchange.inputs)
    return list(got[:n_out]), list(got[n_out:])


class Split(NamedTuple):
    slot: int
    other: jax.Array
    spec: pl.BlockSpec
    use_other: Callable


def _mm_call(name, grid, operands, in_specs, out_shape, out_spec, contract, nk, scale, has_res, tile, exchange=None,
             split=None, into=None, normed=False, norm_back=False):
    n_main = len(operands)
    out_shapes, out_specs = [out_shape], [out_spec]
    if normed:
        out_shapes, out_specs = out_shapes + [jax.ShapeDtypeStruct(out_shape.shape, BF16)], out_specs + [out_spec]
    if norm_back:
        m, n = out_shape.shape
        out_shapes = [jax.ShapeDtypeStruct((m, n), F32), jax.ShapeDtypeStruct((m, n), BF16),
                      jax.ShapeDtypeStruct((1, n), F32)]
        out_specs = [out_spec, out_spec, pl.BlockSpec((1, n), lambda j, i, kk: (0, 0))]

    def compute(ins, outs, scr):
        res_ref = ins[2] if has_res else None
        o_ref = outs[0]
        acc_ref = scr[0] if nk > 1 else None

        def finish_norm_back(v):
            dx, dg = _rms_bwd_rows(ins[2][...], ins[3][...], v)
            dx = dx + ins[4][...]
            outs[0][...] = dx
            outs[1][...] = dx.astype(BF16)
            first_rows = pl.program_id(1) == 0

            @pl.when(first_rows)
            def _():
                outs[2][...] = dg

            @pl.when(jnp.logical_not(first_rows))
            def _():
                outs[2][...] += dg

        def finish(v):
            if scale != 1.0:
                v = v * scale
            if norm_back:
                finish_norm_back(v)
                return
            if has_res:
                v = res_ref[...] + v
            o_ref[...] = v.astype(o_ref.dtype)
            if normed:
                outs[1][...] = _rms_rows(v, ins[2 + has_res][...])[2].astype(BF16)

        def b_block(cols):
            ref = b_ref_of[0]
            stacked = len(ref.shape) == 3
            if contract[0][1] == (1,):
                if not stacked:
                    return ref[cols, :]
                return ref[...].reshape(ref.shape[0] * ref.shape[1], ref.shape[2])[cols, :]
            if not stacked:
                return ref[:, cols]
            b = ref[:, :, cols]
            return b.reshape(b.shape[0] * b.shape[1], b.shape[2])

        b_ref_of = [None]

        def contribute(a_ref, b_ref):
            b_ref_of[0] = b_ref
            if nk == 1:
                finish(lax.dot_general(a_ref[...], b_block(slice(None)), contract, preferred_element_type=F32))
                return
            kk = pl.program_id(2)

            @pl.when(kk == 0)
            def _():
                acc_ref[...] = jnp.zeros(tile, F32)

            a = a_ref[...]
            for start in range(0, tile[1], ACC_CHUNK):
                cols = slice(start, min(start + ACC_CHUNK, tile[1]))
                acc_ref[:, cols] += lax.dot_general(a, b_block(cols), contract, preferred_element_type=F32)

            @pl.when(kk == nk - 1)
            def _():
                finish(acc_ref[...])

        if split is None:
            contribute(ins[0], ins[1])
            return
        use_other = split.use_other(pl.program_id(0), pl.program_id(1), pl.program_id(2))
        pair = [ins[0], ins[1]]
        other = list(pair)
        other[split.slot] = ins[n_main]

        @pl.when(jnp.logical_not(use_other))
        def _():
            contribute(*pair)

        @pl.when(use_other)
        def _():
            contribute(*other)

    aliases = None
    if split is not None:
        operands, in_specs = operands + [split.other], in_specs + [split.spec]
    if into is not None:
        aliases = {len(operands): 0}
        operands, in_specs = operands + [into], in_specs + [ANY]
    got = _tiled_call(name, grid, operands, in_specs, out_shapes, out_specs,
                      [pltpu.VMEM(tile, F32)] if nk > 1 else [], compute, exchange, aliases)
    results, carried = (got[0], got[1]) if exchange else (got, None)
    out = tuple(results) if normed or norm_back else results[0]
    return (out, carried) if exchange else out


def mm_nn(name, a, w, kind, out_dtype, res=None, scale=1.0, exchange=None, norm_gain=None):
    m, k = a.shape
    p, r, c = w.shape
    n = p * c if kind == 'col' else c
    assert k == (r if kind == 'col' else p * r), (name, a.shape, w.shape)
    n_tiles = _divisors(c, LANE, 2816)
    if norm_gain is not None:
        assert kind == 'row'
        n_tiles = [n]
    k_tiles = _divisors(r, LANE, 4096)
    if kind == 'row':
        k_tiles = k_tiles + [q * r for q in (2, 4) if p % q == 0]
    o_item = jnp.dtype(out_dtype).itemsize + (2 if norm_gain is not None else 0)
    tm, tn, tk = _plan_mm(m, n_tiles, k_tiles, n, k, a.dtype.itemsize, o_item, res is not None)
    buffering = None
    if norm_gain is not None:
        per_row = 2 * (k * a.dtype.itemsize + n * (o_item + (4 if res is not None else 0))) + n * 4
        fitting = [t for t in _divisors(m, 16, 1024) if t * per_row + k * n * 2 <= VMEM_PLAN_BYTES]
        if fitting:
            tm, tn, tk, buffering = fitting[-1], n, k, pl.Buffered(1)
    nk = k // tk
    if kind == 'col':
        cpt = c // tn
        w_spec = pl.BlockSpec((None, tk, tn), lambda j, i, kk: (j // cpt, kk, j % cpt))
    elif tk > r:
        w_spec = pl.BlockSpec((tk // r, r, tn), lambda j, i, kk: (kk, 0, j), pipeline_mode=buffering)
    else:
        rpt = r // tk
        w_spec = pl.BlockSpec((None, tk, tn), lambda j, i, kk: (kk // rpt, kk % rpt, j))
    in_specs = [pl.BlockSpec((tm, tk), lambda j, i, kk: (i, kk)), w_spec]
    operands = [a, w]
    if res is not None:
        in_specs.append(pl.BlockSpec((tm, tn), lambda j, i, kk: (i, j)))
        operands.append(res)
    if norm_gain is not None:
        in_specs.append(pl.BlockSpec((1, tn), lambda j, i, kk: (0, 0)))
        operands.append(norm_gain.reshape(1, n))
    return _mm_call(name, (n // tn, m // tm, nk), operands, in_specs, jax.ShapeDtypeStruct((m, n), out_dtype),
                    pl.BlockSpec((tm, tn), lambda j, i, kk: (i, j)), (((1,), (0,)), ((), ())), nk, scale,
                    res is not None, (tm, tn), exchange, normed=norm_gain is not None)


def mm_nt(name, a, w, kind, out_dtype, scale=1.0, exchange=None, a_hi=None, norm_back=None):
    m, kc = a.shape
    if a_hi is not None:
        assert a_hi.shape == a.shape
        kc = 2 * kc
    p, r, c = w.shape
    n = r if kind == 'col' else p * r
    assert kc == (p * c if kind == 'col' else c), (name, a.shape, w.shape)
    n_tiles = _divisors(r, LANE, 2816)
    k_tiles = _divisors(c, LANE, 4096)
    if kind == 'row':
        n_tiles = n_tiles + [q * r for q in (2, 4) if p % q == 0 and q * r <= 2816]
    o_item = jnp.dtype(out_dtype).itemsize
    if norm_back is not None:
        assert a_hi is None and n in n_tiles
        n_tiles, o_item = [n], 4 + 2 + 4
    tm, tn, tk = _plan_mm(m, n_tiles, k_tiles, n, kc, a.dtype.itemsize, o_item, norm_back is not None,
                          1 if a_hi is None else 2)
    nk = kc // tk
    if kind == 'col':
        cpt = c // tk
        w_spec = pl.BlockSpec((None, tn, tk), lambda j, i, kk: (kk // cpt, j, kk % cpt))
    elif tn > r:
        w_spec = pl.BlockSpec((tn // r, r, tk), lambda j, i, kk: (j, 0, kk))
    else:
        rpt = r // tn
        w_spec = pl.BlockSpec((None, tn, tk), lambda j, i, kk: (j // rpt, j % rpt, kk))
    split = None
    a_spec = pl.BlockSpec((tm, tk), lambda j, i, kk: (i, kk))
    if a_hi is not None:
        half = nk // 2
        assert nk % 2 == 0
        a_spec = pl.BlockSpec((tm, tk), lambda j, i, kk: (i, jnp.minimum(kk, half - 1)))
        split = Split(0, a_hi, pl.BlockSpec((tm, tk), lambda j, i, kk: (i, jnp.maximum(kk - half, 0))),
                      lambda j, i, kk: kk >= half)
    operands, in_specs = [a, w], [a_spec, w_spec]
    if norm_back is not None:
        x, gain, dres = norm_back
        rows = pl.BlockSpec((tm, n), lambda j, i, kk: (i, 0))
        operands += [x, gain.reshape(1, n), dres]
        in_specs += [rows, pl.BlockSpec((1, n), lambda j, i, kk: (0, 0)), rows]
    return _mm_call(name, (n // tn, m // tm, nk), operands, in_specs, jax.ShapeDtypeStruct((m, n), out_dtype),
                    pl.BlockSpec((tm, tn), lambda j, i, kk: (i, j)), (((1,), (1,)), ((), ())), nk, scale, False,
                    (tm, tn), exchange, split, norm_back=norm_back is not None)


def _plan_tn(s, ka, nd, r_tiles, n_tiles):
    best, best_cost = None, None
    for ts in _divisors(s, 16, 2048):
        for tr in r_tiles:
            for tn in n_tiles:
                ni, nj, ns = ka // tr, nd // tn, s // ts
                vmem = 2 * (ts * tr * 2 + ts * tn * 2 + tr * tn * 2) + tr * tn * 4 * (2 if ns > 1 else 1)
                if vmem > VMEM_PLAN_BYTES:
                    continue
                traffic = nj * s * ka * 2 + ni * s * nd * 2 + ka * nd * 2
                cost = _tile_time(2 * s * ka * nd, _mxu_fill(ts) * _mxu_fill(tn), traffic, ni * nj * ns,
                                  tr * tn // 1024 if ns > 1 else 0)
                if best_cost is None or cost < best_cost:
                    best, best_cost = (ts, tr, tn), cost
    assert best is not None, (s, ka, nd)
    return best


def mm_tn(name, a, dy, kind, scale=1.0, exchange=None, panels=(0, N_CHIPS), into=None):
    s, ka = a.shape
    s2, nd = dy.shape
    assert s == s2
    p = N_CHIPS
    first_panel, n_panels = panels
    assert kind == 'col' or panels == (0, p)
    r, c = (ka, nd // n_panels) if kind == 'col' else (ka // p, nd)
    ts, tr, tn = _plan_tn(s, ka, nd, _divisors(r, LANE, 2048), _divisors(c, LANE, 2816))
    ns = s // ts
    if kind == 'col':
        cpt = c // tn
        o_spec = pl.BlockSpec((None, tr, tn), lambda j, i, kk: (first_panel + j // cpt, i, j % cpt))
    else:
        rpt = r // tr
        o_spec = pl.BlockSpec((None, tr, tn), lambda j, i, kk: (i // rpt, i % rpt, j))
    in_specs = [pl.BlockSpec((ts, tr), lambda j, i, kk: (kk, i)), pl.BlockSpec((ts, tn), lambda j, i, kk: (kk, j))]
    return _mm_call(name, (nd // tn, ka // tr, ns), [a, dy], in_specs, jax.ShapeDtypeStruct((p, r, c), BF16), o_spec,
                    (((0,), (0,)), ((), ())), ns, scale, False, (tr, tn), exchange, None, into)


def _plan_fused(m, k, f, tiles, n_w, n_io):
    best, best_cost = None, None
    for tm in _divisors(m, 16, 1024):
        for tn in tiles:
            vmem = 2 * (tm * k * 2 + n_io * tm * tn * 2) + n_w * k * tn * 2 + 4 * tm * tn * 4
            if vmem > VMEM_PLAN_BYTES:
                continue
            traffic = (f // tn) * m * k * 2 + n_w * k * f * 2 + n_io * m * f * 2
            cost = _tile_time(2 * m * k * f * n_w, _mxu_fill(tn), traffic, (f // tn) * (m // tm), 0)
            if best_cost is None or cost < best_cost:
                best, best_cost = (tm, tn), cost
    assert best is not None, (m, k, f)
    return best


def mm_swiglu(name, h, w13, exchange=None):
    m, k = h.shape
    p, r, c = w13.shape
    assert r == k and p % 2 == 0
    f = p * c // 2
    tm, tn = _plan_fused(m, k, f, _divisors(c, LANE, 2816), 2, 3)
    cpt = c // tn

    def compute(ins, outs, scr):
        a = ins[0][...]
        g = jnp.dot(a, ins[1][...], preferred_element_type=F32)
        u = jnp.dot(a, ins[2][...], preferred_element_type=F32)
        sg = _sigmoid(g)
        silu = g * sg
        outs[0][...] = (u * (sg * (1.0 + g * (1.0 - sg)))).astype(BF16)
        outs[1][...] = silu.astype(BF16)
        outs[2][...] = (silu * u).astype(BF16)

    tile = pl.BlockSpec((tm, tn), lambda j, i: (i, j))
    in_specs = [pl.BlockSpec((tm, k), lambda j, i: (i, 0)),
                pl.BlockSpec((None, k, tn), lambda j, i: (j // cpt, 0, j % cpt), pipeline_mode=pl.Buffered(1)),
                pl.BlockSpec((None, k, tn), lambda j, i: (j // cpt + p // 2, 0, j % cpt),
                             pipeline_mode=pl.Buffered(1))]
    shape = jax.ShapeDtypeStruct((m, f), BF16)
    got = _tiled_call(name, (f // tn, m // tm), [h, w13, w13], in_specs, [shape] * 3, [tile] * 3, [], compute, exchange)
    return got


def mm_dswiglu(name, dy, w2, by_gate, by_up, scale, exchange=None):
    m, k = dy.shape
    p, r, c = w2.shape
    assert c == k
    f = p * r
    tiles = _divisors(r, LANE, 2816) + [q * r for q in (2, 4) if p % q == 0 and q * r <= 2816]
    tm, tn = _plan_fused(m, k, f, tiles, 1, 4)

    def compute(ins, outs, scr):
        b = ins[1][...]
        if b.ndim == 3:
            b = b.reshape(b.shape[0] * b.shape[1], b.shape[2])
        d = lax.dot_general(ins[0][...], b, (((1,), (1,)), ((), ())), preferred_element_type=F32) * scale
        outs[0][...] = (d * ins[2][...].astype(F32)).astype(BF16)
        outs[1][...] = (d * ins[3][...].astype(F32)).astype(BF16)

    tile = pl.BlockSpec((tm, tn), lambda j, i: (i, j))
    if tn > r:
        w_spec = pl.BlockSpec((tn // r, r, k), lambda j, i: (j, 0, 0), pipeline_mode=pl.Buffered(1))
    else:
        rpt = r // tn
        w_spec = pl.BlockSpec((None, tn, k), lambda j, i: (j // rpt, j % rpt, 0), pipeline_mode=pl.Buffered(1))
    in_specs = [pl.BlockSpec((tm, k), lambda j, i: (i, 0)), w_spec, tile, tile]
    shape = jax.ShapeDtypeStruct((m, f), BF16)
    return _tiled_call(name, (f // tn, m // tm), [dy, w2, by_gate, by_up], in_specs, [shape] * 2, [tile] * 2, [], compute,
                       exchange)


def _rms_rows(x, g):
    r = lax.rsqrt(jnp.mean(x * x, axis=-1, keepdims=True) + RMS_EPS)
    xhat = x * r
    return xhat, r, xhat * g


def rms_fwd(name, x, g, exchange=None):
    s, d = x.shape
    tm = _row_tile(s, d * 4)

    def compute(ins, outs, scr):
        outs[0][...] = _rms_rows(ins[0][...], ins[1][...])[2].astype(BF16)

    got = _tiled_call(name, (s // tm,), [x, g.reshape(1, d)],
                      [pl.BlockSpec((tm, d), lambda i: (i, 0)), pl.BlockSpec((1, d), lambda i: (0, 0))],
                      [jax.ShapeDtypeStruct((s, d), BF16)], [pl.BlockSpec((tm, d), lambda i: (i, 0))], [], compute,
                      exchange)
    return (got[0][0], got[1]) if exchange else got[0]


def _rms_bwd_rows(x, g, dh):
    xhat, r, _ = _rms_rows(x, g)
    u = dh * g
    dx = r * (u - xhat * jnp.mean(u * xhat, axis=-1, keepdims=True))
    return dx, jnp.sum(dh * xhat, axis=0, keepdims=True)


def rms_bwd(name, x, g, dh, dres):
    s, d = x.shape
    tm = _row_tile(s, d * 4, 2 * 2**20)
    has_res = dres is not None

    def body(*refs):
        x_ref, g_ref, dh_ref = refs[:3]
        dres_ref = refs[3] if has_res else None
        dx_ref, dxb_ref, dg_ref = refs[-3:]
        dx, dg = _rms_bwd_rows(x_ref[...], g_ref[...], dh_ref[...].astype(F32))
        if has_res:
            dx = dx + dres_ref[...]
        dx_ref[...] = dx
        dxb_ref[...] = dx.astype(BF16)

        @pl.when(pl.program_id(0) == 0)
        def _():
            dg_ref[...] = dg

        @pl.when(pl.program_id(0) > 0)
        def _():
            dg_ref[...] += dg

    row = pl.BlockSpec((tm, d), lambda i: (i, 0))
    vec = pl.BlockSpec((1, d), lambda i: (0, 0))
    return pl.pallas_call(
        body, name=name, grid=(s // tm,),
        in_specs=[row, vec, row] + ([row] if has_res else []),
        out_specs=[row, row, vec],
        out_shape=[jax.ShapeDtypeStruct((s, d), F32), jax.ShapeDtypeStruct((s, d), BF16),
                   jax.ShapeDtypeStruct((1, d), F32)],
        compiler_params=_params(dimension_semantics=("arbitrary",)),
    )(x, g.reshape(1, d), dh, *([dres] if has_res else []))


def loss_head(name, x, g, target):
    s, d = x.shape
    tm = _row_tile(s, d * 4, 2 * 2**20)

    def body(x_ref, g_ref, t_ref, dx_ref, dxb_ref, dg_ref, loss_ref):
        x = x_ref[...]
        gain = g_ref[...]
        y = _rms_rows(x, gain)[2]
        diff = y - t_ref[...]
        dx, dg = _rms_bwd_rows(x, gain, diff * (1.0 / d))
        dx_ref[...] = dx
        dxb_ref[...] = dx.astype(BF16)
        sq = jnp.sum(diff * diff, axis=0, keepdims=True)

        @pl.when(pl.program_id(0) == 0)
        def _():
            dg_ref[...] = dg
            loss_ref[...] = sq

        @pl.when(pl.program_id(0) > 0)
        def _():
            dg_ref[...] += dg
            loss_ref[...] += sq

    row = pl.BlockSpec((tm, d), lambda i: (i, 0))
    vec = pl.BlockSpec((1, d), lambda i: (0, 0))
    return pl.pallas_call(
        body, name=name, grid=(s // tm,), in_specs=[row, vec, row], out_specs=[row, row, vec, vec],
        out_shape=[jax.ShapeDtypeStruct((s, d), F32), jax.ShapeDtypeStruct((s, d), BF16),
                   jax.ShapeDtypeStruct((1, d), F32), jax.ShapeDtypeStruct((1, d), F32)],
        compiler_params=_params(dimension_semantics=("arbitrary",)),
    )(x, g.reshape(1, d), target)


def _sigmoid(x):
    return 0.5 * jnp.tanh(0.5 * x) + 0.5


_INV_SQRT2 = 0.7071067811865476
_INV_SQRT_2PI = 0.3989422804014327


def _normal_cdf(z):
    return 0.5 * (1.0 + lax.erf(z * _INV_SQRT2))


def _gelu_grad(z, cdf):
    return cdf + z * (_INV_SQRT_2PI * jnp.exp(-0.5 * z * z))


def _causal_weights(ws_ref, g):
    t = ws_ref.shape[-1]
    keep = lax.broadcasted_iota(jnp.int32, (t, t), 0) >= lax.broadcasted_iota(jnp.int32, (t, t), 1)
    return jnp.where(keep, ws_ref[g], 0.0).astype(BF16), keep


def _gmlp_gate_rows(z_ref, lg_ref, lb_ref, e):
    z = z_ref[...].astype(F32)
    cdf = _normal_cdf(z)
    gz = z * cdf
    u, v = gz[:, :e], gz[:, e:]
    mu = jnp.mean(v, axis=-1, keepdims=True)
    xc = v - mu
    rs = lax.rsqrt(jnp.mean(xc * xc, axis=-1, keepdims=True) + LN_EPS)
    vhat = xc * rs
    return (z, cdf), u, vhat, rs, vhat * lg_ref[...] + lb_ref[...]


def gmlp_fwd(name, z, ln_g, ln_b, w_s, bias):
    s, e2 = z.shape
    e = e2 // 2
    eg = e // GMLP_GROUPS

    def body(z_ref, lg_ref, lb_ref, ws_ref, b_ref, o_ref):
        _, u, _, _, vln = _gmlp_gate_rows(z_ref, lg_ref, lb_ref, e)
        vb = vln.astype(BF16)
        for g in range(GMLP_GROUPS):
            cols = slice(g * eg, (g + 1) * eg)
            wm, _ = _causal_weights(ws_ref, g)
            f = jnp.dot(wm, vb[:, cols], preferred_element_type=F32) + b_ref[:, cols]
            o_ref[:, cols] = (u[:, cols] * f).astype(BF16)

    full = lambda shape: pl.BlockSpec(shape, lambda i: (0,) * len(shape))
    return pl.pallas_call(
        body, name=name, grid=(s // CHUNK,),
        in_specs=[pl.BlockSpec((CHUNK, e2), lambda i: (i, 0)), full((1, e)), full((1, e)),
                  full((GMLP_GROUPS, CHUNK, CHUNK)), full((CHUNK, e))],
        out_specs=pl.BlockSpec((CHUNK, e), lambda i: (i, 0)), out_shape=jax.ShapeDtypeStruct((s, e), BF16),
        compiler_params=_params(dimension_semantics=("arbitrary",)),
    )(z, ln_g.reshape(1, e), ln_b.reshape(1, e), w_s, bias)


def gmlp_bwd(name, z, dp, ln_g, ln_b, w_s, bias):
    s, e2 = z.shape
    e = e2 // 2
    eg = e // GMLP_GROUPS
    t = CHUNK

    def body(z_ref, dp_ref, lg_ref, lb_ref, ws_ref, b_ref, dz_ref, dws_ref, dbs_ref, dlg_ref, dlb_ref):
        first = pl.program_id(0) == 0
        (zf, cdf), u, vhat, rs, vln = _gmlp_gate_rows(z_ref, lg_ref, lb_ref, e)
        vb = vln.astype(BF16)
        dp = dp_ref[...].astype(F32)
        lane = lax.broadcasted_iota(jnp.int32, (t, LANE), 1)
        dbs = jnp.zeros((t, LANE), F32)
        dvln_parts = []
        for g in range(GMLP_GROUPS):
            cols = slice(g * eg, (g + 1) * eg)
            wm, keep = _causal_weights(ws_ref, g)
            f = jnp.dot(wm, vb[:, cols], preferred_element_type=F32) + b_ref[:, cols]
            dz_ref[:, cols] = (dp[:, cols] * f * _gelu_grad(zf[:, cols], cdf[:, cols])).astype(BF16)
            df = dp[:, cols] * u[:, cols]
            dfb = df.astype(BF16)
            dbs = dbs + jnp.where(lane == g, jnp.sum(df, axis=-1, keepdims=True), 0.0)
            dw = lax.dot_general(dfb, vb[:, cols], (((1,), (1,)), ((), ())), preferred_element_type=F32)
            dw = jnp.where(keep, dw, 0.0)

            @pl.when(first)
            def _():
                dws_ref[g] = dw

            @pl.when(jnp.logical_not(first))
            def _():
                dws_ref[g] += dw

            dvln_parts.append(lax.dot_general(wm, dfb, (((0,), (0,)), ((), ())), preferred_element_type=F32))
        dvln = jnp.concatenate(dvln_parts, axis=-1)
        dvhat = dvln * lg_ref[...]
        dv = rs * (dvhat - jnp.mean(dvhat, axis=-1, keepdims=True)
                   - vhat * jnp.mean(dvhat * vhat, axis=-1, keepdims=True))
        dz_ref[:, e:] = (dv * _gelu_grad(zf[:, e:], cdf[:, e:])).astype(BF16)
        dlg = jnp.sum(dvln * vhat, axis=0, keepdims=True)
        dlb = jnp.sum(dvln, axis=0, keepdims=True)

        @pl.when(first)
        def _():
            dbs_ref[...] = dbs
            dlg_ref[...] = dlg
            dlb_ref[...] = dlb

        @pl.when(jnp.logical_not(first))
        def _():
            dbs_ref[...] += dbs
            dlg_ref[...] += dlg
            dlb_ref[...] += dlb

    full = lambda shape: pl.BlockSpec(shape, lambda i: (0,) * len(shape))
    return pl.pallas_call(
        body, name=name, grid=(s // t,),
        in_specs=[pl.BlockSpec((t, e2), lambda i: (i, 0)), pl.BlockSpec((t, e), lambda i: (i, 0)), full((1, e)),
                  full((1, e)), full((GMLP_GROUPS, t, t)), full((t, e))],
        out_specs=[pl.BlockSpec((t, e2), lambda i: (i, 0)), full((GMLP_GROUPS, t, t)), full((t, LANE)), full((1, e)),
                   full((1, e))],
        out_shape=[jax.ShapeDtypeStruct((s, e2), BF16), jax.ShapeDtypeStruct((GMLP_GROUPS, t, t), F32),
                   jax.ShapeDtypeStruct((t, LANE), F32), jax.ShapeDtypeStruct((1, e), F32),
                   jax.ShapeDtypeStruct((1, e), F32)],
        compiler_params=_params(dimension_semantics=("arbitrary",)),
    )(z, dp, ln_g.reshape(1, e), ln_b.reshape(1, e), w_s, bias)


EDGE = 16


def _shift_down(zc, prev, k):
    tm = zc.shape[0]
    row = lax.broadcasted_iota(jnp.int32, (tm, 1), 0)
    out = pltpu.roll(zc, k, 0)
    for j in range(k):
        out = jnp.where(row == j, prev[EDGE - k + j:EDGE - k + j + 1, :], out)
    return out


def _shift_up(dc, nxt, k):
    tm = dc.shape[0]
    row = lax.broadcasted_iota(jnp.int32, (tm, 1), 0)
    out = pltpu.roll(dc, tm - k, 0)
    for j in range(k):
        out = jnp.where(row == tm - k + j, nxt[j:j + 1, :], out)
    return out


def conv_fwd(name, bcv, cw):
    s, d3 = bcv.shape
    d = d3 // 3
    tm = _row_tile(s, d * 4, 2 * 2**20)
    per = tm // EDGE

    def body(b_ref, c_ref, v_ref, cp_ref, vp_ref, w_ref, o_ref):
        i = pl.program_id(0)
        zc = c_ref[...].astype(F32) * v_ref[...].astype(F32)
        prev = jnp.where(i > 0, cp_ref[...].astype(F32) * vp_ref[...].astype(F32), 0.0)
        conv = w_ref[2:3, :] * zc + w_ref[1:2, :] * _shift_down(zc, prev, 1) + w_ref[0:1, :] * _shift_down(zc, prev, 2)
        o_ref[...] = (b_ref[...].astype(F32) * conv).astype(BF16)

    blk = lambda col: pl.BlockSpec((tm, d), lambda i: (i, col))
    edge = lambda col: pl.BlockSpec((EDGE, d), lambda i: (jnp.maximum(i * per - 1, 0), col))
    return pl.pallas_call(
        body, name=name, grid=(s // tm,),
        in_specs=[blk(0), blk(1), blk(2), edge(1), edge(2), pl.BlockSpec((3, d), lambda i: (0, 0))],
        out_specs=pl.BlockSpec((tm, d), lambda i: (i, 0)), out_shape=jax.ShapeDtypeStruct((s, d), BF16),
        compiler_params=_params(dimension_semantics=("arbitrary",)),
    )(bcv, bcv, bcv, bcv, bcv, cw)


def conv_bwd(name, bcv, dq, cw):
    s, d3 = bcv.shape
    d = d3 // 3
    tm = _row_tile(s, d * 4, 2**20)
    per = tm // EDGE
    n_tiles = s // tm
    last_edge = s // EDGE - 1

    def body(b_ref, c_ref, v_ref, cp_ref, vp_ref, bn_ref, dq_ref, dqn_ref, w_ref, o_ref, dw_ref):
        i = pl.program_id(0)
        b = b_ref[...].astype(F32)
        c = c_ref[...].astype(F32)
        v = v_ref[...].astype(F32)
        dq = dq_ref[...].astype(F32)
        zc = c * v
        prev = jnp.where(i > 0, cp_ref[...].astype(F32) * vp_ref[...].astype(F32), 0.0)
        z1 = _shift_down(zc, prev, 1)
        z2 = _shift_down(zc, prev, 2)
        w0, w1, w2 = w_ref[0:1, :], w_ref[1:2, :], w_ref[2:3, :]
        conv = w2 * zc + w1 * z1 + w0 * z2
        dconv = dq * b
        nxt = jnp.where(i < n_tiles - 1, dqn_ref[...].astype(F32) * bn_ref[...].astype(F32), 0.0)
        dz = w2 * dconv + w1 * _shift_up(dconv, nxt, 1) + w0 * _shift_up(dconv, nxt, 2)
        o_ref[:, :d] = (dq * conv).astype(BF16)
        o_ref[:, d:2 * d] = (dz * v).astype(BF16)
        o_ref[:, 2 * d:] = (dz * c).astype(BF16)
        dw = jnp.concatenate([jnp.sum(dconv * z2, axis=0, keepdims=True), jnp.sum(dconv * z1, axis=0, keepdims=True),
                              jnp.sum(dconv * zc, axis=0, keepdims=True), jnp.zeros((5, d), F32)], axis=0)

        @pl.when(i == 0)
        def _():
            dw_ref[...] = dw

        @pl.when(i > 0)
        def _():
            dw_ref[...] += dw

    blk = lambda col: pl.BlockSpec((tm, d), lambda i: (i, col))
    before = lambda col: pl.BlockSpec((EDGE, d), lambda i: (jnp.maximum(i * per - 1, 0), col))
    after = lambda col: pl.BlockSpec((EDGE, d), lambda i: (jnp.minimum((i + 1) * per, last_edge), col))
    return pl.pallas_call(
        body, name=name, grid=(n_tiles,),
        in_specs=[blk(0), blk(1), blk(2), before(1), before(2), after(0), blk(0), after(0),
                  pl.BlockSpec((3, d), lambda i: (0, 0))],
        out_specs=[pl.BlockSpec((tm, d3), lambda i: (i, 0)), pl.BlockSpec((8, d), lambda i: (0, 0))],
        out_shape=[jax.ShapeDtypeStruct((s, d3), BF16), jax.ShapeDtypeStruct((8, d), F32)],
        compiler_params=_params(dimension_semantics=("arbitrary",)),
    )(bcv, bcv, bcv, bcv, bcv, bcv, dq, dq, cw)


def _attn_probs(qh, kh, scale):
    sc = lax.dot_general(qh, kh, (((1,), (1,)), ((), ())), preferred_element_type=F32) * scale
    ex = jnp.exp(sc - jnp.max(sc, axis=-1, keepdims=True))
    return ex / jnp.sum(ex, axis=-1, keepdims=True)


def attn_fwd(name, q, kv):
    s, d = q.shape
    mlen = kv.shape[0]
    dh = d // XATTN_HEADS
    scale = dh ** -0.5
    tm = _row_tile(s, d * 4, 4 * 2**20)

    def body(q_ref, kv_ref, o_ref):
        for h in range(XATTN_HEADS):
            cols = slice(h * dh, (h + 1) * dh)
            p = _attn_probs(q_ref[:, cols], kv_ref[:, cols], scale)
            o_ref[:, cols] = jnp.dot(p.astype(BF16), kv_ref[:, d + h * dh:d + (h + 1) * dh],
                                     preferred_element_type=F32).astype(BF16)

    return pl.pallas_call(
        body, name=name, grid=(s // tm,),
        in_specs=[pl.BlockSpec((tm, d), lambda i: (i, 0)), pl.BlockSpec((mlen, 2 * d), lambda i: (0, 0))],
        out_specs=pl.BlockSpec((tm, d), lambda i: (i, 0)), out_shape=jax.ShapeDtypeStruct((s, d), BF16),
        compiler_params=_params(dimension_semantics=("arbitrary",)),
    )(q, kv)


def attn_bwd(name, q, kv, do):
    s, d = q.shape
    mlen = kv.shape[0]
    dh = d // XATTN_HEADS
    scale = dh ** -0.5
    tm = _row_tile(s, d * 4, 4 * 2**20)

    def body(q_ref, kv_ref, do_ref, dq_ref, dkv_ref):
        first = pl.program_id(0) == 0
        for h in range(XATTN_HEADS):
            cols = slice(h * dh, (h + 1) * dh)
            vcols = slice(d + h * dh, d + (h + 1) * dh)
            qh, kh, vh, doh = q_ref[:, cols], kv_ref[:, cols], kv_ref[:, vcols], do_ref[:, cols]
            p = _attn_probs(qh, kh, scale)
            dp = lax.dot_general(doh, vh, (((1,), (1,)), ((), ())), preferred_element_type=F32)
            ds = (p * (dp - jnp.sum(dp * p, axis=-1, keepdims=True)) * scale).astype(BF16)
            dq_ref[:, cols] = jnp.dot(ds, kh, preferred_element_type=F32).astype(BF16)
            dk = lax.dot_general(ds, qh, (((0,), (0,)), ((), ())), preferred_element_type=F32)
            dv = lax.dot_general(p.astype(BF16), doh, (((0,), (0,)), ((), ())), preferred_element_type=F32)

            @pl.when(first)
            def _():
                dkv_ref[:, cols] = dk
                dkv_ref[:, vcols] = dv

            @pl.when(jnp.logical_not(first))
            def _():
                dkv_ref[:, cols] += dk
                dkv_ref[:, vcols] += dv

    row = pl.BlockSpec((tm, d), lambda i: (i, 0))
    whole = pl.BlockSpec((mlen, 2 * d), lambda i: (0, 0))
    return pl.pallas_call(
        body, name=name, grid=(s // tm,), in_specs=[row, whole, row], out_specs=[row, whole],
        out_shape=[jax.ShapeDtypeStruct((s, d), BF16), jax.ShapeDtypeStruct((mlen, 2 * d), F32)],
        compiler_params=_params(dimension_semantics=("arbitrary",)),
    )(q, kv, do)


def _as_rows(a):
    if a.ndim >= 2 and a.shape[-1] % LANE == 0:
        return a.reshape(-1, a.shape[-1])
    return a.reshape(-1, LANE) if a.size % LANE == 0 else a.reshape(1, -1)


def add_halves(name, dw, other, core):
    p, r, c = dw.shape
    h = r // 2
    th = _row_tile(h, c * 2, 4 * 2**20)

    def body(core_ref, a_ref, b_ref, o_ref):
        o_ref[...] = (a_ref[...].astype(F32) + b_ref[...].astype(F32)).astype(BF16)

    grid_spec = pltpu.PrefetchScalarGridSpec(
        num_scalar_prefetch=1, grid=(p, h // th),
        in_specs=[pl.BlockSpec((None, None, th, c), lambda pi, i, core_ref: (pi, core_ref[0], i, 0)),
                  pl.BlockSpec((None, th, c), lambda pi, i, core_ref: (pi, i, 0))],
        out_specs=pl.BlockSpec((None, th, c), lambda pi, i, core_ref: (pi, i, 0)))
    return pl.pallas_call(
        body, name=name, grid_spec=grid_spec, out_shape=jax.ShapeDtypeStruct((p, h, c), BF16),
        compiler_params=_params(dimension_semantics=("arbitrary", "arbitrary")),
    )(core, dw.reshape(p, 2, h, c), other)


def sum_leading(name, parts):
    n, r, c = parts.shape
    tr = _row_tile(r, c * 4 * 2, 2 * 2**20)

    def body(p_ref, o_ref):
        acc = p_ref[0].astype(F32)
        for k in range(1, n):
            acc = acc + p_ref[k].astype(F32)
        o_ref[...] = acc

    return pl.pallas_call(
        body, name=name, grid=(r // tr,), in_specs=[pl.BlockSpec((n, tr, c), lambda i: (0, i, 0))],
        out_specs=pl.BlockSpec((tr, c), lambda i: (i, 0)), out_shape=jax.ShapeDtypeStruct((r, c), F32),
        compiler_params=_params(dimension_semantics=("arbitrary",)),
    )(parts)


def _adamw_rows(w, g, m, v):
    m = ADAM_B1 * m + (1.0 - ADAM_B1) * g
    v = ADAM_B2 * v + (1.0 - ADAM_B2) * (g * g)
    m_hat = m / (1.0 - ADAM_B1 ** ADAM_STEP)
    v_hat = v / (1.0 - ADAM_B2 ** ADAM_STEP)
    delta = -ADAM_LR * (m_hat / (jnp.sqrt(v_hat) + ADAM_EPS) + ADAM_WD * w)
    return delta, m, v


def adamw_layer(name, w, m, v, g, layer, carried):
    nl, r, c = w.shape
    tr = _row_tile(r, c * 4, 3 * 2**19)
    n_carried = 4 if carried is not None else 0

    def body(*refs):
        w_ref, m_ref, v_ref, g_ref = refs[:4]
        go_ref, d_ref, mo_ref, vo_ref = refs[4 + n_carried:]
        g = g_ref[...]
        delta, m_new, v_new = _adamw_rows(w_ref[...], g, m_ref[...], v_ref[...])
        go_ref[...] = g
        d_ref[...] = delta
        mo_ref[...] = m_new
        vo_ref[...] = v_new

    stacked = pl.BlockSpec((None, tr, c), lambda i: (layer, i, 0))
    in_specs = [stacked, stacked, stacked, pl.BlockSpec((tr, c), lambda i: (i, 0))]
    in_specs += [pl.BlockSpec(memory_space=pl.ANY)] * n_carried
    shape = jax.ShapeDtypeStruct((nl, r, c), F32)
    return pl.pallas_call(
        body, name=name, grid=(r // tr,), in_specs=in_specs, out_specs=[stacked] * 4, out_shape=[shape] * 4,
        input_output_aliases={4 + k: k for k in range(n_carried)},
        compiler_params=_params(dimension_semantics=("arbitrary",)),
    )(w, m, v, g, *(carried or ()))


def adamw_flat(name, w, m, v, g):
    r, c = w.shape

    def body(w_ref, m_ref, v_ref, g_ref, d_ref, mo_ref, vo_ref):
        delta, m_new, v_new = _adamw_rows(w_ref[...], g_ref[...], m_ref[...], v_ref[...])
        d_ref[...] = delta
        mo_ref[...] = m_new
        vo_ref[...] = v_new

    shape = jax.ShapeDtypeStruct((r, c), F32)
    return pl.pallas_call(body, name=name, out_shape=[shape] * 3, compiler_params=_params())(w, m, v, g)


def cast_place(name, w, layer, place):
    nl, r, c = w.shape
    tr = _row_tile(r, c * 4, 8 * 2**20)

    def body(x_ref, y_ref, c_ref, w_ref, o_ref):
        o_ref[...] = w_ref[...].astype(BF16)

    grid_spec = pltpu.PrefetchScalarGridSpec(
        num_scalar_prefetch=3, grid=(r // tr,),
        in_specs=[pl.BlockSpec((None, tr, c), lambda i, x_ref, y_ref, c_ref: (layer, i, 0))],
        out_specs=pl.BlockSpec((None, tr, c), lambda i, x_ref, y_ref, c_ref: (2 * x_ref[0] + y_ref[0], i, 0)))
    return pl.pallas_call(
        body, name=name, grid_spec=grid_spec, out_shape=jax.ShapeDtypeStruct((N_CHIPS, r, c), BF16),
        compiler_params=_params(dimension_semantics=("arbitrary",)),
    )(*place, w)


def reduce_sum4(name, own, landed, place):
    p, h, c = own.shape
    tr = _row_tile(h, c * 4, 4 * 2**20)

    def body(x_ref, y_ref, c_ref, t_ref, y1_ref, y2_ref, y3_ref, o_ref):
        acc = t_ref[...].astype(F32)
        for part_ref in (y1_ref, y2_ref, y3_ref):
            acc = acc + part_ref[...].astype(F32)
        o_ref[...] = acc

    def panel(fx, fy):
        return pl.BlockSpec((None, tr, c), lambda i, x_ref, y_ref, c_ref: (
            2 * (1 - x_ref[0] if fx else x_ref[0]) + (1 - y_ref[0] if fy else y_ref[0]), i, 0))

    grid_spec = pltpu.PrefetchScalarGridSpec(
        num_scalar_prefetch=3, grid=(h // tr,),
        in_specs=[panel(0, 0), panel(1, 0), panel(0, 1), panel(1, 1)],
        out_specs=pl.BlockSpec((None, tr, c), lambda i, x_ref, y_ref, c_ref: (c_ref[0], i, 0)))
    return pl.pallas_call(
        body, name=name, grid_spec=grid_spec, out_shape=jax.ShapeDtypeStruct((2, h, c), F32),
        compiler_params=_params(dimension_semantics=("arbitrary",)),
    )(*place, own, landed, landed, landed)


def run_exchange(name, exchange):
    n_in, n_out = len(exchange.inputs), len(exchange.out_shapes)

    def body(*refs):
        ins, outs, sems = refs[:n_in], refs[n_in:n_in + n_out], refs[n_in + n_out:]
        exchange.start(ins, outs, sems)
        exchange.finish(ins, outs, sems)

    return pl.pallas_call(
        body, name=name, in_specs=[ANY] * n_in, out_specs=[ANY] * n_out, out_shape=list(exchange.out_shapes),
        scratch_shapes=[pltpu.SemaphoreType.DMA((exchange.n_sems,)), pltpu.SemaphoreType.DMA((exchange.n_sems,))],
        input_output_aliases=dict(exchange.aliases),
    )(*exchange.inputs)


def _row_halves(ref, c):
    h = ref.shape[1] // 2
    return pl.ds(pl.multiple_of(c * h, 16), h), pl.ds(pl.multiple_of((1 - c) * h, 16), h)


def _in_place(arrays, n_sems, start, finish):
    return Exchange(list(arrays), [jax.ShapeDtypeStruct(f.shape, f.dtype) for f in arrays],
                    {a: a for a in range(len(arrays))}, n_sems, start, finish)


def gather_over_ici(fulls):
    n = len(fulls)

    def sends(outs, sems):
        x, y, c, mine, chips = _place()
        return [_remote(rows, rows, sems, 3 * a + j, (*chip, c)) for a in range(n)
                for rows in [outs[a].at[mine, _row_halves(outs[a], c)[0]]] for j, chip in enumerate(chips)]

    def start(ins, outs, sems):
        for cp in sends(outs, sems):
            cp.start()

    def finish(ins, outs, sems):
        x, y, c, mine, chips = _place()
        for a in range(n):
            for j, chip in enumerate(chips):
                rows = outs[a].at[2 * chip[0] + chip[1], _row_halves(outs[a], c)[0]]
                _remote(rows, rows, sems, 3 * a + j, (*chip, c)).wait_recv()
        for cp in sends(outs, sems):
            cp.wait_send()

    return _in_place(fulls, 3 * n, start, finish)


def gather_over_ici_by_neighbours(fulls):
    n = len(fulls)

    def plan(outs, a):
        x, y, c, mine, (nbr_x, nbr_y, far) = _place()
        h = outs[a].shape[1] // 2
        first = pl.ds(pl.multiple_of(c * h, 16), h // 2)
        second = pl.ds(pl.multiple_of(c * h + h // 2, 16), h // 2)
        index = lambda chip: 2 * chip[0] + chip[1]
        return c, mine, nbr_x, nbr_y, index, _row_halves(outs[a], c)[0], first, second, index(far)

    def direct(outs, sems, a):
        c, mine, nbr_x, nbr_y, _, half, _, _, _ = plan(outs, a)
        rows = outs[a].at[mine, half]
        return [_remote(rows, rows, sems, 4 * a, (*nbr_x, c)), _remote(rows, rows, sems, 4 * a + 1, (*nbr_y, c))]

    def passed_on(outs, sems, a):
        c, _, nbr_x, nbr_y, index, _, first, second, _ = plan(outs, a)
        from_x, from_y = outs[a].at[index(nbr_x), first], outs[a].at[index(nbr_y), second]
        return [_remote(from_x, from_x, sems, 4 * a + 2, (*nbr_y, c)), _remote(from_y, from_y, sems, 4 * a + 3, (*nbr_x, c))]

    def start(ins, outs, sems):
        for a in range(n):
            for cp in direct(outs, sems, a):
                cp.start()

    def finish(ins, outs, sems):
        for a in range(n):
            c, _, nbr_x, nbr_y, index, half, first, second, far = plan(outs, a)
            for k, nbr in enumerate((nbr_x, nbr_y)):
                rows = outs[a].at[index(nbr), half]
                _remote(rows, rows, sems, 4 * a + k, (*nbr, c)).wait_recv()
            for cp in passed_on(outs, sems, a):
                cp.start()
        for a in range(n):
            c, _, nbr_x, nbr_y, index, half, first, second, far = plan(outs, a)
            for k, (rows, nbr) in enumerate(((outs[a].at[far, first], nbr_y), (outs[a].at[far, second], nbr_x))):
                _remote(rows, rows, sems, 4 * a + 2 + k, (*nbr, c)).wait_recv()
            for cp in direct(outs, sems, a) + passed_on(outs, sems, a):
                cp.wait_send()

    return _in_place(fulls, 4 * n, start, finish)


def gather_over_d2d(fulls):
    n = len(fulls)

    def copies(outs, sems, which):
        x, y, c, mine, chips = _place()
        return [_remote(rows, rows, sems, 3 * a + j, (x, y, 1 - c)) for a in range(n) for j, chip in enumerate(chips)
                for rows in [outs[a].at[2 * chip[0] + chip[1], _row_halves(outs[a], c)[which]]]]

    def start(ins, outs, sems):
        for cp in copies(outs, sems, 0):
            cp.start()

    def finish(ins, outs, sems):
        for cp in copies(outs, sems, 1):
            cp.wait_recv()
        for cp in copies(outs, sems, 0):
            cp.wait_send()

    return _in_place(fulls, 3 * n, start, finish)


def gather_whole(fulls):
    ici, d2d = gather_over_ici(fulls), gather_over_d2d(fulls)

    def finish(ins, outs, sems):
        ici.finish(ins, outs, sems)
        later = tuple(_SemaphoresFrom(s, ici.n_sems) for s in sems)
        d2d.start(ins, outs, later)
        d2d.finish(ins, outs, later)

    return _in_place(fulls, ici.n_sems + d2d.n_sems, ici.start, finish)


class _SemaphoresFrom:
    def __init__(self, ref, offset):
        self.ref, self.offset = ref, offset

    @property
    def at(self):
        return self

    def __getitem__(self, k):
        return self.ref.at[self.offset + k]


def combine(exchanges):
    exchanges = [e for e in exchanges if e is not None]
    if len(exchanges) <= 1:
        return exchanges[0] if exchanges else None
    inputs, out_shapes, aliases, spans, n_sems = [], [], {}, [], 0
    for e in exchanges:
        aliases.update({len(inputs) + i: len(out_shapes) + o for i, o in e.aliases.items()})
        spans.append((len(inputs), len(e.inputs), len(out_shapes), len(e.out_shapes), n_sems))
        inputs, out_shapes, n_sems = inputs + list(e.inputs), out_shapes + list(e.out_shapes), n_sems + e.n_sems

    def each(method):
        def run(ins, outs, sems):
            for e, (i0, ni, o0, no, s0) in zip(exchanges, spans, strict=True):
                getattr(e, method)(ins[i0:i0 + ni], outs[o0:o0 + no], tuple(_SemaphoresFrom(s, s0) for s in sems))
        return run

    return Exchange(inputs, out_shapes, aliases, n_sems, each("start"), each("finish"))


def swap_exchange(grads):
    n = len(grads)

    def copies(ins, outs, sems):
        x, y, c, _, _ = _place()
        return [_remote(ins[a].at[:, _row_halves(ins[a], c)[1]], outs[a], sems, a, (x, y, 1 - c)) for a in range(n)]

    def start(ins, outs, sems):
        for cp in copies(ins, outs, sems):
            cp.start()

    def finish(ins, outs, sems):
        for cp in copies(ins, outs, sems):
            cp.wait()

    shapes = [jax.ShapeDtypeStruct((g.shape[0], g.shape[1] // 2, g.shape[2]), g.dtype) for g in grads]
    return Exchange(list(grads), shapes, {}, n, start, finish)


def scatter_exchange(parts):
    n = len(parts)

    def sends(ins, outs, sems):
        x, y, c, mine, chips = _place()
        return [_remote(ins[a].at[2 * chip[0] + chip[1]], outs[a].at[mine], sems, 3 * a + j, (*chip, c))
                for a in range(n) for j, chip in enumerate(chips)]

    def start(ins, outs, sems):
        for cp in sends(ins, outs, sems):
            cp.start()

    def finish(ins, outs, sems):
        x, y, c, mine, chips = _place()
        for a in range(n):
            for j, chip in enumerate(chips):
                landing = outs[a].at[2 * chip[0] + chip[1]]
                _remote(landing, landing, sems, 3 * a + j, (*chip, c)).wait_recv()
        for cp in sends(ins, outs, sems):
            cp.wait_send()

    return Exchange(list(parts), [jax.ShapeDtypeStruct(g.shape, g.dtype) for g in parts], {}, 3 * n, start, finish)


def join_exchange(halves):
    n = len(halves)

    def start(ins, outs, sems):
        x, y, c, _, _ = _place()
        for a in range(n):
            _remote(outs[a].at[c], outs[a].at[c], sems, a, (x, y, 1 - c)).start()

    def finish(ins, outs, sems):
        x, y, c, _, _ = _place()
        for a in range(n):
            _remote(outs[a].at[1 - c], outs[a].at[1 - c], sems, a, (x, y, 1 - c)).wait_recv()
        for a in range(n):
            _remote(outs[a].at[c], outs[a].at[c], sems, a, (x, y, 1 - c)).wait_send()

    return Exchange(list(halves), [jax.ShapeDtypeStruct(g.shape, g.dtype) for g in halves], {a: a for a in range(n)},
                    n, start, finish)


def gather_all(name, rows):
    def body(in_ref, out_ref, send_sems, recv_sems, local_sem):
        sems = (send_sems, recv_sems)
        x, y, c, _, _ = _place()
        me = 4 * x + 2 * y + c
        local = pltpu.make_async_copy(in_ref, out_ref.at[me], local_sem)
        local.start()
        peers = [(1 - x if k & 4 else x, 1 - y if k & 2 else y, 1 - c if k & 1 else c) for k in range(1, N_DEV)]
        sent = []
        for k, peer in enumerate(peers):
            cp = _remote(in_ref, out_ref.at[me], sems, k, peer)
            cp.start()
            sent.append(cp)
        for k, peer in enumerate(peers):
            landing = out_ref.at[4 * peer[0] + 2 * peer[1] + peer[2]]
            _remote(landing, landing, sems, k, peer).wait_recv()
        for cp in sent:
            cp.wait_send()
        local.wait()

    return pl.pallas_call(
        body, name=name, in_specs=[ANY], out_specs=ANY,
        out_shape=jax.ShapeDtypeStruct((N_DEV,) + rows.shape, rows.dtype),
        scratch_shapes=[pltpu.SemaphoreType.DMA((N_DEV - 1,)), pltpu.SemaphoreType.DMA((N_DEV - 1,)),
                        pltpu.SemaphoreType.DMA],
    )(rows)


class _Step:
    def __init__(self, p):
        self.p = p
        xi, yi, ci = lax.axis_index("x"), lax.axis_index("y"), lax.axis_index("c")
        self.chip = 2 * xi + yi
        self.place_refs = tuple(v.astype(jnp.int32).reshape(1) for v in (xi, yi, ci))
        self.core_ref = self.place_refs[2]
        self.depth = p['ffn1_norm'].shape[0]
        self.placed, self.landed, self.w = {}, {}, {}
        self.big_g = {}
        self.waiting_joins = []
        self.waiting_scatter = None

    def block_keys(self, tag, l):
        if l >= self.depth:
            return []
        mixer = ['gmlp_w_in', 'gmlp_w_out'] if l % 2 == 0 else ['conv_w_in', 'conv_w_out']
        names = {"ffn1": ['ffn1_w13', 'ffn1_w2'], "mix": mixer, "xattn": ['xattn_wq', 'xattn_wkv', 'xattn_wo'],
                 "ffn2": ['ffn2_w13', 'ffn2_w2']}[tag]
        return [(n, l // 2 if tag == "mix" else l) for n in names]

    def place(self, keys):
        for n, idx in keys:
            self.placed[(n, idx)] = cast_place(f"place_{n}{idx}", self.p[n], idx, self.place_refs)

    def carrying_gather(self, call, over_ici, over_d2d, whole, *args, **kw):
        over_ici = [k for k in over_ici if k in self.placed]
        over_d2d = [k for k in over_d2d if k in self.landed]
        parts = [gather_over_ici([self.placed.pop(k) for k in over_ici]) if over_ici else None,
                 gather_over_d2d([self.landed.pop(k) for k in over_d2d]) if over_d2d else None,
                 gather_whole([self.placed.pop(k) for k in whole]) if whole else None]
        exchange = combine(parts)
        if exchange is None:
            return call(*args, **kw)
        out, got = call(*args, exchange=exchange, **kw)
        self.landed.update(zip(over_ici, got[:len(over_ici)], strict=True))
        self.w.update(zip(over_d2d + whole, got[len(over_ici):], strict=True))
        return out

    def reduce_begin(self, tag, keys, dws, theirs=None):
        theirs = list(theirs or [None] * len(dws))
        todo = [i for i, t in enumerate(theirs) if t is None]
        for i, t in zip(todo, run_exchange(tag + "_swap", swap_exchange([dws[i] for i in todo])), strict=True):
            theirs[i] = t
        parts = [add_halves(f"{tag}_add{i}", dw, t, self.core_ref) for i, (dw, t) in enumerate(zip(dws, theirs, strict=True))]
        assert self.waiting_scatter is None
        self.waiting_scatter = (tag, keys, parts)

    def carrying_scatter(self, mm, *args, **kw):
        tag, keys, parts = self.waiting_scatter
        self.waiting_scatter = None
        out, landed = mm(*args, exchange=scatter_exchange(parts), **kw)
        halves = [reduce_sum4(f"{tag}_sum{i}", t, y, self.place_refs) for i, (t, y) in enumerate(zip(parts, landed, strict=True))]
        self.waiting_joins += list(zip(keys, halves, strict=True))
        return out

    def take_joined(self, keys, joined):
        for k, g in zip(keys, joined, strict=True):
            self.big_g[k] = g.reshape(-1, g.shape[-1])

    def carrying_joins(self, mm, *args, **kw):
        if not self.waiting_joins:
            return mm(*args, **kw)
        keys, halves = zip(*self.waiting_joins, strict=True)
        self.waiting_joins = []
        out, joined = mm(*args, exchange=join_exchange(list(halves)), **kw)
        self.take_joined(keys, joined)
        return out

    def joins_alone(self, name):
        keys, halves = zip(*self.waiting_joins, strict=True)
        self.waiting_joins = []
        self.take_joined(keys, run_exchange(name, join_exchange(list(halves))))

    def ffn_fwd(self, tag, l, x, gain, carry_norm, carry13, carry2, h=None, next_gain=None):
        name = f"l{l}_{tag}"
        if h is None:
            h = self.carrying_gather(rms_fwd, *carry_norm, name + "_norm", x, gain)
        by_gate, by_up, act = self.carrying_gather(mm_swiglu, *carry13, name + "_w13", h, self.w[(tag + '_w13', l)])
        out = self.carrying_gather(mm_nn, *carry2, name + "_w2", act, self.w[(tag + '_w2', l)], 'row', F32, res=x,
                                   scale=0.5, norm_gain=next_gain)
        return out, (x, h, by_gate, by_up, act)

    def ffn_bwd(self, tag, l, dx, dxb, saved, gain):
        w13, w2 = self.w[(tag + '_w13', l)], self.w[(tag + '_w2', l)]
        name = f"l{l}_{tag}"
        x, h, by_gate, by_up, act = saved
        d_gate, d_up = self.carrying_joins(mm_dswiglu, name + "_dact", dxb, w2, by_gate, by_up, 0.5)
        d_w2 = mm_tn(name + "_dw2", act, dxb, 'row', scale=0.5)
        half = N_CHIPS // 2
        d_w13, their_w2 = mm_tn(name + "_dw13g", h, d_gate, 'col', panels=(0, half), exchange=swap_exchange([d_w2]))
        d_w13 = mm_tn(name + "_dw13u", h, d_up, 'col', panels=(half, half), into=d_w13)
        self.reduce_begin(name, [(tag + '_w13', l), (tag + '_w2', l)], [d_w13, d_w2], [None] + their_w2)
        dh = self.carrying_scatter(mm_nt, name + "_dh", d_gate, w13, 'col', BF16, a_hi=d_up)
        return rms_bwd(name + "_dnorm", x, gain, dh, dx)


def kernel(x, mem, ffn1_norm, ffn1_w13, ffn1_w2, mix_norm, gmlp_w_in, gmlp_ln_g, gmlp_ln_b, gmlp_w_s, gmlp_b_s, gmlp_w_out, conv_w_in, conv_w, conv_w_out, xattn_norm, mem_norm, xattn_wq, xattn_wkv, xattn_wo, ffn2_norm, ffn2_w13, ffn2_w2, final_norm, loss_target, m_ffn1_norm, m_ffn1_w13, m_ffn1_w2, m_mix_norm, m_gmlp_w_in, m_gmlp_ln_g, m_gmlp_ln_b, m_gmlp_w_s, m_gmlp_b_s, m_gmlp_w_out, m_conv_w_in, m_conv_w, m_conv_w_out, m_xattn_norm, m_mem_norm, m_xattn_wq, m_xattn_wkv, m_xattn_wo, m_ffn2_norm, m_ffn2_w13, m_ffn2_w2, m_final_norm, v_ffn1_norm, v_ffn1_w13, v_ffn1_w2, v_mix_norm, v_gmlp_w_in, v_gmlp_ln_g, v_gmlp_ln_b, v_gmlp_w_s, v_gmlp_b_s, v_gmlp_w_out, v_conv_w_in, v_conv_w, v_conv_w_out, v_xattn_norm, v_mem_norm, v_xattn_wq, v_xattn_wkv, v_xattn_wo, v_ffn2_norm, v_ffn2_w13, v_ffn2_w2, v_final_norm):
    return _step(dict(locals()))


def _step(p):
    assert sorted(p) == sorted(ARG_NAMES)
    st = _Step(p)
    x = p['x'][0]
    mem = p['mem'][0]
    target = p['loss_target'][0]
    s, d = x.shape
    depth = st.depth

    for l in range(depth):
        for tag in ("ffn1", "mix", "xattn", "ffn2"):
            st.place(st.block_keys(tag, l))
    first = ('ffn1_w13', 0)
    st.landed[first] = run_exchange("gather_first", gather_over_ici_by_neighbours([st.placed.pop(first)]))[0]

    cw_local = p['conv_w']
    n_conv, cwid, dq4 = cw_local.shape
    cw_rows = jnp.pad(cw_local.reshape(-1, LANE), ((0, (-cw_local.size // LANE) % 8), (0, 0)))
    cw_all = gather_all("gather_conv_w", cw_rows)[0::2, :cw_local.size // LANE]
    conv_w_full = cw_all.reshape(N_CHIPS, n_conv, cwid, dq4).transpose(1, 2, 0, 3).reshape(n_conv, cwid, N_CHIPS * dq4)

    saved, h_ffn1 = [], None
    for l in range(depth):
        j = l // 2
        rec = {}
        soon = st.block_keys("mix", l) + st.block_keys("xattn", l)
        (x, h), rec['ffn1'] = st.ffn_fwd("ffn1", l, x, p['ffn1_norm'][l],
                                         ([], [('ffn1_w13', 0)] if l == 0 else [], []),
                                         (soon, [], [('ffn1_w2', 0)] if l == 0 else []),
                                         ([('ffn2_w13', l)], soon, []), h=h_ffn1, next_gain=p['mix_norm'][l])
        if l % 2 == 0:
            e = p['gmlp_ln_g'].shape[-1]
            bias = jnp.repeat(p['gmlp_b_s'][j].T, e // GMLP_GROUPS, axis=1)
            z = st.carrying_gather(mm_nn, [('ffn2_w2', l)], [('ffn2_w13', l)], [], f"l{l}_gmlp_in", h,
                                   st.w[('gmlp_w_in', j)], 'col', BF16)
            gate = gmlp_fwd(f"l{l}_gmlp_gate", z, p['gmlp_ln_g'][j], p['gmlp_ln_b'][j], p['gmlp_w_s'][j], bias)
            x_new, hq = st.carrying_gather(mm_nn, [], [('ffn2_w2', l)], [], f"l{l}_gmlp_out", gate,
                                           st.w[('gmlp_w_out', j)], 'row', F32, res=x, norm_gain=p['xattn_norm'][l])
            rec['mix'] = (x, h, z, gate, bias)
        else:
            bcv = st.carrying_gather(mm_nn, [('ffn2_w2', l)], [('ffn2_w13', l)], [], f"l{l}_conv_in", h,
                                     st.w[('conv_w_in', j)], 'col', BF16)
            gate = conv_fwd(f"l{l}_conv_gate", bcv, conv_w_full[j])
            x_new, hq = st.carrying_gather(mm_nn, [], [('ffn2_w2', l)], [], f"l{l}_conv_out", gate,
                                           st.w[('conv_w_out', j)], 'row', F32, res=x, norm_gain=p['xattn_norm'][l])
            rec['mix'] = (x, h, bcv, gate)
        x = x_new
        q = mm_nn(f"l{l}_xattn_q", hq, st.w[('xattn_wq', l)], 'row', BF16)
        mem_n = rms_fwd(f"l{l}_mem_norm", mem, p['mem_norm'][l])
        kv = mm_nn(f"l{l}_xattn_kv", mem_n, st.w[('xattn_wkv', l)], 'col', BF16)
        o = attn_fwd(f"l{l}_xattn_core", q, kv)
        x_new, h_ffn2 = mm_nn(f"l{l}_xattn_o", o, st.w[('xattn_wo', l)], 'row', F32, res=x,
                              norm_gain=p['ffn2_norm'][l])
        rec['xattn'] = (x, hq, q, mem_n, kv, o)
        x = x_new
        ahead = st.block_keys("ffn1", l + 1)
        last = l + 1 == depth
        out, rec['ffn2'] = st.ffn_fwd("ffn2", l, x, p['ffn2_norm'][l], None, (ahead, [], []), ([], ahead, []), h=h_ffn2,
                                      next_gain=None if last else p['ffn1_norm'][l + 1])
        x, h_ffn1 = (out, None) if last else out
        saved.append(rec)

    dx, dxb, d_final, loss_lanes = loss_head("loss_head", x, p['final_norm'], target)
    loss = lax.psum(0.5 * jnp.sum(loss_lanes) / d, ("x", "y", "c"))

    small = {n: [None] * p[n].shape[0] for n in ('ffn1_norm', 'mix_norm', 'xattn_norm', 'mem_norm', 'ffn2_norm',
                                                  'gmlp_ln_g', 'gmlp_ln_b', 'gmlp_w_s', 'gmlp_b_s', 'conv_w')}
    for l in reversed(range(depth)):
        j = l // 2
        rec = saved[l]
        dx, dxb, small['ffn2_norm'][l] = st.ffn_bwd("ffn2", l, dx, dxb, rec['ffn2'], p['ffn2_norm'][l])

        x_in, hq, q, mem_n, kv, o = rec['xattn']
        name = f"l{l}_xattn"
        do = st.carrying_joins(mm_nt, name + "_do", dxb, st.w[('xattn_wo', l)], 'row', BF16)
        d_wo = mm_tn(name + "_dwo", o, dxb, 'row')
        dq, dkv = attn_bwd(name + "_dcore", q, kv, do)
        d_wq = mm_tn(name + "_dwq", hq, dq, 'row')
        dkvb = dkv.astype(BF16)
        d_wkv = mm_tn(name + "_dwkv", mem_n, dkvb, 'col')
        st.reduce_begin(name, [('xattn_wq', l), ('xattn_wkv', l), ('xattn_wo', l)], [d_wq, d_wkv, d_wo])
        dx, dxb, small['xattn_norm'][l] = mm_nt(name + "_dh", dq, st.w[('xattn_wq', l)], 'row', F32,
                                                norm_back=(x_in, p['xattn_norm'][l], dx))
        dmem_n = mm_nt(name + "_dmem", dkvb, st.w[('xattn_wkv', l)], 'col', F32)
        small['mem_norm'][l] = rms_bwd(f"l{l}_mem_dnorm", mem, p['mem_norm'][l], dmem_n, None)[2]

        if l % 2 == 0:
            x_in, h, z, gate, bias = rec['mix']
            name = f"l{l}_gmlp"
            w_in, w_out = st.w[('gmlp_w_in', j)], st.w[('gmlp_w_out', j)]
            dgate = mm_nt(name + "_dgate", dxb, w_out, 'row', BF16)
            d_wout = mm_tn(name + "_dwout", gate, dxb, 'row')
            dmix, dws, dbs, dlg, dlb = gmlp_bwd(name + "_dgate_core", z, dgate, p['gmlp_ln_g'][j], p['gmlp_ln_b'][j],
                                                p['gmlp_w_s'][j], bias)
            small['gmlp_w_s'][j], small['gmlp_b_s'][j] = dws, dbs[:, :GMLP_GROUPS].T
            small['gmlp_ln_g'][j], small['gmlp_ln_b'][j] = dlg, dlb
            keys = [('gmlp_w_in', j), ('gmlp_w_out', j)]
        else:
            x_in, h, bcv, gate = rec['mix']
            name = f"l{l}_conv"
            w_in, w_out = st.w[('conv_w_in', j)], st.w[('conv_w_out', j)]
            dgate = mm_nt(name + "_dgate", dxb, w_out, 'row', BF16)
            d_wout = mm_tn(name + "_dwout", gate, dxb, 'row')
            dmix, dcw = conv_bwd(name + "_dgate_core", bcv, dgate, conv_w_full[j])
            small['conv_w'][j] = dcw[:cwid]
            keys = [('conv_w_in', j), ('conv_w_out', j)]
        d_win = st.carrying_scatter(mm_tn, name + "_dwin", h, dmix, 'col')
        st.reduce_begin(name, keys, [d_win, d_wout])
        dx, dxb, small['mix_norm'][l] = st.carrying_scatter(mm_nt, name + "_dh", dmix, w_in, 'col', F32,
                                                            norm_back=(x_in, p['mix_norm'][l], dx))

        dx, dxb, small['ffn1_norm'][l] = st.ffn_bwd("ffn1", l, dx, dxb, rec['ffn1'], p['ffn1_norm'][l])
    st.joins_alone("join_last")

    small_names = ['ffn1_norm', 'mix_norm', 'xattn_norm', 'mem_norm', 'ffn2_norm', 'gmlp_ln_g', 'gmlp_ln_b', 'gmlp_w_s',
                   'gmlp_b_s', 'final_norm', 'conv_w']
    small_full = {n: jnp.stack([g.reshape(p[n].shape[1:]) for g in small[n]]) for n in small_names
                  if n not in ('final_norm', 'conv_w')}
    small_full['final_norm'] = d_final.reshape(p['final_norm'].shape)
    small_full['conv_w'] = jnp.stack(small['conv_w'])
    packed = jnp.concatenate([small_full[n].reshape(-1, LANE) for n in small_names], axis=0)
    total = sum_leading("small_sum", gather_all("small_gather", packed))
    small_g, at = {}, 0
    for n in small_names:
        rows = small_full[n].size // LANE
        small_g[n] = total[at:at + rows].reshape(small_full[n].shape)
        at += rows
    small_g['conv_w'] = lax.dynamic_slice_in_dim(small_g['conv_w'], st.chip * dq4, dq4, axis=2)

    grads, deltas, new_m, new_v = {}, {}, {}, {}
    for n in WEIGHTS:
        w, m, v = p[n], p['m_' + n], p['v_' + n]
        if n in BIG:
            carried = None
            for i in range(w.shape[0]):
                carried = adamw_layer(f"adamw_{n}{i}", w, m, v, st.big_g[(n, i)], i, carried)
            grads[n], deltas[n], new_m[n], new_v[n] = carried
        else:
            g = small_g[n]
            out = adamw_flat(f"adamw_{n}", _as_rows(w), _as_rows(m), _as_rows(v), _as_rows(g))
            grads[n] = g
            deltas[n], new_m[n], new_v[n] = (o.reshape(w.shape) for o in out)

    grad_x = dx.reshape(p['x'].shape)
    return (loss, grad_x, *[grads[n] for n in WEIGHTS], *[deltas[n] for n in WEIGHTS], *[new_m[n] for n in WEIGHTS],
            *[new_v[n] for n in WEIGHTS])
```

```python
from typing import Callable, NamedTuple

import jax
import jax.numpy as jnp
from jax import lax
from jax.experimental import pallas as pl
from jax.experimental.pallas import tpu as pltpu

F32 = jnp.float32
BF16 = jnp.bfloat16
MESH = pl.DeviceIdType.MESH

CHUNK = 128
GMLP_GROUPS = 8
XATTN_HEADS = 4
RMS_EPS = 1e-6
LN_EPS = 1e-5
ADAM_LR = 0.001
ADAM_B1 = 0.9
ADAM_B2 = 0.999
ADAM_EPS = 1e-08
ADAM_WD = 0.01
ADAM_STEP = 10

N_CHIPS = 4
N_DEV = 8

VMEM_LIMIT_BYTES = 60 * 2**20
VMEM_PLAN_BYTES = 52 * 2**20
LANE = 128
MXU_DIM = 256
ACC_CHUNK = 2 * MXU_DIM
MXU_FLOPS_PER_US = 996e6
HBM_BYTES_PER_US = 3.3e6
STEP_US = 0.35
ACC_US_PER_VREG = 0.58e-3

WEIGHTS = ['ffn1_norm', 'ffn1_w13', 'ffn1_w2', 'mix_norm', 'gmlp_w_in', 'gmlp_ln_g', 'gmlp_ln_b', 'gmlp_w_s',
           'gmlp_b_s', 'gmlp_w_out', 'conv_w_in', 'conv_w', 'conv_w_out', 'xattn_norm', 'mem_norm', 'xattn_wq',
           'xattn_wkv', 'xattn_wo', 'ffn2_norm', 'ffn2_w13', 'ffn2_w2', 'final_norm']
BIG = {'ffn1_w13': 'col', 'ffn1_w2': 'row', 'gmlp_w_in': 'col', 'gmlp_w_out': 'row', 'conv_w_in': 'col',
       'conv_w_out': 'row', 'xattn_wq': 'row', 'xattn_wkv': 'col', 'xattn_wo': 'row', 'ffn2_w13': 'col',
       'ffn2_w2': 'row'}
ARG_NAMES = (['x', 'mem'] + WEIGHTS + ['loss_target'] + ['m_' + n for n in WEIGHTS] + ['v_' + n for n in WEIGHTS])


def _params(**kw):
    return pltpu.CompilerParams(vmem_limit_bytes=VMEM_LIMIT_BYTES, **kw)


def _divisors(n, mult, cap):
    return [d for d in range(mult, min(n, cap) + 1, mult) if n % d == 0] or [n]


def _row_tile(rows, width_bytes, budget=4 * 2**20):
    best = None
    for d in _divisors(rows, 16, 1024):
        if d * width_bytes <= budget:
            best = d
    return best or _divisors(rows, 16, 1024)[0]


ANY = pl.BlockSpec(memory_space=pl.ANY)


class Exchange(NamedTuple):
    inputs: list
    out_shapes: list
    aliases: dict
    n_sems: int
    start: Callable
    finish: Callable


def _place():
    x, y, c = lax.axis_index("x"), lax.axis_index("y"), lax.axis_index("c")
    chips = [(1 - x, y), (x, 1 - y), (1 - x, 1 - y)]
    return x, y, c, 2 * x + y, chips


def _remote(src, dst, sems, k, device):
    return pltpu.make_async_remote_copy(src_ref=src, dst_ref=dst, send_sem=sems[0].at[k], recv_sem=sems[1].at[k],
                                        device_id=device, device_id_type=MESH)


def _mxu_fill(dim):
    return dim / (-(-dim // MXU_DIM) * MXU_DIM)


def _tile_time(flops, fill, traffic, steps, acc_vregs):
    return (max(flops / (MXU_FLOPS_PER_US * fill), traffic / HBM_BYTES_PER_US) + steps * STEP_US
            + steps * acc_vregs * ACC_US_PER_VREG)


def _plan_mm(m, n_tiles_of, k_tiles_of, n, k, a_item, o_item, has_res, a_arrays=1):
    best, best_cost = None, None
    for tm in _divisors(m, 16, 1024):
        for tn in n_tiles_of:
            for tk in k_tiles_of:
                ni, nj, nk = m // tm, n // tn, k // tk
                blocks = a_arrays * tm * tk * a_item + tk * tn * 2 + tm * tn * o_item + (tm * tn * 4 if has_res else 0)
                vmem = 2 * blocks + tm * tn * 4 * (2 if nk > 1 else 1)
                if vmem > VMEM_PLAN_BYTES:
                    continue
                traffic = nj * m * k * a_item + (k * n * 2 if nk == 1 else ni * k * n * 2)
                traffic += m * n * (o_item + (4 if has_res else 0))
                cost = _tile_time(2 * m * n * k, _mxu_fill(tk) * _mxu_fill(tn), traffic, ni * nj * nk,
                                  tm * tn // 1024 if nk > 1 else 0)
                if best_cost is None or cost < best_cost:
                    best, best_cost = (tm, tn, tk), cost
    assert best is not None, (m, n, k)
    return best


def _tiled_call(name, grid, operands, in_specs, out_shapes, out_specs, scratch, compute, exchange=None, aliases=None):
    n_reg, n_out, n_scr = len(operands), len(out_shapes), len(scratch)
    n_xin = len(exchange.inputs) if exchange else 0
    n_xout = len(exchange.out_shapes) if exchange else 0
    semantics = ("arbitrary",) * len(grid)

    def body(*refs):
        ins = refs[:n_reg]
        outs = refs[n_reg + n_xin:n_reg + n_xin + n_out]
        scr = refs[n_reg + n_xin + n_out + n_xout:n_reg + n_xin + n_out + n_xout + n_scr]
        if not exchange:
            compute(ins, outs, scr)
            return
        x_ins = refs[n_reg:n_reg + n_xin]
        x_outs = refs[n_reg + n_xin + n_out:n_reg + n_xin + n_out + n_xout]
        sems = refs[-2:]
        at_first, at_last = True, True
        for k, extent in enumerate(grid):
            at_first = jnp.logical_and(at_first, pl.program_id(k) == 0)
            at_last = jnp.logical_and(at_last, pl.program_id(k) == extent - 1)

        @pl.when(at_first)
        def _():
            exchange.start(x_ins, x_outs, sems)

        compute(ins, outs, scr)

        @pl.when(at_last)
        def _():
            exchange.finish(x_ins, x_outs, sems)

    if not exchange:
        return pl.pallas_call(
            body, name=name, grid=grid, in_specs=in_specs, out_specs=out_specs, out_shape=out_shapes,
            scratch_shapes=scratch, input_output_aliases=dict(aliases or {}),
            compiler_params=_params(dimension_semantics=semantics),
        )(*operands)
    assert not aliases
    sems = [pltpu.SemaphoreType.DMA((exchange.n_sems,)), pltpu.SemaphoreType.DMA((exchange.n_sems,))]
    got = pl.pallas_call(
        body, name=name, grid=grid, in_specs=in_specs + [ANY] * n_xin, out_specs=out_specs + [ANY] * n_xout,
        out_shape=out_shapes + list(exchange.out_shapes), scratch_shapes=scratch + sems,
        input_output_aliases={n_reg + i: n_out + o for i, o in exchange.aliases.items()},
        compiler_params=_params(dimension_semantics=semantics),
    )(*operands, *exchange.inputs)
    return list(got[:n_out]), list(got[n_out:])


class Split(NamedTuple):
    slot: int
    other: jax.Array
    spec: pl.BlockSpec
    use_other: Callable


def _mm_call(name, grid, operands, in_specs, out_shape, out_spec, contract, nk, scale, has_res, tile, exchange=None,
             split=None, into=None, normed=False, norm_back=False):
    n_main = len(operands)
    out_shapes, out_specs = [out_shape], [out_spec]
    if normed:
        out_shapes, out_specs = out_shapes + [jax.ShapeDtypeStruct(out_shape.shape, BF16)], out_specs + [out_spec]
    if norm_back:
        m, n = out_shape.shape
        out_shapes = [jax.ShapeDtypeStruct((m, n), F32), jax.ShapeDtypeStruct((m, n), BF16),
                      jax.ShapeDtypeStruct((1, n), F32)]
        out_specs = [out_spec, out_spec, pl.BlockSpec((1, n), lambda j, i, kk: (0, 0))]

    def compute(ins, outs, scr):
        res_ref = ins[2] if has_res else None
        o_ref = outs[0]
        acc_ref = scr[0] if nk > 1 else None

        def finish_norm_back(v):
            dx, dg = _rms_bwd_rows(ins[2][...], ins[3][...], v)
            dx = dx + ins[4][...]
            outs[0][...] = dx
            outs[1][...] = dx.astype(BF16)
            first_rows = pl.program_id(1) == 0

            @pl.when(first_rows)
            def _():
                outs[2][...] = dg

            @pl.when(jnp.logical_not(first_rows))
            def _():
                outs[2][...] += dg

        def finish(v):
            if scale != 1.0:
                v = v * scale
            if norm_back:
                finish_norm_back(v)
                return
            if has_res:
                v = res_ref[...] + v
            o_ref[...] = v.astype(o_ref.dtype)
            if normed:
                outs[1][...] = _rms_rows(v, ins[2 + has_res][...])[2].astype(BF16)

        def b_block(cols):
            ref = b_ref_of[0]
            stacked = len(ref.shape) == 3
            if contract[0][1] == (1,):
                if not stacked:
                    return ref[cols, :]
                return ref[...].reshape(ref.shape[0] * ref.shape[1], ref.shape[2])[cols, :]
            if not stacked:
                return ref[:, cols]
            b = ref[:, :, cols]
            return b.reshape(b.shape[0] * b.shape[1], b.shape[2])

        b_ref_of = [None]

        def contribute(a_ref, b_ref):
            b_ref_of[0] = b_ref
            if nk == 1:
                finish(lax.dot_general(a_ref[...], b_block(slice(None)), contract, preferred_element_type=F32))
                return
            kk = pl.program_id(2)

            @pl.when(kk == 0)
            def _():
                acc_ref[...] = jnp.zeros(tile, F32)

            a = a_ref[...]
            for start in range(0, tile[1], ACC_CHUNK):
                cols = slice(start, min(start + ACC_CHUNK, tile[1]))
                acc_ref[:, cols] += lax.dot_general(a, b_block(cols), contract, preferred_element_type=F32)

            @pl.when(kk == nk - 1)
            def _():
                finish(acc_ref[...])

        if split is None:
            contribute(ins[0], ins[1])
            return
        use_other = split.use_other(pl.program_id(0), pl.program_id(1), pl.program_id(2))
        pair = [ins[0], ins[1]]
        other = list(pair)
        other[split.slot] = ins[n_main]

        @pl.when(jnp.logical_not(use_other))
        def _():
            contribute(*pair)

        @pl.when(use_other)
        def _():
            contribute(*other)

    aliases = None
    if split is not None:
        operands, in_specs = operands + [split.other], in_specs + [split.spec]
    if into is not None:
        aliases = {len(operands): 0}
        operands, in_specs = operands + [into], in_specs + [ANY]
    got = _tiled_call(name, grid, operands, in_specs, out_shapes, out_specs,
                      [pltpu.VMEM(tile, F32)] if nk > 1 else [], compute, exchange, aliases)
    results, carried = (got[0], got[1]) if exchange else (got, None)
    out = tuple(results) if normed or norm_back else results[0]
    return (out, carried) if exchange else out


def mm_nn(name, a, w, kind, out_dtype, res=None, scale=1.0, exchange=None, norm_gain=None):
    m, k = a.shape
    p, r, c = w.shape
    n = p * c if kind == 'col' else c
    assert k == (r if kind == 'col' else p * r), (name, a.shape, w.shape)
    n_tiles = _divisors(c, LANE, 2816)
    if norm_gain is not None:
        assert kind == 'row'
        n_tiles = [n]
    k_tiles = _divisors(r, LANE, 4096)
    if kind == 'row':
        k_tiles = k_tiles + [q * r for q in (2, 4) if p % q == 0]
    o_item = jnp.dtype(out_dtype).itemsize + (2 if norm_gain is not None else 0)
    tm, tn, tk = _plan_mm(m, n_tiles, k_tiles, n, k, a.dtype.itemsize, o_item, res is not None)
    buffering = None
    if norm_gain is not None:
        per_row = 2 * (k * a.dtype.itemsize + n * (o_item + (4 if res is not None else 0))) + n * 4
        fitting = [t for t in _divisors(m, 16, 1024) if t * per_row + k * n * 2 <= VMEM_PLAN_BYTES]
        if fitting:
            tm, tn, tk, buffering = fitting[-1], n, k, pl.Buffered(1)
    nk = k // tk
    if kind == 'col':
        cpt = c // tn
        w_spec = pl.BlockSpec((None, tk, tn), lambda j, i, kk: (j // cpt, kk, j % cpt))
    elif tk > r:
        w_spec = pl.BlockSpec((tk // r, r, tn), lambda j, i, kk: (kk, 0, j), pipeline_mode=buffering)
    else:
        rpt = r // tk
        w_spec = pl.BlockSpec((None, tk, tn), lambda j, i, kk: (kk // rpt, kk % rpt, j))
    in_specs = [pl.BlockSpec((tm, tk), lambda j, i, kk: (i, kk)), w_spec]
    operands = [a, w]
    if res is not None:
        in_specs.append(pl.BlockSpec((tm, tn), lambda j, i, kk: (i, j)))
        operands.append(res)
    if norm_gain is not None:
        in_specs.append(pl.BlockSpec((1, tn), lambda j, i, kk: (0, 0)))
        operands.append(norm_gain.reshape(1, n))
    return _mm_call(name, (n // tn, m // tm, nk), operands, in_specs, jax.ShapeDtypeStruct((m, n), out_dtype),
                    pl.BlockSpec((tm, tn), lambda j, i, kk: (i, j)), (((1,), (0,)), ((), ())), nk, scale,
                    res is not None, (tm, tn), exchange, normed=norm_gain is not None)


def mm_nt(name, a, w, kind, out_dtype, scale=1.0, exchange=None, a_hi=None, norm_back=None):
    m, kc = a.shape
    if a_hi is not None:
        assert a_hi.shape == a.shape
        kc = 2 * kc
    p, r, c = w.shape
    n = r if kind == 'col' else p * r
    assert kc == (p * c if kind == 'col' else c), (name, a.shape, w.shape)
    n_tiles = _divisors(r, LANE, 2816)
    k_tiles = _divisors(c, LANE, 4096)
    if kind == 'row':
        n_tiles = n_tiles + [q * r for q in (2, 4) if p % q == 0 and q * r <= 2816]
    o_item = jnp.dtype(out_dtype).itemsize
    if norm_back is not None:
        assert a_hi is None and n in n_tiles
        n_tiles, o_item = [n], 4 + 2 + 4
    tm, tn, tk = _plan_mm(m, n_tiles, k_tiles, n, kc, a.dtype.itemsize, o_item, norm_back is not None,
                          1 if a_hi is None else 2)
    nk = kc // tk
    if kind == 'col':
        cpt = c // tk
        w_spec = pl.BlockSpec((None, tn, tk), lambda j, i, kk: (kk // cpt, j, kk % cpt))
    elif tn > r:
        w_spec = pl.BlockSpec((tn // r, r, tk), lambda j, i, kk: (j, 0, kk))
    else:
        rpt = r // tn
        w_spec = pl.BlockSpec((None, tn, tk), lambda j, i, kk: (j // rpt, j % rpt, kk))
    split = None
    a_spec = pl.BlockSpec((tm, tk), lambda j, i, kk: (i, kk))
    if a_hi is not None:
        half = nk // 2
        assert nk % 2 == 0
        a_spec = pl.BlockSpec((tm, tk), lambda j, i, kk: (i, jnp.minimum(kk, half - 1)))
        split = Split(0, a_hi, pl.BlockSpec((tm, tk), lambda j, i, kk: (i, jnp.maximum(kk - half, 0))),
                      lambda j, i, kk: kk >= half)
    operands, in_specs = [a, w], [a_spec, w_spec]
    if norm_back is not None:
        x, gain, dres = norm_back
        rows = pl.BlockSpec((tm, n), lambda j, i, kk: (i, 0))
        operands += [x, gain.reshape(1, n), dres]
        in_specs += [rows, pl.BlockSpec((1, n), lambda j, i, kk: (0, 0)), rows]
    return _mm_call(name, (n // tn, m // tm, nk), operands, in_specs, jax.ShapeDtypeStruct((m, n), out_dtype),
                    pl.BlockSpec((tm, tn), lambda j, i, kk: (i, j)), (((1,), (1,)), ((), ())), nk, scale, False,
                    (tm, tn), exchange, split, norm_back=norm_back is not None)


def _plan_tn(s, ka, nd, r_tiles, n_tiles):
    best, best_cost = None, None
    for ts in _divisors(s, 16, 2048):
        for tr in r_tiles:
            for tn in n_tiles:
                ni, nj, ns = ka // tr, nd // tn, s // ts
                vmem = 2 * (ts * tr * 2 + ts * tn * 2 + tr * tn * 2) + tr * tn * 4 * (2 if ns > 1 else 1)
                if vmem > VMEM_PLAN_BYTES:
                    continue
                traffic = nj * s * ka * 2 + ni * s * nd * 2 + ka * nd * 2
                cost = _tile_time(2 * s * ka * nd, _mxu_fill(ts) * _mxu_fill(tn), traffic, ni * nj * ns,
                                  tr * tn // 1024 if ns > 1 else 0)
                if best_cost is None or cost < best_cost:
                    best, best_cost = (ts, tr, tn), cost
    assert best is not None, (s, ka, nd)
    return best


def mm_tn(name, a, dy, kind, scale=1.0, exchange=None, panels=(0, N_CHIPS), into=None):
    s, ka = a.shape
    s2, nd = dy.shape
    assert s == s2
    p = N_CHIPS
    first_panel, n_panels = panels
    assert kind == 'col' or panels == (0, p)
    r, c = (ka, nd // n_panels) if kind == 'col' else (ka // p, nd)
    ts, tr, tn = _plan_tn(s, ka, nd, _divisors(r, LANE, 2048), _divisors(c, LANE, 2816))
    ns = s // ts
    if kind == 'col':
        cpt = c // tn
        o_spec = pl.BlockSpec((None, tr, tn), lambda j, i, kk: (first_panel + j // cpt, i, j % cpt))
    else:
        rpt = r // tr
        o_spec = pl.BlockSpec((None, tr, tn), lambda j, i, kk: (i // rpt, i % rpt, j))
    in_specs = [pl.BlockSpec((ts, tr), lambda j, i, kk: (kk, i)), pl.BlockSpec((ts, tn), lambda j, i, kk: (kk, j))]
    return _mm_call(name, (nd // tn, ka // tr, ns), [a, dy], in_specs, jax.ShapeDtypeStruct((p, r, c), BF16), o_spec,
                    (((0,), (0,)), ((), ())), ns, scale, False, (tr, tn), exchange, None, into)


def _plan_fused(m, k, f, tiles, n_w, n_io):
    best, best_cost = None, None
    for tm in _divisors(m, 16, 1024):
        for tn in tiles:
            vmem = 2 * (tm * k * 2 + n_io * tm * tn * 2) + n_w * k * tn * 2 + 4 * tm * tn * 4
            if vmem > VMEM_PLAN_BYTES:
                continue
            traffic = (f // tn) * m * k * 2 + n_w * k * f * 2 + n_io * m * f * 2
            cost = _tile_time(2 * m * k * f * n_w, _mxu_fill(tn), traffic, (f // tn) * (m // tm), 0)
            if best_cost is None or cost < best_cost:
                best, best_cost = (tm, tn), cost
    assert best is not None, (m, k, f)
    return best


def mm_swiglu(name, h, w13, exchange=None):
    m, k = h.shape
    p, r, c = w13.shape
    assert r == k and p % 2 == 0
    f = p * c // 2
    tm, tn = _plan_fused(m, k, f, _divisors(c, LANE, 2816), 2, 3)
    cpt = c // tn

    def compute(ins, outs, scr):
        a = ins[0][...]
        g = jnp.dot(a, ins[1][...], preferred_element_type=F32)
        u = jnp.dot(a, ins[2][...], preferred_element_type=F32)
        sg = _sigmoid(g)
        silu = g * sg
        outs[0][...] = (u * (sg * (1.0 + g * (1.0 - sg)))).astype(BF16)
        outs[1][...] = silu.astype(BF16)
        outs[2][...] = (silu * u).astype(BF16)

    tile = pl.BlockSpec((tm, tn), lambda j, i: (i, j))
    in_specs = [pl.BlockSpec((tm, k), lambda j, i: (i, 0)),
                pl.BlockSpec((None, k, tn), lambda j, i: (j // cpt, 0, j % cpt), pipeline_mode=pl.Buffered(1)),
                pl.BlockSpec((None, k, tn), lambda j, i: (j // cpt + p // 2, 0, j % cpt),
                             pipeline_mode=pl.Buffered(1))]
    shape = jax.ShapeDtypeStruct((m, f), BF16)
    got = _tiled_call(name, (f // tn, m // tm), [h, w13, w13], in_specs, [shape] * 3, [tile] * 3, [], compute, exchange)
    return got


def mm_dswiglu(name, dy, w2, by_gate, by_up, scale, exchange=None):
    m, k = dy.shape
    p, r, c = w2.shape
    assert c == k
    f = p * r
    tiles = _divisors(r, LANE, 2816) + [q * r for q in (2, 4) if p % q == 0 and q * r <= 2816]
    tm, tn = _plan_fused(m, k, f, tiles, 1, 4)

    def compute(ins, outs, scr):
        b = ins[1][...]
        if b.ndim == 3:
            b = b.reshape(b.shape[0] * b.shape[1], b.shape[2])
        d = lax.dot_general(ins[0][...], b, (((1,), (1,)), ((), ())), preferred_element_type=F32) * scale
        outs[0][...] = (d * ins[2][...].astype(F32)).astype(BF16)
        outs[1][...] = (d * ins[3][...].astype(F32)).astype(BF16)

    tile = pl.BlockSpec((tm, tn), lambda j, i: (i, j))
    if tn > r:
        w_spec = pl.BlockSpec((tn // r, r, k), lambda j, i: (j, 0, 0), pipeline_mode=pl.Buffered(1))
    else:
        rpt = r // tn
        w_spec = pl.BlockSpec((None, tn, k), lambda j, i: (j // rpt, j % rpt, 0), pipeline_mode=pl.Buffered(1))
    in_specs = [pl.BlockSpec((tm, k), lambda j, i: (i, 0)), w_spec, tile, tile]
    shape = jax.ShapeDtypeStruct((m, f), BF16)
    return _tiled_call(name, (f // tn, m // tm), [dy, w2, by_gate, by_up], in_specs, [shape] * 2, [tile] * 2, [], compute,
                       exchange)


def _rms_rows(x, g):
    r = lax.rsqrt(jnp.mean(x * x, axis=-1, keepdims=True) + RMS_EPS)
    xhat = x * r
    return xhat, r, xhat * g


def rms_fwd(name, x, g, exchange=None):
    s, d = x.shape
    tm = _row_tile(s, d * 4)

    def compute(ins, outs, scr):
        outs[0][...] = _rms_rows(ins[0][...], ins[1][...])[2].astype(BF16)

    got = _tiled_call(name, (s // tm,), [x, g.reshape(1, d)],
                      [pl.BlockSpec((tm, d), lambda i: (i, 0)), pl.BlockSpec((1, d), lambda i: (0, 0))],
                      [jax.ShapeDtypeStruct((s, d), BF16)], [pl.BlockSpec((tm, d), lambda i: (i, 0))], [], compute,
                      exchange)
    return (got[0][0], got[1]) if exchange else got[0]


def _rms_bwd_rows(x, g, dh):
    xhat, r, _ = _rms_rows(x, g)
    u = dh * g
    dx = r * (u - xhat * jnp.mean(u * xhat, axis=-1, keepdims=True))
    return dx, jnp.sum(dh * xhat, axis=0, keepdims=True)


def rms_bwd(name, x, g, dh, dres):
    s, d = x.shape
    tm = _row_tile(s, d * 4, 2 * 2**20)
    has_res = dres is not None

    def body(*refs):
        x_ref, g_ref, dh_ref = refs[:3]
        dres_ref = refs[3] if has_res else None
        dx_ref, dxb_ref, dg_ref = refs[-3:]
        dx, dg = _rms_bwd_rows(x_ref[...], g_ref[...], dh_ref[...].astype(F32))
        if has_res:
            dx = dx + dres_ref[...]
        dx_ref[...] = dx
        dxb_ref[...] = dx.astype(BF16)

        @pl.when(pl.program_id(0) == 0)
        def _():
            dg_ref[...] = dg

        @pl.when(pl.program_id(0) > 0)
        def _():
            dg_ref[...] += dg

    row = pl.BlockSpec((tm, d), lambda i: (i, 0))
    vec = pl.BlockSpec((1, d), lambda i: (0, 0))
    return pl.pallas_call(
        body, name=name, grid=(s // tm,),
        in_specs=[row, vec, row] + ([row] if has_res else []),
        out_specs=[row, row, vec],
        out_shape=[jax.ShapeDtypeStruct((s, d), F32), jax.ShapeDtypeStruct((s, d), BF16),
                   jax.ShapeDtypeStruct((1, d), F32)],
        compiler_params=_params(dimension_semantics=("arbitrary",)),
    )(x, g.reshape(1, d), dh, *([dres] if has_res else []))


def loss_head(name, x, g, target):
    s, d = x.shape
    tm = _row_tile(s, d * 4, 2 * 2**20)

    def body(x_ref, g_ref, t_ref, dx_ref, dxb_ref, dg_ref, loss_ref):
        x = x_ref[...]
        gain = g_ref[...]
        y = _rms_rows(x, gain)[2]
        diff = y - t_ref[...]
        dx, dg = _rms_bwd_rows(x, gain, diff * (1.0 / d))
        dx_ref[...] = dx
        dxb_ref[...] = dx.astype(BF16)
        sq = jnp.sum(diff * diff, axis=0, keepdims=True)

        @pl.when(pl.program_id(0) == 0)
        def _():
            dg_ref[...] = dg
            loss_ref[...] = sq

        @pl.when(pl.program_id(0) > 0)
        def _():
            dg_ref[...] += dg
            loss_ref[...] += sq

    row = pl.BlockSpec((tm, d), lambda i: (i, 0))
    vec = pl.BlockSpec((1, d), lambda i: (0, 0))
    return pl.pallas_call(
        body, name=name, grid=(s // tm,), in_specs=[row, vec, row], out_specs=[row, row, vec, vec],
        out_shape=[jax.ShapeDtypeStruct((s, d), F32), jax.ShapeDtypeStruct((s, d), BF16),
                   jax.ShapeDtypeStruct((1, d), F32), jax.ShapeDtypeStruct((1, d), F32)],
        compiler_params=_params(dimension_semantics=("arbitrary",)),
    )(x, g.reshape(1, d), target)


def _sigmoid(x):
    return 0.5 * jnp.tanh(0.5 * x) + 0.5


_INV_SQRT2 = 0.7071067811865476
_INV_SQRT_2PI = 0.3989422804014327


def _normal_cdf(z):
    return 0.5 * (1.0 + lax.erf(z * _INV_SQRT2))


def _gelu_grad(z, cdf):
    return cdf + z * (_INV_SQRT_2PI * jnp.exp(-0.5 * z * z))


def _causal_weights(ws_ref, g):
    t = ws_ref.shape[-1]
    keep = lax.broadcasted_iota(jnp.int32, (t, t), 0) >= lax.broadcasted_iota(jnp.int32, (t, t), 1)
    return jnp.where(keep, ws_ref[g], 0.0).astype(BF16), keep


def _gmlp_gate_rows(z_ref, lg_ref, lb_ref, e):
    z = z_ref[...].astype(F32)
    cdf = _normal_cdf(z)
    gz = z * cdf
    u, v = gz[:, :e], gz[:, e:]
    mu = jnp.mean(v, axis=-1, keepdims=True)
    xc = v - mu
    rs = lax.rsqrt(jnp.mean(xc * xc, axis=-1, keepdims=True) + LN_EPS)
    vhat = xc * rs
    return (z, cdf), u, vhat, rs, vhat * lg_ref[...] + lb_ref[...]


def gmlp_fwd(name, z, ln_g, ln_b, w_s, bias):
    s, e2 = z.shape
    e = e2 // 2
    eg = e // GMLP_GROUPS

    def body(z_ref, lg_ref, lb_ref, ws_ref, b_ref, o_ref):
        _, u, _, _, vln = _gmlp_gate_rows(z_ref, lg_ref, lb_ref, e)
        vb = vln.astype(BF16)
        for g in range(GMLP_GROUPS):
            cols = slice(g * eg, (g + 1) * eg)
            wm, _ = _causal_weights(ws_ref, g)
            f = jnp.dot(wm, vb[:, cols], preferred_element_type=F32) + b_ref[:, cols]
            o_ref[:, cols] = (u[:, cols] * f).astype(BF16)

    full = lambda shape: pl.BlockSpec(shape, lambda i: (0,) * len(shape))
    return pl.pallas_call(
        body, name=name, grid=(s // CHUNK,),
        in_specs=[pl.BlockSpec((CHUNK, e2), lambda i: (i, 0)), full((1, e)), full((1, e)),
                  full((GMLP_GROUPS, CHUNK, CHUNK)), full((CHUNK, e))],
        out_specs=pl.BlockSpec((CHUNK, e), lambda i: (i, 0)), out_shape=jax.ShapeDtypeStruct((s, e), BF16),
        compiler_params=_params(dimension_semantics=("arbitrary",)),
    )(z, ln_g.reshape(1, e), ln_b.reshape(1, e), w_s, bias)


def gmlp_bwd(name, z, dp, ln_g, ln_b, w_s, bias):
    s, e2 = z.shape
    e = e2 // 2
    eg = e // GMLP_GROUPS
    t = CHUNK

    def body(z_ref, dp_ref, lg_ref, lb_ref, ws_ref, b_ref, dz_ref, dws_ref, dbs_ref, dlg_ref, dlb_ref):
        first = pl.program_id(0) == 0
        (zf, cdf), u, vhat, rs, vln = _gmlp_gate_rows(z_ref, lg_ref, lb_ref, e)
        vb = vln.astype(BF16)
        dp = dp_ref[...].astype(F32)
        lane = lax.broadcasted_iota(jnp.int32, (t, LANE), 1)
        dbs = jnp.zeros((t, LANE), F32)
        dvln_parts = []
        for g in range(GMLP_GROUPS):
            cols = slice(g * eg, (g + 1) * eg)
            wm, keep = _causal_weights(ws_ref, g)
            f = jnp.dot(wm, vb[:, cols], preferred_element_type=F32) + b_ref[:, cols]
            dz_ref[:, cols] = (dp[:, cols] * f * _gelu_grad(zf[:, cols], cdf[:, cols])).astype(BF16)
            df = dp[:, cols] * u[:, cols]
            dfb = df.astype(BF16)
            dbs = dbs + jnp.where(lane == g, jnp.sum(df, axis=-1, keepdims=True), 0.0)
            dw = lax.dot_general(dfb, vb[:, cols], (((1,), (1,)), ((), ())), preferred_element_type=F32)
            dw = jnp.where(keep, dw, 0.0)

            @pl.when(first)
            def _():
                dws_ref[g] = dw

            @pl.when(jnp.logical_not(first))
            def _():
                dws_ref[g] += dw

            dvln_parts.append(lax.dot_general(wm, dfb, (((0,), (0,)), ((), ())), preferred_element_type=F32))
        dvln = jnp.concatenate(dvln_parts, axis=-1)
        dvhat = dvln * lg_ref[...]
        dv = rs * (dvhat - jnp.mean(dvhat, axis=-1, keepdims=True)
                   - vhat * jnp.mean(dvhat * vhat, axis=-1, keepdims=True))
        dz_ref[:, e:] = (dv * _gelu_grad(zf[:, e:], cdf[:, e:])).astype(BF16)
        dlg = jnp.sum(dvln * vhat, axis=0, keepdims=True)
        dlb = jnp.sum(dvln, axis=0, keepdims=True)

        @pl.when(first)
        def _():
            dbs_ref[...] = dbs
            dlg_ref[...] = dlg
            dlb_ref[...] = dlb

        @pl.when(jnp.logical_not(first))
        def _():
            dbs_ref[...] += dbs
            dlg_ref[...] += dlg
            dlb_ref[...] += dlb

    full = lambda shape: pl.BlockSpec(shape, lambda i: (0,) * len(shape))
    return pl.pallas_call(
        body, name=name, grid=(s // t,),
        in_specs=[pl.BlockSpec((t, e2), lambda i: (i, 0)), pl.BlockSpec((t, e), lambda i: (i, 0)), full((1, e)),
                  full((1, e)), full((GMLP_GROUPS, t, t)), full((t, e))],
        out_specs=[pl.BlockSpec((t, e2), lambda i: (i, 0)), full((GMLP_GROUPS, t, t)), full((t, LANE)), full((1, e)),
                   full((1, e))],
        out_shape=[jax.ShapeDtypeStruct((s, e2), BF16), jax.ShapeDtypeStruct((GMLP_GROUPS, t, t), F32),
                   jax.ShapeDtypeStruct((t, LANE), F32), jax.ShapeDtypeStruct((1, e), F32),
                   jax.ShapeDtypeStruct((1, e), F32)],
        compiler_params=_params(dimension_semantics=("arbitrary",)),
    )(z, dp, ln_g.reshape(1, e), ln_b.reshape(1, e), w_s, bias)


EDGE = 16


def _shift_down(zc, prev, k):
    tm = zc.shape[0]
    row = lax.broadcasted_iota(jnp.int32, (tm, 1), 0)
    out = pltpu.roll(zc, k, 0)
    for j in range(k):
        out = jnp.where(row == j, prev[EDGE - k + j:EDGE - k + j + 1, :], out)
    return out


def _shift_up(dc, nxt, k):
    tm = dc.shape[0]
    row = lax.broadcasted_iota(jnp.int32, (tm, 1), 0)
    out = pltpu.roll(dc, tm - k, 0)
    for j in range(k):
        out = jnp.where(row == tm - k + j, nxt[j:j + 1, :], out)
    return out


def conv_fwd(name, bcv, cw):
    s, d3 = bcv.shape
    d = d3 // 3
    tm = _row_tile(s, d * 4, 2 * 2**20)
    per = tm // EDGE

    def body(b_ref, c_ref, v_ref, cp_ref, vp_ref, w_ref, o_ref):
        i = pl.program_id(0)
        zc = c_ref[...].astype(F32) * v_ref[...].astype(F32)
        prev = jnp.where(i > 0, cp_ref[...].astype(F32) * vp_ref[...].astype(F32), 0.0)
        conv = w_ref[2:3, :] * zc + w_ref[1:2, :] * _shift_down(zc, prev, 1) + w_ref[0:1, :] * _shift_down(zc, prev, 2)
        o_ref[...] = (b_ref[...].astype(F32) * conv).astype(BF16)

    blk = lambda col: pl.BlockSpec((tm, d), lambda i: (i, col))
    edge = lambda col: pl.BlockSpec((EDGE, d), lambda i: (jnp.maximum(i * per - 1, 0), col))
    return pl.pallas_call(
        body, name=name, grid=(s // tm,),
        in_specs=[blk(0), blk(1), blk(2), edge(1), edge(2), pl.BlockSpec((3, d), lambda i: (0, 0))],
        out_specs=pl.BlockSpec((tm, d), lambda i: (i, 0)), out_shape=jax.ShapeDtypeStruct((s, d), BF16),
        compiler_params=_params(dimension_semantics=("arbitrary",)),
    )(bcv, bcv, bcv, bcv, bcv, cw)


def conv_bwd(name, bcv, dq, cw):
    s, d3 = bcv.shape
    d = d3 // 3
    tm = _row_tile(s, d * 4, 2**20)
    per = tm // EDGE
    n_tiles = s // tm
    last_edge = s // EDGE - 1

    def body(b_ref, c_ref, v_ref, cp_ref, vp_ref, bn_ref, dq_ref, dqn_ref, w_ref, o_ref, dw_ref):
        i = pl.program_id(0)
        b = b_ref[...].astype(F32)
        c = c_ref[...].astype(F32)
        v = v_ref[...].astype(F32)
        dq = dq_ref[...].astype(F32)
        zc = c * v
        prev = jnp.where(i > 0, cp_ref[...].astype(F32) * vp_ref[...].astype(F32), 0.0)
        z1 = _shift_down(zc, prev, 1)
        z2 = _shift_down(zc, prev, 2)
        w0, w1, w2 = w_ref[0:1, :], w_ref[1:2, :], w_ref[2:3, :]
        conv = w2 * zc + w1 * z1 + w0 * z2
        dconv = dq * b
        nxt = jnp.where(i < n_tiles - 1, dqn_ref[...].astype(F32) * bn_ref[...].astype(F32), 0.0)
        dz = w2 * dconv + w1 * _shift_up(dconv, nxt, 1) + w0 * _shift_up(dconv, nxt, 2)
        o_ref[:, :d] = (dq * conv).astype(BF16)
        o_ref[:, d:2 * d] = (dz * v).astype(BF16)
        o_ref[:, 2 * d:] = (dz * c).astype(BF16)
        dw = jnp.concatenate([jnp.sum(dconv * z2, axis=0, keepdims=True), jnp.sum(dconv * z1, axis=0, keepdims=True),
                              jnp.sum(dconv * zc, axis=0, keepdims=True), jnp.zeros((5, d), F32)], axis=0)

        @pl.when(i == 0)
        def _():
            dw_ref[...] = dw

        @pl.when(i > 0)
        def _():
            dw_ref[...] += dw

    blk = lambda col: pl.BlockSpec((tm, d), lambda i: (i, col))
    before = lambda col: pl.BlockSpec((EDGE, d), lambda i: (jnp.maximum(i * per - 1, 0), col))
    after = lambda col: pl.BlockSpec((EDGE, d), lambda i: (jnp.minimum((i + 1) * per, last_edge), col))
    return pl.pallas_call(
        body, name=name, grid=(n_tiles,),
        in_specs=[blk(0), blk(1), blk(2), before(1), before(2), after(0), blk(0), after(0),
                  pl.BlockSpec((3, d), lambda i: (0, 0))],
        out_specs=[pl.BlockSpec((tm, d3), lambda i: (i, 0)), pl.BlockSpec((8, d), lambda i: (0, 0))],
        out_shape=[jax.ShapeDtypeStruct((s, d3), BF16), jax.ShapeDtypeStruct((8, d), F32)],
        compiler_params=_params(dimension_semantics=("arbitrary",)),
    )(bcv, bcv, bcv, bcv, bcv, bcv, dq, dq, cw)


def _attn_probs(qh, kh, scale):
    sc = lax.dot_general(qh, kh, (((1,), (1,)), ((), ())), preferred_element_type=F32) * scale
    ex = jnp.exp(sc - jnp.max(sc, axis=-1, keepdims=True))
    return ex / jnp.sum(ex, axis=-1, keepdims=True)


def attn_fwd(name, q, kv):
    s, d = q.shape
    mlen = kv.shape[0]
    dh = d // XATTN_HEADS
    scale = dh ** -0.5
    tm = _row_tile(s, d * 4, 4 * 2**20)

    def body(q_ref, kv_ref, o_ref):
        for h in range(XATTN_HEADS):
            cols = slice(h * dh, (h + 1) * dh)
            p = _attn_probs(q_ref[:, cols], kv_ref[:, cols], scale)
            o_ref[:, cols] = jnp.dot(p.astype(BF16), kv_ref[:, d + h * dh:d + (h + 1) * dh],
                                     preferred_element_type=F32).astype(BF16)

    return pl.pallas_call(
        body, name=name, grid=(s // tm,),
        in_specs=[pl.BlockSpec((tm, d), lambda i: (i, 0)), pl.BlockSpec((mlen, 2 * d), lambda i: (0, 0))],
        out_specs=pl.BlockSpec((tm, d), lambda i: (i, 0)), out_shape=jax.ShapeDtypeStruct((s, d), BF16),
        compiler_params=_params(dimension_semantics=("arbitrary",)),
    )(q, kv)


def attn_bwd(name, q, kv, do):
    s, d = q.shape
    mlen = kv.shape[0]
    dh = d // XATTN_HEADS
    scale = dh ** -0.5
    tm = _row_tile(s, d * 4, 4 * 2**20)

    def body(q_ref, kv_ref, do_ref, dq_ref, dkv_ref):
        first = pl.program_id(0) == 0
        for h in range(XATTN_HEADS):
            cols = slice(h * dh, (h + 1) * dh)
            vcols = slice(d + h * dh, d + (h + 1) * dh)
            qh, kh, vh, doh = q_ref[:, cols], kv_ref[:, cols], kv_ref[:, vcols], do_ref[:, cols]
            p = _attn_probs(qh, kh, scale)
            dp = lax.dot_general(doh, vh, (((1,), (1,)), ((), ())), preferred_element_type=F32)
            ds = (p * (dp - jnp.sum(dp * p, axis=-1, keepdims=True)) * scale).astype(BF16)
            dq_ref[:, cols] = jnp.dot(ds, kh, preferred_element_type=F32).astype(BF16)
            dk = lax.dot_general(ds, qh, (((0,), (0,)), ((), ())), preferred_element_type=F32)
            dv = lax.dot_general(p.astype(BF16), doh, (((0,), (0,)), ((), ())), preferred_element_type=F32)

            @pl.when(first)
            def _():
                dkv_ref[:, cols] = dk
                dkv_ref[:, vcols] = dv

            @pl.when(jnp.logical_not(first))
            def _():
                dkv_ref[:, cols] += dk
                dkv_ref[:, vcols] += dv

    row = pl.BlockSpec((tm, d), lambda i: (i, 0))
    whole = pl.BlockSpec((mlen, 2 * d), lambda i: (0, 0))
    return pl.pallas_call(
        body, name=name, grid=(s // tm,), in_specs=[row, whole, row], out_specs=[row, whole],
        out_shape=[jax.ShapeDtypeStruct((s, d), BF16), jax.ShapeDtypeStruct((mlen, 2 * d), F32)],
        compiler_params=_params(dimension_semantics=("arbitrary",)),
    )(q, kv, do)


def _as_rows(a):
    if a.ndim >= 2 and a.shape[-1] % LANE == 0:
        return a.reshape(-1, a.shape[-1])
    return a.reshape(-1, LANE) if a.size % LANE == 0 else a.reshape(1, -1)


def add_halves(name, dw, other, core):
    p, r, c = dw.shape
    h = r // 2
    th = _row_tile(h, c * 2, 4 * 2**20)

    def body(core_ref, a_ref, b_ref, o_ref):
        o_ref[...] = (a_ref[...].astype(F32) + b_ref[...].astype(F32)).astype(BF16)

    grid_spec = pltpu.PrefetchScalarGridSpec(
        num_scalar_prefetch=1, grid=(p, h // th),
        in_specs=[pl.BlockSpec((None, None, th, c), lambda pi, i, core_ref: (pi, core_ref[0], i, 0)),
                  pl.BlockSpec((None, th, c), lambda pi, i, core_ref: (pi, i, 0))],
        out_specs=pl.BlockSpec((None, th, c), lambda pi, i, core_ref: (pi, i, 0)))
    return pl.pallas_call(
        body, name=name, grid_spec=grid_spec, out_shape=jax.ShapeDtypeStruct((p, h, c), BF16),
        compiler_params=_params(dimension_semantics=("arbitrary", "arbitrary")),
    )(core, dw.reshape(p, 2, h, c), other)


def sum_leading(name, parts):
    n, r, c = parts.shape
    tr = _row_tile(r, c * 4 * 2, 2 * 2**20)

    def body(p_ref, o_ref):
        acc = p_ref[0].astype(F32)
        for k in range(1, n):
            acc = acc + p_ref[k].astype(F32)
        o_ref[...] = acc

    return pl.pallas_call(
        body, name=name, grid=(r // tr,), in_specs=[pl.BlockSpec((n, tr, c), lambda i: (0, i, 0))],
        out_specs=pl.BlockSpec((tr, c), lambda i: (i, 0)), out_shape=jax.ShapeDtypeStruct((r, c), F32),
        compiler_params=_params(dimension_semantics=("arbitrary",)),
    )(parts)


def _adamw_rows(w, g, m, v):
    m = ADAM_B1 * m + (1.0 - ADAM_B1) * g
    v = ADAM_B2 * v + (1.0 - ADAM_B2) * (g * g)
    m_hat = m / (1.0 - ADAM_B1 ** ADAM_STEP)
    v_hat = v / (1.0 - ADAM_B2 ** ADAM_STEP)
    delta = -ADAM_LR * (m_hat / (jnp.sqrt(v_hat) + ADAM_EPS) + ADAM_WD * w)
    return delta, m, v


def adamw_layer(name, w, m, v, g, layer, carried):
    nl, r, c = w.shape
    tr = _row_tile(r, c * 4, 3 * 2**19)
    n_carried = 4 if carried is not None else 0

    def body(*refs):
        w_ref, m_ref, v_ref, g_ref = refs[:4]
        go_ref, d_ref, mo_ref, vo_ref = refs[4 + n_carried:]
        g = g_ref[...]
        delta, m_new, v_new = _adamw_rows(w_ref[...], g, m_ref[...], v_ref[...])
        go_ref[...] = g
        d_ref[...] = delta
        mo_ref[...] = m_new
        vo_ref[...] = v_new

    stacked = pl.BlockSpec((None, tr, c), lambda i: (layer, i, 0))
    in_specs = [stacked, stacked, stacked, pl.BlockSpec((tr, c), lambda i: (i, 0))]
    in_specs += [pl.BlockSpec(memory_space=pl.ANY)] * n_carried
    shape = jax.ShapeDtypeStruct((nl, r, c), F32)
    return pl.pallas_call(
        body, name=name, grid=(r // tr,), in_specs=in_specs, out_specs=[stacked] * 4, out_shape=[shape] * 4,
        input_output_aliases={4 + k: k for k in range(n_carried)},
        compiler_params=_params(dimension_semantics=("arbitrary",)),
    )(w, m, v, g, *(carried or ()))


def adamw_flat(name, w, m, v, g):
    r, c = w.shape

    def body(w_ref, m_ref, v_ref, g_ref, d_ref, mo_ref, vo_ref):
        delta, m_new, v_new = _adamw_rows(w_ref[...], g_ref[...], m_ref[...], v_ref[...])
        d_ref[...] = delta
        mo_ref[...] = m_new
        vo_ref[...] = v_new

    shape = jax.ShapeDtypeStruct((r, c), F32)
    return pl.pallas_call(body, name=name, out_shape=[shape] * 3, compiler_params=_params())(w, m, v, g)


def cast_place(name, w, layer, place):
    nl, r, c = w.shape
    tr = _row_tile(r, c * 4, 8 * 2**20)

    def body(x_ref, y_ref, c_ref, w_ref, o_ref):
        o_ref[...] = w_ref[...].astype(BF16)

    grid_spec = pltpu.PrefetchScalarGridSpec(
        num_scalar_prefetch=3, grid=(r // tr,),
        in_specs=[pl.BlockSpec((None, tr, c), lambda i, x_ref, y_ref, c_ref: (layer, i, 0))],
        out_specs=pl.BlockSpec((None, tr, c), lambda i, x_ref, y_ref, c_ref: (2 * x_ref[0] + y_ref[0], i, 0)))
    return pl.pallas_call(
        body, name=name, grid_spec=grid_spec, out_shape=jax.ShapeDtypeStruct((N_CHIPS, r, c), BF16),
        compiler_params=_params(dimension_semantics=("arbitrary",)),
    )(*place, w)


def reduce_sum4(name, own, landed, place):
    p, h, c = own.shape
    tr = _row_tile(h, c * 4, 4 * 2**20)

    def body(x_ref, y_ref, c_ref, t_ref, y1_ref, y2_ref, y3_ref, o_ref):
        acc = t_ref[...].astype(F32)
        for part_ref in (y1_ref, y2_ref, y3_ref):
            acc = acc + part_ref[...].astype(F32)
        o_ref[...] = acc

    def panel(fx, fy):
        return pl.BlockSpec((None, tr, c), lambda i, x_ref, y_ref, c_ref: (
            2 * (1 - x_ref[0] if fx else x_ref[0]) + (1 - y_ref[0] if fy else y_ref[0]), i, 0))

    grid_spec = pltpu.PrefetchScalarGridSpec(
        num_scalar_prefetch=3, grid=(h // tr,),
        in_specs=[panel(0, 0), panel(1, 0), panel(0, 1), panel(1, 1)],
        out_specs=pl.BlockSpec((None, tr, c), lambda i, x_ref, y_ref, c_ref: (c_ref[0], i, 0)))
    return pl.pallas_call(
        body, name=name, grid_spec=grid_spec, out_shape=jax.ShapeDtypeStruct((2, h, c), F32),
        compiler_params=_params(dimension_semantics=("arbitrary",)),
    )(*place, own, landed, landed, landed)


def run_exchange(name, exchange):
    n_in, n_out = len(exchange.inputs), len(exchange.out_shapes)

    def body(*refs):
        ins, outs, sems = refs[:n_in], refs[n_in:n_in + n_out], refs[n_in + n_out:]
        exchange.start(ins, outs, sems)
        exchange.finish(ins, outs, sems)

    return pl.pallas_call(
        body, name=name, in_specs=[ANY] * n_in, out_specs=[ANY] * n_out, out_shape=list(exchange.out_shapes),
        scratch_shapes=[pltpu.SemaphoreType.DMA((exchange.n_sems,)), pltpu.SemaphoreType.DMA((exchange.n_sems,))],
        input_output_aliases=dict(exchange.aliases),
    )(*exchange.inputs)


def _row_halves(ref, c):
    h = ref.shape[1] // 2
    return pl.ds(pl.multiple_of(c * h, 16), h), pl.ds(pl.multiple_of((1 - c) * h, 16), h)


def _in_place(arrays, n_sems, start, finish):
    return Exchange(list(arrays), [jax.ShapeDtypeStruct(f.shape, f.dtype) for f in arrays],
                    {a: a for a in range(len(arrays))}, n_sems, start, finish)


def gather_over_ici(fulls):
    n = len(fulls)

    def sends(outs, sems):
        x, y, c, mine, chips = _place()
        return [_remote(rows, rows, sems, 3 * a + j, (*chip, c)) for a in range(n)
                for rows in [outs[a].at[mine, _row_halves(outs[a], c)[0]]] for j, chip in enumerate(chips)]

    def start(ins, outs, sems):
        for cp in sends(outs, sems):
            cp.start()

    def finish(ins, outs, sems):
        x, y, c, mine, chips = _place()
        for a in range(n):
            for j, chip in enumerate(chips):
                rows = outs[a].at[2 * chip[0] + chip[1], _row_halves(outs[a], c)[0]]
                _remote(rows, rows, sems, 3 * a + j, (*chip, c)).wait_recv()
        for cp in sends(outs, sems):
            cp.wait_send()

    return _in_place(fulls, 3 * n, start, finish)


def gather_over_ici_by_neighbours(fulls):
    n = len(fulls)

    def plan(outs, a):
        x, y, c, mine, (nbr_x, nbr_y, far) = _place()
        h = outs[a].shape[1] // 2
        first = pl.ds(pl.multiple_of(c * h, 16), h // 2)
        second = pl.ds(pl.multiple_of(c * h + h // 2, 16), h // 2)
        index = lambda chip: 2 * chip[0] + chip[1]
        return c, mine, nbr_x, nbr_y, index, _row_halves(outs[a], c)[0], first, second, index(far)

    def direct(outs, sems, a):
        c, mine, nbr_x, nbr_y, _, half, _, _, _ = plan(outs, a)
        rows = outs[a].at[mine, half]
        return [_remote(rows, rows, sems, 4 * a, (*nbr_x, c)), _remote(rows, rows, sems, 4 * a + 1, (*nbr_y, c))]

    def passed_on(outs, sems, a):
        c, _, nbr_x, nbr_y, index, _, first, second, _ = plan(outs, a)
        from_x, from_y = outs[a].at[index(nbr_x), first], outs[a].at[index(nbr_y), second]
        return [_remote(from_x, from_x, sems, 4 * a + 2, (*nbr_y, c)), _remote(from_y, from_y, sems, 4 * a + 3, (*nbr_x, c))]

    def start(ins, outs, sems):
        for a in range(n):
            for cp in direct(outs, sems, a):
                cp.start()

    def finish(ins, outs, sems):
        for a in range(n):
            c, _, nbr_x, nbr_y, index, half, first, second, far = plan(outs, a)
            for k, nbr in enumerate((nbr_x, nbr_y)):
                rows = outs[a].at[index(nbr), half]
                _remote(rows, rows, sems, 4 * a + k, (*nbr, c)).wait_recv()
            for cp in passed_on(outs, sems, a):
                cp.start()
        for a in range(n):
            c, _, nbr_x, nbr_y, index, half, first, second, far = plan(outs, a)
            for k, (rows, nbr) in enumerate(((outs[a].at[far, first], nbr_y), (outs[a].at[far, second], nbr_x))):
                _remote(rows, rows, sems, 4 * a + 2 + k, (*nbr, c)).wait_recv()
            for cp in direct(outs, sems, a) + passed_on(outs, sems, a):
                cp.wait_send()

    return _in_place(fulls, 4 * n, start, finish)


def gather_over_d2d(fulls):
    n = len(fulls)

    def copies(outs, sems, which):
        x, y, c, mine, chips = _place()
        return [_remote(rows, rows, sems, 3 * a + j, (x, y, 1 - c)) for a in range(n) for j, chip in enumerate(chips)
                for rows in [outs[a].at[2 * chip[0] + chip[1], _row_halves(outs[a], c)[which]]]]

    def start(ins, outs, sems):
        for cp in copies(outs, sems, 0):
            cp.start()

    def finish(ins, outs, sems):
        for cp in copies(outs, sems, 1):
            cp.wait_recv()
        for cp in copies(outs, sems, 0):
            cp.wait_send()

    return _in_place(fulls, 3 * n, start, finish)


def gather_whole(fulls):
    ici, d2d = gather_over_ici(fulls), gather_over_d2d(fulls)

    def finish(ins, outs, sems):
        ici.finish(ins, outs, sems)
        later = tuple(_SemaphoresFrom(s, ici.n_sems) for s in sems)
        d2d.start(ins, outs, later)
        d2d.finish(ins, outs, later)

    return _in_place(fulls, ici.n_sems + d2d.n_sems, ici.start, finish)


class _SemaphoresFrom:
    def __init__(self, ref, offset):
        self.ref, self.offset = ref, offset

    @property
    def at(self):
        return self

    def __getitem__(self, k):
        return self.ref.at[self.offset + k]


def combine(exchanges):
    exchanges = [e for e in exchanges if e is not None]
    if len(exchanges) <= 1:
        return exchanges[0] if exchanges else None
    inputs, out_shapes, aliases, spans, n_sems = [], [], {}, [], 0
    for e in exchanges:
        aliases.update({len(inputs) + i: len(out_shapes) + o for i, o in e.aliases.items()})
        spans.append((len(inputs), len(e.inputs), len(out_shapes), len(e.out_shapes), n_sems))
        inputs, out_shapes, n_sems = inputs + list(e.inputs), out_shapes + list(e.out_shapes), n_sems + e.n_sems

    def each(method):
        def run(ins, outs, sems):
            for e, (i0, ni, o0, no, s0) in zip(exchanges, spans, strict=True):
                getattr(e, method)(ins[i0:i0 + ni], outs[o0:o0 + no], tuple(_SemaphoresFrom(s, s0) for s in sems))
        return run

    return Exchange(inputs, out_shapes, aliases, n_sems, each("start"), each("finish"))


def swap_exchange(grads):
    n = len(grads)

    def copies(ins, outs, sems):
        x, y, c, _, _ = _place()
        return [_remote(ins[a].at[:, _row_halves(ins[a], c)[1]], outs[a], sems, a, (x, y, 1 - c)) for a in range(n)]

    def start(ins, outs, sems):
        for cp in copies(ins, outs, sems):
            cp.start()

    def finish(ins, outs, sems):
        for cp in copies(ins, outs, sems):
            cp.wait()

    shapes = [jax.ShapeDtypeStruct((g.shape[0], g.shape[1] // 2, g.shape[2]), g.dtype) for g in grads]
    return Exchange(list(grads), shapes, {}, n, start, finish)


def scatter_exchange(parts):
    n = len(parts)

    def sends(ins, outs, sems):
        x, y, c, mine, chips = _place()
        return [_remote(ins[a].at[2 * chip[0] + chip[1]], outs[a].at[mine], sems, 3 * a + j, (*chip, c))
                for a in range(n) for j, chip in enumerate(chips)]

    def start(ins, outs, sems):
        for cp in sends(ins, outs, sems):
            cp.start()

    def finish(ins, outs, sems):
        x, y, c, mine, chips = _place()
        for a in range(n):
            for j, chip in enumerate(chips):
                landing = outs[a].at[2 * chip[0] + chip[1]]
                _remote(landing, landing, sems, 3 * a + j, (*chip, c)).wait_recv()
        for cp in sends(ins, outs, sems):
            cp.wait_send()

    return Exchange(list(parts), [jax.ShapeDtypeStruct(g.shape, g.dtype) for g in parts], {}, 3 * n, start, finish)


def join_exchange(halves):
    n = len(halves)

    def start(ins, outs, sems):
        x, y, c, _, _ = _place()
        for a in range(n):
            _remote(outs[a].at[c], outs[a].at[c], sems, a, (x, y, 1 - c)).start()

    def finish(ins, outs, sems):
        x, y, c, _, _ = _place()
        for a in range(n):
            _remote(outs[a].at[1 - c], outs[a].at[1 - c], sems, a, (x, y, 1 - c)).wait_recv()
        for a in range(n):
            _remote(outs[a].at[c], outs[a].at[c], sems, a, (x, y, 1 - c)).wait_send()

    return Exchange(list(halves), [jax.ShapeDtypeStruct(g.shape, g.dtype) for g in halves], {a: a for a in range(n)},
                    n, start, finish)


def gather_all(name, rows):
    def body(in_ref, out_ref, send_sems, recv_sems, local_sem):
        sems = (send_sems, recv_sems)
        x, y, c, _, _ = _place()
        me = 4 * x + 2 * y + c
        local = pltpu.make_async_copy(in_ref, out_ref.at[me], local_sem)
        local.start()
        peers = [(1 - x if k & 4 else x, 1 - y if k & 2 else y, 1 - c if k & 1 else c) for k in range(1, N_DEV)]
        sent = []
        for k, peer in enumerate(peers):
            cp = _remote(in_ref, out_ref.at[me], sems, k, peer)
            cp.start()
            sent.append(cp)
        for k, peer in enumerate(peers):
            landing = out_ref.at[4 * peer[0] + 2 * peer[1] + peer[2]]
            _remote(landing, landing, sems, k, peer).wait_recv()
        for cp in sent:
            cp.wait_send()
        local.wait()

    return pl.pallas_call(
        body, name=name, in_specs=[ANY], out_specs=ANY,
        out_shape=jax.ShapeDtypeStruct((N_DEV,) + rows.shape, rows.dtype),
        scratch_shapes=[pltpu.SemaphoreType.DMA((N_DEV - 1,)), pltpu.SemaphoreType.DMA((N_DEV - 1,)),
                        pltpu.SemaphoreType.DMA],
    )(rows)


class _Step:
    def __init__(self, p):
        self.p = p
        xi, yi, ci = lax.axis_index("x"), lax.axis_index("y"), lax.axis_index("c")
        self.chip = 2 * xi + yi
        self.place_refs = tuple(v.astype(jnp.int32).reshape(1) for v in (xi, yi, ci))
        self.core_ref = self.place_refs[2]
        self.depth = p['ffn1_norm'].shape[0]
        self.placed, self.landed, self.w = {}, {}, {}
        self.big_g = {}
        self.waiting_joins = []
        self.waiting_scatter = None

    def block_keys(self, tag, l):
        if l >= self.depth:
            return []
        mixer = ['gmlp_w_in', 'gmlp_w_out'] if l % 2 == 0 else ['conv_w_in', 'conv_w_out']
        names = {"ffn1": ['ffn1_w13', 'ffn1_w2'], "mix": mixer, "xattn": ['xattn_wq', 'xattn_wkv', 'xattn_wo'],
                 "ffn2": ['ffn2_w13', 'ffn2_w2']}[tag]
        return [(n, l // 2 if tag == "mix" else l) for n in names]

    def place(self, keys):
        for n, idx in keys:
            self.placed[(n, idx)] = cast_place(f"place_{n}{idx}", self.p[n], idx, self.place_refs)

    def carrying_gather(self, call, over_ici, over_d2d, whole, *args, **kw):
        over_ici = [k for k in over_ici if k in self.placed]
        over_d2d = [k for k in over_d2d if k in self.landed]
        parts = [gather_over_ici([self.placed.pop(k) for k in over_ici]) if over_ici else None,
                 gather_over_d2d([self.landed.pop(k) for k in over_d2d]) if over_d2d else None,
                 gather_whole([self.placed.pop(k) for k in whole]) if whole else None]
        exchange = combine(parts)
        if exchange is None:
            return call(*args, **kw)
        out, got = call(*args, exchange=exchange, **kw)
        self.landed.update(zip(over_ici, got[:len(over_ici)], strict=True))
        self.w.update(zip(over_d2d + whole, got[len(over_ici):], strict=True))
        return out

    def reduce_begin(self, tag, keys, dws, theirs=None):
        theirs = list(theirs or [None] * len(dws))
        todo = [i for i, t in enumerate(theirs) if t is None]
        for i, t in zip(todo, run_exchange(tag + "_swap", swap_exchange([dws[i] for i in todo])), strict=True):
            theirs[i] = t
        parts = [add_halves(f"{tag}_add{i}", dw, t, self.core_ref) for i, (dw, t) in enumerate(zip(dws, theirs, strict=True))]
        assert self.waiting_scatter is None
        self.waiting_scatter = (tag, keys, parts)

    def carrying_scatter(self, mm, *args, **kw):
        tag, keys, parts = self.waiting_scatter
        self.waiting_scatter = None
        out, landed = mm(*args, exchange=scatter_exchange(parts), **kw)
        halves = [reduce_sum4(f"{tag}_sum{i}", t, y, self.place_refs) for i, (t, y) in enumerate(zip(parts, landed, strict=True))]
        self.waiting_joins += list(zip(keys, halves, strict=True))
        return out

    def take_joined(self, keys, joined):
        for k, g in zip(keys, joined, strict=True):
            self.big_g[k] = g.reshape(-1, g.shape[-1])

    def carrying_joins(self, mm, *args, **kw):
        if not self.waiting_joins:
            return mm(*args, **kw)
        keys, halves = zip(*self.waiting_joins, strict=True)
        self.waiting_joins = []
        out, joined = mm(*args, exchange=join_exchange(list(halves)), **kw)
        self.take_joined(keys, joined)
        return out

    def joins_alone(self, name):
        keys, halves = zip(*self.waiting_joins, strict=True)
        self.waiting_joins = []
        self.take_joined(keys, run_exchange(name, join_exchange(list(halves))))

    def ffn_fwd(self, tag, l, x, gain, carry_norm, carry13, carry2, h=None, next_gain=None):
        name = f"l{l}_{tag}"
        if h is None:
            h = self.carrying_gather(rms_fwd, *carry_norm, name + "_norm", x, gain)
        by_gate, by_up, act = self.carrying_gather(mm_swiglu, *carry13, name + "_w13", h, self.w[(tag + '_w13', l)])
        out = self.carrying_gather(mm_nn, *carry2, name + "_w2", act, self.w[(tag + '_w2', l)], 'row', F32, res=x,
                                   scale=0.5, norm_gain=next_gain)
        return out, (x, h, by_gate, by_up, act)

    def ffn_bwd(self, tag, l, dx, dxb, saved, gain):
        w13, w2 = self.w[(tag + '_w13', l)], self.w[(tag + '_w2', l)]
        name = f"l{l}_{tag}"
        x, h, by_gate, by_up, act = saved
        d_gate, d_up = self.carrying_joins(mm_dswiglu, name + "_dact", dxb, w2, by_gate, by_up, 0.5)
        d_w2 = mm_tn(name + "_dw2", act, dxb, 'row', scale=0.5)
        half = N_CHIPS // 2
        d_w13, their_w2 = mm_tn(name + "_dw13g", h, d_gate, 'col', panels=(0, half), exchange=swap_exchange([d_w2]))
        d_w13 = mm_tn(name + "_dw13u", h, d_up, 'col', panels=(half, half), into=d_w13)
        self.reduce_begin(name, [(tag + '_w13', l), (tag + '_w2', l)], [d_w13, d_w2], [None] + their_w2)
        dh = self.carrying_scatter(mm_nt, name + "_dh", d_gate, w13, 'col', BF16, a_hi=d_up)
        return rms_bwd(name + "_dnorm", x, gain, dh, dx)


def kernel(x, mem, ffn1_norm, ffn1_w13, ffn1_w2, mix_norm, gmlp_w_in, gmlp_ln_g, gmlp_ln_b, gmlp_w_s, gmlp_b_s, gmlp_w_out, conv_w_in, conv_w, conv_w_out, xattn_norm, mem_norm, xattn_wq, xattn_wkv, xattn_wo, ffn2_norm, ffn2_w13, ffn2_w2, final_norm, loss_target, m_ffn1_norm, m_ffn1_w13, m_ffn1_w2, m_mix_norm, m_gmlp_w_in, m_gmlp_ln_g, m_gmlp_ln_b, m_gmlp_w_s, m_gmlp_b_s, m_gmlp_w_out, m_conv_w_in, m_conv_w, m_conv_w_out, m_xattn_norm, m_mem_norm, m_xattn_wq, m_xattn_wkv, m_xattn_wo, m_ffn2_norm, m_ffn2_w13, m_ffn2_w2, m_final_norm, v_ffn1_norm, v_ffn1_w13, v_ffn1_w2, v_mix_norm, v_gmlp_w_in, v_gmlp_ln_g, v_gmlp_ln_b, v_gmlp_w_s, v_gmlp_b_s, v_gmlp_w_out, v_conv_w_in, v_conv_w, v_conv_w_out, v_xattn_norm, v_mem_norm, v_xattn_wq, v_xattn_wkv, v_xattn_wo, v_ffn2_norm, v_ffn2_w13, v_ffn2_w2, v_final_norm):
    return _step(dict(locals()))


def _step(p):
    assert sorted(p) == sorted(ARG_NAMES)
    st = _Step(p)
    x = p['x'][0]
    mem = p['mem'][0]
    target = p['loss_target'][0]
    s, d = x.shape
    depth = st.depth

    for l in range(depth):
        for tag in ("ffn1", "mix", "xattn", "ffn2"):
            st.place(st.block_keys(tag, l))
    first = ('ffn1_w13', 0)
    st.landed[first] = run_exchange("gather_first", gather_over_ici_by_neighbours([st.placed.pop(first)]))[0]

    cw_local = p['conv_w']
    n_conv, cwid, dq4 = cw_local.shape
    cw_rows = jnp.pad(cw_local.reshape(-1, LANE), ((0, (-cw_local.size // LANE) % 8), (0, 0)))
    cw_all = gather_all("gather_conv_w", cw_rows)[0::2, :cw_local.size // LANE]
    conv_w_full = cw_all.reshape(N_CHIPS, n_conv, cwid, dq4).transpose(1, 2, 0, 3).reshape(n_conv, cwid, N_CHIPS * dq4)

    saved, h_ffn1 = [], None
    for l in range(depth):
        j = l // 2
        rec = {}
        mix_keys, att_keys, up_keys = st.block_keys("mix", l), st.block_keys("xattn", l), [('ffn2_w13', l)]
        if l == 0:
            on_w13, on_w2 = (mix_keys + att_keys, [], [('ffn1_w2', 0)]), (up_keys, mix_keys + att_keys, [])
            hand_over_on_mix_in = up_keys
        else:
            on_w13, on_w2 = (mix_keys + up_keys, [], []), (att_keys, mix_keys + up_keys, [])
            hand_over_on_mix_in = att_keys
        (x, h), rec['ffn1'] = st.ffn_fwd("ffn1", l, x, p['ffn1_norm'][l],
                                         ([], [('ffn1_w13', 0)] if l == 0 else [], []), on_w13, on_w2,
                                         h=h_ffn1, next_gain=p['mix_norm'][l])
        if l % 2 == 0:
            e = p['gmlp_ln_g'].shape[-1]
            bias = jnp.repeat(p['gmlp_b_s'][j].T, e // GMLP_GROUPS, axis=1)
            z = st.carrying_gather(mm_nn, [('ffn2_w2', l)], hand_over_on_mix_in, [], f"l{l}_gmlp_in", h,
                                   st.w[('gmlp_w_in', j)], 'col', BF16)
            gate = gmlp_fwd(f"l{l}_gmlp_gate", z, p['gmlp_ln_g'][j], p['gmlp_ln_b'][j], p['gmlp_w_s'][j], bias)
            x_new, hq = st.carrying_gather(mm_nn, [], [('ffn2_w2', l)], [], f"l{l}_gmlp_out", gate,
                                           st.w[('gmlp_w_out', j)], 'row', F32, res=x, norm_gain=p['xattn_norm'][l])
            rec['mix'] = (x, h, z, gate, bias)
        else:
            bcv = st.carrying_gather(mm_nn, [('ffn2_w2', l)], hand_over_on_mix_in, [], f"l{l}_conv_in", h,
                                     st.w[('conv_w_in', j)], 'col', BF16)
            gate = conv_fwd(f"l{l}_conv_gate", bcv, conv_w_full[j])
            x_new, hq = st.carrying_gather(mm_nn, [], [('ffn2_w2', l)], [], f"l{l}_conv_out", gate,
                                           st.w[('conv_w_out', j)], 'row', F32, res=x, norm_gain=p['xattn_norm'][l])
            rec['mix'] = (x, h, bcv, gate)
        x = x_new
        q = mm_nn(f"l{l}_xattn_q", hq, st.w[('xattn_wq', l)], 'row', BF16)
        mem_n = rms_fwd(f"l{l}_mem_norm", mem, p['mem_norm'][l])
        kv = mm_nn(f"l{l}_xattn_kv", mem_n, st.w[('xattn_wkv', l)], 'col', BF16)
        o = attn_fwd(f"l{l}_xattn_core", q, kv)
        x_new, h_ffn2 = mm_nn(f"l{l}_xattn_o", o, st.w[('xattn_wo', l)], 'row', F32, res=x,
                              norm_gain=p['ffn2_norm'][l])
        rec['xattn'] = (x, hq, q, mem_n, kv, o)
        x = x_new
        ahead = st.block_keys("ffn1", l + 1)
        last = l + 1 == depth
        out, rec['ffn2'] = st.ffn_fwd("ffn2", l, x, p['ffn2_norm'][l], None, (ahead, [], []), ([], ahead, []), h=h_ffn2,
                                      next_gain=None if last else p['ffn1_norm'][l + 1])
        x, h_ffn1 = (out, None) if last else out
        saved.append(rec)

    dx, dxb, d_final, loss_lanes = loss_head("loss_head", x, p['final_norm'], target)
    loss = lax.psum(0.5 * jnp.sum(loss_lanes) / d, ("x", "y", "c"))

    small = {n: [None] * p[n].shape[0] for n in ('ffn1_norm', 'mix_norm', 'xattn_norm', 'mem_norm', 'ffn2_norm',
                                                  'gmlp_ln_g', 'gmlp_ln_b', 'gmlp_w_s', 'gmlp_b_s', 'conv_w')}
    for l in reversed(range(depth)):
        j = l // 2
        rec = saved[l]
        dx, dxb, small['ffn2_norm'][l] = st.ffn_bwd("ffn2", l, dx, dxb, rec['ffn2'], p['ffn2_norm'][l])

        x_in, hq, q, mem_n, kv, o = rec['xattn']
        name = f"l{l}_xattn"
        do = st.carrying_joins(mm_nt, name + "_do", dxb, st.w[('xattn_wo', l)], 'row', BF16)
        d_wo = mm_tn(name + "_dwo", o, dxb, 'row')
        dq, dkv = attn_bwd(name + "_dcore", q, kv, do)
        d_wq = mm_tn(name + "_dwq", hq, dq, 'row')
        dkvb = dkv.astype(BF16)
        d_wkv = mm_tn(name + "_dwkv", mem_n, dkvb, 'col')
        st.reduce_begin(name, [('xattn_wq', l), ('xattn_wkv', l), ('xattn_wo', l)], [d_wq, d_wkv, d_wo])
        dx, dxb, small['xattn_norm'][l] = mm_nt(name + "_dh", dq, st.w[('xattn_wq', l)], 'row', F32,
                                                norm_back=(x_in, p['xattn_norm'][l], dx))
        dmem_n = mm_nt(name + "_dmem", dkvb, st.w[('xattn_wkv', l)], 'col', F32)
        small['mem_norm'][l] = rms_bwd(f"l{l}_mem_dnorm", mem, p['mem_norm'][l], dmem_n, None)[2]

        if l % 2 == 0:
            x_in, h, z, gate, bias = rec['mix']
            name = f"l{l}_gmlp"
            w_in, w_out = st.w[('gmlp_w_in', j)], st.w[('gmlp_w_out', j)]
            dgate = mm_nt(name + "_dgate", dxb, w_out, 'row', BF16)
            d_wout = mm_tn(name + "_dwout", gate, dxb, 'row')
            dmix, dws, dbs, dlg, dlb = gmlp_bwd(name + "_dgate_core", z, dgate, p['gmlp_ln_g'][j], p['gmlp_ln_b'][j],
                                                p['gmlp_w_s'][j], bias)
            small['gmlp_w_s'][j], small['gmlp_b_s'][j] = dws, dbs[:, :GMLP_GROUPS].T
            small['gmlp_ln_g'][j], small['gmlp_ln_b'][j] = dlg, dlb
            keys = [('gmlp_w_in', j), ('gmlp_w_out', j)]
        else:
            x_in, h, bcv, gate = rec['mix']
            name = f"l{l}_conv"
            w_in, w_out = st.w[('conv_w_in', j)], st.w[('conv_w_out', j)]
            dgate = mm_nt(name + "_dgate", dxb, w_out, 'row', BF16)
            d_wout = mm_tn(name + "_dwout", gate, dxb, 'row')
            dmix, dcw = conv_bwd(name + "_dgate_core", bcv, dgate, conv_w_full[j])
            small['conv_w'][j] = dcw[:cwid]
            keys = [('conv_w_in', j), ('conv_w_out', j)]
        d_win = st.carrying_scatter(mm_tn, name + "_dwin", h, dmix, 'col')
        st.reduce_begin(name, keys, [d_win, d_wout])
        dx, dxb, small['mix_norm'][l] = st.carrying_scatter(mm_nt, name + "_dh", dmix, w_in, 'col', F32,
                                                            norm_back=(x_in, p['mix_norm'][l], dx))

        dx, dxb, small['ffn1_norm'][l] = st.ffn_bwd("ffn1", l, dx, dxb, rec['ffn1'], p['ffn1_norm'][l])
    st.joins_alone("join_last")

    small_names = ['ffn1_norm', 'mix_norm', 'xattn_norm', 'mem_norm', 'ffn2_norm', 'gmlp_ln_g', 'gmlp_ln_b', 'gmlp_w_s',
                   'gmlp_b_s', 'final_norm', 'conv_w']
    small_full = {n: jnp.stack([g.reshape(p[n].shape[1:]) for g in small[n]]) for n in small_names
                  if n not in ('final_norm', 'conv_w')}
    small_full['final_norm'] = d_final.reshape(p['final_norm'].shape)
    small_full['conv_w'] = jnp.stack(small['conv_w'])
    packed = jnp.concatenate([small_full[n].reshape(-1, LANE) for n in small_names], axis=0)
    total = sum_leading("small_sum", gather_all("small_gather", packed))
    small_g, at = {}, 0
    for n in small_names:
        rows = small_full[n].size // LANE
        small_g[n] = total[at:at + rows].reshape(small_full[n].shape)
        at += rows
    small_g['conv_w'] = lax.dynamic_slice_in_dim(small_g['conv_w'], st.chip * dq4, dq4, axis=2)

    grads, deltas, new_m, new_v = {}, {}, {}, {}
    for n in WEIGHTS:
        w, m, v = p[n], p['m_' + n], p['v_' + n]
        if n in BIG:
            carried = None
            for i in range(w.shape[0]):
                carried = adamw_layer(f"adamw_{n}{i}", w, m, v, st.big_g[(n, i)], i, carried)
            grads[n], deltas[n], new_m[n], new_v[n] = carried
        else:
            g = small_g[n]
            out = adamw_flat(f"adamw_{n}", _as_rows(w), _as_rows(m), _as_rows(v), _as_rows(g))
            grads[n] = g
            deltas[n], new_m[n], new_v[n] = (o.reshape(w.shape) for o in out)

    grad_x = dx.reshape(p['x'].shape)
    return (loss, grad_x, *[grads[n] for n in WEIGHTS], *[deltas[n] for n in WEIGHTS], *[new_m[n] for n in WEIGHTS],
            *[new_v[n] for n in WEIGHTS])
```

```python
from typing import Callable, NamedTuple

import jax
import jax.numpy as jnp
from jax import lax
from jax.experimental import pallas as pl
from jax.experimental.pallas import tpu as pltpu

F32 = jnp.float32
BF16 = jnp.bfloat16
MESH = pl.DeviceIdType.MESH

CHUNK = 128
GMLP_GROUPS = 8
XATTN_HEADS = 4
RMS_EPS = 1e-6
LN_EPS = 1e-5
ADAM_LR = 0.001
ADAM_B1 = 0.9
ADAM_B2 = 0.999
ADAM_EPS = 1e-08
ADAM_WD = 0.01
ADAM_STEP = 10

N_CHIPS = 4
N_DEV = 8

VMEM_LIMIT_BYTES = 60 * 2**20
VMEM_PLAN_BYTES = 52 * 2**20
LANE = 128
MXU_DIM = 256
ACC_CHUNK = 2 * MXU_DIM
NORM_CHUNK_ROWS = 32
MXU_FLOPS_PER_US = 996e6
HBM_BYTES_PER_US = 3.3e6
STEP_US = 0.35
ACC_US_PER_VREG = 0.58e-3

WEIGHTS = ['ffn1_norm', 'ffn1_w13', 'ffn1_w2', 'mix_norm', 'gmlp_w_in', 'gmlp_ln_g', 'gmlp_ln_b', 'gmlp_w_s',
           'gmlp_b_s', 'gmlp_w_out', 'conv_w_in', 'conv_w', 'conv_w_out', 'xattn_norm', 'mem_norm', 'xattn_wq',
           'xattn_wkv', 'xattn_wo', 'ffn2_norm', 'ffn2_w13', 'ffn2_w2', 'final_norm']
BIG = {'ffn1_w13': 'col', 'ffn1_w2': 'row', 'gmlp_w_in': 'col', 'gmlp_w_out': 'row', 'conv_w_in': 'col',
       'conv_w_out': 'row', 'xattn_wq': 'row', 'xattn_wkv': 'col', 'xattn_wo': 'row', 'ffn2_w13': 'col',
       'ffn2_w2': 'row'}
ARG_NAMES = (['x', 'mem'] + WEIGHTS + ['loss_target'] + ['m_' + n for n in WEIGHTS] + ['v_' + n for n in WEIGHTS])


def _params(**kw):
    return pltpu.CompilerParams(vmem_limit_bytes=VMEM_LIMIT_BYTES, **kw)


def _divisors(n, mult, cap):
    return [d for d in range(mult, min(n, cap) + 1, mult) if n % d == 0] or [n]


def _row_tile(rows, width_bytes, budget=4 * 2**20):
    best = None
    for d in _divisors(rows, 16, 1024):
        if d * width_bytes <= budget:
            best = d
    return best or _divisors(rows, 16, 1024)[0]


ANY = pl.BlockSpec(memory_space=pl.ANY)


class Exchange(NamedTuple):
    inputs: list
    out_shapes: list
    aliases: dict
    n_sems: int
    start: Callable
    finish: Callable


def _place():
    x, y, c = lax.axis_index("x"), lax.axis_index("y"), lax.axis_index("c")
    chips = [(1 - x, y), (x, 1 - y), (1 - x, 1 - y)]
    return x, y, c, 2 * x + y, chips


def _remote(src, dst, sems, k, device):
    return pltpu.make_async_remote_copy(src_ref=src, dst_ref=dst, send_sem=sems[0].at[k], recv_sem=sems[1].at[k],
                                        device_id=device, device_id_type=MESH)


def _mxu_fill(dim):
    return dim / (-(-dim // MXU_DIM) * MXU_DIM)


def _tile_time(flops, fill, traffic, steps, acc_vregs):
    return (max(flops / (MXU_FLOPS_PER_US * fill), traffic / HBM_BYTES_PER_US) + steps * STEP_US
            + steps * acc_vregs * ACC_US_PER_VREG)


def _plan_mm(m, n_tiles_of, k_tiles_of, n, k, a_item, o_item, has_res, a_arrays=1):
    best, best_cost = None, None
    for tm in _divisors(m, 16, 1024):
        for tn in n_tiles_of:
            for tk in k_tiles_of:
                ni, nj, nk = m // tm, n // tn, k // tk
                blocks = a_arrays * tm * tk * a_item + tk * tn * 2 + tm * tn * o_item + (tm * tn * 4 if has_res else 0)
                vmem = 2 * blocks + tm * tn * 4 * (2 if nk > 1 else 1)
                if vmem > VMEM_PLAN_BYTES:
                    continue
                traffic = nj * m * k * a_item + (k * n * 2 if nk == 1 else ni * k * n * 2)
                traffic += m * n * (o_item + (4 if has_res else 0))
                cost = _tile_time(2 * m * n * k, _mxu_fill(tk) * _mxu_fill(tn), traffic, ni * nj * nk,
                                  tm * tn // 1024 if nk > 1 else 0)
                if best_cost is None or cost < best_cost:
                    best, best_cost = (tm, tn, tk), cost
    assert best is not None, (m, n, k)
    return best


def _tiled_call(name, grid, operands, in_specs, out_shapes, out_specs, scratch, compute, exchange=None, aliases=None):
    n_reg, n_out, n_scr = len(operands), len(out_shapes), len(scratch)
    n_xin = len(exchange.inputs) if exchange else 0
    n_xout = len(exchange.out_shapes) if exchange else 0
    semantics = ("arbitrary",) * len(grid)

    def body(*refs):
        ins = refs[:n_reg]
        outs = refs[n_reg + n_xin:n_reg + n_xin + n_out]
        scr = refs[n_reg + n_xin + n_out + n_xout:n_reg + n_xin + n_out + n_xout + n_scr]
        if not exchange:
            compute(ins, outs, scr)
            return
        x_ins = refs[n_reg:n_reg + n_xin]
        x_outs = refs[n_reg + n_xin + n_out:n_reg + n_xin + n_out + n_xout]
        sems = refs[-2:]
        at_first, at_last = True, True
        for k, extent in enumerate(grid):
            at_first = jnp.logical_and(at_first, pl.program_id(k) == 0)
            at_last = jnp.logical_and(at_last, pl.program_id(k) == extent - 1)

        @pl.when(at_first)
        def _():
            exchange.start(x_ins, x_outs, sems)

        compute(ins, outs, scr)

        @pl.when(at_last)
        def _():
            exchange.finish(x_ins, x_outs, sems)

    if not exchange:
        return pl.pallas_call(
            body, name=name, grid=grid, in_specs=in_specs, out_specs=out_specs, out_shape=out_shapes,
            scratch_shapes=scratch, input_output_aliases=dict(aliases or {}),
            compiler_params=_params(dimension_semantics=semantics),
        )(*operands)
    assert not aliases
    sems = [pltpu.SemaphoreType.DMA((exchange.n_sems,)), pltpu.SemaphoreType.DMA((exchange.n_sems,))]
    got = pl.pallas_call(
        body, name=name, grid=grid, in_specs=in_specs + [ANY] * n_xin, out_specs=out_specs + [ANY] * n_xout,
        out_shape=out_shapes + list(exchange.out_shapes), scratch_shapes=scratch + sems,
        input_output_aliases={n_reg + i: n_out + o for i, o in exchange.aliases.items()},
        compiler_params=_params(dimension_semantics=semantics),
    )(*operands, *exchange.inputs)
    return list(got[:n_out]), list(got[n_out:])


class Split(NamedTuple):
    slot: int
    other: jax.Array
    spec: pl.BlockSpec
    use_other: Callable


def _mm_call(name, grid, operands, in_specs, out_shape, out_spec, contract, nk, scale, has_res, tile, exchange=None,
             split=None, into=None, normed=False, norm_back=False):
    n_main = len(operands)
    out_shapes, out_specs = [out_shape], [out_spec]
    if normed:
        out_shapes, out_specs = out_shapes + [jax.ShapeDtypeStruct(out_shape.shape, BF16)], out_specs + [out_spec]
    if norm_back:
        m, n = out_shape.shape
        out_shapes = [jax.ShapeDtypeStruct((m, n), F32), jax.ShapeDtypeStruct((m, n), BF16),
                      jax.ShapeDtypeStruct((1, n), F32)]
        out_specs = [out_spec, out_spec, pl.BlockSpec((1, n), lambda j, i, kk: (0, 0))]

    def compute(ins, outs, scr):
        res_ref = ins[2] if has_res else None
        o_ref = outs[0]
        acc_ref = scr[0] if nk > 1 else None

        def finish_norm_back(v):
            dx, dg = _rms_bwd_rows(ins[2][...], ins[3][...], v)
            dx = dx + ins[4][...]
            outs[0][...] = dx
            outs[1][...] = dx.astype(BF16)
            first_rows = pl.program_id(1) == 0

            @pl.when(first_rows)
            def _():
                outs[2][...] = dg

            @pl.when(jnp.logical_not(first_rows))
            def _():
                outs[2][...] += dg

        def finish(v):
            if scale != 1.0:
                v = v * scale
            if norm_back:
                finish_norm_back(v)
                return
            if has_res:
                v = res_ref[...] + v
            o_ref[...] = v.astype(o_ref.dtype)
            if normed:
                outs[1][...] = _rms_rows(v, ins[2 + has_res][...])[2].astype(BF16)

        def b_block(cols):
            ref = b_ref_of[0]
            stacked = len(ref.shape) == 3
            if contract[0][1] == (1,):
                if not stacked:
                    return ref[cols, :]
                return ref[...].reshape(ref.shape[0] * ref.shape[1], ref.shape[2])[cols, :]
            if not stacked:
                return ref[:, cols]
            b = ref[:, :, cols]
            return b.reshape(b.shape[0] * b.shape[1], b.shape[2])

        b_ref_of = [None]

        def contribute(a_ref, b_ref):
            b_ref_of[0] = b_ref
            if nk == 1:
                finish(lax.dot_general(a_ref[...], b_block(slice(None)), contract, preferred_element_type=F32))
                return
            kk = pl.program_id(2)

            @pl.when(kk == 0)
            def _():
                acc_ref[...] = jnp.zeros(tile, F32)

            a = a_ref[...]
            for start in range(0, tile[1], ACC_CHUNK):
                cols = slice(start, min(start + ACC_CHUNK, tile[1]))
                acc_ref[:, cols] += lax.dot_general(a, b_block(cols), contract, preferred_element_type=F32)

            @pl.when(kk == nk - 1)
            def _():
                finish(acc_ref[...])

        if split is None:
            contribute(ins[0], ins[1])
            return
        use_other = split.use_other(pl.program_id(0), pl.program_id(1), pl.program_id(2))
        pair = [ins[0], ins[1]]
        other = list(pair)
        other[split.slot] = ins[n_main]

        @pl.when(jnp.logical_not(use_other))
        def _():
            contribute(*pair)

        @pl.when(use_other)
        def _():
            contribute(*other)

    aliases = None
    if split is not None:
        operands, in_specs = operands + [split.other], in_specs + [split.spec]
    if into is not None:
        aliases = {len(operands): 0}
        operands, in_specs = operands + [into], in_specs + [ANY]
    got = _tiled_call(name, grid, operands, in_specs, out_shapes, out_specs,
                      [pltpu.VMEM(tile, F32)] if nk > 1 else [], compute, exchange, aliases)
    results, carried = (got[0], got[1]) if exchange else (got, None)
    out = tuple(results) if normed or norm_back else results[0]
    return (out, carried) if exchange else out


def mm_nn(name, a, w, kind, out_dtype, res=None, scale=1.0, exchange=None, norm_gain=None):
    m, k = a.shape
    p, r, c = w.shape
    n = p * c if kind == 'col' else c
    assert k == (r if kind == 'col' else p * r), (name, a.shape, w.shape)
    n_tiles = _divisors(c, LANE, 2816)
    if norm_gain is not None:
        assert kind == 'row'
        n_tiles = [n]
    k_tiles = _divisors(r, LANE, 4096)
    if kind == 'row':
        k_tiles = k_tiles + [q * r for q in (2, 4) if p % q == 0]
    o_item = jnp.dtype(out_dtype).itemsize + (2 if norm_gain is not None else 0)
    tm, tn, tk = _plan_mm(m, n_tiles, k_tiles, n, k, a.dtype.itemsize, o_item, res is not None)
    buffering = None
    if norm_gain is not None:
        per_row = 2 * (k * a.dtype.itemsize + n * (o_item + (4 if res is not None else 0))) + n * 4
        fitting = [t for t in _divisors(m, 16, 1024) if t * per_row + k * n * 2 <= VMEM_PLAN_BYTES]
        if fitting:
            tm, tn, tk, buffering = fitting[-1], n, k, pl.Buffered(1)
    nk = k // tk
    if kind == 'col':
        cpt = c // tn
        w_spec = pl.BlockSpec((None, tk, tn), lambda j, i, kk: (j // cpt, kk, j % cpt))
    elif tk > r:
        w_spec = pl.BlockSpec((tk // r, r, tn), lambda j, i, kk: (kk, 0, j), pipeline_mode=buffering)
    else:
        rpt = r // tk
        w_spec = pl.BlockSpec((None, tk, tn), lambda j, i, kk: (kk // rpt, kk % rpt, j))
    in_specs = [pl.BlockSpec((tm, tk), lambda j, i, kk: (i, kk)), w_spec]
    operands = [a, w]
    if res is not None:
        in_specs.append(pl.BlockSpec((tm, tn), lambda j, i, kk: (i, j)))
        operands.append(res)
    if norm_gain is not None:
        in_specs.append(pl.BlockSpec((1, tn), lambda j, i, kk: (0, 0)))
        operands.append(norm_gain.reshape(1, n))
    return _mm_call(name, (n // tn, m // tm, nk), operands, in_specs, jax.ShapeDtypeStruct((m, n), out_dtype),
                    pl.BlockSpec((tm, tn), lambda j, i, kk: (i, j)), (((1,), (0,)), ((), ())), nk, scale,
                    res is not None, (tm, tn), exchange, normed=norm_gain is not None)


def mm_nt(name, a, w, kind, out_dtype, scale=1.0, exchange=None, a_hi=None, norm_back=None):
    m, kc = a.shape
    if a_hi is not None:
        assert a_hi.shape == a.shape
        kc = 2 * kc
    p, r, c = w.shape
    n = r if kind == 'col' else p * r
    assert kc == (p * c if kind == 'col' else c), (name, a.shape, w.shape)
    n_tiles = _divisors(r, LANE, 2816)
    k_tiles = _divisors(c, LANE, 4096)
    if kind == 'row':
        n_tiles = n_tiles + [q * r for q in (2, 4) if p % q == 0 and q * r <= 2816]
    o_item = jnp.dtype(out_dtype).itemsize
    if norm_back is not None:
        assert a_hi is None and n in n_tiles
        n_tiles, o_item = [n], 4 + 2 + 4
    tm, tn, tk = _plan_mm(m, n_tiles, k_tiles, n, kc, a.dtype.itemsize, o_item, norm_back is not None,
                          1 if a_hi is None else 2)
    nk = kc // tk
    if kind == 'col':
        cpt = c // tk
        w_spec = pl.BlockSpec((None, tn, tk), lambda j, i, kk: (kk // cpt, j, kk % cpt))
    elif tn > r:
        w_spec = pl.BlockSpec((tn // r, r, tk), lambda j, i, kk: (j, 0, kk))
    else:
        rpt = r // tn
        w_spec = pl.BlockSpec((None, tn, tk), lambda j, i, kk: (j // rpt, j % rpt, kk))
    split = None
    a_spec = pl.BlockSpec((tm, tk), lambda j, i, kk: (i, kk))
    if a_hi is not None:
        half = nk // 2
        assert nk % 2 == 0
        a_spec = pl.BlockSpec((tm, tk), lambda j, i, kk: (i, jnp.minimum(kk, half - 1)))
        split = Split(0, a_hi, pl.BlockSpec((tm, tk), lambda j, i, kk: (i, jnp.maximum(kk - half, 0))),
                      lambda j, i, kk: kk >= half)
    operands, in_specs = [a, w], [a_spec, w_spec]
    if norm_back is not None:
        x, gain, dres = norm_back
        rows = pl.BlockSpec((tm, n), lambda j, i, kk: (i, 0))
        operands += [x, gain.reshape(1, n), dres]
        in_specs += [rows, pl.BlockSpec((1, n), lambda j, i, kk: (0, 0)), rows]
    return _mm_call(name, (n // tn, m // tm, nk), operands, in_specs, jax.ShapeDtypeStruct((m, n), out_dtype),
                    pl.BlockSpec((tm, tn), lambda j, i, kk: (i, j)), (((1,), (1,)), ((), ())), nk, scale, False,
                    (tm, tn), exchange, split, norm_back=norm_back is not None)


def _plan_tn(s, ka, nd, r_tiles, n_tiles):
    best, best_cost = None, None
    for ts in _divisors(s, 16, 2048):
        for tr in r_tiles:
            for tn in n_tiles:
                ni, nj, ns = ka // tr, nd // tn, s // ts
                vmem = 2 * (ts * tr * 2 + ts * tn * 2 + tr * tn * 2) + tr * tn * 4 * (2 if ns > 1 else 1)
                if vmem > VMEM_PLAN_BYTES:
                    continue
                traffic = nj * s * ka * 2 + ni * s * nd * 2 + ka * nd * 2
                cost = _tile_time(2 * s * ka * nd, _mxu_fill(ts) * _mxu_fill(tn), traffic, ni * nj * ns,
                                  tr * tn // 1024 if ns > 1 else 0)
                if best_cost is None or cost < best_cost:
                    best, best_cost = (ts, tr, tn), cost
    assert best is not None, (s, ka, nd)
    return best


def mm_tn(name, a, dy, kind, scale=1.0, exchange=None, panels=(0, N_CHIPS), into=None):
    s, ka = a.shape
    s2, nd = dy.shape
    assert s == s2
    p = N_CHIPS
    first_panel, n_panels = panels
    assert kind == 'col' or panels == (0, p)
    r, c = (ka, nd // n_panels) if kind == 'col' else (ka // p, nd)
    ts, tr, tn = _plan_tn(s, ka, nd, _divisors(r, LANE, 2048), _divisors(c, LANE, 2816))
    ns = s // ts
    if kind == 'col':
        cpt = c // tn
        o_spec = pl.BlockSpec((None, tr, tn), lambda j, i, kk: (first_panel + j // cpt, i, j % cpt))
    else:
        rpt = r // tr
        o_spec = pl.BlockSpec((None, tr, tn), lambda j, i, kk: (i // rpt, i % rpt, j))
    in_specs = [pl.BlockSpec((ts, tr), lambda j, i, kk: (kk, i)), pl.BlockSpec((ts, tn), lambda j, i, kk: (kk, j))]
    return _mm_call(name, (nd // tn, ka // tr, ns), [a, dy], in_specs, jax.ShapeDtypeStruct((p, r, c), BF16), o_spec,
                    (((0,), (0,)), ((), ())), ns, scale, False, (tr, tn), exchange, None, into)


def _plan_fused(m, k, f, tiles, n_w, n_io):
    best, best_cost = None, None
    for tm in _divisors(m, 16, 1024):
        for tn in tiles:
            vmem = 2 * (tm * k * 2 + n_io * tm * tn * 2) + n_w * k * tn * 2 + 4 * tm * tn * 4
            if vmem > VMEM_PLAN_BYTES:
                continue
            traffic = (f // tn) * m * k * 2 + n_w * k * f * 2 + n_io * m * f * 2
            cost = _tile_time(2 * m * k * f * n_w, _mxu_fill(tn), traffic, (f // tn) * (m // tm), 0)
            if best_cost is None or cost < best_cost:
                best, best_cost = (tm, tn), cost
    assert best is not None, (m, k, f)
    return best


def mm_swiglu(name, h, w13, exchange=None):
    m, k = h.shape
    p, r, c = w13.shape
    assert r == k and p % 2 == 0
    f = p * c // 2
    tm, tn = _plan_fused(m, k, f, _divisors(c, LANE, 2816), 2, 3)
    cpt = c // tn

    def compute(ins, outs, scr):
        a = ins[0][...]
        g = jnp.dot(a, ins[1][...], preferred_element_type=F32)
        u = jnp.dot(a, ins[2][...], preferred_element_type=F32)
        sg = _sigmoid(g)
        silu = g * sg
        outs[0][...] = (u * (sg * (1.0 + g * (1.0 - sg)))).astype(BF16)
        outs[1][...] = silu.astype(BF16)
        outs[2][...] = (silu * u).astype(BF16)

    tile = pl.BlockSpec((tm, tn), lambda j, i: (i, j))
    in_specs = [pl.BlockSpec((tm, k), lambda j, i: (i, 0)),
                pl.BlockSpec((None, k, tn), lambda j, i: (j // cpt, 0, j % cpt), pipeline_mode=pl.Buffered(1)),
                pl.BlockSpec((None, k, tn), lambda j, i: (j // cpt + p // 2, 0, j % cpt),
                             pipeline_mode=pl.Buffered(1))]
    shape = jax.ShapeDtypeStruct((m, f), BF16)
    got = _tiled_call(name, (f // tn, m // tm), [h, w13, w13], in_specs, [shape] * 3, [tile] * 3, [], compute, exchange)
    return got


def mm_dswiglu(name, dy, w2, by_gate, by_up, scale, exchange=None):
    m, k = dy.shape
    p, r, c = w2.shape
    assert c == k
    f = p * r
    tiles = _divisors(r, LANE, 2816) + [q * r for q in (2, 4) if p % q == 0 and q * r <= 2816]
    tm, tn = _plan_fused(m, k, f, tiles, 1, 4)

    def compute(ins, outs, scr):
        b = ins[1][...]
        if b.ndim == 3:
            b = b.reshape(b.shape[0] * b.shape[1], b.shape[2])
        d = lax.dot_general(ins[0][...], b, (((1,), (1,)), ((), ())), preferred_element_type=F32) * scale
        outs[0][...] = (d * ins[2][...].astype(F32)).astype(BF16)
        outs[1][...] = (d * ins[3][...].astype(F32)).astype(BF16)

    tile = pl.BlockSpec((tm, tn), lambda j, i: (i, j))
    if tn > r:
        w_spec = pl.BlockSpec((tn // r, r, k), lambda j, i: (j, 0, 0), pipeline_mode=pl.Buffered(1))
    else:
        rpt = r // tn
        w_spec = pl.BlockSpec((None, tn, k), lambda j, i: (j // rpt, j % rpt, 0), pipeline_mode=pl.Buffered(1))
    in_specs = [pl.BlockSpec((tm, k), lambda j, i: (i, 0)), w_spec, tile, tile]
    shape = jax.ShapeDtypeStruct((m, f), BF16)
    return _tiled_call(name, (f // tn, m // tm), [dy, w2, by_gate, by_up], in_specs, [shape] * 2, [tile] * 2, [], compute,
                       exchange)


def _rms_rows(x, g):
    r = lax.rsqrt(jnp.mean(x * x, axis=-1, keepdims=True) + RMS_EPS)
    xhat = x * r
    return xhat, r, xhat * g


def rms_fwd(name, x, g, exchange=None):
    s, d = x.shape
    tm = _row_tile(s, d * 4)

    def compute(ins, outs, scr):
        outs[0][...] = _rms_rows(ins[0][...], ins[1][...])[2].astype(BF16)

    got = _tiled_call(name, (s // tm,), [x, g.reshape(1, d)],
                      [pl.BlockSpec((tm, d), lambda i: (i, 0)), pl.BlockSpec((1, d), lambda i: (0, 0))],
                      [jax.ShapeDtypeStruct((s, d), BF16)], [pl.BlockSpec((tm, d), lambda i: (i, 0))], [], compute,
                      exchange)
    return (got[0][0], got[1]) if exchange else got[0]


def _rms_bwd_rows(x, g, dh):
    xhat, r, _ = _rms_rows(x, g)
    u = dh * g
    dx = r * (u - xhat * jnp.mean(u * xhat, axis=-1, keepdims=True))
    return dx, jnp.sum(dh * xhat, axis=0, keepdims=True)


def rms_bwd(name, x, g, dh, dres):
    s, d = x.shape
    tm = _row_tile(s, d * 4, 2 * 2**20)
    has_res = dres is not None

    def body(*refs):
        x_ref, g_ref, dh_ref = refs[:3]
        dres_ref = refs[3] if has_res else None
        dx_ref, dxb_ref, dg_ref = refs[-3:]
        chunk = min(tm, NORM_CHUNK_ROWS)

        def rows_from(k, dg):
            rows = pl.ds(pl.multiple_of(k * chunk, chunk), chunk)
            dx, dg_rows = _rms_bwd_rows(x_ref[rows, :], g_ref[...], dh_ref[rows, :].astype(F32))
            if has_res:
                dx = dx + dres_ref[rows, :]
            dx_ref[rows, :] = dx
            dxb_ref[rows, :] = dx.astype(BF16)
            return dg + dg_rows

        dg = lax.fori_loop(0, tm // chunk, rows_from, jnp.zeros((1, d), F32))

        @pl.when(pl.program_id(0) == 0)
        def _():
            dg_ref[...] = dg

        @pl.when(pl.program_id(0) > 0)
        def _():
            dg_ref[...] += dg

    row = pl.BlockSpec((tm, d), lambda i: (i, 0))
    vec = pl.BlockSpec((1, d), lambda i: (0, 0))
    return pl.pallas_call(
        body, name=name, grid=(s // tm,),
        in_specs=[row, vec, row] + ([row] if has_res else []),
        out_specs=[row, row, vec],
        out_shape=[jax.ShapeDtypeStruct((s, d), F32), jax.ShapeDtypeStruct((s, d), BF16),
                   jax.ShapeDtypeStruct((1, d), F32)],
        compiler_params=_params(dimension_semantics=("arbitrary",)),
    )(x, g.reshape(1, d), dh, *([dres] if has_res else []))


def loss_head(name, x, g, target):
    s, d = x.shape
    tm = _row_tile(s, d * 4, 2 * 2**20)

    def body(x_ref, g_ref, t_ref, dx_ref, dxb_ref, dg_ref, loss_ref):
        x = x_ref[...]
        gain = g_ref[...]
        y = _rms_rows(x, gain)[2]
        diff = y - t_ref[...]
        dx, dg = _rms_bwd_rows(x, gain, diff * (1.0 / d))
        dx_ref[...] = dx
        dxb_ref[...] = dx.astype(BF16)
        sq = jnp.sum(diff * diff, axis=0, keepdims=True)

        @pl.when(pl.program_id(0) == 0)
        def _():
            dg_ref[...] = dg
            loss_ref[...] = sq

        @pl.when(pl.program_id(0) > 0)
        def _():
            dg_ref[...] += dg
            loss_ref[...] += sq

    row = pl.BlockSpec((tm, d), lambda i: (i, 0))
    vec = pl.BlockSpec((1, d), lambda i: (0, 0))
    return pl.pallas_call(
        body, name=name, grid=(s // tm,), in_specs=[row, vec, row], out_specs=[row, row, vec, vec],
        out_shape=[jax.ShapeDtypeStruct((s, d), F32), jax.ShapeDtypeStruct((s, d), BF16),
                   jax.ShapeDtypeStruct((1, d), F32), jax.ShapeDtypeStruct((1, d), F32)],
        compiler_params=_params(dimension_semantics=("arbitrary",)),
    )(x, g.reshape(1, d), target)


def _sigmoid(x):
    return 0.5 * jnp.tanh(0.5 * x) + 0.5


_INV_SQRT2 = 0.7071067811865476
_INV_SQRT_2PI = 0.3989422804014327


def _normal_cdf(z):
    return 0.5 * (1.0 + lax.erf(z * _INV_SQRT2))


def _gelu_grad(z, cdf):
    return cdf + z * (_INV_SQRT_2PI * jnp.exp(-0.5 * z * z))


def _causal_weights(ws_ref, g):
    t = ws_ref.shape[-1]
    keep = lax.broadcasted_iota(jnp.int32, (t, t), 0) >= lax.broadcasted_iota(jnp.int32, (t, t), 1)
    return jnp.where(keep, ws_ref[g], 0.0).astype(BF16), keep


def _gmlp_gate_rows(z_ref, lg_ref, lb_ref, e):
    z = z_ref[...].astype(F32)
    cdf = _normal_cdf(z)
    gz = z * cdf
    u, v = gz[:, :e], gz[:, e:]
    mu = jnp.mean(v, axis=-1, keepdims=True)
    xc = v - mu
    rs = lax.rsqrt(jnp.mean(xc * xc, axis=-1, keepdims=True) + LN_EPS)
    vhat = xc * rs
    return (z, cdf), u, vhat, rs, vhat * lg_ref[...] + lb_ref[...]


def gmlp_fwd(name, z, ln_g, ln_b, w_s, bias):
    s, e2 = z.shape
    e = e2 // 2
    eg = e // GMLP_GROUPS

    def body(z_ref, lg_ref, lb_ref, ws_ref, b_ref, o_ref):
        _, u, _, _, vln = _gmlp_gate_rows(z_ref, lg_ref, lb_ref, e)
        vb = vln.astype(BF16)
        for g in range(GMLP_GROUPS):
            cols = slice(g * eg, (g + 1) * eg)
            wm, _ = _causal_weights(ws_ref, g)
            f = jnp.dot(wm, vb[:, cols], preferred_element_type=F32) + b_ref[:, cols]
            o_ref[:, cols] = (u[:, cols] * f).astype(BF16)

    full = lambda shape: pl.BlockSpec(shape, lambda i: (0,) * len(shape))
    return pl.pallas_call(
        body, name=name, grid=(s // CHUNK,),
        in_specs=[pl.BlockSpec((CHUNK, e2), lambda i: (i, 0)), full((1, e)), full((1, e)),
                  full((GMLP_GROUPS, CHUNK, CHUNK)), full((CHUNK, e))],
        out_specs=pl.BlockSpec((CHUNK, e), lambda i: (i, 0)), out_shape=jax.ShapeDtypeStruct((s, e), BF16),
        compiler_params=_params(dimension_semantics=("arbitrary",)),
    )(z, ln_g.reshape(1, e), ln_b.reshape(1, e), w_s, bias)


def gmlp_bwd(name, z, dp, ln_g, ln_b, w_s, bias):
    s, e2 = z.shape
    e = e2 // 2
    eg = e // GMLP_GROUPS
    t = CHUNK

    def body(z_ref, dp_ref, lg_ref, lb_ref, ws_ref, b_ref, dz_ref, dws_ref, dbs_ref, dlg_ref, dlb_ref):
        first = pl.program_id(0) == 0
        (zf, cdf), u, vhat, rs, vln = _gmlp_gate_rows(z_ref, lg_ref, lb_ref, e)
        vb = vln.astype(BF16)
        dp = dp_ref[...].astype(F32)
        lane = lax.broadcasted_iota(jnp.int32, (t, LANE), 1)
        dbs = jnp.zeros((t, LANE), F32)
        dvln_parts = []
        for g in range(GMLP_GROUPS):
            cols = slice(g * eg, (g + 1) * eg)
            wm, keep = _causal_weights(ws_ref, g)
            f = jnp.dot(wm, vb[:, cols], preferred_element_type=F32) + b_ref[:, cols]
            dz_ref[:, cols] = (dp[:, cols] * f * _gelu_grad(zf[:, cols], cdf[:, cols])).astype(BF16)
            df = dp[:, cols] * u[:, cols]
            dfb = df.astype(BF16)
            dbs = dbs + jnp.where(lane == g, jnp.sum(df, axis=-1, keepdims=True), 0.0)
            dw = lax.dot_general(dfb, vb[:, cols], (((1,), (1,)), ((), ())), preferred_element_type=F32)
            dw = jnp.where(keep, dw, 0.0)

            @pl.when(first)
            def _():
                dws_ref[g] = dw

            @pl.when(jnp.logical_not(first))
            def _():
                dws_ref[g] += dw

            dvln_parts.append(lax.dot_general(wm, dfb, (((0,), (0,)), ((), ())), preferred_element_type=F32))
        dvln = jnp.concatenate(dvln_parts, axis=-1)
        dvhat = dvln * lg_ref[...]
        dv = rs * (dvhat - jnp.mean(dvhat, axis=-1, keepdims=True)
                   - vhat * jnp.mean(dvhat * vhat, axis=-1, keepdims=True))
        dz_ref[:, e:] = (dv * _gelu_grad(zf[:, e:], cdf[:, e:])).astype(BF16)
        dlg = jnp.sum(dvln * vhat, axis=0, keepdims=True)
        dlb = jnp.sum(dvln, axis=0, keepdims=True)

        @pl.when(first)
        def _():
            dbs_ref[...] = dbs
            dlg_ref[...] = dlg
            dlb_ref[...] = dlb

        @pl.when(jnp.logical_not(first))
        def _():
            dbs_ref[...] += dbs
            dlg_ref[...] += dlg
            dlb_ref[...] += dlb

    full = lambda shape: pl.BlockSpec(shape, lambda i: (0,) * len(shape))
    return pl.pallas_call(
        body, name=name, grid=(s // t,),
        in_specs=[pl.BlockSpec((t, e2), lambda i: (i, 0)), pl.BlockSpec((t, e), lambda i: (i, 0)), full((1, e)),
                  full((1, e)), full((GMLP_GROUPS, t, t)), full((t, e))],
        out_specs=[pl.BlockSpec((t, e2), lambda i: (i, 0)), full((GMLP_GROUPS, t, t)), full((t, LANE)), full((1, e)),
                   full((1, e))],
        out_shape=[jax.ShapeDtypeStruct((s, e2), BF16), jax.ShapeDtypeStruct((GMLP_GROUPS, t, t), F32),
                   jax.ShapeDtypeStruct((t, LANE), F32), jax.ShapeDtypeStruct((1, e), F32),
                   jax.ShapeDtypeStruct((1, e), F32)],
        compiler_params=_params(dimension_semantics=("arbitrary",)),
    )(z, dp, ln_g.reshape(1, e), ln_b.reshape(1, e), w_s, bias)


EDGE = 16


def _shift_down(zc, prev, k):
    tm = zc.shape[0]
    row = lax.broadcasted_iota(jnp.int32, (tm, 1), 0)
    out = pltpu.roll(zc, k, 0)
    for j in range(k):
        out = jnp.where(row == j, prev[EDGE - k + j:EDGE - k + j + 1, :], out)
    return out


def _shift_up(dc, nxt, k):
    tm = dc.shape[0]
    row = lax.broadcasted_iota(jnp.int32, (tm, 1), 0)
    out = pltpu.roll(dc, tm - k, 0)
    for j in range(k):
        out = jnp.where(row == tm - k + j, nxt[j:j + 1, :], out)
    return out


def conv_fwd(name, bcv, cw):
    s, d3 = bcv.shape
    d = d3 // 3
    tm = _row_tile(s, d * 4, 2 * 2**20)
    per = tm // EDGE

    def body(b_ref, c_ref, v_ref, cp_ref, vp_ref, w_ref, o_ref):
        i = pl.program_id(0)
        zc = c_ref[...].astype(F32) * v_ref[...].astype(F32)
        prev = jnp.where(i > 0, cp_ref[...].astype(F32) * vp_ref[...].astype(F32), 0.0)
        conv = w_ref[2:3, :] * zc + w_ref[1:2, :] * _shift_down(zc, prev, 1) + w_ref[0:1, :] * _shift_down(zc, prev, 2)
        o_ref[...] = (b_ref[...].astype(F32) * conv).astype(BF16)

    blk = lambda col: pl.BlockSpec((tm, d), lambda i: (i, col))
    edge = lambda col: pl.BlockSpec((EDGE, d), lambda i: (jnp.maximum(i * per - 1, 0), col))
    return pl.pallas_call(
        body, name=name, grid=(s // tm,),
        in_specs=[blk(0), blk(1), blk(2), edge(1), edge(2), pl.BlockSpec((3, d), lambda i: (0, 0))],
        out_specs=pl.BlockSpec((tm, d), lambda i: (i, 0)), out_shape=jax.ShapeDtypeStruct((s, d), BF16),
        compiler_params=_params(dimension_semantics=("arbitrary",)),
    )(bcv, bcv, bcv, bcv, bcv, cw)


def conv_bwd(name, bcv, dq, cw):
    s, d3 = bcv.shape
    d = d3 // 3
    tm = _row_tile(s, d * 4, 2**20)
    per = tm // EDGE
    n_tiles = s // tm
    last_edge = s // EDGE - 1

    def body(b_ref, c_ref, v_ref, cp_ref, vp_ref, bn_ref, dq_ref, dqn_ref, w_ref, o_ref, dw_ref):
        i = pl.program_id(0)
        b = b_ref[...].astype(F32)
        c = c_ref[...].astype(F32)
        v = v_ref[...].astype(F32)
        dq = dq_ref[...].astype(F32)
        zc = c * v
        prev = jnp.where(i > 0, cp_ref[...].astype(F32) * vp_ref[...].astype(F32), 0.0)
        z1 = _shift_down(zc, prev, 1)
        z2 = _shift_down(zc, prev, 2)
        w0, w1, w2 = w_ref[0:1, :], w_ref[1:2, :], w_ref[2:3, :]
        conv = w2 * zc + w1 * z1 + w0 * z2
        dconv = dq * b
        nxt = jnp.where(i < n_tiles - 1, dqn_ref[...].astype(F32) * bn_ref[...].astype(F32), 0.0)
        dz = w2 * dconv + w1 * _shift_up(dconv, nxt, 1) + w0 * _shift_up(dconv, nxt, 2)
        o_ref[:, :d] = (dq * conv).astype(BF16)
        o_ref[:, d:2 * d] = (dz * v).astype(BF16)
        o_ref[:, 2 * d:] = (dz * c).astype(BF16)
        dw = jnp.concatenate([jnp.sum(dconv * z2, axis=0, keepdims=True), jnp.sum(dconv * z1, axis=0, keepdims=True),
                              jnp.sum(dconv * zc, axis=0, keepdims=True), jnp.zeros((5, d), F32)], axis=0)

        @pl.when(i == 0)
        def _():
            dw_ref[...] = dw

        @pl.when(i > 0)
        def _():
            dw_ref[...] += dw

    blk = lambda col: pl.BlockSpec((tm, d), lambda i: (i, col))
    before = lambda col: pl.BlockSpec((EDGE, d), lambda i: (jnp.maximum(i * per - 1, 0), col))
    after = lambda col: pl.BlockSpec((EDGE, d), lambda i: (jnp.minimum((i + 1) * per, last_edge), col))
    return pl.pallas_call(
        body, name=name, grid=(n_tiles,),
        in_specs=[blk(0), blk(1), blk(2), before(1), before(2), after(0), blk(0), after(0),
                  pl.BlockSpec((3, d), lambda i: (0, 0))],
        out_specs=[pl.BlockSpec((tm, d3), lambda i: (i, 0)), pl.BlockSpec((8, d), lambda i: (0, 0))],
        out_shape=[jax.ShapeDtypeStruct((s, d3), BF16), jax.ShapeDtypeStruct((8, d), F32)],
        compiler_params=_params(dimension_semantics=("arbitrary",)),
    )(bcv, bcv, bcv, bcv, bcv, bcv, dq, dq, cw)


def _attn_probs(qh, kh, scale):
    sc = lax.dot_general(qh, kh, (((1,), (1,)), ((), ())), preferred_element_type=F32) * scale
    ex = jnp.exp(sc - jnp.max(sc, axis=-1, keepdims=True))
    return ex / jnp.sum(ex, axis=-1, keepdims=True)


def attn_fwd(name, q, kv):
    s, d = q.shape
    mlen = kv.shape[0]
    dh = d // XATTN_HEADS
    scale = dh ** -0.5
    tm = _row_tile(s, d * 4, 4 * 2**20)

    def body(q_ref, kv_ref, o_ref):
        for h in range(XATTN_HEADS):
            cols = slice(h * dh, (h + 1) * dh)
            p = _attn_probs(q_ref[:, cols], kv_ref[:, cols], scale)
            o_ref[:, cols] = jnp.dot(p.astype(BF16), kv_ref[:, d + h * dh:d + (h + 1) * dh],
                                     preferred_element_type=F32).astype(BF16)

    return pl.pallas_call(
        body, name=name, grid=(s // tm,),
        in_specs=[pl.BlockSpec((tm, d), lambda i: (i, 0)), pl.BlockSpec((mlen, 2 * d), lambda i: (0, 0))],
        out_specs=pl.BlockSpec((tm, d), lambda i: (i, 0)), out_shape=jax.ShapeDtypeStruct((s, d), BF16),
        compiler_params=_params(dimension_semantics=("arbitrary",)),
    )(q, kv)


def attn_bwd(name, q, kv, do):
    s, d = q.shape
    mlen = kv.shape[0]
    dh = d // XATTN_HEADS
    scale = dh ** -0.5
    tm = _row_tile(s, d * 4, 4 * 2**20)

    def body(q_ref, kv_ref, do_ref, dq_ref, dkv_ref):
        first = pl.program_id(0) == 0
        for h in range(XATTN_HEADS):
            cols = slice(h * dh, (h + 1) * dh)
            vcols = slice(d + h * dh, d + (h + 1) * dh)
            qh, kh, vh, doh = q_ref[:, cols], kv_ref[:, cols], kv_ref[:, vcols], do_ref[:, cols]
            p = _attn_probs(qh, kh, scale)
            dp = lax.dot_general(doh, vh, (((1,), (1,)), ((), ())), preferred_element_type=F32)
            ds = (p * (dp - jnp.sum(dp * p, axis=-1, keepdims=True)) * scale).astype(BF16)
            dq_ref[:, cols] = jnp.dot(ds, kh, preferred_element_type=F32).astype(BF16)
            dk = lax.dot_general(ds, qh, (((0,), (0,)), ((), ())), preferred_element_type=F32)
            dv = lax.dot_general(p.astype(BF16), doh, (((0,), (0,)), ((), ())), preferred_element_type=F32)

            @pl.when(first)
            def _():
                dkv_ref[:, cols] = dk
                dkv_ref[:, vcols] = dv

            @pl.when(jnp.logical_not(first))
            def _():
                dkv_ref[:, cols] += dk
                dkv_ref[:, vcols] += dv

    row = pl.BlockSpec((tm, d), lambda i: (i, 0))
    whole = pl.BlockSpec((mlen, 2 * d), lambda i: (0, 0))
    return pl.pallas_call(
        body, name=name, grid=(s // tm,), in_specs=[row, whole, row], out_specs=[row, whole],
        out_shape=[jax.ShapeDtypeStruct((s, d), BF16), jax.ShapeDtypeStruct((mlen, 2 * d), F32)],
        compiler_params=_params(dimension_semantics=("arbitrary",)),
    )(q, kv, do)


def _as_rows(a):
    if a.ndim >= 2 and a.shape[-1] % LANE == 0:
        return a.reshape(-1, a.shape[-1])
    return a.reshape(-1, LANE) if a.size % LANE == 0 else a.reshape(1, -1)


def add_halves(name, dw, other, core):
    p, r, c = dw.shape
    h = r // 2
    th = _row_tile(h, c * 2, 4 * 2**20)

    def body(core_ref, a_ref, b_ref, o_ref):
        o_ref[...] = (a_ref[...].astype(F32) + b_ref[...].astype(F32)).astype(BF16)

    grid_spec = pltpu.PrefetchScalarGridSpec(
        num_scalar_prefetch=1, grid=(p, h // th),
        in_specs=[pl.BlockSpec((None, None, th, c), lambda pi, i, core_ref: (pi, core_ref[0], i, 0)),
                  pl.BlockSpec((None, th, c), lambda pi, i, core_ref: (pi, i, 0))],
        out_specs=pl.BlockSpec((None, th, c), lambda pi, i, core_ref: (pi, i, 0)))
    return pl.pallas_call(
        body, name=name, grid_spec=grid_spec, out_shape=jax.ShapeDtypeStruct((p, h, c), BF16),
        compiler_params=_params(dimension_semantics=("arbitrary", "arbitrary")),
    )(core, dw.reshape(p, 2, h, c), other)


def sum_leading(name, parts):
    n, r, c = parts.shape
    tr = _row_tile(r, c * 4 * 2, 2 * 2**20)

    def body(p_ref, o_ref):
        acc = p_ref[0].astype(F32)
        for k in range(1, n):
            acc = acc + p_ref[k].astype(F32)
        o_ref[...] = acc

    return pl.pallas_call(
        body, name=name, grid=(r // tr,), in_specs=[pl.BlockSpec((n, tr, c), lambda i: (0, i, 0))],
        out_specs=pl.BlockSpec((tr, c), lambda i: (i, 0)), out_shape=jax.ShapeDtypeStruct((r, c), F32),
        compiler_params=_params(dimension_semantics=("arbitrary",)),
    )(parts)


def _adamw_rows(w, g, m, v):
    m = ADAM_B1 * m + (1.0 - ADAM_B1) * g
    v = ADAM_B2 * v + (1.0 - ADAM_B2) * (g * g)
    m_hat = m / (1.0 - ADAM_B1 ** ADAM_STEP)
    v_hat = v / (1.0 - ADAM_B2 ** ADAM_STEP)
    delta = -ADAM_LR * (m_hat / (jnp.sqrt(v_hat) + ADAM_EPS) + ADAM_WD * w)
    return delta, m, v


def adamw_layer(name, w, m, v, g, layer, carried):
    nl, r, c = w.shape
    tr = _row_tile(r, c * 4, 3 * 2**19)
    n_carried = 4 if carried is not None else 0

    def body(*refs):
        w_ref, m_ref, v_ref, g_ref = refs[:4]
        go_ref, d_ref, mo_ref, vo_ref = refs[4 + n_carried:]
        g = g_ref[...]
        delta, m_new, v_new = _adamw_rows(w_ref[...], g, m_ref[...], v_ref[...])
        go_ref[...] = g
        d_ref[...] = delta
        mo_ref[...] = m_new
        vo_ref[...] = v_new

    stacked = pl.BlockSpec((None, tr, c), lambda i: (layer, i, 0))
    in_specs = [stacked, stacked, stacked, pl.BlockSpec((tr, c), lambda i: (i, 0))]
    in_specs += [pl.BlockSpec(memory_space=pl.ANY)] * n_carried
    shape = jax.ShapeDtypeStruct((nl, r, c), F32)
    return pl.pallas_call(
        body, name=name, grid=(r // tr,), in_specs=in_specs, out_specs=[stacked] * 4, out_shape=[shape] * 4,
        input_output_aliases={4 + k: k for k in range(n_carried)},
        compiler_params=_params(dimension_semantics=("arbitrary",)),
    )(w, m, v, g, *(carried or ()))


def adamw_flat(name, w, m, v, g):
    r, c = w.shape

    def body(w_ref, m_ref, v_ref, g_ref, d_ref, mo_ref, vo_ref):
        delta, m_new, v_new = _adamw_rows(w_ref[...], g_ref[...], m_ref[...], v_ref[...])
        d_ref[...] = delta
        mo_ref[...] = m_new
        vo_ref[...] = v_new

    shape = jax.ShapeDtypeStruct((r, c), F32)
    return pl.pallas_call(body, name=name, out_shape=[shape] * 3, compiler_params=_params())(w, m, v, g)


def cast_place(name, w, layer, place):
    nl, r, c = w.shape
    tr = _row_tile(r, c * 4, 8 * 2**20)

    def body(x_ref, y_ref, c_ref, w_ref, o_ref):
        o_ref[...] = w_ref[...].astype(BF16)

    grid_spec = pltpu.PrefetchScalarGridSpec(
        num_scalar_prefetch=3, grid=(r // tr,),
        in_specs=[pl.BlockSpec((None, tr, c), lambda i, x_ref, y_ref, c_ref: (layer, i, 0))],
        out_specs=pl.BlockSpec((None, tr, c), lambda i, x_ref, y_ref, c_ref: (2 * x_ref[0] + y_ref[0], i, 0)))
    return pl.pallas_call(
        body, name=name, grid_spec=grid_spec, out_shape=jax.ShapeDtypeStruct((N_CHIPS, r, c), BF16),
        compiler_params=_params(dimension_semantics=("arbitrary",)),
    )(*place, w)


def reduce_sum4(name, own, landed, place):
    p, h, c = own.shape
    tr = _row_tile(h, c * 4, 4 * 2**20)

    def body(x_ref, y_ref, c_ref, t_ref, y1_ref, y2_ref, y3_ref, o_ref):
        acc = t_ref[...].astype(F32)
        for part_ref in (y1_ref, y2_ref, y3_ref):
            acc = acc + part_ref[...].astype(F32)
        o_ref[...] = acc

    def panel(fx, fy):
        return pl.BlockSpec((None, tr, c), lambda i, x_ref, y_ref, c_ref: (
            2 * (1 - x_ref[0] if fx else x_ref[0]) + (1 - y_ref[0] if fy else y_ref[0]), i, 0))

    grid_spec = pltpu.PrefetchScalarGridSpec(
        num_scalar_prefetch=3, grid=(h // tr,),
        in_specs=[panel(0, 0), panel(1, 0), panel(0, 1), panel(1, 1)],
        out_specs=pl.BlockSpec((None, tr, c), lambda i, x_ref, y_ref, c_ref: (c_ref[0], i, 0)))
    return pl.pallas_call(
        body, name=name, grid_spec=grid_spec, out_shape=jax.ShapeDtypeStruct((2, h, c), F32),
        compiler_params=_params(dimension_semantics=("arbitrary",)),
    )(*place, own, landed, landed, landed)


def run_exchange(name, exchange):
    n_in, n_out = len(exchange.inputs), len(exchange.out_shapes)

    def body(*refs):
        ins, outs, sems = refs[:n_in], refs[n_in:n_in + n_out], refs[n_in + n_out:]
        exchange.start(ins, outs, sems)
        exchange.finish(ins, outs, sems)

    return pl.pallas_call(
        body, name=name, in_specs=[ANY] * n_in, out_specs=[ANY] * n_out, out_shape=list(exchange.out_shapes),
        scratch_shapes=[pltpu.SemaphoreType.DMA((exchange.n_sems,)), pltpu.SemaphoreType.DMA((exchange.n_sems,))],
        input_output_aliases=dict(exchange.aliases),
    )(*exchange.inputs)


def _row_halves(ref, c):
    h = ref.shape[1] // 2
    return pl.ds(pl.multiple_of(c * h, 16), h), pl.ds(pl.multiple_of((1 - c) * h, 16), h)


def _in_place(arrays, n_sems, start, finish):
    return Exchange(list(arrays), [jax.ShapeDtypeStruct(f.shape, f.dtype) for f in arrays],
                    {a: a for a in range(len(arrays))}, n_sems, start, finish)


def gather_over_ici(fulls):
    n = len(fulls)

    def sends(outs, sems):
        x, y, c, mine, chips = _place()
        return [_remote(rows, rows, sems, 3 * a + j, (*chip, c)) for a in range(n)
                for rows in [outs[a].at[mine, _row_halves(outs[a], c)[0]]] for j, chip in enumerate(chips)]

    def start(ins, outs, sems):
        for cp in sends(outs, sems):
            cp.start()

    def finish(ins, outs, sems):
        x, y, c, mine, chips = _place()
        for a in range(n):
            for j, chip in enumerate(chips):
                rows = outs[a].at[2 * chip[0] + chip[1], _row_halves(outs[a], c)[0]]
                _remote(rows, rows, sems, 3 * a + j, (*chip, c)).wait_recv()
        for cp in sends(outs, sems):
            cp.wait_send()

    return _in_place(fulls, 3 * n, start, finish)


def gather_over_ici_by_neighbours(fulls):
    n = len(fulls)

    def plan(outs, a):
        x, y, c, mine, (nbr_x, nbr_y, far) = _place()
        h = outs[a].shape[1] // 2
        first = pl.ds(pl.multiple_of(c * h, 16), h // 2)
        second = pl.ds(pl.multiple_of(c * h + h // 2, 16), h // 2)
        index = lambda chip: 2 * chip[0] + chip[1]
        return c, mine, nbr_x, nbr_y, index, _row_halves(outs[a], c)[0], first, second, index(far)

    def direct(outs, sems, a):
        c, mine, nbr_x, nbr_y, _, half, _, _, _ = plan(outs, a)
        rows = outs[a].at[mine, half]
        return [_remote(rows, rows, sems, 4 * a, (*nbr_x, c)), _remote(rows, rows, sems, 4 * a + 1, (*nbr_y, c))]

    def passed_on(outs, sems, a):
        c, _, nbr_x, nbr_y, index, _, first, second, _ = plan(outs, a)
        from_x, from_y = outs[a].at[index(nbr_x), first], outs[a].at[index(nbr_y), second]
        return [_remote(from_x, from_x, sems, 4 * a + 2, (*nbr_y, c)), _remote(from_y, from_y, sems, 4 * a + 3, (*nbr_x, c))]

    def start(ins, outs, sems):
        for a in range(n):
            for cp in direct(outs, sems, a):
                cp.start()

    def finish(ins, outs, sems):
        for a in range(n):
            c, _, nbr_x, nbr_y, index, half, first, second, far = plan(outs, a)
            for k, nbr in enumerate((nbr_x, nbr_y)):
                rows = outs[a].at[index(nbr), half]
                _remote(rows, rows, sems, 4 * a + k, (*nbr, c)).wait_recv()
            for cp in passed_on(outs, sems, a):
                cp.start()
        for a in range(n):
            c, _, nbr_x, nbr_y, index, half, first, second, far = plan(outs, a)
            for k, (rows, nbr) in enumerate(((outs[a].at[far, first], nbr_y), (outs[a].at[far, second], nbr_x))):
                _remote(rows, rows, sems, 4 * a + 2 + k, (*nbr, c)).wait_recv()
            for cp in direct(outs, sems, a) + passed_on(outs, sems, a):
                cp.wait_send()

    return _in_place(fulls, 4 * n, start, finish)


def gather_over_d2d(fulls):
    n = len(fulls)

    def copies(outs, sems, which):
        x, y, c, mine, chips = _place()
        return [_remote(rows, rows, sems, 3 * a + j, (x, y, 1 - c)) for a in range(n) for j, chip in enumerate(chips)
                for rows in [outs[a].at[2 * chip[0] + chip[1], _row_halves(outs[a], c)[which]]]]

    def start(ins, outs, sems):
        for cp in copies(outs, sems, 0):
            cp.start()

    def finish(ins, outs, sems):
        for cp in copies(outs, sems, 1):
            cp.wait_recv()
        for cp in copies(outs, sems, 0):
            cp.wait_send()

    return _in_place(fulls, 3 * n, start, finish)


def gather_whole(fulls):
    ici, d2d = gather_over_ici(fulls), gather_over_d2d(fulls)

    def finish(ins, outs, sems):
        ici.finish(ins, outs, sems)
        later = tuple(_SemaphoresFrom(s, ici.n_sems) for s in sems)
        d2d.start(ins, outs, later)
        d2d.finish(ins, outs, later)

    return _in_place(fulls, ici.n_sems + d2d.n_sems, ici.start, finish)


class _SemaphoresFrom:
    def __init__(self, ref, offset):
        self.ref, self.offset = ref, offset

    @property
    def at(self):
        return self

    def __getitem__(self, k):
        return self.ref.at[self.offset + k]


def combine(exchanges):
    exchanges = [e for e in exchanges if e is not None]
    if len(exchanges) <= 1:
        return exchanges[0] if exchanges else None
    inputs, out_shapes, aliases, spans, n_sems = [], [], {}, [], 0
    for e in exchanges:
        aliases.update({len(inputs) + i: len(out_shapes) + o for i, o in e.aliases.items()})
        spans.append((len(inputs), len(e.inputs), len(out_shapes), len(e.out_shapes), n_sems))
        inputs, out_shapes, n_sems = inputs + list(e.inputs), out_shapes + list(e.out_shapes), n_sems + e.n_sems

    def each(method):
        def run(ins, outs, sems):
            for e, (i0, ni, o0, no, s0) in zip(exchanges, spans, strict=True):
                getattr(e, method)(ins[i0:i0 + ni], outs[o0:o0 + no], tuple(_SemaphoresFrom(s, s0) for s in sems))
        return run

    return Exchange(inputs, out_shapes, aliases, n_sems, each("start"), each("finish"))


def swap_exchange(grads):
    n = len(grads)

    def copies(ins, outs, sems):
        x, y, c, _, _ = _place()
        return [_remote(ins[a].at[:, _row_halves(ins[a], c)[1]], outs[a], sems, a, (x, y, 1 - c)) for a in range(n)]

    def start(ins, outs, sems):
        for cp in copies(ins, outs, sems):
            cp.start()

    def finish(ins, outs, sems):
        for cp in copies(ins, outs, sems):
            cp.wait()

    shapes = [jax.ShapeDtypeStruct((g.shape[0], g.shape[1] // 2, g.shape[2]), g.dtype) for g in grads]
    return Exchange(list(grads), shapes, {}, n, start, finish)


def scatter_exchange(parts):
    n = len(parts)

    def sends(ins, outs, sems):
        x, y, c, mine, chips = _place()
        return [_remote(ins[a].at[2 * chip[0] + chip[1]], outs[a].at[mine], sems, 3 * a + j, (*chip, c))
                for a in range(n) for j, chip in enumerate(chips)]

    def start(ins, outs, sems):
        for cp in sends(ins, outs, sems):
            cp.start()

    def finish(ins, outs, sems):
        x, y, c, mine, chips = _place()
        for a in range(n):
            for j, chip in enumerate(chips):
                landing = outs[a].at[2 * chip[0] + chip[1]]
                _remote(landing, landing, sems, 3 * a + j, (*chip, c)).wait_recv()
        for cp in sends(ins, outs, sems):
            cp.wait_send()

    return Exchange(list(parts), [jax.ShapeDtypeStruct(g.shape, g.dtype) for g in parts], {}, 3 * n, start, finish)


def join_exchange(halves):
    n = len(halves)

    def start(ins, outs, sems):
        x, y, c, _, _ = _place()
        for a in range(n):
            _remote(outs[a].at[c], outs[a].at[c], sems, a, (x, y, 1 - c)).start()

    def finish(ins, outs, sems):
        x, y, c, _, _ = _place()
        for a in range(n):
            _remote(outs[a].at[1 - c], outs[a].at[1 - c], sems, a, (x, y, 1 - c)).wait_recv()
        for a in range(n):
            _remote(outs[a].at[c], outs[a].at[c], sems, a, (x, y, 1 - c)).wait_send()

    return Exchange(list(halves), [jax.ShapeDtypeStruct(g.shape, g.dtype) for g in halves], {a: a for a in range(n)},
                    n, start, finish)


def gather_all(name, rows):
    def body(in_ref, out_ref, send_sems, recv_sems, local_sem):
        sems = (send_sems, recv_sems)
        x, y, c, _, _ = _place()
        me = 4 * x + 2 * y + c
        local = pltpu.make_async_copy(in_ref, out_ref.at[me], local_sem)
        local.start()
        peers = [(1 - x if k & 4 else x, 1 - y if k & 2 else y, 1 - c if k & 1 else c) for k in range(1, N_DEV)]
        sent = []
        for k, peer in enumerate(peers):
            cp = _remote(in_ref, out_ref.at[me], sems, k, peer)
            cp.start()
            sent.append(cp)
        for k, peer in enumerate(peers):
            landing = out_ref.at[4 * peer[0] + 2 * peer[1] + peer[2]]
            _remote(landing, landing, sems, k, peer).wait_recv()
        for cp in sent:
            cp.wait_send()
        local.wait()

    return pl.pallas_call(
        body, name=name, in_specs=[ANY], out_specs=ANY,
        out_shape=jax.ShapeDtypeStruct((N_DEV,) + rows.shape, rows.dtype),
        scratch_shapes=[pltpu.SemaphoreType.DMA((N_DEV - 1,)), pltpu.SemaphoreType.DMA((N_DEV - 1,)),
                        pltpu.SemaphoreType.DMA],
    )(rows)


class _Step:
    def __init__(self, p):
        self.p = p
        xi, yi, ci = lax.axis_index("x"), lax.axis_index("y"), lax.axis_index("c")
        self.chip = 2 * xi + yi
        self.place_refs = tuple(v.astype(jnp.int32).reshape(1) for v in (xi, yi, ci))
        self.core_ref = self.place_refs[2]
        self.depth = p['ffn1_norm'].shape[0]
        self.placed, self.landed, self.w = {}, {}, {}
        self.big_g = {}
        self.waiting_joins = []
        self.waiting_scatter = None

    def block_keys(self, tag, l):
        if l >= self.depth:
            return []
        mixer = ['gmlp_w_in', 'gmlp_w_out'] if l % 2 == 0 else ['conv_w_in', 'conv_w_out']
        names = {"ffn1": ['ffn1_w13', 'ffn1_w2'], "mix": mixer, "xattn": ['xattn_wq', 'xattn_wkv', 'xattn_wo'],
                 "ffn2": ['ffn2_w13', 'ffn2_w2']}[tag]
        return [(n, l // 2 if tag == "mix" else l) for n in names]

    def place(self, keys):
        for n, idx in keys:
            self.placed[(n, idx)] = cast_place(f"place_{n}{idx}", self.p[n], idx, self.place_refs)

    def carrying_gather(self, call, over_ici, over_d2d, whole, *args, **kw):
        over_ici = [k for k in over_ici if k in self.placed]
        over_d2d = [k for k in over_d2d if k in self.landed]
        parts = [gather_over_ici([self.placed.pop(k) for k in over_ici]) if over_ici else None,
                 gather_over_d2d([self.landed.pop(k) for k in over_d2d]) if over_d2d else None,
                 gather_whole([self.placed.pop(k) for k in whole]) if whole else None]
        exchange = combine(parts)
        if exchange is None:
            return call(*args, **kw)
        out, got = call(*args, exchange=exchange, **kw)
        self.landed.update(zip(over_ici, got[:len(over_ici)], strict=True))
        self.w.update(zip(over_d2d + whole, got[len(over_ici):], strict=True))
        return out

    def reduce_begin(self, tag, keys, dws, theirs=None):
        theirs = list(theirs or [None] * len(dws))
        todo = [i for i, t in enumerate(theirs) if t is None]
        for i, t in zip(todo, run_exchange(tag + "_swap", swap_exchange([dws[i] for i in todo])), strict=True):
            theirs[i] = t
        parts = [add_halves(f"{tag}_add{i}", dw, t, self.core_ref) for i, (dw, t) in enumerate(zip(dws, theirs, strict=True))]
        assert self.waiting_scatter is None
        self.waiting_scatter = (tag, keys, parts)

    def carrying_scatter(self, mm, *args, **kw):
        tag, keys, parts = self.waiting_scatter
        self.waiting_scatter = None
        out, landed = mm(*args, exchange=scatter_exchange(parts), **kw)
        halves = [reduce_sum4(f"{tag}_sum{i}", t, y, self.place_refs) for i, (t, y) in enumerate(zip(parts, landed, strict=True))]
        self.waiting_joins += list(zip(keys, halves, strict=True))
        return out

    def take_joined(self, keys, joined):
        for k, g in zip(keys, joined, strict=True):
            self.big_g[k] = g.reshape(-1, g.shape[-1])

    def carrying_joins(self, mm, *args, **kw):
        if not self.waiting_joins:
            return mm(*args, **kw)
        keys, halves = zip(*self.waiting_joins, strict=True)
        self.waiting_joins = []
        out, joined = mm(*args, exchange=join_exchange(list(halves)), **kw)
        self.take_joined(keys, joined)
        return out

    def joins_alone(self, name):
        keys, halves = zip(*self.waiting_joins, strict=True)
        self.waiting_joins = []
        self.take_joined(keys, run_exchange(name, join_exchange(list(halves))))

    def ffn_fwd(self, tag, l, x, gain, carry_norm, carry13, carry2, h=None, next_gain=None):
        name = f"l{l}_{tag}"
        if h is None:
            h = self.carrying_gather(rms_fwd, *carry_norm, name + "_norm", x, gain)
        by_gate, by_up, act = self.carrying_gather(mm_swiglu, *carry13, name + "_w13", h, self.w[(tag + '_w13', l)])
        out = self.carrying_gather(mm_nn, *carry2, name + "_w2", act, self.w[(tag + '_w2', l)], 'row', F32, res=x,
                                   scale=0.5, norm_gain=next_gain)
        return out, (x, h, by_gate, by_up, act)

    def ffn_bwd(self, tag, l, dx, dxb, saved, gain):
        w13, w2 = self.w[(tag + '_w13', l)], self.w[(tag + '_w2', l)]
        name = f"l{l}_{tag}"
        x, h, by_gate, by_up, act = saved
        d_gate, d_up = self.carrying_joins(mm_dswiglu, name + "_dact", dxb, w2, by_gate, by_up, 0.5)
        d_w2 = mm_tn(name + "_dw2", act, dxb, 'row', scale=0.5)
        half = N_CHIPS // 2
        d_w13, their_w2 = mm_tn(name + "_dw13g", h, d_gate, 'col', panels=(0, half), exchange=swap_exchange([d_w2]))
        d_w13 = mm_tn(name + "_dw13u", h, d_up, 'col', panels=(half, half), into=d_w13)
        self.reduce_begin(name, [(tag + '_w13', l), (tag + '_w2', l)], [d_w13, d_w2], [None] + their_w2)
        dh = self.carrying_scatter(mm_nt, name + "_dh", d_gate, w13, 'col', BF16, a_hi=d_up)
        return rms_bwd(name + "_dnorm", x, gain, dh, dx)


def kernel(x, mem, ffn1_norm, ffn1_w13, ffn1_w2, mix_norm, gmlp_w_in, gmlp_ln_g, gmlp_ln_b, gmlp_w_s, gmlp_b_s, gmlp_w_out, conv_w_in, conv_w, conv_w_out, xattn_norm, mem_norm, xattn_wq, xattn_wkv, xattn_wo, ffn2_norm, ffn2_w13, ffn2_w2, final_norm, loss_target, m_ffn1_norm, m_ffn1_w13, m_ffn1_w2, m_mix_norm, m_gmlp_w_in, m_gmlp_ln_g, m_gmlp_ln_b, m_gmlp_w_s, m_gmlp_b_s, m_gmlp_w_out, m_conv_w_in, m_conv_w, m_conv_w_out, m_xattn_norm, m_mem_norm, m_xattn_wq, m_xattn_wkv, m_xattn_wo, m_ffn2_norm, m_ffn2_w13, m_ffn2_w2, m_final_norm, v_ffn1_norm, v_ffn1_w13, v_ffn1_w2, v_mix_norm, v_gmlp_w_in, v_gmlp_ln_g, v_gmlp_ln_b, v_gmlp_w_s, v_gmlp_b_s, v_gmlp_w_out, v_conv_w_in, v_conv_w, v_conv_w_out, v_xattn_norm, v_mem_norm, v_xattn_wq, v_xattn_wkv, v_xattn_wo, v_ffn2_norm, v_ffn2_w13, v_ffn2_w2, v_final_norm):
    return _step(dict(locals()))


def _step(p):
    assert sorted(p) == sorted(ARG_NAMES)
    st = _Step(p)
    x = p['x'][0]
    mem = p['mem'][0]
    target = p['loss_target'][0]
    s, d = x.shape
    depth = st.depth

    for l in range(depth):
        for tag in ("ffn1", "mix", "xattn", "ffn2"):
            st.place(st.block_keys(tag, l))
    first = ('ffn1_w13', 0)
    st.landed[first] = run_exchange("gather_first", gather_over_ici_by_neighbours([st.placed.pop(first)]))[0]

    cw_local = p['conv_w']
    n_conv, cwid, dq4 = cw_local.shape
    cw_rows = jnp.pad(cw_local.reshape(-1, LANE), ((0, (-cw_local.size // LANE) % 8), (0, 0)))
    cw_all = gather_all("gather_conv_w", cw_rows)[0::2, :cw_local.size // LANE]
    conv_w_full = cw_all.reshape(N_CHIPS, n_conv, cwid, dq4).transpose(1, 2, 0, 3).reshape(n_conv, cwid, N_CHIPS * dq4)

    saved, h_ffn1 = [], None
    for l in range(depth):
        j = l // 2
        rec = {}
        mix_keys, att_keys, up_keys = st.block_keys("mix", l), st.block_keys("xattn", l), [('ffn2_w13', l)]
        if l == 0:
            on_w13, on_w2 = (mix_keys + att_keys, [], [('ffn1_w2', 0)]), (up_keys, mix_keys + att_keys, [])
            hand_over_on_mix_in = up_keys
        else:
            on_w13, on_w2 = (mix_keys + up_keys, [], []), (att_keys, mix_keys + up_keys, [])
            hand_over_on_mix_in = att_keys
        (x, h), rec['ffn1'] = st.ffn_fwd("ffn1", l, x, p['ffn1_norm'][l],
                                         ([], [('ffn1_w13', 0)] if l == 0 else [], []), on_w13, on_w2,
                                         h=h_ffn1, next_gain=p['mix_norm'][l])
        if l % 2 == 0:
            e = p['gmlp_ln_g'].shape[-1]
            bias = jnp.repeat(p['gmlp_b_s'][j].T, e // GMLP_GROUPS, axis=1)
            z = st.carrying_gather(mm_nn, [('ffn2_w2', l)], hand_over_on_mix_in, [], f"l{l}_gmlp_in", h,
                                   st.w[('gmlp_w_in', j)], 'col', BF16)
            gate = gmlp_fwd(f"l{l}_gmlp_gate", z, p['gmlp_ln_g'][j], p['gmlp_ln_b'][j], p['gmlp_w_s'][j], bias)
            x_new, hq = st.carrying_gather(mm_nn, [], [('ffn2_w2', l)], [], f"l{l}_gmlp_out", gate,
                                           st.w[('gmlp_w_out', j)], 'row', F32, res=x, norm_gain=p['xattn_norm'][l])
            rec['mix'] = (x, h, z, gate, bias)
        else:
            bcv = st.carrying_gather(mm_nn, [('ffn2_w2', l)], hand_over_on_mix_in, [], f"l{l}_conv_in", h,
                                     st.w[('conv_w_in', j)], 'col', BF16)
            gate = conv_fwd(f"l{l}_conv_gate", bcv, conv_w_full[j])
            x_new, hq = st.carrying_gather(mm_nn, [], [('ffn2_w2', l)], [], f"l{l}_conv_out", gate,
                                           st.w[('conv_w_out', j)], 'row', F32, res=x, norm_gain=p['xattn_norm'][l])
            rec['mix'] = (x, h, bcv, gate)
        x = x_new
        q = mm_nn(f"l{l}_xattn_q", hq, st.w[('xattn_wq', l)], 'row', BF16)
        mem_n = rms_fwd(f"l{l}_mem_norm", mem, p['mem_norm'][l])
        kv = mm_nn(f"l{l}_xattn_kv", mem_n, st.w[('xattn_wkv', l)], 'col', BF16)
        o = attn_fwd(f"l{l}_xattn_core", q, kv)
        x_new, h_ffn2 = mm_nn(f"l{l}_xattn_o", o, st.w[('xattn_wo', l)], 'row', F32, res=x,
                              norm_gain=p['ffn2_norm'][l])
        rec['xattn'] = (x, hq, q, mem_n, kv, o)
        x = x_new
        ahead = st.block_keys("ffn1", l + 1)
        last = l + 1 == depth
        out, rec['ffn2'] = st.ffn_fwd("ffn2", l, x, p['ffn2_norm'][l], None, (ahead, [], []), ([], ahead, []), h=h_ffn2,
                                      next_gain=None if last else p['ffn1_norm'][l + 1])
        x, h_ffn1 = (out, None) if last else out
        saved.append(rec)

    dx, dxb, d_final, loss_lanes = loss_head("loss_head", x, p['final_norm'], target)
    loss = lax.psum(0.5 * jnp.sum(loss_lanes) / d, ("x", "y", "c"))

    small = {n: [None] * p[n].shape[0] for n in ('ffn1_norm', 'mix_norm', 'xattn_norm', 'mem_norm', 'ffn2_norm',
                                                  'gmlp_ln_g', 'gmlp_ln_b', 'gmlp_w_s', 'gmlp_b_s', 'conv_w')}
    for l in reversed(range(depth)):
        j = l // 2
        rec = saved[l]
        dx, dxb, small['ffn2_norm'][l] = st.ffn_bwd("ffn2", l, dx, dxb, rec['ffn2'], p['ffn2_norm'][l])

        x_in, hq, q, mem_n, kv, o = rec['xattn']
        name = f"l{l}_xattn"
        do = st.carrying_joins(mm_nt, name + "_do", dxb, st.w[('xattn_wo', l)], 'row', BF16)
        d_wo = mm_tn(name + "_dwo", o, dxb, 'row')
        dq, dkv = attn_bwd(name + "_dcore", q, kv, do)
        d_wq = mm_tn(name + "_dwq", hq, dq, 'row')
        dkvb = dkv.astype(BF16)
        d_wkv = mm_tn(name + "_dwkv", mem_n, dkvb, 'col')
        st.reduce_begin(name, [('xattn_wq', l), ('xattn_wkv', l), ('xattn_wo', l)], [d_wq, d_wkv, d_wo])
        dx, dxb, small['xattn_norm'][l] = mm_nt(name + "_dh", dq, st.w[('xattn_wq', l)], 'row', F32,
                                                norm_back=(x_in, p['xattn_norm'][l], dx))
        dmem_n = mm_nt(name + "_dmem", dkvb, st.w[('xattn_wkv', l)], 'col', F32)
        small['mem_norm'][l] = rms_bwd(f"l{l}_mem_dnorm", mem, p['mem_norm'][l], dmem_n, None)[2]

        if l % 2 == 0:
            x_in, h, z, gate, bias = rec['mix']
            name = f"l{l}_gmlp"
            w_in, w_out = st.w[('gmlp_w_in', j)], st.w[('gmlp_w_out', j)]
            dgate = mm_nt(name + "_dgate", dxb, w_out, 'row', BF16)
            d_wout = mm_tn(name + "_dwout", gate, dxb, 'row')
            dmix, dws, dbs, dlg, dlb = gmlp_bwd(name + "_dgate_core", z, dgate, p['gmlp_ln_g'][j], p['gmlp_ln_b'][j],
                                                p['gmlp_w_s'][j], bias)
            small['gmlp_w_s'][j], small['gmlp_b_s'][j] = dws, dbs[:, :GMLP_GROUPS].T
            small['gmlp_ln_g'][j], small['gmlp_ln_b'][j] = dlg, dlb
            keys = [('gmlp_w_in', j), ('gmlp_w_out', j)]
        else:
            x_in, h, bcv, gate = rec['mix']
            name = f"l{l}_conv"
            w_in, w_out = st.w[('conv_w_in', j)], st.w[('conv_w_out', j)]
            dgate = mm_nt(name + "_dgate", dxb, w_out, 'row', BF16)
            d_wout = mm_tn(name + "_dwout", gate, dxb, 'row')
            dmix, dcw = conv_bwd(name + "_dgate_core", bcv, dgate, conv_w_full[j])
            small['conv_w'][j] = dcw[:cwid]
            keys = [('conv_w_in', j), ('conv_w_out', j)]
        d_win = st.carrying_scatter(mm_tn, name + "_dwin", h, dmix, 'col')
        st.reduce_begin(name, keys, [d_win, d_wout])
        dx, dxb, small['mix_norm'][l] = st.carrying_scatter(mm_nt, name + "_dh", dmix, w_in, 'col', F32,
                                                            norm_back=(x_in, p['mix_norm'][l], dx))

        dx, dxb, small['ffn1_norm'][l] = st.ffn_bwd("ffn1", l, dx, dxb, rec['ffn1'], p['ffn1_norm'][l])
    st.joins_alone("join_last")

    small_names = ['ffn1_norm', 'mix_norm', 'xattn_norm', 'mem_norm', 'ffn2_norm', 'gmlp_ln_g', 'gmlp_ln_b', 'gmlp_w_s',
                   'gmlp_b_s', 'final_norm', 'conv_w']
    small_full = {n: jnp.stack([g.reshape(p[n].shape[1:]) for g in small[n]]) for n in small_names
                  if n not in ('final_norm', 'conv_w')}
    small_full['final_norm'] = d_final.reshape(p['final_norm'].shape)
    small_full['conv_w'] = jnp.stack(small['conv_w'])
    packed = jnp.concatenate([small_full[n].reshape(-1, LANE) for n in small_names], axis=0)
    total = sum_leading("small_sum", gather_all("small_gather", packed))
    small_g, at = {}, 0
    for n in small_names:
        rows = small_full[n].size // LANE
        small_g[n] = total[at:at + rows].reshape(small_full[n].shape)
        at += rows
    small_g['conv_w'] = lax.dynamic_slice_in_dim(small_g['conv_w'], st.chip * dq4, dq4, axis=2)

    grads, deltas, new_m, new_v = {}, {}, {}, {}
    for n in WEIGHTS:
        w, m, v = p[n], p['m_' + n], p['v_' + n]
        if n in BIG:
            carried = None
            for i in range(w.shape[0]):
                carried = adamw_layer(f"adamw_{n}{i}", w, m, v, st.big_g[(n, i)], i, carried)
            grads[n], deltas[n], new_m[n], new_v[n] = carried
        else:
            g = small_g[n]
            out = adamw_flat(f"adamw_{n}", _as_rows(w), _as_rows(m), _as_rows(v), _as_rows(g))
            grads[n] = g
            deltas[n], new_m[n], new_v[n] = (o.reshape(w.shape) for o in out)

    grad_x = dx.reshape(p['x'].shape)
    return (loss, grad_x, *[grads[n] for n in WEIGHTS], *[deltas[n] for n in WEIGHTS], *[new_m[n] for n in WEIGHTS],
            *[new_v[n] for n in WEIGHTS])
```
